```python
import math
import jax, jax.numpy as jnp
from jax import lax
import numpy as np

D_MODEL = 1024
BATCH = 8
SEQ = 16384
DEPTH = 1

MIX_WIDTH = D_MODEL
CONV_WIDTH = MIX_WIDTH // 2
ATTN_HEADS = 8
HEAD_DIM = (MIX_WIDTH - CONV_WIDTH) // ATTN_HEADS
ATTN_WIDTH = ATTN_HEADS * HEAD_DIM
CONV_K = 3
DILATED_CONFIGS = ((128, 1), (512, 4), (2048, 16))
Q_BLOCK = 128
NUM_BUCKETS = 32
MAX_DISTANCE = 1024
EPS = 1e-6
NEG = -1e30

kernel_name = "hybrid_shortconv_dilated_attn_block"


def rmsnorm(x, w):
    xf = x.astype(jnp.float32)
    y = xf * lax.rsqrt(jnp.mean(xf * xf, axis=-1, keepdims=True) + EPS)
    return (y * w.astype(jnp.float32)).astype(x.dtype)


def t5_bucket(rel):
    half_b = NUM_BUCKETS // 2
    max_exact = half_b // 2
    ret = jnp.where(rel > 0, half_b, 0)
    n = jnp.abs(rel)
    nf = jnp.maximum(n, 1).astype(jnp.float32)
    large = max_exact + (jnp.log(nf / max_exact) / math.log(MAX_DISTANCE / max_exact)
                         * (half_b - max_exact)).astype(jnp.int32)
    large = jnp.minimum(large, half_b - 1)
    return ret + jnp.where(n < max_exact, n, large)


def dilated_window_attention(q, k, v, rel_bias, window, dilation):
    b, s, nh, dh = q.shape
    half = window // (2 * dilation)
    length = s // dilation
    n_blk = -(-length // Q_BLOCK)
    padded = n_blk * Q_BLOCK
    kb_len = Q_BLOCK + 2 * half
    bd = b * dilation

    def to_residue(t):
        t = t.reshape(b, length, dilation, nh, dh)
        return t.transpose(0, 2, 3, 1, 4).reshape(bd, nh, length, dh)

    qr = jnp.pad(to_residue(q), ((0, 0), (0, 0), (0, padded - length), (0, 0)))
    qr = qr.reshape(bd, nh, n_blk, Q_BLOCK, dh)
    pad_kv = ((0, 0), (0, 0), (half, padded - length + half), (0, 0))
    kr = jnp.pad(to_residue(k), pad_kv)
    vr = jnp.pad(to_residue(v), pad_kv)

    key_idx = jnp.arange(n_blk)[:, None] * Q_BLOCK + jnp.arange(kb_len)[None, :]
    kblk = kr[:, :, key_idx]
    vblk = vr[:, :, key_idx]

    logits = jnp.einsum('bhnqd,bhnkd->bhnqk', qr, kblk,
                        preferred_element_type=jnp.float32) * (dh ** -0.5)

    rel = jnp.arange(kb_len)[None, :] - half - jnp.arange(Q_BLOCK)[:, None]
    band = jnp.abs(rel) <= half
    buckets = t5_bucket(jnp.clip(rel, -half, half) * dilation)
    bias = rel_bias[buckets].astype(jnp.float32).transpose(2, 0, 1)
    key_pos = key_idx - half
    key_ok = (key_pos >= 0) & (key_pos < length)
    mask = band[None, :, :] & key_ok[:, None, :]
    logits = jnp.where(mask[None, None], logits + bias[None, :, None], NEG)

    m = jnp.max(logits, axis=-1, keepdims=True)
    p = jnp.exp(logits - m)
    denom = jnp.sum(p, axis=-1, keepdims=True)
    o = jnp.einsum('bhnqk,bhnkd->bhnqd', p, vblk.astype(jnp.float32)) / denom
    lse = (m + jnp.log(denom))[..., 0]

    o = o.reshape(bd, nh, padded, dh)[:, :, :length]
    o = o.reshape(b, dilation, nh, length, dh).transpose(0, 3, 1, 2, 4).reshape(b, s, nh, dh)
    lse = lse.reshape(bd, nh, padded)[:, :, :length]
    lse = lse.reshape(b, dilation, nh, length).transpose(0, 3, 1, 2).reshape(b, s, nh)
    return o, lse


def short_gated_conv(u, gate_b, gate_c, conv_w, conv_b):
    pad = (CONV_K - 1) // 2
    z = lax.conv_general_dilated(gate_c * u, conv_w[:, None, :].astype(u.dtype),
                                 window_strides=(1,), padding=((pad, pad),),
                                 dimension_numbers=('NWC', 'WIO', 'NWC'),
                                 feature_group_count=CONV_WIDTH)
    return gate_b * (z + conv_b.astype(u.dtype))


def _fwd_setup_inputs(seed: int = 0) -> dict:
    key = jax.random.key(seed)
    ks = jax.random.split(key, 9)
    proj_cols = 4 * CONV_WIDTH + 4 * ATTN_WIDTH
    x = jax.random.normal(ks[0], (BATCH, SEQ, D_MODEL), jnp.float32)
    norm_w = 1.0 + 0.05 * jax.random.normal(ks[1], (D_MODEL,), jnp.float32)
    w_in = jax.random.normal(ks[2], (D_MODEL, proj_cols), jnp.float32) * D_MODEL ** -0.5
    conv_w = jax.random.normal(ks[3], (CONV_K, CONV_WIDTH), jnp.float32) * CONV_K ** -0.5
    conv_b = 0.01 * jax.random.normal(ks[4], (CONV_WIDTH,), jnp.float32)
    q_norm_w = 1.0 + 0.05 * jax.random.normal(ks[5], (HEAD_DIM,), jnp.float32)
    k_norm_w = 1.0 + 0.05 * jax.random.normal(ks[6], (HEAD_DIM,), jnp.float32)
    rel_bias = 0.5 * jax.random.normal(ks[7], (NUM_BUCKETS, ATTN_HEADS), jnp.float32)
    w_out = jax.random.normal(ks[8], (MIX_WIDTH, D_MODEL), jnp.float32) * MIX_WIDTH ** -0.5
    return {"x": x, "norm_w": norm_w, "w_in": w_in, "conv_w": conv_w, "conv_b": conv_b,
            "q_norm_w": q_norm_w, "k_norm_w": k_norm_w, "rel_bias": rel_bias, "w_out": w_out}


def _fwd_reference(x, norm_w, w_in, conv_w, conv_b, q_norm_w, k_norm_w, rel_bias, w_out):
    b, s, _ = x.shape
    for _layer in range(DEPTH):
        h = rmsnorm(x, norm_w)
        proj = jnp.einsum('bsd,de->bse', h, w_in)
        c, a = CONV_WIDTH, ATTN_WIDTH
        splits = [c, 2 * c, 3 * c, 4 * c, 4 * c + a, 4 * c + 2 * a, 4 * c + 3 * a]
        u, g_b, g_c, z_conv, q, k, v, z_attn = jnp.split(proj, splits, axis=-1)

        y_conv = short_gated_conv(u, g_b, g_c, conv_w, conv_b) * jax.nn.silu(z_conv)

        q = rmsnorm(q.reshape(b, s, ATTN_HEADS, HEAD_DIM), q_norm_w)
        k = rmsnorm(k.reshape(b, s, ATTN_HEADS, HEAD_DIM), k_norm_w)
        v = v.reshape(b, s, ATTN_HEADS, HEAD_DIM)
        outs, lses = [], []
        for window, dilation in DILATED_CONFIGS:
            o_i, lse_i = dilated_window_attention(q, k, v, rel_bias, window, dilation)
            outs.append(o_i)
            lses.append(lse_i)
        mix = jax.nn.softmax(jnp.stack(lses, axis=0), axis=0)
        o = jnp.einsum('gbsh,gbshd->bshd', mix, jnp.stack(outs, axis=0))
        y_attn = o.reshape(b, s, ATTN_WIDTH).astype(x.dtype) * jax.nn.silu(z_attn)

        y = jnp.concatenate([y_conv, y_attn], axis=-1)
        x = x + jnp.einsum('bse,ed->bsd', y, w_out)
    return x


import jax as _jax
import jax.numpy as _jnp

TWIN_FORMAT = 'train_step'
FWD_PARAMS = ['x', 'norm_w', 'w_in', 'conv_w', 'conv_b', 'q_norm_w', 'k_norm_w', 'rel_bias', 'w_out']
TWIN_WEIGHTS = ['norm_w', 'w_in', 'conv_w', 'conv_b', 'q_norm_w', 'k_norm_w', 'rel_bias', 'w_out']
TWIN_DIFF_INPUT = 'x'
TWIN_INPUTS = ['x', 'norm_w', 'w_in', 'conv_w', 'conv_b', 'q_norm_w', 'k_norm_w', 'rel_bias', 'w_out', 'loss_target', 'm_norm_w', 'm_w_in', 'm_conv_w', 'm_conv_b', 'm_q_norm_w', 'm_k_norm_w', 'm_rel_bias', 'm_w_out', 'v_norm_w', 'v_w_in', 'v_conv_w', 'v_conv_b', 'v_q_norm_w', 'v_k_norm_w', 'v_rel_bias', 'v_w_out']
TWIN_OUTPUTS = ['loss', 'grad_x', 'grad_norm_w', 'grad_w_in', 'grad_conv_w', 'grad_conv_b', 'grad_q_norm_w', 'grad_k_norm_w', 'grad_rel_bias', 'grad_w_out', 'delta_norm_w', 'delta_w_in', 'delta_conv_w', 'delta_conv_b', 'delta_q_norm_w', 'delta_k_norm_w', 'delta_rel_bias', 'delta_w_out', 'new_m_norm_w', 'new_m_w_in', 'new_m_conv_w', 'new_m_conv_b', 'new_m_q_norm_w', 'new_m_k_norm_w', 'new_m_rel_bias', 'new_m_w_out', 'new_v_norm_w', 'new_v_w_in', 'new_v_conv_w', 'new_v_conv_b', 'new_v_q_norm_w', 'new_v_k_norm_w', 'new_v_rel_bias', 'new_v_w_out']
TWIN_LEAF_KINDS = {'loss': 'loss', 'grad_x': 'grad_x', 'grad_norm_w': 'grad_w', 'grad_w_in': 'grad_w', 'grad_conv_w': 'grad_w', 'grad_conv_b': 'grad_w', 'grad_q_norm_w': 'grad_w', 'grad_k_norm_w': 'grad_w', 'grad_rel_bias': 'grad_w', 'grad_w_out': 'grad_w', 'delta_norm_w': 'delta_w', 'delta_w_in': 'delta_w', 'delta_conv_w': 'delta_w', 'delta_conv_b': 'delta_w', 'delta_q_norm_w': 'delta_w', 'delta_k_norm_w': 'delta_w', 'delta_rel_bias': 'delta_w', 'delta_w_out': 'delta_w', 'new_m_norm_w': 'new_m', 'new_m_w_in': 'new_m', 'new_m_conv_w': 'new_m', 'new_m_conv_b': 'new_m', 'new_m_q_norm_w': 'new_m', 'new_m_k_norm_w': 'new_m', 'new_m_rel_bias': 'new_m', 'new_m_w_out': 'new_m', 'new_v_norm_w': 'new_v', 'new_v_w_in': 'new_v', 'new_v_conv_w': 'new_v', 'new_v_conv_b': 'new_v', 'new_v_q_norm_w': 'new_v', 'new_v_k_norm_w': 'new_v', 'new_v_rel_bias': 'new_v', 'new_v_w_out': 'new_v'}


def _forward(args):
    return _fwd_reference(*[args[k] for k in FWD_PARAMS])


def _output_shape():
    def fwd():
        inp = _fwd_setup_inputs(0)
        return _fwd_reference(*[inp[k] for k in FWD_PARAMS])
    out = _jax.eval_shape(fwd)
    return out.shape, out.dtype

N_MICROBATCH = 1
ADAM_LR = 0.001
ADAM_B1 = 0.9
ADAM_B2 = 0.999
ADAM_EPS = 1e-08
ADAM_WD = 0.01
ADAM_STEP = 10
PER_EXAMPLE_BATCH_AXIS = {'x': 0, 'loss_target': 0}
SHARED_INPUTS = []
_WEIGHT_DTYPES = {'norm_w': _jnp.float32, 'w_in': _jnp.float32, 'conv_w': _jnp.float32, 'conv_b': _jnp.float32, 'q_norm_w': _jnp.float32, 'k_norm_w': _jnp.float32, 'rel_bias': _jnp.float32, 'w_out': _jnp.float32}
MOMENT_SCALE = {'norm_w': 8.809127e+01, 'w_in': 9.574845e-01, 'conv_w': 2.367651e+01, 'conv_b': 1.166005e+00, 'q_norm_w': 2.186652e+00, 'k_norm_w': 2.201322e+00, 'rel_bias': 2.740162e-01, 'w_out': 7.359634e-01}


def _to_microbatches(a, axis):
    t = _jnp.moveaxis(a, axis, 0)
    t = t.reshape((N_MICROBATCH, t.shape[0] // N_MICROBATCH) + t.shape[1:])
    return _jnp.moveaxis(t, 1, axis + 1)


def setup_inputs(seed: int = 0) -> dict:
    inp = _fwd_setup_inputs(seed)
    key = _jax.random.fold_in(_jax.random.key(seed), 7919)
    shape, _ = _output_shape()
    out = dict(inp)
    out["loss_target"] = _jax.random.normal(_jax.random.fold_in(key, 0), shape, _jnp.float32)
    for i, name in enumerate(TWIN_WEIGHTS):
        w = inp[name].astype(_jnp.float32)
        if MOMENT_SCALE is None:
            s = _jnp.sqrt(_jnp.mean(_jnp.square(w)) + 1e-30)
        else:
            s = MOMENT_SCALE[name]
        km, kv = _jax.random.split(_jax.random.fold_in(key, i + 1))
        out[name] = w
        out["m_" + name] = s * _jax.random.normal(km, w.shape, _jnp.float32)
        out["v_" + name] = (s * s) * _jax.random.uniform(kv, w.shape, _jnp.float32, 0.5, 1.5)
    if N_MICROBATCH > 1:
        for name, axis in PER_EXAMPLE_BATCH_AXIS.items():
            out[name] = _to_microbatches(out[name], axis)
    return {'x': out['x'], 'norm_w': out['norm_w'], 'w_in': out['w_in'], 'conv_w': out['conv_w'], 'conv_b': out['conv_b'], 'q_norm_w': out['q_norm_w'], 'k_norm_w': out['k_norm_w'], 'rel_bias': out['rel_bias'], 'w_out': out['w_out'], 'loss_target': out['loss_target'], 'm_norm_w': out['m_norm_w'], 'm_w_in': out['m_w_in'], 'm_conv_w': out['m_conv_w'], 'm_conv_b': out['m_conv_b'], 'm_q_norm_w': out['m_q_norm_w'], 'm_k_norm_w': out['m_k_norm_w'], 'm_rel_bias': out['m_rel_bias'], 'm_w_out': out['m_w_out'], 'v_norm_w': out['v_norm_w'], 'v_w_in': out['v_w_in'], 'v_conv_w': out['v_conv_w'], 'v_conv_b': out['v_conv_b'], 'v_q_norm_w': out['v_q_norm_w'], 'v_k_norm_w': out['v_k_norm_w'], 'v_rel_bias': out['v_rel_bias'], 'v_w_out': out['v_w_out']}


def _loss(weights, diff, rest, loss_target):
    with _jax.named_scope("forward"):
        args = {**rest, TWIN_DIFF_INPUT: diff, **{k: w.astype(_WEIGHT_DTYPES[k]) for k, w in weights.items()}}
        y = _forward(args)
    with _jax.named_scope("loss_head"):
        err = _jnp.square(y.astype(_jnp.float32) - loss_target)
        return 0.5 * _jnp.sum(_jnp.mean(err, axis=-1)) if err.ndim else 0.5 * err


def _adamw(w, g, m, v):
    m = ADAM_B1 * m + (1.0 - ADAM_B1) * g
    v = ADAM_B2 * v + (1.0 - ADAM_B2) * _jnp.square(g)
    m_hat = m / (1.0 - ADAM_B1 ** ADAM_STEP)
    v_hat = v / (1.0 - ADAM_B2 ** ADAM_STEP)
    delta = -ADAM_LR * (m_hat / (_jnp.sqrt(v_hat) + ADAM_EPS) + ADAM_WD * w)
    return delta, m, v


def reference(x, norm_w, w_in, conv_w, conv_b, q_norm_w, k_norm_w, rel_bias, w_out, loss_target, m_norm_w, m_w_in, m_conv_w, m_conv_b, m_q_norm_w, m_k_norm_w, m_rel_bias, m_w_out, v_norm_w, v_w_in, v_conv_w, v_conv_b, v_q_norm_w, v_k_norm_w, v_rel_bias, v_w_out):
    given = dict(x=x, norm_w=norm_w, w_in=w_in, conv_w=conv_w, conv_b=conv_b, q_norm_w=q_norm_w, k_norm_w=k_norm_w, rel_bias=rel_bias, w_out=w_out, loss_target=loss_target, m_norm_w=m_norm_w, m_w_in=m_w_in, m_conv_w=m_conv_w, m_conv_b=m_conv_b, m_q_norm_w=m_q_norm_w, m_k_norm_w=m_k_norm_w, m_rel_bias=m_rel_bias, m_w_out=m_w_out, v_norm_w=v_norm_w, v_w_in=v_w_in, v_conv_w=v_conv_w, v_conv_b=v_conv_b, v_q_norm_w=v_q_norm_w, v_k_norm_w=v_k_norm_w, v_rel_bias=v_rel_bias, v_w_out=v_w_out)
    weights = {n: given[n] for n in TWIN_WEIGHTS}
    shared = {n: given[n] for n in SHARED_INPUTS}
    per_example = {n: given[n] for n in ['x']}
    grad_fn = _jax.value_and_grad(_loss, argnums=(0, 1))

    def one_microbatch(ex, loss_target):
        ex = dict(ex)
        diff = ex.pop(TWIN_DIFF_INPUT)
        return grad_fn(weights, diff, {**shared, **ex}, loss_target)

    if N_MICROBATCH == 1:
        loss, (grad_w, grad_x) = one_microbatch(per_example, given["loss_target"])
    else:
        def body(carry, xs):
            loss_sum, grad_sum = carry
            l_k, (gw_k, gx_k) = one_microbatch(xs[0], xs[1])
            with _jax.named_scope("update"):
                return (loss_sum + l_k, _jax.tree.map(_jnp.add, grad_sum, gw_k)), gx_k

        init = (_jnp.zeros((), _jnp.float32), _jax.tree.map(_jnp.zeros_like, weights))
        (loss, grad_w), grad_x = _jax.lax.scan(body, init, (per_example, given["loss_target"]))
    with _jax.named_scope("update"):
        delta_w, new_m, new_v = {}, {}, {}
        for n in TWIN_WEIGHTS:
            delta_w[n], new_m[n], new_v[n] = _adamw(weights[n], grad_w[n], given["m_" + n], given["v_" + n])
    return (loss, grad_x, *[grad_w[n] for n in TWIN_WEIGHTS], *[delta_w[n] for n in TWIN_WEIGHTS],
            *[new_m[n] for n in TWIN_WEIGHTS], *[new_v[n] for n in TWIN_WEIGHTS])
```

```python
import functools
import math

import numpy as np
import jax
import jax.numpy as jnp
from jax import lax
from jax.experimental import pallas as pl
from jax.experimental.pallas import tpu as pltpu

F32 = jnp.float32
BF16 = jnp.bfloat16

N_DEV = 8
D_MODEL = 1024
CONV_W = 512
ATTN_W = 512
N_HEADS = 8
HEAD_DIM = 64
N_PAIRS = N_HEADS // 2
LANES = 128
HALF = 64
QB = 128
KB = QB + 2 * HALF
DILATIONS = (1, 4, 16)
N_BUCKETS = 32
MAX_DISTANCE = 1024
EPS = 1e-6
NEG = -1e30
LOGIT_SCALE = HEAD_DIM ** -0.5

ADAM_LR = 0.001
ADAM_B1 = 0.9
ADAM_B2 = 0.999
ADAM_EPS = 1e-08
ADAM_WD = 0.01
ADAM_STEP = 10

MESH = pl.DeviceIdType.MESH
MIB = 1024 * 1024


def _params(semantics, vmem_mib):
    return pltpu.CompilerParams(dimension_semantics=semantics, vmem_limit_bytes=vmem_mib * MIB)


def _lane(shape):
    return lax.broadcasted_iota(jnp.int32, shape, len(shape) - 1)


def _sigmoid(z):
    return 1.0 / (1.0 + jnp.exp(-z))


def _head_sums(prod, lo_mask):
    s_lo = jnp.sum(jnp.where(lo_mask, prod, 0.0), axis=-1, keepdims=True)
    s_hi = jnp.sum(jnp.where(lo_mask, 0.0, prod), axis=-1, keepdims=True)
    return s_lo, s_hi


def _bucket_matrix(dilation):
    rel = np.arange(KB)[None, :] - HALF - np.arange(QB)[:, None]
    band = np.abs(rel) <= HALF
    dist = np.clip(rel, -HALF, HALF) * dilation
    half_b = N_BUCKETS // 2
    max_exact = half_b // 2
    ret = np.where(dist > 0, half_b, 0)
    n = np.abs(dist)
    nf = np.maximum(n, 1).astype(np.float32)
    large = max_exact + (np.log(nf / np.float32(max_exact)) / np.float32(math.log(MAX_DISTANCE / max_exact))
                         * np.float32(half_b - max_exact)).astype(np.int32)
    large = np.minimum(large, half_b - 1)
    bucket = ret + np.where(n < max_exact, n, large)
    return np.where(band, bucket, -1).astype(np.int32)


def _ag_weights(w_in, w_out, cw_pad):
    n_arr = 3

    def body(win_ref, wout_ref, cw_ref, gin_ref, gout_ref, gcw_ref, send_sems, recv_sems):
        x, y, c = lax.axis_index("x"), lax.axis_index("y"), lax.axis_index("c")
        me = (x, y, c)
        sibling = (x, y, 1 - c)
        chips = [(1 - x, y), (x, 1 - y), (1 - x, 1 - y)]
        arrays = (gin_ref, gout_ref, gcw_ref)

        def slot(px, py, pc):
            return 4 * px + 2 * py + pc

        gin_ref[slot(*me)] = win_ref[...].astype(BF16)
        gout_ref[slot(*me)] = wout_ref[...].astype(BF16)
        gcw_ref[slot(*me)] = cw_ref[...]

        def copy(a, k, block, to):
            ref = arrays[a].at[slot(*block)]
            return pltpu.make_async_remote_copy(
                src_ref=ref, dst_ref=ref, send_sem=send_sems.at[a * 7 + k], recv_sem=recv_sems.at[a * 7 + k],
                device_id=to, device_id_type=MESH)

        first = [copy(a, 0, me, sibling) for a in range(n_arr)]
        for j, chip in enumerate(chips):
            first += [copy(a, 1 + j, me, (*chip, c)) for a in range(n_arr)]
        for cp in first:
            cp.start()
        passed = []
        for j, chip in enumerate(chips):
            for a in range(n_arr):
                copy(a, 1 + j, (*chip, c), me).wait_recv()
            for a in range(n_arr):
                cp = copy(a, 4 + j, (*chip, c), sibling)
                cp.start()
                passed.append(cp)
        for a in range(n_arr):
            copy(a, 0, sibling, me).wait_recv()
        for j, chip in enumerate(chips):
            for a in range(n_arr):
                copy(a, 4 + j, (*chip, 1 - c), me).wait_recv()
        for cp in first + passed:
            cp.wait_send()

    vm = pl.BlockSpec(memory_space=pltpu.VMEM)
    return pl.pallas_call(
        body, name="ag_weights",
        out_shape=(jax.ShapeDtypeStruct((N_DEV,) + w_in.shape, BF16),
                   jax.ShapeDtypeStruct((N_DEV,) + w_out.shape, BF16),
                   jax.ShapeDtypeStruct((N_DEV,) + cw_pad.shape, F32)),
        in_specs=[vm, vm, vm], out_specs=(vm, vm, vm),
        scratch_shapes=[pltpu.SemaphoreType.DMA((n_arr * 7,)), pltpu.SemaphoreType.DMA((n_arr * 7,))],
        compiler_params=pltpu.CompilerParams(vmem_limit_bytes=40 * MIB),
    )(w_in, w_out, cw_pad)


def _fwd_proj(x, norm_w, wg):
    S = x.shape[0]
    tm = 512

    def body(x_ref, nw_ref, wg_ref, u_ref, gb_ref, gc_ref, zc_ref, q_ref, k_ref, v_ref, za_ref, rq_ref, rk_ref):
        xv = x_ref[...]
        r = lax.rsqrt(jnp.mean(xv * xv, axis=-1, keepdims=True) + EPS)
        h = (xv * r * nw_ref[...]).astype(BF16)
        lane = _lane((tm, LANES))
        lo = lane < HEAD_DIM
        plain = {0: u_ref, 1: gb_ref, 2: gc_ref, 3: zc_ref, 6: v_ref, 7: za_ref}
        normed = {4: (q_ref, rq_ref), 5: (k_ref, rk_ref)}
        for j in range(8):
            acc = jnp.dot(h, wg_ref[j], preferred_element_type=F32)
            if j in plain:
                plain[j][...] = acc.astype(BF16)
                continue
            out_ref, r_ref = normed[j]
            r_tile = jnp.zeros((tm, LANES), F32)
            for p in range(N_PAIRS):
                blk = acc[:, p * LANES:(p + 1) * LANES]
                s_lo, s_hi = _head_sums(blk * blk, lo)
                r_lo = lax.rsqrt(s_lo * (1.0 / HEAD_DIM) + EPS)
                r_hi = lax.rsqrt(s_hi * (1.0 / HEAD_DIM) + EPS)
                out_ref[:, p * LANES:(p + 1) * LANES] = (blk * jnp.where(lo, r_lo, r_hi)).astype(BF16)
                r_tile = jnp.where(lane == 2 * p, r_lo, r_tile)
                r_tile = jnp.where(lane == 2 * p + 1, r_hi, r_tile)
            r_ref[...] = r_tile

    piece = jax.ShapeDtypeStruct((S, CONV_W), BF16)
    stat = jax.ShapeDtypeStruct((S, LANES), F32)
    pspec = pl.BlockSpec((tm, CONV_W), lambda i: (i, 0))
    sspec = pl.BlockSpec((tm, LANES), lambda i: (i, 0))
    return pl.pallas_call(
        body, name="fwd_proj", grid=(S // tm,),
        out_shape=(piece,) * 8 + (stat, stat),
        in_specs=[pl.BlockSpec((tm, D_MODEL), lambda i: (i, 0)),
                  pl.BlockSpec((1, D_MODEL), lambda i: (0, 0)),
                  pl.BlockSpec(wg.shape, lambda i: (0, 0, 0), pipeline_mode=pl.Buffered(1))],
        out_specs=(pspec,) * 8 + (sspec, sspec),
        compiler_params=_params(("arbitrary",), 48),
    )(x, norm_w.reshape(1, D_MODEL), wg)


def _bias_tables(rel_bias, bmat):
    def body(rb_ref, b_ref, out_ref):
        h = pl.program_id(1)
        b = b_ref[0]
        t = jnp.full((QB, KB), NEG, F32)
        for bk in range(N_BUCKETS):
            t = jnp.where(b == bk, rb_ref[bk, h], t)
        out_ref[0, 0] = t

    return pl.pallas_call(
        body, name="bias_tables", grid=(len(DILATIONS), N_HEADS),
        out_shape=jax.ShapeDtypeStruct((len(DILATIONS), N_HEADS, QB, KB), F32),
        in_specs=[pl.BlockSpec(memory_space=pltpu.SMEM),
                  pl.BlockSpec((1, QB, KB), lambda g, h: (g, 0, 0))],
        out_specs=pl.BlockSpec((1, 1, QB, KB), lambda g, h: (g, h, 0, 0)),
        compiler_params=_params(("arbitrary", "arbitrary"), 16),
    )(rel_bias, bmat)


def _bias_grad(ds_acc, bmat):
    def body(ds_ref, b_ref, out_ref):
        lane = _lane((8, LANES))
        vec = jnp.zeros((8, LANES), F32)
        for g in range(len(DILATIONS)):
            b = b_ref[g]
            ds = ds_ref[g, 0]
            for bk in range(N_BUCKETS):
                s = jnp.sum(jnp.where(b == bk, ds, 0.0), axis=-1, keepdims=True)
                s = jnp.sum(s, axis=0, keepdims=True)
                vec = vec + jnp.where(lane == bk, s, 0.0)
        out_ref[0] = vec

    return pl.pallas_call(
        body, name="bias_grad", grid=(N_HEADS,),
        out_shape=jax.ShapeDtypeStruct((N_HEADS, 8, LANES), F32),
        in_specs=[pl.BlockSpec((len(DILATIONS), 1, QB, KB), lambda h: (0, h, 0, 0)),
                  pl.BlockSpec((len(DILATIONS), QB, KB), lambda h: (0, 0, 0))],
        out_specs=pl.BlockSpec((1, 8, LANES), lambda h: (h, 0, 0)),
        compiler_params=_params(("arbitrary",), 16),
    )(ds_acc, bmat)


def _halo_specs(width, rows, T, L, cols_of):
    per = T // rows
    last = L // rows - 1
    prev = pl.BlockSpec((rows, width), lambda r, i: (jnp.maximum(i * per - 1, 0), cols_of(r)))
    nxt = pl.BlockSpec((rows, width), lambda r, i: (jnp.minimum((i + 1) * per, last), cols_of(r)))
    return prev, nxt


def _attn_fwd(g, dil, q_hat, k_hat, v, gain, tables):
    S = q_hat.shape[0]
    L = S // dil
    T = min(1024, L)
    n_sub = T // QB
    view = lambda a: a.reshape(L, dil * a.shape[1])

    def body(q_ref, km_ref, kp_ref, kn_ref, vm_ref, vp_ref, vn_ref, gain_ref, bias_ref, o_ref, lse_ref, kwin, vwin):
        i = pl.program_id(1)
        kwin[0:HALF] = kp_ref[...]
        kwin[HALF:HALF + T] = km_ref[...]
        kwin[HALF + T:] = kn_ref[...]
        vwin[0:HALF] = vp_ref[...]
        vwin[HALF:HALF + T] = vm_ref[...]
        vwin[HALF + T:] = vn_ref[...]
        lane = _lane((QB, LANES))
        lo = lane < HEAD_DIM
        col = _lane((1, KB))

        def sub(j, carry):
            r0 = pl.multiple_of(j * QB, QB)
            kpos = i * T + j * QB - HALF + col
            kvalid = (kpos >= 0) & (kpos < L)
            lse_tile = jnp.zeros((QB, LANES), F32)
            for p in range(N_PAIRS):
                cs = slice(p * LANES, (p + 1) * LANES)
                qf = q_ref[pl.ds(r0, QB), cs].astype(F32) * gain_ref[:, cs]
                kw = kwin[pl.ds(r0, KB), cs]
                vw = vwin[pl.ds(r0, KB), cs]
                outs = []
                for e in range(2):
                    h = 2 * p + e
                    qm = jnp.where(lo if e == 0 else ~lo, qf, 0.0).astype(BF16)
                    s = lax.dot_general(qm, kw, (((1,), (1,)), ((), ())), preferred_element_type=F32)
                    logits = jnp.where(kvalid, s + bias_ref[0, h], NEG)
                    m = jnp.max(logits, axis=-1, keepdims=True)
                    pe = jnp.exp(logits - m)
                    l = jnp.sum(pe, axis=-1, keepdims=True)
                    pv = jnp.dot(pe.astype(BF16), vw, preferred_element_type=F32)
                    outs.append(pv * (1.0 / l))
                    lse_tile = jnp.where(lane == h, m + jnp.log(l), lse_tile)
                o_ref[pl.ds(r0, QB), cs] = jnp.where(lo, outs[0], outs[1]).astype(BF16)
            lse_ref[pl.ds(r0, QB), :] = lse_tile
            return carry

        lax.fori_loop(0, n_sub, sub, 0)

    main = pl.BlockSpec((T, ATTN_W), lambda r, i: (i, r))
    prev, nxt = _halo_specs(ATTN_W, HALF, T, L, lambda r: r)
    o_g, lse_g = pl.pallas_call(
        body, name=f"attn_fwd_d{dil}", grid=(dil, L // T),
        out_shape=(jax.ShapeDtypeStruct((L, dil * ATTN_W), BF16), jax.ShapeDtypeStruct((L, dil * LANES), F32)),
        in_specs=[main, main, prev, nxt, main, prev, nxt,
                  pl.BlockSpec((1, ATTN_W), lambda r, i: (0, 0)),
                  pl.BlockSpec((1, N_HEADS, QB, KB), lambda r, i: (g, 0, 0, 0))],
        out_specs=(main, pl.BlockSpec((T, LANES), lambda r, i: (i, r))),
        scratch_shapes=[pltpu.VMEM((T + 2 * HALF, ATTN_W), BF16), pltpu.VMEM((T + 2 * HALF, ATTN_W), BF16)],
        compiler_params=_params(("arbitrary", "arbitrary"), 40),
    )(view(q_hat), view(k_hat), view(k_hat), view(k_hat), view(v), view(v), view(v), gain, tables)
    return o_g.reshape(S, ATTN_W), lse_g.reshape(S, LANES)


def _row_halo_specs(width, tm, S, col=0):
    rows = 16
    per = tm // rows
    last = S // rows - 1
    prev = pl.BlockSpec((rows, width), lambda i: (jnp.maximum(i * per - 1, 0), col))
    nxt = pl.BlockSpec((rows, width), lambda i: (jnp.minimum((i + 1) * per, last), col))
    return prev, nxt


def _shift_rows(a, prev_row, next_row, tm):
    row = lax.broadcasted_iota(jnp.int32, a.shape, 0)
    a_m1 = jnp.where(row == 0, prev_row, pltpu.roll(a, 1, 0))
    a_p1 = jnp.where(row == tm - 1, next_row, pltpu.roll(a, tm - 1, 0))
    return a_m1, a_p1


def _combine_gates(o_parts, lse_parts, u, gb, gc, zc, za, conv_w, conv_b):
    S = u.shape[0]
    tm = 512
    n_t = S // tm

    def body(o1, o2, o3, l1, l2, l3, u_ref, up_ref, un_ref, gb_ref, gc_ref, gcp_ref, gcn_ref, zc_ref, za_ref,
             cw_ref, cb_ref, y_ref, o_ref, lse_ref):
        i = pl.program_id(0)
        lane = _lane((tm, LANES))
        lo = lane < HEAD_DIM
        ls = [l1[...], l2[...], l3[...]]
        lmax = jnp.maximum(jnp.maximum(ls[0], ls[1]), ls[2])
        es = [jnp.exp(l - lmax) for l in ls]
        den = es[0] + es[1] + es[2]
        lse_ref[...] = lmax + jnp.log(den)
        inv = 1.0 / den
        ws = [e * inv for e in es]
        for p in range(N_PAIRS):
            cs = slice(p * LANES, (p + 1) * LANES)
            acc = jnp.zeros((tm, LANES), F32)
            for w, o_g in zip(ws, (o1, o2, o3)):
                wp = jnp.where(lo, w[:, 2 * p:2 * p + 1], w[:, 2 * p + 1:2 * p + 2])
                acc = acc + wp * o_g[:, cs].astype(F32)
            o_ref[:, cs] = acc.astype(BF16)
            za = za_ref[:, cs].astype(F32)
            y_ref[:, CONV_W + p * LANES:CONV_W + (p + 1) * LANES] = (acc * za * _sigmoid(za)).astype(BF16)
        a = gc_ref[...].astype(F32) * u_ref[...].astype(F32)
        a_prev = gcp_ref[15:16, :].astype(F32) * up_ref[15:16, :].astype(F32) * (i > 0).astype(F32)
        a_next = gcn_ref[0:1, :].astype(F32) * un_ref[0:1, :].astype(F32) * (i < n_t - 1).astype(F32)
        a_m1, a_p1 = _shift_rows(a, a_prev, a_next, tm)
        z3 = cw_ref[0:1, :] * a_m1 + cw_ref[1:2, :] * a + cw_ref[2:3, :] * a_p1
        zc = zc_ref[...].astype(F32)
        y_ref[:, 0:CONV_W] = (gb_ref[...].astype(F32) * (z3 + cb_ref[...]) * zc * _sigmoid(zc)).astype(BF16)

    pspec = pl.BlockSpec((tm, CONV_W), lambda i: (i, 0))
    sspec = pl.BlockSpec((tm, LANES), lambda i: (i, 0))
    hp, hn = _row_halo_specs(CONV_W, tm, S)
    return pl.pallas_call(
        body, name="combine_gates", grid=(n_t,),
        out_shape=(jax.ShapeDtypeStruct((S, D_MODEL), BF16), jax.ShapeDtypeStruct((S, ATTN_W), BF16),
                   jax.ShapeDtypeStruct((S, LANES), F32)),
        in_specs=[pspec, pspec, pspec, sspec, sspec, sspec, pspec, hp, hn, pspec, pspec, hp, hn, pspec, pspec,
                  pl.BlockSpec((3, CONV_W), lambda i: (0, 0)), pl.BlockSpec((1, CONV_W), lambda i: (0, 0))],
        out_specs=(pl.BlockSpec((tm, D_MODEL), lambda i: (i, 0)), pspec, sspec),
        compiler_params=_params(("arbitrary",), 40),
    )(*o_parts, *lse_parts, u, u, u, gb, gc, gc, gc, zc, za, conv_w, conv_b.reshape(1, CONV_W))


def _out_proj(y, x, target, wo):
    S = x.shape[0]
    tm = 512

    def body(y_ref, x_ref, t_ref, wo_ref, g_ref, dy_ref, dwo_ref, loss_ref):
        i = pl.program_id(0)

        @pl.when(i == 0)
        def _():
            dwo_ref[...] = jnp.zeros_like(dwo_ref)
            loss_ref[...] = jnp.zeros_like(loss_ref)

        yv = y_ref[...]
        wo_v = wo_ref[...]
        err = x_ref[...] + jnp.dot(yv, wo_v, preferred_element_type=F32) - t_ref[...]
        e2 = (err * err).reshape(tm // 8, 8, D_MODEL).sum(axis=0)
        part = e2[:, 0:LANES]
        for k in range(1, D_MODEL // LANES):
            part = part + e2[:, k * LANES:(k + 1) * LANES]
        loss_ref[...] += part
        gv = (err * (1.0 / D_MODEL)).astype(BF16)
        g_ref[...] = gv
        dy_ref[...] = lax.dot_general(gv, wo_v, (((1,), (1,)), ((), ())), preferred_element_type=F32).astype(BF16)
        dwo_ref[...] += lax.dot_general(yv, gv, (((0,), (0,)), ((), ())), preferred_element_type=F32)

    tile = pl.BlockSpec((tm, D_MODEL), lambda i: (i, 0))
    return pl.pallas_call(
        body, name="out_proj", grid=(S // tm,),
        out_shape=(jax.ShapeDtypeStruct((S, D_MODEL), BF16), jax.ShapeDtypeStruct((S, D_MODEL), BF16),
                   jax.ShapeDtypeStruct((D_MODEL, D_MODEL), F32), jax.ShapeDtypeStruct((8, LANES), F32)),
        in_specs=[tile, tile, tile, pl.BlockSpec((D_MODEL, D_MODEL), lambda i: (0, 0), pipeline_mode=pl.Buffered(1))],
        out_specs=(tile, tile, pl.BlockSpec((D_MODEL, D_MODEL), lambda i: (0, 0)),
                   pl.BlockSpec((8, LANES), lambda i: (0, 0))),
        compiler_params=_params(("arbitrary",), 48),
    )(y, x, target, wo)


def _gates_bwd(dy, u, gb, gc, zc, za, o, conv_w, conv_b):
    S = u.shape[0]
    tm = 512
    n_t = S // tm

    def body(dy_ref, dyp_ref, dyn_ref, u_ref, up_ref, un_ref, gb_ref, gbp_ref, gbn_ref, gc_ref, gcp_ref, gcn_ref,
             zc_ref, zcp_ref, zcn_ref, za_ref, o_ref, cw_ref, cb_ref,
             du_ref, dgb_ref, dgc_ref, dzc_ref, dza_ref, do_ref, delta_ref, small_ref):
        i = pl.program_id(0)

        @pl.when(i == 0)
        def _():
            small_ref[...] = jnp.zeros_like(small_ref)

        def dz3_of(dyc, zc, gb):
            return dyc * zc * _sigmoid(zc) * gb

        f = lambda ref, rows=slice(None): ref[rows, :].astype(F32)
        has_prev = (i > 0).astype(F32)
        has_next = (i < n_t - 1).astype(F32)
        pr, nr = slice(15, 16), slice(0, 1)
        uv, gbv, gcv, zcv = f(u_ref), f(gb_ref), f(gc_ref), f(zc_ref)
        dyc = dy_ref[:, 0:CONV_W].astype(F32)
        a = gcv * uv
        a_m1, a_p1 = _shift_rows(a, f(gcp_ref, pr) * f(up_ref, pr) * has_prev,
                                 f(gcn_ref, nr) * f(un_ref, nr) * has_next, tm)
        w0, w1, w2 = cw_ref[0:1, :], cw_ref[1:2, :], cw_ref[2:3, :]
        cvp = w0 * a_m1 + w1 * a + w2 * a_p1 + cb_ref[...]
        sz = _sigmoid(zcv)
        dcv = dyc * zcv * sz
        dzc_ref[...] = (dyc * gbv * cvp * (sz * (1.0 + zcv * (1.0 - sz)))).astype(BF16)
        dgb_ref[...] = (dcv * cvp).astype(BF16)
        dz3 = dcv * gbv
        dz3_prev = dz3_of(f(dyp_ref, pr), f(zcp_ref, pr), f(gbp_ref, pr)) * has_prev
        dz3_next = dz3_of(f(dyn_ref, nr), f(zcn_ref, nr), f(gbn_ref, nr)) * has_next
        dz3_m1, dz3_p1 = _shift_rows(dz3, dz3_prev, dz3_next, tm)
        da = w0 * dz3_p1 + w1 * dz3 + w2 * dz3_m1
        du_ref[...] = (da * gcv).astype(BF16)
        dgc_ref[...] = (da * uv).astype(BF16)
        csum = lambda t: jnp.sum(t, axis=0, keepdims=True)
        small_ref[0:1, :] += csum(dz3 * a_m1)
        small_ref[1:2, :] += csum(dz3 * a)
        small_ref[2:3, :] += csum(dz3 * a_p1)
        small_ref[3:4, :] += csum(dz3)

        lane = _lane((tm, LANES))
        lo = lane < HEAD_DIM
        delta = jnp.zeros((tm, LANES), F32)
        for p in range(N_PAIRS):
            cs = slice(p * LANES, (p + 1) * LANES)
            dya = dy_ref[:, CONV_W + p * LANES:CONV_W + (p + 1) * LANES].astype(F32)
            zav = za_ref[:, cs].astype(F32)
            ov = o_ref[:, cs].astype(F32)
            sa = _sigmoid(zav)
            d_o = dya * zav * sa
            do_ref[:, cs] = d_o.astype(BF16)
            dza_ref[:, cs] = (dya * ov * (sa * (1.0 + zav * (1.0 - sa)))).astype(BF16)
            s_lo, s_hi = _head_sums(d_o * ov, lo)
            delta = jnp.where(lane == 2 * p, s_lo, delta)
            delta = jnp.where(lane == 2 * p + 1, s_hi, delta)
        delta_ref[...] = delta

    pspec = pl.BlockSpec((tm, CONV_W), lambda i: (i, 0))
    hp, hn = _row_halo_specs(CONV_W, tm, S)
    piece = jax.ShapeDtypeStruct((S, CONV_W), BF16)
    return pl.pallas_call(
        body, name="gates_bwd", grid=(n_t,),
        out_shape=(piece,) * 6 + (jax.ShapeDtypeStruct((S, LANES), F32), jax.ShapeDtypeStruct((8, CONV_W), F32)),
        in_specs=[pl.BlockSpec((tm, D_MODEL), lambda i: (i, 0)), hp, hn,
                  pspec, hp, hn, pspec, hp, hn, pspec, hp, hn, pspec, hp, hn, pspec, pspec,
                  pl.BlockSpec((3, CONV_W), lambda i: (0, 0)), pl.BlockSpec((1, CONV_W), lambda i: (0, 0))],
        out_specs=(pspec,) * 6 + (pl.BlockSpec((tm, LANES), lambda i: (i, 0)),
                                  pl.BlockSpec((8, CONV_W), lambda i: (0, 0))),
        compiler_params=_params(("arbitrary",), 40),
    )(dy, dy, dy, u, u, u, gb, gb, gb, gc, gc, gc, zc, zc, zc, za, o, conv_w, conv_b.reshape(1, CONV_W))


def _attn_bwd(g, dil, q_hat, k_hat, v, d_o, lse, delta, rq, rk, gain, tables):
    S = q_hat.shape[0]
    L = S // dil
    T = min(1024, L)
    n_sub = T // QB + 1
    QW = T + 2 * HALF
    KW = T + 2 * QB
    view = lambda a: a.reshape(L, dil * a.shape[1])

    def body(qm_ref, qp_ref, qn_ref, dom_ref, dop_ref, don_ref, km_ref, kp_ref, kn_ref, vm_ref, vp_ref, vn_ref,
             lm_ref, lp_ref, ln_ref, dm_ref, dp_ref, dn_ref, rqm_ref, rqp_ref, rqn_ref, rk_ref,
             gain_ref, bias_ref,
             dq_ref, dk_ref, dv_ref, ds_ref, a_ref,
             qwin, dowin, kwin, vwin, lwin, dwin, rqwin, dqwin, dkacc, dvacc):
        r = pl.program_id(0)
        i = pl.program_id(1)

        @pl.when((r == 0) & (i == 0))
        def _():
            ds_ref[...] = jnp.zeros_like(ds_ref)
            a_ref[...] = jnp.zeros_like(a_ref)

        for win, (pr, mn, nx), h_rows in ((qwin, (qp_ref, qm_ref, qn_ref), HALF),
                                          (dowin, (dop_ref, dom_ref, don_ref), HALF),
                                          (lwin, (lp_ref, lm_ref, ln_ref), HALF),
                                          (dwin, (dp_ref, dm_ref, dn_ref), HALF),
                                          (rqwin, (rqp_ref, rqm_ref, rqn_ref), HALF),
                                          (kwin, (kp_ref, km_ref, kn_ref), QB),
                                          (vwin, (vp_ref, vm_ref, vn_ref), QB)):
            win[0:h_rows] = pr[...]
            win[h_rows:h_rows + T] = mn[...]
            win[h_rows + T:] = nx[...]
        dkacc[...] = jnp.zeros_like(dkacc)
        dvacc[...] = jnp.zeros_like(dvacc)

        lane = _lane((QB, LANES))
        lo = lane < HEAD_DIM
        col = _lane((1, KB))
        row = lax.broadcasted_iota(jnp.int32, (QB, 1), 0)

        def sub(j, carry):
            r0 = pl.multiple_of(j * QB, QB)
            kpos = i * T - QB + j * QB + col
            qpos = i * T - HALF + j * QB + row
            valid = (kpos >= 0) & (kpos < L) & (qpos >= 0) & (qpos < L)
            owned = ((qpos >= i * T) & (qpos < i * T + T)).astype(F32)
            lse_t = lwin[pl.ds(r0, QB), :]
            delta_t = dwin[pl.ds(r0, QB), :]
            rq_t = rqwin[pl.ds(r0, QB), :]
            for p in range(N_PAIRS):
                cs = slice(p * LANES, (p + 1) * LANES)
                g2 = gain_ref[:, cs]
                q_f = qwin[pl.ds(r0, QB), cs].astype(F32)
                qf = q_f * g2
                do2 = dowin[pl.ds(r0, QB), cs]
                kw = kwin[pl.ds(r0, KB), cs]
                vw = vwin[pl.ds(r0, KB), cs]
                dqm = []
                dk_c = jnp.zeros((KB, LANES), F32)
                dv_c = jnp.zeros((KB, LANES), F32)
                for e in range(2):
                    h = 2 * p + e
                    mask = lo if e == 0 else ~lo
                    qm = jnp.where(mask, qf, 0.0).astype(BF16)
                    do_e = jnp.where(mask, do2, jnp.zeros_like(do2))
                    s = lax.dot_general(qm, kw, (((1,), (1,)), ((), ())), preferred_element_type=F32)
                    pe = jnp.where(valid, jnp.exp(s + bias_ref[0, h] - lse_t[:, h:h + 1]), 0.0)
                    dp = lax.dot_general(do_e, vw, (((1,), (1,)), ((), ())), preferred_element_type=F32)
                    ds = pe * (dp - delta_t[:, h:h + 1])
                    ds_ref[h] += ds * owned
                    pb = pe.astype(BF16)
                    dsb = ds.astype(BF16)
                    dv_c = dv_c + lax.dot_general(pb, do_e, (((0,), (0,)), ((), ())), preferred_element_type=F32)
                    dk_c = dk_c + lax.dot_general(dsb, qm, (((0,), (0,)), ((), ())), preferred_element_type=F32)
                    dqm.append(jnp.dot(dsb, kw, preferred_element_type=F32))
                dkacc[pl.ds(r0, KB), cs] += dk_c
                dvacc[pl.ds(r0, KB), cs] += dv_c
                dqm2 = jnp.where(lo, dqm[0], dqm[1])
                a_ref[p] += (dqm2 * q_f * owned).reshape(QB // 8, 8, LANES).sum(axis=0)
                dqh = dqm2 * g2
                c_lo, c_hi = _head_sums(dqh * q_f, lo)
                mean_c = jnp.where(lo, c_lo, c_hi) * (1.0 / HEAD_DIM)
                rq2 = jnp.where(lo, rq_t[:, 2 * p:2 * p + 1], rq_t[:, 2 * p + 1:2 * p + 2])
                dqwin[pl.ds(r0, QB), cs] = (rq2 * (dqh - q_f * mean_c)).astype(BF16)
            return carry

        lax.fori_loop(0, n_sub, sub, 0)
        dq_ref[...] = dqwin[HALF:HALF + T]

        def finish(j, carry):
            r0 = pl.multiple_of(j * QB, QB)
            rk_t = rk_ref[pl.ds(r0, QB), :]
            for p in range(N_PAIRS):
                cs = slice(p * LANES, (p + 1) * LANES)
                dkh = dkacc[pl.ds(QB + r0, QB), cs]
                k_f = km_ref[pl.ds(r0, QB), cs].astype(F32)
                c_lo, c_hi = _head_sums(dkh * k_f, lo)
                mean_c = jnp.where(lo, c_lo, c_hi) * (1.0 / HEAD_DIM)
                rk2 = jnp.where(lo, rk_t[:, 2 * p:2 * p + 1], rk_t[:, 2 * p + 1:2 * p + 2])
                dk_ref[pl.ds(r0, QB), cs] = (rk2 * (dkh - k_f * mean_c)).astype(BF16)
                dv_ref[pl.ds(r0, QB), cs] = dvacc[pl.ds(QB + r0, QB), cs].astype(BF16)
            return carry

        lax.fori_loop(0, T // QB, finish, 0)

    main = pl.BlockSpec((T, ATTN_W), lambda r, i: (i, r))
    smain = pl.BlockSpec((T, LANES), lambda r, i: (i, r))
    q_prev, q_next = _halo_specs(ATTN_W, HALF, T, L, lambda r: r)
    k_prev, k_next = _halo_specs(ATTN_W, QB, T, L, lambda r: r)
    s_prev, s_next = _halo_specs(LANES, HALF, T, L, lambda r: r)
    piece = jax.ShapeDtypeStruct((L, dil * ATTN_W), BF16)
    qv, dov, kv, vv = view(q_hat), view(d_o), view(k_hat), view(v)
    lv, dv_, rqv, rkv = view(lse), view(delta), view(rq), view(rk)
    dq, dk, dv, ds_sum, a_sum = pl.pallas_call(
        body, name=f"attn_bwd_d{dil}", grid=(dil, L // T),
        out_shape=(piece, piece, piece, jax.ShapeDtypeStruct((N_HEADS, QB, KB), F32),
                   jax.ShapeDtypeStruct((N_PAIRS, 8, LANES), F32)),
        in_specs=[main, q_prev, q_next, main, q_prev, q_next, main, k_prev, k_next, main, k_prev, k_next,
                  smain, s_prev, s_next, smain, s_prev, s_next, smain, s_prev, s_next, smain,
                  pl.BlockSpec((1, ATTN_W), lambda r, i: (0, 0)),
                  pl.BlockSpec((1, N_HEADS, QB, KB), lambda r, i: (g, 0, 0, 0))],
        out_specs=(main, main, main, pl.BlockSpec((N_HEADS, QB, KB), lambda r, i: (0, 0, 0)),
                   pl.BlockSpec((N_PAIRS, 8, LANES), lambda r, i: (0, 0, 0))),
        scratch_shapes=[pltpu.VMEM((QW, ATTN_W), BF16), pltpu.VMEM((QW, ATTN_W), BF16),
                        pltpu.VMEM((KW, ATTN_W), BF16), pltpu.VMEM((KW, ATTN_W), BF16),
                        pltpu.VMEM((QW, LANES), F32), pltpu.VMEM((QW, LANES), F32), pltpu.VMEM((QW, LANES), F32),
                        pltpu.VMEM((QW, ATTN_W), BF16),
                        pltpu.VMEM((KW, ATTN_W), F32), pltpu.VMEM((KW, ATTN_W), F32)],
        compiler_params=_params(("arbitrary", "arbitrary"), 56),
    )(qv, qv, qv, dov, dov, dov, kv, kv, kv, vv, vv, vv, lv, lv, lv, dv_, dv_, dv_, rqv, rqv, rqv, rkv, gain, tables)
    return dq.reshape(S, ATTN_W), dk.reshape(S, ATTN_W), dv.reshape(S, ATTN_W), ds_sum, a_sum


def _proj_bwd(x, d_out, norm_w, wg, pieces):
    S = x.shape[0]
    tm = 256
    counts = [len(p) for p in pieces]
    flat = [a for p in pieces for a in p]
    n_t = S // tm

    def body(*refs):
        x_ref, g_ref, nw_ref, wg_ref = refs[:4]
        piece_refs = refs[4:4 + len(flat)]
        gx_ref, dwin_hbm, dnw_ref, acc, sem = refs[4 + len(flat):]
        i = pl.program_id(0)

        @pl.when(i == 0)
        def _():
            acc[...] = jnp.zeros_like(acc)
            dnw_ref[...] = jnp.zeros_like(dnw_ref)

        xv = x_ref[...]
        r = lax.rsqrt(jnp.mean(xv * xv, axis=-1, keepdims=True) + EPS)
        xh = xv * r
        nw = nw_ref[...]
        h = (xh * nw).astype(BF16)
        dh = jnp.zeros((tm, D_MODEL), F32)
        k = 0
        for j in range(8):
            dp = piece_refs[k][...]
            if counts[j] > 1:
                dp = dp.astype(F32)
                for t in range(1, counts[j]):
                    dp = dp + piece_refs[k + t][...].astype(F32)
                dp = dp.astype(BF16)
            k += counts[j]
            dh = dh + lax.dot_general(dp, wg_ref[j], (((1,), (1,)), ((), ())), preferred_element_type=F32)
            acc[j] += lax.dot_general(h, dp, (((0,), (0,)), ((), ())), preferred_element_type=F32)
        dnw_ref[...] += (dh * xh).reshape(tm // 8, 8, D_MODEL).sum(axis=0)
        dxh = dh * nw
        mean_c = jnp.mean(dxh * xh, axis=-1, keepdims=True)
        gx_ref[...] = g_ref[...].astype(F32) + r * (dxh - xh * mean_c)

        @pl.when(i == n_t - 1)
        def _():
            cp = pltpu.make_async_copy(acc, dwin_hbm, sem)
            cp.start()
            cp.wait()

    tile = pl.BlockSpec((tm, D_MODEL), lambda i: (i, 0))
    pspec = pl.BlockSpec((tm, CONV_W), lambda i: (i, 0))
    return pl.pallas_call(
        body, name="proj_bwd", grid=(n_t,),
        out_shape=(jax.ShapeDtypeStruct((S, D_MODEL), F32), jax.ShapeDtypeStruct(wg.shape, F32),
                   jax.ShapeDtypeStruct((8, D_MODEL), F32)),
        in_specs=[tile, tile, pl.BlockSpec((1, D_MODEL), lambda i: (0, 0)),
                  pl.BlockSpec(wg.shape, lambda i: (0, 0, 0), pipeline_mode=pl.Buffered(1))] + [pspec] * len(flat),
        out_specs=(tile, pl.BlockSpec(memory_space=pl.ANY), pl.BlockSpec((8, D_MODEL), lambda i: (0, 0))),
        scratch_shapes=[pltpu.VMEM(wg.shape, F32), pltpu.SemaphoreType.DMA],
        compiler_params=_params(("arbitrary",), 56),
    )(x, d_out, norm_w.reshape(1, D_MODEL), wg, *flat)


def _exchange_grads(gw_in, gw_out, small):
    n_arr = 3

    def body(gin_ref, gout_ref, sm_ref, rin_ref, rout_ref, rsm_ref, send_sems, recv_sems, local_sems):
        x, y, c = lax.axis_index("x"), lax.axis_index("y"), lax.axis_index("c")
        me = 4 * x + 2 * y + c
        locals_ = [pltpu.make_async_copy(gin_ref.at[me], rin_ref.at[me], local_sems.at[0]),
                   pltpu.make_async_copy(gout_ref.at[me], rout_ref.at[me], local_sems.at[1])]
        for cp in locals_:
            cp.start()
        rsm_ref[me] = sm_ref[...]
        copies = []
        for k in range(1, N_DEV):
            px = 1 - x if k & 4 else x
            py = 1 - y if k & 2 else y
            pc = 1 - c if k & 1 else c
            peer = 4 * px + 2 * py + pc
            srcs = (gin_ref.at[peer], gout_ref.at[peer], sm_ref)
            dsts = (rin_ref.at[me], rout_ref.at[me], rsm_ref.at[me])
            for a in range(n_arr):
                copies.append(pltpu.make_async_remote_copy(
                    src_ref=srcs[a], dst_ref=dsts[a], send_sem=send_sems.at[a * 7 + k - 1],
                    recv_sem=recv_sems.at[a * 7 + k - 1], device_id=(px, py, pc), device_id_type=MESH))
        for cp in copies:
            cp.start()
        for cp in copies:
            cp.wait_recv()
        for cp in copies:
            cp.wait_send()
        for cp in locals_:
            cp.wait()

    hbm = pl.BlockSpec(memory_space=pl.ANY)
    vm = pl.BlockSpec(memory_space=pltpu.VMEM)
    return pl.pallas_call(
        body, name="exchange_grads",
        out_shape=(jax.ShapeDtypeStruct(gw_in.shape, F32), jax.ShapeDtypeStruct(gw_out.shape, F32),
                   jax.ShapeDtypeStruct((N_DEV,) + small.shape, F32)),
        in_specs=[hbm, hbm, vm], out_specs=(hbm, hbm, vm),
        scratch_shapes=[pltpu.SemaphoreType.DMA((n_arr * 7,)), pltpu.SemaphoreType.DMA((n_arr * 7,)),
                        pltpu.SemaphoreType.DMA((2,))],
        compiler_params=pltpu.CompilerParams(vmem_limit_bytes=16 * MIB),
    )(gw_in, gw_out, small)


def _adamw_math(w, g, m, v):
    m2 = ADAM_B1 * m + (1.0 - ADAM_B1) * g
    v2 = ADAM_B2 * v + (1.0 - ADAM_B2) * (g * g)
    m_hat = m2 / (1.0 - ADAM_B1 ** ADAM_STEP)
    v_hat = v2 / (1.0 - ADAM_B2 ** ADAM_STEP)
    delta = -ADAM_LR * (m_hat / (jnp.sqrt(v_hat) + ADAM_EPS) + ADAM_WD * w)
    return delta, m2, v2


def _adamw_sharded(name, parts, w, m, v, rows):
    R, C = w.shape

    def body(p_ref, w_ref, m_ref, v_ref, g_ref, d_ref, m2_ref, v2_ref):
        g = p_ref[0]
        for s in range(1, N_DEV):
            g = g + p_ref[s]
        g_ref[...] = g
        d_ref[...], m2_ref[...], v2_ref[...] = _adamw_math(w_ref[...], g, m_ref[...], v_ref[...])

    spec = pl.BlockSpec((rows, C), lambda i: (i, 0))
    out = jax.ShapeDtypeStruct((R, C), F32)
    return pl.pallas_call(
        body, name=name, grid=(R // rows,),
        out_shape=(out,) * 4,
        in_specs=[pl.BlockSpec((N_DEV, rows, C), lambda i: (0, i, 0)), spec, spec, spec],
        out_specs=(spec,) * 4,
        compiler_params=_params(("arbitrary",), 40),
    )(parts, w, m, v)


def _adamw_small(parts, w, m, v):
    R = w.shape[0]

    def body(p_ref, w_ref, m_ref, v_ref, g_ref, d_ref, m2_ref, v2_ref):
        g = p_ref[0]
        for s in range(1, N_DEV):
            g = g + p_ref[s]
        g_ref[...] = g
        d_ref[...], m2_ref[...], v2_ref[...] = _adamw_math(w_ref[...], g, m_ref[...], v_ref[...])

    vm = pl.BlockSpec(memory_space=pltpu.VMEM)
    out = jax.ShapeDtypeStruct((R, LANES), F32)
    return pl.pallas_call(
        body, name="adamw_small", out_shape=(out,) * 4, in_specs=[vm] * 4, out_specs=(vm,) * 4,
    )(parts, w, m, v)


def _qk_gain_grads(a_parts, q_norm_w, k_norm_w, bias_g):
    def body(a_ref, qw_ref, kw_ref, bg_ref, out_ref):
        tot = jnp.zeros((8, LANES), F32)
        for g in range(len(DILATIONS)):
            for p in range(N_PAIRS):
                tot = tot + a_ref[g, p]
        col = jnp.sum(tot, axis=0, keepdims=True)
        a64 = col[:, 0:HEAD_DIM] + col[:, HEAD_DIM:LANES]
        out_ref[...] = jnp.zeros_like(out_ref)
        out_ref[0:1, 0:HEAD_DIM] = LOGIT_SCALE * kw_ref[...] * a64
        out_ref[1:2, 0:HEAD_DIM] = LOGIT_SCALE * qw_ref[...] * a64
        for h in range(N_HEADS):
            r0, c0 = 2 + h // 4, (h % 4) * N_BUCKETS
            out_ref[r0:r0 + 1, c0:c0 + N_BUCKETS] = bg_ref[h, 0:1, 0:N_BUCKETS]

    vm = pl.BlockSpec(memory_space=pltpu.VMEM)
    return pl.pallas_call(
        body, name="qk_gain_grads", out_shape=jax.ShapeDtypeStruct((8, LANES), F32),
        in_specs=[vm] * 4, out_specs=vm,
    )(a_parts, q_norm_w.reshape(1, HEAD_DIM), k_norm_w.reshape(1, HEAD_DIM), bias_g)


SMALL_ROWS = 32


def _pack_small(norm_w, conv_b, q_norm_w, k_norm_w, rel_bias, conv_w_full):
    pad64 = lambda a: jnp.pad(a, (0, LANES - HEAD_DIM)).reshape(1, LANES)
    return jnp.concatenate([
        norm_w.reshape(8, LANES), conv_b.reshape(4, LANES), pad64(q_norm_w), pad64(k_norm_w),
        rel_bias.T.reshape(2, LANES), conv_w_full.reshape(12, LANES), jnp.zeros((4, LANES), F32)], axis=0)


def _unpack_small(s):
    return (s[0:8].reshape(D_MODEL), s[8:12].reshape(CONV_W), s[12, 0:HEAD_DIM], s[13, 0:HEAD_DIM],
            s[14:16].reshape(N_HEADS, N_BUCKETS).T, s[16:28].reshape(3, CONV_W))


def kernel(x, norm_w, w_in, conv_w, conv_b, q_norm_w, k_norm_w, rel_bias, w_out, loss_target, m_norm_w, m_w_in, m_conv_w, m_conv_b, m_q_norm_w, m_k_norm_w, m_rel_bias, m_w_out, v_norm_w, v_w_in, v_conv_w, v_conv_b, v_q_norm_w, v_k_norm_w, v_rel_bias, v_w_out):
    S = x.shape[1]
    x2 = x.reshape(S, D_MODEL)
    tgt = loss_target.reshape(S, D_MODEL)
    me = 4 * lax.axis_index("x") + 2 * lax.axis_index("y") + lax.axis_index("c")
    shard_w = CONV_W // N_DEV

    cw_pad = jnp.pad(conv_w, ((0, 5), (0, LANES - shard_w)))
    wg, wo_g, cw_g = _ag_weights(w_in, w_out, cw_pad)
    wo = wo_g.reshape(D_MODEL, D_MODEL)
    conv_w_full = cw_g[:, 0:3, 0:shard_w].transpose(1, 0, 2).reshape(3, CONV_W)

    bmat = jnp.asarray(np.stack([_bucket_matrix(d) for d in DILATIONS]))
    tables = _bias_tables(rel_bias, bmat)
    gain = jnp.tile(q_norm_w * k_norm_w * LOGIT_SCALE, N_HEADS).reshape(1, ATTN_W)

    u, gb, gc, zc, q_hat, k_hat, v, za, rq, rk = _fwd_proj(x2, norm_w, wg)
    o_parts, lse_parts = [], []
    for g, dil in enumerate(DILATIONS):
        o_g, lse_g = _attn_fwd(g, dil, q_hat, k_hat, v, gain, tables)
        o_parts.append(o_g)
        lse_parts.append(lse_g)
    y, o, lse = _combine_gates(o_parts, lse_parts, u, gb, gc, zc, za, conv_w_full, conv_b)
    d_out, dy, gwo_part, loss_part = _out_proj(y, x2, tgt, wo)
    loss = lax.psum(jnp.sum(loss_part) * (0.5 / D_MODEL), ("x", "y", "c"))

    du, dgb, dgc, dzc, dza, d_o, delta, conv_small = _gates_bwd(dy, u, gb, gc, zc, za, o, conv_w_full, conv_b)
    dqs, dks, dvs, ds_parts, a_parts = [], [], [], [], []
    for g, dil in enumerate(DILATIONS):
        dq, dk, dv, ds_sum, a_sum = _attn_bwd(g, dil, q_hat, k_hat, v, d_o, lse, delta, rq, rk, gain, tables)
        dqs.append(dq)
        dks.append(dk)
        dvs.append(dv)
        ds_parts.append(ds_sum)
        a_parts.append(a_sum)
    grad_x, gwin_part, gnw_part = _proj_bwd(x2, d_out, norm_w, wg, [[du], [dgb], [dgc], [dzc], dqs, dks, dvs, [dza]])
    bias_g = _bias_grad(jnp.stack(ds_parts), bmat)
    qk_small = _qk_gain_grads(jnp.stack(a_parts), q_norm_w, k_norm_w, bias_g)

    small_part = jnp.concatenate([
        gnw_part.sum(axis=0).reshape(8, LANES), conv_small[3].reshape(4, LANES), qk_small[0:4],
        conv_small[0:3].reshape(12, LANES), jnp.zeros((4, LANES), F32)], axis=0)
    r_in, r_out, r_small = _exchange_grads(gwin_part, gwo_part.reshape(N_DEV, D_MODEL // N_DEV, D_MODEL), small_part)

    g_win, d_win, m_win, v_win = _adamw_sharded("adamw_w_in", r_in, w_in, m_w_in, v_w_in, 128)
    g_wo, d_wo, m_wo, v_wo = _adamw_sharded("adamw_w_out", r_out, w_out, m_w_out, v_w_out, 32)

    def full_conv(a):
        return jnp.zeros((3, CONV_W), F32).at[:, 0:shard_w].set(a)

    packs = [_pack_small(nw_, cb_, qw_, kw_, rb_, full_conv(cw_)) for nw_, cb_, qw_, kw_, rb_, cw_ in (
        (norm_w, conv_b, q_norm_w, k_norm_w, rel_bias, conv_w),
        (m_norm_w, m_conv_b, m_q_norm_w, m_k_norm_w, m_rel_bias, m_conv_w),
        (v_norm_w, v_conv_b, v_q_norm_w, v_k_norm_w, v_rel_bias, v_conv_w))]
    r_small_mine = r_small.at[:, 16:28, :].set(
        jnp.pad(lax.dynamic_slice_in_dim(r_small[:, 16:28, :].reshape(N_DEV, 3, CONV_W), me * shard_w, shard_w, axis=2),
                ((0, 0), (0, 0), (0, CONV_W - shard_w))).reshape(N_DEV, 12, LANES))
    outs_small = _adamw_small(r_small_mine, *packs)
    g_s, d_s, m_s, v_s = [_unpack_small(t) for t in outs_small]

    def leaves(small, big_in, big_out):
        nw_, cb_, qw_, kw_, rb_, cwf = small
        return (nw_, big_in, cwf[:, 0:shard_w], cb_, qw_, kw_, rb_, big_out)

    return (loss, grad_x.reshape(x.shape),
            *leaves(g_s, g_win, g_wo), *leaves(d_s, d_win, d_wo), *leaves(m_s, m_win, m_wo), *leaves(v_s, v_win, v_wo))
```

```python
import functools
import math

import numpy as np
import jax
import jax.numpy as jnp
from jax import lax
from jax.experimental import pallas as pl
from jax.experimental.pallas import tpu as pltpu

F32 = jnp.float32
BF16 = jnp.bfloat16

N_DEV = 8
D_MODEL = 1024
CONV_W = 512
ATTN_W = 512
N_HEADS = 8
HEAD_DIM = 64
N_PAIRS = N_HEADS // 2
LANES = 128
HALF = 64
QB = 128
KB = QB + 2 * HALF
DILATIONS = (1, 4, 16)
N_BUCKETS = 32
MAX_DISTANCE = 1024
EPS = 1e-6
NEG = -1e30
LOGIT_SCALE = HEAD_DIM ** -0.5

ADAM_LR = 0.001
ADAM_B1 = 0.9
ADAM_B2 = 0.999
ADAM_EPS = 1e-08
ADAM_WD = 0.01
ADAM_STEP = 10

MESH = pl.DeviceIdType.MESH
MIB = 1024 * 1024


def _params(semantics, vmem_mib):
    return pltpu.CompilerParams(dimension_semantics=semantics, vmem_limit_bytes=vmem_mib * MIB)


def _lane(shape):
    return lax.broadcasted_iota(jnp.int32, shape, len(shape) - 1)


def _sigmoid(z):
    return 1.0 / (1.0 + jnp.exp(-z))


def _head_sums(prod, lo_mask):
    s_lo = jnp.sum(jnp.where(lo_mask, prod, 0.0), axis=-1, keepdims=True)
    s_hi = jnp.sum(jnp.where(lo_mask, 0.0, prod), axis=-1, keepdims=True)
    return s_lo, s_hi


VIEW_DILATIONS = DILATIONS[1:]


def _view_shape(S, dil, width, dtype):
    return jax.ShapeDtypeStruct((S // dil, dil * width), dtype)


def _view_spec(tm, dil, width):
    return pl.BlockSpec((tm // dil, dil * width), lambda i: (i, 0))


def _tile_scratch(tm, width):
    return pltpu.VMEM((width // LANES, tm, LANES), F32)


def _scratch_value(scr_ref):
    n = scr_ref.shape[0]
    return scr_ref[0] if n == 1 else jnp.concatenate([scr_ref[c] for c in range(n)], axis=1)


def _to_view(scr_ref, out_ref, dil, dtype):
    n, tm, _ = scr_ref.shape
    for r in range(dil):
        for c in range(n):
            col = (r * n + c) * LANES
            out_ref[:, col:col + LANES] = scr_ref[c, pl.ds(r, tm // dil, stride=dil), :].astype(dtype)


def _from_view(blk_ref, scr_ref, dil):
    n, tm, _ = scr_ref.shape
    for r in range(dil):
        for c in range(n):
            col = (r * n + c) * LANES
            scr_ref[c, pl.ds(r, tm // dil, stride=dil), :] = blk_ref[:, col:col + LANES].astype(F32)


def _bucket_matrix(dilation):
    rel = np.arange(KB)[None, :] - HALF - np.arange(QB)[:, None]
    band = np.abs(rel) <= HALF
    dist = np.clip(rel, -HALF, HALF) * dilation
    half_b = N_BUCKETS // 2
    max_exact = half_b // 2
    ret = np.where(dist > 0, half_b, 0)
    n = np.abs(dist)
    nf = np.maximum(n, 1).astype(np.float32)
    large = max_exact + (np.log(nf / np.float32(max_exact)) / np.float32(math.log(MAX_DISTANCE / max_exact))
                         * np.float32(half_b - max_exact)).astype(np.int32)
    large = np.minimum(large, half_b - 1)
    bucket = ret + np.where(n < max_exact, n, large)
    return np.where(band, bucket, -1).astype(np.int32)


def _ag_weights(w_in, w_out, cw_pad):
    n_arr = 3

    def body(win_ref, wout_ref, cw_ref, gin_ref, gout_ref, gcw_ref, send_sems, recv_sems):
        x, y, c = lax.axis_index("x"), lax.axis_index("y"), lax.axis_index("c")
        me = (x, y, c)
        sibling = (x, y, 1 - c)
        chips = [(1 - x, y), (x, 1 - y), (1 - x, 1 - y)]
        arrays = (gin_ref, gout_ref, gcw_ref)

        def slot(px, py, pc):
            return 4 * px + 2 * py + pc

        gin_ref[slot(*me)] = win_ref[...].astype(BF16)
        gout_ref[slot(*me)] = wout_ref[...].astype(BF16)
        gcw_ref[slot(*me)] = cw_ref[...]

        def copy(a, k, block, to):
            ref = arrays[a].at[slot(*block)]
            return pltpu.make_async_remote_copy(
                src_ref=ref, dst_ref=ref, send_sem=send_sems.at[a * 7 + k], recv_sem=recv_sems.at[a * 7 + k],
                device_id=to, device_id_type=MESH)

        first = [copy(a, 0, me, sibling) for a in range(n_arr)]
        for j, chip in enumerate(chips):
            first += [copy(a, 1 + j, me, (*chip, c)) for a in range(n_arr)]
        for cp in first:
            cp.start()
        passed = []
        for j, chip in enumerate(chips):
            for a in range(n_arr):
                copy(a, 1 + j, (*chip, c), me).wait_recv()
            for a in range(n_arr):
                cp = copy(a, 4 + j, (*chip, c), sibling)
                cp.start()
                passed.append(cp)
        for a in range(n_arr):
            copy(a, 0, sibling, me).wait_recv()
        for j, chip in enumerate(chips):
            for a in range(n_arr):
                copy(a, 4 + j, (*chip, 1 - c), me).wait_recv()
        for cp in first + passed:
            cp.wait_send()

    vm = pl.BlockSpec(memory_space=pltpu.VMEM)
    return pl.pallas_call(
        body, name="ag_weights",
        out_shape=(jax.ShapeDtypeStruct((N_DEV,) + w_in.shape, BF16),
                   jax.ShapeDtypeStruct((N_DEV,) + w_out.shape, BF16),
                   jax.ShapeDtypeStruct((N_DEV,) + cw_pad.shape, F32)),
        in_specs=[vm, vm, vm], out_specs=(vm, vm, vm),
        scratch_shapes=[pltpu.SemaphoreType.DMA((n_arr * 7,)), pltpu.SemaphoreType.DMA((n_arr * 7,))],
        compiler_params=pltpu.CompilerParams(vmem_limit_bytes=40 * MIB),
    )(w_in, w_out, cw_pad)


def _fwd_proj(x, norm_w, wg):
    S = x.shape[0]
    tm = 512

    def body(x_ref, nw_ref, wg_ref, u_ref, gb_ref, gc_ref, zc_ref, za_ref, *rest):
        q_refs, k_refs, v_refs, rq_refs, rk_refs = (rest[3 * n:3 * n + 3] for n in range(5))
        scr, rscr = rest[15:]
        xv = x_ref[...]
        r = lax.rsqrt(jnp.mean(xv * xv, axis=-1, keepdims=True) + EPS)
        h = (xv * r * nw_ref[...]).astype(BF16)
        lane = _lane((tm, LANES))
        lo = lane < HEAD_DIM
        plain = {0: u_ref, 1: gb_ref, 2: gc_ref, 3: zc_ref, 7: za_ref}
        normed = {4: (q_refs, rq_refs), 5: (k_refs, rk_refs)}

        def emit(src, refs, dtype):
            refs[0][...] = _scratch_value(src).astype(dtype)
            for dil, ref in zip(VIEW_DILATIONS, refs[1:]):
                _to_view(src, ref, dil, dtype)

        for j in range(8):
            acc = jnp.dot(h, wg_ref[j], preferred_element_type=F32)
            if j in plain:
                plain[j][...] = acc.astype(BF16)
                continue
            if j == 6:
                for p in range(N_PAIRS):
                    scr[p] = acc[:, p * LANES:(p + 1) * LANES]
                emit(scr, v_refs, BF16)
                continue
            out_refs, r_refs = normed[j]
            r_tile = jnp.zeros((tm, LANES), F32)
            for p in range(N_PAIRS):
                blk = acc[:, p * LANES:(p + 1) * LANES]
                s_lo, s_hi = _head_sums(blk * blk, lo)
                r_lo = lax.rsqrt(s_lo * (1.0 / HEAD_DIM) + EPS)
                r_hi = lax.rsqrt(s_hi * (1.0 / HEAD_DIM) + EPS)
                scr[p] = blk * jnp.where(lo, r_lo, r_hi)
                r_tile = jnp.where(lane == 2 * p, r_lo, r_tile)
                r_tile = jnp.where(lane == 2 * p + 1, r_hi, r_tile)
            rscr[0] = r_tile
            emit(scr, out_refs, BF16)
            emit(rscr, r_refs, F32)

    piece = jax.ShapeDtypeStruct((S, CONV_W), BF16)
    pspec = pl.BlockSpec((tm, CONV_W), lambda i: (i, 0))
    wide = [_view_shape(S, d, ATTN_W, BF16) for d in DILATIONS]
    wide_specs = [_view_spec(tm, d, ATTN_W) for d in DILATIONS]
    stat = [_view_shape(S, d, LANES, F32) for d in DILATIONS]
    stat_specs = [_view_spec(tm, d, LANES) for d in DILATIONS]
    outs = pl.pallas_call(
        body, name="fwd_proj", grid=(S // tm,),
        out_shape=[piece] * 5 + wide * 3 + stat * 2,
        in_specs=[pl.BlockSpec((tm, D_MODEL), lambda i: (i, 0)),
                  pl.BlockSpec((1, D_MODEL), lambda i: (0, 0)),
                  pl.BlockSpec(wg.shape, lambda i: (0, 0, 0), pipeline_mode=pl.Buffered(1))],
        out_specs=[pspec] * 5 + wide_specs * 3 + stat_specs * 2,
        scratch_shapes=[_tile_scratch(tm, ATTN_W), _tile_scratch(tm, LANES)],
        compiler_params=_params(("arbitrary",), 48),
    )(x, norm_w.reshape(1, D_MODEL), wg)
    u, gb, gc, zc, za = outs[:5]
    q_hat, k_hat, v, rq, rk = (outs[5 + 3 * n:8 + 3 * n] for n in range(5))
    return u, gb, gc, zc, za, q_hat, k_hat, v, rq, rk


def _bias_tables(rel_bias, bmat):
    def body(rb_ref, b_ref, out_ref):
        h = pl.program_id(1)
        b = b_ref[0]
        t = jnp.full((QB, KB), NEG, F32)
        for bk in range(N_BUCKETS):
            t = jnp.where(b == bk, rb_ref[bk, h], t)
        out_ref[0, 0] = t

    return pl.pallas_call(
        body, name="bias_tables", grid=(len(DILATIONS), N_HEADS),
        out_shape=jax.ShapeDtypeStruct((len(DILATIONS), N_HEADS, QB, KB), F32),
        in_specs=[pl.BlockSpec(memory_space=pltpu.SMEM),
                  pl.BlockSpec((1, QB, KB), lambda g, h: (g, 0, 0))],
        out_specs=pl.BlockSpec((1, 1, QB, KB), lambda g, h: (g, h, 0, 0)),
        compiler_params=_params(("arbitrary", "arbitrary"), 16),
    )(rel_bias, bmat)


def _bias_grad(ds_acc, bmat):
    def body(ds_ref, b_ref, out_ref):
        lane = _lane((8, LANES))
        vec = jnp.zeros((8, LANES), F32)
        for g in range(len(DILATIONS)):
            b = b_ref[g]
            ds = ds_ref[g, 0]
            for bk in range(N_BUCKETS):
                s = jnp.sum(jnp.where(b == bk, ds, 0.0), axis=-1, keepdims=True)
                s = jnp.sum(s, axis=0, keepdims=True)
                vec = vec + jnp.where(lane == bk, s, 0.0)
        out_ref[0] = vec

    return pl.pallas_call(
        body, name="bias_grad", grid=(N_HEADS,),
        out_shape=jax.ShapeDtypeStruct((N_HEADS, 8, LANES), F32),
        in_specs=[pl.BlockSpec((len(DILATIONS), 1, QB, KB), lambda h: (0, h, 0, 0)),
                  pl.BlockSpec((len(DILATIONS), QB, KB), lambda h: (0, 0, 0))],
        out_specs=pl.BlockSpec((1, 8, LANES), lambda h: (h, 0, 0)),
        compiler_params=_params(("arbitrary",), 16),
    )(ds_acc, bmat)


def _halo_specs(width, rows, T, L, cols_of):
    per = T // rows
    last = L // rows - 1
    prev = pl.BlockSpec((rows, width), lambda r, i: (jnp.maximum(i * per - 1, 0), cols_of(r)))
    nxt = pl.BlockSpec((rows, width), lambda r, i: (jnp.minimum((i + 1) * per, last), cols_of(r)))
    return prev, nxt


def _attn_fwd(g, dil, qv, kv, vv, gain, tables):
    L = qv.shape[0]
    T = min(1024, L)
    n_sub = T // QB

    def body(q_ref, km_ref, kp_ref, kn_ref, vm_ref, vp_ref, vn_ref, gain_ref, bias_ref, o_ref, lse_ref, kwin, vwin):
        i = pl.program_id(1)
        kwin[0:HALF] = kp_ref[...]
        kwin[HALF:HALF + T] = km_ref[...]
        kwin[HALF + T:] = kn_ref[...]
        vwin[0:HALF] = vp_ref[...]
        vwin[HALF:HALF + T] = vm_ref[...]
        vwin[HALF + T:] = vn_ref[...]
        lane = _lane((QB, LANES))
        lo = lane < HEAD_DIM
        col = _lane((1, KB))

        def sub(j, carry):
            r0 = pl.multiple_of(j * QB, QB)
            kpos = i * T + j * QB - HALF + col
            kvalid = (kpos >= 0) & (kpos < L)
            lse_tile = jnp.zeros((QB, LANES), F32)
            for p in range(N_PAIRS):
                cs = slice(p * LANES, (p + 1) * LANES)
                qf = q_ref[pl.ds(r0, QB), cs].astype(F32) * gain_ref[:, cs]
                kw = kwin[pl.ds(r0, KB), cs]
                vw = vwin[pl.ds(r0, KB), cs]
                outs = []
                for e in range(2):
                    h = 2 * p + e
                    qm = jnp.where(lo if e == 0 else ~lo, qf, 0.0).astype(BF16)
                    s = lax.dot_general(qm, kw, (((1,), (1,)), ((), ())), preferred_element_type=F32)
                    logits = jnp.where(kvalid, s + bias_ref[0, h], NEG)
                    m = jnp.max(logits, axis=-1, keepdims=True)
                    pe = jnp.exp(logits - m)
                    l = jnp.sum(pe, axis=-1, keepdims=True)
                    pv = jnp.dot(pe.astype(BF16), vw, preferred_element_type=F32)
                    outs.append(pv * (1.0 / l))
                    lse_tile = jnp.where(lane == h, m + jnp.log(l), lse_tile)
                o_ref[pl.ds(r0, QB), cs] = jnp.where(lo, outs[0], outs[1]).astype(BF16)
            lse_ref[pl.ds(r0, QB), :] = lse_tile
            return carry

        lax.fori_loop(0, n_sub, sub, 0)

    main = pl.BlockSpec((T, ATTN_W), lambda r, i: (i, r))
    prev, nxt = _halo_specs(ATTN_W, HALF, T, L, lambda r: r)
    o_g, lse_g = pl.pallas_call(
        body, name=f"attn_fwd_d{dil}", grid=(dil, L // T),
        out_shape=(jax.ShapeDtypeStruct((L, dil * ATTN_W), BF16), jax.ShapeDtypeStruct((L, dil * LANES), F32)),
        in_specs=[main, main, prev, nxt, main, prev, nxt,
                  pl.BlockSpec((1, ATTN_W), lambda r, i: (0, 0)),
                  pl.BlockSpec((1, N_HEADS, QB, KB), lambda r, i: (g, 0, 0, 0))],
        out_specs=(main, pl.BlockSpec((T, LANES), lambda r, i: (i, r))),
        scratch_shapes=[pltpu.VMEM((T + 2 * HALF, ATTN_W), BF16), pltpu.VMEM((T + 2 * HALF, ATTN_W), BF16)],
        compiler_params=_params(("arbitrary", "arbitrary"), 40),
    )(qv, kv, kv, kv, vv, vv, vv, gain, tables)
    return o_g, lse_g


def _row_halo_specs(width, tm, S, col=0):
    rows = 16
    per = tm // rows
    last = S // rows - 1
    prev = pl.BlockSpec((rows, width), lambda i: (jnp.maximum(i * per - 1, 0), col))
    nxt = pl.BlockSpec((rows, width), lambda i: (jnp.minimum((i + 1) * per, last), col))
    return prev, nxt


def _shift_rows(a, prev_row, next_row, tm):
    row = lax.broadcasted_iota(jnp.int32, a.shape, 0)
    a_m1 = jnp.where(row == 0, prev_row, pltpu.roll(a, 1, 0))
    a_p1 = jnp.where(row == tm - 1, next_row, pltpu.roll(a, tm - 1, 0))
    return a_m1, a_p1


def _combine_gates(o_parts, lse_parts, u, gb, gc, zc, za, conv_w, conv_b):
    S = u.shape[0]
    tm = 512
    n_t = S // tm

    def body(o1, o2, o3, l1, l2, l3, u_ref, up_ref, un_ref, gb_ref, gc_ref, gcp_ref, gcn_ref, zc_ref, za_ref,
             cw_ref, cb_ref, y_ref, o_ref, lse_ref, lse2_ref, lse3_ref, so2, so3, sl2, sl3, slse):
        i = pl.program_id(0)
        lane = _lane((tm, LANES))
        lo = lane < HEAD_DIM
        for blk, scr, dil in ((o2, so2, DILATIONS[1]), (o3, so3, DILATIONS[2]),
                              (l2, sl2, DILATIONS[1]), (l3, sl3, DILATIONS[2])):
            _from_view(blk, scr, dil)
        ls = [l1[...], sl2[0], sl3[0]]
        lmax = jnp.maximum(jnp.maximum(ls[0], ls[1]), ls[2])
        es = [jnp.exp(l - lmax) for l in ls]
        den = es[0] + es[1] + es[2]
        slse[0] = lmax + jnp.log(den)
        lse_ref[...] = slse[0]
        _to_view(slse, lse2_ref, DILATIONS[1], F32)
        _to_view(slse, lse3_ref, DILATIONS[2], F32)
        inv = 1.0 / den
        ws = [e * inv for e in es]
        for p in range(N_PAIRS):
            cs = slice(p * LANES, (p + 1) * LANES)
            acc = jnp.zeros((tm, LANES), F32)
            for w, o_g in zip(ws, (o1[:, cs].astype(F32), so2[p], so3[p])):
                wp = jnp.where(lo, w[:, 2 * p:2 * p + 1], w[:, 2 * p + 1:2 * p + 2])
                acc = acc + wp * o_g
            o_ref[:, cs] = acc.astype(BF16)
            za = za_ref[:, cs].astype(F32)
            y_ref[:, CONV_W + p * LANES:CONV_W + (p + 1) * LANES] = (acc * za * _sigmoid(za)).astype(BF16)
        a = gc_ref[...].astype(F32) * u_ref[...].astype(F32)
        a_prev = gcp_ref[15:16, :].astype(F32) * up_ref[15:16, :].astype(F32) * (i > 0).astype(F32)
        a_next = gcn_ref[0:1, :].astype(F32) * un_ref[0:1, :].astype(F32) * (i < n_t - 1).astype(F32)
        a_m1, a_p1 = _shift_rows(a, a_prev, a_next, tm)
        z3 = cw_ref[0:1, :] * a_m1 + cw_ref[1:2, :] * a + cw_ref[2:3, :] * a_p1
        zc = zc_ref[...].astype(F32)
        y_ref[:, 0:CONV_W] = (gb_ref[...].astype(F32) * (z3 + cb_ref[...]) * zc * _sigmoid(zc)).astype(BF16)

    pspec = pl.BlockSpec((tm, CONV_W), lambda i: (i, 0))
    hp, hn = _row_halo_specs(CONV_W, tm, S)
    wide_specs = [_view_spec(tm, d, ATTN_W) for d in DILATIONS]
    stat_specs = [_view_spec(tm, d, LANES) for d in DILATIONS]
    outs = pl.pallas_call(
        body, name="combine_gates", grid=(n_t,),
        out_shape=[jax.ShapeDtypeStruct((S, D_MODEL), BF16), jax.ShapeDtypeStruct((S, ATTN_W), BF16)]
        + [_view_shape(S, d, LANES, F32) for d in DILATIONS],
        in_specs=wide_specs + stat_specs + [pspec, hp, hn, pspec, pspec, hp, hn, pspec, pspec,
                                            pl.BlockSpec((3, CONV_W), lambda i: (0, 0)),
                                            pl.BlockSpec((1, CONV_W), lambda i: (0, 0))],
        out_specs=[pl.BlockSpec((tm, D_MODEL), lambda i: (i, 0)), pspec] + stat_specs,
        scratch_shapes=[_tile_scratch(tm, ATTN_W), _tile_scratch(tm, ATTN_W),
                        _tile_scratch(tm, LANES), _tile_scratch(tm, LANES), _tile_scratch(tm, LANES)],
        compiler_params=_params(("arbitrary",), 40),
    )(*o_parts, *lse_parts, u, u, u, gb, gc, gc, gc, zc, za, conv_w, conv_b.reshape(1, CONV_W))
    return outs[0], outs[1], outs[2:]


def _out_proj(y, x, target, wo):
    S = x.shape[0]
    tm = 512

    def body(y_ref, x_ref, t_ref, wo_ref, g_ref, dy_ref, dwo_ref, loss_ref):
        i = pl.program_id(0)

        @pl.when(i == 0)
        def _():
            dwo_ref[...] = jnp.zeros_like(dwo_ref)
            loss_ref[...] = jnp.zeros_like(loss_ref)

        yv = y_ref[...]
        wo_v = wo_ref[...]
        err = x_ref[...] + jnp.dot(yv, wo_v, preferred_element_type=F32) - t_ref[...]
        e2 = (err * err).reshape(tm // 8, 8, D_MODEL).sum(axis=0)
        part = e2[:, 0:LANES]
        for k in range(1, D_MODEL // LANES):
            part = part + e2[:, k * LANES:(k + 1) * LANES]
        loss_ref[...] += part
        gv = (err * (1.0 / D_MODEL)).astype(BF16)
        g_ref[...] = gv
        dy_ref[...] = lax.dot_general(gv, wo_v, (((1,), (1,)), ((), ())), preferred_element_type=F32).astype(BF16)
        dwo_ref[...] += lax.dot_general(yv, gv, (((0,), (0,)), ((), ())), preferred_element_type=F32)

    tile = pl.BlockSpec((tm, D_MODEL), lambda i: (i, 0))
    return pl.pallas_call(
        body, name="out_proj", grid=(S // tm,),
        out_shape=(jax.ShapeDtypeStruct((S, D_MODEL), BF16), jax.ShapeDtypeStruct((S, D_MODEL), BF16),
                   jax.ShapeDtypeStruct((D_MODEL, D_MODEL), F32), jax.ShapeDtypeStruct((8, LANES), F32)),
        in_specs=[tile, tile, tile, pl.BlockSpec((D_MODEL, D_MODEL), lambda i: (0, 0), pipeline_mode=pl.Buffered(1))],
        out_specs=(tile, tile, pl.BlockSpec((D_MODEL, D_MODEL), lambda i: (0, 0)),
                   pl.BlockSpec((8, LANES), lambda i: (0, 0))),
        compiler_params=_params(("arbitrary",), 48),
    )(y, x, target, wo)


def _gates_bwd(dy, u, gb, gc, zc, za, o, conv_w, conv_b):
    S = u.shape[0]
    tm = 512
    n_t = S // tm

    def body(dy_ref, dyp_ref, dyn_ref, u_ref, up_ref, un_ref, gb_ref, gbp_ref, gbn_ref, gc_ref, gcp_ref, gcn_ref,
             zc_ref, zcp_ref, zcn_ref, za_ref, o_ref, cw_ref, cb_ref,
             du_ref, dgb_ref, dgc_ref, dzc_ref, dza_ref, small_ref,
             do_ref, do2_ref, do3_ref, delta_ref, delta2_ref, delta3_ref, sdo, sdelta):
        i = pl.program_id(0)

        @pl.when(i == 0)
        def _():
            small_ref[...] = jnp.zeros_like(small_ref)

        def dz3_of(dyc, zc, gb):
            return dyc * zc * _sigmoid(zc) * gb

        f = lambda ref, rows=slice(None): ref[rows, :].astype(F32)
        has_prev = (i > 0).astype(F32)
        has_next = (i < n_t - 1).astype(F32)
        pr, nr = slice(15, 16), slice(0, 1)
        uv, gbv, gcv, zcv = f(u_ref), f(gb_ref), f(gc_ref), f(zc_ref)
        dyc = dy_ref[:, 0:CONV_W].astype(F32)
        a = gcv * uv
        a_m1, a_p1 = _shift_rows(a, f(gcp_ref, pr) * f(up_ref, pr) * has_prev,
                                 f(gcn_ref, nr) * f(un_ref, nr) * has_next, tm)
        w0, w1, w2 = cw_ref[0:1, :], cw_ref[1:2, :], cw_ref[2:3, :]
        cvp = w0 * a_m1 + w1 * a + w2 * a_p1 + cb_ref[...]
        sz = _sigmoid(zcv)
        dcv = dyc * zcv * sz
        dzc_ref[...] = (dyc * gbv * cvp * (sz * (1.0 + zcv * (1.0 - sz)))).astype(BF16)
        dgb_ref[...] = (dcv * cvp).astype(BF16)
        dz3 = dcv * gbv
        dz3_prev = dz3_of(f(dyp_ref, pr), f(zcp_ref, pr), f(gbp_ref, pr)) * has_prev
        dz3_next = dz3_of(f(dyn_ref, nr), f(zcn_ref, nr), f(gbn_ref, nr)) * has_next
        dz3_m1, dz3_p1 = _shift_rows(dz3, dz3_prev, dz3_next, tm)
        da = w0 * dz3_p1 + w1 * dz3 + w2 * dz3_m1
        du_ref[...] = (da * gcv).astype(BF16)
        dgc_ref[...] = (da * uv).astype(BF16)
        csum = lambda t: jnp.sum(t, axis=0, keepdims=True)
        small_ref[0:1, :] += csum(dz3 * a_m1)
        small_ref[1:2, :] += csum(dz3 * a)
        small_ref[2:3, :] += csum(dz3 * a_p1)
        small_ref[3:4, :] += csum(dz3)

        lane = _lane((tm, LANES))
        lo = lane < HEAD_DIM
        delta = jnp.zeros((tm, LANES), F32)
        for p in range(N_PAIRS):
            cs = slice(p * LANES, (p + 1) * LANES)
            dya = dy_ref[:, CONV_W + p * LANES:CONV_W + (p + 1) * LANES].astype(F32)
            zav = za_ref[:, cs].astype(F32)
            ov = o_ref[:, cs].astype(F32)
            sa = _sigmoid(zav)
            d_o = dya * zav * sa
            sdo[p] = d_o
            do_ref[:, cs] = d_o.astype(BF16)
            dza_ref[:, cs] = (dya * ov * (sa * (1.0 + zav * (1.0 - sa)))).astype(BF16)
            s_lo, s_hi = _head_sums(d_o * ov, lo)
            delta = jnp.where(lane == 2 * p, s_lo, delta)
            delta = jnp.where(lane == 2 * p + 1, s_hi, delta)
        sdelta[0] = delta
        delta_ref[...] = delta
        for dil, wide_ref, stat_ref in zip(VIEW_DILATIONS, (do2_ref, do3_ref), (delta2_ref, delta3_ref)):
            _to_view(sdo, wide_ref, dil, BF16)
            _to_view(sdelta, stat_ref, dil, F32)

    pspec = pl.BlockSpec((tm, CONV_W), lambda i: (i, 0))
    hp, hn = _row_halo_specs(CONV_W, tm, S)
    piece = jax.ShapeDtypeStruct((S, CONV_W), BF16)
    outs = pl.pallas_call(
        body, name="gates_bwd", grid=(n_t,),
        out_shape=[piece] * 5 + [jax.ShapeDtypeStruct((8, CONV_W), F32)]
        + [_view_shape(S, d, ATTN_W, BF16) for d in DILATIONS] + [_view_shape(S, d, LANES, F32) for d in DILATIONS],
        in_specs=[pl.BlockSpec((tm, D_MODEL), lambda i: (i, 0)), hp, hn,
                  pspec, hp, hn, pspec, hp, hn, pspec, hp, hn, pspec, hp, hn, pspec, pspec,
                  pl.BlockSpec((3, CONV_W), lambda i: (0, 0)), pl.BlockSpec((1, CONV_W), lambda i: (0, 0))],
        out_specs=[pspec] * 5 + [pl.BlockSpec((8, CONV_W), lambda i: (0, 0))]
        + [_view_spec(tm, d, ATTN_W) for d in DILATIONS] + [_view_spec(tm, d, LANES) for d in DILATIONS],
        scratch_shapes=[_tile_scratch(tm, ATTN_W), _tile_scratch(tm, LANES)],
        compiler_params=_params(("arbitrary",), 40),
    )(dy, dy, dy, u, u, u, gb, gb, gb, gc, gc, gc, zc, zc, zc, za, o, conv_w, conv_b.reshape(1, CONV_W))
    return outs[:5], outs[5], outs[6:9], outs[9:12]


def _attn_bwd(g, dil, qv, kv, vv, dov, lv, dv_, rqv, rkv, gain, tables):
    L = qv.shape[0]
    T = min(1024, L)
    n_sub = T // QB + 1
    QW = T + 2 * HALF
    KW = T + 2 * QB

    def body(qm_ref, qp_ref, qn_ref, dom_ref, dop_ref, don_ref, km_ref, kp_ref, kn_ref, vm_ref, vp_ref, vn_ref,
             lm_ref, lp_ref, ln_ref, dm_ref, dp_ref, dn_ref, rqm_ref, rqp_ref, rqn_ref, rk_ref,
             gain_ref, bias_ref,
             dq_ref, dk_ref, dv_ref, ds_ref, a_ref,
             qwin, dowin, kwin, vwin, lwin, dwin, rqwin, dqwin, dkacc, dvacc):
        r = pl.program_id(0)
        i = pl.program_id(1)

        @pl.when((r == 0) & (i == 0))
        def _():
            ds_ref[...] = jnp.zeros_like(ds_ref)
            a_ref[...] = jnp.zeros_like(a_ref)

        for win, (pr, mn, nx), h_rows in ((qwin, (qp_ref, qm_ref, qn_ref), HALF),
                                          (dowin, (dop_ref, dom_ref, don_ref), HALF),
                                          (lwin, (lp_ref, lm_ref, ln_ref), HALF),
                                          (dwin, (dp_ref, dm_ref, dn_ref), HALF),
                                          (rqwin, (rqp_ref, rqm_ref, rqn_ref), HALF),
                                          (kwin, (kp_ref, km_ref, kn_ref), QB),
                                          (vwin, (vp_ref, vm_ref, vn_ref), QB)):
            win[0:h_rows] = pr[...]
            win[h_rows:h_rows + T] = mn[...]
            win[h_rows + T:] = nx[...]
        dkacc[...] = jnp.zeros_like(dkacc)
        dvacc[...] = jnp.zeros_like(dvacc)

        lane = _lane((QB, LANES))
        lo = lane < HEAD_DIM
        col = _lane((1, KB))
        row = lax.broadcasted_iota(jnp.int32, (QB, 1), 0)

        def sub(j, carry):
            r0 = pl.multiple_of(j * QB, QB)
            kpos = i * T - QB + j * QB + col
            qpos = i * T - HALF + j * QB + row
            valid = (kpos >= 0) & (kpos < L) & (qpos >= 0) & (qpos < L)
            owned = ((qpos >= i * T) & (qpos < i * T + T)).astype(F32)
            lse_t = lwin[pl.ds(r0, QB), :]
            delta_t = dwin[pl.ds(r0, QB), :]
            rq_t = rqwin[pl.ds(r0, QB), :]
            for p in range(N_PAIRS):
                cs = slice(p * LANES, (p + 1) * LANES)
                g2 = gain_ref[:, cs]
                q_f = qwin[pl.ds(r0, QB), cs].astype(F32)
                qf = q_f * g2
                do2 = dowin[pl.ds(r0, QB), cs]
                kw = kwin[pl.ds(r0, KB), cs]
                vw = vwin[pl.ds(r0, KB), cs]
                dqm = []
                dk_c = jnp.zeros((KB, LANES), F32)
                dv_c = jnp.zeros((KB, LANES), F32)
                for e in range(2):
                    h = 2 * p + e
                    mask = lo if e == 0 else ~lo
                    qm = jnp.where(mask, qf, 0.0).astype(BF16)
                    do_e = jnp.where(mask, do2, jnp.zeros_like(do2))
                    s = lax.dot_general(qm, kw, (((1,), (1,)), ((), ())), preferred_element_type=F32)
                    pe = jnp.where(valid, jnp.exp(s + bias_ref[0, h] - lse_t[:, h:h + 1]), 0.0)
                    dp = lax.dot_general(do_e, vw, (((1,), (1,)), ((), ())), preferred_element_type=F32)
                    ds = pe * (dp - delta_t[:, h:h + 1])
                    ds_ref[h] += ds * owned
                    pb = pe.astype(BF16)
                    dsb = ds.astype(BF16)
                    dv_c = dv_c + lax.dot_general(pb, do_e, (((0,), (0,)), ((), ())), preferred_element_type=F32)
                    dk_c = dk_c + lax.dot_general(dsb, qm, (((0,), (0,)), ((), ())), preferred_element_type=F32)
                    dqm.append(jnp.dot(dsb, kw, preferred_element_type=F32))
                dkacc[pl.ds(r0, KB), cs] += dk_c
                dvacc[pl.ds(r0, KB), cs] += dv_c
                dqm2 = jnp.where(lo, dqm[0], dqm[1])
                a_ref[p] += (dqm2 * q_f * owned).reshape(QB // 8, 8, LANES).sum(axis=0)
                dqh = dqm2 * g2
                c_lo, c_hi = _head_sums(dqh * q_f, lo)
                mean_c = jnp.where(lo, c_lo, c_hi) * (1.0 / HEAD_DIM)
                rq2 = jnp.where(lo, rq_t[:, 2 * p:2 * p + 1], rq_t[:, 2 * p + 1:2 * p + 2])
                dqwin[pl.ds(r0, QB), cs] = (rq2 * (dqh - q_f * mean_c)).astype(BF16)
            return carry

        lax.fori_loop(0, n_sub, sub, 0)
        dq_ref[...] = dqwin[HALF:HALF + T]

        def finish(j, carry):
            r0 = pl.multiple_of(j * QB, QB)
            rk_t = rk_ref[pl.ds(r0, QB), :]
            for p in range(N_PAIRS):
                cs = slice(p * LANES, (p + 1) * LANES)
                dkh = dkacc[pl.ds(QB + r0, QB), cs]
                k_f = km_ref[pl.ds(r0, QB), cs].astype(F32)
                c_lo, c_hi = _head_sums(dkh * k_f, lo)
                mean_c = jnp.where(lo, c_lo, c_hi) * (1.0 / HEAD_DIM)
                rk2 = jnp.where(lo, rk_t[:, 2 * p:2 * p + 1], rk_t[:, 2 * p + 1:2 * p + 2])
                dk_ref[pl.ds(r0, QB), cs] = (rk2 * (dkh - k_f * mean_c)).astype(BF16)
                dv_ref[pl.ds(r0, QB), cs] = dvacc[pl.ds(QB + r0, QB), cs].astype(BF16)
            return carry

        lax.fori_loop(0, T // QB, finish, 0)

    main = pl.BlockSpec((T, ATTN_W), lambda r, i: (i, r))
    smain = pl.BlockSpec((T, LANES), lambda r, i: (i, r))
    q_prev, q_next = _halo_specs(ATTN_W, HALF, T, L, lambda r: r)
    k_prev, k_next = _halo_specs(ATTN_W, QB, T, L, lambda r: r)
    s_prev, s_next = _halo_specs(LANES, HALF, T, L, lambda r: r)
    piece = jax.ShapeDtypeStruct((L, dil * ATTN_W), BF16)
    return pl.pallas_call(
        body, name=f"attn_bwd_d{dil}", grid=(dil, L // T),
        out_shape=(piece, piece, piece, jax.ShapeDtypeStruct((N_HEADS, QB, KB), F32),
                   jax.ShapeDtypeStruct((N_PAIRS, 8, LANES), F32)),
        in_specs=[main, q_prev, q_next, main, q_prev, q_next, main, k_prev, k_next, main, k_prev, k_next,
                  smain, s_prev, s_next, smain, s_prev, s_next, smain, s_prev, s_next, smain,
                  pl.BlockSpec((1, ATTN_W), lambda r, i: (0, 0)),
                  pl.BlockSpec((1, N_HEADS, QB, KB), lambda r, i: (g, 0, 0, 0))],
        out_specs=(main, main, main, pl.BlockSpec((N_HEADS, QB, KB), lambda r, i: (0, 0, 0)),
                   pl.BlockSpec((N_PAIRS, 8, LANES), lambda r, i: (0, 0, 0))),
        scratch_shapes=[pltpu.VMEM((QW, ATTN_W), BF16), pltpu.VMEM((QW, ATTN_W), BF16),
                        pltpu.VMEM((KW, ATTN_W), BF16), pltpu.VMEM((KW, ATTN_W), BF16),
                        pltpu.VMEM((QW, LANES), F32), pltpu.VMEM((QW, LANES), F32), pltpu.VMEM((QW, LANES), F32),
                        pltpu.VMEM((QW, ATTN_W), BF16),
                        pltpu.VMEM((KW, ATTN_W), F32), pltpu.VMEM((KW, ATTN_W), F32)],
        compiler_params=_params(("arbitrary", "arbitrary"), 56),
    )(qv, qv, qv, dov, dov, dov, kv, kv, kv, vv, vv, vv, lv, lv, lv, dv_, dv_, dv_, rqv, rqv, rqv, rkv, gain, tables)


def _proj_bwd(x, d_out, norm_w, wg, pieces):
    S = x.shape[0]
    tm = 256
    flat = [share for p in pieces for share in p]
    n_t = S // tm

    def body(*refs):
        x_ref, g_ref, nw_ref, wg_ref = refs[:4]
        piece_refs = refs[4:4 + len(flat)]
        gx_ref, dwin_hbm, dnw_ref, acc, sview, sem = refs[4 + len(flat):]
        i = pl.program_id(0)

        @pl.when(i == 0)
        def _():
            acc[...] = jnp.zeros_like(acc)
            dnw_ref[...] = jnp.zeros_like(dnw_ref)

        xv = x_ref[...]
        r = lax.rsqrt(jnp.mean(xv * xv, axis=-1, keepdims=True) + EPS)
        xh = xv * r
        nw = nw_ref[...]
        h = (xh * nw).astype(BF16)
        dh = jnp.zeros((tm, D_MODEL), F32)
        k = 0
        for j in range(8):
            shares = pieces[j]
            if len(shares) == 1:
                dp = piece_refs[k][...]
            else:
                dp = jnp.zeros((tm, CONV_W), F32)
                for t, (_, dil) in enumerate(shares):
                    if dil == 1:
                        dp = dp + piece_refs[k + t][...].astype(F32)
                    else:
                        _from_view(piece_refs[k + t], sview, dil)
                        dp = dp + _scratch_value(sview)
                dp = dp.astype(BF16)
            k += len(shares)
            dh = dh + lax.dot_general(dp, wg_ref[j], (((1,), (1,)), ((), ())), preferred_element_type=F32)
            acc[j] += lax.dot_general(h, dp, (((0,), (0,)), ((), ())), preferred_element_type=F32)
        dnw_ref[...] += (dh * xh).reshape(tm // 8, 8, D_MODEL).sum(axis=0)
        dxh = dh * nw
        mean_c = jnp.mean(dxh * xh, axis=-1, keepdims=True)
        gx_ref[...] = g_ref[...].astype(F32) + r * (dxh - xh * mean_c)

        @pl.when(i == n_t - 1)
        def _():
            cp = pltpu.make_async_copy(acc, dwin_hbm, sem)
            cp.start()
            cp.wait()

    tile = pl.BlockSpec((tm, D_MODEL), lambda i: (i, 0))
    return pl.pallas_call(
        body, name="proj_bwd", grid=(n_t,),
        out_shape=(jax.ShapeDtypeStruct((S, D_MODEL), F32), jax.ShapeDtypeStruct(wg.shape, F32),
                   jax.ShapeDtypeStruct((8, D_MODEL), F32)),
        in_specs=[tile, tile, pl.BlockSpec((1, D_MODEL), lambda i: (0, 0)),
                  pl.BlockSpec(wg.shape, lambda i: (0, 0, 0), pipeline_mode=pl.Buffered(1))]
        + [_view_spec(tm, dil, CONV_W) for _, dil in flat],
        out_specs=(tile, pl.BlockSpec(memory_space=pl.ANY), pl.BlockSpec((8, D_MODEL), lambda i: (0, 0))),
        scratch_shapes=[pltpu.VMEM(wg.shape, F32), _tile_scratch(tm, CONV_W), pltpu.SemaphoreType.DMA],
        compiler_params=_params(("arbitrary",), 56),
    )(x, d_out, norm_w.reshape(1, D_MODEL), wg, *[a for a, _ in flat])


def _exchange_grads(gw_in, gw_out, small):
    n_arr = 3

    def body(gin_ref, gout_ref, sm_ref, rin_ref, rout_ref, rsm_ref, send_sems, recv_sems, local_sems):
        x, y, c = lax.axis_index("x"), lax.axis_index("y"), lax.axis_index("c")
        me = 4 * x + 2 * y + c
        locals_ = [pltpu.make_async_copy(gin_ref.at[me], rin_ref.at[me], local_sems.at[0]),
                   pltpu.make_async_copy(gout_ref.at[me], rout_ref.at[me], local_sems.at[1])]
        for cp in locals_:
            cp.start()
        rsm_ref[me] = sm_ref[...]
        copies = []
        for k in range(1, N_DEV):
            px = 1 - x if k & 4 else x
            py = 1 - y if k & 2 else y
            pc = 1 - c if k & 1 else c
            peer = 4 * px + 2 * py + pc
            srcs = (gin_ref.at[peer], gout_ref.at[peer], sm_ref)
            dsts = (rin_ref.at[me], rout_ref.at[me], rsm_ref.at[me])
            for a in range(n_arr):
                copies.append(pltpu.make_async_remote_copy(
                    src_ref=srcs[a], dst_ref=dsts[a], send_sem=send_sems.at[a * 7 + k - 1],
                    recv_sem=recv_sems.at[a * 7 + k - 1], device_id=(px, py, pc), device_id_type=MESH))
        for cp in copies:
            cp.start()
        for cp in copies:
            cp.wait_recv()
        for cp in copies:
            cp.wait_send()
        for cp in locals_:
            cp.wait()

    hbm = pl.BlockSpec(memory_space=pl.ANY)
    vm = pl.BlockSpec(memory_space=pltpu.VMEM)
    return pl.pallas_call(
        body, name="exchange_grads",
        out_shape=(jax.ShapeDtypeStruct(gw_in.shape, F32), jax.ShapeDtypeStruct(gw_out.shape, F32),
                   jax.ShapeDtypeStruct((N_DEV,) + small.shape, F32)),
        in_specs=[hbm, hbm, vm], out_specs=(hbm, hbm, vm),
        scratch_shapes=[pltpu.SemaphoreType.DMA((n_arr * 7,)), pltpu.SemaphoreType.DMA((n_arr * 7,)),
                        pltpu.SemaphoreType.DMA((2,))],
        compiler_params=pltpu.CompilerParams(vmem_limit_bytes=16 * MIB),
    )(gw_in, gw_out, small)


def _adamw_math(w, g, m, v):
    m2 = ADAM_B1 * m + (1.0 - ADAM_B1) * g
    v2 = ADAM_B2 * v + (1.0 - ADAM_B2) * (g * g)
    m_hat = m2 / (1.0 - ADAM_B1 ** ADAM_STEP)
    v_hat = v2 / (1.0 - ADAM_B2 ** ADAM_STEP)
    delta = -ADAM_LR * (m_hat / (jnp.sqrt(v_hat) + ADAM_EPS) + ADAM_WD * w)
    return delta, m2, v2


def _adamw_sharded(name, parts, w, m, v, rows):
    R, C = w.shape

    def body(p_ref, w_ref, m_ref, v_ref, g_ref, d_ref, m2_ref, v2_ref):
        g = p_ref[0]
        for s in range(1, N_DEV):
            g = g + p_ref[s]
        g_ref[...] = g
        d_ref[...], m2_ref[...], v2_ref[...] = _adamw_math(w_ref[...], g, m_ref[...], v_ref[...])

    spec = pl.BlockSpec((rows, C), lambda i: (i, 0))
    out = jax.ShapeDtypeStruct((R, C), F32)
    return pl.pallas_call(
        body, name=name, grid=(R // rows,),
        out_shape=(out,) * 4,
        in_specs=[pl.BlockSpec((N_DEV, rows, C), lambda i: (0, i, 0)), spec, spec, spec],
        out_specs=(spec,) * 4,
        compiler_params=_params(("arbitrary",), 40),
    )(parts, w, m, v)


def _adamw_small(parts, w, m, v):
    R = w.shape[0]

    def body(p_ref, w_ref, m_ref, v_ref, g_ref, d_ref, m2_ref, v2_ref):
        g = p_ref[0]
        for s in range(1, N_DEV):
            g = g + p_ref[s]
        g_ref[...] = g
        d_ref[...], m2_ref[...], v2_ref[...] = _adamw_math(w_ref[...], g, m_ref[...], v_ref[...])

    vm = pl.BlockSpec(memory_space=pltpu.VMEM)
    out = jax.ShapeDtypeStruct((R, LANES), F32)
    return pl.pallas_call(
        body, name="adamw_small", out_shape=(out,) * 4, in_specs=[vm] * 4, out_specs=(vm,) * 4,
    )(parts, w, m, v)


def _qk_gain_grads(a_parts, q_norm_w, k_norm_w, bias_g):
    def body(a_ref, qw_ref, kw_ref, bg_ref, out_ref):
        tot = jnp.zeros((8, LANES), F32)
        for g in range(len(DILATIONS)):
            for p in range(N_PAIRS):
                tot = tot + a_ref[g, p]
        col = jnp.sum(tot, axis=0, keepdims=True)
        a64 = col[:, 0:HEAD_DIM] + col[:, HEAD_DIM:LANES]
        out_ref[...] = jnp.zeros_like(out_ref)
        out_ref[0:1, 0:HEAD_DIM] = LOGIT_SCALE * kw_ref[...] * a64
        out_ref[1:2, 0:HEAD_DIM] = LOGIT_SCALE * qw_ref[...] * a64
        for h in range(N_HEADS):
            r0, c0 = 2 + h // 4, (h % 4) * N_BUCKETS
            out_ref[r0:r0 + 1, c0:c0 + N_BUCKETS] = bg_ref[h, 0:1, 0:N_BUCKETS]

    vm = pl.BlockSpec(memory_space=pltpu.VMEM)
    return pl.pallas_call(
        body, name="qk_gain_grads", out_shape=jax.ShapeDtypeStruct((8, LANES), F32),
        in_specs=[vm] * 4, out_specs=vm,
    )(a_parts, q_norm_w.reshape(1, HEAD_DIM), k_norm_w.reshape(1, HEAD_DIM), bias_g)


SMALL_ROWS = 32


def _pack_small(norm_w, conv_b, q_norm_w, k_norm_w, rel_bias, conv_w_full):
    pad64 = lambda a: jnp.pad(a, (0, LANES - HEAD_DIM)).reshape(1, LANES)
    return jnp.concatenate([
        norm_w.reshape(8, LANES), conv_b.reshape(4, LANES), pad64(q_norm_w), pad64(k_norm_w),
        rel_bias.T.reshape(2, LANES), conv_w_full.reshape(12, LANES), jnp.zeros((4, LANES), F32)], axis=0)


def _unpack_small(s):
    return (s[0:8].reshape(D_MODEL), s[8:12].reshape(CONV_W), s[12, 0:HEAD_DIM], s[13, 0:HEAD_DIM],
            s[14:16].reshape(N_HEADS, N_BUCKETS).T, s[16:28].reshape(3, CONV_W))


def kernel(x, norm_w, w_in, conv_w, conv_b, q_norm_w, k_norm_w, rel_bias, w_out, loss_target, m_norm_w, m_w_in, m_conv_w, m_conv_b, m_q_norm_w, m_k_norm_w, m_rel_bias, m_w_out, v_norm_w, v_w_in, v_conv_w, v_conv_b, v_q_norm_w, v_k_norm_w, v_rel_bias, v_w_out):
    S = x.shape[1]
    x2 = x.reshape(S, D_MODEL)
    tgt = loss_target.reshape(S, D_MODEL)
    me = 4 * lax.axis_index("x") + 2 * lax.axis_index("y") + lax.axis_index("c")
    shard_w = CONV_W // N_DEV

    cw_pad = jnp.pad(conv_w, ((0, 5), (0, LANES - shard_w)))
    wg, wo_g, cw_g = _ag_weights(w_in, w_out, cw_pad)
    wo = wo_g.reshape(D_MODEL, D_MODEL)
    conv_w_full = cw_g[:, 0:3, 0:shard_w].transpose(1, 0, 2).reshape(3, CONV_W)

    bmat = jnp.asarray(np.stack([_bucket_matrix(d) for d in DILATIONS]))
    tables = _bias_tables(rel_bias, bmat)
    gain = jnp.tile(q_norm_w * k_norm_w * LOGIT_SCALE, N_HEADS).reshape(1, ATTN_W)

    u, gb, gc, zc, za, q_hat, k_hat, v, rq, rk = _fwd_proj(x2, norm_w, wg)
    o_parts, lse_parts = [], []
    for g, dil in enumerate(DILATIONS):
        o_g, lse_g = _attn_fwd(g, dil, q_hat[g], k_hat[g], v[g], gain, tables)
        o_parts.append(o_g)
        lse_parts.append(lse_g)
    y, o, lse = _combine_gates(o_parts, lse_parts, u, gb, gc, zc, za, conv_w_full, conv_b)
    d_out, dy, gwo_part, loss_part = _out_proj(y, x2, tgt, wo)
    loss = lax.psum(jnp.sum(loss_part) * (0.5 / D_MODEL), ("x", "y", "c"))

    (du, dgb, dgc, dzc, dza), conv_small, d_o, delta = _gates_bwd(dy, u, gb, gc, zc, za, o, conv_w_full, conv_b)
    dqs, dks, dvs, ds_parts, a_parts = [], [], [], [], []
    for g, dil in enumerate(DILATIONS):
        dq, dk, dv, ds_sum, a_sum = _attn_bwd(g, dil, q_hat[g], k_hat[g], v[g], d_o[g], lse[g], delta[g], rq[g], rk[g],
                                              gain, tables)
        dqs.append((dq, dil))
        dks.append((dk, dil))
        dvs.append((dv, dil))
        ds_parts.append(ds_sum)
        a_parts.append(a_sum)
    grad_x, gwin_part, gnw_part = _proj_bwd(
        x2, d_out, norm_w, wg, [[(du, 1)], [(dgb, 1)], [(dgc, 1)], [(dzc, 1)], dqs, dks, dvs, [(dza, 1)]])
    bias_g = _bias_grad(jnp.stack(ds_parts), bmat)
    qk_small = _qk_gain_grads(jnp.stack(a_parts), q_norm_w, k_norm_w, bias_g)

    small_part = jnp.concatenate([
        gnw_part.sum(axis=0).reshape(8, LANES), conv_small[3].reshape(4, LANES), qk_small[0:4],
        conv_small[0:3].reshape(12, LANES), jnp.zeros((4, LANES), F32)], axis=0)
    r_in, r_out, r_small = _exchange_grads(gwin_part, gwo_part.reshape(N_DEV, D_MODEL // N_DEV, D_MODEL), small_part)

    g_win, d_win, m_win, v_win = _adamw_sharded("adamw_w_in", r_in, w_in, m_w_in, v_w_in, 128)
    g_wo, d_wo, m_wo, v_wo = _adamw_sharded("adamw_w_out", r_out, w_out, m_w_out, v_w_out, 32)

    def full_conv(a):
        return jnp.zeros((3, CONV_W), F32).at[:, 0:shard_w].set(a)

    packs = [_pack_small(nw_, cb_, qw_, kw_, rb_, full_conv(cw_)) for nw_, cb_, qw_, kw_, rb_, cw_ in (
        (norm_w, conv_b, q_norm_w, k_norm_w, rel_bias, conv_w),
        (m_norm_w, m_conv_b, m_q_norm_w, m_k_norm_w, m_rel_bias, m_conv_w),
        (v_norm_w, v_conv_b, v_q_norm_w, v_k_norm_w, v_rel_bias, v_conv_w))]
    r_small_mine = r_small.at[:, 16:28, :].set(
        jnp.pad(lax.dynamic_slice_in_dim(r_small[:, 16:28, :].reshape(N_DEV, 3, CONV_W), me * shard_w, shard_w, axis=2),
                ((0, 0), (0, 0), (0, CONV_W - shard_w))).reshape(N_DEV, 12, LANES))
    outs_small = _adamw_small(r_small_mine, *packs)
    g_s, d_s, m_s, v_s = [_unpack_small(t) for t in outs_small]

    def leaves(small, big_in, big_out):
        nw_, cb_, qw_, kw_, rb_, cwf = small
        return (nw_, big_in, cwf[:, 0:shard_w], cb_, qw_, kw_, rb_, big_out)

    return (loss, grad_x.reshape(x.shape),
            *leaves(g_s, g_win, g_wo), *leaves(d_s, d_win, d_wo), *leaves(m_s, m_win, m_wo), *leaves(v_s, v_win, v_wo))
```

```python
import functools
import math

import numpy as np
import jax
import jax.numpy as jnp
from jax import lax
from jax.experimental import pallas as pl
from jax.experimental.pallas import tpu as pltpu

F32 = jnp.float32
BF16 = jnp.bfloat16

N_DEV = 8
D_MODEL = 1024
CONV_W = 512
ATTN_W = 512
N_HEADS = 8
HEAD_DIM = 64
N_PAIRS = N_HEADS // 2
LANES = 128
HALF = 64
QB = 128
KB = QB + 2 * HALF
DILATIONS = (1, 4, 16)
N_BUCKETS = 32
MAX_DISTANCE = 1024
EPS = 1e-6
NEG = -1e30
LOGIT_SCALE = HEAD_DIM ** -0.5

ADAM_LR = 0.001
ADAM_B1 = 0.9
ADAM_B2 = 0.999
ADAM_EPS = 1e-08
ADAM_WD = 0.01
ADAM_STEP = 10

MESH = pl.DeviceIdType.MESH
MIB = 1024 * 1024


def _params(semantics, vmem_mib):
    return pltpu.CompilerParams(dimension_semantics=semantics, vmem_limit_bytes=vmem_mib * MIB)


def _lane(shape):
    return lax.broadcasted_iota(jnp.int32, shape, len(shape) - 1)


def _sigmoid(z):
    return 1.0 / (1.0 + jnp.exp(-z))


def _head_sums(prod, lo_mask):
    s_lo = jnp.sum(jnp.where(lo_mask, prod, 0.0), axis=-1, keepdims=True)
    s_hi = jnp.sum(jnp.where(lo_mask, 0.0, prod), axis=-1, keepdims=True)
    return s_lo, s_hi


VIEW_DILATIONS = DILATIONS[1:]


def _view_shape(S, dil, width, dtype):
    return jax.ShapeDtypeStruct((S // dil, dil * width), dtype)


def _view_spec(tm, dil, width):
    return pl.BlockSpec((tm // dil, dil * width), lambda i: (i, 0))


def _tile_scratch(tm, width):
    return pltpu.VMEM((width // LANES, tm, LANES), F32)


def _scratch_value(scr_ref):
    n = scr_ref.shape[0]
    return scr_ref[0] if n == 1 else jnp.concatenate([scr_ref[c] for c in range(n)], axis=1)


def _to_view(scr_ref, out_ref, dil, dtype):
    n, tm, _ = scr_ref.shape
    for r in range(dil):
        for c in range(n):
            col = (r * n + c) * LANES
            out_ref[:, col:col + LANES] = scr_ref[c, pl.ds(r, tm // dil, stride=dil), :].astype(dtype)


def _from_view(blk_ref, scr_ref, dil):
    n, tm, _ = scr_ref.shape
    for r in range(dil):
        for c in range(n):
            col = (r * n + c) * LANES
            scr_ref[c, pl.ds(r, tm // dil, stride=dil), :] = blk_ref[:, col:col + LANES].astype(F32)


def _bucket_matrix(dilation):
    rel = np.arange(KB)[None, :] - HALF - np.arange(QB)[:, None]
    band = np.abs(rel) <= HALF
    dist = np.clip(rel, -HALF, HALF) * dilation
    half_b = N_BUCKETS // 2
    max_exact = half_b // 2
    ret = np.where(dist > 0, half_b, 0)
    n = np.abs(dist)
    nf = np.maximum(n, 1).astype(np.float32)
    large = max_exact + (np.log(nf / np.float32(max_exact)) / np.float32(math.log(MAX_DISTANCE / max_exact))
                         * np.float32(half_b - max_exact)).astype(np.int32)
    large = np.minimum(large, half_b - 1)
    bucket = ret + np.where(n < max_exact, n, large)
    return np.where(band, bucket, -1).astype(np.int32)


def _ag_weights(w_in, w_out, cw_pad):
    n_arr = 3

    def body(win_ref, wout_ref, cw_ref, gin_ref, gout_ref, gcw_ref, send_sems, recv_sems):
        x, y, c = lax.axis_index("x"), lax.axis_index("y"), lax.axis_index("c")
        me = (x, y, c)
        sibling = (x, y, 1 - c)
        chips = [(1 - x, y), (x, 1 - y), (1 - x, 1 - y)]
        arrays = (gin_ref, gout_ref, gcw_ref)

        def slot(px, py, pc):
            return 4 * px + 2 * py + pc

        gin_ref[slot(*me)] = win_ref[...].astype(BF16)
        gout_ref[slot(*me)] = wout_ref[...].astype(BF16)
        gcw_ref[slot(*me)] = cw_ref[...]

        def copy(a, k, block, to):
            ref = arrays[a].at[slot(*block)]
            return pltpu.make_async_remote_copy(
                src_ref=ref, dst_ref=ref, send_sem=send_sems.at[a * 7 + k], recv_sem=recv_sems.at[a * 7 + k],
                device_id=to, device_id_type=MESH)

        first = [copy(a, 0, me, sibling) for a in range(n_arr)]
        for j, chip in enumerate(chips):
            first += [copy(a, 1 + j, me, (*chip, c)) for a in range(n_arr)]
        for cp in first:
            cp.start()
        passed = []
        for j, chip in enumerate(chips):
            for a in range(n_arr):
                copy(a, 1 + j, (*chip, c), me).wait_recv()
            for a in range(n_arr):
                cp = copy(a, 4 + j, (*chip, c), sibling)
                cp.start()
                passed.append(cp)
        for a in range(n_arr):
            copy(a, 0, sibling, me).wait_recv()
        for j, chip in enumerate(chips):
            for a in range(n_arr):
                copy(a, 4 + j, (*chip, 1 - c), me).wait_recv()
        for cp in first + passed:
            cp.wait_send()

    vm = pl.BlockSpec(memory_space=pltpu.VMEM)
    return pl.pallas_call(
        body, name="ag_weights",
        out_shape=(jax.ShapeDtypeStruct((N_DEV,) + w_in.shape, BF16),
                   jax.ShapeDtypeStruct((N_DEV,) + w_out.shape, BF16),
                   jax.ShapeDtypeStruct((N_DEV,) + cw_pad.shape, F32)),
        in_specs=[vm, vm, vm], out_specs=(vm, vm, vm),
        scratch_shapes=[pltpu.SemaphoreType.DMA((n_arr * 7,)), pltpu.SemaphoreType.DMA((n_arr * 7,))],
        compiler_params=pltpu.CompilerParams(vmem_limit_bytes=40 * MIB),
    )(w_in, w_out, cw_pad)


def _fwd_proj(x, norm_w, wg):
    S = x.shape[0]
    tm = 512

    def body(x_ref, nw_ref, wg_ref, u_ref, gb_ref, gc_ref, zc_ref, za_ref, *rest):
        q_refs, k_refs, v_refs, rq_refs, rk_refs = (rest[3 * n:3 * n + 3] for n in range(5))
        scr, rscr = rest[15:]
        xv = x_ref[...]
        r = lax.rsqrt(jnp.mean(xv * xv, axis=-1, keepdims=True) + EPS)
        h = (xv * r * nw_ref[...]).astype(BF16)
        lane = _lane((tm, LANES))
        lo = lane < HEAD_DIM
        plain = {0: u_ref, 1: gb_ref, 2: gc_ref, 3: zc_ref, 7: za_ref}
        normed = {4: (q_refs, rq_refs), 5: (k_refs, rk_refs)}

        def emit(src, refs, dtype):
            refs[0][...] = _scratch_value(src).astype(dtype)
            for dil, ref in zip(VIEW_DILATIONS, refs[1:]):
                _to_view(src, ref, dil, dtype)

        for j in range(8):
            acc = jnp.dot(h, wg_ref[j], preferred_element_type=F32)
            if j in plain:
                plain[j][...] = acc.astype(BF16)
                continue
            if j == 6:
                for p in range(N_PAIRS):
                    scr[p] = acc[:, p * LANES:(p + 1) * LANES]
                emit(scr, v_refs, BF16)
                continue
            out_refs, r_refs = normed[j]
            r_tile = jnp.zeros((tm, LANES), F32)
            for p in range(N_PAIRS):
                blk = acc[:, p * LANES:(p + 1) * LANES]
                s_lo, s_hi = _head_sums(blk * blk, lo)
                r_lo = lax.rsqrt(s_lo * (1.0 / HEAD_DIM) + EPS)
                r_hi = lax.rsqrt(s_hi * (1.0 / HEAD_DIM) + EPS)
                scr[p] = blk * jnp.where(lo, r_lo, r_hi)
                r_tile = jnp.where(lane == 2 * p, r_lo, r_tile)
                r_tile = jnp.where(lane == 2 * p + 1, r_hi, r_tile)
            rscr[0] = r_tile
            emit(scr, out_refs, BF16)
            emit(rscr, r_refs, F32)

    piece = jax.ShapeDtypeStruct((S, CONV_W), BF16)
    pspec = pl.BlockSpec((tm, CONV_W), lambda i: (i, 0))
    wide = [_view_shape(S, d, ATTN_W, BF16) for d in DILATIONS]
    wide_specs = [_view_spec(tm, d, ATTN_W) for d in DILATIONS]
    stat = [_view_shape(S, d, LANES, F32) for d in DILATIONS]
    stat_specs = [_view_spec(tm, d, LANES) for d in DILATIONS]
    outs = pl.pallas_call(
        body, name="fwd_proj", grid=(S // tm,),
        out_shape=[piece] * 5 + wide * 3 + stat * 2,
        in_specs=[pl.BlockSpec((tm, D_MODEL), lambda i: (i, 0)),
                  pl.BlockSpec((1, D_MODEL), lambda i: (0, 0)),
                  pl.BlockSpec(wg.shape, lambda i: (0, 0, 0), pipeline_mode=pl.Buffered(1))],
        out_specs=[pspec] * 5 + wide_specs * 3 + stat_specs * 2,
        scratch_shapes=[_tile_scratch(tm, ATTN_W), _tile_scratch(tm, LANES)],
        compiler_params=_params(("arbitrary",), 48),
    )(x, norm_w.reshape(1, D_MODEL), wg)
    u, gb, gc, zc, za = outs[:5]
    q_hat, k_hat, v, rq, rk = (outs[5 + 3 * n:8 + 3 * n] for n in range(5))
    return u, gb, gc, zc, za, q_hat, k_hat, v, rq, rk


def _bias_tables(rel_bias, bmat_t):
    def body(rb_ref, b_ref, out_ref):
        h = pl.program_id(1)
        b = b_ref[0]
        t = jnp.full((KB, QB), NEG, F32)
        for bk in range(N_BUCKETS):
            t = jnp.where(b == bk, rb_ref[bk, h], t)
        out_ref[0, 0] = t

    return pl.pallas_call(
        body, name="bias_tables", grid=(len(DILATIONS), N_HEADS),
        out_shape=jax.ShapeDtypeStruct((len(DILATIONS), N_HEADS, KB, QB), F32),
        in_specs=[pl.BlockSpec(memory_space=pltpu.SMEM),
                  pl.BlockSpec((1, KB, QB), lambda g, h: (g, 0, 0))],
        out_specs=pl.BlockSpec((1, 1, KB, QB), lambda g, h: (g, h, 0, 0)),
        compiler_params=_params(("arbitrary", "arbitrary"), 16),
    )(rel_bias, bmat_t)


def _bias_grad(ds_acc, bmat_t):
    def body(ds_ref, b_ref, out_ref):
        lane = _lane((1, LANES))
        row = lax.broadcasted_iota(jnp.int32, (8, LANES), 0)
        out = jnp.zeros((8, LANES), F32)
        for e in range(2):
            vec = jnp.zeros((1, LANES), F32)
            for g in range(len(DILATIONS)):
                b = b_ref[g]
                ds = ds_ref[g, 0, :, e * QB:(e + 1) * QB]
                for bk in range(N_BUCKETS):
                    s = jnp.sum(jnp.where(b == bk, ds, 0.0), axis=-1, keepdims=True)
                    s = jnp.sum(s, axis=0, keepdims=True)
                    vec = vec + jnp.where(lane == bk, s, 0.0)
            out = jnp.where(row == e, vec, out)
        out_ref[0] = out

    return pl.pallas_call(
        body, name="bias_grad", grid=(N_PAIRS,),
        out_shape=jax.ShapeDtypeStruct((N_PAIRS, 8, LANES), F32),
        in_specs=[pl.BlockSpec((len(DILATIONS), 1, KB, 2 * QB), lambda p: (0, p, 0, 0)),
                  pl.BlockSpec((len(DILATIONS), KB, QB), lambda p: (0, 0, 0))],
        out_specs=pl.BlockSpec((1, 8, LANES), lambda p: (p, 0, 0)),
        compiler_params=_params(("arbitrary",), 16),
    )(ds_acc, bmat_t)


def _halo_specs(width, rows, T, L, cols_of):
    per = T // rows
    last = L // rows - 1
    prev = pl.BlockSpec((rows, width), lambda r, i: (jnp.maximum(i * per - 1, 0), cols_of(r)))
    nxt = pl.BlockSpec((rows, width), lambda r, i: (jnp.minimum((i + 1) * per, last), cols_of(r)))
    return prev, nxt


def _attn_fwd(g, dil, qv, kv, vv, gain, tables):
    L = qv.shape[0]
    T = min(1024, L)
    n_sub = T // QB

    def body(q_ref, km_ref, kp_ref, kn_ref, vm_ref, vp_ref, vn_ref, gain_ref, bias_ref, o_ref, lse_ref,
             kwin, vwin, s_scr, p_scr):
        i = pl.program_id(1)
        kwin[0:HALF] = kp_ref[...]
        kwin[HALF:HALF + T] = km_ref[...]
        kwin[HALF + T:] = kn_ref[...]
        vwin[0:HALF] = vp_ref[...]
        vwin[HALF:HALF + T] = vm_ref[...]
        vwin[HALF + T:] = vn_ref[...]
        lo = _lane((QB, LANES)) < HEAD_DIM
        krow = lax.broadcasted_iota(jnp.int32, (KB, 1), 0)
        chan_lo = lax.broadcasted_iota(jnp.int32, (LANES, 1), 0) < HEAD_DIM
        hrow = lax.broadcasted_iota(jnp.int32, (LANES, QB), 0)

        def sub(j, carry):
            r0 = pl.multiple_of(j * QB, QB)
            kpos = i * T + j * QB - HALF + krow
            kvalid = (kpos >= 0) & (kpos < L)
            lse_rows = jnp.zeros((LANES, QB), F32)
            cols = [slice(p * LANES, (p + 1) * LANES) for p in range(N_PAIRS)]
            for p, cs in enumerate(cols):
                qf = q_ref[pl.ds(r0, QB), cs].astype(F32) * gain_ref[:, cs]
                q2 = jnp.concatenate([jnp.where(lo, qf, 0.0), jnp.where(lo, 0.0, qf)], axis=0).astype(BF16)
                s_scr[p] = lax.dot_general(kwin[pl.ds(r0, KB), cs], q2, (((1,), (1,)), ((), ())),
                                           preferred_element_type=F32)
            inv_l = []
            for p in range(N_PAIRS):
                bias2 = jnp.concatenate([bias_ref[0, 2 * p], bias_ref[0, 2 * p + 1]], axis=1)
                logits = jnp.where(kvalid, s_scr[p] + bias2, NEG)
                m = jnp.max(logits, axis=0, keepdims=True)
                pt = jnp.exp(logits - m)
                l = jnp.sum(pt, axis=0, keepdims=True)
                p_scr[p] = pt.astype(BF16)
                inv_l.append(1.0 / l)
                lse2 = m + jnp.log(l)
                lse_rows = jnp.where(hrow == 2 * p, lse2[:, 0:QB], lse_rows)
                lse_rows = jnp.where(hrow == 2 * p + 1, lse2[:, QB:2 * QB], lse_rows)
            for p, cs in enumerate(cols):
                s_scr[p, 0:LANES, :] = lax.dot_general(vwin[pl.ds(r0, KB), cs], p_scr[p], (((0,), (0,)), ((), ())),
                                                       preferred_element_type=F32) * inv_l[p]
            for p, cs in enumerate(cols):
                ot2 = s_scr[p, 0:LANES, :]
                ot = jnp.where(chan_lo, ot2[:, 0:QB], ot2[:, QB:2 * QB])
                o_ref[pl.ds(r0, QB), cs] = ot.T.astype(BF16)
            lse_ref[pl.ds(r0, QB), :] = lse_rows.T
            return carry

        lax.fori_loop(0, n_sub, sub, 0)

    main = pl.BlockSpec((T, ATTN_W), lambda r, i: (i, r))
    prev, nxt = _halo_specs(ATTN_W, HALF, T, L, lambda r: r)
    o_g, lse_g = pl.pallas_call(
        body, name=f"attn_fwd_d{dil}", grid=(dil, L // T),
        out_shape=(jax.ShapeDtypeStruct((L, dil * ATTN_W), BF16), jax.ShapeDtypeStruct((L, dil * LANES), F32)),
        in_specs=[main, main, prev, nxt, main, prev, nxt,
                  pl.BlockSpec((1, ATTN_W), lambda r, i: (0, 0)),
                  pl.BlockSpec((1, N_HEADS, KB, QB), lambda r, i: (g, 0, 0, 0))],
        out_specs=(main, pl.BlockSpec((T, LANES), lambda r, i: (i, r))),
        scratch_shapes=[pltpu.VMEM((T + 2 * HALF, ATTN_W), BF16), pltpu.VMEM((T + 2 * HALF, ATTN_W), BF16),
                        pltpu.VMEM((N_PAIRS, KB, 2 * QB), F32), pltpu.VMEM((N_PAIRS, KB, 2 * QB), BF16)],
        compiler_params=_params(("arbitrary", "arbitrary"), 40),
    )(qv, kv, kv, kv, vv, vv, vv, gain, tables)
    return o_g, lse_g


def _row_halo_specs(width, tm, S, col=0):
    rows = 16
    per = tm // rows
    last = S // rows - 1
    prev = pl.BlockSpec((rows, width), lambda i: (jnp.maximum(i * per - 1, 0), col))
    nxt = pl.BlockSpec((rows, width), lambda i: (jnp.minimum((i + 1) * per, last), col))
    return prev, nxt


def _shift_rows(a, prev_row, next_row, tm):
    row = lax.broadcasted_iota(jnp.int32, a.shape, 0)
    a_m1 = jnp.where(row == 0, prev_row, pltpu.roll(a, 1, 0))
    a_p1 = jnp.where(row == tm - 1, next_row, pltpu.roll(a, tm - 1, 0))
    return a_m1, a_p1


def _combine_gates(o_parts, lse_parts, u, gb, gc, zc, za, conv_w, conv_b):
    S = u.shape[0]
    tm = 512
    n_t = S // tm

    def body(o1, o2, o3, l1, l2, l3, u_ref, up_ref, un_ref, gb_ref, gc_ref, gcp_ref, gcn_ref, zc_ref, za_ref,
             cw_ref, cb_ref, y_ref, o_ref, lse_ref, lse2_ref, lse3_ref, so2, so3, sl2, sl3, slse):
        i = pl.program_id(0)
        lane = _lane((tm, LANES))
        lo = lane < HEAD_DIM
        for blk, scr, dil in ((o2, so2, DILATIONS[1]), (o3, so3, DILATIONS[2]),
                              (l2, sl2, DILATIONS[1]), (l3, sl3, DILATIONS[2])):
            _from_view(blk, scr, dil)
        ls = [l1[...], sl2[0], sl3[0]]
        lmax = jnp.maximum(jnp.maximum(ls[0], ls[1]), ls[2])
        es = [jnp.exp(l - lmax) for l in ls]
        den = es[0] + es[1] + es[2]
        slse[0] = lmax + jnp.log(den)
        lse_ref[...] = slse[0]
        _to_view(slse, lse2_ref, DILATIONS[1], F32)
        _to_view(slse, lse3_ref, DILATIONS[2], F32)
        inv = 1.0 / den
        ws = [e * inv for e in es]
        for p in range(N_PAIRS):
            cs = slice(p * LANES, (p + 1) * LANES)
            acc = jnp.zeros((tm, LANES), F32)
            for w, o_g in zip(ws, (o1[:, cs].astype(F32), so2[p], so3[p])):
                wp = jnp.where(lo, w[:, 2 * p:2 * p + 1], w[:, 2 * p + 1:2 * p + 2])
                acc = acc + wp * o_g
            o_ref[:, cs] = acc.astype(BF16)
            za = za_ref[:, cs].astype(F32)
            y_ref[:, CONV_W + p * LANES:CONV_W + (p + 1) * LANES] = (acc * za * _sigmoid(za)).astype(BF16)
        a = gc_ref[...].astype(F32) * u_ref[...].astype(F32)
        a_prev = gcp_ref[15:16, :].astype(F32) * up_ref[15:16, :].astype(F32) * (i > 0).astype(F32)
        a_next = gcn_ref[0:1, :].astype(F32) * un_ref[0:1, :].astype(F32) * (i < n_t - 1).astype(F32)
        a_m1, a_p1 = _shift_rows(a, a_prev, a_next, tm)
        z3 = cw_ref[0:1, :] * a_m1 + cw_ref[1:2, :] * a + cw_ref[2:3, :] * a_p1
        zc = zc_ref[...].astype(F32)
        y_ref[:, 0:CONV_W] = (gb_ref[...].astype(F32) * (z3 + cb_ref[...]) * zc * _sigmoid(zc)).astype(BF16)

    pspec = pl.BlockSpec((tm, CONV_W), lambda i: (i, 0))
    hp, hn = _row_halo_specs(CONV_W, tm, S)
    wide_specs = [_view_spec(tm, d, ATTN_W) for d in DILATIONS]
    stat_specs = [_view_spec(tm, d, LANES) for d in DILATIONS]
    outs = pl.pallas_call(
        body, name="combine_gates", grid=(n_t,),
        out_shape=[jax.ShapeDtypeStruct((S, D_MODEL), BF16), jax.ShapeDtypeStruct((S, ATTN_W), BF16)]
        + [_view_shape(S, d, LANES, F32) for d in DILATIONS],
        in_specs=wide_specs + stat_specs + [pspec, hp, hn, pspec, pspec, hp, hn, pspec, pspec,
                                            pl.BlockSpec((3, CONV_W), lambda i: (0, 0)),
                                            pl.BlockSpec((1, CONV_W), lambda i: (0, 0))],
        out_specs=[pl.BlockSpec((tm, D_MODEL), lambda i: (i, 0)), pspec] + stat_specs,
        scratch_shapes=[_tile_scratch(tm, ATTN_W), _tile_scratch(tm, ATTN_W),
                        _tile_scratch(tm, LANES), _tile_scratch(tm, LANES), _tile_scratch(tm, LANES)],
        compiler_params=_params(("arbitrary",), 40),
    )(*o_parts, *lse_parts, u, u, u, gb, gc, gc, gc, zc, za, conv_w, conv_b.reshape(1, CONV_W))
    return outs[0], outs[1], outs[2:]


def _out_proj(y, x, target, wo):
    S = x.shape[0]
    tm = 512

    def body(y_ref, x_ref, t_ref, wo_ref, g_ref, dy_ref, dwo_ref, loss_ref):
        i = pl.program_id(0)

        @pl.when(i == 0)
        def _():
            dwo_ref[...] = jnp.zeros_like(dwo_ref)
            loss_ref[...] = jnp.zeros_like(loss_ref)

        yv = y_ref[...]
        wo_v = wo_ref[...]
        err = x_ref[...] + jnp.dot(yv, wo_v, preferred_element_type=F32) - t_ref[...]
        e2 = (err * err).reshape(tm // 8, 8, D_MODEL).sum(axis=0)
        part = e2[:, 0:LANES]
        for k in range(1, D_MODEL // LANES):
            part = part + e2[:, k * LANES:(k + 1) * LANES]
        loss_ref[...] += part
        gv = (err * (1.0 / D_MODEL)).astype(BF16)
        g_ref[...] = gv
        dy_ref[...] = lax.dot_general(gv, wo_v, (((1,), (1,)), ((), ())), preferred_element_type=F32).astype(BF16)
        dwo_ref[...] += lax.dot_general(yv, gv, (((0,), (0,)), ((), ())), preferred_element_type=F32)

    tile = pl.BlockSpec((tm, D_MODEL), lambda i: (i, 0))
    return pl.pallas_call(
        body, name="out_proj", grid=(S // tm,),
        out_shape=(jax.ShapeDtypeStruct((S, D_MODEL), BF16), jax.ShapeDtypeStruct((S, D_MODEL), BF16),
                   jax.ShapeDtypeStruct((D_MODEL, D_MODEL), F32), jax.ShapeDtypeStruct((8, LANES), F32)),
        in_specs=[tile, tile, tile, pl.BlockSpec((D_MODEL, D_MODEL), lambda i: (0, 0), pipeline_mode=pl.Buffered(1))],
        out_specs=(tile, tile, pl.BlockSpec((D_MODEL, D_MODEL), lambda i: (0, 0)),
                   pl.BlockSpec((8, LANES), lambda i: (0, 0))),
        compiler_params=_params(("arbitrary",), 48),
    )(y, x, target, wo)


def _gates_bwd(dy, u, gb, gc, zc, za, o, conv_w, conv_b):
    S = u.shape[0]
    tm = 512
    n_t = S // tm

    def body(dy_ref, dyp_ref, dyn_ref, u_ref, up_ref, un_ref, gb_ref, gbp_ref, gbn_ref, gc_ref, gcp_ref, gcn_ref,
             zc_ref, zcp_ref, zcn_ref, za_ref, o_ref, cw_ref, cb_ref,
             du_ref, dgb_ref, dgc_ref, dzc_ref, dza_ref, small_ref,
             do_ref, do2_ref, do3_ref, delta_ref, delta2_ref, delta3_ref, sdo, sdelta):
        i = pl.program_id(0)

        @pl.when(i == 0)
        def _():
            small_ref[...] = jnp.zeros_like(small_ref)

        def dz3_of(dyc, zc, gb):
            return dyc * zc * _sigmoid(zc) * gb

        f = lambda ref, rows=slice(None): ref[rows, :].astype(F32)
        has_prev = (i > 0).astype(F32)
        has_next = (i < n_t - 1).astype(F32)
        pr, nr = slice(15, 16), slice(0, 1)
        uv, gbv, gcv, zcv = f(u_ref), f(gb_ref), f(gc_ref), f(zc_ref)
        dyc = dy_ref[:, 0:CONV_W].astype(F32)
        a = gcv * uv
        a_m1, a_p1 = _shift_rows(a, f(gcp_ref, pr) * f(up_ref, pr) * has_prev,
                                 f(gcn_ref, nr) * f(un_ref, nr) * has_next, tm)
        w0, w1, w2 = cw_ref[0:1, :], cw_ref[1:2, :], cw_ref[2:3, :]
        cvp = w0 * a_m1 + w1 * a + w2 * a_p1 + cb_ref[...]
        sz = _sigmoid(zcv)
        dcv = dyc * zcv * sz
        dzc_ref[...] = (dyc * gbv * cvp * (sz * (1.0 + zcv * (1.0 - sz)))).astype(BF16)
        dgb_ref[...] = (dcv * cvp).astype(BF16)
        dz3 = dcv * gbv
        dz3_prev = dz3_of(f(dyp_ref, pr), f(zcp_ref, pr), f(gbp_ref, pr)) * has_prev
        dz3_next = dz3_of(f(dyn_ref, nr), f(zcn_ref, nr), f(gbn_ref, nr)) * has_next
        dz3_m1, dz3_p1 = _shift_rows(dz3, dz3_prev, dz3_next, tm)
        da = w0 * dz3_p1 + w1 * dz3 + w2 * dz3_m1
        du_ref[...] = (da * gcv).astype(BF16)
        dgc_ref[...] = (da * uv).astype(BF16)
        csum = lambda t: jnp.sum(t, axis=0, keepdims=True)
        small_ref[0:1, :] += csum(dz3 * a_m1)
        small_ref[1:2, :] += csum(dz3 * a)
        small_ref[2:3, :] += csum(dz3 * a_p1)
        small_ref[3:4, :] += csum(dz3)

        lane = _lane((tm, LANES))
        lo = lane < HEAD_DIM
        delta = jnp.zeros((tm, LANES), F32)
        for p in range(N_PAIRS):
            cs = slice(p * LANES, (p + 1) * LANES)
            dya = dy_ref[:, CONV_W + p * LANES:CONV_W + (p + 1) * LANES].astype(F32)
            zav = za_ref[:, cs].astype(F32)
            ov = o_ref[:, cs].astype(F32)
            sa = _sigmoid(zav)
            d_o = dya * zav * sa
            sdo[p] = d_o
            do_ref[:, cs] = d_o.astype(BF16)
            dza_ref[:, cs] = (dya * ov * (sa * (1.0 + zav * (1.0 - sa)))).astype(BF16)
            s_lo, s_hi = _head_sums(d_o * ov, lo)
            delta = jnp.where(lane == 2 * p, s_lo, delta)
            delta = jnp.where(lane == 2 * p + 1, s_hi, delta)
        sdelta[0] = delta
        delta_ref[...] = delta
        for dil, wide_ref, stat_ref in zip(VIEW_DILATIONS, (do2_ref, do3_ref), (delta2_ref, delta3_ref)):
            _to_view(sdo, wide_ref, dil, BF16)
            _to_view(sdelta, stat_ref, dil, F32)

    pspec = pl.BlockSpec((tm, CONV_W), lambda i: (i, 0))
    hp, hn = _row_halo_specs(CONV_W, tm, S)
    piece = jax.ShapeDtypeStruct((S, CONV_W), BF16)
    outs = pl.pallas_call(
        body, name="gates_bwd", grid=(n_t,),
        out_shape=[piece] * 5 + [jax.ShapeDtypeStruct((8, CONV_W), F32)]
        + [_view_shape(S, d, ATTN_W, BF16) for d in DILATIONS] + [_view_shape(S, d, LANES, F32) for d in DILATIONS],
        in_specs=[pl.BlockSpec((tm, D_MODEL), lambda i: (i, 0)), hp, hn,
                  pspec, hp, hn, pspec, hp, hn, pspec, hp, hn, pspec, hp, hn, pspec, pspec,
                  pl.BlockSpec((3, CONV_W), lambda i: (0, 0)), pl.BlockSpec((1, CONV_W), lambda i: (0, 0))],
        out_specs=[pspec] * 5 + [pl.BlockSpec((8, CONV_W), lambda i: (0, 0))]
        + [_view_spec(tm, d, ATTN_W) for d in DILATIONS] + [_view_spec(tm, d, LANES) for d in DILATIONS],
        scratch_shapes=[_tile_scratch(tm, ATTN_W), _tile_scratch(tm, LANES)],
        compiler_params=_params(("arbitrary",), 40),
    )(dy, dy, dy, u, u, u, gb, gb, gb, gc, gc, gc, zc, zc, zc, za, o, conv_w, conv_b.reshape(1, CONV_W))
    return outs[:5], outs[5], outs[6:9], outs[9:12]


def _attn_bwd(g, dil, qv, kv, vv, dov, lv, dv_, rqv, rkv, gain, tables):
    L = qv.shape[0]
    T = min(1024, L)
    n_sub = T // QB + 1
    QW = T + 2 * HALF
    KW = T + 2 * QB

    def body(qm_ref, qp_ref, qn_ref, dom_ref, dop_ref, don_ref, km_ref, kp_ref, kn_ref, vm_ref, vp_ref, vn_ref,
             lm_ref, lp_ref, ln_ref, dm_ref, dp_ref, dn_ref, rqm_ref, rqp_ref, rqn_ref, rk_ref,
             gain_ref, bias_ref,
             dq_ref, dk_ref, dv_ref, ds_ref, a_ref,
             qwin, dowin, kwin, vwin, lwin, dwin, rqwin, dqwin, dkacc, dvacc,
             s_scr, d_scr, p_scr, ds_scr, q2_scr, do2_scr):
        r = pl.program_id(0)
        i = pl.program_id(1)

        @pl.when((r == 0) & (i == 0))
        def _():
            ds_ref[...] = jnp.zeros_like(ds_ref)
            a_ref[...] = jnp.zeros_like(a_ref)

        for win, (pr, mn, nx), h_rows in ((qwin, (qp_ref, qm_ref, qn_ref), HALF),
                                          (dowin, (dop_ref, dom_ref, don_ref), HALF),
                                          (lwin, (lp_ref, lm_ref, ln_ref), HALF),
                                          (dwin, (dp_ref, dm_ref, dn_ref), HALF),
                                          (rqwin, (rqp_ref, rqm_ref, rqn_ref), HALF),
                                          (kwin, (kp_ref, km_ref, kn_ref), QB),
                                          (vwin, (vp_ref, vm_ref, vn_ref), QB)):
            win[0:h_rows] = pr[...]
            win[h_rows:h_rows + T] = mn[...]
            win[h_rows + T:] = nx[...]
        dkacc[...] = jnp.zeros_like(dkacc)
        dvacc[...] = jnp.zeros_like(dvacc)

        lo = _lane((QB, LANES)) < HEAD_DIM
        krow = lax.broadcasted_iota(jnp.int32, (KB, 1), 0)
        qcol = _lane((1, 2 * QB)) % QB
        row = lax.broadcasted_iota(jnp.int32, (QB, 1), 0)
        chan_lo = lax.broadcasted_iota(jnp.int32, (LANES, 1), 0) < HEAD_DIM

        def sub(j, carry):
            r0 = pl.multiple_of(j * QB, QB)
            kpos = i * T - QB + j * QB + krow
            qpos = i * T - HALF + j * QB + qcol
            valid = (kpos >= 0) & (kpos < L) & (qpos >= 0) & (qpos < L)
            owned_t = ((qpos >= i * T) & (qpos < i * T + T)).astype(F32)
            qpos_c = i * T - HALF + j * QB + row
            owned = ((qpos_c >= i * T) & (qpos_c < i * T + T)).astype(F32)
            lse_rows = lwin[pl.ds(r0, QB), :].T
            delta_rows = dwin[pl.ds(r0, QB), :].T
            rq_t = rqwin[pl.ds(r0, QB), :]
            cols = [slice(p * LANES, (p + 1) * LANES) for p in range(N_PAIRS)]
            nt = (((1,), (1,)), ((), ()))
            for p, cs in enumerate(cols):
                qf = qwin[pl.ds(r0, QB), cs].astype(F32) * gain_ref[:, cs]
                q2_scr[p] = jnp.concatenate([jnp.where(lo, qf, 0.0), jnp.where(lo, 0.0, qf)], axis=0).astype(BF16)
                dov = dowin[pl.ds(r0, QB), cs]
                zero = jnp.zeros_like(dov)
                do2_scr[p] = jnp.concatenate([jnp.where(lo, dov, zero), jnp.where(lo, zero, dov)], axis=0)
                s_scr[p] = lax.dot_general(kwin[pl.ds(r0, KB), cs], q2_scr[p], nt, preferred_element_type=F32)
                d_scr[p] = lax.dot_general(vwin[pl.ds(r0, KB), cs], do2_scr[p], nt, preferred_element_type=F32)
            for p in range(N_PAIRS):
                bias2 = jnp.concatenate([bias_ref[0, 2 * p], bias_ref[0, 2 * p + 1]], axis=1)
                lse2 = jnp.concatenate([lse_rows[2 * p:2 * p + 1, :], lse_rows[2 * p + 1:2 * p + 2, :]], axis=1)
                delta2 = jnp.concatenate([delta_rows[2 * p:2 * p + 1, :], delta_rows[2 * p + 1:2 * p + 2, :]], axis=1)
                pt = jnp.where(valid, jnp.exp(s_scr[p] + bias2 - lse2), 0.0)
                dst = pt * (d_scr[p] - delta2)
                ds_ref[p] += dst * owned_t
                p_scr[p] = pt.astype(BF16)
                ds_scr[p] = dst.astype(BF16)
            for p, cs in enumerate(cols):
                dvacc[pl.ds(r0, KB), cs] += jnp.dot(p_scr[p], do2_scr[p], preferred_element_type=F32)
                dkacc[pl.ds(r0, KB), cs] += jnp.dot(ds_scr[p], q2_scr[p], preferred_element_type=F32)
                s_scr[p, 0:LANES, :] = lax.dot_general(kwin[pl.ds(r0, KB), cs], ds_scr[p], (((0,), (0,)), ((), ())),
                                                       preferred_element_type=F32)
            for p, cs in enumerate(cols):
                g2 = gain_ref[:, cs]
                q_f = qwin[pl.ds(r0, QB), cs].astype(F32)
                dqt2 = s_scr[p, 0:LANES, :]
                dqm2 = jnp.where(chan_lo, dqt2[:, 0:QB], dqt2[:, QB:2 * QB]).T
                a_ref[p] += (dqm2 * q_f * owned).reshape(QB // 8, 8, LANES).sum(axis=0)
                dqh = dqm2 * g2
                c_lo, c_hi = _head_sums(dqh * q_f, lo)
                mean_c = jnp.where(lo, c_lo, c_hi) * (1.0 / HEAD_DIM)
                rq2 = jnp.where(lo, rq_t[:, 2 * p:2 * p + 1], rq_t[:, 2 * p + 1:2 * p + 2])
                dqwin[pl.ds(r0, QB), cs] = (rq2 * (dqh - q_f * mean_c)).astype(BF16)
            return carry

        lax.fori_loop(0, n_sub, sub, 0)
        dq_ref[...] = dqwin[HALF:HALF + T]

        def finish(j, carry):
            r0 = pl.multiple_of(j * QB, QB)
            rk_t = rk_ref[pl.ds(r0, QB), :]
            for p in range(N_PAIRS):
                cs = slice(p * LANES, (p + 1) * LANES)
                dkh = dkacc[pl.ds(QB + r0, QB), cs]
                k_f = km_ref[pl.ds(r0, QB), cs].astype(F32)
                c_lo, c_hi = _head_sums(dkh * k_f, lo)
                mean_c = jnp.where(lo, c_lo, c_hi) * (1.0 / HEAD_DIM)
                rk2 = jnp.where(lo, rk_t[:, 2 * p:2 * p + 1], rk_t[:, 2 * p + 1:2 * p + 2])
                dk_ref[pl.ds(r0, QB), cs] = (rk2 * (dkh - k_f * mean_c)).astype(BF16)
                dv_ref[pl.ds(r0, QB), cs] = dvacc[pl.ds(QB + r0, QB), cs].astype(BF16)
            return carry

        lax.fori_loop(0, T // QB, finish, 0)

    main = pl.BlockSpec((T, ATTN_W), lambda r, i: (i, r))
    smain = pl.BlockSpec((T, LANES), lambda r, i: (i, r))
    q_prev, q_next = _halo_specs(ATTN_W, HALF, T, L, lambda r: r)
    k_prev, k_next = _halo_specs(ATTN_W, QB, T, L, lambda r: r)
    s_prev, s_next = _halo_specs(LANES, HALF, T, L, lambda r: r)
    piece = jax.ShapeDtypeStruct((L, dil * ATTN_W), BF16)
    return pl.pallas_call(
        body, name=f"attn_bwd_d{dil}", grid=(dil, L // T),
        out_shape=(piece, piece, piece, jax.ShapeDtypeStruct((N_PAIRS, KB, 2 * QB), F32),
                   jax.ShapeDtypeStruct((N_PAIRS, 8, LANES), F32)),
        in_specs=[main, q_prev, q_next, main, q_prev, q_next, main, k_prev, k_next, main, k_prev, k_next,
                  smain, s_prev, s_next, smain, s_prev, s_next, smain, s_prev, s_next, smain,
                  pl.BlockSpec((1, ATTN_W), lambda r, i: (0, 0)),
                  pl.BlockSpec((1, N_HEADS, KB, QB), lambda r, i: (g, 0, 0, 0))],
        out_specs=(main, main, main, pl.BlockSpec((N_PAIRS, KB, 2 * QB), lambda r, i: (0, 0, 0)),
                   pl.BlockSpec((N_PAIRS, 8, LANES), lambda r, i: (0, 0, 0))),
        scratch_shapes=[pltpu.VMEM((QW, ATTN_W), BF16), pltpu.VMEM((QW, ATTN_W), BF16),
                        pltpu.VMEM((KW, ATTN_W), BF16), pltpu.VMEM((KW, ATTN_W), BF16),
                        pltpu.VMEM((QW, LANES), F32), pltpu.VMEM((QW, LANES), F32), pltpu.VMEM((QW, LANES), F32),
                        pltpu.VMEM((QW, ATTN_W), BF16),
                        pltpu.VMEM((KW, ATTN_W), F32), pltpu.VMEM((KW, ATTN_W), F32),
                        pltpu.VMEM((N_PAIRS, KB, 2 * QB), F32), pltpu.VMEM((N_PAIRS, KB, 2 * QB), F32),
                        pltpu.VMEM((N_PAIRS, KB, 2 * QB), BF16), pltpu.VMEM((N_PAIRS, KB, 2 * QB), BF16),
                        pltpu.VMEM((N_PAIRS, 2 * QB, LANES), BF16), pltpu.VMEM((N_PAIRS, 2 * QB, LANES), BF16)],
        compiler_params=_params(("arbitrary", "arbitrary"), 56),
    )(qv, qv, qv, dov, dov, dov, kv, kv, kv, vv, vv, vv, lv, lv, lv, dv_, dv_, dv_, rqv, rqv, rqv, rkv, gain, tables)


def _proj_bwd(x, d_out, norm_w, wg, pieces):
    S = x.shape[0]
    tm = 256
    flat = [share for p in pieces for share in p]
    n_t = S // tm

    def body(*refs):
        x_ref, g_ref, nw_ref, wg_ref = refs[:4]
        piece_refs = refs[4:4 + len(flat)]
        gx_ref, dwin_hbm, dnw_ref, acc, sview, sem = refs[4 + len(flat):]
        i = pl.program_id(0)

        @pl.when(i == 0)
        def _():
            acc[...] = jnp.zeros_like(acc)
            dnw_ref[...] = jnp.zeros_like(dnw_ref)

        xv = x_ref[...]
        r = lax.rsqrt(jnp.mean(xv * xv, axis=-1, keepdims=True) + EPS)
        xh = xv * r
        nw = nw_ref[...]
        h = (xh * nw).astype(BF16)
        dh = jnp.zeros((tm, D_MODEL), F32)
        k = 0
        for j in range(8):
            shares = pieces[j]
            if len(shares) == 1:
                dp = piece_refs[k][...]
            else:
                dp = jnp.zeros((tm, CONV_W), F32)
                for t, (_, dil) in enumerate(shares):
                    if dil == 1:
                        dp = dp + piece_refs[k + t][...].astype(F32)
                    else:
                        _from_view(piece_refs[k + t], sview, dil)
                        dp = dp + _scratch_value(sview)
                dp = dp.astype(BF16)
            k += len(shares)
            dh = dh + lax.dot_general(dp, wg_ref[j], (((1,), (1,)), ((), ())), preferred_element_type=F32)
            acc[j] += lax.dot_general(h, dp, (((0,), (0,)), ((), ())), preferred_element_type=F32)
        dnw_ref[...] += (dh * xh).reshape(tm // 8, 8, D_MODEL).sum(axis=0)
        dxh = dh * nw
        mean_c = jnp.mean(dxh * xh, axis=-1, keepdims=True)
        gx_ref[...] = g_ref[...].astype(F32) + r * (dxh - xh * mean_c)

        @pl.when(i == n_t - 1)
        def _():
            cp = pltpu.make_async_copy(acc, dwin_hbm, sem)
            cp.start()
            cp.wait()

    tile = pl.BlockSpec((tm, D_MODEL), lambda i: (i, 0))
    return pl.pallas_call(
        body, name="proj_bwd", grid=(n_t,),
        out_shape=(jax.ShapeDtypeStruct((S, D_MODEL), F32), jax.ShapeDtypeStruct(wg.shape, F32),
                   jax.ShapeDtypeStruct((8, D_MODEL), F32)),
        in_specs=[tile, tile, pl.BlockSpec((1, D_MODEL), lambda i: (0, 0)),
                  pl.BlockSpec(wg.shape, lambda i: (0, 0, 0), pipeline_mode=pl.Buffered(1))]
        + [_view_spec(tm, dil, CONV_W) for _, dil in flat],
        out_specs=(tile, pl.BlockSpec(memory_space=pl.ANY), pl.BlockSpec((8, D_MODEL), lambda i: (0, 0))),
        scratch_shapes=[pltpu.VMEM(wg.shape, F32), _tile_scratch(tm, CONV_W), pltpu.SemaphoreType.DMA],
        compiler_params=_params(("arbitrary",), 56),
    )(x, d_out, norm_w.reshape(1, D_MODEL), wg, *[a for a, _ in flat])


def _exchange_grads(gw_in, gw_out, small):
    n_arr = 3

    def body(gin_ref, gout_ref, sm_ref, rin_ref, rout_ref, rsm_ref, send_sems, recv_sems, local_sems):
        x, y, c = lax.axis_index("x"), lax.axis_index("y"), lax.axis_index("c")
        me = 4 * x + 2 * y + c
        locals_ = [pltpu.make_async_copy(gin_ref.at[me], rin_ref.at[me], local_sems.at[0]),
                   pltpu.make_async_copy(gout_ref.at[me], rout_ref.at[me], local_sems.at[1])]
        for cp in locals_:
            cp.start()
        rsm_ref[me] = sm_ref[...]
        copies = []
        for k in range(1, N_DEV):
            px = 1 - x if k & 4 else x
            py = 1 - y if k & 2 else y
            pc = 1 - c if k & 1 else c
            peer = 4 * px + 2 * py + pc
            srcs = (gin_ref.at[peer], gout_ref.at[peer], sm_ref)
            dsts = (rin_ref.at[me], rout_ref.at[me], rsm_ref.at[me])
            for a in range(n_arr):
                copies.append(pltpu.make_async_remote_copy(
                    src_ref=srcs[a], dst_ref=dsts[a], send_sem=send_sems.at[a * 7 + k - 1],
                    recv_sem=recv_sems.at[a * 7 + k - 1], device_id=(px, py, pc), device_id_type=MESH))
        for cp in copies:
            cp.start()
        for cp in copies:
            cp.wait_recv()
        for cp in copies:
            cp.wait_send()
        for cp in locals_:
            cp.wait()

    hbm = pl.BlockSpec(memory_space=pl.ANY)
    vm = pl.BlockSpec(memory_space=pltpu.VMEM)
    return pl.pallas_call(
        body, name="exchange_grads",
        out_shape=(jax.ShapeDtypeStruct(gw_in.shape, F32), jax.ShapeDtypeStruct(gw_out.shape, F32),
                   jax.ShapeDtypeStruct((N_DEV,) + small.shape, F32)),
        in_specs=[hbm, hbm, vm], out_specs=(hbm, hbm, vm),
        scratch_shapes=[pltpu.SemaphoreType.DMA((n_arr * 7,)), pltpu.SemaphoreType.DMA((n_arr * 7,)),
                        pltpu.SemaphoreType.DMA((2,))],
        compiler_params=pltpu.CompilerParams(vmem_limit_bytes=16 * MIB),
    )(gw_in, gw_out, small)


def _adamw_math(w, g, m, v):
    m2 = ADAM_B1 * m + (1.0 - ADAM_B1) * g
    v2 = ADAM_B2 * v + (1.0 - ADAM_B2) * (g * g)
    m_hat = m2 / (1.0 - ADAM_B1 ** ADAM_STEP)
    v_hat = v2 / (1.0 - ADAM_B2 ** ADAM_STEP)
    delta = -ADAM_LR * (m_hat / (jnp.sqrt(v_hat) + ADAM_EPS) + ADAM_WD * w)
    return delta, m2, v2


def _adamw_sharded(name, parts, w, m, v, rows):
    R, C = w.shape

    def body(p_ref, w_ref, m_ref, v_ref, g_ref, d_ref, m2_ref, v2_ref):
        g = p_ref[0]
        for s in range(1, N_DEV):
            g = g + p_ref[s]
        g_ref[...] = g
        d_ref[...], m2_ref[...], v2_ref[...] = _adamw_math(w_ref[...], g, m_ref[...], v_ref[...])

    spec = pl.BlockSpec((rows, C), lambda i: (i, 0))
    out = jax.ShapeDtypeStruct((R, C), F32)
    return pl.pallas_call(
        body, name=name, grid=(R // rows,),
        out_shape=(out,) * 4,
        in_specs=[pl.BlockSpec((N_DEV, rows, C), lambda i: (0, i, 0)), spec, spec, spec],
        out_specs=(spec,) * 4,
        compiler_params=_params(("arbitrary",), 40),
    )(parts, w, m, v)


def _adamw_small(parts, w, m, v):
    R = w.shape[0]

    def body(p_ref, w_ref, m_ref, v_ref, g_ref, d_ref, m2_ref, v2_ref):
        g = p_ref[0]
        for s in range(1, N_DEV):
            g = g + p_ref[s]
        g_ref[...] = g
        d_ref[...], m2_ref[...], v2_ref[...] = _adamw_math(w_ref[...], g, m_ref[...], v_ref[...])

    vm = pl.BlockSpec(memory_space=pltpu.VMEM)
    out = jax.ShapeDtypeStruct((R, LANES), F32)
    return pl.pallas_call(
        body, name="adamw_small", out_shape=(out,) * 4, in_specs=[vm] * 4, out_specs=(vm,) * 4,
    )(parts, w, m, v)


def _qk_gain_grads(a_parts, q_norm_w, k_norm_w, bias_g):
    def body(a_ref, qw_ref, kw_ref, bg_ref, out_ref):
        tot = jnp.zeros((8, LANES), F32)
        for g in range(len(DILATIONS)):
            for p in range(N_PAIRS):
                tot = tot + a_ref[g, p]
        col = jnp.sum(tot, axis=0, keepdims=True)
        a64 = col[:, 0:HEAD_DIM] + col[:, HEAD_DIM:LANES]
        out_ref[...] = jnp.zeros_like(out_ref)
        out_ref[0:1, 0:HEAD_DIM] = LOGIT_SCALE * kw_ref[...] * a64
        out_ref[1:2, 0:HEAD_DIM] = LOGIT_SCALE * qw_ref[...] * a64
        for h in range(N_HEADS):
            r0, c0 = 2 + h // 4, (h % 4) * N_BUCKETS
            out_ref[r0:r0 + 1, c0:c0 + N_BUCKETS] = bg_ref[h // 2, h % 2:h % 2 + 1, 0:N_BUCKETS]

    vm = pl.BlockSpec(memory_space=pltpu.VMEM)
    return pl.pallas_call(
        body, name="qk_gain_grads", out_shape=jax.ShapeDtypeStruct((8, LANES), F32),
        in_specs=[vm] * 4, out_specs=vm,
    )(a_parts, q_norm_w.reshape(1, HEAD_DIM), k_norm_w.reshape(1, HEAD_DIM), bias_g)


SMALL_ROWS = 32


def _pack_small(norm_w, conv_b, q_norm_w, k_norm_w, rel_bias, conv_w_full):
    pad64 = lambda a: jnp.pad(a, (0, LANES - HEAD_DIM)).reshape(1, LANES)
    return jnp.concatenate([
        norm_w.reshape(8, LANES), conv_b.reshape(4, LANES), pad64(q_norm_w), pad64(k_norm_w),
        rel_bias.T.reshape(2, LANES), conv_w_full.reshape(12, LANES), jnp.zeros((4, LANES), F32)], axis=0)


def _unpack_small(s):
    return (s[0:8].reshape(D_MODEL), s[8:12].reshape(CONV_W), s[12, 0:HEAD_DIM], s[13, 0:HEAD_DIM],
            s[14:16].reshape(N_HEADS, N_BUCKETS).T, s[16:28].reshape(3, CONV_W))


def kernel(x, norm_w, w_in, conv_w, conv_b, q_norm_w, k_norm_w, rel_bias, w_out, loss_target, m_norm_w, m_w_in, m_conv_w, m_conv_b, m_q_norm_w, m_k_norm_w, m_rel_bias, m_w_out, v_norm_w, v_w_in, v_conv_w, v_conv_b, v_q_norm_w, v_k_norm_w, v_rel_bias, v_w_out):
    S = x.shape[1]
    x2 = x.reshape(S, D_MODEL)
    tgt = loss_target.reshape(S, D_MODEL)
    me = 4 * lax.axis_index("x") + 2 * lax.axis_index("y") + lax.axis_index("c")
    shard_w = CONV_W // N_DEV

    cw_pad = jnp.pad(conv_w, ((0, 5), (0, LANES - shard_w)))
    wg, wo_g, cw_g = _ag_weights(w_in, w_out, cw_pad)
    wo = wo_g.reshape(D_MODEL, D_MODEL)
    conv_w_full = cw_g[:, 0:3, 0:shard_w].transpose(1, 0, 2).reshape(3, CONV_W)

    bmat_t = jnp.asarray(np.stack([_bucket_matrix(d).T for d in DILATIONS]))
    tables = _bias_tables(rel_bias, bmat_t)
    gain = jnp.tile(q_norm_w * k_norm_w * LOGIT_SCALE, N_HEADS).reshape(1, ATTN_W)

    u, gb, gc, zc, za, q_hat, k_hat, v, rq, rk = _fwd_proj(x2, norm_w, wg)
    o_parts, lse_parts = [], []
    for g, dil in enumerate(DILATIONS):
        o_g, lse_g = _attn_fwd(g, dil, q_hat[g], k_hat[g], v[g], gain, tables)
        o_parts.append(o_g)
        lse_parts.append(lse_g)
    y, o, lse = _combine_gates(o_parts, lse_parts, u, gb, gc, zc, za, conv_w_full, conv_b)
    d_out, dy, gwo_part, loss_part = _out_proj(y, x2, tgt, wo)
    loss = lax.psum(jnp.sum(loss_part) * (0.5 / D_MODEL), ("x", "y", "c"))

    (du, dgb, dgc, dzc, dza), conv_small, d_o, delta = _gates_bwd(dy, u, gb, gc, zc, za, o, conv_w_full, conv_b)
    dqs, dks, dvs, ds_parts, a_parts = [], [], [], [], []
    for g, dil in enumerate(DILATIONS):
        dq, dk, dv, ds_sum, a_sum = _attn_bwd(g, dil, q_hat[g], k_hat[g], v[g], d_o[g], lse[g], delta[g], rq[g], rk[g],
                                              gain, tables)
        dqs.append((dq, dil))
        dks.append((dk, dil))
        dvs.append((dv, dil))
        ds_parts.append(ds_sum)
        a_parts.append(a_sum)
    grad_x, gwin_part, gnw_part = _proj_bwd(
        x2, d_out, norm_w, wg, [[(du, 1)], [(dgb, 1)], [(dgc, 1)], [(dzc, 1)], dqs, dks, dvs, [(dza, 1)]])
    bias_g = _bias_grad(jnp.stack(ds_parts), bmat_t)
    qk_small = _qk_gain_grads(jnp.stack(a_parts), q_norm_w, k_norm_w, bias_g)

    small_part = jnp.concatenate([
        gnw_part.sum(axis=0).reshape(8, LANES), conv_small[3].reshape(4, LANES), qk_small[0:4],
        conv_small[0:3].reshape(12, LANES), jnp.zeros((4, LANES), F32)], axis=0)
    r_in, r_out, r_small = _exchange_grads(gwin_part, gwo_part.reshape(N_DEV, D_MODEL // N_DEV, D_MODEL), small_part)

    g_win, d_win, m_win, v_win = _adamw_sharded("adamw_w_in", r_in, w_in, m_w_in, v_w_in, 128)
    g_wo, d_wo, m_wo, v_wo = _adamw_sharded("adamw_w_out", r_out, w_out, m_w_out, v_w_out, 32)

    def full_conv(a):
        return jnp.zeros((3, CONV_W), F32).at[:, 0:shard_w].set(a)

    packs = [_pack_small(nw_, cb_, qw_, kw_, rb_, full_conv(cw_)) for nw_, cb_, qw_, kw_, rb_, cw_ in (
        (norm_w, conv_b, q_norm_w, k_norm_w, rel_bias, conv_w),
        (m_norm_w, m_conv_b, m_q_norm_w, m_k_norm_w, m_rel_bias, m_conv_w),
        (v_norm_w, v_conv_b, v_q_norm_w, v_k_norm_w, v_rel_bias, v_conv_w))]
    r_small_mine = r_small.at[:, 16:28, :].set(
        jnp.pad(lax.dynamic_slice_in_dim(r_small[:, 16:28, :].reshape(N_DEV, 3, CONV_W), me * shard_w, shard_w, axis=2),
                ((0, 0), (0, 0), (0, CONV_W - shard_w))).reshape(N_DEV, 12, LANES))
    outs_small = _adamw_small(r_small_mine, *packs)
    g_s, d_s, m_s, v_s = [_unpack_small(t) for t in outs_small]

    def leaves(small, big_in, big_out):
        nw_, cb_, qw_, kw_, rb_, cwf = small
        return (nw_, big_in, cwf[:, 0:shard_w], cb_, qw_, kw_, rb_, big_out)

    return (loss, grad_x.reshape(x.shape),
            *leaves(g_s, g_win, g_wo), *leaves(d_s, d_win, d_wo), *leaves(m_s, m_win, m_wo), *leaves(v_s, v_win, v_wo))
```

```python
import functools
import math

import numpy as np
import jax
import jax.numpy as jnp
from jax import lax
from jax.experimental import pallas as pl
from jax.experimental.pallas import tpu as pltpu

F32 = jnp.float32
BF16 = jnp.bfloat16

N_DEV = 8
D_MODEL = 1024
CONV_W = 512
ATTN_W = 512
N_HEADS = 8
HEAD_DIM = 64
N_PAIRS = N_HEADS // 2
LANES = 128
HALF = 64
QB = 128
KB = QB + 2 * HALF
DILATIONS = (1, 4, 16)
N_BUCKETS = 32
MAX_DISTANCE = 1024
EPS = 1e-6
NEG = -1e30
LOGIT_SCALE = HEAD_DIM ** -0.5

ADAM_LR = 0.001
ADAM_B1 = 0.9
ADAM_B2 = 0.999
ADAM_EPS = 1e-08
ADAM_WD = 0.01
ADAM_STEP = 10

MESH = pl.DeviceIdType.MESH
MIB = 1024 * 1024


def _params(semantics, vmem_mib):
    return pltpu.CompilerParams(dimension_semantics=semantics, vmem_limit_bytes=vmem_mib * MIB)


def _lane(shape):
    return lax.broadcasted_iota(jnp.int32, shape, len(shape) - 1)


def _sigmoid(z):
    return 1.0 / (1.0 + jnp.exp(-z))


def _head_sums(prod, lo_mask):
    s_lo = jnp.sum(jnp.where(lo_mask, prod, 0.0), axis=-1, keepdims=True)
    s_hi = jnp.sum(jnp.where(lo_mask, 0.0, prod), axis=-1, keepdims=True)
    return s_lo, s_hi


VIEW_DILATIONS = DILATIONS[1:]


def _view_shape(S, dil, width, dtype):
    return jax.ShapeDtypeStruct((S // dil, dil * width), dtype)


def _view_spec(tm, dil, width):
    return pl.BlockSpec((tm // dil, dil * width), lambda i: (i, 0))


def _tile_scratch(tm, width):
    return pltpu.VMEM((width // LANES, tm, LANES), F32)


def _scratch_value(scr_ref):
    n = scr_ref.shape[0]
    return scr_ref[0] if n == 1 else jnp.concatenate([scr_ref[c] for c in range(n)], axis=1)


def _to_view(scr_ref, out_ref, dil, dtype):
    n, tm, _ = scr_ref.shape
    for r in range(dil):
        for c in range(n):
            col = (r * n + c) * LANES
            out_ref[:, col:col + LANES] = scr_ref[c, pl.ds(r, tm // dil, stride=dil), :].astype(dtype)


def _from_view(blk_ref, scr_ref, dil):
    n, tm, _ = scr_ref.shape
    for r in range(dil):
        for c in range(n):
            col = (r * n + c) * LANES
            scr_ref[c, pl.ds(r, tm // dil, stride=dil), :] = blk_ref[:, col:col + LANES].astype(F32)


def _bucket_matrix(dilation):
    rel = np.arange(KB)[None, :] - HALF - np.arange(QB)[:, None]
    band = np.abs(rel) <= HALF
    dist = np.clip(rel, -HALF, HALF) * dilation
    half_b = N_BUCKETS // 2
    max_exact = half_b // 2
    ret = np.where(dist > 0, half_b, 0)
    n = np.abs(dist)
    nf = np.maximum(n, 1).astype(np.float32)
    large = max_exact + (np.log(nf / np.float32(max_exact)) / np.float32(math.log(MAX_DISTANCE / max_exact))
                         * np.float32(half_b - max_exact)).astype(np.int32)
    large = np.minimum(large, half_b - 1)
    bucket = ret + np.where(n < max_exact, n, large)
    return np.where(band, bucket, -1).astype(np.int32)


def _ag_weights(w_in, w_out, cw_pad):
    n_arr = 3

    def body(win_ref, wout_ref, cw_ref, gin_ref, gout_ref, gcw_ref, send_sems, recv_sems):
        x, y, c = lax.axis_index("x"), lax.axis_index("y"), lax.axis_index("c")
        me = (x, y, c)
        sibling = (x, y, 1 - c)
        chips = [(1 - x, y), (x, 1 - y), (1 - x, 1 - y)]
        arrays = (gin_ref, gout_ref, gcw_ref)

        def slot(px, py, pc):
            return 4 * px + 2 * py + pc

        gin_ref[slot(*me)] = win_ref[...].astype(BF16)
        gout_ref[slot(*me)] = wout_ref[...].astype(BF16)
        gcw_ref[slot(*me)] = cw_ref[...]

        def copy(a, k, block, to):
            ref = arrays[a].at[slot(*block)]
            return pltpu.make_async_remote_copy(
                src_ref=ref, dst_ref=ref, send_sem=send_sems.at[a * 7 + k], recv_sem=recv_sems.at[a * 7 + k],
                device_id=to, device_id_type=MESH)

        first = [copy(a, 0, me, sibling) for a in range(n_arr)]
        for j, chip in enumerate(chips):
            first += [copy(a, 1 + j, me, (*chip, c)) for a in range(n_arr)]
        for cp in first:
            cp.start()
        passed = []
        for j, chip in enumerate(chips):
            for a in range(n_arr):
                copy(a, 1 + j, (*chip, c), me).wait_recv()
            for a in range(n_arr):
                cp = copy(a, 4 + j, (*chip, c), sibling)
                cp.start()
                passed.append(cp)
        for a in range(n_arr):
            copy(a, 0, sibling, me).wait_recv()
        for j, chip in enumerate(chips):
            for a in range(n_arr):
                copy(a, 4 + j, (*chip, 1 - c), me).wait_recv()
        for cp in first + passed:
            cp.wait_send()

    vm = pl.BlockSpec(memory_space=pltpu.VMEM)
    return pl.pallas_call(
        body, name="ag_weights",
        out_shape=(jax.ShapeDtypeStruct((N_DEV,) + w_in.shape, BF16),
                   jax.ShapeDtypeStruct((N_DEV,) + w_out.shape, BF16),
                   jax.ShapeDtypeStruct((N_DEV,) + cw_pad.shape, F32)),
        in_specs=[vm, vm, vm], out_specs=(vm, vm, vm),
        scratch_shapes=[pltpu.SemaphoreType.DMA((n_arr * 7,)), pltpu.SemaphoreType.DMA((n_arr * 7,))],
        compiler_params=pltpu.CompilerParams(vmem_limit_bytes=40 * MIB),
    )(w_in, w_out, cw_pad)


def _fwd_proj(x, norm_w, wg):
    S = x.shape[0]
    tm = 512

    def body(x_ref, nw_ref, wg_ref, u_ref, gb_ref, gc_ref, zc_ref, za_ref, *rest):
        q_refs, k_refs, v_refs, rq_refs, rk_refs = (rest[3 * n:3 * n + 3] for n in range(5))
        scr, rscr = rest[15:]
        xv = x_ref[...]
        r = lax.rsqrt(jnp.mean(xv * xv, axis=-1, keepdims=True) + EPS)
        h = (xv * r * nw_ref[...]).astype(BF16)
        lane = _lane((tm, LANES))
        lo = lane < HEAD_DIM
        plain = {0: u_ref, 1: gb_ref, 2: gc_ref, 3: zc_ref, 7: za_ref}
        normed = {4: (q_refs, rq_refs), 5: (k_refs, rk_refs)}

        def emit(src, refs, dtype):
            refs[0][...] = _scratch_value(src).astype(dtype)
            for dil, ref in zip(VIEW_DILATIONS, refs[1:]):
                _to_view(src, ref, dil, dtype)

        for j in range(8):
            acc = jnp.dot(h, wg_ref[j], preferred_element_type=F32)
            if j in plain:
                plain[j][...] = acc.astype(BF16)
                continue
            if j == 6:
                for p in range(N_PAIRS):
                    scr[p] = acc[:, p * LANES:(p + 1) * LANES]
                emit(scr, v_refs, BF16)
                continue
            out_refs, r_refs = normed[j]
            r_tile = jnp.zeros((tm, LANES), F32)
            for p in range(N_PAIRS):
                blk = acc[:, p * LANES:(p + 1) * LANES]
                s_lo, s_hi = _head_sums(blk * blk, lo)
                r_lo = lax.rsqrt(s_lo * (1.0 / HEAD_DIM) + EPS)
                r_hi = lax.rsqrt(s_hi * (1.0 / HEAD_DIM) + EPS)
                scr[p] = blk * jnp.where(lo, r_lo, r_hi)
                r_tile = jnp.where(lane == 2 * p, r_lo, r_tile)
                r_tile = jnp.where(lane == 2 * p + 1, r_hi, r_tile)
            rscr[0] = r_tile
            emit(scr, out_refs, BF16)
            emit(rscr, r_refs, F32)

    piece = jax.ShapeDtypeStruct((S, CONV_W), BF16)
    pspec = pl.BlockSpec((tm, CONV_W), lambda i: (i, 0))
    wide = [_view_shape(S, d, ATTN_W, BF16) for d in DILATIONS]
    wide_specs = [_view_spec(tm, d, ATTN_W) for d in DILATIONS]
    stat = [_view_shape(S, d, LANES, F32) for d in DILATIONS]
    stat_specs = [_view_spec(tm, d, LANES) for d in DILATIONS]
    outs = pl.pallas_call(
        body, name="fwd_proj", grid=(S // tm,),
        out_shape=[piece] * 5 + wide * 3 + stat * 2,
        in_specs=[pl.BlockSpec((tm, D_MODEL), lambda i: (i, 0)),
                  pl.BlockSpec((1, D_MODEL), lambda i: (0, 0)),
                  pl.BlockSpec(wg.shape, lambda i: (0, 0, 0), pipeline_mode=pl.Buffered(1))],
        out_specs=[pspec] * 5 + wide_specs * 3 + stat_specs * 2,
        scratch_shapes=[_tile_scratch(tm, ATTN_W), _tile_scratch(tm, LANES)],
        compiler_params=_params(("arbitrary",), 48),
    )(x, norm_w.reshape(1, D_MODEL), wg)
    u, gb, gc, zc, za = outs[:5]
    q_hat, k_hat, v, rq, rk = (outs[5 + 3 * n:8 + 3 * n] for n in range(5))
    return u, gb, gc, zc, za, q_hat, k_hat, v, rq, rk


def _bias_tables(rel_bias, bmat_t):
    def body(rb_ref, b_ref, out_ref):
        h = pl.program_id(1)
        b = b_ref[0]
        t = jnp.full((KB, QB), NEG, F32)
        for bk in range(N_BUCKETS):
            t = jnp.where(b == bk, rb_ref[bk, h], t)
        out_ref[0, 0] = t

    return pl.pallas_call(
        body, name="bias_tables", grid=(len(DILATIONS), N_HEADS),
        out_shape=jax.ShapeDtypeStruct((len(DILATIONS), N_HEADS, KB, QB), F32),
        in_specs=[pl.BlockSpec(memory_space=pltpu.SMEM),
                  pl.BlockSpec((1, KB, QB), lambda g, h: (g, 0, 0))],
        out_specs=pl.BlockSpec((1, 1, KB, QB), lambda g, h: (g, h, 0, 0)),
        compiler_params=_params(("arbitrary", "arbitrary"), 16),
    )(rel_bias, bmat_t)


def _bias_grad(ds_acc, bmat_t):
    def body(ds_ref, b_ref, out_ref):
        lane = _lane((1, LANES))
        row = lax.broadcasted_iota(jnp.int32, (8, LANES), 0)
        out = jnp.zeros((8, LANES), F32)
        for e in range(2):
            vec = jnp.zeros((1, LANES), F32)
            for g in range(len(DILATIONS)):
                b = b_ref[g]
                ds = ds_ref[g, 0, :, e * QB:(e + 1) * QB]
                for bk in range(N_BUCKETS):
                    s = jnp.sum(jnp.where(b == bk, ds, 0.0), axis=-1, keepdims=True)
                    s = jnp.sum(s, axis=0, keepdims=True)
                    vec = vec + jnp.where(lane == bk, s, 0.0)
            out = jnp.where(row == e, vec, out)
        out_ref[0] = out

    return pl.pallas_call(
        body, name="bias_grad", grid=(N_PAIRS,),
        out_shape=jax.ShapeDtypeStruct((N_PAIRS, 8, LANES), F32),
        in_specs=[pl.BlockSpec((len(DILATIONS), 1, KB, 2 * QB), lambda p: (0, p, 0, 0)),
                  pl.BlockSpec((len(DILATIONS), KB, QB), lambda p: (0, 0, 0))],
        out_specs=pl.BlockSpec((1, 8, LANES), lambda p: (p, 0, 0)),
        compiler_params=_params(("arbitrary",), 16),
    )(ds_acc, bmat_t)


def _halo_specs(width, rows, T, L, cols_of):
    per = T // rows
    last = L // rows - 1
    prev = pl.BlockSpec((rows, width), lambda r, i: (jnp.maximum(i * per - 1, 0), cols_of(r)))
    nxt = pl.BlockSpec((rows, width), lambda r, i: (jnp.minimum((i + 1) * per, last), cols_of(r)))
    return prev, nxt


def _attn_fwd(g, dil, qv, kv, vv, gain, tables):
    L = qv.shape[0]
    T = min(1024, L)
    n_sub = T // QB

    def body(q_ref, km_ref, kp_ref, kn_ref, vm_ref, vp_ref, vn_ref, gain_ref, bias_ref, o_ref, lse_ref,
             kwin, vwin, s_scr, p_scr):
        i = pl.program_id(1)
        kwin[0:HALF] = kp_ref[...]
        kwin[HALF:HALF + T] = km_ref[...]
        kwin[HALF + T:] = kn_ref[...]
        vwin[0:HALF] = vp_ref[...]
        vwin[HALF:HALF + T] = vm_ref[...]
        vwin[HALF + T:] = vn_ref[...]
        lo = _lane((QB, LANES)) < HEAD_DIM
        krow = lax.broadcasted_iota(jnp.int32, (KB, 1), 0)
        chan_lo = lax.broadcasted_iota(jnp.int32, (LANES, 1), 0) < HEAD_DIM
        hrow = lax.broadcasted_iota(jnp.int32, (LANES, QB), 0)

        def sub(j, carry):
            r0 = pl.multiple_of(j * QB, QB)
            kpos = i * T + j * QB - HALF + krow
            kvalid = (kpos >= 0) & (kpos < L)
            lse_rows = jnp.zeros((LANES, QB), F32)
            cols = [slice(p * LANES, (p + 1) * LANES) for p in range(N_PAIRS)]
            for p, cs in enumerate(cols):
                qf = q_ref[pl.ds(r0, QB), cs].astype(F32) * gain_ref[:, cs]
                q2 = jnp.concatenate([jnp.where(lo, qf, 0.0), jnp.where(lo, 0.0, qf)], axis=0).astype(BF16)
                s_scr[p] = lax.dot_general(kwin[pl.ds(r0, KB), cs], q2, (((1,), (1,)), ((), ())),
                                           preferred_element_type=F32)
            inv_l = []
            for p in range(N_PAIRS):
                bias2 = jnp.concatenate([bias_ref[0, 2 * p], bias_ref[0, 2 * p + 1]], axis=1)
                logits = jnp.where(kvalid, s_scr[p] + bias2, NEG)
                m = jnp.max(logits, axis=0, keepdims=True)
                pt = jnp.exp(logits - m)
                l = jnp.sum(pt, axis=0, keepdims=True)
                p_scr[p] = pt.astype(BF16)
                inv_l.append(1.0 / l)
                lse2 = m + jnp.log(l)
                lse_rows = jnp.where(hrow == 2 * p, lse2[:, 0:QB], lse_rows)
                lse_rows = jnp.where(hrow == 2 * p + 1, lse2[:, QB:2 * QB], lse_rows)
            for p, cs in enumerate(cols):
                s_scr[p, 0:LANES, :] = lax.dot_general(vwin[pl.ds(r0, KB), cs], p_scr[p], (((0,), (0,)), ((), ())),
                                                       preferred_element_type=F32) * inv_l[p]
            for p, cs in enumerate(cols):
                ot2 = s_scr[p, 0:LANES, :]
                ot = jnp.where(chan_lo, ot2[:, 0:QB], ot2[:, QB:2 * QB])
                o_ref[pl.ds(r0, QB), cs] = ot.T.astype(BF16)
            lse_ref[pl.ds(r0, QB), :] = lse_rows.T
            return carry

        lax.fori_loop(0, n_sub, sub, 0)

    main = pl.BlockSpec((T, ATTN_W), lambda r, i: (i, r))
    prev, nxt = _halo_specs(ATTN_W, HALF, T, L, lambda r: r)
    o_g, lse_g = pl.pallas_call(
        body, name=f"attn_fwd_d{dil}", grid=(dil, L // T),
        out_shape=(jax.ShapeDtypeStruct((L, dil * ATTN_W), BF16), jax.ShapeDtypeStruct((L, dil * LANES), F32)),
        in_specs=[main, main, prev, nxt, main, prev, nxt,
                  pl.BlockSpec((1, ATTN_W), lambda r, i: (0, 0)),
                  pl.BlockSpec((1, N_HEADS, KB, QB), lambda r, i: (g, 0, 0, 0))],
        out_specs=(main, pl.BlockSpec((T, LANES), lambda r, i: (i, r))),
        scratch_shapes=[pltpu.VMEM((T + 2 * HALF, ATTN_W), BF16), pltpu.VMEM((T + 2 * HALF, ATTN_W), BF16),
                        pltpu.VMEM((N_PAIRS, KB, 2 * QB), F32), pltpu.VMEM((N_PAIRS, KB, 2 * QB), BF16)],
        compiler_params=_params(("arbitrary", "arbitrary"), 40),
    )(qv, kv, kv, kv, vv, vv, vv, gain, tables)
    return o_g, lse_g


def _row_halo_specs(width, tm, S, col=0):
    rows = 16
    per = tm // rows
    last = S // rows - 1
    prev = pl.BlockSpec((rows, width), lambda i: (jnp.maximum(i * per - 1, 0), col))
    nxt = pl.BlockSpec((rows, width), lambda i: (jnp.minimum((i + 1) * per, last), col))
    return prev, nxt


def _shift_rows(a, prev_row, next_row, tm):
    row = lax.broadcasted_iota(jnp.int32, a.shape, 0)
    a_m1 = jnp.where(row == 0, prev_row, pltpu.roll(a, 1, 0))
    a_p1 = jnp.where(row == tm - 1, next_row, pltpu.roll(a, tm - 1, 0))
    return a_m1, a_p1


def _combine_gates(o_parts, lse_parts, u, gb, gc, zc, za, conv_w, conv_b):
    S = u.shape[0]
    tm = 512
    n_t = S // tm

    def body(o1, o2, o3, l1, l2, l3, u_ref, up_ref, un_ref, gb_ref, gc_ref, gcp_ref, gcn_ref, zc_ref, za_ref,
             cw_ref, cb_ref, y_ref, o_ref, lse_ref, lse2_ref, lse3_ref, so2, so3, sl2, sl3, slse):
        i = pl.program_id(0)
        lane = _lane((tm, LANES))
        lo = lane < HEAD_DIM
        for blk, scr, dil in ((o2, so2, DILATIONS[1]), (o3, so3, DILATIONS[2]),
                              (l2, sl2, DILATIONS[1]), (l3, sl3, DILATIONS[2])):
            _from_view(blk, scr, dil)
        ls = [l1[...], sl2[0], sl3[0]]
        lmax = jnp.maximum(jnp.maximum(ls[0], ls[1]), ls[2])
        es = [jnp.exp(l - lmax) for l in ls]
        den = es[0] + es[1] + es[2]
        slse[0] = lmax + jnp.log(den)
        lse_ref[...] = slse[0]
        _to_view(slse, lse2_ref, DILATIONS[1], F32)
        _to_view(slse, lse3_ref, DILATIONS[2], F32)
        inv = 1.0 / den
        ws = [e * inv for e in es]
        for p in range(N_PAIRS):
            cs = slice(p * LANES, (p + 1) * LANES)
            acc = jnp.zeros((tm, LANES), F32)
            for w, o_g in zip(ws, (o1[:, cs].astype(F32), so2[p], so3[p])):
                wp = jnp.where(lo, w[:, 2 * p:2 * p + 1], w[:, 2 * p + 1:2 * p + 2])
                acc = acc + wp * o_g
            o_ref[:, cs] = acc.astype(BF16)
            za = za_ref[:, cs].astype(F32)
            y_ref[:, CONV_W + p * LANES:CONV_W + (p + 1) * LANES] = (acc * za * _sigmoid(za)).astype(BF16)
        a = gc_ref[...].astype(F32) * u_ref[...].astype(F32)
        a_prev = gcp_ref[15:16, :].astype(F32) * up_ref[15:16, :].astype(F32) * (i > 0).astype(F32)
        a_next = gcn_ref[0:1, :].astype(F32) * un_ref[0:1, :].astype(F32) * (i < n_t - 1).astype(F32)
        a_m1, a_p1 = _shift_rows(a, a_prev, a_next, tm)
        z3 = cw_ref[0:1, :] * a_m1 + cw_ref[1:2, :] * a + cw_ref[2:3, :] * a_p1
        zc = zc_ref[...].astype(F32)
        y_ref[:, 0:CONV_W] = (gb_ref[...].astype(F32) * (z3 + cb_ref[...]) * zc * _sigmoid(zc)).astype(BF16)

    pspec = pl.BlockSpec((tm, CONV_W), lambda i: (i, 0))
    hp, hn = _row_halo_specs(CONV_W, tm, S)
    wide_specs = [_view_spec(tm, d, ATTN_W) for d in DILATIONS]
    stat_specs = [_view_spec(tm, d, LANES) for d in DILATIONS]
    outs = pl.pallas_call(
        body, name="combine_gates", grid=(n_t,),
        out_shape=[jax.ShapeDtypeStruct((S, D_MODEL), BF16), jax.ShapeDtypeStruct((S, ATTN_W), BF16)]
        + [_view_shape(S, d, LANES, F32) for d in DILATIONS],
        in_specs=wide_specs + stat_specs + [pspec, hp, hn, pspec, pspec, hp, hn, pspec, pspec,
                                            pl.BlockSpec((3, CONV_W), lambda i: (0, 0)),
                                            pl.BlockSpec((1, CONV_W), lambda i: (0, 0))],
        out_specs=[pl.BlockSpec((tm, D_MODEL), lambda i: (i, 0)), pspec] + stat_specs,
        scratch_shapes=[_tile_scratch(tm, ATTN_W), _tile_scratch(tm, ATTN_W),
                        _tile_scratch(tm, LANES), _tile_scratch(tm, LANES), _tile_scratch(tm, LANES)],
        compiler_params=_params(("arbitrary",), 40),
    )(*o_parts, *lse_parts, u, u, u, gb, gc, gc, gc, zc, za, conv_w, conv_b.reshape(1, CONV_W))
    return outs[0], outs[1], outs[2:]


def _out_proj(y, x, target, wo):
    S = x.shape[0]
    tm = 512

    def body(y_ref, x_ref, t_ref, wo_ref, g_ref, dy_ref, dwo_ref, loss_ref):
        i = pl.program_id(0)

        @pl.when(i == 0)
        def _():
            dwo_ref[...] = jnp.zeros_like(dwo_ref)
            loss_ref[...] = jnp.zeros_like(loss_ref)

        yv = y_ref[...]
        wo_v = wo_ref[...]
        err = x_ref[...] + jnp.dot(yv, wo_v, preferred_element_type=F32) - t_ref[...]
        e2 = (err * err).reshape(tm // 8, 8, D_MODEL).sum(axis=0)
        part = e2[:, 0:LANES]
        for k in range(1, D_MODEL // LANES):
            part = part + e2[:, k * LANES:(k + 1) * LANES]
        loss_ref[...] += part
        gv = (err * (1.0 / D_MODEL)).astype(BF16)
        g_ref[...] = gv
        dy_ref[...] = lax.dot_general(gv, wo_v, (((1,), (1,)), ((), ())), preferred_element_type=F32).astype(BF16)
        dwo_ref[...] += lax.dot_general(yv, gv, (((0,), (0,)), ((), ())), preferred_element_type=F32)

    tile = pl.BlockSpec((tm, D_MODEL), lambda i: (i, 0))
    return pl.pallas_call(
        body, name="out_proj", grid=(S // tm,),
        out_shape=(jax.ShapeDtypeStruct((S, D_MODEL), BF16), jax.ShapeDtypeStruct((S, D_MODEL), BF16),
                   jax.ShapeDtypeStruct((D_MODEL, D_MODEL), F32), jax.ShapeDtypeStruct((8, LANES), F32)),
        in_specs=[tile, tile, tile, pl.BlockSpec((D_MODEL, D_MODEL), lambda i: (0, 0), pipeline_mode=pl.Buffered(1))],
        out_specs=(tile, tile, pl.BlockSpec((D_MODEL, D_MODEL), lambda i: (0, 0)),
                   pl.BlockSpec((8, LANES), lambda i: (0, 0))),
        compiler_params=_params(("arbitrary",), 48),
    )(y, x, target, wo)


def _gates_bwd(dy, u, gb, gc, zc, za, o, conv_w, conv_b):
    S = u.shape[0]
    tm = 512
    n_t = S // tm

    def body(dy_ref, dyp_ref, dyn_ref, u_ref, up_ref, un_ref, gb_ref, gbp_ref, gbn_ref, gc_ref, gcp_ref, gcn_ref,
             zc_ref, zcp_ref, zcn_ref, za_ref, o_ref, cw_ref, cb_ref,
             du_ref, dgb_ref, dgc_ref, dzc_ref, dza_ref, small_ref,
             do_ref, do2_ref, do3_ref, delta_ref, delta2_ref, delta3_ref, sdo, sdelta):
        i = pl.program_id(0)

        @pl.when(i == 0)
        def _():
            small_ref[...] = jnp.zeros_like(small_ref)

        def dz3_of(dyc, zc, gb):
            return dyc * zc * _sigmoid(zc) * gb

        f = lambda ref, rows=slice(None): ref[rows, :].astype(F32)
        has_prev = (i > 0).astype(F32)
        has_next = (i < n_t - 1).astype(F32)
        pr, nr = slice(15, 16), slice(0, 1)
        uv, gbv, gcv, zcv = f(u_ref), f(gb_ref), f(gc_ref), f(zc_ref)
        dyc = dy_ref[:, 0:CONV_W].astype(F32)
        a = gcv * uv
        a_m1, a_p1 = _shift_rows(a, f(gcp_ref, pr) * f(up_ref, pr) * has_prev,
                                 f(gcn_ref, nr) * f(un_ref, nr) * has_next, tm)
        w0, w1, w2 = cw_ref[0:1, :], cw_ref[1:2, :], cw_ref[2:3, :]
        cvp = w0 * a_m1 + w1 * a + w2 * a_p1 + cb_ref[...]
        sz = _sigmoid(zcv)
        dcv = dyc * zcv * sz
        dzc_ref[...] = (dyc * gbv * cvp * (sz * (1.0 + zcv * (1.0 - sz)))).astype(BF16)
        dgb_ref[...] = (dcv * cvp).astype(BF16)
        dz3 = dcv * gbv
        dz3_prev = dz3_of(f(dyp_ref, pr), f(zcp_ref, pr), f(gbp_ref, pr)) * has_prev
        dz3_next = dz3_of(f(dyn_ref, nr), f(zcn_ref, nr), f(gbn_ref, nr)) * has_next
        dz3_m1, dz3_p1 = _shift_rows(dz3, dz3_prev, dz3_next, tm)
        da = w0 * dz3_p1 + w1 * dz3 + w2 * dz3_m1
        du_ref[...] = (da * gcv).astype(BF16)
        dgc_ref[...] = (da * uv).astype(BF16)
        csum = lambda t: jnp.sum(t, axis=0, keepdims=True)
        small_ref[0:1, :] += csum(dz3 * a_m1)
        small_ref[1:2, :] += csum(dz3 * a)
        small_ref[2:3, :] += csum(dz3 * a_p1)
        small_ref[3:4, :] += csum(dz3)

        lane = _lane((tm, LANES))
        lo = lane < HEAD_DIM
        delta = jnp.zeros((tm, LANES), F32)
        for p in range(N_PAIRS):
            cs = slice(p * LANES, (p + 1) * LANES)
            dya = dy_ref[:, CONV_W + p * LANES:CONV_W + (p + 1) * LANES].astype(F32)
            zav = za_ref[:, cs].astype(F32)
            ov = o_ref[:, cs].astype(F32)
            sa = _sigmoid(zav)
            d_o = dya * zav * sa
            sdo[p] = d_o
            do_ref[:, cs] = d_o.astype(BF16)
            dza_ref[:, cs] = (dya * ov * (sa * (1.0 + zav * (1.0 - sa)))).astype(BF16)
            s_lo, s_hi = _head_sums(d_o * ov, lo)
            delta = jnp.where(lane == 2 * p, s_lo, delta)
            delta = jnp.where(lane == 2 * p + 1, s_hi, delta)
        sdelta[0] = delta
        delta_ref[...] = delta
        for dil, wide_ref, stat_ref in zip(VIEW_DILATIONS, (do2_ref, do3_ref), (delta2_ref, delta3_ref)):
            _to_view(sdo, wide_ref, dil, BF16)
            _to_view(sdelta, stat_ref, dil, F32)

    pspec = pl.BlockSpec((tm, CONV_W), lambda i: (i, 0))
    hp, hn = _row_halo_specs(CONV_W, tm, S)
    piece = jax.ShapeDtypeStruct((S, CONV_W), BF16)
    outs = pl.pallas_call(
        body, name="gates_bwd", grid=(n_t,),
        out_shape=[piece] * 5 + [jax.ShapeDtypeStruct((8, CONV_W), F32)]
        + [_view_shape(S, d, ATTN_W, BF16) for d in DILATIONS] + [_view_shape(S, d, LANES, F32) for d in DILATIONS],
        in_specs=[pl.BlockSpec((tm, D_MODEL), lambda i: (i, 0)), hp, hn,
                  pspec, hp, hn, pspec, hp, hn, pspec, hp, hn, pspec, hp, hn, pspec, pspec,
                  pl.BlockSpec((3, CONV_W), lambda i: (0, 0)), pl.BlockSpec((1, CONV_W), lambda i: (0, 0))],
        out_specs=[pspec] * 5 + [pl.BlockSpec((8, CONV_W), lambda i: (0, 0))]
        + [_view_spec(tm, d, ATTN_W) for d in DILATIONS] + [_view_spec(tm, d, LANES) for d in DILATIONS],
        scratch_shapes=[_tile_scratch(tm, ATTN_W), _tile_scratch(tm, LANES)],
        compiler_params=_params(("arbitrary",), 40),
    )(dy, dy, dy, u, u, u, gb, gb, gb, gc, gc, gc, zc, zc, zc, za, o, conv_w, conv_b.reshape(1, CONV_W))
    return outs[:5], outs[5], outs[6:9], outs[9:12]


def _attn_bwd(g, dil, qv, kv, vv, dov, lv, dv_, rqv, rkv, gain, tables):
    L = qv.shape[0]
    T = min(1024, L)
    n_sub = T // QB + 1
    QW = T + 2 * HALF
    KW = T + 2 * QB

    def body(qm_ref, qp_ref, qn_ref, dom_ref, dop_ref, don_ref, km_ref, kp_ref, kn_ref, vm_ref, vp_ref, vn_ref,
             lm_ref, lp_ref, ln_ref, dm_ref, dp_ref, dn_ref, rqm_ref, rqp_ref, rqn_ref, rk_ref,
             gain_ref, bias_ref,
             dq_ref, dk_ref, dv_ref, ds_ref, a_ref,
             qwin, dowin, kwin, vwin, lwin, dwin, rqwin, dqwin, dkacc, dvacc,
             s_scr, d_scr, p_scr, ds_scr, q2_scr, do2_scr):
        r = pl.program_id(0)
        i = pl.program_id(1)

        @pl.when((r == 0) & (i == 0))
        def _():
            ds_ref[...] = jnp.zeros_like(ds_ref)
            a_ref[...] = jnp.zeros_like(a_ref)

        for win, (pr, mn, nx), h_rows in ((qwin, (qp_ref, qm_ref, qn_ref), HALF),
                                          (dowin, (dop_ref, dom_ref, don_ref), HALF),
                                          (lwin, (lp_ref, lm_ref, ln_ref), HALF),
                                          (dwin, (dp_ref, dm_ref, dn_ref), HALF),
                                          (rqwin, (rqp_ref, rqm_ref, rqn_ref), HALF),
                                          (kwin, (kp_ref, km_ref, kn_ref), QB),
                                          (vwin, (vp_ref, vm_ref, vn_ref), QB)):
            win[0:h_rows] = pr[...]
            win[h_rows:h_rows + T] = mn[...]
            win[h_rows + T:] = nx[...]
        dkacc[...] = jnp.zeros_like(dkacc)
        dvacc[...] = jnp.zeros_like(dvacc)

        lo = _lane((QB, LANES)) < HEAD_DIM
        krow = lax.broadcasted_iota(jnp.int32, (KB, 1), 0)
        qcol = _lane((1, 2 * QB)) % QB
        row = lax.broadcasted_iota(jnp.int32, (QB, 1), 0)
        chan_lo = lax.broadcasted_iota(jnp.int32, (LANES, 1), 0) < HEAD_DIM

        def sub(j, carry):
            r0 = pl.multiple_of(j * QB, QB)
            kpos = i * T - QB + j * QB + krow
            qpos = i * T - HALF + j * QB + qcol
            valid = (kpos >= 0) & (kpos < L) & (qpos >= 0) & (qpos < L)
            owned_t = ((qpos >= i * T) & (qpos < i * T + T)).astype(F32)
            qpos_c = i * T - HALF + j * QB + row
            owned = ((qpos_c >= i * T) & (qpos_c < i * T + T)).astype(F32)
            lse_rows = lwin[pl.ds(r0, QB), :].T
            delta_rows = dwin[pl.ds(r0, QB), :].T
            rq_t = rqwin[pl.ds(r0, QB), :]
            cols = [slice(p * LANES, (p + 1) * LANES) for p in range(N_PAIRS)]
            nt = (((1,), (1,)), ((), ()))
            for p, cs in enumerate(cols):
                qf = qwin[pl.ds(r0, QB), cs].astype(F32) * gain_ref[:, cs]
                q2_scr[p] = jnp.concatenate([jnp.where(lo, qf, 0.0), jnp.where(lo, 0.0, qf)], axis=0).astype(BF16)
                dov = dowin[pl.ds(r0, QB), cs]
                zero = jnp.zeros_like(dov)
                do2_scr[p] = jnp.concatenate([jnp.where(lo, dov, zero), jnp.where(lo, zero, dov)], axis=0)
                s_scr[p] = lax.dot_general(kwin[pl.ds(r0, KB), cs], q2_scr[p], nt, preferred_element_type=F32)
                d_scr[p] = lax.dot_general(vwin[pl.ds(r0, KB), cs], do2_scr[p], nt, preferred_element_type=F32)
            for p in range(N_PAIRS):
                bias2 = jnp.concatenate([bias_ref[0, 2 * p], bias_ref[0, 2 * p + 1]], axis=1)
                lse2 = jnp.concatenate([lse_rows[2 * p:2 * p + 1, :], lse_rows[2 * p + 1:2 * p + 2, :]], axis=1)
                delta2 = jnp.concatenate([delta_rows[2 * p:2 * p + 1, :], delta_rows[2 * p + 1:2 * p + 2, :]], axis=1)
                pt = jnp.where(valid, jnp.exp(s_scr[p] + bias2 - lse2), 0.0)
                dst = pt * (d_scr[p] - delta2)
                ds_ref[p] += dst * owned_t
                p_scr[p] = pt.astype(BF16)
                ds_scr[p] = dst.astype(BF16)
            for p, cs in enumerate(cols):
                dvacc[pl.ds(r0, KB), cs] += jnp.dot(p_scr[p], do2_scr[p], preferred_element_type=F32)
                dkacc[pl.ds(r0, KB), cs] += jnp.dot(ds_scr[p], q2_scr[p], preferred_element_type=F32)
                s_scr[p, 0:LANES, :] = lax.dot_general(kwin[pl.ds(r0, KB), cs], ds_scr[p], (((0,), (0,)), ((), ())),
                                                       preferred_element_type=F32)
            for p, cs in enumerate(cols):
                g2 = gain_ref[:, cs]
                q_f = qwin[pl.ds(r0, QB), cs].astype(F32)
                dqt2 = s_scr[p, 0:LANES, :]
                dqm2 = jnp.where(chan_lo, dqt2[:, 0:QB], dqt2[:, QB:2 * QB]).T
                a_ref[p] += (dqm2 * q_f * owned).reshape(QB // 8, 8, LANES).sum(axis=0)
                dqh = dqm2 * g2
                c_lo, c_hi = _head_sums(dqh * q_f, lo)
                mean_c = jnp.where(lo, c_lo, c_hi) * (1.0 / HEAD_DIM)
                rq2 = jnp.where(lo, rq_t[:, 2 * p:2 * p + 1], rq_t[:, 2 * p + 1:2 * p + 2])
                dqwin[pl.ds(r0, QB), cs] = (rq2 * (dqh - q_f * mean_c)).astype(BF16)
            return carry

        lax.fori_loop(0, n_sub, sub, 0)
        dq_ref[...] = dqwin[HALF:HALF + T]

        def finish(j, carry):
            r0 = pl.multiple_of(j * QB, QB)
            rk_t = rk_ref[pl.ds(r0, QB), :]
            for p in range(N_PAIRS):
                cs = slice(p * LANES, (p + 1) * LANES)
                dkh = dkacc[pl.ds(QB + r0, QB), cs]
                k_f = km_ref[pl.ds(r0, QB), cs].astype(F32)
                c_lo, c_hi = _head_sums(dkh * k_f, lo)
                mean_c = jnp.where(lo, c_lo, c_hi) * (1.0 / HEAD_DIM)
                rk2 = jnp.where(lo, rk_t[:, 2 * p:2 * p + 1], rk_t[:, 2 * p + 1:2 * p + 2])
                dk_ref[pl.ds(r0, QB), cs] = (rk2 * (dkh - k_f * mean_c)).astype(BF16)
                dv_ref[pl.ds(r0, QB), cs] = dvacc[pl.ds(QB + r0, QB), cs].astype(BF16)
            return carry

        lax.fori_loop(0, T // QB, finish, 0)

    main = pl.BlockSpec((T, ATTN_W), lambda r, i: (i, r))
    smain = pl.BlockSpec((T, LANES), lambda r, i: (i, r))
    q_prev, q_next = _halo_specs(ATTN_W, HALF, T, L, lambda r: r)
    k_prev, k_next = _halo_specs(ATTN_W, QB, T, L, lambda r: r)
    s_prev, s_next = _halo_specs(LANES, HALF, T, L, lambda r: r)
    piece = jax.ShapeDtypeStruct((L, dil * ATTN_W), BF16)
    return pl.pallas_call(
        body, name=f"attn_bwd_d{dil}", grid=(dil, L // T),
        out_shape=(piece, piece, piece, jax.ShapeDtypeStruct((N_PAIRS, KB, 2 * QB), F32),
                   jax.ShapeDtypeStruct((N_PAIRS, 8, LANES), F32)),
        in_specs=[main, q_prev, q_next, main, q_prev, q_next, main, k_prev, k_next, main, k_prev, k_next,
                  smain, s_prev, s_next, smain, s_prev, s_next, smain, s_prev, s_next, smain,
                  pl.BlockSpec((1, ATTN_W), lambda r, i: (0, 0)),
                  pl.BlockSpec((1, N_HEADS, KB, QB), lambda r, i: (g, 0, 0, 0))],
        out_specs=(main, main, main, pl.BlockSpec((N_PAIRS, KB, 2 * QB), lambda r, i: (0, 0, 0)),
                   pl.BlockSpec((N_PAIRS, 8, LANES), lambda r, i: (0, 0, 0))),
        scratch_shapes=[pltpu.VMEM((QW, ATTN_W), BF16), pltpu.VMEM((QW, ATTN_W), BF16),
                        pltpu.VMEM((KW, ATTN_W), BF16), pltpu.VMEM((KW, ATTN_W), BF16),
                        pltpu.VMEM((QW, LANES), F32), pltpu.VMEM((QW, LANES), F32), pltpu.VMEM((QW, LANES), F32),
                        pltpu.VMEM((QW, ATTN_W), BF16),
                        pltpu.VMEM((KW, ATTN_W), F32), pltpu.VMEM((KW, ATTN_W), F32),
                        pltpu.VMEM((N_PAIRS, KB, 2 * QB), F32), pltpu.VMEM((N_PAIRS, KB, 2 * QB), F32),
                        pltpu.VMEM((N_PAIRS, KB, 2 * QB), BF16), pltpu.VMEM((N_PAIRS, KB, 2 * QB), BF16),
                        pltpu.VMEM((N_PAIRS, 2 * QB, LANES), BF16), pltpu.VMEM((N_PAIRS, 2 * QB, LANES), BF16)],
        compiler_params=_params(("arbitrary", "arbitrary"), 56),
    )(qv, qv, qv, dov, dov, dov, kv, kv, kv, vv, vv, vv, lv, lv, lv, dv_, dv_, dv_, rqv, rqv, rqv, rkv, gain, tables)


def _proj_bwd(x, d_out, norm_w, wg, pieces):
    S = x.shape[0]
    tm = 512
    flat = [share for p in pieces for share in p]
    n_t = S // tm

    def body(*refs):
        x_ref, g_ref, nw_ref, wg_ref = refs[:4]
        piece_refs = refs[4:4 + len(flat)]
        gx_ref, dnw_ref, ht_ref, dproj_ref, sview = refs[4 + len(flat):]
        i = pl.program_id(0)

        @pl.when(i == 0)
        def _():
            dnw_ref[...] = jnp.zeros_like(dnw_ref)

        xv = x_ref[...]
        r = lax.rsqrt(jnp.mean(xv * xv, axis=-1, keepdims=True) + EPS)
        xh = xv * r
        nw = nw_ref[...]
        ht_ref[...] = (xh * nw).T.astype(BF16)
        dh = jnp.zeros((tm, D_MODEL), F32)
        k = 0
        for j in range(8):
            shares = pieces[j]
            if len(shares) == 1:
                dp = piece_refs[k][...]
            else:
                dp = jnp.zeros((tm, CONV_W), F32)
                for t, (_, dil) in enumerate(shares):
                    if dil == 1:
                        dp = dp + piece_refs[k + t][...].astype(F32)
                    else:
                        _from_view(piece_refs[k + t], sview, dil)
                        dp = dp + _scratch_value(sview)
                dp = dp.astype(BF16)
            k += len(shares)
            dproj_ref[j] = dp
            dh = dh + lax.dot_general(dp, wg_ref[j], (((1,), (1,)), ((), ())), preferred_element_type=F32)
        dnw_ref[...] += (dh * xh).reshape(tm // 8, 8, D_MODEL).sum(axis=0)
        dxh = dh * nw
        mean_c = jnp.mean(dxh * xh, axis=-1, keepdims=True)
        gx_ref[...] = g_ref[...].astype(F32) + r * (dxh - xh * mean_c)

    tile = pl.BlockSpec((tm, D_MODEL), lambda i: (i, 0))
    return pl.pallas_call(
        body, name="proj_bwd", grid=(n_t,),
        out_shape=(jax.ShapeDtypeStruct((S, D_MODEL), F32), jax.ShapeDtypeStruct((8, D_MODEL), F32),
                   jax.ShapeDtypeStruct((D_MODEL, S), BF16), jax.ShapeDtypeStruct((8, S, CONV_W), BF16)),
        in_specs=[tile, tile, pl.BlockSpec((1, D_MODEL), lambda i: (0, 0)),
                  pl.BlockSpec(wg.shape, lambda i: (0, 0, 0), pipeline_mode=pl.Buffered(1))]
        + [_view_spec(tm, dil, CONV_W) for _, dil in flat],
        out_specs=(tile, pl.BlockSpec((8, D_MODEL), lambda i: (0, 0)),
                   pl.BlockSpec((D_MODEL, tm), lambda i: (0, i)), pl.BlockSpec((8, tm, CONV_W), lambda i: (0, i, 0))),
        scratch_shapes=[_tile_scratch(tm, CONV_W)],
        compiler_params=_params(("arbitrary",), 56),
    )(x, d_out, norm_w.reshape(1, D_MODEL), wg, *[a for a, _ in flat])


def _dw_exchange(ht, dproj, gw_out, small):
    D, S = ht.shape
    tk = 1024
    n_t = S // tk
    me_outer = 4 * lax.axis_index("x") + 2 * lax.axis_index("y") + lax.axis_index("c")
    order = ((me_outer + 1 + jnp.arange(N_DEV, dtype=jnp.int32)) % N_DEV).astype(jnp.int32)

    def body(order_ref, ht_ref, dp_ref, gout_ref, sm_ref, rin_ref, rout_ref, rsm_ref,
             acc, sbuf, in_send, in_recv, side_send, side_recv, local_sems):
        s = pl.program_id(0)
        t = pl.program_id(1)
        x, y, c = lax.axis_index("x"), lax.axis_index("y"), lax.axis_index("c")
        me = 4 * x + 2 * y + c

        def in_copy(k):
            to = (me + 1 + k) % N_DEV
            frm = (me + 2 * N_DEV - 1 - k) % N_DEV
            return pltpu.make_async_remote_copy(
                src_ref=sbuf.at[k % 2], dst_ref=rin_ref.at[me], send_sem=in_send.at[k], recv_sem=in_recv.at[k],
                device_id=(to // 4, (to // 2) % 2, to % 2), device_id_type=MESH), rin_ref.at[frm]

        def side_copies():
            local = [pltpu.make_async_copy(gout_ref.at[me], rout_ref.at[me], local_sems.at[0]),
                     pltpu.make_async_copy(sm_ref, rsm_ref.at[me], local_sems.at[1])]
            remote = []
            for k in range(1, N_DEV):
                px = 1 - x if k & 4 else x
                py = 1 - y if k & 2 else y
                pc = 1 - c if k & 1 else c
                peer = 4 * px + 2 * py + pc
                for a, (src, dst) in enumerate(((gout_ref.at[peer], rout_ref.at[me]), (sm_ref, rsm_ref.at[me]))):
                    remote.append(pltpu.make_async_remote_copy(
                        src_ref=src, dst_ref=dst, send_sem=side_send.at[a * 7 + k - 1],
                        recv_sem=side_recv.at[a * 7 + k - 1], device_id=(px, py, pc), device_id_type=MESH))
            return local, remote

        @pl.when((s == 0) & (t == 0))
        def _():
            local, remote = side_copies()
            for cp in local + remote:
                cp.start()

        @pl.when(t == 0)
        def _():
            acc[...] = jnp.zeros_like(acc)

        acc[...] += jnp.dot(ht_ref[...], dp_ref[0], preferred_element_type=F32)

        for k in range(N_DEV):
            @pl.when((s == k) & (t == n_t - 1))
            def _(k=k):
                if k >= 2:
                    in_copy(k - 2)[0].wait_send()
                sbuf[k % 2] = acc[...].astype(BF16)
                if k < N_DEV - 1:
                    in_copy(k)[0].start()
                else:
                    own = pltpu.make_async_copy(sbuf.at[k % 2], rin_ref.at[me], local_sems.at[2])
                    own.start()
                    in_copy(k - 1)[0].wait_send()
                    for j in range(N_DEV - 1):
                        cp, landed = in_copy(j)
                        pltpu.make_async_remote_copy(
                            src_ref=sbuf.at[0], dst_ref=landed, send_sem=in_send.at[j], recv_sem=in_recv.at[j],
                            device_id=(x, y, c), device_id_type=MESH).wait_recv()
                    local, remote = side_copies()
                    for cp in remote:
                        cp.wait_recv()
                    for cp in remote:
                        cp.wait_send()
                    for cp in local:
                        cp.wait()
                    own.wait()

    hbm = pl.BlockSpec(memory_space=pl.ANY)
    grid_spec = pltpu.PrefetchScalarGridSpec(
        num_scalar_prefetch=1, grid=(N_DEV, n_t),
        in_specs=[pl.BlockSpec((D, tk), lambda s, t, order_ref: (0, t)),
                  pl.BlockSpec((1, tk, CONV_W), lambda s, t, order_ref: (order_ref[s], t, 0)),
                  hbm, hbm],
        out_specs=(hbm, hbm, hbm),
        scratch_shapes=[pltpu.VMEM((D, CONV_W), F32), pltpu.VMEM((2, D, CONV_W), BF16),
                        pltpu.SemaphoreType.DMA((N_DEV - 1,)), pltpu.SemaphoreType.DMA((N_DEV - 1,)),
                        pltpu.SemaphoreType.DMA((14,)), pltpu.SemaphoreType.DMA((14,)),
                        pltpu.SemaphoreType.DMA((3,))])
    return pl.pallas_call(
        body, name="dw_exchange", grid_spec=grid_spec,
        out_shape=(jax.ShapeDtypeStruct((N_DEV, D, CONV_W), BF16), jax.ShapeDtypeStruct(gw_out.shape, F32),
                   jax.ShapeDtypeStruct((N_DEV,) + small.shape, F32)),
        compiler_params=_params(("arbitrary", "arbitrary"), 32),
    )(order, ht, dproj, gw_out, small)


def _adamw_math(w, g, m, v):
    m2 = ADAM_B1 * m + (1.0 - ADAM_B1) * g
    v2 = ADAM_B2 * v + (1.0 - ADAM_B2) * (g * g)
    m_hat = m2 / (1.0 - ADAM_B1 ** ADAM_STEP)
    v_hat = v2 / (1.0 - ADAM_B2 ** ADAM_STEP)
    delta = -ADAM_LR * (m_hat / (jnp.sqrt(v_hat) + ADAM_EPS) + ADAM_WD * w)
    return delta, m2, v2


def _adamw_sharded(name, parts, w, m, v, rows):
    R, C = w.shape

    def body(p_ref, w_ref, m_ref, v_ref, g_ref, d_ref, m2_ref, v2_ref):
        g = p_ref[0].astype(F32)
        for s in range(1, N_DEV):
            g = g + p_ref[s].astype(F32)
        g_ref[...] = g
        d_ref[...], m2_ref[...], v2_ref[...] = _adamw_math(w_ref[...], g, m_ref[...], v_ref[...])

    spec = pl.BlockSpec((rows, C), lambda i: (i, 0))
    out = jax.ShapeDtypeStruct((R, C), F32)
    return pl.pallas_call(
        body, name=name, grid=(R // rows,),
        out_shape=(out,) * 4,
        in_specs=[pl.BlockSpec((N_DEV, rows, C), lambda i: (0, i, 0)), spec, spec, spec],
        out_specs=(spec,) * 4,
        compiler_params=_params(("arbitrary",), 40),
    )(parts, w, m, v)


def _adamw_small(parts, w, m, v):
    R = w.shape[0]

    def body(p_ref, w_ref, m_ref, v_ref, g_ref, d_ref, m2_ref, v2_ref):
        g = p_ref[0]
        for s in range(1, N_DEV):
            g = g + p_ref[s]
        g_ref[...] = g
        d_ref[...], m2_ref[...], v2_ref[...] = _adamw_math(w_ref[...], g, m_ref[...], v_ref[...])

    vm = pl.BlockSpec(memory_space=pltpu.VMEM)
    out = jax.ShapeDtypeStruct((R, LANES), F32)
    return pl.pallas_call(
        body, name="adamw_small", out_shape=(out,) * 4, in_specs=[vm] * 4, out_specs=(vm,) * 4,
    )(parts, w, m, v)


def _qk_gain_grads(a_parts, q_norm_w, k_norm_w, bias_g):
    def body(a_ref, qw_ref, kw_ref, bg_ref, out_ref):
        tot = jnp.zeros((8, LANES), F32)
        for g in range(len(DILATIONS)):
            for p in range(N_PAIRS):
                tot = tot + a_ref[g, p]
        col = jnp.sum(tot, axis=0, keepdims=True)
        a64 = col[:, 0:HEAD_DIM] + col[:, HEAD_DIM:LANES]
        out_ref[...] = jnp.zeros_like(out_ref)
        out_ref[0:1, 0:HEAD_DIM] = LOGIT_SCALE * kw_ref[...] * a64
        out_ref[1:2, 0:HEAD_DIM] = LOGIT_SCALE * qw_ref[...] * a64
        for h in range(N_HEADS):
            r0, c0 = 2 + h // 4, (h % 4) * N_BUCKETS
            out_ref[r0:r0 + 1, c0:c0 + N_BUCKETS] = bg_ref[h // 2, h % 2:h % 2 + 1, 0:N_BUCKETS]

    vm = pl.BlockSpec(memory_space=pltpu.VMEM)
    return pl.pallas_call(
        body, name="qk_gain_grads", out_shape=jax.ShapeDtypeStruct((8, LANES), F32),
        in_specs=[vm] * 4, out_specs=vm,
    )(a_parts, q_norm_w.reshape(1, HEAD_DIM), k_norm_w.reshape(1, HEAD_DIM), bias_g)


SMALL_ROWS = 32


def _pack_small(norm_w, conv_b, q_norm_w, k_norm_w, rel_bias, conv_w_full):
    pad64 = lambda a: jnp.pad(a, (0, LANES - HEAD_DIM)).reshape(1, LANES)
    return jnp.concatenate([
        norm_w.reshape(8, LANES), conv_b.reshape(4, LANES), pad64(q_norm_w), pad64(k_norm_w),
        rel_bias.T.reshape(2, LANES), conv_w_full.reshape(12, LANES), jnp.zeros((4, LANES), F32)], axis=0)


def _unpack_small(s):
    return (s[0:8].reshape(D_MODEL), s[8:12].reshape(CONV_W), s[12, 0:HEAD_DIM], s[13, 0:HEAD_DIM],
            s[14:16].reshape(N_HEADS, N_BUCKETS).T, s[16:28].reshape(3, CONV_W))


def kernel(x, norm_w, w_in, conv_w, conv_b, q_norm_w, k_norm_w, rel_bias, w_out, loss_target, m_norm_w, m_w_in, m_conv_w, m_conv_b, m_q_norm_w, m_k_norm_w, m_rel_bias, m_w_out, v_norm_w, v_w_in, v_conv_w, v_conv_b, v_q_norm_w, v_k_norm_w, v_rel_bias, v_w_out):
    S = x.shape[1]
    x2 = x.reshape(S, D_MODEL)
    tgt = loss_target.reshape(S, D_MODEL)
    me = 4 * lax.axis_index("x") + 2 * lax.axis_index("y") + lax.axis_index("c")
    shard_w = CONV_W // N_DEV

    cw_pad = jnp.pad(conv_w, ((0, 5), (0, LANES - shard_w)))
    wg, wo_g, cw_g = _ag_weights(w_in, w_out, cw_pad)
    wo = wo_g.reshape(D_MODEL, D_MODEL)
    conv_w_full = cw_g[:, 0:3, 0:shard_w].transpose(1, 0, 2).reshape(3, CONV_W)

    bmat_t = jnp.asarray(np.stack([_bucket_matrix(d).T for d in DILATIONS]))
    tables = _bias_tables(rel_bias, bmat_t)
    gain = jnp.tile(q_norm_w * k_norm_w * LOGIT_SCALE, N_HEADS).reshape(1, ATTN_W)

    u, gb, gc, zc, za, q_hat, k_hat, v, rq, rk = _fwd_proj(x2, norm_w, wg)
    o_parts, lse_parts = [], []
    for g, dil in enumerate(DILATIONS):
        o_g, lse_g = _attn_fwd(g, dil, q_hat[g], k_hat[g], v[g], gain, tables)
        o_parts.append(o_g)
        lse_parts.append(lse_g)
    y, o, lse = _combine_gates(o_parts, lse_parts, u, gb, gc, zc, za, conv_w_full, conv_b)
    d_out, dy, gwo_part, loss_part = _out_proj(y, x2, tgt, wo)
    loss = lax.psum(jnp.sum(loss_part) * (0.5 / D_MODEL), ("x", "y", "c"))

    (du, dgb, dgc, dzc, dza), conv_small, d_o, delta = _gates_bwd(dy, u, gb, gc, zc, za, o, conv_w_full, conv_b)
    dqs, dks, dvs, ds_parts, a_parts = [], [], [], [], []
    for g, dil in enumerate(DILATIONS):
        dq, dk, dv, ds_sum, a_sum = _attn_bwd(g, dil, q_hat[g], k_hat[g], v[g], d_o[g], lse[g], delta[g], rq[g], rk[g],
                                              gain, tables)
        dqs.append((dq, dil))
        dks.append((dk, dil))
        dvs.append((dv, dil))
        ds_parts.append(ds_sum)
        a_parts.append(a_sum)
    grad_x, gnw_part, h_t, dproj = _proj_bwd(
        x2, d_out, norm_w, wg, [[(du, 1)], [(dgb, 1)], [(dgc, 1)], [(dzc, 1)], dqs, dks, dvs, [(dza, 1)]])
    bias_g = _bias_grad(jnp.stack(ds_parts), bmat_t)
    qk_small = _qk_gain_grads(jnp.stack(a_parts), q_norm_w, k_norm_w, bias_g)

    small_part = jnp.concatenate([
        gnw_part.sum(axis=0).reshape(8, LANES), conv_small[3].reshape(4, LANES), qk_small[0:4],
        conv_small[0:3].reshape(12, LANES), jnp.zeros((4, LANES), F32)], axis=0)
    r_in, r_out, r_small = _dw_exchange(h_t, dproj, gwo_part.reshape(N_DEV, D_MODEL // N_DEV, D_MODEL), small_part)

    g_win, d_win, m_win, v_win = _adamw_sharded("adamw_w_in", r_in, w_in, m_w_in, v_w_in, 128)
    g_wo, d_wo, m_wo, v_wo = _adamw_sharded("adamw_w_out", r_out, w_out, m_w_out, v_w_out, 32)

    def full_conv(a):
        return jnp.zeros((3, CONV_W), F32).at[:, 0:shard_w].set(a)

    packs = [_pack_small(nw_, cb_, qw_, kw_, rb_, full_conv(cw_)) for nw_, cb_, qw_, kw_, rb_, cw_ in (
        (norm_w, conv_b, q_norm_w, k_norm_w, rel_bias, conv_w),
        (m_norm_w, m_conv_b, m_q_norm_w, m_k_norm_w, m_rel_bias, m_conv_w),
        (v_norm_w, v_conv_b, v_q_norm_w, v_k_norm_w, v_rel_bias, v_conv_w))]
    r_small_mine = r_small.at[:, 16:28, :].set(
        jnp.pad(lax.dynamic_slice_in_dim(r_small[:, 16:28, :].reshape(N_DEV, 3, CONV_W), me * shard_w, shard_w, axis=2),
                ((0, 0), (0, 0), (0, CONV_W - shard_w))).reshape(N_DEV, 12, LANES))
    outs_small = _adamw_small(r_small_mine, *packs)
    g_s, d_s, m_s, v_s = [_unpack_small(t) for t in outs_small]

    def leaves(small, big_in, big_out):
        nw_, cb_, qw_, kw_, rb_, cwf = small
        return (nw_, big_in, cwf[:, 0:shard_w], cb_, qw_, kw_, rb_, big_out)

    return (loss, grad_x.reshape(x.shape),
            *leaves(g_s, g_win, g_wo), *leaves(d_s, d_win, d_wo), *leaves(m_s, m_win, m_wo), *leaves(v_s, v_win, v_wo))
```

```python
import functools
import math

import numpy as np
import jax
import jax.numpy as jnp
from jax import lax
from jax.experimental import pallas as pl
from jax.experimental.pallas import tpu as pltpu

F32 = jnp.float32
BF16 = jnp.bfloat16

N_DEV = 8
D_MODEL = 1024
CONV_W = 512
ATTN_W = 512
N_HEADS = 8
HEAD_DIM = 64
N_PAIRS = N_HEADS // 2
LANES = 128
HALF = 64
QB = 128
KB = QB + 2 * HALF
DILATIONS = (1, 4, 16)
N_BUCKETS = 32
MAX_DISTANCE = 1024
EPS = 1e-6
NEG = -1e30
LOGIT_SCALE = HEAD_DIM ** -0.5

ADAM_LR = 0.001
ADAM_B1 = 0.9
ADAM_B2 = 0.999
ADAM_EPS = 1e-08
ADAM_WD = 0.01
ADAM_STEP = 10

MESH = pl.DeviceIdType.MESH
MIB = 1024 * 1024


def _params(semantics, vmem_mib):
    return pltpu.CompilerParams(dimension_semantics=semantics, vmem_limit_bytes=vmem_mib * MIB)


def _lane(shape):
    return lax.broadcasted_iota(jnp.int32, shape, len(shape) - 1)


def _sigmoid(z):
    return 1.0 / (1.0 + jnp.exp(-z))


def _split_dot(x, w):
    hi = x.astype(BF16)
    lo = (x - hi.astype(F32)).astype(BF16)
    return jnp.dot(hi, w, preferred_element_type=F32) + jnp.dot(lo, w, preferred_element_type=F32)


def _same_head():
    r = lax.broadcasted_iota(jnp.int32, (LANES, LANES), 0) // HEAD_DIM
    c = lax.broadcasted_iota(jnp.int32, (LANES, LANES), 1) // HEAD_DIM
    return (r == c).astype(BF16)


def _head_to_lanes(p):
    r = lax.broadcasted_iota(jnp.int32, (LANES, LANES), 0)
    c = lax.broadcasted_iota(jnp.int32, (LANES, LANES), 1) // HEAD_DIM
    return (r == 2 * p + c).astype(BF16)


def _lanes_to_head(p, scale):
    r = lax.broadcasted_iota(jnp.int32, (LANES, LANES), 0) // HEAD_DIM
    c = lax.broadcasted_iota(jnp.int32, (LANES, LANES), 1)
    return jnp.where(c == 2 * p + r, scale, 0.0).astype(BF16)


VIEW_DILATIONS = DILATIONS[1:]


def _view_shape(S, dil, width, dtype):
    return jax.ShapeDtypeStruct((S // dil, dil * width), dtype)


def _view_spec(tm, dil, width):
    return pl.BlockSpec((tm // dil, dil * width), lambda i: (i, 0))


def _tile_scratch(tm, width):
    return pltpu.VMEM((width // LANES, tm, LANES), F32)


def _scratch_value(scr_ref):
    n = scr_ref.shape[0]
    return scr_ref[0] if n == 1 else jnp.concatenate([scr_ref[c] for c in range(n)], axis=1)


def _to_view(scr_ref, out_ref, dil, dtype):
    n, tm, _ = scr_ref.shape
    for r in range(dil):
        for c in range(n):
            col = (r * n + c) * LANES
            out_ref[:, col:col + LANES] = scr_ref[c, pl.ds(r, tm // dil, stride=dil), :].astype(dtype)


def _from_view(blk_ref, scr_ref, dil):
    n, tm, _ = scr_ref.shape
    for r in range(dil):
        for c in range(n):
            col = (r * n + c) * LANES
            scr_ref[c, pl.ds(r, tm // dil, stride=dil), :] = blk_ref[:, col:col + LANES].astype(F32)


def _bucket_matrix(dilation):
    rel = np.arange(KB)[None, :] - HALF - np.arange(QB)[:, None]
    band = np.abs(rel) <= HALF
    dist = np.clip(rel, -HALF, HALF) * dilation
    half_b = N_BUCKETS // 2
    max_exact = half_b // 2
    ret = np.where(dist > 0, half_b, 0)
    n = np.abs(dist)
    nf = np.maximum(n, 1).astype(np.float32)
    large = max_exact + (np.log(nf / np.float32(max_exact)) / np.float32(math.log(MAX_DISTANCE / max_exact))
                         * np.float32(half_b - max_exact)).astype(np.int32)
    large = np.minimum(large, half_b - 1)
    bucket = ret + np.where(n < max_exact, n, large)
    return np.where(band, bucket, -1).astype(np.int32)


def _ag_weights(w_in, w_out, cw_pad):
    n_arr = 3

    def body(win_ref, wout_ref, cw_ref, gin_ref, gout_ref, gcw_ref, send_sems, recv_sems):
        x, y, c = lax.axis_index("x"), lax.axis_index("y"), lax.axis_index("c")
        me = (x, y, c)
        sibling = (x, y, 1 - c)
        chips = [(1 - x, y), (x, 1 - y), (1 - x, 1 - y)]
        arrays = (gin_ref, gout_ref, gcw_ref)

        def slot(px, py, pc):
            return 4 * px + 2 * py + pc

        gin_ref[slot(*me)] = win_ref[...].astype(BF16)
        gout_ref[slot(*me)] = wout_ref[...].astype(BF16)
        gcw_ref[slot(*me)] = cw_ref[...]

        def copy(a, k, block, to):
            ref = arrays[a].at[slot(*block)]
            return pltpu.make_async_remote_copy(
                src_ref=ref, dst_ref=ref, send_sem=send_sems.at[a * 7 + k], recv_sem=recv_sems.at[a * 7 + k],
                device_id=to, device_id_type=MESH)

        first = [copy(a, 0, me, sibling) for a in range(n_arr)]
        for j, chip in enumerate(chips):
            first += [copy(a, 1 + j, me, (*chip, c)) for a in range(n_arr)]
        for cp in first:
            cp.start()
        passed = []
        for j, chip in enumerate(chips):
            for a in range(n_arr):
                copy(a, 1 + j, (*chip, c), me).wait_recv()
            for a in range(n_arr):
                cp = copy(a, 4 + j, (*chip, c), sibling)
                cp.start()
                passed.append(cp)
        for a in range(n_arr):
            copy(a, 0, sibling, me).wait_recv()
        for j, chip in enumerate(chips):
            for a in range(n_arr):
                copy(a, 4 + j, (*chip, 1 - c), me).wait_recv()
        for cp in first + passed:
            cp.wait_send()

    vm = pl.BlockSpec(memory_space=pltpu.VMEM)
    return pl.pallas_call(
        body, name="ag_weights",
        out_shape=(jax.ShapeDtypeStruct((N_DEV,) + w_in.shape, BF16),
                   jax.ShapeDtypeStruct((N_DEV,) + w_out.shape, BF16),
                   jax.ShapeDtypeStruct((N_DEV,) + cw_pad.shape, F32)),
        in_specs=[vm, vm, vm], out_specs=(vm, vm, vm),
        scratch_shapes=[pltpu.SemaphoreType.DMA((n_arr * 7,)), pltpu.SemaphoreType.DMA((n_arr * 7,))],
        compiler_params=pltpu.CompilerParams(vmem_limit_bytes=40 * MIB),
    )(w_in, w_out, cw_pad)


def _fwd_proj(x, norm_w, wg):
    S = x.shape[0]
    tm = 512

    def body(x_ref, nw_ref, wg_ref, u_ref, gb_ref, gc_ref, zc_ref, za_ref, *rest):
        q_refs, k_refs, v_refs, rq_refs, rk_refs = (rest[3 * n:3 * n + 3] for n in range(5))
        scr, rscr = rest[15:]
        xv = x_ref[...]
        r = lax.rsqrt(jnp.mean(xv * xv, axis=-1, keepdims=True) + EPS)
        h = (xv * r * nw_ref[...]).astype(BF16)
        lane = _lane((tm, LANES))
        lo = lane < HEAD_DIM
        plain = {0: u_ref, 1: gb_ref, 2: gc_ref, 3: zc_ref, 7: za_ref}
        normed = {4: (q_refs, rq_refs), 5: (k_refs, rk_refs)}

        def emit(src, refs, dtype):
            refs[0][...] = _scratch_value(src).astype(dtype)
            for dil, ref in zip(VIEW_DILATIONS, refs[1:]):
                _to_view(src, ref, dil, dtype)

        for j in range(8):
            acc = jnp.dot(h, wg_ref[j], preferred_element_type=F32)
            if j in plain:
                plain[j][...] = acc.astype(BF16)
                continue
            if j == 6:
                for p in range(N_PAIRS):
                    scr[p] = acc[:, p * LANES:(p + 1) * LANES]
                emit(scr, v_refs, BF16)
                continue
            out_refs, r_refs = normed[j]
            r_tile = jnp.zeros((tm, LANES), F32)
            for p in range(N_PAIRS):
                blk = acc[:, p * LANES:(p + 1) * LANES]
                sq = blk * blk
                s_lo = jnp.sum(jnp.where(lo, sq, 0.0), axis=-1, keepdims=True)
                s_hi = jnp.sum(jnp.where(lo, 0.0, sq), axis=-1, keepdims=True)
                r_lo = lax.rsqrt(s_lo * (1.0 / HEAD_DIM) + EPS)
                r_hi = lax.rsqrt(s_hi * (1.0 / HEAD_DIM) + EPS)
                scr[p] = blk * jnp.where(lo, r_lo, r_hi)
                r_tile = jnp.where(lane == 2 * p, r_lo, r_tile)
                r_tile = jnp.where(lane == 2 * p + 1, r_hi, r_tile)
            rscr[0] = r_tile
            emit(scr, out_refs, BF16)
            emit(rscr, r_refs, F32)

    piece = jax.ShapeDtypeStruct((S, CONV_W), BF16)
    pspec = pl.BlockSpec((tm, CONV_W), lambda i: (i, 0))
    wide = [_view_shape(S, d, ATTN_W, BF16) for d in DILATIONS]
    wide_specs = [_view_spec(tm, d, ATTN_W) for d in DILATIONS]
    stat = [_view_shape(S, d, LANES, F32) for d in DILATIONS]
    stat_specs = [_view_spec(tm, d, LANES) for d in DILATIONS]
    outs = pl.pallas_call(
        body, name="fwd_proj", grid=(S // tm,),
        out_shape=[piece] * 5 + wide * 3 + stat * 2,
        in_specs=[pl.BlockSpec((tm, D_MODEL), lambda i: (i, 0)),
                  pl.BlockSpec((1, D_MODEL), lambda i: (0, 0)),
                  pl.BlockSpec(wg.shape, lambda i: (0, 0, 0), pipeline_mode=pl.Buffered(1))],
        out_specs=[pspec] * 5 + wide_specs * 3 + stat_specs * 2,
        scratch_shapes=[_tile_scratch(tm, ATTN_W), _tile_scratch(tm, LANES)],
        compiler_params=_params(("arbitrary",), 48),
    )(x, norm_w.reshape(1, D_MODEL), wg)
    u, gb, gc, zc, za = outs[:5]
    q_hat, k_hat, v, rq, rk = (outs[5 + 3 * n:8 + 3 * n] for n in range(5))
    return u, gb, gc, zc, za, q_hat, k_hat, v, rq, rk


def _bias_tables(rel_bias, bmat_t):
    def body(rb_ref, b_ref, out_ref):
        h = pl.program_id(1)
        b = b_ref[0]
        t = jnp.full((KB, QB), NEG, F32)
        for bk in range(N_BUCKETS):
            t = jnp.where(b == bk, rb_ref[bk, h], t)
        out_ref[0, 0] = t

    return pl.pallas_call(
        body, name="bias_tables", grid=(len(DILATIONS), N_HEADS),
        out_shape=jax.ShapeDtypeStruct((len(DILATIONS), N_HEADS, KB, QB), F32),
        in_specs=[pl.BlockSpec(memory_space=pltpu.SMEM),
                  pl.BlockSpec((1, KB, QB), lambda g, h: (g, 0, 0))],
        out_specs=pl.BlockSpec((1, 1, KB, QB), lambda g, h: (g, h, 0, 0)),
        compiler_params=_params(("arbitrary", "arbitrary"), 16),
    )(rel_bias, bmat_t)


def _bias_grad(ds_acc, bmat_t):
    def body(ds_ref, b_ref, out_ref):
        lane = _lane((1, LANES))
        row = lax.broadcasted_iota(jnp.int32, (8, LANES), 0)
        out = jnp.zeros((8, LANES), F32)
        for e in range(2):
            vec = jnp.zeros((1, LANES), F32)
            for g in range(len(DILATIONS)):
                b = b_ref[g]
                ds = ds_ref[g, 0, :, e * QB:(e + 1) * QB]
                for bk in range(N_BUCKETS):
                    s = jnp.sum(jnp.where(b == bk, ds, 0.0), axis=-1, keepdims=True)
                    s = jnp.sum(s, axis=0, keepdims=True)
                    vec = vec + jnp.where(lane == bk, s, 0.0)
            out = jnp.where(row == e, vec, out)
        out_ref[0] = out

    return pl.pallas_call(
        body, name="bias_grad", grid=(N_PAIRS,),
        out_shape=jax.ShapeDtypeStruct((N_PAIRS, 8, LANES), F32),
        in_specs=[pl.BlockSpec((len(DILATIONS), 1, KB, 2 * QB), lambda p: (0, p, 0, 0)),
                  pl.BlockSpec((len(DILATIONS), KB, QB), lambda p: (0, 0, 0))],
        out_specs=pl.BlockSpec((1, 8, LANES), lambda p: (p, 0, 0)),
        compiler_params=_params(("arbitrary",), 16),
    )(ds_acc, bmat_t)


def _halo_specs(width, rows, T, L, cols_of):
    per = T // rows
    last = L // rows - 1
    prev = pl.BlockSpec((rows, width), lambda r, i: (jnp.maximum(i * per - 1, 0), cols_of(r)))
    nxt = pl.BlockSpec((rows, width), lambda r, i: (jnp.minimum((i + 1) * per, last), cols_of(r)))
    return prev, nxt


def _attn_fwd(g, dil, qv, kv, vv, gain, tables):
    L = qv.shape[0]
    T = min(1024, L)
    n_sub = T // QB

    def body(q_ref, km_ref, kp_ref, kn_ref, vm_ref, vp_ref, vn_ref, gain_ref, bias_ref, o_ref, lse_ref,
             kwin, vwin, s_scr, p_scr):
        i = pl.program_id(1)
        kwin[0:HALF] = kp_ref[...]
        kwin[HALF:HALF + T] = km_ref[...]
        kwin[HALF + T:] = kn_ref[...]
        vwin[0:HALF] = vp_ref[...]
        vwin[HALF:HALF + T] = vm_ref[...]
        vwin[HALF + T:] = vn_ref[...]
        lo = _lane((QB, LANES)) < HEAD_DIM
        krow = lax.broadcasted_iota(jnp.int32, (KB, 1), 0)
        chan_lo = lax.broadcasted_iota(jnp.int32, (LANES, 1), 0) < HEAD_DIM
        hrow = lax.broadcasted_iota(jnp.int32, (LANES, QB), 0)

        def sub(j, carry):
            r0 = pl.multiple_of(j * QB, QB)
            kpos = i * T + j * QB - HALF + krow
            kvalid = (kpos >= 0) & (kpos < L)
            lse_rows = jnp.zeros((LANES, QB), F32)
            cols = [slice(p * LANES, (p + 1) * LANES) for p in range(N_PAIRS)]
            for p, cs in enumerate(cols):
                qf = q_ref[pl.ds(r0, QB), cs].astype(F32) * gain_ref[:, cs]
                q2 = jnp.concatenate([jnp.where(lo, qf, 0.0), jnp.where(lo, 0.0, qf)], axis=0).astype(BF16)
                s_scr[p] = lax.dot_general(kwin[pl.ds(r0, KB), cs], q2, (((1,), (1,)), ((), ())),
                                           preferred_element_type=F32)
            inv_l = []
            for p in range(N_PAIRS):
                bias2 = jnp.concatenate([bias_ref[0, 2 * p], bias_ref[0, 2 * p + 1]], axis=1)
                logits = jnp.where(kvalid, s_scr[p] + bias2, NEG)
                m = jnp.max(logits, axis=0, keepdims=True)
                pt = jnp.exp(logits - m)
                l = jnp.sum(pt, axis=0, keepdims=True)
                p_scr[p] = pt.astype(BF16)
                inv_l.append(1.0 / l)
                lse2 = m + jnp.log(l)
                lse_rows = jnp.where(hrow == 2 * p, lse2[:, 0:QB], lse_rows)
                lse_rows = jnp.where(hrow == 2 * p + 1, lse2[:, QB:2 * QB], lse_rows)
            for p, cs in enumerate(cols):
                s_scr[p, 0:LANES, :] = lax.dot_general(vwin[pl.ds(r0, KB), cs], p_scr[p], (((0,), (0,)), ((), ())),
                                                       preferred_element_type=F32) * inv_l[p]
            for p, cs in enumerate(cols):
                ot2 = s_scr[p, 0:LANES, :]
                ot = jnp.where(chan_lo, ot2[:, 0:QB], ot2[:, QB:2 * QB])
                o_ref[pl.ds(r0, QB), cs] = ot.T.astype(BF16)
            lse_ref[pl.ds(r0, QB), :] = lse_rows.T
            return carry

        lax.fori_loop(0, n_sub, sub, 0)

    main = pl.BlockSpec((T, ATTN_W), lambda r, i: (i, r))
    prev, nxt = _halo_specs(ATTN_W, HALF, T, L, lambda r: r)
    o_g, lse_g = pl.pallas_call(
        body, name=f"attn_fwd_d{dil}", grid=(dil, L // T),
        out_shape=(jax.ShapeDtypeStruct((L, dil * ATTN_W), BF16), jax.ShapeDtypeStruct((L, dil * LANES), F32)),
        in_specs=[main, main, prev, nxt, main, prev, nxt,
                  pl.BlockSpec((1, ATTN_W), lambda r, i: (0, 0)),
                  pl.BlockSpec((1, N_HEADS, KB, QB), lambda r, i: (g, 0, 0, 0))],
        out_specs=(main, pl.BlockSpec((T, LANES), lambda r, i: (i, r))),
        scratch_shapes=[pltpu.VMEM((T + 2 * HALF, ATTN_W), BF16), pltpu.VMEM((T + 2 * HALF, ATTN_W), BF16),
                        pltpu.VMEM((N_PAIRS, KB, 2 * QB), F32), pltpu.VMEM((N_PAIRS, KB, 2 * QB), BF16)],
        compiler_params=_params(("arbitrary", "arbitrary"), 40),
    )(qv, kv, kv, kv, vv, vv, vv, gain, tables)
    return o_g, lse_g


def _row_halo_specs(width, tm, S, col=0):
    rows = 16
    per = tm // rows
    last = S // rows - 1
    prev = pl.BlockSpec((rows, width), lambda i: (jnp.maximum(i * per - 1, 0), col))
    nxt = pl.BlockSpec((rows, width), lambda i: (jnp.minimum((i + 1) * per, last), col))
    return prev, nxt


def _shift_rows(a, prev_row, next_row, tm):
    row = lax.broadcasted_iota(jnp.int32, a.shape, 0)
    a_m1 = jnp.where(row == 0, prev_row, pltpu.roll(a, 1, 0))
    a_p1 = jnp.where(row == tm - 1, next_row, pltpu.roll(a, tm - 1, 0))
    return a_m1, a_p1


def _combine_gates(o_parts, lse_parts, u, gb, gc, zc, za, conv_w, conv_b):
    S = u.shape[0]
    tm = 512
    n_t = S // tm

    def body(o1, o2, o3, l1, l2, l3, u_ref, up_ref, un_ref, gb_ref, gc_ref, gcp_ref, gcn_ref, zc_ref, za_ref,
             cw_ref, cb_ref, y_ref, o_ref, lse_ref, lse2_ref, lse3_ref, so2, so3, sl2, sl3, slse):
        i = pl.program_id(0)
        for blk, scr, dil in ((o2, so2, DILATIONS[1]), (o3, so3, DILATIONS[2]),
                              (l2, sl2, DILATIONS[1]), (l3, sl3, DILATIONS[2])):
            _from_view(blk, scr, dil)
        ls = [l1[...], sl2[0], sl3[0]]
        lmax = jnp.maximum(jnp.maximum(ls[0], ls[1]), ls[2])
        es = [jnp.exp(l - lmax) for l in ls]
        den = es[0] + es[1] + es[2]
        slse[0] = lmax + jnp.log(den)
        lse_ref[...] = slse[0]
        _to_view(slse, lse2_ref, DILATIONS[1], F32)
        _to_view(slse, lse3_ref, DILATIONS[2], F32)
        inv = 1.0 / den
        ws = [e * inv for e in es]
        for p in range(N_PAIRS):
            cs = slice(p * LANES, (p + 1) * LANES)
            acc = jnp.zeros((tm, LANES), F32)
            spread = _head_to_lanes(p)
            for w, o_g in zip(ws, (o1[:, cs].astype(F32), so2[p], so3[p])):
                acc = acc + _split_dot(w, spread) * o_g
            o_ref[:, cs] = acc.astype(BF16)
            za = za_ref[:, cs].astype(F32)
            y_ref[:, CONV_W + p * LANES:CONV_W + (p + 1) * LANES] = (acc * za * _sigmoid(za)).astype(BF16)
        a = gc_ref[...].astype(F32) * u_ref[...].astype(F32)
        a_prev = gcp_ref[15:16, :].astype(F32) * up_ref[15:16, :].astype(F32) * (i > 0).astype(F32)
        a_next = gcn_ref[0:1, :].astype(F32) * un_ref[0:1, :].astype(F32) * (i < n_t - 1).astype(F32)
        a_m1, a_p1 = _shift_rows(a, a_prev, a_next, tm)
        z3 = cw_ref[0:1, :] * a_m1 + cw_ref[1:2, :] * a + cw_ref[2:3, :] * a_p1
        zc = zc_ref[...].astype(F32)
        y_ref[:, 0:CONV_W] = (gb_ref[...].astype(F32) * (z3 + cb_ref[...]) * zc * _sigmoid(zc)).astype(BF16)

    pspec = pl.BlockSpec((tm, CONV_W), lambda i: (i, 0))
    hp, hn = _row_halo_specs(CONV_W, tm, S)
    wide_specs = [_view_spec(tm, d, ATTN_W) for d in DILATIONS]
    stat_specs = [_view_spec(tm, d, LANES) for d in DILATIONS]
    outs = pl.pallas_call(
        body, name="combine_gates", grid=(n_t,),
        out_shape=[jax.ShapeDtypeStruct((S, D_MODEL), BF16), jax.ShapeDtypeStruct((S, ATTN_W), BF16)]
        + [_view_shape(S, d, LANES, F32) for d in DILATIONS],
        in_specs=wide_specs + stat_specs + [pspec, hp, hn, pspec, pspec, hp, hn, pspec, pspec,
                                            pl.BlockSpec((3, CONV_W), lambda i: (0, 0)),
                                            pl.BlockSpec((1, CONV_W), lambda i: (0, 0))],
        out_specs=[pl.BlockSpec((tm, D_MODEL), lambda i: (i, 0)), pspec] + stat_specs,
        scratch_shapes=[_tile_scratch(tm, ATTN_W), _tile_scratch(tm, ATTN_W),
                        _tile_scratch(tm, LANES), _tile_scratch(tm, LANES), _tile_scratch(tm, LANES)],
        compiler_params=_params(("arbitrary",), 40),
    )(*o_parts, *lse_parts, u, u, u, gb, gc, gc, gc, zc, za, conv_w, conv_b.reshape(1, CONV_W))
    return outs[0], outs[1], outs[2:]


def _out_proj(y, x, target, wo):
    S = x.shape[0]
    tm = 512

    def body(y_ref, x_ref, t_ref, wo_ref, g_ref, dy_ref, dwo_ref, loss_ref):
        i = pl.program_id(0)

        @pl.when(i == 0)
        def _():
            dwo_ref[...] = jnp.zeros_like(dwo_ref)
            loss_ref[...] = jnp.zeros_like(loss_ref)

        yv = y_ref[...]
        wo_v = wo_ref[...]
        err = x_ref[...] + jnp.dot(yv, wo_v, preferred_element_type=F32) - t_ref[...]
        e2 = (err * err).reshape(tm // 8, 8, D_MODEL).sum(axis=0)
        part = e2[:, 0:LANES]
        for k in range(1, D_MODEL // LANES):
            part = part + e2[:, k * LANES:(k + 1) * LANES]
        loss_ref[...] += part
        gv = (err * (1.0 / D_MODEL)).astype(BF16)
        g_ref[...] = gv
        dy_ref[...] = lax.dot_general(gv, wo_v, (((1,), (1,)), ((), ())), preferred_element_type=F32).astype(BF16)
        dwo_ref[...] += lax.dot_general(yv, gv, (((0,), (0,)), ((), ())), preferred_element_type=F32)

    tile = pl.BlockSpec((tm, D_MODEL), lambda i: (i, 0))
    return pl.pallas_call(
        body, name="out_proj", grid=(S // tm,),
        out_shape=(jax.ShapeDtypeStruct((S, D_MODEL), BF16), jax.ShapeDtypeStruct((S, D_MODEL), BF16),
                   jax.ShapeDtypeStruct((D_MODEL, D_MODEL), F32), jax.ShapeDtypeStruct((8, LANES), F32)),
        in_specs=[tile, tile, tile, pl.BlockSpec((D_MODEL, D_MODEL), lambda i: (0, 0), pipeline_mode=pl.Buffered(1))],
        out_specs=(tile, tile, pl.BlockSpec((D_MODEL, D_MODEL), lambda i: (0, 0)),
                   pl.BlockSpec((8, LANES), lambda i: (0, 0))),
        compiler_params=_params(("arbitrary",), 48),
    )(y, x, target, wo)


def _gates_bwd(dy, u, gb, gc, zc, za, o, conv_w, conv_b):
    S = u.shape[0]
    tm = 512
    n_t = S // tm

    def body(dy_ref, dyp_ref, dyn_ref, u_ref, up_ref, un_ref, gb_ref, gbp_ref, gbn_ref, gc_ref, gcp_ref, gcn_ref,
             zc_ref, zcp_ref, zcn_ref, za_ref, o_ref, cw_ref, cb_ref,
             du_ref, dgb_ref, dgc_ref, dzc_ref, dza_ref, small_ref,
             do_ref, do2_ref, do3_ref, delta_ref, delta2_ref, delta3_ref, sdo, sdelta):
        i = pl.program_id(0)

        @pl.when(i == 0)
        def _():
            small_ref[...] = jnp.zeros_like(small_ref)

        def dz3_of(dyc, zc, gb):
            return dyc * zc * _sigmoid(zc) * gb

        f = lambda ref, rows=slice(None): ref[rows, :].astype(F32)
        has_prev = (i > 0).astype(F32)
        has_next = (i < n_t - 1).astype(F32)
        pr, nr = slice(15, 16), slice(0, 1)
        uv, gbv, gcv, zcv = f(u_ref), f(gb_ref), f(gc_ref), f(zc_ref)
        dyc = dy_ref[:, 0:CONV_W].astype(F32)
        a = gcv * uv
        a_m1, a_p1 = _shift_rows(a, f(gcp_ref, pr) * f(up_ref, pr) * has_prev,
                                 f(gcn_ref, nr) * f(un_ref, nr) * has_next, tm)
        w0, w1, w2 = cw_ref[0:1, :], cw_ref[1:2, :], cw_ref[2:3, :]
        cvp = w0 * a_m1 + w1 * a + w2 * a_p1 + cb_ref[...]
        sz = _sigmoid(zcv)
        dcv = dyc * zcv * sz
        dzc_ref[...] = (dyc * gbv * cvp * (sz * (1.0 + zcv * (1.0 - sz)))).astype(BF16)
        dgb_ref[...] = (dcv * cvp).astype(BF16)
        dz3 = dcv * gbv
        dz3_prev = dz3_of(f(dyp_ref, pr), f(zcp_ref, pr), f(gbp_ref, pr)) * has_prev
        dz3_next = dz3_of(f(dyn_ref, nr), f(zcn_ref, nr), f(gbn_ref, nr)) * has_next
        dz3_m1, dz3_p1 = _shift_rows(dz3, dz3_prev, dz3_next, tm)
        da = w0 * dz3_p1 + w1 * dz3 + w2 * dz3_m1
        du_ref[...] = (da * gcv).astype(BF16)
        dgc_ref[...] = (da * uv).astype(BF16)
        csum = lambda t: jnp.sum(t, axis=0, keepdims=True)
        small_ref[0:1, :] += csum(dz3 * a_m1)
        small_ref[1:2, :] += csum(dz3 * a)
        small_ref[2:3, :] += csum(dz3 * a_p1)
        small_ref[3:4, :] += csum(dz3)

        delta = jnp.zeros((tm, LANES), F32)
        for p in range(N_PAIRS):
            cs = slice(p * LANES, (p + 1) * LANES)
            dya = dy_ref[:, CONV_W + p * LANES:CONV_W + (p + 1) * LANES].astype(F32)
            zav = za_ref[:, cs].astype(F32)
            ov = o_ref[:, cs].astype(F32)
            sa = _sigmoid(zav)
            d_o = dya * zav * sa
            sdo[p] = d_o
            do_ref[:, cs] = d_o.astype(BF16)
            dza_ref[:, cs] = (dya * ov * (sa * (1.0 + zav * (1.0 - sa)))).astype(BF16)
            delta = delta + _split_dot(d_o * ov, _lanes_to_head(p, 1.0))
        sdelta[0] = delta
        delta_ref[...] = delta
        for dil, wide_ref, stat_ref in zip(VIEW_DILATIONS, (do2_ref, do3_ref), (delta2_ref, delta3_ref)):
            _to_view(sdo, wide_ref, dil, BF16)
            _to_view(sdelta, stat_ref, dil, F32)

    pspec = pl.BlockSpec((tm, CONV_W), lambda i: (i, 0))
    hp, hn = _row_halo_specs(CONV_W, tm, S)
    piece = jax.ShapeDtypeStruct((S, CONV_W), BF16)
    outs = pl.pallas_call(
        body, name="gates_bwd", grid=(n_t,),
        out_shape=[piece] * 5 + [jax.ShapeDtypeStruct((8, CONV_W), F32)]
        + [_view_shape(S, d, ATTN_W, BF16) for d in DILATIONS] + [_view_shape(S, d, LANES, F32) for d in DILATIONS],
        in_specs=[pl.BlockSpec((tm, D_MODEL), lambda i: (i, 0)), hp, hn,
                  pspec, hp, hn, pspec, hp, hn, pspec, hp, hn, pspec, hp, hn, pspec, pspec,
                  pl.BlockSpec((3, CONV_W), lambda i: (0, 0)), pl.BlockSpec((1, CONV_W), lambda i: (0, 0))],
        out_specs=[pspec] * 5 + [pl.BlockSpec((8, CONV_W), lambda i: (0, 0))]
        + [_view_spec(tm, d, ATTN_W) for d in DILATIONS] + [_view_spec(tm, d, LANES) for d in DILATIONS],
        scratch_shapes=[_tile_scratch(tm, ATTN_W), _tile_scratch(tm, LANES)],
        compiler_params=_params(("arbitrary",), 40),
    )(dy, dy, dy, u, u, u, gb, gb, gb, gc, gc, gc, zc, zc, zc, za, o, conv_w, conv_b.reshape(1, CONV_W))
    return outs[:5], outs[5], outs[6:9], outs[9:12]


def _attn_bwd(g, dil, qv, kv, vv, dov, lv, dv_, rqv, rkv, gain, tables):
    L = qv.shape[0]
    T = min(1024, L)
    n_sub = T // QB + 1
    QW = T + 2 * HALF
    KW = T + 2 * QB

    def body(qm_ref, qp_ref, qn_ref, dom_ref, dop_ref, don_ref, km_ref, kp_ref, kn_ref, vm_ref, vp_ref, vn_ref,
             lm_ref, lp_ref, ln_ref, dm_ref, dp_ref, dn_ref, rqm_ref, rqp_ref, rqn_ref, rk_ref,
             gain_ref, bias_ref,
             dq_ref, dk_ref, dv_ref, ds_ref, a_ref,
             qwin, dowin, kwin, vwin, lwin, dwin, rqwin, dqwin, dkacc, dvacc,
             s_scr, d_scr, p_scr, ds_scr, q2_scr, do2_scr):
        r = pl.program_id(0)
        i = pl.program_id(1)

        @pl.when((r == 0) & (i == 0))
        def _():
            ds_ref[...] = jnp.zeros_like(ds_ref)
            a_ref[...] = jnp.zeros_like(a_ref)

        for win, (pr, mn, nx), h_rows in ((qwin, (qp_ref, qm_ref, qn_ref), HALF),
                                          (dowin, (dop_ref, dom_ref, don_ref), HALF),
                                          (lwin, (lp_ref, lm_ref, ln_ref), HALF),
                                          (dwin, (dp_ref, dm_ref, dn_ref), HALF),
                                          (rqwin, (rqp_ref, rqm_ref, rqn_ref), HALF),
                                          (kwin, (kp_ref, km_ref, kn_ref), QB),
                                          (vwin, (vp_ref, vm_ref, vn_ref), QB)):
            win[0:h_rows] = pr[...]
            win[h_rows:h_rows + T] = mn[...]
            win[h_rows + T:] = nx[...]
        dkacc[...] = jnp.zeros_like(dkacc)
        dvacc[...] = jnp.zeros_like(dvacc)

        lo = _lane((QB, LANES)) < HEAD_DIM
        same_head = _same_head()
        krow = lax.broadcasted_iota(jnp.int32, (KB, 1), 0)
        qcol = _lane((1, 2 * QB)) % QB
        row = lax.broadcasted_iota(jnp.int32, (QB, 1), 0)
        chan_lo = lax.broadcasted_iota(jnp.int32, (LANES, 1), 0) < HEAD_DIM

        def sub(j, carry):
            r0 = pl.multiple_of(j * QB, QB)
            kpos = i * T - QB + j * QB + krow
            qpos = i * T - HALF + j * QB + qcol
            valid = (kpos >= 0) & (kpos < L) & (qpos >= 0) & (qpos < L)
            owned_t = ((qpos >= i * T) & (qpos < i * T + T)).astype(F32)
            qpos_c = i * T - HALF + j * QB + row
            owned = ((qpos_c >= i * T) & (qpos_c < i * T + T)).astype(F32)
            lse_rows = lwin[pl.ds(r0, QB), :].T
            delta_rows = dwin[pl.ds(r0, QB), :].T
            rq_t = rqwin[pl.ds(r0, QB), :]
            cols = [slice(p * LANES, (p + 1) * LANES) for p in range(N_PAIRS)]
            nt = (((1,), (1,)), ((), ()))
            for p, cs in enumerate(cols):
                qf = qwin[pl.ds(r0, QB), cs].astype(F32) * gain_ref[:, cs]
                q2_scr[p] = jnp.concatenate([jnp.where(lo, qf, 0.0), jnp.where(lo, 0.0, qf)], axis=0).astype(BF16)
                dov = dowin[pl.ds(r0, QB), cs]
                zero = jnp.zeros_like(dov)
                do2_scr[p] = jnp.concatenate([jnp.where(lo, dov, zero), jnp.where(lo, zero, dov)], axis=0)
                s_scr[p] = lax.dot_general(kwin[pl.ds(r0, KB), cs], q2_scr[p], nt, preferred_element_type=F32)
                d_scr[p] = lax.dot_general(vwin[pl.ds(r0, KB), cs], do2_scr[p], nt, preferred_element_type=F32)
            for p in range(N_PAIRS):
                bias2 = jnp.concatenate([bias_ref[0, 2 * p], bias_ref[0, 2 * p + 1]], axis=1)
                lse2 = jnp.concatenate([lse_rows[2 * p:2 * p + 1, :], lse_rows[2 * p + 1:2 * p + 2, :]], axis=1)
                delta2 = jnp.concatenate([delta_rows[2 * p:2 * p + 1, :], delta_rows[2 * p + 1:2 * p + 2, :]], axis=1)
                pt = jnp.where(valid, jnp.exp(s_scr[p] + bias2 - lse2), 0.0)
                dst = pt * (d_scr[p] - delta2)
                ds_ref[p] += dst * owned_t
                p_scr[p] = pt.astype(BF16)
                ds_scr[p] = dst.astype(BF16)
            for p, cs in enumerate(cols):
                dvacc[pl.ds(r0, KB), cs] += jnp.dot(p_scr[p], do2_scr[p], preferred_element_type=F32)
                dkacc[pl.ds(r0, KB), cs] += jnp.dot(ds_scr[p], q2_scr[p], preferred_element_type=F32)
                s_scr[p, 0:LANES, :] = lax.dot_general(kwin[pl.ds(r0, KB), cs], ds_scr[p], (((0,), (0,)), ((), ())),
                                                       preferred_element_type=F32)
            for p, cs in enumerate(cols):
                g2 = gain_ref[:, cs]
                q_f = qwin[pl.ds(r0, QB), cs].astype(F32)
                dqt2 = s_scr[p, 0:LANES, :]
                dqm2 = jnp.where(chan_lo, dqt2[:, 0:QB], dqt2[:, QB:2 * QB]).T
                a_ref[p] += (dqm2 * q_f * owned).reshape(QB // 8, 8, LANES).sum(axis=0)
                dqh = dqm2 * g2
                mean_c = _split_dot(dqh * q_f, same_head) * (1.0 / HEAD_DIM)
                rq2 = _split_dot(rq_t, _head_to_lanes(p))
                dqwin[pl.ds(r0, QB), cs] = (rq2 * (dqh - q_f * mean_c)).astype(BF16)
            return carry

        lax.fori_loop(0, n_sub, sub, 0)
        dq_ref[...] = dqwin[HALF:HALF + T]

        def finish(j, carry):
            r0 = pl.multiple_of(j * QB, QB)
            rk_t = rk_ref[pl.ds(r0, QB), :]
            for p in range(N_PAIRS):
                cs = slice(p * LANES, (p + 1) * LANES)
                dkh = dkacc[pl.ds(QB + r0, QB), cs]
                k_f = km_ref[pl.ds(r0, QB), cs].astype(F32)
                mean_c = _split_dot(dkh * k_f, same_head) * (1.0 / HEAD_DIM)
                rk2 = _split_dot(rk_t, _head_to_lanes(p))
                dk_ref[pl.ds(r0, QB), cs] = (rk2 * (dkh - k_f * mean_c)).astype(BF16)
                dv_ref[pl.ds(r0, QB), cs] = dvacc[pl.ds(QB + r0, QB), cs].astype(BF16)
            return carry

        lax.fori_loop(0, T // QB, finish, 0)

    main = pl.BlockSpec((T, ATTN_W), lambda r, i: (i, r))
    smain = pl.BlockSpec((T, LANES), lambda r, i: (i, r))
    q_prev, q_next = _halo_specs(ATTN_W, HALF, T, L, lambda r: r)
    k_prev, k_next = _halo_specs(ATTN_W, QB, T, L, lambda r: r)
    s_prev, s_next = _halo_specs(LANES, HALF, T, L, lambda r: r)
    piece = jax.ShapeDtypeStruct((L, dil * ATTN_W), BF16)
    return pl.pallas_call(
        body, name=f"attn_bwd_d{dil}", grid=(dil, L // T),
        out_shape=(piece, piece, piece, jax.ShapeDtypeStruct((N_PAIRS, KB, 2 * QB), F32),
                   jax.ShapeDtypeStruct((N_PAIRS, 8, LANES), F32)),
        in_specs=[main, q_prev, q_next, main, q_prev, q_next, main, k_prev, k_next, main, k_prev, k_next,
                  smain, s_prev, s_next, smain, s_prev, s_next, smain, s_prev, s_next, smain,
                  pl.BlockSpec((1, ATTN_W), lambda r, i: (0, 0)),
                  pl.BlockSpec((1, N_HEADS, KB, QB), lambda r, i: (g, 0, 0, 0))],
        out_specs=(main, main, main, pl.BlockSpec((N_PAIRS, KB, 2 * QB), lambda r, i: (0, 0, 0)),
                   pl.BlockSpec((N_PAIRS, 8, LANES), lambda r, i: (0, 0, 0))),
        scratch_shapes=[pltpu.VMEM((QW, ATTN_W), BF16), pltpu.VMEM((QW, ATTN_W), BF16),
                        pltpu.VMEM((KW, ATTN_W), BF16), pltpu.VMEM((KW, ATTN_W), BF16),
                        pltpu.VMEM((QW, LANES), F32), pltpu.VMEM((QW, LANES), F32), pltpu.VMEM((QW, LANES), F32),
                        pltpu.VMEM((QW, ATTN_W), BF16),
                        pltpu.VMEM((KW, ATTN_W), F32), pltpu.VMEM((KW, ATTN_W), F32),
                        pltpu.VMEM((N_PAIRS, KB, 2 * QB), F32), pltpu.VMEM((N_PAIRS, KB, 2 * QB), F32),
                        pltpu.VMEM((N_PAIRS, KB, 2 * QB), BF16), pltpu.VMEM((N_PAIRS, KB, 2 * QB), BF16),
                        pltpu.VMEM((N_PAIRS, 2 * QB, LANES), BF16), pltpu.VMEM((N_PAIRS, 2 * QB, LANES), BF16)],
        compiler_params=_params(("arbitrary", "arbitrary"), 56),
    )(qv, qv, qv, dov, dov, dov, kv, kv, kv, vv, vv, vv, lv, lv, lv, dv_, dv_, dv_, rqv, rqv, rqv, rkv, gain, tables)


def _proj_bwd(x, d_out, norm_w, wg, pieces):
    S = x.shape[0]
    tm = 512
    flat = [share for p in pieces for share in p]
    n_t = S // tm

    def body(*refs):
        x_ref, g_ref, nw_ref, wg_ref = refs[:4]
        piece_refs = refs[4:4 + len(flat)]
        gx_ref, dnw_ref, ht_ref, dproj_ref, sview = refs[4 + len(flat):]
        i = pl.program_id(0)

        @pl.when(i == 0)
        def _():
            dnw_ref[...] = jnp.zeros_like(dnw_ref)

        xv = x_ref[...]
        r = lax.rsqrt(jnp.mean(xv * xv, axis=-1, keepdims=True) + EPS)
        xh = xv * r
        nw = nw_ref[...]
        ht_ref[...] = (xh * nw).T.astype(BF16)
        dh = jnp.zeros((tm, D_MODEL), F32)
        k = 0
        for j in range(8):
            shares = pieces[j]
            if len(shares) == 1:
                dp = piece_refs[k][...]
            else:
                dp = jnp.zeros((tm, CONV_W), F32)
                for t, (_, dil) in enumerate(shares):
                    if dil == 1:
                        dp = dp + piece_refs[k + t][...].astype(F32)
                    else:
                        _from_view(piece_refs[k + t], sview, dil)
                        dp = dp + _scratch_value(sview)
                dp = dp.astype(BF16)
            k += len(shares)
            dproj_ref[j] = dp
            dh = dh + lax.dot_general(dp, wg_ref[j], (((1,), (1,)), ((), ())), preferred_element_type=F32)
        dnw_ref[...] += (dh * xh).reshape(tm // 8, 8, D_MODEL).sum(axis=0)
        dxh = dh * nw
        mean_c = jnp.mean(dxh * xh, axis=-1, keepdims=True)
        gx_ref[...] = g_ref[...].astype(F32) + r * (dxh - xh * mean_c)

    tile = pl.BlockSpec((tm, D_MODEL), lambda i: (i, 0))
    return pl.pallas_call(
        body, name="proj_bwd", grid=(n_t,),
        out_shape=(jax.ShapeDtypeStruct((S, D_MODEL), F32), jax.ShapeDtypeStruct((8, D_MODEL), F32),
                   jax.ShapeDtypeStruct((D_MODEL, S), BF16), jax.ShapeDtypeStruct((8, S, CONV_W), BF16)),
        in_specs=[tile, tile, pl.BlockSpec((1, D_MODEL), lambda i: (0, 0)),
                  pl.BlockSpec(wg.shape, lambda i: (0, 0, 0), pipeline_mode=pl.Buffered(1))]
        + [_view_spec(tm, dil, CONV_W) for _, dil in flat],
        out_specs=(tile, pl.BlockSpec((8, D_MODEL), lambda i: (0, 0)),
                   pl.BlockSpec((D_MODEL, tm), lambda i: (0, i)), pl.BlockSpec((8, tm, CONV_W), lambda i: (0, i, 0))),
        scratch_shapes=[_tile_scratch(tm, CONV_W)],
        compiler_params=_params(("arbitrary",), 56),
    )(x, d_out, norm_w.reshape(1, D_MODEL), wg, *[a for a, _ in flat])


def _dw_exchange(ht, dproj, gw_out, small):
    D, S = ht.shape
    tk = 1024
    n_t = S // tk
    me_outer = 4 * lax.axis_index("x") + 2 * lax.axis_index("y") + lax.axis_index("c")
    order = ((me_outer + 1 + jnp.arange(N_DEV, dtype=jnp.int32)) % N_DEV).astype(jnp.int32)

    def body(order_ref, ht_ref, dp_ref, gout_ref, sm_ref, rin_ref, rout_ref, rsm_ref,
             acc, sbuf, in_send, in_recv, side_send, side_recv, local_sems):
        s = pl.program_id(0)
        t = pl.program_id(1)
        x, y, c = lax.axis_index("x"), lax.axis_index("y"), lax.axis_index("c")
        me = 4 * x + 2 * y + c

        def in_copy(k):
            to = (me + 1 + k) % N_DEV
            return pltpu.make_async_remote_copy(
                src_ref=sbuf.at[k % 2], dst_ref=rin_ref.at[me], send_sem=in_send.at[k], recv_sem=in_recv.at[k],
                device_id=(to // 4, (to // 2) % 2, to % 2), device_id_type=MESH)

        def in_landing(k):
            frm = (me + 2 * N_DEV - 1 - k) % N_DEV
            return pltpu.make_async_remote_copy(
                src_ref=sbuf.at[0], dst_ref=rin_ref.at[frm], send_sem=in_send.at[k], recv_sem=in_recv.at[k],
                device_id=(x, y, c), device_id_type=MESH)

        def side_copies():
            local = [pltpu.make_async_copy(gout_ref.at[me], rout_ref.at[me], local_sems.at[0]),
                     pltpu.make_async_copy(sm_ref, rsm_ref.at[me], local_sems.at[1])]
            remote = []
            for k in range(1, N_DEV):
                px = 1 - x if k & 4 else x
                py = 1 - y if k & 2 else y
                pc = 1 - c if k & 1 else c
                peer = 4 * px + 2 * py + pc
                for a, (src, dst) in enumerate(((gout_ref.at[peer], rout_ref.at[me]), (sm_ref, rsm_ref.at[me]))):
                    remote.append(pltpu.make_async_remote_copy(
                        src_ref=src, dst_ref=dst, send_sem=side_send.at[a * 7 + k - 1],
                        recv_sem=side_recv.at[a * 7 + k - 1], device_id=(px, py, pc), device_id_type=MESH))
            return local, remote

        @pl.when((s == 0) & (t == 0))
        def _():
            local, remote = side_copies()
            for cp in local + remote:
                cp.start()

        @pl.when(t == 0)
        def _():
            acc[...] = jnp.zeros_like(acc)

        acc[...] += jnp.dot(ht_ref[...], dp_ref[0], preferred_element_type=F32)

        for k in range(N_DEV):
            @pl.when((s == k) & (t == n_t - 1))
            def _(k=k):
                if k >= 2:
                    in_copy(k - 2).wait_send()
                sbuf[k % 2] = acc[...].astype(BF16)
                if k < N_DEV - 1:
                    in_copy(k).start()
                else:
                    own = pltpu.make_async_copy(sbuf.at[k % 2], rin_ref.at[me], local_sems.at[2])
                    own.start()
                    in_copy(k - 1).wait_send()
                    for j in range(N_DEV - 1):
                        in_landing(j).wait_recv()
                    local, remote = side_copies()
                    for cp in remote:
                        cp.wait_recv()
                    for cp in remote:
                        cp.wait_send()
                    for cp in local:
                        cp.wait()
                    own.wait()

    hbm = pl.BlockSpec(memory_space=pl.ANY)
    grid_spec = pltpu.PrefetchScalarGridSpec(
        num_scalar_prefetch=1, grid=(N_DEV, n_t),
        in_specs=[pl.BlockSpec((D, tk), lambda s, t, order_ref: (0, t)),
                  pl.BlockSpec((1, tk, CONV_W), lambda s, t, order_ref: (order_ref[s], t, 0)),
                  hbm, hbm],
        out_specs=(hbm, hbm, hbm),
        scratch_shapes=[pltpu.VMEM((D, CONV_W), F32), pltpu.VMEM((2, D, CONV_W), BF16),
                        pltpu.SemaphoreType.DMA((N_DEV - 1,)), pltpu.SemaphoreType.DMA((N_DEV - 1,)),
                        pltpu.SemaphoreType.DMA((14,)), pltpu.SemaphoreType.DMA((14,)),
                        pltpu.SemaphoreType.DMA((3,))])
    return pl.pallas_call(
        body, name="dw_exchange", grid_spec=grid_spec,
        out_shape=(jax.ShapeDtypeStruct((N_DEV, D, CONV_W), BF16), jax.ShapeDtypeStruct(gw_out.shape, F32),
                   jax.ShapeDtypeStruct((N_DEV,) + small.shape, F32)),
        compiler_params=_params(("arbitrary", "arbitrary"), 32),
    )(order, ht, dproj, gw_out, small)


def _adamw_math(w, g, m, v):
    m2 = ADAM_B1 * m + (1.0 - ADAM_B1) * g
    v2 = ADAM_B2 * v + (1.0 - ADAM_B2) * (g * g)
    m_hat = m2 / (1.0 - ADAM_B1 ** ADAM_STEP)
    v_hat = v2 / (1.0 - ADAM_B2 ** ADAM_STEP)
    delta = -ADAM_LR * (m_hat / (jnp.sqrt(v_hat) + ADAM_EPS) + ADAM_WD * w)
    return delta, m2, v2


def _adamw_sharded(name, parts, w, m, v, rows):
    R, C = w.shape

    def body(p_ref, w_ref, m_ref, v_ref, g_ref, d_ref, m2_ref, v2_ref):
        g = p_ref[0].astype(F32)
        for s in range(1, N_DEV):
            g = g + p_ref[s].astype(F32)
        g_ref[...] = g
        d_ref[...], m2_ref[...], v2_ref[...] = _adamw_math(w_ref[...], g, m_ref[...], v_ref[...])

    spec = pl.BlockSpec((rows, C), lambda i: (i, 0))
    out = jax.ShapeDtypeStruct((R, C), F32)
    return pl.pallas_call(
        body, name=name, grid=(R // rows,),
        out_shape=(out,) * 4,
        in_specs=[pl.BlockSpec((N_DEV, rows, C), lambda i: (0, i, 0)), spec, spec, spec],
        out_specs=(spec,) * 4,
        compiler_params=_params(("arbitrary",), 40),
    )(parts, w, m, v)


def _adamw_small(parts, w, m, v):
    R = w.shape[0]

    def body(p_ref, w_ref, m_ref, v_ref, g_ref, d_ref, m2_ref, v2_ref):
        g = p_ref[0]
        for s in range(1, N_DEV):
            g = g + p_ref[s]
        g_ref[...] = g
        d_ref[...], m2_ref[...], v2_ref[...] = _adamw_math(w_ref[...], g, m_ref[...], v_ref[...])

    vm = pl.BlockSpec(memory_space=pltpu.VMEM)
    out = jax.ShapeDtypeStruct((R, LANES), F32)
    return pl.pallas_call(
        body, name="adamw_small", out_shape=(out,) * 4, in_specs=[vm] * 4, out_specs=(vm,) * 4,
    )(parts, w, m, v)


def _qk_gain_grads(a_parts, q_norm_w, k_norm_w, bias_g):
    def body(a_ref, qw_ref, kw_ref, bg_ref, out_ref):
        tot = jnp.zeros((8, LANES), F32)
        for g in range(len(DILATIONS)):
            for p in range(N_PAIRS):
                tot = tot + a_ref[g, p]
        col = jnp.sum(tot, axis=0, keepdims=True)
        a64 = col[:, 0:HEAD_DIM] + col[:, HEAD_DIM:LANES]
        out_ref[...] = jnp.zeros_like(out_ref)
        out_ref[0:1, 0:HEAD_DIM] = LOGIT_SCALE * kw_ref[...] * a64
        out_ref[1:2, 0:HEAD_DIM] = LOGIT_SCALE * qw_ref[...] * a64
        for h in range(N_HEADS):
            r0, c0 = 2 + h // 4, (h % 4) * N_BUCKETS
            out_ref[r0:r0 + 1, c0:c0 + N_BUCKETS] = bg_ref[h // 2, h % 2:h % 2 + 1, 0:N_BUCKETS]

    vm = pl.BlockSpec(memory_space=pltpu.VMEM)
    return pl.pallas_call(
        body, name="qk_gain_grads", out_shape=jax.ShapeDtypeStruct((8, LANES), F32),
        in_specs=[vm] * 4, out_specs=vm,
    )(a_parts, q_norm_w.reshape(1, HEAD_DIM), k_norm_w.reshape(1, HEAD_DIM), bias_g)


SMALL_ROWS = 32


def _pack_small(norm_w, conv_b, q_norm_w, k_norm_w, rel_bias, conv_w_full):
    pad64 = lambda a: jnp.pad(a, (0, LANES - HEAD_DIM)).reshape(1, LANES)
    return jnp.concatenate([
        norm_w.reshape(8, LANES), conv_b.reshape(4, LANES), pad64(q_norm_w), pad64(k_norm_w),
        rel_bias.T.reshape(2, LANES), conv_w_full.reshape(12, LANES), jnp.zeros((4, LANES), F32)], axis=0)


def _unpack_small(s):
    return (s[0:8].reshape(D_MODEL), s[8:12].reshape(CONV_W), s[12, 0:HEAD_DIM], s[13, 0:HEAD_DIM],
            s[14:16].reshape(N_HEADS, N_BUCKETS).T, s[16:28].reshape(3, CONV_W))


def kernel(x, norm_w, w_in, conv_w, conv_b, q_norm_w, k_norm_w, rel_bias, w_out, loss_target, m_norm_w, m_w_in, m_conv_w, m_conv_b, m_q_norm_w, m_k_norm_w, m_rel_bias, m_w_out, v_norm_w, v_w_in, v_conv_w, v_conv_b, v_q_norm_w, v_k_norm_w, v_rel_bias, v_w_out):
    S = x.shape[1]
    x2 = x.reshape(S, D_MODEL)
    tgt = loss_target.reshape(S, D_MODEL)
    me = 4 * lax.axis_index("x") + 2 * lax.axis_index("y") + lax.axis_index("c")
    shard_w = CONV_W // N_DEV

    cw_pad = jnp.pad(conv_w, ((0, 5), (0, LANES - shard_w)))
    wg, wo_g, cw_g = _ag_weights(w_in, w_out, cw_pad)
    wo = wo_g.reshape(D_MODEL, D_MODEL)
    conv_w_full = cw_g[:, 0:3, 0:shard_w].transpose(1, 0, 2).reshape(3, CONV_W)

    bmat_t = jnp.asarray(np.stack([_bucket_matrix(d).T for d in DILATIONS]))
    tables = _bias_tables(rel_bias, bmat_t)
    gain = jnp.tile(q_norm_w * k_norm_w * LOGIT_SCALE, N_HEADS).reshape(1, ATTN_W)

    u, gb, gc, zc, za, q_hat, k_hat, v, rq, rk = _fwd_proj(x2, norm_w, wg)
    o_parts, lse_parts = [], []
    for g, dil in enumerate(DILATIONS):
        o_g, lse_g = _attn_fwd(g, dil, q_hat[g], k_hat[g], v[g], gain, tables)
        o_parts.append(o_g)
        lse_parts.append(lse_g)
    y, o, lse = _combine_gates(o_parts, lse_parts, u, gb, gc, zc, za, conv_w_full, conv_b)
    d_out, dy, gwo_part, loss_part = _out_proj(y, x2, tgt, wo)
    loss = lax.psum(jnp.sum(loss_part) * (0.5 / D_MODEL), ("x", "y", "c"))

    (du, dgb, dgc, dzc, dza), conv_small, d_o, delta = _gates_bwd(dy, u, gb, gc, zc, za, o, conv_w_full, conv_b)
    dqs, dks, dvs, ds_parts, a_parts = [], [], [], [], []
    for g, dil in enumerate(DILATIONS):
        dq, dk, dv, ds_sum, a_sum = _attn_bwd(g, dil, q_hat[g], k_hat[g], v[g], d_o[g], lse[g], delta[g], rq[g], rk[g],
                                              gain, tables)
        dqs.append((dq, dil))
        dks.append((dk, dil))
        dvs.append((dv, dil))
        ds_parts.append(ds_sum)
        a_parts.append(a_sum)
    grad_x, gnw_part, h_t, dproj = _proj_bwd(
        x2, d_out, norm_w, wg, [[(du, 1)], [(dgb, 1)], [(dgc, 1)], [(dzc, 1)], dqs, dks, dvs, [(dza, 1)]])
    bias_g = _bias_grad(jnp.stack(ds_parts), bmat_t)
    qk_small = _qk_gain_grads(jnp.stack(a_parts), q_norm_w, k_norm_w, bias_g)

    small_part = jnp.concatenate([
        gnw_part.sum(axis=0).reshape(8, LANES), conv_small[3].reshape(4, LANES), qk_small[0:4],
        conv_small[0:3].reshape(12, LANES), jnp.zeros((4, LANES), F32)], axis=0)
    r_in, r_out, r_small = _dw_exchange(h_t, dproj, gwo_part.reshape(N_DEV, D_MODEL // N_DEV, D_MODEL), small_part)

    g_win, d_win, m_win, v_win = _adamw_sharded("adamw_w_in", r_in, w_in, m_w_in, v_w_in, 128)
    g_wo, d_wo, m_wo, v_wo = _adamw_sharded("adamw_w_out", r_out, w_out, m_w_out, v_w_out, 32)

    def full_conv(a):
        return jnp.zeros((3, CONV_W), F32).at[:, 0:shard_w].set(a)

    packs = [_pack_small(nw_, cb_, qw_, kw_, rb_, full_conv(cw_)) for nw_, cb_, qw_, kw_, rb_, cw_ in (
        (norm_w, conv_b, q_norm_w, k_norm_w, rel_bias, conv_w),
        (m_norm_w, m_conv_b, m_q_norm_w, m_k_norm_w, m_rel_bias, m_conv_w),
        (v_norm_w, v_conv_b, v_q_norm_w, v_k_norm_w, v_rel_bias, v_conv_w))]
    r_small_mine = r_small.at[:, 16:28, :].set(
        jnp.pad(lax.dynamic_slice_in_dim(r_small[:, 16:28, :].reshape(N_DEV, 3, CONV_W), me * shard_w, shard_w, axis=2),
                ((0, 0), (0, 0), (0, CONV_W - shard_w))).reshape(N_DEV, 12, LANES))
    outs_small = _adamw_small(r_small_mine, *packs)
    g_s, d_s, m_s, v_s = [_unpack_small(t) for t in outs_small]

    def leaves(small, big_in, big_out):
        nw_, cb_, qw_, kw_, rb_, cwf = small
        return (nw_, big_in, cwf[:, 0:shard_w], cb_, qw_, kw_, rb_, big_out)

    return (loss, grad_x.reshape(x.shape),
            *leaves(g_s, g_win, g_wo), *leaves(d_s, d_win, d_wo), *leaves(m_s, m_win, m_wo), *leaves(v_s, v_win, v_wo))
```

```python
import functools
import math

import numpy as np
import jax
import jax.numpy as jnp
from jax import lax
from jax.experimental import pallas as pl
from jax.experimental.pallas import tpu as pltpu

F32 = jnp.float32
BF16 = jnp.bfloat16

N_DEV = 8
D_MODEL = 1024
CONV_W = 512
ATTN_W = 512
N_HEADS = 8
HEAD_DIM = 64
N_PAIRS = N_HEADS // 2
LANES = 128
HALF = 64
QB = 128
KB = QB + 2 * HALF
DILATIONS = (1, 4, 16)
N_BUCKETS = 32
MAX_DISTANCE = 1024
EPS = 1e-6
NEG = -1e30
LOGIT_SCALE = HEAD_DIM ** -0.5

ADAM_LR = 0.001
ADAM_B1 = 0.9
ADAM_B2 = 0.999
ADAM_EPS = 1e-08
ADAM_WD = 0.01
ADAM_STEP = 10

MESH = pl.DeviceIdType.MESH
MIB = 1024 * 1024


def _params(semantics, vmem_mib):
    return pltpu.CompilerParams(dimension_semantics=semantics, vmem_limit_bytes=vmem_mib * MIB)


def _lane(shape):
    return lax.broadcasted_iota(jnp.int32, shape, len(shape) - 1)


def _sigmoid(z):
    return 1.0 / (1.0 + jnp.exp(-z))


def _split_dot(x, w):
    hi = x.astype(BF16)
    lo = (x - hi.astype(F32)).astype(BF16)
    return jnp.dot(hi, w, preferred_element_type=F32) + jnp.dot(lo, w, preferred_element_type=F32)


def _same_head():
    r = lax.broadcasted_iota(jnp.int32, (LANES, LANES), 0) // HEAD_DIM
    c = lax.broadcasted_iota(jnp.int32, (LANES, LANES), 1) // HEAD_DIM
    return (r == c).astype(BF16)


def _head_to_lanes(p):
    r = lax.broadcasted_iota(jnp.int32, (LANES, LANES), 0)
    c = lax.broadcasted_iota(jnp.int32, (LANES, LANES), 1) // HEAD_DIM
    return (r == 2 * p + c).astype(BF16)


def _lanes_to_head(p, scale):
    r = lax.broadcasted_iota(jnp.int32, (LANES, LANES), 0) // HEAD_DIM
    c = lax.broadcasted_iota(jnp.int32, (LANES, LANES), 1)
    return jnp.where(c == 2 * p + r, scale, 0.0).astype(BF16)


VIEW_DILATIONS = DILATIONS[1:]


def _view_shape(S, dil, width, dtype):
    return jax.ShapeDtypeStruct((S // dil, dil * width), dtype)


def _view_spec(tm, dil, width):
    return pl.BlockSpec((tm // dil, dil * width), lambda i: (i, 0))


def _tile_scratch(tm, width):
    return pltpu.VMEM((width // LANES, tm, LANES), F32)


def _scratch_value(scr_ref):
    n = scr_ref.shape[0]
    return scr_ref[0] if n == 1 else jnp.concatenate([scr_ref[c] for c in range(n)], axis=1)


def _to_view(scr_ref, out_ref, dil, dtype):
    n, tm, _ = scr_ref.shape
    for r in range(dil):
        for c in range(n):
            col = (r * n + c) * LANES
            out_ref[:, col:col + LANES] = scr_ref[c, pl.ds(r, tm // dil, stride=dil), :].astype(dtype)


def _from_view(blk_ref, scr_ref, dil):
    n, tm, _ = scr_ref.shape
    for r in range(dil):
        for c in range(n):
            col = (r * n + c) * LANES
            scr_ref[c, pl.ds(r, tm // dil, stride=dil), :] = blk_ref[:, col:col + LANES].astype(F32)


def _bucket_matrix(dilation):
    rel = np.arange(KB)[None, :] - HALF - np.arange(QB)[:, None]
    band = np.abs(rel) <= HALF
    dist = np.clip(rel, -HALF, HALF) * dilation
    half_b = N_BUCKETS // 2
    max_exact = half_b // 2
    ret = np.where(dist > 0, half_b, 0)
    n = np.abs(dist)
    nf = np.maximum(n, 1).astype(np.float32)
    large = max_exact + (np.log(nf / np.float32(max_exact)) / np.float32(math.log(MAX_DISTANCE / max_exact))
                         * np.float32(half_b - max_exact)).astype(np.int32)
    large = np.minimum(large, half_b - 1)
    bucket = ret + np.where(n < max_exact, n, large)
    return np.where(band, bucket, -1).astype(np.int32)


def _ag_weights(w_in, w_out, cw_pad):
    n_arr = 3

    def body(win_ref, wout_ref, cw_ref, gin_ref, gout_ref, gcw_ref, send_sems, recv_sems):
        x, y, c = lax.axis_index("x"), lax.axis_index("y"), lax.axis_index("c")
        me = (x, y, c)
        sibling = (x, y, 1 - c)
        chips = [(1 - x, y), (x, 1 - y), (1 - x, 1 - y)]
        arrays = (gin_ref, gout_ref, gcw_ref)

        def slot(px, py, pc):
            return 4 * px + 2 * py + pc

        gin_ref[slot(*me)] = win_ref[...].astype(BF16)
        gout_ref[slot(*me)] = wout_ref[...].astype(BF16)
        gcw_ref[slot(*me)] = cw_ref[...]

        def copy(a, k, block, to):
            ref = arrays[a].at[slot(*block)]
            return pltpu.make_async_remote_copy(
                src_ref=ref, dst_ref=ref, send_sem=send_sems.at[a * 7 + k], recv_sem=recv_sems.at[a * 7 + k],
                device_id=to, device_id_type=MESH)

        first = [copy(a, 0, me, sibling) for a in range(n_arr)]
        for j, chip in enumerate(chips):
            first += [copy(a, 1 + j, me, (*chip, c)) for a in range(n_arr)]
        for cp in first:
            cp.start()
        passed = []
        for j, chip in enumerate(chips):
            for a in range(n_arr):
                copy(a, 1 + j, (*chip, c), me).wait_recv()
            for a in range(n_arr):
                cp = copy(a, 4 + j, (*chip, c), sibling)
                cp.start()
                passed.append(cp)
        for a in range(n_arr):
            copy(a, 0, sibling, me).wait_recv()
        for j, chip in enumerate(chips):
            for a in range(n_arr):
                copy(a, 4 + j, (*chip, 1 - c), me).wait_recv()
        for cp in first + passed:
            cp.wait_send()

    vm = pl.BlockSpec(memory_space=pltpu.VMEM)
    return pl.pallas_call(
        body, name="ag_weights",
        out_shape=(jax.ShapeDtypeStruct((N_DEV,) + w_in.shape, BF16),
                   jax.ShapeDtypeStruct((N_DEV,) + w_out.shape, BF16),
                   jax.ShapeDtypeStruct((N_DEV,) + cw_pad.shape, F32)),
        in_specs=[vm, vm, vm], out_specs=(vm, vm, vm),
        scratch_shapes=[pltpu.SemaphoreType.DMA((n_arr * 7,)), pltpu.SemaphoreType.DMA((n_arr * 7,))],
        compiler_params=pltpu.CompilerParams(vmem_limit_bytes=40 * MIB),
    )(w_in, w_out, cw_pad)


def _fwd_proj(x, norm_w, wg):
    S = x.shape[0]
    tm = 512

    def body(x_ref, nw_ref, wg_ref, u_ref, gb_ref, gc_ref, zc_ref, za_ref, *rest):
        q_refs, k_refs, v_refs, rq_refs, rk_refs = (rest[3 * n:3 * n + 3] for n in range(5))
        scr, rscr = rest[15:]
        xv = x_ref[...]
        r = lax.rsqrt(jnp.mean(xv * xv, axis=-1, keepdims=True) + EPS)
        h = (xv * r * nw_ref[...]).astype(BF16)
        lane = _lane((tm, LANES))
        lo = lane < HEAD_DIM
        plain = {0: u_ref, 1: gb_ref, 2: gc_ref, 3: zc_ref, 7: za_ref}
        normed = {4: (q_refs, rq_refs), 5: (k_refs, rk_refs)}

        def emit(src, refs, dtype):
            refs[0][...] = _scratch_value(src).astype(dtype)
            for dil, ref in zip(VIEW_DILATIONS, refs[1:]):
                _to_view(src, ref, dil, dtype)

        acc_next = jnp.dot(h, wg_ref[0], preferred_element_type=F32)
        for j in range(8):
            acc = acc_next
            if j < 7:
                acc_next = jnp.dot(h, wg_ref[j + 1], preferred_element_type=F32)
            if j in plain:
                plain[j][...] = acc.astype(BF16)
                continue
            if j == 6:
                for p in range(N_PAIRS):
                    scr[p] = acc[:, p * LANES:(p + 1) * LANES]
                emit(scr, v_refs, BF16)
                continue
            out_refs, r_refs = normed[j]
            r_tile = jnp.zeros((tm, LANES), F32)
            for p in range(N_PAIRS):
                blk = acc[:, p * LANES:(p + 1) * LANES]
                sq = blk * blk
                s_lo = jnp.sum(jnp.where(lo, sq, 0.0), axis=-1, keepdims=True)
                s_hi = jnp.sum(jnp.where(lo, 0.0, sq), axis=-1, keepdims=True)
                r_lo = lax.rsqrt(s_lo * (1.0 / HEAD_DIM) + EPS)
                r_hi = lax.rsqrt(s_hi * (1.0 / HEAD_DIM) + EPS)
                scr[p] = blk * jnp.where(lo, r_lo, r_hi)
                r_tile = jnp.where(lane == 2 * p, r_lo, r_tile)
                r_tile = jnp.where(lane == 2 * p + 1, r_hi, r_tile)
            rscr[0] = r_tile
            emit(scr, out_refs, BF16)
            emit(rscr, r_refs, F32)

    piece = jax.ShapeDtypeStruct((S, CONV_W), BF16)
    pspec = pl.BlockSpec((tm, CONV_W), lambda i: (i, 0))
    wide = [_view_shape(S, d, ATTN_W, BF16) for d in DILATIONS]
    wide_specs = [_view_spec(tm, d, ATTN_W) for d in DILATIONS]
    stat = [_view_shape(S, d, LANES, F32) for d in DILATIONS]
    stat_specs = [_view_spec(tm, d, LANES) for d in DILATIONS]
    outs = pl.pallas_call(
        body, name="fwd_proj", grid=(S // tm,),
        out_shape=[piece] * 5 + wide * 3 + stat * 2,
        in_specs=[pl.BlockSpec((tm, D_MODEL), lambda i: (i, 0)),
                  pl.BlockSpec((1, D_MODEL), lambda i: (0, 0)),
                  pl.BlockSpec(wg.shape, lambda i: (0, 0, 0), pipeline_mode=pl.Buffered(1))],
        out_specs=[pspec] * 5 + wide_specs * 3 + stat_specs * 2,
        scratch_shapes=[_tile_scratch(tm, ATTN_W), _tile_scratch(tm, LANES)],
        compiler_params=_params(("arbitrary",), 48),
    )(x, norm_w.reshape(1, D_MODEL), wg)
    u, gb, gc, zc, za = outs[:5]
    q_hat, k_hat, v, rq, rk = (outs[5 + 3 * n:8 + 3 * n] for n in range(5))
    return u, gb, gc, zc, za, q_hat, k_hat, v, rq, rk


def _bias_tables(rel_bias, bmat_t):
    def body(rb_ref, b_ref, out_ref):
        h = pl.program_id(1)
        b = b_ref[0]
        t = jnp.full((KB, QB), NEG, F32)
        for bk in range(N_BUCKETS):
            t = jnp.where(b == bk, rb_ref[bk, h], t)
        out_ref[0, 0] = t

    return pl.pallas_call(
        body, name="bias_tables", grid=(len(DILATIONS), N_HEADS),
        out_shape=jax.ShapeDtypeStruct((len(DILATIONS), N_HEADS, KB, QB), F32),
        in_specs=[pl.BlockSpec(memory_space=pltpu.SMEM),
                  pl.BlockSpec((1, KB, QB), lambda g, h: (g, 0, 0))],
        out_specs=pl.BlockSpec((1, 1, KB, QB), lambda g, h: (g, h, 0, 0)),
        compiler_params=_params(("arbitrary", "arbitrary"), 16),
    )(rel_bias, bmat_t)


def _bias_grad(ds_acc, bmat_t):
    def body(ds_ref, b_ref, out_ref):
        lane = _lane((1, LANES))
        row = lax.broadcasted_iota(jnp.int32, (8, LANES), 0)
        out = jnp.zeros((8, LANES), F32)
        for e in range(2):
            vec = jnp.zeros((1, LANES), F32)
            for g in range(len(DILATIONS)):
                b = b_ref[g]
                ds = ds_ref[g, 0, :, e * QB:(e + 1) * QB]
                for bk in range(N_BUCKETS):
                    s = jnp.sum(jnp.where(b == bk, ds, 0.0), axis=-1, keepdims=True)
                    s = jnp.sum(s, axis=0, keepdims=True)
                    vec = vec + jnp.where(lane == bk, s, 0.0)
            out = jnp.where(row == e, vec, out)
        out_ref[0] = out

    return pl.pallas_call(
        body, name="bias_grad", grid=(N_PAIRS,),
        out_shape=jax.ShapeDtypeStruct((N_PAIRS, 8, LANES), F32),
        in_specs=[pl.BlockSpec((len(DILATIONS), 1, KB, 2 * QB), lambda p: (0, p, 0, 0)),
                  pl.BlockSpec((len(DILATIONS), KB, QB), lambda p: (0, 0, 0))],
        out_specs=pl.BlockSpec((1, 8, LANES), lambda p: (p, 0, 0)),
        compiler_params=_params(("arbitrary",), 16),
    )(ds_acc, bmat_t)


def _halo_specs(width, rows, T, L, cols_of):
    per = T // rows
    last = L // rows - 1
    prev = pl.BlockSpec((rows, width), lambda r, i: (jnp.maximum(i * per - 1, 0), cols_of(r)))
    nxt = pl.BlockSpec((rows, width), lambda r, i: (jnp.minimum((i + 1) * per, last), cols_of(r)))
    return prev, nxt


def _attn_fwd(g, dil, qv, kv, vv, gain, tables):
    L = qv.shape[0]
    T = min(1024, L)
    n_sub = T // QB

    def body(q_ref, km_ref, kp_ref, kn_ref, vm_ref, vp_ref, vn_ref, gain_ref, bias_ref, o_ref, lse_ref,
             kwin, vwin, s_scr, p_scr):
        i = pl.program_id(1)
        kwin[0:HALF] = kp_ref[...]
        kwin[HALF:HALF + T] = km_ref[...]
        kwin[HALF + T:] = kn_ref[...]
        vwin[0:HALF] = vp_ref[...]
        vwin[HALF:HALF + T] = vm_ref[...]
        vwin[HALF + T:] = vn_ref[...]
        lo = _lane((QB, LANES)) < HEAD_DIM
        krow = lax.broadcasted_iota(jnp.int32, (KB, 1), 0)
        chan_lo = lax.broadcasted_iota(jnp.int32, (LANES, 1), 0) < HEAD_DIM
        hrow = lax.broadcasted_iota(jnp.int32, (LANES, QB), 0)

        def sub(j, carry):
            r0 = pl.multiple_of(j * QB, QB)
            kpos = i * T + j * QB - HALF + krow
            kvalid = (kpos >= 0) & (kpos < L)
            lse_rows = jnp.zeros((LANES, QB), F32)
            cols = [slice(p * LANES, (p + 1) * LANES) for p in range(N_PAIRS)]
            for p, cs in enumerate(cols):
                qf = q_ref[pl.ds(r0, QB), cs].astype(F32) * gain_ref[:, cs]
                q2 = jnp.concatenate([jnp.where(lo, qf, 0.0), jnp.where(lo, 0.0, qf)], axis=0).astype(BF16)
                s_scr[p] = lax.dot_general(kwin[pl.ds(r0, KB), cs], q2, (((1,), (1,)), ((), ())),
                                           preferred_element_type=F32)
            inv_l = []
            for p in range(N_PAIRS):
                bias2 = jnp.concatenate([bias_ref[0, 2 * p], bias_ref[0, 2 * p + 1]], axis=1)
                logits = jnp.where(kvalid, s_scr[p] + bias2, NEG)
                m = jnp.max(logits, axis=0, keepdims=True)
                pt = jnp.exp(logits - m)
                l = jnp.sum(pt, axis=0, keepdims=True)
                p_scr[p] = pt.astype(BF16)
                inv_l.append(1.0 / l)
                lse2 = m + jnp.log(l)
                lse_rows = jnp.where(hrow == 2 * p, lse2[:, 0:QB], lse_rows)
                lse_rows = jnp.where(hrow == 2 * p + 1, lse2[:, QB:2 * QB], lse_rows)
            for p, cs in enumerate(cols):
                s_scr[p, 0:LANES, :] = lax.dot_general(vwin[pl.ds(r0, KB), cs], p_scr[p], (((0,), (0,)), ((), ())),
                                                       preferred_element_type=F32) * inv_l[p]
            for p, cs in enumerate(cols):
                ot2 = s_scr[p, 0:LANES, :]
                ot = jnp.where(chan_lo, ot2[:, 0:QB], ot2[:, QB:2 * QB])
                o_ref[pl.ds(r0, QB), cs] = ot.T.astype(BF16)
            lse_ref[pl.ds(r0, QB), :] = lse_rows.T
            return carry

        lax.fori_loop(0, n_sub, sub, 0)

    main = pl.BlockSpec((T, ATTN_W), lambda r, i: (i, r))
    prev, nxt = _halo_specs(ATTN_W, HALF, T, L, lambda r: r)
    o_g, lse_g = pl.pallas_call(
        body, name=f"attn_fwd_d{dil}", grid=(dil, L // T),
        out_shape=(jax.ShapeDtypeStruct((L, dil * ATTN_W), BF16), jax.ShapeDtypeStruct((L, dil * LANES), F32)),
        in_specs=[main, main, prev, nxt, main, prev, nxt,
                  pl.BlockSpec((1, ATTN_W), lambda r, i: (0, 0)),
                  pl.BlockSpec((1, N_HEADS, KB, QB), lambda r, i: (g, 0, 0, 0))],
        out_specs=(main, pl.BlockSpec((T, LANES), lambda r, i: (i, r))),
        scratch_shapes=[pltpu.VMEM((T + 2 * HALF, ATTN_W), BF16), pltpu.VMEM((T + 2 * HALF, ATTN_W), BF16),
                        pltpu.VMEM((N_PAIRS, KB, 2 * QB), F32), pltpu.VMEM((N_PAIRS, KB, 2 * QB), BF16)],
        compiler_params=_params(("arbitrary", "arbitrary"), 40),
    )(qv, kv, kv, kv, vv, vv, vv, gain, tables)
    return o_g, lse_g


def _row_halo_specs(width, tm, S, col=0):
    rows = 16
    per = tm // rows
    last = S // rows - 1
    prev = pl.BlockSpec((rows, width), lambda i: (jnp.maximum(i * per - 1, 0), col))
    nxt = pl.BlockSpec((rows, width), lambda i: (jnp.minimum((i + 1) * per, last), col))
    return prev, nxt


def _shift_rows(a, prev_row, next_row, tm):
    row = lax.broadcasted_iota(jnp.int32, a.shape, 0)
    a_m1 = jnp.where(row == 0, prev_row, pltpu.roll(a, 1, 0))
    a_p1 = jnp.where(row == tm - 1, next_row, pltpu.roll(a, tm - 1, 0))
    return a_m1, a_p1


def _combine_gates(o_parts, lse_parts, u, gb, gc, zc, za, conv_w, conv_b):
    S = u.shape[0]
    tm = 512
    n_t = S // tm

    def body(o1, o2, o3, l1, l2, l3, u_ref, up_ref, un_ref, gb_ref, gc_ref, gcp_ref, gcn_ref, zc_ref, za_ref,
             cw_ref, cb_ref, y_ref, o_ref, lse_ref, lse2_ref, lse3_ref, so2, so3, sl2, sl3, slse):
        i = pl.program_id(0)
        for blk, scr, dil in ((o2, so2, DILATIONS[1]), (o3, so3, DILATIONS[2]),
                              (l2, sl2, DILATIONS[1]), (l3, sl3, DILATIONS[2])):
            _from_view(blk, scr, dil)
        ls = [l1[...], sl2[0], sl3[0]]
        lmax = jnp.maximum(jnp.maximum(ls[0], ls[1]), ls[2])
        es = [jnp.exp(l - lmax) for l in ls]
        den = es[0] + es[1] + es[2]
        slse[0] = lmax + jnp.log(den)
        lse_ref[...] = slse[0]
        _to_view(slse, lse2_ref, DILATIONS[1], F32)
        _to_view(slse, lse3_ref, DILATIONS[2], F32)
        inv = 1.0 / den
        ws = [e * inv for e in es]
        for p in range(N_PAIRS):
            cs = slice(p * LANES, (p + 1) * LANES)
            acc = jnp.zeros((tm, LANES), F32)
            spread = _head_to_lanes(p)
            for w, o_g in zip(ws, (o1[:, cs].astype(F32), so2[p], so3[p])):
                acc = acc + _split_dot(w, spread) * o_g
            o_ref[:, cs] = acc.astype(BF16)
            za = za_ref[:, cs].astype(F32)
            y_ref[:, CONV_W + p * LANES:CONV_W + (p + 1) * LANES] = (acc * za * _sigmoid(za)).astype(BF16)
        a = gc_ref[...].astype(F32) * u_ref[...].astype(F32)
        a_prev = gcp_ref[15:16, :].astype(F32) * up_ref[15:16, :].astype(F32) * (i > 0).astype(F32)
        a_next = gcn_ref[0:1, :].astype(F32) * un_ref[0:1, :].astype(F32) * (i < n_t - 1).astype(F32)
        a_m1, a_p1 = _shift_rows(a, a_prev, a_next, tm)
        z3 = cw_ref[0:1, :] * a_m1 + cw_ref[1:2, :] * a + cw_ref[2:3, :] * a_p1
        zc = zc_ref[...].astype(F32)
        y_ref[:, 0:CONV_W] = (gb_ref[...].astype(F32) * (z3 + cb_ref[...]) * zc * _sigmoid(zc)).astype(BF16)

    pspec = pl.BlockSpec((tm, CONV_W), lambda i: (i, 0))
    hp, hn = _row_halo_specs(CONV_W, tm, S)
    wide_specs = [_view_spec(tm, d, ATTN_W) for d in DILATIONS]
    stat_specs = [_view_spec(tm, d, LANES) for d in DILATIONS]
    outs = pl.pallas_call(
        body, name="combine_gates", grid=(n_t,),
        out_shape=[jax.ShapeDtypeStruct((S, D_MODEL), BF16), jax.ShapeDtypeStruct((S, ATTN_W), BF16)]
        + [_view_shape(S, d, LANES, F32) for d in DILATIONS],
        in_specs=wide_specs + stat_specs + [pspec, hp, hn, pspec, pspec, hp, hn, pspec, pspec,
                                            pl.BlockSpec((3, CONV_W), lambda i: (0, 0)),
                                            pl.BlockSpec((1, CONV_W), lambda i: (0, 0))],
        out_specs=[pl.BlockSpec((tm, D_MODEL), lambda i: (i, 0)), pspec] + stat_specs,
        scratch_shapes=[_tile_scratch(tm, ATTN_W), _tile_scratch(tm, ATTN_W),
                        _tile_scratch(tm, LANES), _tile_scratch(tm, LANES), _tile_scratch(tm, LANES)],
        compiler_params=_params(("arbitrary",), 40),
    )(*o_parts, *lse_parts, u, u, u, gb, gc, gc, gc, zc, za, conv_w, conv_b.reshape(1, CONV_W))
    return outs[0], outs[1], outs[2:]


def _out_proj(y, x, target, wo):
    S = x.shape[0]
    tm = 512

    def body(y_ref, x_ref, t_ref, wo_ref, g_ref, dy_ref, dwo_ref, loss_ref):
        i = pl.program_id(0)

        @pl.when(i == 0)
        def _():
            dwo_ref[...] = jnp.zeros_like(dwo_ref)
            loss_ref[...] = jnp.zeros_like(loss_ref)

        yv = y_ref[...]
        wo_v = wo_ref[...]
        err = x_ref[...] + jnp.dot(yv, wo_v, preferred_element_type=F32) - t_ref[...]
        e2 = (err * err).reshape(tm // 8, 8, D_MODEL).sum(axis=0)
        part = e2[:, 0:LANES]
        for k in range(1, D_MODEL // LANES):
            part = part + e2[:, k * LANES:(k + 1) * LANES]
        loss_ref[...] += part
        gv = (err * (1.0 / D_MODEL)).astype(BF16)
        g_ref[...] = gv
        dy_ref[...] = lax.dot_general(gv, wo_v, (((1,), (1,)), ((), ())), preferred_element_type=F32).astype(BF16)
        dwo_ref[...] += lax.dot_general(yv, gv, (((0,), (0,)), ((), ())), preferred_element_type=F32)

    tile = pl.BlockSpec((tm, D_MODEL), lambda i: (i, 0))
    return pl.pallas_call(
        body, name="out_proj", grid=(S // tm,),
        out_shape=(jax.ShapeDtypeStruct((S, D_MODEL), BF16), jax.ShapeDtypeStruct((S, D_MODEL), BF16),
                   jax.ShapeDtypeStruct((D_MODEL, D_MODEL), F32), jax.ShapeDtypeStruct((8, LANES), F32)),
        in_specs=[tile, tile, tile, pl.BlockSpec((D_MODEL, D_MODEL), lambda i: (0, 0), pipeline_mode=pl.Buffered(1))],
        out_specs=(tile, tile, pl.BlockSpec((D_MODEL, D_MODEL), lambda i: (0, 0)),
                   pl.BlockSpec((8, LANES), lambda i: (0, 0))),
        compiler_params=_params(("arbitrary",), 48),
    )(y, x, target, wo)


def _gates_bwd(dy, u, gb, gc, zc, za, o, conv_w, conv_b):
    S = u.shape[0]
    tm = 512
    n_t = S // tm

    def body(dy_ref, dyp_ref, dyn_ref, u_ref, up_ref, un_ref, gb_ref, gbp_ref, gbn_ref, gc_ref, gcp_ref, gcn_ref,
             zc_ref, zcp_ref, zcn_ref, za_ref, o_ref, cw_ref, cb_ref,
             du_ref, dgb_ref, dgc_ref, dzc_ref, dza_ref, small_ref,
             do_ref, do2_ref, do3_ref, delta_ref, delta2_ref, delta3_ref, sdo, sdelta):
        i = pl.program_id(0)

        @pl.when(i == 0)
        def _():
            small_ref[...] = jnp.zeros_like(small_ref)

        def dz3_of(dyc, zc, gb):
            return dyc * zc * _sigmoid(zc) * gb

        f = lambda ref, rows=slice(None): ref[rows, :].astype(F32)
        has_prev = (i > 0).astype(F32)
        has_next = (i < n_t - 1).astype(F32)
        pr, nr = slice(15, 16), slice(0, 1)
        uv, gbv, gcv, zcv = f(u_ref), f(gb_ref), f(gc_ref), f(zc_ref)
        dyc = dy_ref[:, 0:CONV_W].astype(F32)
        a = gcv * uv
        a_m1, a_p1 = _shift_rows(a, f(gcp_ref, pr) * f(up_ref, pr) * has_prev,
                                 f(gcn_ref, nr) * f(un_ref, nr) * has_next, tm)
        w0, w1, w2 = cw_ref[0:1, :], cw_ref[1:2, :], cw_ref[2:3, :]
        cvp = w0 * a_m1 + w1 * a + w2 * a_p1 + cb_ref[...]
        sz = _sigmoid(zcv)
        dcv = dyc * zcv * sz
        dzc_ref[...] = (dyc * gbv * cvp * (sz * (1.0 + zcv * (1.0 - sz)))).astype(BF16)
        dgb_ref[...] = (dcv * cvp).astype(BF16)
        dz3 = dcv * gbv
        dz3_prev = dz3_of(f(dyp_ref, pr), f(zcp_ref, pr), f(gbp_ref, pr)) * has_prev
        dz3_next = dz3_of(f(dyn_ref, nr), f(zcn_ref, nr), f(gbn_ref, nr)) * has_next
        dz3_m1, dz3_p1 = _shift_rows(dz3, dz3_prev, dz3_next, tm)
        da = w0 * dz3_p1 + w1 * dz3 + w2 * dz3_m1
        du_ref[...] = (da * gcv).astype(BF16)
        dgc_ref[...] = (da * uv).astype(BF16)
        csum = lambda t: jnp.sum(t, axis=0, keepdims=True)
        small_ref[0:1, :] += csum(dz3 * a_m1)
        small_ref[1:2, :] += csum(dz3 * a)
        small_ref[2:3, :] += csum(dz3 * a_p1)
        small_ref[3:4, :] += csum(dz3)

        delta = jnp.zeros((tm, LANES), F32)
        for p in range(N_PAIRS):
            cs = slice(p * LANES, (p + 1) * LANES)
            dya = dy_ref[:, CONV_W + p * LANES:CONV_W + (p + 1) * LANES].astype(F32)
            zav = za_ref[:, cs].astype(F32)
            ov = o_ref[:, cs].astype(F32)
            sa = _sigmoid(zav)
            d_o = dya * zav * sa
            sdo[p] = d_o
            do_ref[:, cs] = d_o.astype(BF16)
            dza_ref[:, cs] = (dya * ov * (sa * (1.0 + zav * (1.0 - sa)))).astype(BF16)
            delta = delta + _split_dot(d_o * ov, _lanes_to_head(p, 1.0))
        sdelta[0] = delta
        delta_ref[...] = delta
        for dil, wide_ref, stat_ref in zip(VIEW_DILATIONS, (do2_ref, do3_ref), (delta2_ref, delta3_ref)):
            _to_view(sdo, wide_ref, dil, BF16)
            _to_view(sdelta, stat_ref, dil, F32)

    pspec = pl.BlockSpec((tm, CONV_W), lambda i: (i, 0))
    hp, hn = _row_halo_specs(CONV_W, tm, S)
    piece = jax.ShapeDtypeStruct((S, CONV_W), BF16)
    outs = pl.pallas_call(
        body, name="gates_bwd", grid=(n_t,),
        out_shape=[piece] * 5 + [jax.ShapeDtypeStruct((8, CONV_W), F32)]
        + [_view_shape(S, d, ATTN_W, BF16) for d in DILATIONS] + [_view_shape(S, d, LANES, F32) for d in DILATIONS],
        in_specs=[pl.BlockSpec((tm, D_MODEL), lambda i: (i, 0)), hp, hn,
                  pspec, hp, hn, pspec, hp, hn, pspec, hp, hn, pspec, hp, hn, pspec, pspec,
                  pl.BlockSpec((3, CONV_W), lambda i: (0, 0)), pl.BlockSpec((1, CONV_W), lambda i: (0, 0))],
        out_specs=[pspec] * 5 + [pl.BlockSpec((8, CONV_W), lambda i: (0, 0))]
        + [_view_spec(tm, d, ATTN_W) for d in DILATIONS] + [_view_spec(tm, d, LANES) for d in DILATIONS],
        scratch_shapes=[_tile_scratch(tm, ATTN_W), _tile_scratch(tm, LANES)],
        compiler_params=_params(("arbitrary",), 40),
    )(dy, dy, dy, u, u, u, gb, gb, gb, gc, gc, gc, zc, zc, zc, za, o, conv_w, conv_b.reshape(1, CONV_W))
    return outs[:5], outs[5], outs[6:9], outs[9:12]


def _attn_bwd(g, dil, qv, kv, vv, dov, lv, dv_, rqv, rkv, gain, tables):
    L = qv.shape[0]
    T = min(1024, L)
    n_sub = T // QB + 1
    QW = T + 2 * HALF
    KW = T + 2 * QB

    def body(qm_ref, qp_ref, qn_ref, dom_ref, dop_ref, don_ref, km_ref, kp_ref, kn_ref, vm_ref, vp_ref, vn_ref,
             lm_ref, lp_ref, ln_ref, dm_ref, dp_ref, dn_ref, rqm_ref, rqp_ref, rqn_ref, rk_ref,
             gain_ref, bias_ref,
             dq_ref, dk_ref, dv_ref, ds_ref, a_ref,
             qwin, dowin, kwin, vwin, lwin, dwin, rqwin, dqwin, dkacc, dvacc,
             s_scr, d_scr, p_scr, ds_scr, q2_scr, do2_scr):
        r = pl.program_id(0)
        i = pl.program_id(1)

        @pl.when((r == 0) & (i == 0))
        def _():
            ds_ref[...] = jnp.zeros_like(ds_ref)
            a_ref[...] = jnp.zeros_like(a_ref)

        for win, (pr, mn, nx), h_rows in ((qwin, (qp_ref, qm_ref, qn_ref), HALF),
                                          (dowin, (dop_ref, dom_ref, don_ref), HALF),
                                          (lwin, (lp_ref, lm_ref, ln_ref), HALF),
                                          (dwin, (dp_ref, dm_ref, dn_ref), HALF),
                                          (rqwin, (rqp_ref, rqm_ref, rqn_ref), HALF),
                                          (kwin, (kp_ref, km_ref, kn_ref), QB),
                                          (vwin, (vp_ref, vm_ref, vn_ref), QB)):
            win[0:h_rows] = pr[...]
            win[h_rows:h_rows + T] = mn[...]
            win[h_rows + T:] = nx[...]
        dkacc[...] = jnp.zeros_like(dkacc)
        dvacc[...] = jnp.zeros_like(dvacc)

        lo = _lane((QB, LANES)) < HEAD_DIM
        same_head = _same_head()
        krow = lax.broadcasted_iota(jnp.int32, (KB, 1), 0)
        qcol = _lane((1, 2 * QB)) % QB
        row = lax.broadcasted_iota(jnp.int32, (QB, 1), 0)
        chan_lo = lax.broadcasted_iota(jnp.int32, (LANES, 1), 0) < HEAD_DIM

        def sub(j, carry):
            r0 = pl.multiple_of(j * QB, QB)
            kpos = i * T - QB + j * QB + krow
            qpos = i * T - HALF + j * QB + qcol
            valid = (kpos >= 0) & (kpos < L) & (qpos >= 0) & (qpos < L)
            owned_t = ((qpos >= i * T) & (qpos < i * T + T)).astype(F32)
            qpos_c = i * T - HALF + j * QB + row
            owned = ((qpos_c >= i * T) & (qpos_c < i * T + T)).astype(F32)
            lse_rows = lwin[pl.ds(r0, QB), :].T
            delta_rows = dwin[pl.ds(r0, QB), :].T
            rq_t = rqwin[pl.ds(r0, QB), :]
            cols = [slice(p * LANES, (p + 1) * LANES) for p in range(N_PAIRS)]
            nt = (((1,), (1,)), ((), ()))
            for p, cs in enumerate(cols):
                qf = qwin[pl.ds(r0, QB), cs].astype(F32) * gain_ref[:, cs]
                q2_scr[p] = jnp.concatenate([jnp.where(lo, qf, 0.0), jnp.where(lo, 0.0, qf)], axis=0).astype(BF16)
                dov = dowin[pl.ds(r0, QB), cs]
                zero = jnp.zeros_like(dov)
                do2_scr[p] = jnp.concatenate([jnp.where(lo, dov, zero), jnp.where(lo, zero, dov)], axis=0)
                s_scr[p] = lax.dot_general(kwin[pl.ds(r0, KB), cs], q2_scr[p], nt, preferred_element_type=F32)
                d_scr[p] = lax.dot_general(vwin[pl.ds(r0, KB), cs], do2_scr[p], nt, preferred_element_type=F32)
            for p in range(N_PAIRS):
                bias2 = jnp.concatenate([bias_ref[0, 2 * p], bias_ref[0, 2 * p + 1]], axis=1)
                lse2 = jnp.concatenate([lse_rows[2 * p:2 * p + 1, :], lse_rows[2 * p + 1:2 * p + 2, :]], axis=1)
                delta2 = jnp.concatenate([delta_rows[2 * p:2 * p + 1, :], delta_rows[2 * p + 1:2 * p + 2, :]], axis=1)
                pt = jnp.where(valid, jnp.exp(s_scr[p] + bias2 - lse2), 0.0)
                dst = pt * (d_scr[p] - delta2)
                ds_ref[p] += dst * owned_t
                p_scr[p] = pt.astype(BF16)
                ds_scr[p] = dst.astype(BF16)
            for p, cs in enumerate(cols):
                dvacc[pl.ds(r0, KB), cs] += jnp.dot(p_scr[p], do2_scr[p], preferred_element_type=F32)
                dkacc[pl.ds(r0, KB), cs] += jnp.dot(ds_scr[p], q2_scr[p], preferred_element_type=F32)
                s_scr[p, 0:LANES, :] = lax.dot_general(kwin[pl.ds(r0, KB), cs], ds_scr[p], (((0,), (0,)), ((), ())),
                                                       preferred_element_type=F32)
            for p, cs in enumerate(cols):
                g2 = gain_ref[:, cs]
                q_f = qwin[pl.ds(r0, QB), cs].astype(F32)
                dqt2 = s_scr[p, 0:LANES, :]
                dqm2 = jnp.where(chan_lo, dqt2[:, 0:QB], dqt2[:, QB:2 * QB]).T
                a_ref[p] += (dqm2 * q_f * owned).reshape(QB // 8, 8, LANES).sum(axis=0)
                dqh = dqm2 * g2
                mean_c = _split_dot(dqh * q_f, same_head) * (1.0 / HEAD_DIM)
                rq2 = _split_dot(rq_t, _head_to_lanes(p))
                dqwin[pl.ds(r0, QB), cs] = (rq2 * (dqh - q_f * mean_c)).astype(BF16)
            return carry

        lax.fori_loop(0, n_sub, sub, 0)
        dq_ref[...] = dqwin[HALF:HALF + T]

        def finish(j, carry):
            r0 = pl.multiple_of(j * QB, QB)
            rk_t = rk_ref[pl.ds(r0, QB), :]
            for p in range(N_PAIRS):
                cs = slice(p * LANES, (p + 1) * LANES)
                dkh = dkacc[pl.ds(QB + r0, QB), cs]
                k_f = km_ref[pl.ds(r0, QB), cs].astype(F32)
                mean_c = _split_dot(dkh * k_f, same_head) * (1.0 / HEAD_DIM)
                rk2 = _split_dot(rk_t, _head_to_lanes(p))
                dk_ref[pl.ds(r0, QB), cs] = (rk2 * (dkh - k_f * mean_c)).astype(BF16)
                dv_ref[pl.ds(r0, QB), cs] = dvacc[pl.ds(QB + r0, QB), cs].astype(BF16)
            return carry

        lax.fori_loop(0, T // QB, finish, 0)

    main = pl.BlockSpec((T, ATTN_W), lambda r, i: (i, r))
    smain = pl.BlockSpec((T, LANES), lambda r, i: (i, r))
    q_prev, q_next = _halo_specs(ATTN_W, HALF, T, L, lambda r: r)
    k_prev, k_next = _halo_specs(ATTN_W, QB, T, L, lambda r: r)
    s_prev, s_next = _halo_specs(LANES, HALF, T, L, lambda r: r)
    piece = jax.ShapeDtypeStruct((L, dil * ATTN_W), BF16)
    return pl.pallas_call(
        body, name=f"attn_bwd_d{dil}", grid=(dil, L // T),
        out_shape=(piece, piece, piece, jax.ShapeDtypeStruct((N_PAIRS, KB, 2 * QB), F32),
                   jax.ShapeDtypeStruct((N_PAIRS, 8, LANES), F32)),
        in_specs=[main, q_prev, q_next, main, q_prev, q_next, main, k_prev, k_next, main, k_prev, k_next,
                  smain, s_prev, s_next, smain, s_prev, s_next, smain, s_prev, s_next, smain,
                  pl.BlockSpec((1, ATTN_W), lambda r, i: (0, 0)),
                  pl.BlockSpec((1, N_HEADS, KB, QB), lambda r, i: (g, 0, 0, 0))],
        out_specs=(main, main, main, pl.BlockSpec((N_PAIRS, KB, 2 * QB), lambda r, i: (0, 0, 0)),
                   pl.BlockSpec((N_PAIRS, 8, LANES), lambda r, i: (0, 0, 0))),
        scratch_shapes=[pltpu.VMEM((QW, ATTN_W), BF16), pltpu.VMEM((QW, ATTN_W), BF16),
                        pltpu.VMEM((KW, ATTN_W), BF16), pltpu.VMEM((KW, ATTN_W), BF16),
                        pltpu.VMEM((QW, LANES), F32), pltpu.VMEM((QW, LANES), F32), pltpu.VMEM((QW, LANES), F32),
                        pltpu.VMEM((QW, ATTN_W), BF16),
                        pltpu.VMEM((KW, ATTN_W), F32), pltpu.VMEM((KW, ATTN_W), F32),
                        pltpu.VMEM((N_PAIRS, KB, 2 * QB), F32), pltpu.VMEM((N_PAIRS, KB, 2 * QB), F32),
                        pltpu.VMEM((N_PAIRS, KB, 2 * QB), BF16), pltpu.VMEM((N_PAIRS, KB, 2 * QB), BF16),
                        pltpu.VMEM((N_PAIRS, 2 * QB, LANES), BF16), pltpu.VMEM((N_PAIRS, 2 * QB, LANES), BF16)],
        compiler_params=_params(("arbitrary", "arbitrary"), 56),
    )(qv, qv, qv, dov, dov, dov, kv, kv, kv, vv, vv, vv, lv, lv, lv, dv_, dv_, dv_, rqv, rqv, rqv, rkv, gain, tables)


def _proj_bwd(x, d_out, norm_w, wg, pieces):
    S = x.shape[0]
    tm = 512
    flat = [share for p in pieces for share in p]
    n_t = S // tm

    def body(*refs):
        x_ref, g_ref, nw_ref, wg_ref = refs[:4]
        piece_refs = refs[4:4 + len(flat)]
        gx_ref, dnw_ref, ht_ref, dproj_ref, sview = refs[4 + len(flat):]
        i = pl.program_id(0)

        @pl.when(i == 0)
        def _():
            dnw_ref[...] = jnp.zeros_like(dnw_ref)

        xv = x_ref[...]
        r = lax.rsqrt(jnp.mean(xv * xv, axis=-1, keepdims=True) + EPS)
        xh = xv * r
        nw = nw_ref[...]
        ht_ref[...] = (xh * nw).T.astype(BF16)
        dh = jnp.zeros((tm, D_MODEL), F32)
        k = 0
        for j in range(8):
            shares = pieces[j]
            if len(shares) == 1:
                dp = piece_refs[k][...]
            else:
                dp = jnp.zeros((tm, CONV_W), F32)
                for t, (_, dil) in enumerate(shares):
                    if dil == 1:
                        dp = dp + piece_refs[k + t][...].astype(F32)
                    else:
                        _from_view(piece_refs[k + t], sview, dil)
                        dp = dp + _scratch_value(sview)
                dp = dp.astype(BF16)
            k += len(shares)
            dproj_ref[j] = dp
            dh = dh + lax.dot_general(dp, wg_ref[j], (((1,), (1,)), ((), ())), preferred_element_type=F32)
        dnw_ref[...] += (dh * xh).reshape(tm // 8, 8, D_MODEL).sum(axis=0)
        dxh = dh * nw
        mean_c = jnp.mean(dxh * xh, axis=-1, keepdims=True)
        gx_ref[...] = g_ref[...].astype(F32) + r * (dxh - xh * mean_c)

    tile = pl.BlockSpec((tm, D_MODEL), lambda i: (i, 0))
    return pl.pallas_call(
        body, name="proj_bwd", grid=(n_t,),
        out_shape=(jax.ShapeDtypeStruct((S, D_MODEL), F32), jax.ShapeDtypeStruct((8, D_MODEL), F32),
                   jax.ShapeDtypeStruct((D_MODEL, S), BF16), jax.ShapeDtypeStruct((8, S, CONV_W), BF16)),
        in_specs=[tile, tile, pl.BlockSpec((1, D_MODEL), lambda i: (0, 0)),
                  pl.BlockSpec(wg.shape, lambda i: (0, 0, 0), pipeline_mode=pl.Buffered(1))]
        + [_view_spec(tm, dil, CONV_W) for _, dil in flat],
        out_specs=(tile, pl.BlockSpec((8, D_MODEL), lambda i: (0, 0)),
                   pl.BlockSpec((D_MODEL, tm), lambda i: (0, i)), pl.BlockSpec((8, tm, CONV_W), lambda i: (0, i, 0))),
        scratch_shapes=[_tile_scratch(tm, CONV_W)],
        compiler_params=_params(("arbitrary",), 56),
    )(x, d_out, norm_w.reshape(1, D_MODEL), wg, *[a for a, _ in flat])


def _dw_exchange(ht, dproj, gw_out, small):
    D, S = ht.shape
    tk = 1024
    n_t = S // tk
    me_outer = 4 * lax.axis_index("x") + 2 * lax.axis_index("y") + lax.axis_index("c")
    order = ((me_outer + 1 + jnp.arange(N_DEV, dtype=jnp.int32)) % N_DEV).astype(jnp.int32)

    def body(order_ref, ht_hbm, dp_ref, gout_ref, sm_ref, rin_ref, rout_ref, rsm_ref,
             acc, sbuf, ht_vmem, in_send, in_recv, side_send, side_recv, local_sems, ht_sems):
        s = pl.program_id(0)
        t = pl.program_id(1)
        x, y, c = lax.axis_index("x"), lax.axis_index("y"), lax.axis_index("c")
        me = 4 * x + 2 * y + c

        def ht_copy(j):
            return pltpu.make_async_copy(ht_hbm.at[:, j * tk:(j + 1) * tk], ht_vmem.at[j], ht_sems.at[j])

        def in_copy(k):
            to = (me + 1 + k) % N_DEV
            return pltpu.make_async_remote_copy(
                src_ref=sbuf.at[k % 2], dst_ref=rin_ref.at[me], send_sem=in_send.at[k], recv_sem=in_recv.at[k],
                device_id=(to // 4, (to // 2) % 2, to % 2), device_id_type=MESH)

        def in_landing(k):
            frm = (me + 2 * N_DEV - 1 - k) % N_DEV
            return pltpu.make_async_remote_copy(
                src_ref=sbuf.at[0], dst_ref=rin_ref.at[frm], send_sem=in_send.at[k], recv_sem=in_recv.at[k],
                device_id=(x, y, c), device_id_type=MESH)

        def side_copies():
            local = [pltpu.make_async_copy(gout_ref.at[me], rout_ref.at[me], local_sems.at[0]),
                     pltpu.make_async_copy(sm_ref, rsm_ref.at[me], local_sems.at[1])]
            remote = []
            for k in range(1, N_DEV):
                px = 1 - x if k & 4 else x
                py = 1 - y if k & 2 else y
                pc = 1 - c if k & 1 else c
                peer = 4 * px + 2 * py + pc
                for a, (src, dst) in enumerate(((gout_ref.at[peer], rout_ref.at[me]), (sm_ref, rsm_ref.at[me]))):
                    remote.append(pltpu.make_async_remote_copy(
                        src_ref=src, dst_ref=dst, send_sem=side_send.at[a * 7 + k - 1],
                        recv_sem=side_recv.at[a * 7 + k - 1], device_id=(px, py, pc), device_id_type=MESH))
            return local, remote

        @pl.when((s == 0) & (t == 0))
        def _():
            for j in range(n_t):
                ht_copy(j).start()
            local, remote = side_copies()
            for cp in local + remote:
                cp.start()

        for j in range(n_t):
            @pl.when((s == 0) & (t == j))
            def _(j=j):
                ht_copy(j).wait()

        @pl.when(t == 0)
        def _():
            acc[...] = jnp.zeros_like(acc)

        acc[...] += jnp.dot(ht_vmem[t], dp_ref[0], preferred_element_type=F32)

        for k in range(N_DEV):
            @pl.when((s == k) & (t == n_t - 1))
            def _(k=k):
                if k >= 2:
                    in_copy(k - 2).wait_send()
                sbuf[k % 2] = acc[...].astype(BF16)
                if k < N_DEV - 1:
                    in_copy(k).start()
                else:
                    own = pltpu.make_async_copy(sbuf.at[k % 2], rin_ref.at[me], local_sems.at[2])
                    own.start()
                    in_copy(k - 1).wait_send()
                    for j in range(N_DEV - 1):
                        in_landing(j).wait_recv()
                    local, remote = side_copies()
                    for cp in remote:
                        cp.wait_recv()
                    for cp in remote:
                        cp.wait_send()
                    for cp in local:
                        cp.wait()
                    own.wait()

    hbm = pl.BlockSpec(memory_space=pl.ANY)
    grid_spec = pltpu.PrefetchScalarGridSpec(
        num_scalar_prefetch=1, grid=(N_DEV, n_t),
        in_specs=[hbm,
                  pl.BlockSpec((1, tk, CONV_W), lambda s, t, order_ref: (order_ref[s], t, 0)),
                  hbm, hbm],
        out_specs=(hbm, hbm, hbm),
        scratch_shapes=[pltpu.VMEM((D, CONV_W), F32), pltpu.VMEM((2, D, CONV_W), BF16),
                        pltpu.VMEM((n_t, D, tk), BF16),
                        pltpu.SemaphoreType.DMA((N_DEV - 1,)), pltpu.SemaphoreType.DMA((N_DEV - 1,)),
                        pltpu.SemaphoreType.DMA((14,)), pltpu.SemaphoreType.DMA((14,)),
                        pltpu.SemaphoreType.DMA((3,)), pltpu.SemaphoreType.DMA((n_t,))])
    return pl.pallas_call(
        body, name="dw_exchange", grid_spec=grid_spec,
        out_shape=(jax.ShapeDtypeStruct((N_DEV, D, CONV_W), BF16), jax.ShapeDtypeStruct(gw_out.shape, F32),
                   jax.ShapeDtypeStruct((N_DEV,) + small.shape, F32)),
        compiler_params=_params(("arbitrary", "arbitrary"), 52),
    )(order, ht, dproj, gw_out, small)


def _adamw_math(w, g, m, v):
    m2 = ADAM_B1 * m + (1.0 - ADAM_B1) * g
    v2 = ADAM_B2 * v + (1.0 - ADAM_B2) * (g * g)
    m_hat = m2 / (1.0 - ADAM_B1 ** ADAM_STEP)
    v_hat = v2 / (1.0 - ADAM_B2 ** ADAM_STEP)
    delta = -ADAM_LR * (m_hat / (jnp.sqrt(v_hat) + ADAM_EPS) + ADAM_WD * w)
    return delta, m2, v2


def _adamw_sharded(name, parts, w, m, v, rows):
    R, C = w.shape

    def body(p_ref, w_ref, m_ref, v_ref, g_ref, d_ref, m2_ref, v2_ref):
        g = p_ref[0].astype(F32)
        for s in range(1, N_DEV):
            g = g + p_ref[s].astype(F32)
        g_ref[...] = g
        d_ref[...], m2_ref[...], v2_ref[...] = _adamw_math(w_ref[...], g, m_ref[...], v_ref[...])

    spec = pl.BlockSpec((rows, C), lambda i: (i, 0))
    out = jax.ShapeDtypeStruct((R, C), F32)
    return pl.pallas_call(
        body, name=name, grid=(R // rows,),
        out_shape=(out,) * 4,
        in_specs=[pl.BlockSpec((N_DEV, rows, C), lambda i: (0, i, 0)), spec, spec, spec],
        out_specs=(spec,) * 4,
        compiler_params=_params(("arbitrary",), 40),
    )(parts, w, m, v)


def _adamw_small(parts, w, m, v):
    R = w.shape[0]

    def body(p_ref, w_ref, m_ref, v_ref, g_ref, d_ref, m2_ref, v2_ref):
        g = p_ref[0]
        for s in range(1, N_DEV):
            g = g + p_ref[s]
        g_ref[...] = g
        d_ref[...], m2_ref[...], v2_ref[...] = _adamw_math(w_ref[...], g, m_ref[...], v_ref[...])

    vm = pl.BlockSpec(memory_space=pltpu.VMEM)
    out = jax.ShapeDtypeStruct((R, LANES), F32)
    return pl.pallas_call(
        body, name="adamw_small", out_shape=(out,) * 4, in_specs=[vm] * 4, out_specs=(vm,) * 4,
    )(parts, w, m, v)


def _qk_gain_grads(a_parts, q_norm_w, k_norm_w, bias_g):
    def body(a_ref, qw_ref, kw_ref, bg_ref, out_ref):
        tot = jnp.zeros((8, LANES), F32)
        for g in range(len(DILATIONS)):
            for p in range(N_PAIRS):
                tot = tot + a_ref[g, p]
        col = jnp.sum(tot, axis=0, keepdims=True)
        a64 = col[:, 0:HEAD_DIM] + col[:, HEAD_DIM:LANES]
        out_ref[...] = jnp.zeros_like(out_ref)
        out_ref[0:1, 0:HEAD_DIM] = LOGIT_SCALE * kw_ref[...] * a64
        out_ref[1:2, 0:HEAD_DIM] = LOGIT_SCALE * qw_ref[...] * a64
        for h in range(N_HEADS):
            r0, c0 = 2 + h // 4, (h % 4) * N_BUCKETS
            out_ref[r0:r0 + 1, c0:c0 + N_BUCKETS] = bg_ref[h // 2, h % 2:h % 2 + 1, 0:N_BUCKETS]

    vm = pl.BlockSpec(memory_space=pltpu.VMEM)
    return pl.pallas_call(
        body, name="qk_gain_grads", out_shape=jax.ShapeDtypeStruct((8, LANES), F32),
        in_specs=[vm] * 4, out_specs=vm,
    )(a_parts, q_norm_w.reshape(1, HEAD_DIM), k_norm_w.reshape(1, HEAD_DIM), bias_g)


SMALL_ROWS = 32


def _pack_small(norm_w, conv_b, q_norm_w, k_norm_w, rel_bias, conv_w_full):
    pad64 = lambda a: jnp.pad(a, (0, LANES - HEAD_DIM)).reshape(1, LANES)
    return jnp.concatenate([
        norm_w.reshape(8, LANES), conv_b.reshape(4, LANES), pad64(q_norm_w), pad64(k_norm_w),
        rel_bias.T.reshape(2, LANES), conv_w_full.reshape(12, LANES), jnp.zeros((4, LANES), F32)], axis=0)


def _unpack_small(s):
    return (s[0:8].reshape(D_MODEL), s[8:12].reshape(CONV_W), s[12, 0:HEAD_DIM], s[13, 0:HEAD_DIM],
            s[14:16].reshape(N_HEADS, N_BUCKETS).T, s[16:28].reshape(3, CONV_W))


def kernel(x, norm_w, w_in, conv_w, conv_b, q_norm_w, k_norm_w, rel_bias, w_out, loss_target, m_norm_w, m_w_in, m_conv_w, m_conv_b, m_q_norm_w, m_k_norm_w, m_rel_bias, m_w_out, v_norm_w, v_w_in, v_conv_w, v_conv_b, v_q_norm_w, v_k_norm_w, v_rel_bias, v_w_out):
    S = x.shape[1]
    x2 = x.reshape(S, D_MODEL)
    tgt = loss_target.reshape(S, D_MODEL)
    me = 4 * lax.axis_index("x") + 2 * lax.axis_index("y") + lax.axis_index("c")
    shard_w = CONV_W // N_DEV

    cw_pad = jnp.pad(conv_w, ((0, 5), (0, LANES - shard_w)))
    wg, wo_g, cw_g = _ag_weights(w_in, w_out, cw_pad)
    wo = wo_g.reshape(D_MODEL, D_MODEL)
    conv_w_full = cw_g[:, 0:3, 0:shard_w].transpose(1, 0, 2).reshape(3, CONV_W)

    bmat_t = jnp.asarray(np.stack([_bucket_matrix(d).T for d in DILATIONS]))
    tables = _bias_tables(rel_bias, bmat_t)
    gain = jnp.tile(q_norm_w * k_norm_w * LOGIT_SCALE, N_HEADS).reshape(1, ATTN_W)

    u, gb, gc, zc, za, q_hat, k_hat, v, rq, rk = _fwd_proj(x2, norm_w, wg)
    o_parts, lse_parts = [], []
    for g, dil in enumerate(DILATIONS):
        o_g, lse_g = _attn_fwd(g, dil, q_hat[g], k_hat[g], v[g], gain, tables)
        o_parts.append(o_g)
        lse_parts.append(lse_g)
    y, o, lse = _combine_gates(o_parts, lse_parts, u, gb, gc, zc, za, conv_w_full, conv_b)
    d_out, dy, gwo_part, loss_part = _out_proj(y, x2, tgt, wo)
    loss_mine = jnp.sum(loss_part) * (0.5 / D_MODEL)

    (du, dgb, dgc, dzc, dza), conv_small, d_o, delta = _gates_bwd(dy, u, gb, gc, zc, za, o, conv_w_full, conv_b)
    dqs, dks, dvs, ds_parts, a_parts = [], [], [], [], []
    for g, dil in enumerate(DILATIONS):
        dq, dk, dv, ds_sum, a_sum = _attn_bwd(g, dil, q_hat[g], k_hat[g], v[g], d_o[g], lse[g], delta[g], rq[g], rk[g],
                                              gain, tables)
        dqs.append((dq, dil))
        dks.append((dk, dil))
        dvs.append((dv, dil))
        ds_parts.append(ds_sum)
        a_parts.append(a_sum)
    grad_x, gnw_part, h_t, dproj = _proj_bwd(
        x2, d_out, norm_w, wg, [[(du, 1)], [(dgb, 1)], [(dgc, 1)], [(dzc, 1)], dqs, dks, dvs, [(dza, 1)]])
    bias_g = _bias_grad(jnp.stack(ds_parts), bmat_t)
    qk_small = _qk_gain_grads(jnp.stack(a_parts), q_norm_w, k_norm_w, bias_g)

    small_part = jnp.concatenate([
        gnw_part.sum(axis=0).reshape(8, LANES), conv_small[3].reshape(4, LANES), qk_small[0:4],
        conv_small[0:3].reshape(12, LANES), jnp.zeros((4, LANES), F32).at[0, 0].set(loss_mine)], axis=0)
    r_in, r_out, r_small = _dw_exchange(h_t, dproj, gwo_part.reshape(N_DEV, D_MODEL // N_DEV, D_MODEL), small_part)

    g_win, d_win, m_win, v_win = _adamw_sharded("adamw_w_in", r_in, w_in, m_w_in, v_w_in, 128)
    g_wo, d_wo, m_wo, v_wo = _adamw_sharded("adamw_w_out", r_out, w_out, m_w_out, v_w_out, 32)

    def full_conv(a):
        return jnp.zeros((3, CONV_W), F32).at[:, 0:shard_w].set(a)

    packs = [_pack_small(nw_, cb_, qw_, kw_, rb_, full_conv(cw_)) for nw_, cb_, qw_, kw_, rb_, cw_ in (
        (norm_w, conv_b, q_norm_w, k_norm_w, rel_bias, conv_w),
        (m_norm_w, m_conv_b, m_q_norm_w, m_k_norm_w, m_rel_bias, m_conv_w),
        (v_norm_w, v_conv_b, v_q_norm_w, v_k_norm_w, v_rel_bias, v_conv_w))]
    r_small_mine = r_small.at[:, 16:28, :].set(
        jnp.pad(lax.dynamic_slice_in_dim(r_small[:, 16:28, :].reshape(N_DEV, 3, CONV_W), me * shard_w, shard_w, axis=2),
                ((0, 0), (0, 0), (0, CONV_W - shard_w))).reshape(N_DEV, 12, LANES))
    outs_small = _adamw_small(r_small_mine, *packs)
    g_s, d_s, m_s, v_s = [_unpack_small(t) for t in outs_small]
    loss = outs_small[0][28, 0]

    def leaves(small, big_in, big_out):
        nw_, cb_, qw_, kw_, rb_, cwf = small
        return (nw_, big_in, cwf[:, 0:shard_w], cb_, qw_, kw_, rb_, big_out)

    return (loss, grad_x.reshape(x.shape),
            *leaves(g_s, g_win, g_wo), *leaves(d_s, d_win, d_wo), *leaves(m_s, m_win, m_wo), *leaves(v_s, v_win, v_wo))
```

```python
import functools
import math

import numpy as np
import jax
import jax.numpy as jnp
from jax import lax
from jax.experimental import pallas as pl
from jax.experimental.pallas import tpu as pltpu

F32 = jnp.float32
BF16 = jnp.bfloat16

N_DEV = 8
D_MODEL = 1024
CONV_W = 512
ATTN_W = 512
N_HEADS = 8
HEAD_DIM = 64
N_PAIRS = N_HEADS // 2
LANES = 128
HALF = 64
QB = 128
KB = QB + 2 * HALF
DILATIONS = (1, 4, 16)
N_BUCKETS = 32
MAX_DISTANCE = 1024
EPS = 1e-6
NEG = -1e30
LOGIT_SCALE = HEAD_DIM ** -0.5

ADAM_LR = 0.001
ADAM_B1 = 0.9
ADAM_B2 = 0.999
ADAM_EPS = 1e-08
ADAM_WD = 0.01
ADAM_STEP = 10

MESH = pl.DeviceIdType.MESH
MIB = 1024 * 1024


def _params(semantics, vmem_mib):
    return pltpu.CompilerParams(dimension_semantics=semantics, vmem_limit_bytes=vmem_mib * MIB)


def _lane(shape):
    return lax.broadcasted_iota(jnp.int32, shape, len(shape) - 1)


def _sigmoid(z):
    return 1.0 / (1.0 + jnp.exp(-z))


def _split_dot(x, w):
    hi = x.astype(BF16)
    lo = (x - hi.astype(F32)).astype(BF16)
    return jnp.dot(hi, w, preferred_element_type=F32) + jnp.dot(lo, w, preferred_element_type=F32)


def _same_head():
    r = lax.broadcasted_iota(jnp.int32, (LANES, LANES), 0) // HEAD_DIM
    c = lax.broadcasted_iota(jnp.int32, (LANES, LANES), 1) // HEAD_DIM
    return (r == c).astype(BF16)


def _head_to_lanes(p):
    r = lax.broadcasted_iota(jnp.int32, (LANES, LANES), 0)
    c = lax.broadcasted_iota(jnp.int32, (LANES, LANES), 1) // HEAD_DIM
    return (r == 2 * p + c).astype(BF16)


def _lanes_to_head(p, scale):
    r = lax.broadcasted_iota(jnp.int32, (LANES, LANES), 0) // HEAD_DIM
    c = lax.broadcasted_iota(jnp.int32, (LANES, LANES), 1)
    return jnp.where(c == 2 * p + r, scale, 0.0).astype(BF16)


VIEW_DILATIONS = DILATIONS[1:]


def _view_shape(S, dil, width, dtype):
    return jax.ShapeDtypeStruct((S // dil, dil * width), dtype)


def _view_spec(tm, dil, width):
    return pl.BlockSpec((tm // dil, dil * width), lambda i: (i, 0))


def _tile_scratch(tm, width):
    return pltpu.VMEM((width // LANES, tm, LANES), F32)


def _scratch_value(scr_ref):
    n = scr_ref.shape[0]
    return scr_ref[0] if n == 1 else jnp.concatenate([scr_ref[c] for c in range(n)], axis=1)


def _to_view(scr_ref, out_ref, dil, dtype):
    n, tm, _ = scr_ref.shape
    for r in range(dil):
        for c in range(n):
            col = (r * n + c) * LANES
            out_ref[:, col:col + LANES] = scr_ref[c, pl.ds(r, tm // dil, stride=dil), :].astype(dtype)


def _from_view(blk_ref, scr_ref, dil):
    n, tm, _ = scr_ref.shape
    for r in range(dil):
        for c in range(n):
            col = (r * n + c) * LANES
            scr_ref[c, pl.ds(r, tm // dil, stride=dil), :] = blk_ref[:, col:col + LANES].astype(F32)


def _bucket_matrix(dilation):
    rel = np.arange(KB)[None, :] - HALF - np.arange(QB)[:, None]
    band = np.abs(rel) <= HALF
    dist = np.clip(rel, -HALF, HALF) * dilation
    half_b = N_BUCKETS // 2
    max_exact = half_b // 2
    ret = np.where(dist > 0, half_b, 0)
    n = np.abs(dist)
    nf = np.maximum(n, 1).astype(np.float32)
    large = max_exact + (np.log(nf / np.float32(max_exact)) / np.float32(math.log(MAX_DISTANCE / max_exact))
                         * np.float32(half_b - max_exact)).astype(np.int32)
    large = np.minimum(large, half_b - 1)
    bucket = ret + np.where(n < max_exact, n, large)
    return np.where(band, bucket, -1).astype(np.int32)


def _ag_weights(w_in, w_out, cw_pad):
    n_arr = 3

    def body(win_ref, wout_ref, cw_ref, gin_ref, gout_ref, gcw_ref, send_sems, recv_sems):
        x, y, c = lax.axis_index("x"), lax.axis_index("y"), lax.axis_index("c")
        me = (x, y, c)
        sibling = (x, y, 1 - c)
        chips = [(1 - x, y), (x, 1 - y), (1 - x, 1 - y)]
        arrays = (gin_ref, gout_ref, gcw_ref)

        def slot(px, py, pc):
            return 4 * px + 2 * py + pc

        gin_ref[slot(*me)] = win_ref[...].astype(BF16)
        gout_ref[slot(*me)] = wout_ref[...].astype(BF16)
        gcw_ref[slot(*me)] = cw_ref[...]

        def copy(a, k, block, to):
            ref = arrays[a].at[slot(*block)]
            return pltpu.make_async_remote_copy(
                src_ref=ref, dst_ref=ref, send_sem=send_sems.at[a * 7 + k], recv_sem=recv_sems.at[a * 7 + k],
                device_id=to, device_id_type=MESH)

        first = [copy(a, 0, me, sibling) for a in range(n_arr)]
        for j, chip in enumerate(chips):
            first += [copy(a, 1 + j, me, (*chip, c)) for a in range(n_arr)]
        for cp in first:
            cp.start()
        passed = []
        for j, chip in enumerate(chips):
            for a in range(n_arr):
                copy(a, 1 + j, (*chip, c), me).wait_recv()
            for a in range(n_arr):
                cp = copy(a, 4 + j, (*chip, c), sibling)
                cp.start()
                passed.append(cp)
        for a in range(n_arr):
            copy(a, 0, sibling, me).wait_recv()
        for j, chip in enumerate(chips):
            for a in range(n_arr):
                copy(a, 4 + j, (*chip, 1 - c), me).wait_recv()
        for cp in first + passed:
            cp.wait_send()

    vm = pl.BlockSpec(memory_space=pltpu.VMEM)
    return pl.pallas_call(
        body, name="ag_weights",
        out_shape=(jax.ShapeDtypeStruct((N_DEV,) + w_in.shape, BF16),
                   jax.ShapeDtypeStruct((N_DEV,) + w_out.shape, BF16),
                   jax.ShapeDtypeStruct((N_DEV,) + cw_pad.shape, F32)),
        in_specs=[vm, vm, vm], out_specs=(vm, vm, vm),
        scratch_shapes=[pltpu.SemaphoreType.DMA((n_arr * 7,)), pltpu.SemaphoreType.DMA((n_arr * 7,))],
        compiler_params=pltpu.CompilerParams(vmem_limit_bytes=40 * MIB),
    )(w_in, w_out, cw_pad)


def _fwd_proj(x, norm_w, wg):
    S = x.shape[0]
    tm = 512

    def body(x_ref, nw_ref, wg_ref, u_ref, gb_ref, gc_ref, zc_ref, za_ref, *rest):
        q_refs, k_refs, v_refs, rq_refs, rk_refs = (rest[3 * n:3 * n + 3] for n in range(5))
        scr, rscr = rest[15:]
        xv = x_ref[...]
        r = lax.rsqrt(jnp.mean(xv * xv, axis=-1, keepdims=True) + EPS)
        h = (xv * r * nw_ref[...]).astype(BF16)
        lane = _lane((tm, LANES))
        lo = lane < HEAD_DIM
        plain = {0: u_ref, 1: gb_ref, 2: gc_ref, 3: zc_ref, 7: za_ref}
        normed = {4: (q_refs, rq_refs), 5: (k_refs, rk_refs)}

        def emit(src, refs, dtype):
            refs[0][...] = _scratch_value(src).astype(dtype)
            for dil, ref in zip(VIEW_DILATIONS, refs[1:]):
                _to_view(src, ref, dil, dtype)

        acc_next = jnp.dot(h, wg_ref[0], preferred_element_type=F32)
        for j in range(8):
            acc = acc_next
            if j < 7:
                acc_next = jnp.dot(h, wg_ref[j + 1], preferred_element_type=F32)
            if j in plain:
                plain[j][...] = acc.astype(BF16)
                continue
            if j == 6:
                for p in range(N_PAIRS):
                    scr[p] = acc[:, p * LANES:(p + 1) * LANES]
                emit(scr, v_refs, BF16)
                continue
            out_refs, r_refs = normed[j]
            r_tile = jnp.zeros((tm, LANES), F32)
            for p in range(N_PAIRS):
                blk = acc[:, p * LANES:(p + 1) * LANES]
                sq = blk * blk
                s_lo = jnp.sum(jnp.where(lo, sq, 0.0), axis=-1, keepdims=True)
                s_hi = jnp.sum(jnp.where(lo, 0.0, sq), axis=-1, keepdims=True)
                r_lo = lax.rsqrt(s_lo * (1.0 / HEAD_DIM) + EPS)
                r_hi = lax.rsqrt(s_hi * (1.0 / HEAD_DIM) + EPS)
                scr[p] = blk * jnp.where(lo, r_lo, r_hi)
                r_tile = jnp.where(lane == 2 * p, r_lo, r_tile)
                r_tile = jnp.where(lane == 2 * p + 1, r_hi, r_tile)
            rscr[0] = r_tile
            emit(scr, out_refs, BF16)
            emit(rscr, r_refs, F32)

    piece = jax.ShapeDtypeStruct((S, CONV_W), BF16)
    pspec = pl.BlockSpec((tm, CONV_W), lambda i: (i, 0))
    wide = [_view_shape(S, d, ATTN_W, BF16) for d in DILATIONS]
    wide_specs = [_view_spec(tm, d, ATTN_W) for d in DILATIONS]
    stat = [_view_shape(S, d, LANES, F32) for d in DILATIONS]
    stat_specs = [_view_spec(tm, d, LANES) for d in DILATIONS]
    outs = pl.pallas_call(
        body, name="fwd_proj", grid=(S // tm,),
        out_shape=[piece] * 5 + wide * 3 + stat * 2,
        in_specs=[pl.BlockSpec((tm, D_MODEL), lambda i: (i, 0)),
                  pl.BlockSpec((1, D_MODEL), lambda i: (0, 0)),
                  pl.BlockSpec(wg.shape, lambda i: (0, 0, 0), pipeline_mode=pl.Buffered(1))],
        out_specs=[pspec] * 5 + wide_specs * 3 + stat_specs * 2,
        scratch_shapes=[_tile_scratch(tm, ATTN_W), _tile_scratch(tm, LANES)],
        compiler_params=_params(("arbitrary",), 48),
    )(x, norm_w.reshape(1, D_MODEL), wg)
    u, gb, gc, zc, za = outs[:5]
    q_hat, k_hat, v, rq, rk = (outs[5 + 3 * n:8 + 3 * n] for n in range(5))
    return u, gb, gc, zc, za, q_hat, k_hat, v, rq, rk


def _bias_tables(rel_bias, bmat_t):
    def body(rb_ref, b_ref, out_ref):
        h = pl.program_id(1)
        b = b_ref[0]
        t = jnp.full((KB, QB), NEG, F32)
        for bk in range(N_BUCKETS):
            t = jnp.where(b == bk, rb_ref[bk, h], t)
        out_ref[0, 0] = t

    return pl.pallas_call(
        body, name="bias_tables", grid=(len(DILATIONS), N_HEADS),
        out_shape=jax.ShapeDtypeStruct((len(DILATIONS), N_HEADS, KB, QB), F32),
        in_specs=[pl.BlockSpec(memory_space=pltpu.SMEM),
                  pl.BlockSpec((1, KB, QB), lambda g, h: (g, 0, 0))],
        out_specs=pl.BlockSpec((1, 1, KB, QB), lambda g, h: (g, h, 0, 0)),
        compiler_params=_params(("arbitrary", "arbitrary"), 16),
    )(rel_bias, bmat_t)


def _bias_grad(ds_acc, bmat_t):
    present = [sorted(set(_bucket_matrix(d).ravel().tolist()) - {-1}) for d in DILATIONS]

    def body(ds_ref, b_ref, out_ref):
        lane = _lane((1, LANES))
        row = lax.broadcasted_iota(jnp.int32, (8, LANES), 0)
        out = jnp.zeros((8, LANES), F32)
        for e in range(2):
            vec = jnp.zeros((1, LANES), F32)
            for g in range(len(DILATIONS)):
                b = b_ref[g]
                ds = ds_ref[g, 0, :, e * QB:(e + 1) * QB]
                for bk in present[g]:
                    s = jnp.sum(jnp.where(b == bk, ds, 0.0), axis=-1, keepdims=True)
                    s = jnp.sum(s, axis=0, keepdims=True)
                    vec = vec + jnp.where(lane == bk, s, 0.0)
            out = jnp.where(row == e, vec, out)
        out_ref[0] = out

    return pl.pallas_call(
        body, name="bias_grad", grid=(N_PAIRS,),
        out_shape=jax.ShapeDtypeStruct((N_PAIRS, 8, LANES), F32),
        in_specs=[pl.BlockSpec((len(DILATIONS), 1, KB, 2 * QB), lambda p: (0, p, 0, 0)),
                  pl.BlockSpec((len(DILATIONS), KB, QB), lambda p: (0, 0, 0))],
        out_specs=pl.BlockSpec((1, 8, LANES), lambda p: (p, 0, 0)),
        compiler_params=_params(("arbitrary",), 16),
    )(ds_acc, bmat_t)


def _halo_specs(width, rows, T, L, cols_of):
    per = T // rows
    last = L // rows - 1
    prev = pl.BlockSpec((rows, width), lambda r, i: (jnp.maximum(i * per - 1, 0), cols_of(r)))
    nxt = pl.BlockSpec((rows, width), lambda r, i: (jnp.minimum((i + 1) * per, last), cols_of(r)))
    return prev, nxt


def _attn_fwd(g, dil, qv, kv, vv, gain, tables):
    L = qv.shape[0]
    T = min(1024, L)
    n_sub = T // QB

    def body(q_ref, km_ref, kp_ref, kn_ref, vm_ref, vp_ref, vn_ref, gain_ref, bias_ref, o_ref, lse_ref,
             kwin, vwin, s_scr, p_scr):
        i = pl.program_id(1)
        kwin[0:HALF] = kp_ref[...]
        kwin[HALF:HALF + T] = km_ref[...]
        kwin[HALF + T:] = kn_ref[...]
        vwin[0:HALF] = vp_ref[...]
        vwin[HALF:HALF + T] = vm_ref[...]
        vwin[HALF + T:] = vn_ref[...]
        lo = _lane((QB, LANES)) < HEAD_DIM
        krow = lax.broadcasted_iota(jnp.int32, (KB, 1), 0)
        chan_lo = lax.broadcasted_iota(jnp.int32, (LANES, 1), 0) < HEAD_DIM
        hrow = lax.broadcasted_iota(jnp.int32, (LANES, QB), 0)

        def sub(j, carry):
            r0 = pl.multiple_of(j * QB, QB)
            kpos = i * T + j * QB - HALF + krow
            kvalid = (kpos >= 0) & (kpos < L)
            lse_rows = jnp.zeros((LANES, QB), F32)
            cols = [slice(p * LANES, (p + 1) * LANES) for p in range(N_PAIRS)]
            for p, cs in enumerate(cols):
                qf = q_ref[pl.ds(r0, QB), cs].astype(F32) * gain_ref[:, cs]
                q2 = jnp.concatenate([jnp.where(lo, qf, 0.0), jnp.where(lo, 0.0, qf)], axis=0).astype(BF16)
                s_scr[p] = lax.dot_general(kwin[pl.ds(r0, KB), cs], q2, (((1,), (1,)), ((), ())),
                                           preferred_element_type=F32)
            inv_l = []
            for p in range(N_PAIRS):
                bias2 = jnp.concatenate([bias_ref[0, 2 * p], bias_ref[0, 2 * p + 1]], axis=1)
                logits = jnp.where(kvalid, s_scr[p] + bias2, NEG)
                m = jnp.max(logits, axis=0, keepdims=True)
                pt = jnp.exp(logits - m)
                l = jnp.sum(pt, axis=0, keepdims=True)
                p_scr[p] = pt.astype(BF16)
                inv_l.append(1.0 / l)
                lse2 = m + jnp.log(l)
                lse_rows = jnp.where(hrow == 2 * p, lse2[:, 0:QB], lse_rows)
                lse_rows = jnp.where(hrow == 2 * p + 1, lse2[:, QB:2 * QB], lse_rows)
            for p, cs in enumerate(cols):
                s_scr[p, 0:LANES, :] = lax.dot_general(vwin[pl.ds(r0, KB), cs], p_scr[p], (((0,), (0,)), ((), ())),
                                                       preferred_element_type=F32) * inv_l[p]
            for p, cs in enumerate(cols):
                ot2 = s_scr[p, 0:LANES, :]
                ot = jnp.where(chan_lo, ot2[:, 0:QB], ot2[:, QB:2 * QB])
                o_ref[pl.ds(r0, QB), cs] = ot.T.astype(BF16)
            lse_ref[pl.ds(r0, QB), :] = lse_rows.T
            return carry

        lax.fori_loop(0, n_sub, sub, 0)

    main = pl.BlockSpec((T, ATTN_W), lambda r, i: (i, r))
    prev, nxt = _halo_specs(ATTN_W, HALF, T, L, lambda r: r)
    o_g, lse_g = pl.pallas_call(
        body, name=f"attn_fwd_d{dil}", grid=(dil, L // T),
        out_shape=(jax.ShapeDtypeStruct((L, dil * ATTN_W), BF16), jax.ShapeDtypeStruct((L, dil * LANES), F32)),
        in_specs=[main, main, prev, nxt, main, prev, nxt,
                  pl.BlockSpec((1, ATTN_W), lambda r, i: (0, 0)),
                  pl.BlockSpec((1, N_HEADS, KB, QB), lambda r, i: (g, 0, 0, 0))],
        out_specs=(main, pl.BlockSpec((T, LANES), lambda r, i: (i, r))),
        scratch_shapes=[pltpu.VMEM((T + 2 * HALF, ATTN_W), BF16), pltpu.VMEM((T + 2 * HALF, ATTN_W), BF16),
                        pltpu.VMEM((N_PAIRS, KB, 2 * QB), F32), pltpu.VMEM((N_PAIRS, KB, 2 * QB), BF16)],
        compiler_params=_params(("arbitrary", "arbitrary"), 40),
    )(qv, kv, kv, kv, vv, vv, vv, gain, tables)
    return o_g, lse_g


HALO = 16
CHUNK = 128


def _row_halo_specs(width, tm, S, col=0):
    rows = HALO
    per = tm // rows
    last = S // rows - 1
    prev = pl.BlockSpec((rows, width), lambda i: (jnp.maximum(i * per - 1, 0), col))
    nxt = pl.BlockSpec((rows, width), lambda i: (jnp.minimum((i + 1) * per, last), col))
    return prev, nxt


def _shift_rows(a, prev_row, next_row, tm):
    row = lax.broadcasted_iota(jnp.int32, a.shape, 0)
    a_m1 = jnp.where(row == 0, prev_row, pltpu.roll(a, 1, 0))
    a_p1 = jnp.where(row == tm - 1, next_row, pltpu.roll(a, tm - 1, 0))
    return a_m1, a_p1


def _combine_gates(o_parts, lse_parts, u, gb, gc, zc, za, conv_w, conv_b):
    S = u.shape[0]
    tm = 512
    n_t = S // tm

    def body(o1, o2, o3, l1, l2, l3, u_ref, up_ref, un_ref, gb_ref, gc_ref, gcp_ref, gcn_ref, zc_ref, za_ref,
             cw_ref, cb_ref, y_ref, o_ref, lse_ref, lse2_ref, lse3_ref, so2, so3, sl2, sl3, slse):
        i = pl.program_id(0)
        for blk, scr, dil in ((o2, so2, DILATIONS[1]), (o3, so3, DILATIONS[2]),
                              (l2, sl2, DILATIONS[1]), (l3, sl3, DILATIONS[2])):
            _from_view(blk, scr, dil)
        ls = [l1[...], sl2[0], sl3[0]]
        lmax = jnp.maximum(jnp.maximum(ls[0], ls[1]), ls[2])
        es = [jnp.exp(l - lmax) for l in ls]
        den = es[0] + es[1] + es[2]
        slse[0] = lmax + jnp.log(den)
        lse_ref[...] = slse[0]
        _to_view(slse, lse2_ref, DILATIONS[1], F32)
        _to_view(slse, lse3_ref, DILATIONS[2], F32)
        inv = 1.0 / den
        ws = [e * inv for e in es]
        for p in range(N_PAIRS):
            cs = slice(p * LANES, (p + 1) * LANES)
            acc = jnp.zeros((tm, LANES), F32)
            spread = _head_to_lanes(p)
            for w, o_g in zip(ws, (o1[:, cs].astype(F32), so2[p], so3[p])):
                acc = acc + _split_dot(w, spread) * o_g
            o_ref[:, cs] = acc.astype(BF16)
            za = za_ref[:, cs].astype(F32)
            y_ref[:, CONV_W + p * LANES:CONV_W + (p + 1) * LANES] = (acc * za * _sigmoid(za)).astype(BF16)
        a = gc_ref[...].astype(F32) * u_ref[...].astype(F32)
        a_prev = gcp_ref[15:16, :].astype(F32) * up_ref[15:16, :].astype(F32) * (i > 0).astype(F32)
        a_next = gcn_ref[0:1, :].astype(F32) * un_ref[0:1, :].astype(F32) * (i < n_t - 1).astype(F32)
        a_m1, a_p1 = _shift_rows(a, a_prev, a_next, tm)
        z3 = cw_ref[0:1, :] * a_m1 + cw_ref[1:2, :] * a + cw_ref[2:3, :] * a_p1
        zc = zc_ref[...].astype(F32)
        y_ref[:, 0:CONV_W] = (gb_ref[...].astype(F32) * (z3 + cb_ref[...]) * zc * _sigmoid(zc)).astype(BF16)

    pspec = pl.BlockSpec((tm, CONV_W), lambda i: (i, 0))
    hp, hn = _row_halo_specs(CONV_W, tm, S)
    wide_specs = [_view_spec(tm, d, ATTN_W) for d in DILATIONS]
    stat_specs = [_view_spec(tm, d, LANES) for d in DILATIONS]
    outs = pl.pallas_call(
        body, name="combine_gates", grid=(n_t,),
        out_shape=[jax.ShapeDtypeStruct((S, D_MODEL), BF16), jax.ShapeDtypeStruct((S, ATTN_W), BF16)]
        + [_view_shape(S, d, LANES, F32) for d in DILATIONS],
        in_specs=wide_specs + stat_specs + [pspec, hp, hn, pspec, pspec, hp, hn, pspec, pspec,
                                            pl.BlockSpec((3, CONV_W), lambda i: (0, 0)),
                                            pl.BlockSpec((1, CONV_W), lambda i: (0, 0))],
        out_specs=[pl.BlockSpec((tm, D_MODEL), lambda i: (i, 0)), pspec] + stat_specs,
        scratch_shapes=[_tile_scratch(tm, ATTN_W), _tile_scratch(tm, ATTN_W),
                        _tile_scratch(tm, LANES), _tile_scratch(tm, LANES), _tile_scratch(tm, LANES)],
        compiler_params=_params(("arbitrary",), 40),
    )(*o_parts, *lse_parts, u, u, u, gb, gc, gc, gc, zc, za, conv_w, conv_b.reshape(1, CONV_W))
    return outs[0], outs[1], outs[2:]


def _out_proj(y, x, target, wo):
    S = x.shape[0]
    tm = 512

    def body(y_ref, x_ref, t_ref, wo_ref, g_ref, dy_ref, dwo_ref, loss_ref):
        i = pl.program_id(0)

        @pl.when(i == 0)
        def _():
            dwo_ref[...] = jnp.zeros_like(dwo_ref)
            loss_ref[...] = jnp.zeros_like(loss_ref)

        yv = y_ref[...]
        wo_v = wo_ref[...]
        err = x_ref[...] + jnp.dot(yv, wo_v, preferred_element_type=F32) - t_ref[...]
        e2 = (err * err).reshape(tm // 8, 8, D_MODEL).sum(axis=0)
        part = e2[:, 0:LANES]
        for k in range(1, D_MODEL // LANES):
            part = part + e2[:, k * LANES:(k + 1) * LANES]
        loss_ref[...] += part
        gv = (err * (1.0 / D_MODEL)).astype(BF16)
        g_ref[...] = gv
        dy_ref[...] = lax.dot_general(gv, wo_v, (((1,), (1,)), ((), ())), preferred_element_type=F32).astype(BF16)
        dwo_ref[...] += lax.dot_general(yv, gv, (((0,), (0,)), ((), ())), preferred_element_type=F32)

    tile = pl.BlockSpec((tm, D_MODEL), lambda i: (i, 0))
    return pl.pallas_call(
        body, name="out_proj", grid=(S // tm,),
        out_shape=(jax.ShapeDtypeStruct((S, D_MODEL), BF16), jax.ShapeDtypeStruct((S, D_MODEL), BF16),
                   jax.ShapeDtypeStruct((D_MODEL, D_MODEL), F32), jax.ShapeDtypeStruct((8, LANES), F32)),
        in_specs=[tile, tile, tile, pl.BlockSpec((D_MODEL, D_MODEL), lambda i: (0, 0), pipeline_mode=pl.Buffered(1))],
        out_specs=(tile, tile, pl.BlockSpec((D_MODEL, D_MODEL), lambda i: (0, 0)),
                   pl.BlockSpec((8, LANES), lambda i: (0, 0))),
        compiler_params=_params(("arbitrary",), 48),
    )(y, x, target, wo)


def _gates_bwd(dy, u, gb, gc, zc, za, o, conv_w, conv_b):
    S = u.shape[0]
    tm = 512
    n_t = S // tm

    def body(dy_ref, dyp_ref, dyn_ref, u_ref, up_ref, un_ref, gb_ref, gbp_ref, gbn_ref, gc_ref, gcp_ref, gcn_ref,
             zc_ref, zcp_ref, zcn_ref, za_ref, o_ref, cw_ref, cb_ref,
             du_ref, dgb_ref, dgc_ref, dzc_ref, dza_ref, small_ref,
             do_ref, do2_ref, do3_ref, delta_ref, delta2_ref, delta3_ref, sdo, sdelta,
             u_ext, gc_ext, gb_ext, zc_ext, dy_ext):
        i = pl.program_id(0)

        @pl.when(i == 0)
        def _():
            small_ref[...] = jnp.zeros_like(small_ref)

        exts = ((u_ext, up_ref, u_ref, un_ref), (gc_ext, gcp_ref, gc_ref, gcn_ref), (gb_ext, gbp_ref, gb_ref, gbn_ref),
                (zc_ext, zcp_ref, zc_ref, zcn_ref))
        for ext, pr, mn, nx in exts:
            ext[0:HALO] = pr[...]
            ext[HALO:HALO + tm] = mn[...]
            ext[HALO + tm:] = nx[...]
        dy_ext[0:HALO] = dyp_ref[...]
        dy_ext[HALO:HALO + tm] = dy_ref[:, 0:CONV_W]
        dy_ext[HALO + tm:] = dyn_ref[...]
        zero_halo = jnp.zeros((HALO, CONV_W), BF16)

        @pl.when(i == 0)
        def _():
            u_ext[0:HALO] = zero_halo
            dy_ext[0:HALO] = zero_halo

        @pl.when(i == n_t - 1)
        def _():
            u_ext[HALO + tm:] = zero_halo
            dy_ext[HALO + tm:] = zero_halo

        wide_rows = CHUNK + 2 * HALO
        mid = slice(HALO, HALO + CHUNK)
        csum = lambda t: jnp.sum(t, axis=0, keepdims=True)

        def one_group(p, c0, sums):
            cs = slice(p * LANES, (p + 1) * LANES)
            w0, w1, w2, cb = cw_ref[0:1, cs], cw_ref[1:2, cs], cw_ref[2:3, cs], cb_ref[:, cs]
            rows = pl.ds(c0, CHUNK)
            wide = pl.ds(c0, wide_rows)
            u_e, gc_e, gb_e, zc_e, dy_e = (r[wide, cs].astype(F32) for r in (u_ext, gc_ext, gb_ext, zc_ext, dy_ext))
            a_e = gc_e * u_e
            sz_e = _sigmoid(zc_e)
            dz3_e = dy_e * zc_e * sz_e * gb_e
            a, a_m1, a_p1 = a_e[mid], pltpu.roll(a_e, 1, 0)[mid], pltpu.roll(a_e, wide_rows - 1, 0)[mid]
            dz3, dz3_m1, dz3_p1 = dz3_e[mid], pltpu.roll(dz3_e, 1, 0)[mid], pltpu.roll(dz3_e, wide_rows - 1, 0)[mid]
            uv, gcv, gbv, zcv, dyc, sz = u_e[mid], gc_e[mid], gb_e[mid], zc_e[mid], dy_e[mid], sz_e[mid]
            cvp = w0 * a_m1 + w1 * a + w2 * a_p1 + cb
            dcv = dyc * zcv * sz
            dzc_ref[rows, cs] = (dyc * gbv * cvp * (sz * (1.0 + zcv * (1.0 - sz)))).astype(BF16)
            dgb_ref[rows, cs] = (dcv * cvp).astype(BF16)
            da = w0 * dz3_p1 + w1 * dz3 + w2 * dz3_m1
            du_ref[rows, cs] = (da * gcv).astype(BF16)
            dgc_ref[rows, cs] = (da * uv).astype(BF16)
            sums = (sums[0] + csum(dz3 * a_m1), sums[1] + csum(dz3 * a), sums[2] + csum(dz3 * a_p1),
                    sums[3] + csum(dz3))

            dya = dy_ref[rows, CONV_W + p * LANES:CONV_W + (p + 1) * LANES].astype(F32)
            zav = za_ref[rows, cs].astype(F32)
            ov = o_ref[rows, cs].astype(F32)
            sa = _sigmoid(zav)
            d_o = dya * zav * sa
            sdo[p, rows, :] = d_o
            do_ref[rows, cs] = d_o.astype(BF16)
            dza_ref[rows, cs] = (dya * ov * (sa * (1.0 + zav * (1.0 - sa)))).astype(BF16)
            return sums, _split_dot(d_o * ov, _lanes_to_head(p, 1.0))

        def chunk(ci, carry):
            c0 = pl.multiple_of(ci * CHUNK, CHUNK)
            out, delta = [], jnp.zeros((CHUNK, LANES), F32)
            for p in range(N_PAIRS):
                sums, part = one_group(p, c0, carry[p])
                out.append(sums)
                delta = delta + part
            sdelta[0, pl.ds(c0, CHUNK), :] = delta
            return tuple(out)

        z = jnp.zeros((1, LANES), F32)
        all_sums = lax.fori_loop(0, tm // CHUNK, chunk, ((z, z, z, z),) * N_PAIRS)
        for p in range(N_PAIRS):
            for k in range(4):
                small_ref[k:k + 1, p * LANES:(p + 1) * LANES] += all_sums[p][k]
        delta_ref[...] = sdelta[0]
        for dil, wide_ref, stat_ref in zip(VIEW_DILATIONS, (do2_ref, do3_ref), (delta2_ref, delta3_ref)):
            _to_view(sdo, wide_ref, dil, BF16)
            _to_view(sdelta, stat_ref, dil, F32)

    pspec = pl.BlockSpec((tm, CONV_W), lambda i: (i, 0))
    hp, hn = _row_halo_specs(CONV_W, tm, S)
    piece = jax.ShapeDtypeStruct((S, CONV_W), BF16)
    outs = pl.pallas_call(
        body, name="gates_bwd", grid=(n_t,),
        out_shape=[piece] * 5 + [jax.ShapeDtypeStruct((8, CONV_W), F32)]
        + [_view_shape(S, d, ATTN_W, BF16) for d in DILATIONS] + [_view_shape(S, d, LANES, F32) for d in DILATIONS],
        in_specs=[pl.BlockSpec((tm, D_MODEL), lambda i: (i, 0)), hp, hn,
                  pspec, hp, hn, pspec, hp, hn, pspec, hp, hn, pspec, hp, hn, pspec, pspec,
                  pl.BlockSpec((3, CONV_W), lambda i: (0, 0)), pl.BlockSpec((1, CONV_W), lambda i: (0, 0))],
        out_specs=[pspec] * 5 + [pl.BlockSpec((8, CONV_W), lambda i: (0, 0))]
        + [_view_spec(tm, d, ATTN_W) for d in DILATIONS] + [_view_spec(tm, d, LANES) for d in DILATIONS],
        scratch_shapes=[_tile_scratch(tm, ATTN_W), _tile_scratch(tm, LANES)]
        + [pltpu.VMEM((tm + 2 * HALO, CONV_W), BF16)] * 5,
        compiler_params=_params(("arbitrary",), 40),
    )(dy, dy, dy, u, u, u, gb, gb, gb, gc, gc, gc, zc, zc, zc, za, o, conv_w, conv_b.reshape(1, CONV_W))
    return outs[:5], outs[5], outs[6:9], outs[9:12]


def _attn_bwd(g, dil, qv, kv, vv, dov, lv, dv_, rqv, rkv, gain, tables):
    L = qv.shape[0]
    T = min(1024, L)
    n_sub = T // QB + 1
    QW = T + 2 * HALF
    KW = T + 2 * QB

    def body(qm_ref, qp_ref, qn_ref, dom_ref, dop_ref, don_ref, km_ref, kp_ref, kn_ref, vm_ref, vp_ref, vn_ref,
             lm_ref, lp_ref, ln_ref, dm_ref, dp_ref, dn_ref, rqm_ref, rqp_ref, rqn_ref, rk_ref,
             gain_ref, bias_ref,
             dq_ref, dk_ref, dv_ref, ds_ref, a_ref,
             qwin, dowin, kwin, vwin, lwin, dwin, rqwin, dqwin, dkacc, dvacc,
             s_scr, d_scr, p_scr, ds_scr, q2_scr, do2_scr, dq_scr):
        r = pl.program_id(0)
        i = pl.program_id(1)

        @pl.when((r == 0) & (i == 0))
        def _():
            ds_ref[...] = jnp.zeros_like(ds_ref)
            a_ref[...] = jnp.zeros_like(a_ref)

        for win, (pr, mn, nx), h_rows in ((qwin, (qp_ref, qm_ref, qn_ref), HALF),
                                          (dowin, (dop_ref, dom_ref, don_ref), HALF),
                                          (lwin, (lp_ref, lm_ref, ln_ref), HALF),
                                          (dwin, (dp_ref, dm_ref, dn_ref), HALF),
                                          (rqwin, (rqp_ref, rqm_ref, rqn_ref), HALF),
                                          (kwin, (kp_ref, km_ref, kn_ref), QB),
                                          (vwin, (vp_ref, vm_ref, vn_ref), QB)):
            win[0:h_rows] = pr[...]
            win[h_rows:h_rows + T] = mn[...]
            win[h_rows + T:] = nx[...]
        dkacc[...] = jnp.zeros_like(dkacc)
        dvacc[...] = jnp.zeros_like(dvacc)

        lo = _lane((QB, LANES)) < HEAD_DIM
        same_head = _same_head()
        krow = lax.broadcasted_iota(jnp.int32, (KB, 1), 0)
        qcol = _lane((1, 2 * QB)) % QB
        row = lax.broadcasted_iota(jnp.int32, (QB, 1), 0)
        chan_lo = lax.broadcasted_iota(jnp.int32, (LANES, 1), 0) < HEAD_DIM

        cols = [slice(p * LANES, (p + 1) * LANES) for p in range(N_PAIRS)]
        nt = (((1,), (1,)), ((), ()))

        def scores(j):
            r0 = pl.multiple_of(j * QB, QB)
            for k, cs in enumerate(cols):
                qf = qwin[pl.ds(r0, QB), cs].astype(F32) * gain_ref[:, cs]
                q2_scr[k] = jnp.concatenate([jnp.where(lo, qf, 0.0), jnp.where(lo, 0.0, qf)], axis=0).astype(BF16)
                dov = dowin[pl.ds(r0, QB), cs]
                zero = jnp.zeros_like(dov)
                do2_scr[k] = jnp.concatenate([jnp.where(lo, dov, zero), jnp.where(lo, zero, dov)], axis=0)
                s_scr[k] = lax.dot_general(kwin[pl.ds(r0, KB), cs], q2_scr[k], nt, preferred_element_type=F32)
                d_scr[k] = lax.dot_general(vwin[pl.ds(r0, KB), cs], do2_scr[k], nt, preferred_element_type=F32)

        def sub(j):
            r0 = pl.multiple_of(j * QB, QB)
            kpos = i * T - QB + j * QB + krow
            qpos = i * T - HALF + j * QB + qcol
            valid = (kpos >= 0) & (kpos < L) & (qpos >= 0) & (qpos < L)
            owned_t = ((qpos >= i * T) & (qpos < i * T + T)).astype(F32)
            qpos_c = i * T - HALF + j * QB + row
            owned = ((qpos_c >= i * T) & (qpos_c < i * T + T)).astype(F32)
            lse_rows = lwin[pl.ds(r0, QB), :].T
            delta_rows = dwin[pl.ds(r0, QB), :].T
            rq_t = rqwin[pl.ds(r0, QB), :]
            for p in range(N_PAIRS):
                k = p
                bias2 = jnp.concatenate([bias_ref[0, 2 * p], bias_ref[0, 2 * p + 1]], axis=1)
                lse2 = jnp.concatenate([lse_rows[2 * p:2 * p + 1, :], lse_rows[2 * p + 1:2 * p + 2, :]], axis=1)
                delta2 = jnp.concatenate([delta_rows[2 * p:2 * p + 1, :], delta_rows[2 * p + 1:2 * p + 2, :]], axis=1)
                pt = jnp.where(valid, jnp.exp(s_scr[k] + bias2 - lse2), 0.0)
                dst = pt * (d_scr[k] - delta2)
                ds_ref[p] += dst * owned_t
                p_scr[p] = pt.astype(BF16)
                ds_scr[p] = dst.astype(BF16)
            for p, cs in enumerate(cols):
                k = p
                dvacc[pl.ds(r0, KB), cs] += jnp.dot(p_scr[p], do2_scr[k], preferred_element_type=F32)
                dkacc[pl.ds(r0, KB), cs] += jnp.dot(ds_scr[p], q2_scr[k], preferred_element_type=F32)
                dq_scr[p] = lax.dot_general(kwin[pl.ds(r0, KB), cs], ds_scr[p], (((0,), (0,)), ((), ())),
                                            preferred_element_type=F32)
            for p, cs in enumerate(cols):
                g2 = gain_ref[:, cs]
                q_f = qwin[pl.ds(r0, QB), cs].astype(F32)
                dqt2 = dq_scr[p]
                dqm2 = jnp.where(chan_lo, dqt2[:, 0:QB], dqt2[:, QB:2 * QB]).T
                a_ref[p] += (dqm2 * q_f * owned).reshape(QB // 8, 8, LANES).sum(axis=0)
                dqh = dqm2 * g2
                mean_c = _split_dot(dqh * q_f, same_head) * (1.0 / HEAD_DIM)
                rq2 = _split_dot(rq_t, _head_to_lanes(p))
                dqwin[pl.ds(r0, QB), cs] = (rq2 * (dqh - q_f * mean_c)).astype(BF16)

        def one_sub(j, carry):
            scores(j)
            sub(j)
            return carry

        lax.fori_loop(0, n_sub, one_sub, 0)
        dq_ref[...] = dqwin[HALF:HALF + T]

        def finish(j, carry):
            r0 = pl.multiple_of(j * QB, QB)
            rk_t = rk_ref[pl.ds(r0, QB), :]
            for p in range(N_PAIRS):
                cs = slice(p * LANES, (p + 1) * LANES)
                dkh = dkacc[pl.ds(QB + r0, QB), cs]
                k_f = km_ref[pl.ds(r0, QB), cs].astype(F32)
                mean_c = _split_dot(dkh * k_f, same_head) * (1.0 / HEAD_DIM)
                rk2 = _split_dot(rk_t, _head_to_lanes(p))
                dk_ref[pl.ds(r0, QB), cs] = (rk2 * (dkh - k_f * mean_c)).astype(BF16)
                dv_ref[pl.ds(r0, QB), cs] = dvacc[pl.ds(QB + r0, QB), cs].astype(BF16)
            return carry

        lax.fori_loop(0, T // QB, finish, 0)

    main = pl.BlockSpec((T, ATTN_W), lambda r, i: (i, r))
    smain = pl.BlockSpec((T, LANES), lambda r, i: (i, r))
    q_prev, q_next = _halo_specs(ATTN_W, HALF, T, L, lambda r: r)
    k_prev, k_next = _halo_specs(ATTN_W, QB, T, L, lambda r: r)
    s_prev, s_next = _halo_specs(LANES, HALF, T, L, lambda r: r)
    piece = jax.ShapeDtypeStruct((L, dil * ATTN_W), BF16)
    return pl.pallas_call(
        body, name=f"attn_bwd_d{dil}", grid=(dil, L // T),
        out_shape=(piece, piece, piece, jax.ShapeDtypeStruct((N_PAIRS, KB, 2 * QB), F32),
                   jax.ShapeDtypeStruct((N_PAIRS, 8, LANES), F32)),
        in_specs=[main, q_prev, q_next, main, q_prev, q_next, main, k_prev, k_next, main, k_prev, k_next,
                  smain, s_prev, s_next, smain, s_prev, s_next, smain, s_prev, s_next, smain,
                  pl.BlockSpec((1, ATTN_W), lambda r, i: (0, 0)),
                  pl.BlockSpec((1, N_HEADS, KB, QB), lambda r, i: (g, 0, 0, 0))],
        out_specs=(main, main, main, pl.BlockSpec((N_PAIRS, KB, 2 * QB), lambda r, i: (0, 0, 0)),
                   pl.BlockSpec((N_PAIRS, 8, LANES), lambda r, i: (0, 0, 0))),
        scratch_shapes=[pltpu.VMEM((QW, ATTN_W), BF16), pltpu.VMEM((QW, ATTN_W), BF16),
                        pltpu.VMEM((KW, ATTN_W), BF16), pltpu.VMEM((KW, ATTN_W), BF16),
                        pltpu.VMEM((QW, LANES), F32), pltpu.VMEM((QW, LANES), F32), pltpu.VMEM((QW, LANES), F32),
                        pltpu.VMEM((QW, ATTN_W), BF16),
                        pltpu.VMEM((KW, ATTN_W), F32), pltpu.VMEM((KW, ATTN_W), F32),
                        pltpu.VMEM((N_PAIRS, KB, 2 * QB), F32), pltpu.VMEM((N_PAIRS, KB, 2 * QB), F32),
                        pltpu.VMEM((N_PAIRS, KB, 2 * QB), BF16), pltpu.VMEM((N_PAIRS, KB, 2 * QB), BF16),
                        pltpu.VMEM((N_PAIRS, 2 * QB, LANES), BF16), pltpu.VMEM((N_PAIRS, 2 * QB, LANES), BF16),
                        pltpu.VMEM((N_PAIRS, LANES, 2 * QB), F32)],
        compiler_params=_params(("arbitrary", "arbitrary"), 56),
    )(qv, qv, qv, dov, dov, dov, kv, kv, kv, vv, vv, vv, lv, lv, lv, dv_, dv_, dv_, rqv, rqv, rqv, rkv, gain, tables)


def _proj_bwd(x, d_out, norm_w, wg, pieces):
    S = x.shape[0]
    tm = 512
    flat = [share for p in pieces for share in p]
    n_t = S // tm

    def body(*refs):
        x_ref, g_ref, nw_ref, wg_ref = refs[:4]
        piece_refs = refs[4:4 + len(flat)]
        gx_ref, dnw_ref, ht_ref, dproj_ref, sview = refs[4 + len(flat):]
        i = pl.program_id(0)

        @pl.when(i == 0)
        def _():
            dnw_ref[...] = jnp.zeros_like(dnw_ref)

        xv = x_ref[...]
        r = lax.rsqrt(jnp.mean(xv * xv, axis=-1, keepdims=True) + EPS)
        xh = xv * r
        nw = nw_ref[...]
        ht_ref[...] = (xh * nw).T.astype(BF16)
        dh = jnp.zeros((tm, D_MODEL), F32)
        k = 0
        for j in range(8):
            shares = pieces[j]
            if len(shares) == 1:
                dp = piece_refs[k][...]
            else:
                dp = jnp.zeros((tm, CONV_W), F32)
                for t, (_, dil) in enumerate(shares):
                    if dil == 1:
                        dp = dp + piece_refs[k + t][...].astype(F32)
                    else:
                        _from_view(piece_refs[k + t], sview, dil)
                        dp = dp + _scratch_value(sview)
                dp = dp.astype(BF16)
            k += len(shares)
            dproj_ref[j] = dp
            dh = dh + lax.dot_general(dp, wg_ref[j], (((1,), (1,)), ((), ())), preferred_element_type=F32)
        dnw_ref[...] += (dh * xh).reshape(tm // 8, 8, D_MODEL).sum(axis=0)
        dxh = dh * nw
        mean_c = jnp.mean(dxh * xh, axis=-1, keepdims=True)
        gx_ref[...] = g_ref[...].astype(F32) + r * (dxh - xh * mean_c)

    tile = pl.BlockSpec((tm, D_MODEL), lambda i: (i, 0))
    return pl.pallas_call(
        body, name="proj_bwd", grid=(n_t,),
        out_shape=(jax.ShapeDtypeStruct((S, D_MODEL), F32), jax.ShapeDtypeStruct((8, D_MODEL), F32),
                   jax.ShapeDtypeStruct((D_MODEL, S), BF16), jax.ShapeDtypeStruct((8, S, CONV_W), BF16)),
        in_specs=[tile, tile, pl.BlockSpec((1, D_MODEL), lambda i: (0, 0)),
                  pl.BlockSpec(wg.shape, lambda i: (0, 0, 0), pipeline_mode=pl.Buffered(1))]
        + [_view_spec(tm, dil, CONV_W) for _, dil in flat],
        out_specs=(tile, pl.BlockSpec((8, D_MODEL), lambda i: (0, 0)),
                   pl.BlockSpec((D_MODEL, tm), lambda i: (0, i)), pl.BlockSpec((8, tm, CONV_W), lambda i: (0, i, 0))),
        scratch_shapes=[_tile_scratch(tm, CONV_W)],
        compiler_params=_params(("arbitrary",), 56),
    )(x, d_out, norm_w.reshape(1, D_MODEL), wg, *[a for a, _ in flat])


def _dw_exchange(ht, dproj, gw_out, small):
    D, S = ht.shape
    tk = 2048
    n_t = S // tk
    me_outer = 4 * lax.axis_index("x") + 2 * lax.axis_index("y") + lax.axis_index("c")
    order = ((me_outer + 1 + jnp.arange(N_DEV, dtype=jnp.int32)) % N_DEV).astype(jnp.int32)

    def body(order_ref, ht_hbm, dp_ref, gout_ref, sm_ref, rin_ref, rout_ref, rsm_ref,
             acc, sbuf, ht_vmem, in_send, in_recv, side_send, side_recv, local_sems, ht_sems):
        s = pl.program_id(0)
        t = pl.program_id(1)
        x, y, c = lax.axis_index("x"), lax.axis_index("y"), lax.axis_index("c")
        me = 4 * x + 2 * y + c

        def ht_copy(j):
            return pltpu.make_async_copy(ht_hbm.at[:, j * tk:(j + 1) * tk], ht_vmem.at[j], ht_sems.at[j])

        def in_copy(k):
            to = (me + 1 + k) % N_DEV
            return pltpu.make_async_remote_copy(
                src_ref=sbuf.at[k % 2], dst_ref=rin_ref.at[me], send_sem=in_send.at[k], recv_sem=in_recv.at[k],
                device_id=(to // 4, (to // 2) % 2, to % 2), device_id_type=MESH)

        def in_landing(k):
            frm = (me + 2 * N_DEV - 1 - k) % N_DEV
            return pltpu.make_async_remote_copy(
                src_ref=sbuf.at[0], dst_ref=rin_ref.at[frm], send_sem=in_send.at[k], recv_sem=in_recv.at[k],
                device_id=(x, y, c), device_id_type=MESH)

        def side_copies():
            local = [pltpu.make_async_copy(gout_ref.at[me], rout_ref.at[me], local_sems.at[0]),
                     pltpu.make_async_copy(sm_ref, rsm_ref.at[me], local_sems.at[1])]
            remote = []
            for k in range(1, N_DEV):
                px = 1 - x if k & 4 else x
                py = 1 - y if k & 2 else y
                pc = 1 - c if k & 1 else c
                peer = 4 * px + 2 * py + pc
                for a, (src, dst) in enumerate(((gout_ref.at[peer], rout_ref.at[me]), (sm_ref, rsm_ref.at[me]))):
                    remote.append(pltpu.make_async_remote_copy(
                        src_ref=src, dst_ref=dst, send_sem=side_send.at[a * 7 + k - 1],
                        recv_sem=side_recv.at[a * 7 + k - 1], device_id=(px, py, pc), device_id_type=MESH))
            return local, remote

        @pl.when((s == 0) & (t == 0))
        def _():
            for j in range(n_t):
                ht_copy(j).start()
            local, remote = side_copies()
            for cp in local + remote:
                cp.start()

        for j in range(n_t):
            @pl.when((s == 0) & (t == j))
            def _(j=j):
                ht_copy(j).wait()

        @pl.when(t == 0)
        def _():
            acc[...] = jnp.zeros_like(acc)

        acc[...] += jnp.dot(ht_vmem[t], dp_ref[0], preferred_element_type=F32)

        for k in range(N_DEV):
            @pl.when((s == k) & (t == n_t - 1))
            def _(k=k):
                if k >= 2:
                    in_copy(k - 2).wait_send()
                sbuf[k % 2] = acc[...].astype(BF16)
                if k < N_DEV - 1:
                    in_copy(k).start()
                else:
                    own = pltpu.make_async_copy(sbuf.at[k % 2], rin_ref.at[me], local_sems.at[2])
                    own.start()
                    in_copy(k - 1).wait_send()
                    for j in range(N_DEV - 1):
                        in_landing(j).wait_recv()
                    local, remote = side_copies()
                    for cp in remote:
                        cp.wait_recv()
                    for cp in remote:
                        cp.wait_send()
                    for cp in local:
                        cp.wait()
                    own.wait()

    hbm = pl.BlockSpec(memory_space=pl.ANY)
    grid_spec = pltpu.PrefetchScalarGridSpec(
        num_scalar_prefetch=1, grid=(N_DEV, n_t),
        in_specs=[hbm,
                  pl.BlockSpec((1, tk, CONV_W), lambda s, t, order_ref: (order_ref[s], t, 0)),
                  hbm, hbm],
        out_specs=(hbm, hbm, hbm),
        scratch_shapes=[pltpu.VMEM((D, CONV_W), F32), pltpu.VMEM((2, D, CONV_W), BF16),
                        pltpu.VMEM((n_t, D, tk), BF16),
                        pltpu.SemaphoreType.DMA((N_DEV - 1,)), pltpu.SemaphoreType.DMA((N_DEV - 1,)),
                        pltpu.SemaphoreType.DMA((14,)), pltpu.SemaphoreType.DMA((14,)),
                        pltpu.SemaphoreType.DMA((3,)), pltpu.SemaphoreType.DMA((n_t,))])
    return pl.pallas_call(
        body, name="dw_exchange", grid_spec=grid_spec,
        out_shape=(jax.ShapeDtypeStruct((N_DEV, D, CONV_W), BF16), jax.ShapeDtypeStruct(gw_out.shape, F32),
                   jax.ShapeDtypeStruct((N_DEV,) + small.shape, F32)),
        compiler_params=_params(("arbitrary", "arbitrary"), 52),
    )(order, ht, dproj, gw_out, small)


def _adamw_math(w, g, m, v):
    m2 = ADAM_B1 * m + (1.0 - ADAM_B1) * g
    v2 = ADAM_B2 * v + (1.0 - ADAM_B2) * (g * g)
    m_hat = m2 / (1.0 - ADAM_B1 ** ADAM_STEP)
    v_hat = v2 / (1.0 - ADAM_B2 ** ADAM_STEP)
    delta = -ADAM_LR * (m_hat / (jnp.sqrt(v_hat) + ADAM_EPS) + ADAM_WD * w)
    return delta, m2, v2


def _adamw_sharded(name, parts, w, m, v, rows):
    R, C = w.shape

    def body(p_ref, w_ref, m_ref, v_ref, g_ref, d_ref, m2_ref, v2_ref):
        g = p_ref[0].astype(F32)
        for s in range(1, N_DEV):
            g = g + p_ref[s].astype(F32)
        g_ref[...] = g
        d_ref[...], m2_ref[...], v2_ref[...] = _adamw_math(w_ref[...], g, m_ref[...], v_ref[...])

    spec = pl.BlockSpec((rows, C), lambda i: (i, 0))
    out = jax.ShapeDtypeStruct((R, C), F32)
    return pl.pallas_call(
        body, name=name, grid=(R // rows,),
        out_shape=(out,) * 4,
        in_specs=[pl.BlockSpec((N_DEV, rows, C), lambda i: (0, i, 0)), spec, spec, spec],
        out_specs=(spec,) * 4,
        compiler_params=_params(("arbitrary",), 40),
    )(parts, w, m, v)


def _adamw_small(parts, w, m, v):
    R = w.shape[0]

    def body(p_ref, w_ref, m_ref, v_ref, g_ref, d_ref, m2_ref, v2_ref):
        g = p_ref[0]
        for s in range(1, N_DEV):
            g = g + p_ref[s]
        g_ref[...] = g
        d_ref[...], m2_ref[...], v2_ref[...] = _adamw_math(w_ref[...], g, m_ref[...], v_ref[...])

    vm = pl.BlockSpec(memory_space=pltpu.VMEM)
    out = jax.ShapeDtypeStruct((R, LANES), F32)
    return pl.pallas_call(
        body, name="adamw_small", out_shape=(out,) * 4, in_specs=[vm] * 4, out_specs=(vm,) * 4,
    )(parts, w, m, v)


def _qk_gain_grads(a_parts, q_norm_w, k_norm_w, bias_g):
    def body(a_ref, qw_ref, kw_ref, bg_ref, out_ref):
        tot = jnp.zeros((8, LANES), F32)
        for g in range(len(DILATIONS)):
            for p in range(N_PAIRS):
                tot = tot + a_ref[g, p]
        col = jnp.sum(tot, axis=0, keepdims=True)
        a64 = col[:, 0:HEAD_DIM] + col[:, HEAD_DIM:LANES]
        out_ref[...] = jnp.zeros_like(out_ref)
        out_ref[0:1, 0:HEAD_DIM] = LOGIT_SCALE * kw_ref[...] * a64
        out_ref[1:2, 0:HEAD_DIM] = LOGIT_SCALE * qw_ref[...] * a64
        for h in range(N_HEADS):
            r0, c0 = 2 + h // 4, (h % 4) * N_BUCKETS
            out_ref[r0:r0 + 1, c0:c0 + N_BUCKETS] = bg_ref[h // 2, h % 2:h % 2 + 1, 0:N_BUCKETS]

    vm = pl.BlockSpec(memory_space=pltpu.VMEM)
    return pl.pallas_call(
        body, name="qk_gain_grads", out_shape=jax.ShapeDtypeStruct((8, LANES), F32),
        in_specs=[vm] * 4, out_specs=vm,
    )(a_parts, q_norm_w.reshape(1, HEAD_DIM), k_norm_w.reshape(1, HEAD_DIM), bias_g)


SMALL_ROWS = 32


def _pack_small(norm_w, conv_b, q_norm_w, k_norm_w, rel_bias, conv_w_full):
    pad64 = lambda a: jnp.pad(a, (0, LANES - HEAD_DIM)).reshape(1, LANES)
    return jnp.concatenate([
        norm_w.reshape(8, LANES), conv_b.reshape(4, LANES), pad64(q_norm_w), pad64(k_norm_w),
        rel_bias.T.reshape(2, LANES), conv_w_full.reshape(12, LANES), jnp.zeros((4, LANES), F32)], axis=0)


def _unpack_small(s):
    return (s[0:8].reshape(D_MODEL), s[8:12].reshape(CONV_W), s[12, 0:HEAD_DIM], s[13, 0:HEAD_DIM],
            s[14:16].reshape(N_HEADS, N_BUCKETS).T, s[16:28].reshape(3, CONV_W))


def kernel(x, norm_w, w_in, conv_w, conv_b, q_norm_w, k_norm_w, rel_bias, w_out, loss_target, m_norm_w, m_w_in, m_conv_w, m_conv_b, m_q_norm_w, m_k_norm_w, m_rel_bias, m_w_out, v_norm_w, v_w_in, v_conv_w, v_conv_b, v_q_norm_w, v_k_norm_w, v_rel_bias, v_w_out):
    S = x.shape[1]
    x2 = x.reshape(S, D_MODEL)
    tgt = loss_target.reshape(S, D_MODEL)
    me = 4 * lax.axis_index("x") + 2 * lax.axis_index("y") + lax.axis_index("c")
    shard_w = CONV_W // N_DEV

    cw_pad = jnp.pad(conv_w, ((0, 5), (0, LANES - shard_w)))
    wg, wo_g, cw_g = _ag_weights(w_in, w_out, cw_pad)
    wo = wo_g.reshape(D_MODEL, D_MODEL)
    conv_w_full = cw_g[:, 0:3, 0:shard_w].transpose(1, 0, 2).reshape(3, CONV_W)

    bmat_t = jnp.asarray(np.stack([_bucket_matrix(d).T for d in DILATIONS]))
    tables = _bias_tables(rel_bias, bmat_t)
    gain = jnp.tile(q_norm_w * k_norm_w * LOGIT_SCALE, N_HEADS).reshape(1, ATTN_W)

    u, gb, gc, zc, za, q_hat, k_hat, v, rq, rk = _fwd_proj(x2, norm_w, wg)
    o_parts, lse_parts = [], []
    for g, dil in enumerate(DILATIONS):
        o_g, lse_g = _attn_fwd(g, dil, q_hat[g], k_hat[g], v[g], gain, tables)
        o_parts.append(o_g)
        lse_parts.append(lse_g)
    y, o, lse = _combine_gates(o_parts, lse_parts, u, gb, gc, zc, za, conv_w_full, conv_b)
    d_out, dy, gwo_part, loss_part = _out_proj(y, x2, tgt, wo)
    loss_mine = jnp.sum(loss_part) * (0.5 / D_MODEL)

    (du, dgb, dgc, dzc, dza), conv_small, d_o, delta = _gates_bwd(dy, u, gb, gc, zc, za, o, conv_w_full, conv_b)
    dqs, dks, dvs, ds_parts, a_parts = [], [], [], [], []
    for g, dil in enumerate(DILATIONS):
        dq, dk, dv, ds_sum, a_sum = _attn_bwd(g, dil, q_hat[g], k_hat[g], v[g], d_o[g], lse[g], delta[g], rq[g], rk[g],
                                              gain, tables)
        dqs.append((dq, dil))
        dks.append((dk, dil))
        dvs.append((dv, dil))
        ds_parts.append(ds_sum)
        a_parts.append(a_sum)
    grad_x, gnw_part, h_t, dproj = _proj_bwd(
        x2, d_out, norm_w, wg, [[(du, 1)], [(dgb, 1)], [(dgc, 1)], [(dzc, 1)], dqs, dks, dvs, [(dza, 1)]])
    bias_g = _bias_grad(jnp.stack(ds_parts), bmat_t)
    qk_small = _qk_gain_grads(jnp.stack(a_parts), q_norm_w, k_norm_w, bias_g)

    small_part = jnp.concatenate([
        gnw_part.sum(axis=0).reshape(8, LANES), conv_small[3].reshape(4, LANES), qk_small[0:4],
        conv_small[0:3].reshape(12, LANES), jnp.zeros((4, LANES), F32).at[0, 0].set(loss_mine)], axis=0)
    r_in, r_out, r_small = _dw_exchange(h_t, dproj, gwo_part.reshape(N_DEV, D_MODEL // N_DEV, D_MODEL), small_part)

    g_win, d_win, m_win, v_win = _adamw_sharded("adamw_w_in", r_in, w_in, m_w_in, v_w_in, 128)
    g_wo, d_wo, m_wo, v_wo = _adamw_sharded("adamw_w_out", r_out, w_out, m_w_out, v_w_out, 32)

    def full_conv(a):
        return jnp.zeros((3, CONV_W), F32).at[:, 0:shard_w].set(a)

    packs = [_pack_small(nw_, cb_, qw_, kw_, rb_, full_conv(cw_)) for nw_, cb_, qw_, kw_, rb_, cw_ in (
        (norm_w, conv_b, q_norm_w, k_norm_w, rel_bias, conv_w),
        (m_norm_w, m_conv_b, m_q_norm_w, m_k_norm_w, m_rel_bias, m_conv_w),
        (v_norm_w, v_conv_b, v_q_norm_w, v_k_norm_w, v_rel_bias, v_conv_w))]
    r_small_mine = r_small.at[:, 16:28, :].set(
        jnp.pad(lax.dynamic_slice_in_dim(r_small[:, 16:28, :].reshape(N_DEV, 3, CONV_W), me * shard_w, shard_w, axis=2),
                ((0, 0), (0, 0), (0, CONV_W - shard_w))).reshape(N_DEV, 12, LANES))
    outs_small = _adamw_small(r_small_mine, *packs)
    g_s, d_s, m_s, v_s = [_unpack_small(t) for t in outs_small]
    loss = outs_small[0][28, 0]

    def leaves(small, big_in, big_out):
        nw_, cb_, qw_, kw_, rb_, cwf = small
        return (nw_, big_in, cwf[:, 0:shard_w], cb_, qw_, kw_, rb_, big_out)

    return (loss, grad_x.reshape(x.shape),
            *leaves(g_s, g_win, g_wo), *leaves(d_s, d_win, d_wo), *leaves(m_s, m_win, m_wo), *leaves(v_s, v_win, v_wo))
```

```python
import functools
import math

import numpy as np
import jax
import jax.numpy as jnp
from jax import lax
from jax.experimental import pallas as pl
from jax.experimental.pallas import tpu as pltpu

F32 = jnp.float32
BF16 = jnp.bfloat16

N_DEV = 8
D_MODEL = 1024
CONV_W = 512
ATTN_W = 512
N_HEADS = 8
HEAD_DIM = 64
N_PAIRS = N_HEADS // 2
LANES = 128
HALF = 64
QB = 128
KB = QB + 2 * HALF
DILATIONS = (1, 4, 16)
N_BUCKETS = 32
MAX_DISTANCE = 1024
EPS = 1e-6
NEG = -1e30
LOGIT_SCALE = HEAD_DIM ** -0.5

ADAM_LR = 0.001
ADAM_B1 = 0.9
ADAM_B2 = 0.999
ADAM_EPS = 1e-08
ADAM_WD = 0.01
ADAM_STEP = 10

MESH = pl.DeviceIdType.MESH
MIB = 1024 * 1024


def _params(semantics, vmem_mib):
    return pltpu.CompilerParams(dimension_semantics=semantics, vmem_limit_bytes=vmem_mib * MIB)


def _lane(shape):
    return lax.broadcasted_iota(jnp.int32, shape, len(shape) - 1)


def _sigmoid(z):
    return 1.0 / (1.0 + jnp.exp(-z))


def _split_dot(x, w):
    hi = x.astype(BF16)
    lo = (x - hi.astype(F32)).astype(BF16)
    return jnp.dot(hi, w, preferred_element_type=F32) + jnp.dot(lo, w, preferred_element_type=F32)


def _same_head():
    r = lax.broadcasted_iota(jnp.int32, (LANES, LANES), 0) // HEAD_DIM
    c = lax.broadcasted_iota(jnp.int32, (LANES, LANES), 1) // HEAD_DIM
    return (r == c).astype(BF16)


def _head_to_lanes(p):
    r = lax.broadcasted_iota(jnp.int32, (LANES, LANES), 0)
    c = lax.broadcasted_iota(jnp.int32, (LANES, LANES), 1) // HEAD_DIM
    return (r == 2 * p + c).astype(BF16)


def _lanes_to_head(p, scale):
    r = lax.broadcasted_iota(jnp.int32, (LANES, LANES), 0) // HEAD_DIM
    c = lax.broadcasted_iota(jnp.int32, (LANES, LANES), 1)
    return jnp.where(c == 2 * p + r, scale, 0.0).astype(BF16)


VIEW_DILATIONS = DILATIONS[1:]


def _view_shape(S, dil, width, dtype):
    return jax.ShapeDtypeStruct((S // dil, dil * width), dtype)


def _view_spec(tm, dil, width):
    return pl.BlockSpec((tm // dil, dil * width), lambda i: (i, 0))


def _tile_scratch(tm, width):
    return pltpu.VMEM((width // LANES, tm, LANES), F32)


def _scratch_value(scr_ref):
    n = scr_ref.shape[0]
    return scr_ref[0] if n == 1 else jnp.concatenate([scr_ref[c] for c in range(n)], axis=1)


def _to_view(scr_ref, out_ref, dil, dtype):
    n, tm, _ = scr_ref.shape
    for r in range(dil):
        for c in range(n):
            col = (r * n + c) * LANES
            out_ref[:, col:col + LANES] = scr_ref[c, pl.ds(r, tm // dil, stride=dil), :].astype(dtype)


def _from_view(blk_ref, scr_ref, dil):
    n, tm, _ = scr_ref.shape
    for r in range(dil):
        for c in range(n):
            col = (r * n + c) * LANES
            scr_ref[c, pl.ds(r, tm // dil, stride=dil), :] = blk_ref[:, col:col + LANES].astype(F32)


def _bucket_matrix(dilation):
    rel = np.arange(KB)[None, :] - HALF - np.arange(QB)[:, None]
    band = np.abs(rel) <= HALF
    dist = np.clip(rel, -HALF, HALF) * dilation
    half_b = N_BUCKETS // 2
    max_exact = half_b // 2
    ret = np.where(dist > 0, half_b, 0)
    n = np.abs(dist)
    nf = np.maximum(n, 1).astype(np.float32)
    large = max_exact + (np.log(nf / np.float32(max_exact)) / np.float32(math.log(MAX_DISTANCE / max_exact))
                         * np.float32(half_b - max_exact)).astype(np.int32)
    large = np.minimum(large, half_b - 1)
    bucket = ret + np.where(n < max_exact, n, large)
    return np.where(band, bucket, -1).astype(np.int32)


def _ag_weights(w_in, w_out, cw_pad):
    n_arr = 3

    def body(win_ref, wout_ref, cw_ref, gin_ref, gout_ref, gcw_ref, send_sems, recv_sems):
        x, y, c = lax.axis_index("x"), lax.axis_index("y"), lax.axis_index("c")
        me = (x, y, c)
        sibling = (x, y, 1 - c)
        chips = [(1 - x, y), (x, 1 - y), (1 - x, 1 - y)]
        arrays = (gin_ref, gout_ref, gcw_ref)

        def slot(px, py, pc):
            return 4 * px + 2 * py + pc

        gin_ref[slot(*me)] = win_ref[...].astype(BF16)
        gout_ref[slot(*me)] = wout_ref[...].astype(BF16)
        gcw_ref[slot(*me)] = cw_ref[...]

        def copy(a, k, block, to):
            ref = arrays[a].at[slot(*block)]
            return pltpu.make_async_remote_copy(
                src_ref=ref, dst_ref=ref, send_sem=send_sems.at[a * 7 + k], recv_sem=recv_sems.at[a * 7 + k],
                device_id=to, device_id_type=MESH)

        first = [copy(a, 0, me, sibling) for a in range(n_arr)]
        for j, chip in enumerate(chips):
            first += [copy(a, 1 + j, me, (*chip, c)) for a in range(n_arr)]
        for cp in first:
            cp.start()
        passed = []
        for j, chip in enumerate(chips):
            for a in range(n_arr):
                copy(a, 1 + j, (*chip, c), me).wait_recv()
            for a in range(n_arr):
                cp = copy(a, 4 + j, (*chip, c), sibling)
                cp.start()
                passed.append(cp)
        for a in range(n_arr):
            copy(a, 0, sibling, me).wait_recv()
        for j, chip in enumerate(chips):
            for a in range(n_arr):
                copy(a, 4 + j, (*chip, 1 - c), me).wait_recv()
        for cp in first + passed:
            cp.wait_send()

    vm = pl.BlockSpec(memory_space=pltpu.VMEM)
    return pl.pallas_call(
        body, name="ag_weights",
        out_shape=(jax.ShapeDtypeStruct((N_DEV,) + w_in.shape, BF16),
                   jax.ShapeDtypeStruct((N_DEV,) + w_out.shape, BF16),
                   jax.ShapeDtypeStruct((N_DEV,) + cw_pad.shape, F32)),
        in_specs=[vm, vm, vm], out_specs=(vm, vm, vm),
        scratch_shapes=[pltpu.SemaphoreType.DMA((n_arr * 7,)), pltpu.SemaphoreType.DMA((n_arr * 7,))],
        compiler_params=pltpu.CompilerParams(vmem_limit_bytes=40 * MIB),
    )(w_in, w_out, cw_pad)


def _fwd_proj(x, norm_w, wg):
    S = x.shape[0]
    tm = 512

    def body(x_ref, nw_ref, wg_ref, u_ref, gb_ref, gc_ref, zc_ref, za_ref, *rest):
        q_refs, k_refs, v_refs, rq_refs, rk_refs = (rest[3 * n:3 * n + 3] for n in range(5))
        scr, rscr = rest[15:]
        xv = x_ref[...]
        r = lax.rsqrt(jnp.mean(xv * xv, axis=-1, keepdims=True) + EPS)
        h = (xv * r * nw_ref[...]).astype(BF16)
        lane = _lane((tm, LANES))
        lo = lane < HEAD_DIM
        plain = {0: u_ref, 1: gb_ref, 2: gc_ref, 3: zc_ref, 7: za_ref}
        normed = {4: (q_refs, rq_refs), 5: (k_refs, rk_refs)}

        def emit(src, refs, dtype):
            refs[0][...] = _scratch_value(src).astype(dtype)
            for dil, ref in zip(VIEW_DILATIONS, refs[1:]):
                _to_view(src, ref, dil, dtype)

        acc_next = jnp.dot(h, wg_ref[0], preferred_element_type=F32)
        for j in range(8):
            acc = acc_next
            if j < 7:
                acc_next = jnp.dot(h, wg_ref[j + 1], preferred_element_type=F32)
            if j in plain:
                plain[j][...] = acc.astype(BF16)
                continue
            if j == 6:
                for p in range(N_PAIRS):
                    scr[p] = acc[:, p * LANES:(p + 1) * LANES]
                emit(scr, v_refs, BF16)
                continue
            out_refs, r_refs = normed[j]
            r_tile = jnp.zeros((tm, LANES), F32)
            for p in range(N_PAIRS):
                blk = acc[:, p * LANES:(p + 1) * LANES]
                sq = blk * blk
                s_lo = jnp.sum(jnp.where(lo, sq, 0.0), axis=-1, keepdims=True)
                s_hi = jnp.sum(jnp.where(lo, 0.0, sq), axis=-1, keepdims=True)
                r_lo = lax.rsqrt(s_lo * (1.0 / HEAD_DIM) + EPS)
                r_hi = lax.rsqrt(s_hi * (1.0 / HEAD_DIM) + EPS)
                scr[p] = blk * jnp.where(lo, r_lo, r_hi)
                r_tile = jnp.where(lane == 2 * p, r_lo, r_tile)
                r_tile = jnp.where(lane == 2 * p + 1, r_hi, r_tile)
            rscr[0] = r_tile
            emit(scr, out_refs, BF16)
            emit(rscr, r_refs, F32)

    piece = jax.ShapeDtypeStruct((S, CONV_W), BF16)
    pspec = pl.BlockSpec((tm, CONV_W), lambda i: (i, 0))
    wide = [_view_shape(S, d, ATTN_W, BF16) for d in DILATIONS]
    wide_specs = [_view_spec(tm, d, ATTN_W) for d in DILATIONS]
    stat = [_view_shape(S, d, LANES, F32) for d in DILATIONS]
    stat_specs = [_view_spec(tm, d, LANES) for d in DILATIONS]
    outs = pl.pallas_call(
        body, name="fwd_proj", grid=(S // tm,),
        out_shape=[piece] * 5 + wide * 3 + stat * 2,
        in_specs=[pl.BlockSpec((tm, D_MODEL), lambda i: (i, 0)),
                  pl.BlockSpec((1, D_MODEL), lambda i: (0, 0)),
                  pl.BlockSpec(wg.shape, lambda i: (0, 0, 0), pipeline_mode=pl.Buffered(1))],
        out_specs=[pspec] * 5 + wide_specs * 3 + stat_specs * 2,
        scratch_shapes=[_tile_scratch(tm, ATTN_W), _tile_scratch(tm, LANES)],
        compiler_params=_params(("arbitrary",), 48),
    )(x, norm_w.reshape(1, D_MODEL), wg)
    u, gb, gc, zc, za = outs[:5]
    q_hat, k_hat, v, rq, rk = (outs[5 + 3 * n:8 + 3 * n] for n in range(5))
    return u, gb, gc, zc, za, q_hat, k_hat, v, rq, rk


def _bias_tables(rel_bias, bmat_t):
    def body(rb_ref, b_ref, out_ref):
        h = pl.program_id(1)
        b = b_ref[0]
        t = jnp.full((KB, QB), NEG, F32)
        for bk in range(N_BUCKETS):
            t = jnp.where(b == bk, rb_ref[bk, h], t)
        out_ref[0, 0] = t

    return pl.pallas_call(
        body, name="bias_tables", grid=(len(DILATIONS), N_HEADS),
        out_shape=jax.ShapeDtypeStruct((len(DILATIONS), N_HEADS, KB, QB), F32),
        in_specs=[pl.BlockSpec(memory_space=pltpu.SMEM),
                  pl.BlockSpec((1, KB, QB), lambda g, h: (g, 0, 0))],
        out_specs=pl.BlockSpec((1, 1, KB, QB), lambda g, h: (g, h, 0, 0)),
        compiler_params=_params(("arbitrary", "arbitrary"), 16),
    )(rel_bias, bmat_t)


def _bias_grad(ds_acc, bmat_t):
    present = [sorted(set(_bucket_matrix(d).ravel().tolist()) - {-1}) for d in DILATIONS]

    def body(ds_ref, b_ref, out_ref):
        lane = _lane((1, LANES))
        row = lax.broadcasted_iota(jnp.int32, (8, LANES), 0)
        out = jnp.zeros((8, LANES), F32)
        for e in range(2):
            vec = jnp.zeros((1, LANES), F32)
            for g in range(len(DILATIONS)):
                b = b_ref[g]
                ds = ds_ref[g, 0, :, e * QB:(e + 1) * QB]
                for bk in present[g]:
                    s = jnp.sum(jnp.where(b == bk, ds, 0.0), axis=-1, keepdims=True)
                    s = jnp.sum(s, axis=0, keepdims=True)
                    vec = vec + jnp.where(lane == bk, s, 0.0)
            out = jnp.where(row == e, vec, out)
        out_ref[0] = out

    return pl.pallas_call(
        body, name="bias_grad", grid=(N_PAIRS,),
        out_shape=jax.ShapeDtypeStruct((N_PAIRS, 8, LANES), F32),
        in_specs=[pl.BlockSpec((len(DILATIONS), 1, KB, 2 * QB), lambda p: (0, p, 0, 0)),
                  pl.BlockSpec((len(DILATIONS), KB, QB), lambda p: (0, 0, 0))],
        out_specs=pl.BlockSpec((1, 8, LANES), lambda p: (p, 0, 0)),
        compiler_params=_params(("arbitrary",), 16),
    )(ds_acc, bmat_t)


def _halo_specs(width, rows, T, L, cols_of):
    per = T // rows
    last = L // rows - 1
    prev = pl.BlockSpec((rows, width), lambda r, i: (jnp.maximum(i * per - 1, 0), cols_of(r)))
    nxt = pl.BlockSpec((rows, width), lambda r, i: (jnp.minimum((i + 1) * per, last), cols_of(r)))
    return prev, nxt


def _attn_fwd(g, dil, qv, kv, vv, gain, tables):
    L = qv.shape[0]
    T = min(1024, L)
    n_sub = T // QB

    def body(q_ref, km_ref, kp_ref, kn_ref, vm_ref, vp_ref, vn_ref, gain_ref, bias_ref, o_ref, lse_ref,
             kwin, vwin, s_scr, p_scr):
        i = pl.program_id(1)
        kwin[0:HALF] = kp_ref[...]
        kwin[HALF:HALF + T] = km_ref[...]
        kwin[HALF + T:] = kn_ref[...]
        vwin[0:HALF] = vp_ref[...]
        vwin[HALF:HALF + T] = vm_ref[...]
        vwin[HALF + T:] = vn_ref[...]
        lo = _lane((QB, LANES)) < HEAD_DIM
        krow = lax.broadcasted_iota(jnp.int32, (KB, 1), 0)
        chan_lo = lax.broadcasted_iota(jnp.int32, (LANES, 1), 0) < HEAD_DIM
        hrow = lax.broadcasted_iota(jnp.int32, (LANES, QB), 0)

        def sub(j, carry):
            r0 = pl.multiple_of(j * QB, QB)
            kpos = i * T + j * QB - HALF + krow
            kvalid = (kpos >= 0) & (kpos < L)
            lse_rows = jnp.zeros((LANES, QB), F32)
            cols = [slice(p * LANES, (p + 1) * LANES) for p in range(N_PAIRS)]
            for p, cs in enumerate(cols):
                qf = q_ref[pl.ds(r0, QB), cs].astype(F32) * gain_ref[:, cs]
                q2 = jnp.concatenate([jnp.where(lo, qf, 0.0), jnp.where(lo, 0.0, qf)], axis=0).astype(BF16)
                s_scr[p] = lax.dot_general(kwin[pl.ds(r0, KB), cs], q2, (((1,), (1,)), ((), ())),
                                           preferred_element_type=F32)
            inv_l = []
            for p in range(N_PAIRS):
                bias2 = jnp.concatenate([bias_ref[0, 2 * p], bias_ref[0, 2 * p + 1]], axis=1)
                logits = jnp.where(kvalid, s_scr[p] + bias2, NEG)
                m = jnp.max(logits, axis=0, keepdims=True)
                pt = jnp.exp(logits - m)
                l = jnp.sum(pt, axis=0, keepdims=True)
                p_scr[p] = pt.astype(BF16)
                inv_l.append(1.0 / l)
                lse2 = m + jnp.log(l)
                lse_rows = jnp.where(hrow == 2 * p, lse2[:, 0:QB], lse_rows)
                lse_rows = jnp.where(hrow == 2 * p + 1, lse2[:, QB:2 * QB], lse_rows)
            for p, cs in enumerate(cols):
                s_scr[p, 0:LANES, :] = lax.dot_general(vwin[pl.ds(r0, KB), cs], p_scr[p], (((0,), (0,)), ((), ())),
                                                       preferred_element_type=F32) * inv_l[p]
            for p, cs in enumerate(cols):
                ot2 = s_scr[p, 0:LANES, :]
                ot = jnp.where(chan_lo, ot2[:, 0:QB], ot2[:, QB:2 * QB])
                o_ref[pl.ds(r0, QB), cs] = ot.T.astype(BF16)
            lse_ref[pl.ds(r0, QB), :] = lse_rows.T
            return carry

        lax.fori_loop(0, n_sub, sub, 0)

    main = pl.BlockSpec((T, ATTN_W), lambda r, i: (i, r))
    prev, nxt = _halo_specs(ATTN_W, HALF, T, L, lambda r: r)
    o_g, lse_g = pl.pallas_call(
        body, name=f"attn_fwd_d{dil}", grid=(dil, L // T),
        out_shape=(jax.ShapeDtypeStruct((L, dil * ATTN_W), BF16), jax.ShapeDtypeStruct((L, dil * LANES), F32)),
        in_specs=[main, main, prev, nxt, main, prev, nxt,
                  pl.BlockSpec((1, ATTN_W), lambda r, i: (0, 0)),
                  pl.BlockSpec((1, N_HEADS, KB, QB), lambda r, i: (g, 0, 0, 0))],
        out_specs=(main, pl.BlockSpec((T, LANES), lambda r, i: (i, r))),
        scratch_shapes=[pltpu.VMEM((T + 2 * HALF, ATTN_W), BF16), pltpu.VMEM((T + 2 * HALF, ATTN_W), BF16),
                        pltpu.VMEM((N_PAIRS, KB, 2 * QB), F32), pltpu.VMEM((N_PAIRS, KB, 2 * QB), BF16)],
        compiler_params=_params(("arbitrary", "arbitrary"), 40),
    )(qv, kv, kv, kv, vv, vv, vv, gain, tables)
    return o_g, lse_g


HALO = 16
CHUNK = 128


def _row_halo_specs(width, tm, S, col=0):
    rows = HALO
    per = tm // rows
    last = S // rows - 1
    prev = pl.BlockSpec((rows, width), lambda i: (jnp.maximum(i * per - 1, 0), col))
    nxt = pl.BlockSpec((rows, width), lambda i: (jnp.minimum((i + 1) * per, last), col))
    return prev, nxt


def _shift_rows(a, prev_row, next_row, tm):
    row = lax.broadcasted_iota(jnp.int32, a.shape, 0)
    a_m1 = jnp.where(row == 0, prev_row, pltpu.roll(a, 1, 0))
    a_p1 = jnp.where(row == tm - 1, next_row, pltpu.roll(a, tm - 1, 0))
    return a_m1, a_p1


def _combine_gates(o_parts, lse_parts, u, gb, gc, zc, za, conv_w, conv_b):
    S = u.shape[0]
    tm = 512
    n_t = S // tm

    def body(o1, o2, o3, l1, l2, l3, u_ref, up_ref, un_ref, gb_ref, gc_ref, gcp_ref, gcn_ref, zc_ref, za_ref,
             cw_ref, cb_ref, y_ref, o_ref, lse_ref, lse2_ref, lse3_ref, so2, so3, sl2, sl3, slse):
        i = pl.program_id(0)
        for blk, scr, dil in ((o2, so2, DILATIONS[1]), (o3, so3, DILATIONS[2]),
                              (l2, sl2, DILATIONS[1]), (l3, sl3, DILATIONS[2])):
            _from_view(blk, scr, dil)
        ls = [l1[...], sl2[0], sl3[0]]
        lmax = jnp.maximum(jnp.maximum(ls[0], ls[1]), ls[2])
        es = [jnp.exp(l - lmax) for l in ls]
        den = es[0] + es[1] + es[2]
        slse[0] = lmax + jnp.log(den)
        lse_ref[...] = slse[0]
        _to_view(slse, lse2_ref, DILATIONS[1], F32)
        _to_view(slse, lse3_ref, DILATIONS[2], F32)
        inv = 1.0 / den
        ws = [e * inv for e in es]
        for p in range(N_PAIRS):
            cs = slice(p * LANES, (p + 1) * LANES)
            acc = jnp.zeros((tm, LANES), F32)
            spread = _head_to_lanes(p)
            for w, o_g in zip(ws, (o1[:, cs].astype(F32), so2[p], so3[p])):
                acc = acc + _split_dot(w, spread) * o_g
            o_ref[:, cs] = acc.astype(BF16)
            za = za_ref[:, cs].astype(F32)
            y_ref[:, CONV_W + p * LANES:CONV_W + (p + 1) * LANES] = (acc * za * _sigmoid(za)).astype(BF16)
        a = gc_ref[...].astype(F32) * u_ref[...].astype(F32)
        a_prev = gcp_ref[15:16, :].astype(F32) * up_ref[15:16, :].astype(F32) * (i > 0).astype(F32)
        a_next = gcn_ref[0:1, :].astype(F32) * un_ref[0:1, :].astype(F32) * (i < n_t - 1).astype(F32)
        a_m1, a_p1 = _shift_rows(a, a_prev, a_next, tm)
        z3 = cw_ref[0:1, :] * a_m1 + cw_ref[1:2, :] * a + cw_ref[2:3, :] * a_p1
        zc = zc_ref[...].astype(F32)
        y_ref[:, 0:CONV_W] = (gb_ref[...].astype(F32) * (z3 + cb_ref[...]) * zc * _sigmoid(zc)).astype(BF16)

    pspec = pl.BlockSpec((tm, CONV_W), lambda i: (i, 0))
    hp, hn = _row_halo_specs(CONV_W, tm, S)
    wide_specs = [_view_spec(tm, d, ATTN_W) for d in DILATIONS]
    stat_specs = [_view_spec(tm, d, LANES) for d in DILATIONS]
    outs = pl.pallas_call(
        body, name="combine_gates", grid=(n_t,),
        out_shape=[jax.ShapeDtypeStruct((S, D_MODEL), BF16), jax.ShapeDtypeStruct((S, ATTN_W), BF16)]
        + [_view_shape(S, d, LANES, F32) for d in DILATIONS],
        in_specs=wide_specs + stat_specs + [pspec, hp, hn, pspec, pspec, hp, hn, pspec, pspec,
                                            pl.BlockSpec((3, CONV_W), lambda i: (0, 0)),
                                            pl.BlockSpec((1, CONV_W), lambda i: (0, 0))],
        out_specs=[pl.BlockSpec((tm, D_MODEL), lambda i: (i, 0)), pspec] + stat_specs,
        scratch_shapes=[_tile_scratch(tm, ATTN_W), _tile_scratch(tm, ATTN_W),
                        _tile_scratch(tm, LANES), _tile_scratch(tm, LANES), _tile_scratch(tm, LANES)],
        compiler_params=_params(("arbitrary",), 40),
    )(*o_parts, *lse_parts, u, u, u, gb, gc, gc, gc, zc, za, conv_w, conv_b.reshape(1, CONV_W))
    return outs[0], outs[1], outs[2:]


def _out_proj(y, x, target, wo):
    S = x.shape[0]
    tm = 512

    def body(y_ref, x_ref, t_ref, wo_ref, g_ref, dy_ref, dwo_ref, loss_ref):
        i = pl.program_id(0)

        @pl.when(i == 0)
        def _():
            dwo_ref[...] = jnp.zeros_like(dwo_ref)
            loss_ref[...] = jnp.zeros_like(loss_ref)

        yv = y_ref[...]
        wo_v = wo_ref[...]
        err = x_ref[...] + jnp.dot(yv, wo_v, preferred_element_type=F32) - t_ref[...]
        e2 = (err * err).reshape(tm // 8, 8, D_MODEL).sum(axis=0)
        part = e2[:, 0:LANES]
        for k in range(1, D_MODEL // LANES):
            part = part + e2[:, k * LANES:(k + 1) * LANES]
        loss_ref[...] += part
        gv = (err * (1.0 / D_MODEL)).astype(BF16)
        g_ref[...] = gv
        dy_ref[...] = lax.dot_general(gv, wo_v, (((1,), (1,)), ((), ())), preferred_element_type=F32).astype(BF16)
        dwo_ref[...] += lax.dot_general(yv, gv, (((0,), (0,)), ((), ())), preferred_element_type=F32)

    tile = pl.BlockSpec((tm, D_MODEL), lambda i: (i, 0))
    return pl.pallas_call(
        body, name="out_proj", grid=(S // tm,),
        out_shape=(jax.ShapeDtypeStruct((S, D_MODEL), BF16), jax.ShapeDtypeStruct((S, D_MODEL), BF16),
                   jax.ShapeDtypeStruct((D_MODEL, D_MODEL), F32), jax.ShapeDtypeStruct((8, LANES), F32)),
        in_specs=[tile, tile, tile, pl.BlockSpec((D_MODEL, D_MODEL), lambda i: (0, 0), pipeline_mode=pl.Buffered(1))],
        out_specs=(tile, tile, pl.BlockSpec((D_MODEL, D_MODEL), lambda i: (0, 0)),
                   pl.BlockSpec((8, LANES), lambda i: (0, 0))),
        compiler_params=_params(("arbitrary",), 48),
    )(y, x, target, wo)


def _gates_bwd(dy, u, gb, gc, zc, za, o, conv_w, conv_b):
    S = u.shape[0]
    tm = 512
    n_t = S // tm

    def body(dy_ref, dyp_ref, dyn_ref, u_ref, up_ref, un_ref, gb_ref, gbp_ref, gbn_ref, gc_ref, gcp_ref, gcn_ref,
             zc_ref, zcp_ref, zcn_ref, za_ref, o_ref, cw_ref, cb_ref,
             du_ref, dgb_ref, dgc_ref, dzc_ref, dza_ref, small_ref,
             do_ref, do2_ref, do3_ref, delta_ref, delta2_ref, delta3_ref, sdo, sdelta,
             u_ext, gc_ext, gb_ext, zc_ext, dy_ext):
        i = pl.program_id(0)

        @pl.when(i == 0)
        def _():
            small_ref[...] = jnp.zeros_like(small_ref)

        exts = ((u_ext, up_ref, u_ref, un_ref), (gc_ext, gcp_ref, gc_ref, gcn_ref), (gb_ext, gbp_ref, gb_ref, gbn_ref),
                (zc_ext, zcp_ref, zc_ref, zcn_ref))
        for ext, pr, mn, nx in exts:
            ext[0:HALO] = pr[...]
            ext[HALO:HALO + tm] = mn[...]
            ext[HALO + tm:] = nx[...]
        dy_ext[0:HALO] = dyp_ref[...]
        dy_ext[HALO:HALO + tm] = dy_ref[:, 0:CONV_W]
        dy_ext[HALO + tm:] = dyn_ref[...]
        zero_halo = jnp.zeros((HALO, CONV_W), BF16)

        @pl.when(i == 0)
        def _():
            u_ext[0:HALO] = zero_halo
            dy_ext[0:HALO] = zero_halo

        @pl.when(i == n_t - 1)
        def _():
            u_ext[HALO + tm:] = zero_halo
            dy_ext[HALO + tm:] = zero_halo

        wide_rows = CHUNK + 2 * HALO
        mid = slice(HALO, HALO + CHUNK)
        csum = lambda t: jnp.sum(t, axis=0, keepdims=True)

        def one_group(p, c0, sums):
            cs = slice(p * LANES, (p + 1) * LANES)
            w0, w1, w2, cb = cw_ref[0:1, cs], cw_ref[1:2, cs], cw_ref[2:3, cs], cb_ref[:, cs]
            rows = pl.ds(c0, CHUNK)
            wide = pl.ds(c0, wide_rows)
            u_e, gc_e, gb_e, zc_e, dy_e = (r[wide, cs].astype(F32) for r in (u_ext, gc_ext, gb_ext, zc_ext, dy_ext))
            a_e = gc_e * u_e
            sz_e = _sigmoid(zc_e)
            dz3_e = dy_e * zc_e * sz_e * gb_e
            a, a_m1, a_p1 = a_e[mid], pltpu.roll(a_e, 1, 0)[mid], pltpu.roll(a_e, wide_rows - 1, 0)[mid]
            dz3, dz3_m1, dz3_p1 = dz3_e[mid], pltpu.roll(dz3_e, 1, 0)[mid], pltpu.roll(dz3_e, wide_rows - 1, 0)[mid]
            uv, gcv, gbv, zcv, dyc, sz = u_e[mid], gc_e[mid], gb_e[mid], zc_e[mid], dy_e[mid], sz_e[mid]
            cvp = w0 * a_m1 + w1 * a + w2 * a_p1 + cb
            dcv = dyc * zcv * sz
            dzc_ref[rows, cs] = (dyc * gbv * cvp * (sz * (1.0 + zcv * (1.0 - sz)))).astype(BF16)
            dgb_ref[rows, cs] = (dcv * cvp).astype(BF16)
            da = w0 * dz3_p1 + w1 * dz3 + w2 * dz3_m1
            du_ref[rows, cs] = (da * gcv).astype(BF16)
            dgc_ref[rows, cs] = (da * uv).astype(BF16)
            sums = (sums[0] + csum(dz3 * a_m1), sums[1] + csum(dz3 * a), sums[2] + csum(dz3 * a_p1),
                    sums[3] + csum(dz3))

            dya = dy_ref[rows, CONV_W + p * LANES:CONV_W + (p + 1) * LANES].astype(F32)
            zav = za_ref[rows, cs].astype(F32)
            ov = o_ref[rows, cs].astype(F32)
            sa = _sigmoid(zav)
            d_o = dya * zav * sa
            sdo[p, rows, :] = d_o
            do_ref[rows, cs] = d_o.astype(BF16)
            dza_ref[rows, cs] = (dya * ov * (sa * (1.0 + zav * (1.0 - sa)))).astype(BF16)
            return sums, _split_dot(d_o * ov, _lanes_to_head(p, 1.0))

        def chunk(ci, carry):
            c0 = pl.multiple_of(ci * CHUNK, CHUNK)
            out, delta = [], jnp.zeros((CHUNK, LANES), F32)
            for p in range(N_PAIRS):
                sums, part = one_group(p, c0, carry[p])
                out.append(sums)
                delta = delta + part
            sdelta[0, pl.ds(c0, CHUNK), :] = delta
            return tuple(out)

        z = jnp.zeros((1, LANES), F32)
        all_sums = lax.fori_loop(0, tm // CHUNK, chunk, ((z, z, z, z),) * N_PAIRS)
        for p in range(N_PAIRS):
            for k in range(4):
                small_ref[k:k + 1, p * LANES:(p + 1) * LANES] += all_sums[p][k]
        delta_ref[...] = sdelta[0]
        for dil, wide_ref, stat_ref in zip(VIEW_DILATIONS, (do2_ref, do3_ref), (delta2_ref, delta3_ref)):
            _to_view(sdo, wide_ref, dil, BF16)
            _to_view(sdelta, stat_ref, dil, F32)

    pspec = pl.BlockSpec((tm, CONV_W), lambda i: (i, 0))
    hp, hn = _row_halo_specs(CONV_W, tm, S)
    piece = jax.ShapeDtypeStruct((S, CONV_W), BF16)
    outs = pl.pallas_call(
        body, name="gates_bwd", grid=(n_t,),
        out_shape=[piece] * 5 + [jax.ShapeDtypeStruct((8, CONV_W), F32)]
        + [_view_shape(S, d, ATTN_W, BF16) for d in DILATIONS] + [_view_shape(S, d, LANES, F32) for d in DILATIONS],
        in_specs=[pl.BlockSpec((tm, D_MODEL), lambda i: (i, 0)), hp, hn,
                  pspec, hp, hn, pspec, hp, hn, pspec, hp, hn, pspec, hp, hn, pspec, pspec,
                  pl.BlockSpec((3, CONV_W), lambda i: (0, 0)), pl.BlockSpec((1, CONV_W), lambda i: (0, 0))],
        out_specs=[pspec] * 5 + [pl.BlockSpec((8, CONV_W), lambda i: (0, 0))]
        + [_view_spec(tm, d, ATTN_W) for d in DILATIONS] + [_view_spec(tm, d, LANES) for d in DILATIONS],
        scratch_shapes=[_tile_scratch(tm, ATTN_W), _tile_scratch(tm, LANES)]
        + [pltpu.VMEM((tm + 2 * HALO, CONV_W), BF16)] * 5,
        compiler_params=_params(("arbitrary",), 40),
    )(dy, dy, dy, u, u, u, gb, gb, gb, gc, gc, gc, zc, zc, zc, za, o, conv_w, conv_b.reshape(1, CONV_W))
    return outs[:5], outs[5], outs[6:9], outs[9:12]


def _attn_bwd(g, dil, qv, kv, vv, dov, lv, dv_, rqv, rkv, gain, tables):
    L = qv.shape[0]
    T = min(1024, L)
    n_sub = T // QB + 1
    QW = T + 2 * HALF
    KW = T + 2 * QB

    def body(qm_ref, qp_ref, qn_ref, dom_ref, dop_ref, don_ref, km_ref, kp_ref, kn_ref, vm_ref, vp_ref, vn_ref,
             lm_ref, lp_ref, ln_ref, dm_ref, dp_ref, dn_ref, rqm_ref, rqp_ref, rqn_ref, rk_ref,
             gain_ref, bias_ref,
             dq_ref, dk_ref, dv_ref, ds_ref, a_ref,
             qwin, dowin, kwin, vwin, lwin, dwin, rqwin, dqwin, dkacc, dvacc,
             s_scr, d_scr, p_scr, ds_scr, q2_scr, do2_scr, dq_scr):
        r = pl.program_id(0)
        i = pl.program_id(1)

        @pl.when((r == 0) & (i == 0))
        def _():
            ds_ref[...] = jnp.zeros_like(ds_ref)
            a_ref[...] = jnp.zeros_like(a_ref)

        for win, (pr, mn, nx), h_rows in ((qwin, (qp_ref, qm_ref, qn_ref), HALF),
                                          (dowin, (dop_ref, dom_ref, don_ref), HALF),
                                          (lwin, (lp_ref, lm_ref, ln_ref), HALF),
                                          (dwin, (dp_ref, dm_ref, dn_ref), HALF),
                                          (rqwin, (rqp_ref, rqm_ref, rqn_ref), HALF),
                                          (kwin, (kp_ref, km_ref, kn_ref), QB),
                                          (vwin, (vp_ref, vm_ref, vn_ref), QB)):
            win[0:h_rows] = pr[...]
            win[h_rows:h_rows + T] = mn[...]
            win[h_rows + T:] = nx[...]
        dkacc[0:QB] = jnp.zeros((QB, ATTN_W), F32)
        dvacc[0:QB] = jnp.zeros((QB, ATTN_W), F32)

        lo = _lane((QB, LANES)) < HEAD_DIM
        same_head = _same_head()
        krow = lax.broadcasted_iota(jnp.int32, (KB, 1), 0)
        qcol = _lane((1, 2 * QB)) % QB
        row = lax.broadcasted_iota(jnp.int32, (QB, 1), 0)
        chan_lo = lax.broadcasted_iota(jnp.int32, (LANES, 1), 0) < HEAD_DIM

        cols = [slice(p * LANES, (p + 1) * LANES) for p in range(N_PAIRS)]
        nt = (((1,), (1,)), ((), ()))

        def scores(j):
            r0 = pl.multiple_of(j * QB, QB)
            for k, cs in enumerate(cols):
                qf = qwin[pl.ds(r0, QB), cs].astype(F32) * gain_ref[:, cs]
                q2_scr[k] = jnp.concatenate([jnp.where(lo, qf, 0.0), jnp.where(lo, 0.0, qf)], axis=0).astype(BF16)
                dov = dowin[pl.ds(r0, QB), cs]
                zero = jnp.zeros_like(dov)
                do2_scr[k] = jnp.concatenate([jnp.where(lo, dov, zero), jnp.where(lo, zero, dov)], axis=0)
                s_scr[k] = lax.dot_general(kwin[pl.ds(r0, KB), cs], q2_scr[k], nt, preferred_element_type=F32)
                d_scr[k] = lax.dot_general(vwin[pl.ds(r0, KB), cs], do2_scr[k], nt, preferred_element_type=F32)

        def sub(j):
            r0 = pl.multiple_of(j * QB, QB)
            kpos = i * T - QB + j * QB + krow
            qpos = i * T - HALF + j * QB + qcol
            valid = (kpos >= 0) & (kpos < L) & (qpos >= 0) & (qpos < L)
            owned_t = ((qpos >= i * T) & (qpos < i * T + T)).astype(F32)
            qpos_c = i * T - HALF + j * QB + row
            owned = ((qpos_c >= i * T) & (qpos_c < i * T + T)).astype(F32)
            lse_rows = lwin[pl.ds(r0, QB), :].T
            delta_rows = dwin[pl.ds(r0, QB), :].T
            rq_t = rqwin[pl.ds(r0, QB), :]
            for p in range(N_PAIRS):
                k = p
                bias2 = jnp.concatenate([bias_ref[0, 2 * p], bias_ref[0, 2 * p + 1]], axis=1)
                lse2 = jnp.concatenate([lse_rows[2 * p:2 * p + 1, :], lse_rows[2 * p + 1:2 * p + 2, :]], axis=1)
                delta2 = jnp.concatenate([delta_rows[2 * p:2 * p + 1, :], delta_rows[2 * p + 1:2 * p + 2, :]], axis=1)
                pt = jnp.where(valid, jnp.exp(s_scr[k] + bias2 - lse2), 0.0)
                dst = pt * (d_scr[k] - delta2)
                ds_ref[p] += dst * owned_t
                p_scr[p] = pt.astype(BF16)
                ds_scr[p] = dst.astype(BF16)
            for p, cs in enumerate(cols):
                k = p
                for acc_ref, lhs, rhs in ((dvacc, p_scr[p], do2_scr[k]), (dkacc, ds_scr[p], q2_scr[k])):
                    part = jnp.dot(lhs, rhs, preferred_element_type=F32)
                    acc_ref[pl.ds(r0, QB), cs] += part[0:QB]
                    acc_ref[pl.ds(r0 + QB, QB), cs] = part[QB:KB]
                dq_scr[p] = lax.dot_general(kwin[pl.ds(r0, KB), cs], ds_scr[p], (((0,), (0,)), ((), ())),
                                            preferred_element_type=F32)
            for p, cs in enumerate(cols):
                g2 = gain_ref[:, cs]
                q_f = qwin[pl.ds(r0, QB), cs].astype(F32)
                dqt2 = dq_scr[p]
                dqm2 = jnp.where(chan_lo, dqt2[:, 0:QB], dqt2[:, QB:2 * QB]).T
                a_ref[p] += (dqm2 * q_f * owned).reshape(QB // 8, 8, LANES).sum(axis=0)
                dqh = dqm2 * g2
                mean_c = _split_dot(dqh * q_f, same_head) * (1.0 / HEAD_DIM)
                rq2 = _split_dot(rq_t, _head_to_lanes(p))
                dqwin[pl.ds(r0, QB), cs] = (rq2 * (dqh - q_f * mean_c)).astype(BF16)

        def one_sub(j, carry):
            scores(j)
            sub(j)
            return carry

        lax.fori_loop(0, n_sub, one_sub, 0)
        dq_ref[...] = dqwin[HALF:HALF + T]

        def finish(j, carry):
            r0 = pl.multiple_of(j * QB, QB)
            rk_t = rk_ref[pl.ds(r0, QB), :]
            for p in range(N_PAIRS):
                cs = slice(p * LANES, (p + 1) * LANES)
                dkh = dkacc[pl.ds(QB + r0, QB), cs]
                k_f = km_ref[pl.ds(r0, QB), cs].astype(F32)
                mean_c = _split_dot(dkh * k_f, same_head) * (1.0 / HEAD_DIM)
                rk2 = _split_dot(rk_t, _head_to_lanes(p))
                dk_ref[pl.ds(r0, QB), cs] = (rk2 * (dkh - k_f * mean_c)).astype(BF16)
                dv_ref[pl.ds(r0, QB), cs] = dvacc[pl.ds(QB + r0, QB), cs].astype(BF16)
            return carry

        lax.fori_loop(0, T // QB, finish, 0)

    main = pl.BlockSpec((T, ATTN_W), lambda r, i: (i, r))
    smain = pl.BlockSpec((T, LANES), lambda r, i: (i, r))
    q_prev, q_next = _halo_specs(ATTN_W, HALF, T, L, lambda r: r)
    k_prev, k_next = _halo_specs(ATTN_W, QB, T, L, lambda r: r)
    s_prev, s_next = _halo_specs(LANES, HALF, T, L, lambda r: r)
    piece = jax.ShapeDtypeStruct((L, dil * ATTN_W), BF16)
    return pl.pallas_call(
        body, name=f"attn_bwd_d{dil}", grid=(dil, L // T),
        out_shape=(piece, piece, piece, jax.ShapeDtypeStruct((N_PAIRS, KB, 2 * QB), F32),
                   jax.ShapeDtypeStruct((N_PAIRS, 8, LANES), F32)),
        in_specs=[main, q_prev, q_next, main, q_prev, q_next, main, k_prev, k_next, main, k_prev, k_next,
                  smain, s_prev, s_next, smain, s_prev, s_next, smain, s_prev, s_next, smain,
                  pl.BlockSpec((1, ATTN_W), lambda r, i: (0, 0)),
                  pl.BlockSpec((1, N_HEADS, KB, QB), lambda r, i: (g, 0, 0, 0))],
        out_specs=(main, main, main, pl.BlockSpec((N_PAIRS, KB, 2 * QB), lambda r, i: (0, 0, 0)),
                   pl.BlockSpec((N_PAIRS, 8, LANES), lambda r, i: (0, 0, 0))),
        scratch_shapes=[pltpu.VMEM((QW, ATTN_W), BF16), pltpu.VMEM((QW, ATTN_W), BF16),
                        pltpu.VMEM((KW, ATTN_W), BF16), pltpu.VMEM((KW, ATTN_W), BF16),
                        pltpu.VMEM((QW, LANES), F32), pltpu.VMEM((QW, LANES), F32), pltpu.VMEM((QW, LANES), F32),
                        pltpu.VMEM((QW, ATTN_W), BF16),
                        pltpu.VMEM((KW, ATTN_W), F32), pltpu.VMEM((KW, ATTN_W), F32),
                        pltpu.VMEM((N_PAIRS, KB, 2 * QB), F32), pltpu.VMEM((N_PAIRS, KB, 2 * QB), F32),
                        pltpu.VMEM((N_PAIRS, KB, 2 * QB), BF16), pltpu.VMEM((N_PAIRS, KB, 2 * QB), BF16),
                        pltpu.VMEM((N_PAIRS, 2 * QB, LANES), BF16), pltpu.VMEM((N_PAIRS, 2 * QB, LANES), BF16),
                        pltpu.VMEM((N_PAIRS, LANES, 2 * QB), F32)],
        compiler_params=_params(("arbitrary", "arbitrary"), 56),
    )(qv, qv, qv, dov, dov, dov, kv, kv, kv, vv, vv, vv, lv, lv, lv, dv_, dv_, dv_, rqv, rqv, rqv, rkv, gain, tables)


def _proj_bwd(x, d_out, norm_w, wg, pieces):
    S = x.shape[0]
    tm = 512
    flat = [share for p in pieces for share in p]
    n_t = S // tm

    def body(*refs):
        x_ref, g_ref, nw_ref, wg_ref = refs[:4]
        piece_refs = refs[4:4 + len(flat)]
        gx_ref, dnw_ref, ht_ref, dproj_ref, sview = refs[4 + len(flat):]
        i = pl.program_id(0)

        @pl.when(i == 0)
        def _():
            dnw_ref[...] = jnp.zeros_like(dnw_ref)

        xv = x_ref[...]
        r = lax.rsqrt(jnp.mean(xv * xv, axis=-1, keepdims=True) + EPS)
        xh = xv * r
        nw = nw_ref[...]
        ht_ref[...] = (xh * nw).T.astype(BF16)
        dh = jnp.zeros((tm, D_MODEL), F32)
        k = 0
        for j in range(8):
            shares = pieces[j]
            if len(shares) == 1:
                dp = piece_refs[k][...]
            else:
                dp = jnp.zeros((tm, CONV_W), F32)
                for t, (_, dil) in enumerate(shares):
                    if dil == 1:
                        dp = dp + piece_refs[k + t][...].astype(F32)
                    else:
                        _from_view(piece_refs[k + t], sview, dil)
                        dp = dp + _scratch_value(sview)
                dp = dp.astype(BF16)
            k += len(shares)
            dproj_ref[j] = dp
            dh = dh + lax.dot_general(dp, wg_ref[j], (((1,), (1,)), ((), ())), preferred_element_type=F32)
        dnw_ref[...] += (dh * xh).reshape(tm // 8, 8, D_MODEL).sum(axis=0)
        dxh = dh * nw
        mean_c = jnp.mean(dxh * xh, axis=-1, keepdims=True)
        gx_ref[...] = g_ref[...].astype(F32) + r * (dxh - xh * mean_c)

    tile = pl.BlockSpec((tm, D_MODEL), lambda i: (i, 0))
    return pl.pallas_call(
        body, name="proj_bwd", grid=(n_t,),
        out_shape=(jax.ShapeDtypeStruct((S, D_MODEL), F32), jax.ShapeDtypeStruct((8, D_MODEL), F32),
                   jax.ShapeDtypeStruct((D_MODEL, S), BF16), jax.ShapeDtypeStruct((8, S, CONV_W), BF16)),
        in_specs=[tile, tile, pl.BlockSpec((1, D_MODEL), lambda i: (0, 0)),
                  pl.BlockSpec(wg.shape, lambda i: (0, 0, 0), pipeline_mode=pl.Buffered(1))]
        + [_view_spec(tm, dil, CONV_W) for _, dil in flat],
        out_specs=(tile, pl.BlockSpec((8, D_MODEL), lambda i: (0, 0)),
                   pl.BlockSpec((D_MODEL, tm), lambda i: (0, i)), pl.BlockSpec((8, tm, CONV_W), lambda i: (0, i, 0))),
        scratch_shapes=[_tile_scratch(tm, CONV_W)],
        compiler_params=_params(("arbitrary",), 56),
    )(x, d_out, norm_w.reshape(1, D_MODEL), wg, *[a for a, _ in flat])


def _dw_exchange(ht, dproj, gw_out, small):
    D, S = ht.shape
    tk = min(4096, S)
    n_t = S // tk
    me_outer = 4 * lax.axis_index("x") + 2 * lax.axis_index("y") + lax.axis_index("c")
    order = ((me_outer + 1 + jnp.arange(N_DEV, dtype=jnp.int32)) % N_DEV).astype(jnp.int32)

    def body(order_ref, ht_hbm, dp_ref, gout_ref, sm_ref, rin_ref, rout_ref, rsm_ref,
             acc, sbuf, ht_vmem, in_send, in_recv, side_send, side_recv, local_sems, ht_sems):
        s = pl.program_id(0)
        t = pl.program_id(1)
        x, y, c = lax.axis_index("x"), lax.axis_index("y"), lax.axis_index("c")
        me = 4 * x + 2 * y + c

        def ht_copy(j):
            return pltpu.make_async_copy(ht_hbm.at[:, j * tk:(j + 1) * tk], ht_vmem.at[j], ht_sems.at[j])

        def in_copy(k):
            to = (me + 1 + k) % N_DEV
            return pltpu.make_async_remote_copy(
                src_ref=sbuf.at[k % 2], dst_ref=rin_ref.at[me], send_sem=in_send.at[k], recv_sem=in_recv.at[k],
                device_id=(to // 4, (to // 2) % 2, to % 2), device_id_type=MESH)

        def in_landing(k):
            frm = (me + 2 * N_DEV - 1 - k) % N_DEV
            return pltpu.make_async_remote_copy(
                src_ref=sbuf.at[0], dst_ref=rin_ref.at[frm], send_sem=in_send.at[k], recv_sem=in_recv.at[k],
                device_id=(x, y, c), device_id_type=MESH)

        def side_copies():
            local = [pltpu.make_async_copy(gout_ref.at[me], rout_ref.at[me], local_sems.at[0]),
                     pltpu.make_async_copy(sm_ref, rsm_ref.at[me], local_sems.at[1])]
            remote = []
            for k in range(1, N_DEV):
                px = 1 - x if k & 4 else x
                py = 1 - y if k & 2 else y
                pc = 1 - c if k & 1 else c
                peer = 4 * px + 2 * py + pc
                for a, (src, dst) in enumerate(((gout_ref.at[peer], rout_ref.at[me]), (sm_ref, rsm_ref.at[me]))):
                    remote.append(pltpu.make_async_remote_copy(
                        src_ref=src, dst_ref=dst, send_sem=side_send.at[a * 7 + k - 1],
                        recv_sem=side_recv.at[a * 7 + k - 1], device_id=(px, py, pc), device_id_type=MESH))
            return local, remote

        @pl.when((s == 0) & (t == 0))
        def _():
            for j in range(n_t):
                ht_copy(j).start()
            local, remote = side_copies()
            for cp in local + remote:
                cp.start()

        for j in range(n_t):
            @pl.when((s == 0) & (t == j))
            def _(j=j):
                ht_copy(j).wait()

        @pl.when(t == 0)
        def _():
            acc[...] = jnp.zeros_like(acc)

        acc[...] += jnp.dot(ht_vmem[t], dp_ref[0], preferred_element_type=F32)

        for k in range(N_DEV):
            @pl.when((s == k) & (t == n_t - 1))
            def _(k=k):
                if k >= 2:
                    in_copy(k - 2).wait_send()
                sbuf[k % 2] = acc[...].astype(BF16)
                if k < N_DEV - 1:
                    in_copy(k).start()
                else:
                    own = pltpu.make_async_copy(sbuf.at[k % 2], rin_ref.at[me], local_sems.at[2])
                    own.start()
                    in_copy(k - 1).wait_send()
                    for j in range(N_DEV - 1):
                        in_landing(j).wait_recv()
                    local, remote = side_copies()
                    for cp in remote:
                        cp.wait_recv()
                    for cp in remote:
                        cp.wait_send()
                    for cp in local:
                        cp.wait()
                    own.wait()

    hbm = pl.BlockSpec(memory_space=pl.ANY)
    grid_spec = pltpu.PrefetchScalarGridSpec(
        num_scalar_prefetch=1, grid=(N_DEV, n_t),
        in_specs=[hbm,
                  pl.BlockSpec((1, tk, CONV_W), lambda s, t, order_ref: (order_ref[s], t, 0)),
                  hbm, hbm],
        out_specs=(hbm, hbm, hbm),
        scratch_shapes=[pltpu.VMEM((D, CONV_W), F32), pltpu.VMEM((2, D, CONV_W), BF16),
                        pltpu.VMEM((n_t, D, tk), BF16),
                        pltpu.SemaphoreType.DMA((N_DEV - 1,)), pltpu.SemaphoreType.DMA((N_DEV - 1,)),
                        pltpu.SemaphoreType.DMA((14,)), pltpu.SemaphoreType.DMA((14,)),
                        pltpu.SemaphoreType.DMA((3,)), pltpu.SemaphoreType.DMA((n_t,))])
    return pl.pallas_call(
        body, name="dw_exchange", grid_spec=grid_spec,
        out_shape=(jax.ShapeDtypeStruct((N_DEV, D, CONV_W), BF16), jax.ShapeDtypeStruct(gw_out.shape, F32),
                   jax.ShapeDtypeStruct((N_DEV,) + small.shape, F32)),
        compiler_params=_params(("arbitrary", "arbitrary"), 52),
    )(order, ht, dproj, gw_out, small)


def _adamw_math(w, g, m, v):
    m2 = ADAM_B1 * m + (1.0 - ADAM_B1) * g
    v2 = ADAM_B2 * v + (1.0 - ADAM_B2) * (g * g)
    m_hat = m2 / (1.0 - ADAM_B1 ** ADAM_STEP)
    v_hat = v2 / (1.0 - ADAM_B2 ** ADAM_STEP)
    delta = -ADAM_LR * (m_hat / (jnp.sqrt(v_hat) + ADAM_EPS) + ADAM_WD * w)
    return delta, m2, v2


def _adamw_sharded(name, parts, w, m, v, rows):
    R, C = w.shape

    def body(p_ref, w_ref, m_ref, v_ref, g_ref, d_ref, m2_ref, v2_ref):
        g = p_ref[0].astype(F32)
        for s in range(1, N_DEV):
            g = g + p_ref[s].astype(F32)
        g_ref[...] = g
        d_ref[...], m2_ref[...], v2_ref[...] = _adamw_math(w_ref[...], g, m_ref[...], v_ref[...])

    spec = pl.BlockSpec((rows, C), lambda i: (i, 0))
    out = jax.ShapeDtypeStruct((R, C), F32)
    return pl.pallas_call(
        body, name=name, grid=(R // rows,),
        out_shape=(out,) * 4,
        in_specs=[pl.BlockSpec((N_DEV, rows, C), lambda i: (0, i, 0)), spec, spec, spec],
        out_specs=(spec,) * 4,
        compiler_params=_params(("arbitrary",), 40),
    )(parts, w, m, v)


def _adamw_small(parts, w, m, v):
    R = w.shape[0]

    def body(p_ref, w_ref, m_ref, v_ref, g_ref, d_ref, m2_ref, v2_ref):
        g = p_ref[0]
        for s in range(1, N_DEV):
            g = g + p_ref[s]
        g_ref[...] = g
        d_ref[...], m2_ref[...], v2_ref[...] = _adamw_math(w_ref[...], g, m_ref[...], v_ref[...])

    vm = pl.BlockSpec(memory_space=pltpu.VMEM)
    out = jax.ShapeDtypeStruct((R, LANES), F32)
    return pl.pallas_call(
        body, name="adamw_small", out_shape=(out,) * 4, in_specs=[vm] * 4, out_specs=(vm,) * 4,
    )(parts, w, m, v)


def _qk_gain_grads(a_parts, q_norm_w, k_norm_w, bias_g):
    def body(a_ref, qw_ref, kw_ref, bg_ref, out_ref):
        tot = jnp.zeros((8, LANES), F32)
        for g in range(len(DILATIONS)):
            for p in range(N_PAIRS):
                tot = tot + a_ref[g, p]
        col = jnp.sum(tot, axis=0, keepdims=True)
        a64 = col[:, 0:HEAD_DIM] + col[:, HEAD_DIM:LANES]
        out_ref[...] = jnp.zeros_like(out_ref)
        out_ref[0:1, 0:HEAD_DIM] = LOGIT_SCALE * kw_ref[...] * a64
        out_ref[1:2, 0:HEAD_DIM] = LOGIT_SCALE * qw_ref[...] * a64
        for h in range(N_HEADS):
            r0, c0 = 2 + h // 4, (h % 4) * N_BUCKETS
            out_ref[r0:r0 + 1, c0:c0 + N_BUCKETS] = bg_ref[h // 2, h % 2:h % 2 + 1, 0:N_BUCKETS]

    vm = pl.BlockSpec(memory_space=pltpu.VMEM)
    return pl.pallas_call(
        body, name="qk_gain_grads", out_shape=jax.ShapeDtypeStruct((8, LANES), F32),
        in_specs=[vm] * 4, out_specs=vm,
    )(a_parts, q_norm_w.reshape(1, HEAD_DIM), k_norm_w.reshape(1, HEAD_DIM), bias_g)


SMALL_ROWS = 32


def _pack_small(norm_w, conv_b, q_norm_w, k_norm_w, rel_bias, conv_w_full):
    pad64 = lambda a: jnp.pad(a, (0, LANES - HEAD_DIM)).reshape(1, LANES)
    return jnp.concatenate([
        norm_w.reshape(8, LANES), conv_b.reshape(4, LANES), pad64(q_norm_w), pad64(k_norm_w),
        rel_bias.T.reshape(2, LANES), conv_w_full.reshape(12, LANES), jnp.zeros((4, LANES), F32)], axis=0)


def _unpack_small(s):
    return (s[0:8].reshape(D_MODEL), s[8:12].reshape(CONV_W), s[12, 0:HEAD_DIM], s[13, 0:HEAD_DIM],
            s[14:16].reshape(N_HEADS, N_BUCKETS).T, s[16:28].reshape(3, CONV_W))


def kernel(x, norm_w, w_in, conv_w, conv_b, q_norm_w, k_norm_w, rel_bias, w_out, loss_target, m_norm_w, m_w_in, m_conv_w, m_conv_b, m_q_norm_w, m_k_norm_w, m_rel_bias, m_w_out, v_norm_w, v_w_in, v_conv_w, v_conv_b, v_q_norm_w, v_k_norm_w, v_rel_bias, v_w_out):
    S = x.shape[1]
    x2 = x.reshape(S, D_MODEL)
    tgt = loss_target.reshape(S, D_MODEL)
    me = 4 * lax.axis_index("x") + 2 * lax.axis_index("y") + lax.axis_index("c")
    shard_w = CONV_W // N_DEV

    cw_pad = jnp.pad(conv_w, ((0, 5), (0, LANES - shard_w)))
    wg, wo_g, cw_g = _ag_weights(w_in, w_out, cw_pad)
    wo = wo_g.reshape(D_MODEL, D_MODEL)
    conv_w_full = cw_g[:, 0:3, 0:shard_w].transpose(1, 0, 2).reshape(3, CONV_W)

    bmat_t = jnp.asarray(np.stack([_bucket_matrix(d).T for d in DILATIONS]))
    tables = _bias_tables(rel_bias, bmat_t)
    gain = jnp.tile(q_norm_w * k_norm_w * LOGIT_SCALE, N_HEADS).reshape(1, ATTN_W)

    u, gb, gc, zc, za, q_hat, k_hat, v, rq, rk = _fwd_proj(x2, norm_w, wg)
    o_parts, lse_parts = [], []
    for g, dil in enumerate(DILATIONS):
        o_g, lse_g = _attn_fwd(g, dil, q_hat[g], k_hat[g], v[g], gain, tables)
        o_parts.append(o_g)
        lse_parts.append(lse_g)
    y, o, lse = _combine_gates(o_parts, lse_parts, u, gb, gc, zc, za, conv_w_full, conv_b)
    d_out, dy, gwo_part, loss_part = _out_proj(y, x2, tgt, wo)
    loss_mine = jnp.sum(loss_part) * (0.5 / D_MODEL)

    (du, dgb, dgc, dzc, dza), conv_small, d_o, delta = _gates_bwd(dy, u, gb, gc, zc, za, o, conv_w_full, conv_b)
    dqs, dks, dvs, ds_parts, a_parts = [], [], [], [], []
    for g, dil in enumerate(DILATIONS):
        dq, dk, dv, ds_sum, a_sum = _attn_bwd(g, dil, q_hat[g], k_hat[g], v[g], d_o[g], lse[g], delta[g], rq[g], rk[g],
                                              gain, tables)
        dqs.append((dq, dil))
        dks.append((dk, dil))
        dvs.append((dv, dil))
        ds_parts.append(ds_sum)
        a_parts.append(a_sum)
    grad_x, gnw_part, h_t, dproj = _proj_bwd(
        x2, d_out, norm_w, wg, [[(du, 1)], [(dgb, 1)], [(dgc, 1)], [(dzc, 1)], dqs, dks, dvs, [(dza, 1)]])
    bias_g = _bias_grad(jnp.stack(ds_parts), bmat_t)
    qk_small = _qk_gain_grads(jnp.stack(a_parts), q_norm_w, k_norm_w, bias_g)

    small_part = jnp.concatenate([
        gnw_part.sum(axis=0).reshape(8, LANES), conv_small[3].reshape(4, LANES), qk_small[0:4],
        conv_small[0:3].reshape(12, LANES), jnp.zeros((4, LANES), F32).at[0, 0].set(loss_mine)], axis=0)
    r_in, r_out, r_small = _dw_exchange(h_t, dproj, gwo_part.reshape(N_DEV, D_MODEL // N_DEV, D_MODEL), small_part)

    g_win, d_win, m_win, v_win = _adamw_sharded("adamw_w_in", r_in, w_in, m_w_in, v_w_in, 128)
    g_wo, d_wo, m_wo, v_wo = _adamw_sharded("adamw_w_out", r_out, w_out, m_w_out, v_w_out, 32)

    def full_conv(a):
        return jnp.zeros((3, CONV_W), F32).at[:, 0:shard_w].set(a)

    packs = [_pack_small(nw_, cb_, qw_, kw_, rb_, full_conv(cw_)) for nw_, cb_, qw_, kw_, rb_, cw_ in (
        (norm_w, conv_b, q_norm_w, k_norm_w, rel_bias, conv_w),
        (m_norm_w, m_conv_b, m_q_norm_w, m_k_norm_w, m_rel_bias, m_conv_w),
        (v_norm_w, v_conv_b, v_q_norm_w, v_k_norm_w, v_rel_bias, v_conv_w))]
    r_small_mine = r_small.at[:, 16:28, :].set(
        jnp.pad(lax.dynamic_slice_in_dim(r_small[:, 16:28, :].reshape(N_DEV, 3, CONV_W), me * shard_w, shard_w, axis=2),
                ((0, 0), (0, 0), (0, CONV_W - shard_w))).reshape(N_DEV, 12, LANES))
    outs_small = _adamw_small(r_small_mine, *packs)
    g_s, d_s, m_s, v_s = [_unpack_small(t) for t in outs_small]
    loss = outs_small[0][28, 0]

    def leaves(small, big_in, big_out):
        nw_, cb_, qw_, kw_, rb_, cwf = small
        return (nw_, big_in, cwf[:, 0:shard_w], cb_, qw_, kw_, rb_, big_out)

    return (loss, grad_x.reshape(x.shape),
            *leaves(g_s, g_win, g_wo), *leaves(d_s, d_win, d_wo), *leaves(m_s, m_win, m_wo), *leaves(v_s, v_win, v_wo))
```

```python
import functools
import math

import numpy as np
import jax
import jax.numpy as jnp
from jax import lax
from jax.experimental import pallas as pl
from jax.experimental.pallas import tpu as pltpu

F32 = jnp.float32
BF16 = jnp.bfloat16

N_DEV = 8
D_MODEL = 1024
CONV_W = 512
ATTN_W = 512
N_HEADS = 8
HEAD_DIM = 64
N_PAIRS = N_HEADS // 2
LANES = 128
HALF = 64
QB = 128
KB = QB + 2 * HALF
DILATIONS = (1, 4, 16)
N_BUCKETS = 32
MAX_DISTANCE = 1024
EPS = 1e-6
NEG = -1e30
LOGIT_SCALE = HEAD_DIM ** -0.5

ADAM_LR = 0.001
ADAM_B1 = 0.9
ADAM_B2 = 0.999
ADAM_EPS = 1e-08
ADAM_WD = 0.01
ADAM_STEP = 10

MESH = pl.DeviceIdType.MESH
MIB = 1024 * 1024


def _params(semantics, vmem_mib):
    return pltpu.CompilerParams(dimension_semantics=semantics, vmem_limit_bytes=vmem_mib * MIB)


def _lane(shape):
    return lax.broadcasted_iota(jnp.int32, shape, len(shape) - 1)


def _sigmoid(z):
    return 1.0 / (1.0 + jnp.exp(-z))


def _split_dot(x, w):
    hi = x.astype(BF16)
    lo = (x - hi.astype(F32)).astype(BF16)
    return jnp.dot(hi, w, preferred_element_type=F32) + jnp.dot(lo, w, preferred_element_type=F32)


def _same_head():
    r = lax.broadcasted_iota(jnp.int32, (LANES, LANES), 0) // HEAD_DIM
    c = lax.broadcasted_iota(jnp.int32, (LANES, LANES), 1) // HEAD_DIM
    return (r == c).astype(BF16)


def _head_to_lanes(p):
    r = lax.broadcasted_iota(jnp.int32, (LANES, LANES), 0)
    c = lax.broadcasted_iota(jnp.int32, (LANES, LANES), 1) // HEAD_DIM
    return (r == 2 * p + c).astype(BF16)


def _lanes_to_head(p, scale):
    r = lax.broadcasted_iota(jnp.int32, (LANES, LANES), 0) // HEAD_DIM
    c = lax.broadcasted_iota(jnp.int32, (LANES, LANES), 1)
    return jnp.where(c == 2 * p + r, scale, 0.0).astype(BF16)


VIEW_DILATIONS = DILATIONS[1:]


def _view_shape(S, dil, width, dtype):
    return jax.ShapeDtypeStruct((S // dil, dil * width), dtype)


def _view_spec(tm, dil, width):
    return pl.BlockSpec((tm // dil, dil * width), lambda i: (i, 0))


def _tile_scratch(tm, width):
    return pltpu.VMEM((width // LANES, tm, LANES), F32)


def _scratch_value(scr_ref):
    n = scr_ref.shape[0]
    return scr_ref[0] if n == 1 else jnp.concatenate([scr_ref[c] for c in range(n)], axis=1)


def _to_view(scr_ref, out_ref, dil, dtype):
    n, tm, _ = scr_ref.shape
    for r in range(dil):
        for c in range(n):
            col = (r * n + c) * LANES
            out_ref[:, col:col + LANES] = scr_ref[c, pl.ds(r, tm // dil, stride=dil), :].astype(dtype)


def _from_view(blk_ref, scr_ref, dil):
    n, tm, _ = scr_ref.shape
    for r in range(dil):
        for c in range(n):
            col = (r * n + c) * LANES
            scr_ref[c, pl.ds(r, tm // dil, stride=dil), :] = blk_ref[:, col:col + LANES].astype(F32)


def _bucket_matrix(dilation):
    rel = np.arange(KB)[None, :] - HALF - np.arange(QB)[:, None]
    band = np.abs(rel) <= HALF
    dist = np.clip(rel, -HALF, HALF) * dilation
    half_b = N_BUCKETS // 2
    max_exact = half_b // 2
    ret = np.where(dist > 0, half_b, 0)
    n = np.abs(dist)
    nf = np.maximum(n, 1).astype(np.float32)
    large = max_exact + (np.log(nf / np.float32(max_exact)) / np.float32(math.log(MAX_DISTANCE / max_exact))
                         * np.float32(half_b - max_exact)).astype(np.int32)
    large = np.minimum(large, half_b - 1)
    bucket = ret + np.where(n < max_exact, n, large)
    return np.where(band, bucket, -1).astype(np.int32)


def _ag_weights(w_in, w_out, cw_pad):
    n_arr = 3

    def body(win_ref, wout_ref, cw_ref, gin_ref, gout_ref, gcw_ref, send_sems, recv_sems):
        x, y, c = lax.axis_index("x"), lax.axis_index("y"), lax.axis_index("c")
        me = (x, y, c)
        sibling = (x, y, 1 - c)
        chips = [(1 - x, y), (x, 1 - y), (1 - x, 1 - y)]
        arrays = (gin_ref, gout_ref, gcw_ref)

        def slot(px, py, pc):
            return 4 * px + 2 * py + pc

        gin_ref[slot(*me)] = win_ref[...].astype(BF16)
        gout_ref[slot(*me)] = wout_ref[...].astype(BF16)
        gcw_ref[slot(*me)] = cw_ref[...]

        def copy(a, k, block, to):
            ref = arrays[a].at[slot(*block)]
            return pltpu.make_async_remote_copy(
                src_ref=ref, dst_ref=ref, send_sem=send_sems.at[a * 7 + k], recv_sem=recv_sems.at[a * 7 + k],
                device_id=to, device_id_type=MESH)

        first = [copy(a, 0, me, sibling) for a in range(n_arr)]
        for j, chip in enumerate(chips):
            first += [copy(a, 1 + j, me, (*chip, c)) for a in range(n_arr)]
        for cp in first:
            cp.start()
        passed = []
        for j, chip in enumerate(chips):
            for a in range(n_arr):
                copy(a, 1 + j, (*chip, c), me).wait_recv()
            for a in range(n_arr):
                cp = copy(a, 4 + j, (*chip, c), sibling)
                cp.start()
                passed.append(cp)
        for a in range(n_arr):
            copy(a, 0, sibling, me).wait_recv()
        for j, chip in enumerate(chips):
            for a in range(n_arr):
                copy(a, 4 + j, (*chip, 1 - c), me).wait_recv()
        for cp in first + passed:
            cp.wait_send()

    vm = pl.BlockSpec(memory_space=pltpu.VMEM)
    return pl.pallas_call(
        body, name="ag_weights",
        out_shape=(jax.ShapeDtypeStruct((N_DEV,) + w_in.shape, BF16),
                   jax.ShapeDtypeStruct((N_DEV,) + w_out.shape, BF16),
                   jax.ShapeDtypeStruct((N_DEV,) + cw_pad.shape, F32)),
        in_specs=[vm, vm, vm], out_specs=(vm, vm, vm),
        scratch_shapes=[pltpu.SemaphoreType.DMA((n_arr * 7,)), pltpu.SemaphoreType.DMA((n_arr * 7,))],
        compiler_params=pltpu.CompilerParams(vmem_limit_bytes=40 * MIB),
    )(w_in, w_out, cw_pad)


def _fwd_proj(x, norm_w, wg):
    S = x.shape[0]
    tm = 512

    def body(x_ref, nw_ref, wg_ref, u_ref, gb_ref, gc_ref, zc_ref, za_ref, *rest):
        q_refs, k_refs, v_refs, rq_refs, rk_refs = (rest[3 * n:3 * n + 3] for n in range(5))
        scr, rscr = rest[15:]
        xv = x_ref[...]
        r = lax.rsqrt(jnp.mean(xv * xv, axis=-1, keepdims=True) + EPS)
        h = (xv * r * nw_ref[...]).astype(BF16)
        lane = _lane((tm, LANES))
        lo = lane < HEAD_DIM
        plain = {0: u_ref, 1: gb_ref, 2: gc_ref, 3: zc_ref, 7: za_ref}
        normed = {4: (q_refs, rq_refs), 5: (k_refs, rk_refs)}

        def emit(src, refs, dtype):
            refs[0][...] = _scratch_value(src).astype(dtype)
            for dil, ref in zip(VIEW_DILATIONS, refs[1:]):
                _to_view(src, ref, dil, dtype)

        acc_next = jnp.dot(h, wg_ref[0], preferred_element_type=F32)
        for j in range(8):
            acc = acc_next
            if j < 7:
                acc_next = jnp.dot(h, wg_ref[j + 1], preferred_element_type=F32)
            if j in plain:
                plain[j][...] = acc.astype(BF16)
                continue
            if j == 6:
                for p in range(N_PAIRS):
                    scr[p] = acc[:, p * LANES:(p + 1) * LANES]
                emit(scr, v_refs, BF16)
                continue
            out_refs, r_refs = normed[j]
            r_tile = jnp.zeros((tm, LANES), F32)
            for p in range(N_PAIRS):
                blk = acc[:, p * LANES:(p + 1) * LANES]
                sq = blk * blk
                s_lo = jnp.sum(jnp.where(lo, sq, 0.0), axis=-1, keepdims=True)
                s_hi = jnp.sum(jnp.where(lo, 0.0, sq), axis=-1, keepdims=True)
                r_lo = lax.rsqrt(s_lo * (1.0 / HEAD_DIM) + EPS)
                r_hi = lax.rsqrt(s_hi * (1.0 / HEAD_DIM) + EPS)
                scr[p] = blk * jnp.where(lo, r_lo, r_hi)
                r_tile = jnp.where(lane == 2 * p, r_lo, r_tile)
                r_tile = jnp.where(lane == 2 * p + 1, r_hi, r_tile)
            rscr[0] = r_tile
            emit(scr, out_refs, BF16)
            emit(rscr, r_refs, F32)

    piece = jax.ShapeDtypeStruct((S, CONV_W), BF16)
    pspec = pl.BlockSpec((tm, CONV_W), lambda i: (i, 0))
    wide = [_view_shape(S, d, ATTN_W, BF16) for d in DILATIONS]
    wide_specs = [_view_spec(tm, d, ATTN_W) for d in DILATIONS]
    stat = [_view_shape(S, d, LANES, F32) for d in DILATIONS]
    stat_specs = [_view_spec(tm, d, LANES) for d in DILATIONS]
    outs = pl.pallas_call(
        body, name="fwd_proj", grid=(S // tm,),
        out_shape=[piece] * 5 + wide * 3 + stat * 2,
        in_specs=[pl.BlockSpec((tm, D_MODEL), lambda i: (i, 0)),
                  pl.BlockSpec((1, D_MODEL), lambda i: (0, 0)),
                  pl.BlockSpec(wg.shape, lambda i: (0, 0, 0), pipeline_mode=pl.Buffered(1))],
        out_specs=[pspec] * 5 + wide_specs * 3 + stat_specs * 2,
        scratch_shapes=[_tile_scratch(tm, ATTN_W), _tile_scratch(tm, LANES)],
        compiler_params=_params(("arbitrary",), 48),
    )(x, norm_w.reshape(1, D_MODEL), wg)
    u, gb, gc, zc, za = outs[:5]
    q_hat, k_hat, v, rq, rk = (outs[5 + 3 * n:8 + 3 * n] for n in range(5))
    return u, gb, gc, zc, za, q_hat, k_hat, v, rq, rk


def _bias_tables(rel_bias, bmat_t):
    def body(rb_ref, b_ref, out_ref):
        h = pl.program_id(1)
        b = b_ref[0]
        t = jnp.full((KB, QB), NEG, F32)
        for bk in range(N_BUCKETS):
            t = jnp.where(b == bk, rb_ref[bk, h], t)
        out_ref[0, 0] = t

    return pl.pallas_call(
        body, name="bias_tables", grid=(len(DILATIONS), N_HEADS),
        out_shape=jax.ShapeDtypeStruct((len(DILATIONS), N_HEADS, KB, QB), F32),
        in_specs=[pl.BlockSpec(memory_space=pltpu.SMEM),
                  pl.BlockSpec((1, KB, QB), lambda g, h: (g, 0, 0))],
        out_specs=pl.BlockSpec((1, 1, KB, QB), lambda g, h: (g, h, 0, 0)),
        compiler_params=_params(("arbitrary", "arbitrary"), 16),
    )(rel_bias, bmat_t)


def _bias_grad(ds_acc, bmat_t):
    present = [sorted(set(_bucket_matrix(d).ravel().tolist()) - {-1}) for d in DILATIONS]

    def body(ds_ref, b_ref, out_ref):
        lane = _lane((1, LANES))
        row = lax.broadcasted_iota(jnp.int32, (8, LANES), 0)
        out = jnp.zeros((8, LANES), F32)
        for e in range(2):
            vec = jnp.zeros((1, LANES), F32)
            for g in range(len(DILATIONS)):
                b = b_ref[g]
                ds = ds_ref[g, 0, :, e * QB:(e + 1) * QB]
                for bk in present[g]:
                    s = jnp.sum(jnp.where(b == bk, ds, 0.0), axis=-1, keepdims=True)
                    s = jnp.sum(s, axis=0, keepdims=True)
                    vec = vec + jnp.where(lane == bk, s, 0.0)
            out = jnp.where(row == e, vec, out)
        out_ref[0] = out

    return pl.pallas_call(
        body, name="bias_grad", grid=(N_PAIRS,),
        out_shape=jax.ShapeDtypeStruct((N_PAIRS, 8, LANES), F32),
        in_specs=[pl.BlockSpec((len(DILATIONS), 1, KB, 2 * QB), lambda p: (0, p, 0, 0)),
                  pl.BlockSpec((len(DILATIONS), KB, QB), lambda p: (0, 0, 0))],
        out_specs=pl.BlockSpec((1, 8, LANES), lambda p: (p, 0, 0)),
        compiler_params=_params(("arbitrary",), 16),
    )(ds_acc, bmat_t)


def _halo_specs(width, rows, T, L, cols_of):
    per = T // rows
    last = L // rows - 1
    prev = pl.BlockSpec((rows, width), lambda r, i: (jnp.maximum(i * per - 1, 0), cols_of(r)))
    nxt = pl.BlockSpec((rows, width), lambda r, i: (jnp.minimum((i + 1) * per, last), cols_of(r)))
    return prev, nxt


def _attn_fwd(g, dil, qv, kv, vv, gain, tables):
    L = qv.shape[0]
    T = min(2048, L)
    n_sub = T // QB

    def body(q_ref, km_ref, kp_ref, kn_ref, vm_ref, vp_ref, vn_ref, gain_ref, bias_ref, o_ref, lse_ref,
             kwin, vwin, s_scr, p_scr):
        i = pl.program_id(1)
        kwin[0:HALF] = kp_ref[...]
        kwin[HALF:HALF + T] = km_ref[...]
        kwin[HALF + T:] = kn_ref[...]
        vwin[0:HALF] = vp_ref[...]
        vwin[HALF:HALF + T] = vm_ref[...]
        vwin[HALF + T:] = vn_ref[...]
        lo = _lane((QB, LANES)) < HEAD_DIM
        krow = lax.broadcasted_iota(jnp.int32, (KB, 1), 0)
        chan_lo = lax.broadcasted_iota(jnp.int32, (LANES, 1), 0) < HEAD_DIM
        hrow = lax.broadcasted_iota(jnp.int32, (LANES, QB), 0)

        def sub(j, carry):
            r0 = pl.multiple_of(j * QB, QB)
            kpos = i * T + j * QB - HALF + krow
            kvalid = (kpos >= 0) & (kpos < L)
            lse_rows = jnp.zeros((LANES, QB), F32)
            cols = [slice(p * LANES, (p + 1) * LANES) for p in range(N_PAIRS)]
            for p, cs in enumerate(cols):
                qf = q_ref[pl.ds(r0, QB), cs].astype(F32) * gain_ref[:, cs]
                q2 = jnp.concatenate([jnp.where(lo, qf, 0.0), jnp.where(lo, 0.0, qf)], axis=0).astype(BF16)
                s_scr[p] = lax.dot_general(kwin[pl.ds(r0, KB), cs], q2, (((1,), (1,)), ((), ())),
                                           preferred_element_type=F32)
            inv_l = []
            for p in range(N_PAIRS):
                bias2 = jnp.concatenate([bias_ref[0, 2 * p], bias_ref[0, 2 * p + 1]], axis=1)
                logits = jnp.where(kvalid, s_scr[p] + bias2, NEG)
                m = jnp.max(logits, axis=0, keepdims=True)
                pt = jnp.exp(logits - m)
                l = jnp.sum(pt, axis=0, keepdims=True)
                p_scr[p] = pt.astype(BF16)
                inv_l.append(1.0 / l)
                lse2 = m + jnp.log(l)
                lse_rows = jnp.where(hrow == 2 * p, lse2[:, 0:QB], lse_rows)
                lse_rows = jnp.where(hrow == 2 * p + 1, lse2[:, QB:2 * QB], lse_rows)
            for p, cs in enumerate(cols):
                s_scr[p, 0:LANES, :] = lax.dot_general(vwin[pl.ds(r0, KB), cs], p_scr[p], (((0,), (0,)), ((), ())),
                                                       preferred_element_type=F32) * inv_l[p]
            for p, cs in enumerate(cols):
                ot2 = s_scr[p, 0:LANES, :]
                ot = jnp.where(chan_lo, ot2[:, 0:QB], ot2[:, QB:2 * QB])
                o_ref[pl.ds(r0, QB), cs] = ot.T.astype(BF16)
            lse_ref[pl.ds(r0, QB), :] = lse_rows.T
            return carry

        lax.fori_loop(0, n_sub, sub, 0)

    main = pl.BlockSpec((T, ATTN_W), lambda r, i: (i, r))
    prev, nxt = _halo_specs(ATTN_W, HALF, T, L, lambda r: r)
    o_g, lse_g = pl.pallas_call(
        body, name=f"attn_fwd_d{dil}", grid=(dil, L // T),
        out_shape=(jax.ShapeDtypeStruct((L, dil * ATTN_W), BF16), jax.ShapeDtypeStruct((L, dil * LANES), F32)),
        in_specs=[main, main, prev, nxt, main, prev, nxt,
                  pl.BlockSpec((1, ATTN_W), lambda r, i: (0, 0)),
                  pl.BlockSpec((1, N_HEADS, KB, QB), lambda r, i: (g, 0, 0, 0))],
        out_specs=(main, pl.BlockSpec((T, LANES), lambda r, i: (i, r))),
        scratch_shapes=[pltpu.VMEM((T + 2 * HALF, ATTN_W), BF16), pltpu.VMEM((T + 2 * HALF, ATTN_W), BF16),
                        pltpu.VMEM((N_PAIRS, KB, 2 * QB), F32), pltpu.VMEM((N_PAIRS, KB, 2 * QB), BF16)],
        compiler_params=_params(("arbitrary", "arbitrary"), 40),
    )(qv, kv, kv, kv, vv, vv, vv, gain, tables)
    return o_g, lse_g


HALO = 16
CHUNK = 128


def _row_halo_specs(width, tm, S, col=0):
    rows = HALO
    per = tm // rows
    last = S // rows - 1
    prev = pl.BlockSpec((rows, width), lambda i: (jnp.maximum(i * per - 1, 0), col))
    nxt = pl.BlockSpec((rows, width), lambda i: (jnp.minimum((i + 1) * per, last), col))
    return prev, nxt


def _shift_rows(a, prev_row, next_row, tm):
    row = lax.broadcasted_iota(jnp.int32, a.shape, 0)
    a_m1 = jnp.where(row == 0, prev_row, pltpu.roll(a, 1, 0))
    a_p1 = jnp.where(row == tm - 1, next_row, pltpu.roll(a, tm - 1, 0))
    return a_m1, a_p1


def _combine_gates(o_parts, lse_parts, u, gb, gc, zc, za, conv_w, conv_b):
    S = u.shape[0]
    tm = 512
    n_t = S // tm

    def body(o1, o2, o3, l1, l2, l3, u_ref, up_ref, un_ref, gb_ref, gc_ref, gcp_ref, gcn_ref, zc_ref, za_ref,
             cw_ref, cb_ref, y_ref, o_ref, lse_ref, lse2_ref, lse3_ref, so2, so3, sl2, sl3, slse):
        i = pl.program_id(0)
        for blk, scr, dil in ((o2, so2, DILATIONS[1]), (o3, so3, DILATIONS[2]),
                              (l2, sl2, DILATIONS[1]), (l3, sl3, DILATIONS[2])):
            _from_view(blk, scr, dil)
        ls = [l1[...], sl2[0], sl3[0]]
        lmax = jnp.maximum(jnp.maximum(ls[0], ls[1]), ls[2])
        es = [jnp.exp(l - lmax) for l in ls]
        den = es[0] + es[1] + es[2]
        slse[0] = lmax + jnp.log(den)
        lse_ref[...] = slse[0]
        _to_view(slse, lse2_ref, DILATIONS[1], F32)
        _to_view(slse, lse3_ref, DILATIONS[2], F32)
        inv = 1.0 / den
        ws = [e * inv for e in es]
        for p in range(N_PAIRS):
            cs = slice(p * LANES, (p + 1) * LANES)
            acc = jnp.zeros((tm, LANES), F32)
            spread = _head_to_lanes(p)
            for w, o_g in zip(ws, (o1[:, cs].astype(F32), so2[p], so3[p])):
                acc = acc + _split_dot(w, spread) * o_g
            o_ref[:, cs] = acc.astype(BF16)
            za = za_ref[:, cs].astype(F32)
            y_ref[:, CONV_W + p * LANES:CONV_W + (p + 1) * LANES] = (acc * za * _sigmoid(za)).astype(BF16)
        a = gc_ref[...].astype(F32) * u_ref[...].astype(F32)
        a_prev = gcp_ref[15:16, :].astype(F32) * up_ref[15:16, :].astype(F32) * (i > 0).astype(F32)
        a_next = gcn_ref[0:1, :].astype(F32) * un_ref[0:1, :].astype(F32) * (i < n_t - 1).astype(F32)
        a_m1, a_p1 = _shift_rows(a, a_prev, a_next, tm)
        z3 = cw_ref[0:1, :] * a_m1 + cw_ref[1:2, :] * a + cw_ref[2:3, :] * a_p1
        zc = zc_ref[...].astype(F32)
        y_ref[:, 0:CONV_W] = (gb_ref[...].astype(F32) * (z3 + cb_ref[...]) * zc * _sigmoid(zc)).astype(BF16)

    pspec = pl.BlockSpec((tm, CONV_W), lambda i: (i, 0))
    hp, hn = _row_halo_specs(CONV_W, tm, S)
    wide_specs = [_view_spec(tm, d, ATTN_W) for d in DILATIONS]
    stat_specs = [_view_spec(tm, d, LANES) for d in DILATIONS]
    outs = pl.pallas_call(
        body, name="combine_gates", grid=(n_t,),
        out_shape=[jax.ShapeDtypeStruct((S, D_MODEL), BF16), jax.ShapeDtypeStruct((S, ATTN_W), BF16)]
        + [_view_shape(S, d, LANES, F32) for d in DILATIONS],
        in_specs=wide_specs + stat_specs + [pspec, hp, hn, pspec, pspec, hp, hn, pspec, pspec,
                                            pl.BlockSpec((3, CONV_W), lambda i: (0, 0)),
                                            pl.BlockSpec((1, CONV_W), lambda i: (0, 0))],
        out_specs=[pl.BlockSpec((tm, D_MODEL), lambda i: (i, 0)), pspec] + stat_specs,
        scratch_shapes=[_tile_scratch(tm, ATTN_W), _tile_scratch(tm, ATTN_W),
                        _tile_scratch(tm, LANES), _tile_scratch(tm, LANES), _tile_scratch(tm, LANES)],
        compiler_params=_params(("arbitrary",), 40),
    )(*o_parts, *lse_parts, u, u, u, gb, gc, gc, gc, zc, za, conv_w, conv_b.reshape(1, CONV_W))
    return outs[0], outs[1], outs[2:]


def _out_proj(y, x, target, wo):
    S = x.shape[0]
    tm = 512

    def body(y_ref, x_ref, t_ref, wo_ref, g_ref, dy_ref, dwo_ref, loss_ref):
        i = pl.program_id(0)

        @pl.when(i == 0)
        def _():
            dwo_ref[...] = jnp.zeros_like(dwo_ref)
            loss_ref[...] = jnp.zeros_like(loss_ref)

        yv = y_ref[...]
        wo_v = wo_ref[...]
        err = x_ref[...] + jnp.dot(yv, wo_v, preferred_element_type=F32) - t_ref[...]
        e2 = (err * err).reshape(tm // 8, 8, D_MODEL).sum(axis=0)
        part = e2[:, 0:LANES]
        for k in range(1, D_MODEL // LANES):
            part = part + e2[:, k * LANES:(k + 1) * LANES]
        loss_ref[...] += part
        gv = (err * (1.0 / D_MODEL)).astype(BF16)
        g_ref[...] = gv
        dy_ref[...] = lax.dot_general(gv, wo_v, (((1,), (1,)), ((), ())), preferred_element_type=F32).astype(BF16)
        dwo_ref[...] += lax.dot_general(yv, gv, (((0,), (0,)), ((), ())), preferred_element_type=F32)

    tile = pl.BlockSpec((tm, D_MODEL), lambda i: (i, 0))
    return pl.pallas_call(
        body, name="out_proj", grid=(S // tm,),
        out_shape=(jax.ShapeDtypeStruct((S, D_MODEL), BF16), jax.ShapeDtypeStruct((S, D_MODEL), BF16),
                   jax.ShapeDtypeStruct((D_MODEL, D_MODEL), F32), jax.ShapeDtypeStruct((8, LANES), F32)),
        in_specs=[tile, tile, tile, pl.BlockSpec((D_MODEL, D_MODEL), lambda i: (0, 0), pipeline_mode=pl.Buffered(1))],
        out_specs=(tile, tile, pl.BlockSpec((D_MODEL, D_MODEL), lambda i: (0, 0)),
                   pl.BlockSpec((8, LANES), lambda i: (0, 0))),
        compiler_params=_params(("arbitrary",), 48),
    )(y, x, target, wo)


def _gates_bwd(dy, u, gb, gc, zc, za, o, conv_w, conv_b):
    S = u.shape[0]
    tm = 512
    n_t = S // tm

    def body(dy_ref, dyp_ref, dyn_ref, u_ref, up_ref, un_ref, gb_ref, gbp_ref, gbn_ref, gc_ref, gcp_ref, gcn_ref,
             zc_ref, zcp_ref, zcn_ref, za_ref, o_ref, cw_ref, cb_ref,
             du_ref, dgb_ref, dgc_ref, dzc_ref, dza_ref, small_ref,
             do_ref, do2_ref, do3_ref, delta_ref, delta2_ref, delta3_ref, sdo, sdelta,
             u_ext, gc_ext, gb_ext, zc_ext, dy_ext):
        i = pl.program_id(0)

        @pl.when(i == 0)
        def _():
            small_ref[...] = jnp.zeros_like(small_ref)

        exts = ((u_ext, up_ref, u_ref, un_ref), (gc_ext, gcp_ref, gc_ref, gcn_ref), (gb_ext, gbp_ref, gb_ref, gbn_ref),
                (zc_ext, zcp_ref, zc_ref, zcn_ref))
        for ext, pr, mn, nx in exts:
            ext[0:HALO] = pr[...]
            ext[HALO:HALO + tm] = mn[...]
            ext[HALO + tm:] = nx[...]
        dy_ext[0:HALO] = dyp_ref[...]
        dy_ext[HALO:HALO + tm] = dy_ref[:, 0:CONV_W]
        dy_ext[HALO + tm:] = dyn_ref[...]
        zero_halo = jnp.zeros((HALO, CONV_W), BF16)

        @pl.when(i == 0)
        def _():
            u_ext[0:HALO] = zero_halo
            dy_ext[0:HALO] = zero_halo

        @pl.when(i == n_t - 1)
        def _():
            u_ext[HALO + tm:] = zero_halo
            dy_ext[HALO + tm:] = zero_halo

        wide_rows = CHUNK + 2 * HALO
        mid = slice(HALO, HALO + CHUNK)
        csum = lambda t: jnp.sum(t, axis=0, keepdims=True)

        def one_group(p, c0, sums):
            cs = slice(p * LANES, (p + 1) * LANES)
            w0, w1, w2, cb = cw_ref[0:1, cs], cw_ref[1:2, cs], cw_ref[2:3, cs], cb_ref[:, cs]
            rows = pl.ds(c0, CHUNK)
            wide = pl.ds(c0, wide_rows)
            u_e, gc_e, gb_e, zc_e, dy_e = (r[wide, cs].astype(F32) for r in (u_ext, gc_ext, gb_ext, zc_ext, dy_ext))
            a_e = gc_e * u_e
            sz_e = _sigmoid(zc_e)
            dz3_e = dy_e * zc_e * sz_e * gb_e
            a, a_m1, a_p1 = a_e[mid], pltpu.roll(a_e, 1, 0)[mid], pltpu.roll(a_e, wide_rows - 1, 0)[mid]
            dz3, dz3_m1, dz3_p1 = dz3_e[mid], pltpu.roll(dz3_e, 1, 0)[mid], pltpu.roll(dz3_e, wide_rows - 1, 0)[mid]
            uv, gcv, gbv, zcv, dyc, sz = u_e[mid], gc_e[mid], gb_e[mid], zc_e[mid], dy_e[mid], sz_e[mid]
            cvp = w0 * a_m1 + w1 * a + w2 * a_p1 + cb
            dcv = dyc * zcv * sz
            dzc_ref[rows, cs] = (dyc * gbv * cvp * (sz * (1.0 + zcv * (1.0 - sz)))).astype(BF16)
            dgb_ref[rows, cs] = (dcv * cvp).astype(BF16)
            da = w0 * dz3_p1 + w1 * dz3 + w2 * dz3_m1
            du_ref[rows, cs] = (da * gcv).astype(BF16)
            dgc_ref[rows, cs] = (da * uv).astype(BF16)
            sums = (sums[0] + csum(dz3 * a_m1), sums[1] + csum(dz3 * a), sums[2] + csum(dz3 * a_p1),
                    sums[3] + csum(dz3))

            dya = dy_ref[rows, CONV_W + p * LANES:CONV_W + (p + 1) * LANES].astype(F32)
            zav = za_ref[rows, cs].astype(F32)
            ov = o_ref[rows, cs].astype(F32)
            sa = _sigmoid(zav)
            d_o = dya * zav * sa
            sdo[p, rows, :] = d_o
            do_ref[rows, cs] = d_o.astype(BF16)
            dza_ref[rows, cs] = (dya * ov * (sa * (1.0 + zav * (1.0 - sa)))).astype(BF16)
            return sums, _split_dot(d_o * ov, _lanes_to_head(p, 1.0))

        def chunk(ci, carry):
            c0 = pl.multiple_of(ci * CHUNK, CHUNK)
            out, delta = [], jnp.zeros((CHUNK, LANES), F32)
            for p in range(N_PAIRS):
                sums, part = one_group(p, c0, carry[p])
                out.append(sums)
                delta = delta + part
            sdelta[0, pl.ds(c0, CHUNK), :] = delta
            return tuple(out)

        z = jnp.zeros((1, LANES), F32)
        all_sums = lax.fori_loop(0, tm // CHUNK, chunk, ((z, z, z, z),) * N_PAIRS)
        for p in range(N_PAIRS):
            for k in range(4):
                small_ref[k:k + 1, p * LANES:(p + 1) * LANES] += all_sums[p][k]
        delta_ref[...] = sdelta[0]
        for dil, wide_ref, stat_ref in zip(VIEW_DILATIONS, (do2_ref, do3_ref), (delta2_ref, delta3_ref)):
            _to_view(sdo, wide_ref, dil, BF16)
            _to_view(sdelta, stat_ref, dil, F32)

    pspec = pl.BlockSpec((tm, CONV_W), lambda i: (i, 0))
    hp, hn = _row_halo_specs(CONV_W, tm, S)
    piece = jax.ShapeDtypeStruct((S, CONV_W), BF16)
    outs = pl.pallas_call(
        body, name="gates_bwd", grid=(n_t,),
        out_shape=[piece] * 5 + [jax.ShapeDtypeStruct((8, CONV_W), F32)]
        + [_view_shape(S, d, ATTN_W, BF16) for d in DILATIONS] + [_view_shape(S, d, LANES, F32) for d in DILATIONS],
        in_specs=[pl.BlockSpec((tm, D_MODEL), lambda i: (i, 0)), hp, hn,
                  pspec, hp, hn, pspec, hp, hn, pspec, hp, hn, pspec, hp, hn, pspec, pspec,
                  pl.BlockSpec((3, CONV_W), lambda i: (0, 0)), pl.BlockSpec((1, CONV_W), lambda i: (0, 0))],
        out_specs=[pspec] * 5 + [pl.BlockSpec((8, CONV_W), lambda i: (0, 0))]
        + [_view_spec(tm, d, ATTN_W) for d in DILATIONS] + [_view_spec(tm, d, LANES) for d in DILATIONS],
        scratch_shapes=[_tile_scratch(tm, ATTN_W), _tile_scratch(tm, LANES)]
        + [pltpu.VMEM((tm + 2 * HALO, CONV_W), BF16)] * 5,
        compiler_params=_params(("arbitrary",), 40),
    )(dy, dy, dy, u, u, u, gb, gb, gb, gc, gc, gc, zc, zc, zc, za, o, conv_w, conv_b.reshape(1, CONV_W))
    return outs[:5], outs[5], outs[6:9], outs[9:12]


def _attn_bwd(g, dil, qv, kv, vv, dov, lv, dv_, rqv, rkv, gain, tables):
    L = qv.shape[0]
    T = min(1024, L)
    n_sub = T // QB + 1
    QW = T + 2 * HALF
    KW = T + 2 * QB

    def body(qm_ref, qp_ref, qn_ref, dom_ref, dop_ref, don_ref, km_ref, kp_ref, kn_ref, vm_ref, vp_ref, vn_ref,
             lm_ref, lp_ref, ln_ref, dm_ref, dp_ref, dn_ref, rqm_ref, rqp_ref, rqn_ref, rk_ref,
             gain_ref, bias_ref,
             dq_ref, dk_ref, dv_ref, ds_ref, a_ref,
             qwin, dowin, kwin, vwin, lwin, dwin, rqwin, dqwin, dkacc, dvacc,
             s_scr, d_scr, p_scr, ds_scr, q2_scr, do2_scr, dq_scr):
        r = pl.program_id(0)
        i = pl.program_id(1)

        @pl.when((r == 0) & (i == 0))
        def _():
            ds_ref[...] = jnp.zeros_like(ds_ref)
            a_ref[...] = jnp.zeros_like(a_ref)

        for win, (pr, mn, nx), h_rows in ((qwin, (qp_ref, qm_ref, qn_ref), HALF),
                                          (dowin, (dop_ref, dom_ref, don_ref), HALF),
                                          (lwin, (lp_ref, lm_ref, ln_ref), HALF),
                                          (dwin, (dp_ref, dm_ref, dn_ref), HALF),
                                          (rqwin, (rqp_ref, rqm_ref, rqn_ref), HALF),
                                          (kwin, (kp_ref, km_ref, kn_ref), QB),
                                          (vwin, (vp_ref, vm_ref, vn_ref), QB)):
            win[0:h_rows] = pr[...]
            win[h_rows:h_rows + T] = mn[...]
            win[h_rows + T:] = nx[...]
        dkacc[0:QB] = jnp.zeros((QB, ATTN_W), F32)
        dvacc[0:QB] = jnp.zeros((QB, ATTN_W), F32)

        lo = _lane((QB, LANES)) < HEAD_DIM
        same_head = _same_head()
        krow = lax.broadcasted_iota(jnp.int32, (KB, 1), 0)
        qcol = _lane((1, 2 * QB)) % QB
        row = lax.broadcasted_iota(jnp.int32, (QB, 1), 0)
        chan_lo = lax.broadcasted_iota(jnp.int32, (LANES, 1), 0) < HEAD_DIM

        cols = [slice(p * LANES, (p + 1) * LANES) for p in range(N_PAIRS)]
        nt = (((1,), (1,)), ((), ()))

        def scores(j):
            r0 = pl.multiple_of(j * QB, QB)
            for k, cs in enumerate(cols):
                qf = qwin[pl.ds(r0, QB), cs].astype(F32) * gain_ref[:, cs]
                q2_scr[k] = jnp.concatenate([jnp.where(lo, qf, 0.0), jnp.where(lo, 0.0, qf)], axis=0).astype(BF16)
                dov = dowin[pl.ds(r0, QB), cs]
                zero = jnp.zeros_like(dov)
                do2_scr[k] = jnp.concatenate([jnp.where(lo, dov, zero), jnp.where(lo, zero, dov)], axis=0)
                s_scr[k] = lax.dot_general(kwin[pl.ds(r0, KB), cs], q2_scr[k], nt, preferred_element_type=F32)
                d_scr[k] = lax.dot_general(vwin[pl.ds(r0, KB), cs], do2_scr[k], nt, preferred_element_type=F32)

        def sub(j):
            r0 = pl.multiple_of(j * QB, QB)
            kpos = i * T - QB + j * QB + krow
            qpos = i * T - HALF + j * QB + qcol
            valid = (kpos >= 0) & (kpos < L) & (qpos >= 0) & (qpos < L)
            owned_t = ((qpos >= i * T) & (qpos < i * T + T)).astype(F32)
            qpos_c = i * T - HALF + j * QB + row
            owned = ((qpos_c >= i * T) & (qpos_c < i * T + T)).astype(F32)
            lse_rows = lwin[pl.ds(r0, QB), :].T
            delta_rows = dwin[pl.ds(r0, QB), :].T
            rq_t = rqwin[pl.ds(r0, QB), :]
            for p in range(N_PAIRS):
                k = p
                bias2 = jnp.concatenate([bias_ref[0, 2 * p], bias_ref[0, 2 * p + 1]], axis=1)
                lse2 = jnp.concatenate([lse_rows[2 * p:2 * p + 1, :], lse_rows[2 * p + 1:2 * p + 2, :]], axis=1)
                delta2 = jnp.concatenate([delta_rows[2 * p:2 * p + 1, :], delta_rows[2 * p + 1:2 * p + 2, :]], axis=1)
                pt = jnp.where(valid, jnp.exp(s_scr[k] + bias2 - lse2), 0.0)
                dst = pt * (d_scr[k] - delta2)
                ds_ref[p] += dst * owned_t
                p_scr[p] = pt.astype(BF16)
                ds_scr[p] = dst.astype(BF16)
            for p, cs in enumerate(cols):
                k = p
                for acc_ref, lhs, rhs in ((dvacc, p_scr[p], do2_scr[k]), (dkacc, ds_scr[p], q2_scr[k])):
                    part = jnp.dot(lhs, rhs, preferred_element_type=F32)
                    acc_ref[pl.ds(r0, QB), cs] += part[0:QB]
                    acc_ref[pl.ds(r0 + QB, QB), cs] = part[QB:KB]
                dq_scr[p] = lax.dot_general(kwin[pl.ds(r0, KB), cs], ds_scr[p], (((0,), (0,)), ((), ())),
                                            preferred_element_type=F32)
            for p, cs in enumerate(cols):
                g2 = gain_ref[:, cs]
                q_f = qwin[pl.ds(r0, QB), cs].astype(F32)
                dqt2 = dq_scr[p]
                dqm2 = jnp.where(chan_lo, dqt2[:, 0:QB], dqt2[:, QB:2 * QB]).T
                a_ref[p] += (dqm2 * q_f * owned).reshape(QB // 8, 8, LANES).sum(axis=0)
                dqh = dqm2 * g2
                mean_c = _split_dot(dqh * q_f, same_head) * (1.0 / HEAD_DIM)
                rq2 = _split_dot(rq_t, _head_to_lanes(p))
                dqwin[pl.ds(r0, QB), cs] = (rq2 * (dqh - q_f * mean_c)).astype(BF16)

        def one_sub(j, carry):
            scores(j)
            sub(j)
            return carry

        lax.fori_loop(0, n_sub, one_sub, 0)
        dq_ref[...] = dqwin[HALF:HALF + T]

        def finish(j, carry):
            r0 = pl.multiple_of(j * QB, QB)
            rk_t = rk_ref[pl.ds(r0, QB), :]
            for p in range(N_PAIRS):
                cs = slice(p * LANES, (p + 1) * LANES)
                dkh = dkacc[pl.ds(QB + r0, QB), cs]
                k_f = km_ref[pl.ds(r0, QB), cs].astype(F32)
                mean_c = _split_dot(dkh * k_f, same_head) * (1.0 / HEAD_DIM)
                rk2 = _split_dot(rk_t, _head_to_lanes(p))
                dk_ref[pl.ds(r0, QB), cs] = (rk2 * (dkh - k_f * mean_c)).astype(BF16)
                dv_ref[pl.ds(r0, QB), cs] = dvacc[pl.ds(QB + r0, QB), cs].astype(BF16)
            return carry

        lax.fori_loop(0, T // QB, finish, 0)

    main = pl.BlockSpec((T, ATTN_W), lambda r, i: (i, r))
    smain = pl.BlockSpec((T, LANES), lambda r, i: (i, r))
    q_prev, q_next = _halo_specs(ATTN_W, HALF, T, L, lambda r: r)
    k_prev, k_next = _halo_specs(ATTN_W, QB, T, L, lambda r: r)
    s_prev, s_next = _halo_specs(LANES, HALF, T, L, lambda r: r)
    piece = jax.ShapeDtypeStruct((L, dil * ATTN_W), BF16)
    return pl.pallas_call(
        body, name=f"attn_bwd_d{dil}", grid=(dil, L // T),
        out_shape=(piece, piece, piece, jax.ShapeDtypeStruct((N_PAIRS, KB, 2 * QB), F32),
                   jax.ShapeDtypeStruct((N_PAIRS, 8, LANES), F32)),
        in_specs=[main, q_prev, q_next, main, q_prev, q_next, main, k_prev, k_next, main, k_prev, k_next,
                  smain, s_prev, s_next, smain, s_prev, s_next, smain, s_prev, s_next, smain,
                  pl.BlockSpec((1, ATTN_W), lambda r, i: (0, 0)),
                  pl.BlockSpec((1, N_HEADS, KB, QB), lambda r, i: (g, 0, 0, 0))],
        out_specs=(main, main, main, pl.BlockSpec((N_PAIRS, KB, 2 * QB), lambda r, i: (0, 0, 0)),
                   pl.BlockSpec((N_PAIRS, 8, LANES), lambda r, i: (0, 0, 0))),
        scratch_shapes=[pltpu.VMEM((QW, ATTN_W), BF16), pltpu.VMEM((QW, ATTN_W), BF16),
                        pltpu.VMEM((KW, ATTN_W), BF16), pltpu.VMEM((KW, ATTN_W), BF16),
                        pltpu.VMEM((QW, LANES), F32), pltpu.VMEM((QW, LANES), F32), pltpu.VMEM((QW, LANES), F32),
                        pltpu.VMEM((QW, ATTN_W), BF16),
                        pltpu.VMEM((KW, ATTN_W), F32), pltpu.VMEM((KW, ATTN_W), F32),
                        pltpu.VMEM((N_PAIRS, KB, 2 * QB), F32), pltpu.VMEM((N_PAIRS, KB, 2 * QB), F32),
                        pltpu.VMEM((N_PAIRS, KB, 2 * QB), BF16), pltpu.VMEM((N_PAIRS, KB, 2 * QB), BF16),
                        pltpu.VMEM((N_PAIRS, 2 * QB, LANES), BF16), pltpu.VMEM((N_PAIRS, 2 * QB, LANES), BF16),
                        pltpu.VMEM((N_PAIRS, LANES, 2 * QB), F32)],
        compiler_params=_params(("arbitrary", "arbitrary"), 56),
    )(qv, qv, qv, dov, dov, dov, kv, kv, kv, vv, vv, vv, lv, lv, lv, dv_, dv_, dv_, rqv, rqv, rqv, rkv, gain, tables)


def _proj_bwd(x, d_out, norm_w, wg, pieces):
    S = x.shape[0]
    tm = 512
    flat = [share for p in pieces for share in p]
    n_t = S // tm

    def body(*refs):
        x_ref, g_ref, nw_ref, wg_ref = refs[:4]
        piece_refs = refs[4:4 + len(flat)]
        gx_ref, dnw_ref, ht_ref, dproj_ref, sview = refs[4 + len(flat):]
        i = pl.program_id(0)

        @pl.when(i == 0)
        def _():
            dnw_ref[...] = jnp.zeros_like(dnw_ref)

        xv = x_ref[...]
        r = lax.rsqrt(jnp.mean(xv * xv, axis=-1, keepdims=True) + EPS)
        xh = xv * r
        nw = nw_ref[...]
        ht_ref[...] = (xh * nw).T.astype(BF16)
        dh = jnp.zeros((tm, D_MODEL), F32)
        k = 0
        for j in range(8):
            shares = pieces[j]
            if len(shares) == 1:
                dp = piece_refs[k][...]
            else:
                dp = jnp.zeros((tm, CONV_W), F32)
                for t, (_, dil) in enumerate(shares):
                    if dil == 1:
                        dp = dp + piece_refs[k + t][...].astype(F32)
                    else:
                        _from_view(piece_refs[k + t], sview, dil)
                        dp = dp + _scratch_value(sview)
                dp = dp.astype(BF16)
            k += len(shares)
            dproj_ref[j] = dp
            dh = dh + lax.dot_general(dp, wg_ref[j], (((1,), (1,)), ((), ())), preferred_element_type=F32)
        dnw_ref[...] += (dh * xh).reshape(tm // 8, 8, D_MODEL).sum(axis=0)
        dxh = dh * nw
        mean_c = jnp.mean(dxh * xh, axis=-1, keepdims=True)
        gx_ref[...] = g_ref[...].astype(F32) + r * (dxh - xh * mean_c)

    tile = pl.BlockSpec((tm, D_MODEL), lambda i: (i, 0))
    return pl.pallas_call(
        body, name="proj_bwd", grid=(n_t,),
        out_shape=(jax.ShapeDtypeStruct((S, D_MODEL), F32), jax.ShapeDtypeStruct((8, D_MODEL), F32),
                   jax.ShapeDtypeStruct((D_MODEL, S), BF16), jax.ShapeDtypeStruct((8, S, CONV_W), BF16)),
        in_specs=[tile, tile, pl.BlockSpec((1, D_MODEL), lambda i: (0, 0)),
                  pl.BlockSpec(wg.shape, lambda i: (0, 0, 0), pipeline_mode=pl.Buffered(1))]
        + [_view_spec(tm, dil, CONV_W) for _, dil in flat],
        out_specs=(tile, pl.BlockSpec((8, D_MODEL), lambda i: (0, 0)),
                   pl.BlockSpec((D_MODEL, tm), lambda i: (0, i)), pl.BlockSpec((8, tm, CONV_W), lambda i: (0, i, 0))),
        scratch_shapes=[_tile_scratch(tm, CONV_W)],
        compiler_params=_params(("arbitrary",), 56),
    )(x, d_out, norm_w.reshape(1, D_MODEL), wg, *[a for a, _ in flat])


def _dw_exchange(ht, dproj, gw_out, small):
    D, S = ht.shape
    tk = 2048
    n_t = S // tk
    me_outer = 4 * lax.axis_index("x") + 2 * lax.axis_index("y") + lax.axis_index("c")
    order = ((me_outer + 1 + jnp.arange(N_DEV, dtype=jnp.int32)) % N_DEV).astype(jnp.int32)

    def body(order_ref, ht_hbm, dp_ref, gout_ref, sm_ref, rin_ref, rout_ref, rsm_ref,
             acc, sbuf, ht_vmem, in_send, in_recv, side_send, side_recv, local_sems, ht_sems):
        s = pl.program_id(0)
        t = pl.program_id(1)
        x, y, c = lax.axis_index("x"), lax.axis_index("y"), lax.axis_index("c")
        me = 4 * x + 2 * y + c

        def ht_copy(j):
            return pltpu.make_async_copy(ht_hbm.at[:, j * tk:(j + 1) * tk], ht_vmem.at[j], ht_sems.at[j])

        def in_copy(k):
            to = (me + 1 + k) % N_DEV
            return pltpu.make_async_remote_copy(
                src_ref=sbuf.at[k % 2], dst_ref=rin_ref.at[me], send_sem=in_send.at[k], recv_sem=in_recv.at[k],
                device_id=(to // 4, (to // 2) % 2, to % 2), device_id_type=MESH)

        def in_landing(k):
            frm = (me + 2 * N_DEV - 1 - k) % N_DEV
            return pltpu.make_async_remote_copy(
                src_ref=sbuf.at[0], dst_ref=rin_ref.at[frm], send_sem=in_send.at[k], recv_sem=in_recv.at[k],
                device_id=(x, y, c), device_id_type=MESH)

        def side_copies():
            local = [pltpu.make_async_copy(gout_ref.at[me], rout_ref.at[me], local_sems.at[0]),
                     pltpu.make_async_copy(sm_ref, rsm_ref.at[me], local_sems.at[1])]
            remote = []
            for k in range(1, N_DEV):
                px = 1 - x if k & 4 else x
                py = 1 - y if k & 2 else y
                pc = 1 - c if k & 1 else c
                peer = 4 * px + 2 * py + pc
                for a, (src, dst) in enumerate(((gout_ref.at[peer], rout_ref.at[me]), (sm_ref, rsm_ref.at[me]))):
                    remote.append(pltpu.make_async_remote_copy(
                        src_ref=src, dst_ref=dst, send_sem=side_send.at[a * 7 + k - 1],
                        recv_sem=side_recv.at[a * 7 + k - 1], device_id=(px, py, pc), device_id_type=MESH))
            return local, remote

        @pl.when((s == 0) & (t == 0))
        def _():
            for j in range(n_t):
                ht_copy(j).start()
            local, remote = side_copies()
            for cp in local + remote:
                cp.start()

        for j in range(n_t):
            @pl.when((s == 0) & (t == j))
            def _(j=j):
                ht_copy(j).wait()

        @pl.when(t == 0)
        def _():
            acc[...] = jnp.zeros_like(acc)

        acc[...] += jnp.dot(ht_vmem[t], dp_ref[0], preferred_element_type=F32)

        for k in range(N_DEV):
            @pl.when((s == k) & (t == n_t - 1))
            def _(k=k):
                if k >= 2:
                    in_copy(k - 2).wait_send()
                sbuf[k % 2] = acc[...].astype(BF16)
                if k < N_DEV - 1:
                    in_copy(k).start()
                else:
                    own = pltpu.make_async_copy(sbuf.at[k % 2], rin_ref.at[me], local_sems.at[2])
                    own.start()
                    in_copy(k - 1).wait_send()
                    for j in range(N_DEV - 1):
                        in_landing(j).wait_recv()
                    local, remote = side_copies()
                    for cp in remote:
                        cp.wait_recv()
                    for cp in remote:
                        cp.wait_send()
                    for cp in local:
                        cp.wait()
                    own.wait()

    hbm = pl.BlockSpec(memory_space=pl.ANY)
    grid_spec = pltpu.PrefetchScalarGridSpec(
        num_scalar_prefetch=1, grid=(N_DEV, n_t),
        in_specs=[hbm,
                  pl.BlockSpec((1, tk, CONV_W), lambda s, t, order_ref: (order_ref[s], t, 0)),
                  hbm, hbm],
        out_specs=(hbm, hbm, hbm),
        scratch_shapes=[pltpu.VMEM((D, CONV_W), F32), pltpu.VMEM((2, D, CONV_W), BF16),
                        pltpu.VMEM((n_t, D, tk), BF16),
                        pltpu.SemaphoreType.DMA((N_DEV - 1,)), pltpu.SemaphoreType.DMA((N_DEV - 1,)),
                        pltpu.SemaphoreType.DMA((14,)), pltpu.SemaphoreType.DMA((14,)),
                        pltpu.SemaphoreType.DMA((3,)), pltpu.SemaphoreType.DMA((n_t,))])
    return pl.pallas_call(
        body, name="dw_exchange", grid_spec=grid_spec,
        out_shape=(jax.ShapeDtypeStruct((N_DEV, D, CONV_W), BF16), jax.ShapeDtypeStruct(gw_out.shape, F32),
                   jax.ShapeDtypeStruct((N_DEV,) + small.shape, F32)),
        compiler_params=_params(("arbitrary", "arbitrary"), 52),
    )(order, ht, dproj, gw_out, small)


def _adamw_math(w, g, m, v):
    m2 = ADAM_B1 * m + (1.0 - ADAM_B1) * g
    v2 = ADAM_B2 * v + (1.0 - ADAM_B2) * (g * g)
    m_hat = m2 / (1.0 - ADAM_B1 ** ADAM_STEP)
    v_hat = v2 / (1.0 - ADAM_B2 ** ADAM_STEP)
    delta = -ADAM_LR * (m_hat / (jnp.sqrt(v_hat) + ADAM_EPS) + ADAM_WD * w)
    return delta, m2, v2


def _adamw_sharded(name, parts, w, m, v, rows):
    R, C = w.shape

    def body(p_ref, w_ref, m_ref, v_ref, g_ref, d_ref, m2_ref, v2_ref):
        g = p_ref[0].astype(F32)
        for s in range(1, N_DEV):
            g = g + p_ref[s].astype(F32)
        g_ref[...] = g
        d_ref[...], m2_ref[...], v2_ref[...] = _adamw_math(w_ref[...], g, m_ref[...], v_ref[...])

    spec = pl.BlockSpec((rows, C), lambda i: (i, 0))
    out = jax.ShapeDtypeStruct((R, C), F32)
    return pl.pallas_call(
        body, name=name, grid=(R // rows,),
        out_shape=(out,) * 4,
        in_specs=[pl.BlockSpec((N_DEV, rows, C), lambda i: (0, i, 0)), spec, spec, spec],
        out_specs=(spec,) * 4,
        compiler_params=_params(("arbitrary",), 40),
    )(parts, w, m, v)


def _adamw_small(parts, w, m, v):
    R = w.shape[0]

    def body(p_ref, w_ref, m_ref, v_ref, g_ref, d_ref, m2_ref, v2_ref):
        g = p_ref[0]
        for s in range(1, N_DEV):
            g = g + p_ref[s]
        g_ref[...] = g
        d_ref[...], m2_ref[...], v2_ref[...] = _adamw_math(w_ref[...], g, m_ref[...], v_ref[...])

    vm = pl.BlockSpec(memory_space=pltpu.VMEM)
    out = jax.ShapeDtypeStruct((R, LANES), F32)
    return pl.pallas_call(
        body, name="adamw_small", out_shape=(out,) * 4, in_specs=[vm] * 4, out_specs=(vm,) * 4,
    )(parts, w, m, v)


def _qk_gain_grads(a_parts, q_norm_w, k_norm_w, bias_g):
    def body(a_ref, qw_ref, kw_ref, bg_ref, out_ref):
        tot = jnp.zeros((8, LANES), F32)
        for g in range(len(DILATIONS)):
            for p in range(N_PAIRS):
                tot = tot + a_ref[g, p]
        col = jnp.sum(tot, axis=0, keepdims=True)
        a64 = col[:, 0:HEAD_DIM] + col[:, HEAD_DIM:LANES]
        out_ref[...] = jnp.zeros_like(out_ref)
        out_ref[0:1, 0:HEAD_DIM] = LOGIT_SCALE * kw_ref[...] * a64
        out_ref[1:2, 0:HEAD_DIM] = LOGIT_SCALE * qw_ref[...] * a64
        for h in range(N_HEADS):
            r0, c0 = 2 + h // 4, (h % 4) * N_BUCKETS
            out_ref[r0:r0 + 1, c0:c0 + N_BUCKETS] = bg_ref[h // 2, h % 2:h % 2 + 1, 0:N_BUCKETS]

    vm = pl.BlockSpec(memory_space=pltpu.VMEM)
    return pl.pallas_call(
        body, name="qk_gain_grads", out_shape=jax.ShapeDtypeStruct((8, LANES), F32),
        in_specs=[vm] * 4, out_specs=vm,
    )(a_parts, q_norm_w.reshape(1, HEAD_DIM), k_norm_w.reshape(1, HEAD_DIM), bias_g)


SMALL_ROWS = 32


def _pack_small(norm_w, conv_b, q_norm_w, k_norm_w, rel_bias, conv_w_full):
    pad64 = lambda a: jnp.pad(a, (0, LANES - HEAD_DIM)).reshape(1, LANES)
    return jnp.concatenate([
        norm_w.reshape(8, LANES), conv_b.reshape(4, LANES), pad64(q_norm_w), pad64(k_norm_w),
        rel_bias.T.reshape(2, LANES), conv_w_full.reshape(12, LANES), jnp.zeros((4, LANES), F32)], axis=0)


def _unpack_small(s):
    return (s[0:8].reshape(D_MODEL), s[8:12].reshape(CONV_W), s[12, 0:HEAD_DIM], s[13, 0:HEAD_DIM],
            s[14:16].reshape(N_HEADS, N_BUCKETS).T, s[16:28].reshape(3, CONV_W))


def kernel(x, norm_w, w_in, conv_w, conv_b, q_norm_w, k_norm_w, rel_bias, w_out, loss_target, m_norm_w, m_w_in, m_conv_w, m_conv_b, m_q_norm_w, m_k_norm_w, m_rel_bias, m_w_out, v_norm_w, v_w_in, v_conv_w, v_conv_b, v_q_norm_w, v_k_norm_w, v_rel_bias, v_w_out):
    S = x.shape[1]
    x2 = x.reshape(S, D_MODEL)
    tgt = loss_target.reshape(S, D_MODEL)
    me = 4 * lax.axis_index("x") + 2 * lax.axis_index("y") + lax.axis_index("c")
    shard_w = CONV_W // N_DEV

    cw_pad = jnp.pad(conv_w, ((0, 5), (0, LANES - shard_w)))
    wg, wo_g, cw_g = _ag_weights(w_in, w_out, cw_pad)
    wo = wo_g.reshape(D_MODEL, D_MODEL)
    conv_w_full = cw_g[:, 0:3, 0:shard_w].transpose(1, 0, 2).reshape(3, CONV_W)

    bmat_t = jnp.asarray(np.stack([_bucket_matrix(d).T for d in DILATIONS]))
    tables = _bias_tables(rel_bias, bmat_t)
    gain = jnp.tile(q_norm_w * k_norm_w * LOGIT_SCALE, N_HEADS).reshape(1, ATTN_W)

    u, gb, gc, zc, za, q_hat, k_hat, v, rq, rk = _fwd_proj(x2, norm_w, wg)
    o_parts, lse_parts = [], []
    for g, dil in enumerate(DILATIONS):
        o_g, lse_g = _attn_fwd(g, dil, q_hat[g], k_hat[g], v[g], gain, tables)
        o_parts.append(o_g)
        lse_parts.append(lse_g)
    y, o, lse = _combine_gates(o_parts, lse_parts, u, gb, gc, zc, za, conv_w_full, conv_b)
    d_out, dy, gwo_part, loss_part = _out_proj(y, x2, tgt, wo)
    loss_mine = jnp.sum(loss_part) * (0.5 / D_MODEL)

    (du, dgb, dgc, dzc, dza), conv_small, d_o, delta = _gates_bwd(dy, u, gb, gc, zc, za, o, conv_w_full, conv_b)
    dqs, dks, dvs, ds_parts, a_parts = [], [], [], [], []
    for g, dil in enumerate(DILATIONS):
        dq, dk, dv, ds_sum, a_sum = _attn_bwd(g, dil, q_hat[g], k_hat[g], v[g], d_o[g], lse[g], delta[g], rq[g], rk[g],
                                              gain, tables)
        dqs.append((dq, dil))
        dks.append((dk, dil))
        dvs.append((dv, dil))
        ds_parts.append(ds_sum)
        a_parts.append(a_sum)
    grad_x, gnw_part, h_t, dproj = _proj_bwd(
        x2, d_out, norm_w, wg, [[(du, 1)], [(dgb, 1)], [(dgc, 1)], [(dzc, 1)], dqs, dks, dvs, [(dza, 1)]])
    bias_g = _bias_grad(jnp.stack(ds_parts), bmat_t)
    qk_small = _qk_gain_grads(jnp.stack(a_parts), q_norm_w, k_norm_w, bias_g)

    small_part = jnp.concatenate([
        gnw_part.sum(axis=0).reshape(8, LANES), conv_small[3].reshape(4, LANES), qk_small[0:4],
        conv_small[0:3].reshape(12, LANES), jnp.zeros((4, LANES), F32).at[0, 0].set(loss_mine)], axis=0)
    r_in, r_out, r_small = _dw_exchange(h_t, dproj, gwo_part.reshape(N_DEV, D_MODEL // N_DEV, D_MODEL), small_part)

    g_win, d_win, m_win, v_win = _adamw_sharded("adamw_w_in", r_in, w_in, m_w_in, v_w_in, 128)
    g_wo, d_wo, m_wo, v_wo = _adamw_sharded("adamw_w_out", r_out, w_out, m_w_out, v_w_out, 32)

    def full_conv(a):
        return jnp.zeros((3, CONV_W), F32).at[:, 0:shard_w].set(a)

    packs = [_pack_small(nw_, cb_, qw_, kw_, rb_, full_conv(cw_)) for nw_, cb_, qw_, kw_, rb_, cw_ in (
        (norm_w, conv_b, q_norm_w, k_norm_w, rel_bias, conv_w),
        (m_norm_w, m_conv_b, m_q_norm_w, m_k_norm_w, m_rel_bias, m_conv_w),
        (v_norm_w, v_conv_b, v_q_norm_w, v_k_norm_w, v_rel_bias, v_conv_w))]
    r_small_mine = r_small.at[:, 16:28, :].set(
        jnp.pad(lax.dynamic_slice_in_dim(r_small[:, 16:28, :].reshape(N_DEV, 3, CONV_W), me * shard_w, shard_w, axis=2),
                ((0, 0), (0, 0), (0, CONV_W - shard_w))).reshape(N_DEV, 12, LANES))
    outs_small = _adamw_small(r_small_mine, *packs)
    g_s, d_s, m_s, v_s = [_unpack_small(t) for t in outs_small]
    loss = outs_small[0][28, 0]

    def leaves(small, big_in, big_out):
        nw_, cb_, qw_, kw_, rb_, cwf = small
        return (nw_, big_in, cwf[:, 0:shard_w], cb_, qw_, kw_, rb_, big_out)

    return (loss, grad_x.reshape(x.shape),
            *leaves(g_s, g_win, g_wo), *leaves(d_s, d_win, d_wo), *leaves(m_s, m_win, m_wo), *leaves(v_s, v_win, v_wo))
```

```python
import math

import numpy as np
import jax
import jax.numpy as jnp
from jax import lax
from jax.experimental import pallas as pl
from jax.experimental.pallas import tpu as pltpu

F32 = jnp.float32
BF16 = jnp.bfloat16

N_DEV = 8
D_MODEL = 1024
CONV_W = 512
ATTN_W = 512
N_HEADS = 8
HEAD_DIM = 64
N_PAIRS = N_HEADS // 2
LANES = 128
HALF = 64
QB = 128
KB = QB + 2 * HALF
DILATIONS = (1, 4, 16)
N_BUCKETS = 32
MAX_DISTANCE = 1024
EPS = 1e-6
NEG = -1e30
LOGIT_SCALE = HEAD_DIM ** -0.5

ADAM_LR = 0.001
ADAM_B1 = 0.9
ADAM_B2 = 0.999
ADAM_EPS = 1e-08
ADAM_WD = 0.01
ADAM_STEP = 10

MESH = pl.DeviceIdType.MESH
MIB = 1024 * 1024


def _params(semantics, vmem_mib):
    return pltpu.CompilerParams(dimension_semantics=semantics, vmem_limit_bytes=vmem_mib * MIB)


def _lane(shape):
    return lax.broadcasted_iota(jnp.int32, shape, len(shape) - 1)


def _sigmoid(z):
    return 1.0 / (1.0 + jnp.exp(-z))


def _split_dot(x, w):
    hi = x.astype(BF16)
    lo = (x - hi.astype(F32)).astype(BF16)
    return jnp.dot(hi, w, preferred_element_type=F32) + jnp.dot(lo, w, preferred_element_type=F32)


def _same_head():
    r = lax.broadcasted_iota(jnp.int32, (LANES, LANES), 0) // HEAD_DIM
    c = lax.broadcasted_iota(jnp.int32, (LANES, LANES), 1) // HEAD_DIM
    return (r == c).astype(BF16)


def _head_to_lanes(p):
    r = lax.broadcasted_iota(jnp.int32, (LANES, LANES), 0)
    c = lax.broadcasted_iota(jnp.int32, (LANES, LANES), 1) // HEAD_DIM
    return (r == 2 * p + c).astype(BF16)


def _lanes_to_head(p, scale):
    r = lax.broadcasted_iota(jnp.int32, (LANES, LANES), 0) // HEAD_DIM
    c = lax.broadcasted_iota(jnp.int32, (LANES, LANES), 1)
    return jnp.where(c == 2 * p + r, scale, 0.0).astype(BF16)


VIEW_DILATIONS = DILATIONS[1:]


def _view_shape(S, dil, width, dtype):
    return jax.ShapeDtypeStruct((S // dil, dil * width), dtype)


def _view_spec(tm, dil, width):
    return pl.BlockSpec((tm // dil, dil * width), lambda i: (i, 0))


def _tile_scratch(tm, width):
    return pltpu.VMEM((width // LANES, tm, LANES), F32)


def _scratch_value(scr_ref):
    n = scr_ref.shape[0]
    return scr_ref[0] if n == 1 else jnp.concatenate([scr_ref[c] for c in range(n)], axis=1)


def _to_view(scr_ref, out_ref, dil, dtype):
    n, tm, _ = scr_ref.shape
    for r in range(dil):
        for c in range(n):
            col = (r * n + c) * LANES
            out_ref[:, col:col + LANES] = scr_ref[c, pl.ds(r, tm // dil, stride=dil), :].astype(dtype)


def _from_view(blk_ref, scr_ref, dil):
    n, tm, _ = scr_ref.shape
    for r in range(dil):
        for c in range(n):
            col = (r * n + c) * LANES
            scr_ref[c, pl.ds(r, tm // dil, stride=dil), :] = blk_ref[:, col:col + LANES].astype(F32)


def _bucket_matrix(dilation):
    rel = np.arange(KB)[None, :] - HALF - np.arange(QB)[:, None]
    band = np.abs(rel) <= HALF
    dist = np.clip(rel, -HALF, HALF) * dilation
    half_b = N_BUCKETS // 2
    max_exact = half_b // 2
    ret = np.where(dist > 0, half_b, 0)
    n = np.abs(dist)
    nf = np.maximum(n, 1).astype(np.float32)
    large = max_exact + (np.log(nf / np.float32(max_exact)) / np.float32(math.log(MAX_DISTANCE / max_exact))
                         * np.float32(half_b - max_exact)).astype(np.int32)
    large = np.minimum(large, half_b - 1)
    bucket = ret + np.where(n < max_exact, n, large)
    return np.where(band, bucket, -1).astype(np.int32)


def _ag_weights(w_in, w_out, cw_pad):
    n_arr = 3

    def body(win_ref, wout_ref, cw_ref, gin_ref, gout_ref, gcw_ref, send_sems, recv_sems):
        x, y, c = lax.axis_index("x"), lax.axis_index("y"), lax.axis_index("c")
        me = (x, y, c)
        sibling = (x, y, 1 - c)
        chips = [(1 - x, y), (x, 1 - y), (1 - x, 1 - y)]
        arrays = (gin_ref, gout_ref, gcw_ref)

        def slot(px, py, pc):
            return 4 * px + 2 * py + pc

        gin_ref[slot(*me)] = win_ref[...].astype(BF16)
        gout_ref[slot(*me)] = wout_ref[...].astype(BF16)
        gcw_ref[slot(*me)] = cw_ref[...]

        def copy(a, k, block, to):
            ref = arrays[a].at[slot(*block)]
            return pltpu.make_async_remote_copy(
                src_ref=ref, dst_ref=ref, send_sem=send_sems.at[a * 7 + k], recv_sem=recv_sems.at[a * 7 + k],
                device_id=to, device_id_type=MESH)

        first = [copy(a, 0, me, sibling) for a in range(n_arr)]
        for j, chip in enumerate(chips):
            first += [copy(a, 1 + j, me, (*chip, c)) for a in range(n_arr)]
        for cp in first:
            cp.start()
        passed = []
        for j, chip in enumerate(chips):
            for a in range(n_arr):
                copy(a, 1 + j, (*chip, c), me).wait_recv()
            for a in range(n_arr):
                cp = copy(a, 4 + j, (*chip, c), sibling)
                cp.start()
                passed.append(cp)
        for a in range(n_arr):
            copy(a, 0, sibling, me).wait_recv()
        for j, chip in enumerate(chips):
            for a in range(n_arr):
                copy(a, 4 + j, (*chip, 1 - c), me).wait_recv()
        for cp in first + passed:
            cp.wait_send()

    vm = pl.BlockSpec(memory_space=pltpu.VMEM)
    return pl.pallas_call(
        body, name="ag_weights",
        out_shape=(jax.ShapeDtypeStruct((N_DEV,) + w_in.shape, BF16),
                   jax.ShapeDtypeStruct((N_DEV,) + w_out.shape, BF16),
                   jax.ShapeDtypeStruct((N_DEV,) + cw_pad.shape, F32)),
        in_specs=[vm, vm, vm], out_specs=(vm, vm, vm),
        scratch_shapes=[pltpu.SemaphoreType.DMA((n_arr * 7,)), pltpu.SemaphoreType.DMA((n_arr * 7,))],
        compiler_params=pltpu.CompilerParams(vmem_limit_bytes=40 * MIB),
    )(w_in, w_out, cw_pad)


def _fwd_proj(x, norm_w, wg):
    S = x.shape[0]
    tm = 512

    def body(x_ref, nw_ref, wg_ref, u_ref, gb_ref, gc_ref, zc_ref, za_ref, *rest):
        q_refs, k_refs, v_refs, rq_refs, rk_refs = (rest[3 * n:3 * n + 3] for n in range(5))
        scr, rscr = rest[15:]
        xv = x_ref[...]
        r = lax.rsqrt(jnp.mean(xv * xv, axis=-1, keepdims=True) + EPS)
        h = (xv * r * nw_ref[...]).astype(BF16)
        lane = _lane((tm, LANES))
        lo = lane < HEAD_DIM
        plain = {0: u_ref, 1: gb_ref, 2: gc_ref, 3: zc_ref, 7: za_ref}
        normed = {4: (q_refs, rq_refs), 5: (k_refs, rk_refs)}

        def emit(src, refs, dtype):
            refs[0][...] = _scratch_value(src).astype(dtype)
            for dil, ref in zip(VIEW_DILATIONS, refs[1:]):
                _to_view(src, ref, dil, dtype)

        acc_next = jnp.dot(h, wg_ref[0], preferred_element_type=F32)
        for j in range(8):
            acc = acc_next
            if j < 7:
                acc_next = jnp.dot(h, wg_ref[j + 1], preferred_element_type=F32)
            if j in plain:
                plain[j][...] = acc.astype(BF16)
                continue
            if j == 6:
                for p in range(N_PAIRS):
                    scr[p] = acc[:, p * LANES:(p + 1) * LANES]
                emit(scr, v_refs, BF16)
                continue
            out_refs, r_refs = normed[j]
            r_tile = jnp.zeros((tm, LANES), F32)
            for p in range(N_PAIRS):
                blk = acc[:, p * LANES:(p + 1) * LANES]
                sq = blk * blk
                s_lo = jnp.sum(jnp.where(lo, sq, 0.0), axis=-1, keepdims=True)
                s_hi = jnp.sum(jnp.where(lo, 0.0, sq), axis=-1, keepdims=True)
                r_lo = lax.rsqrt(s_lo * (1.0 / HEAD_DIM) + EPS)
                r_hi = lax.rsqrt(s_hi * (1.0 / HEAD_DIM) + EPS)
                scr[p] = blk * jnp.where(lo, r_lo, r_hi)
                r_tile = jnp.where(lane == 2 * p, r_lo, r_tile)
                r_tile = jnp.where(lane == 2 * p + 1, r_hi, r_tile)
            rscr[0] = r_tile
            emit(scr, out_refs, BF16)
            emit(rscr, r_refs, F32)

    piece = jax.ShapeDtypeStruct((S, CONV_W), BF16)
    pspec = pl.BlockSpec((tm, CONV_W), lambda i: (i, 0))
    wide = [_view_shape(S, d, ATTN_W, BF16) for d in DILATIONS]
    wide_specs = [_view_spec(tm, d, ATTN_W) for d in DILATIONS]
    stat = [_view_shape(S, d, LANES, F32) for d in DILATIONS]
    stat_specs = [_view_spec(tm, d, LANES) for d in DILATIONS]
    outs = pl.pallas_call(
        body, name="fwd_proj", grid=(S // tm,),
        out_shape=[piece] * 5 + wide * 3 + stat * 2,
        in_specs=[pl.BlockSpec((tm, D_MODEL), lambda i: (i, 0)),
                  pl.BlockSpec((1, D_MODEL), lambda i: (0, 0)),
                  pl.BlockSpec(wg.shape, lambda i: (0, 0, 0), pipeline_mode=pl.Buffered(1))],
        out_specs=[pspec] * 5 + wide_specs * 3 + stat_specs * 2,
        scratch_shapes=[_tile_scratch(tm, ATTN_W), _tile_scratch(tm, LANES)],
        compiler_params=_params(("arbitrary",), 48),
    )(x, norm_w.reshape(1, D_MODEL), wg)
    u, gb, gc, zc, za = outs[:5]
    q_hat, k_hat, v, rq, rk = (outs[5 + 3 * n:8 + 3 * n] for n in range(5))
    return u, gb, gc, zc, za, q_hat, k_hat, v, rq, rk


def _bias_tables(rel_bias, bmat_t):
    def body(rb_ref, b_ref, out_ref):
        h = pl.program_id(1)
        b = b_ref[0]
        t = jnp.full((KB, QB), NEG, F32)
        for bk in range(N_BUCKETS):
            t = jnp.where(b == bk, rb_ref[bk, h], t)
        out_ref[0, 0] = t

    return pl.pallas_call(
        body, name="bias_tables", grid=(len(DILATIONS), N_HEADS),
        out_shape=jax.ShapeDtypeStruct((len(DILATIONS), N_HEADS, KB, QB), F32),
        in_specs=[pl.BlockSpec(memory_space=pltpu.SMEM),
                  pl.BlockSpec((1, KB, QB), lambda g, h: (g, 0, 0))],
        out_specs=pl.BlockSpec((1, 1, KB, QB), lambda g, h: (g, h, 0, 0)),
        compiler_params=_params(("arbitrary", "arbitrary"), 16),
    )(rel_bias, bmat_t)


def _bias_grad(ds_acc, bmat_t):
    present = [sorted(set(_bucket_matrix(d).ravel().tolist()) - {-1}) for d in DILATIONS]

    def body(ds_ref, b_ref, out_ref):
        lane = _lane((1, LANES))
        row = lax.broadcasted_iota(jnp.int32, (8, LANES), 0)
        out = jnp.zeros((8, LANES), F32)
        for e in range(2):
            vec = jnp.zeros((1, LANES), F32)
            for g in range(len(DILATIONS)):
                b = b_ref[g]
                ds = ds_ref[g, 0, :, e * QB:(e + 1) * QB]
                for bk in present[g]:
                    s = jnp.sum(jnp.where(b == bk, ds, 0.0), axis=-1, keepdims=True)
                    s = jnp.sum(s, axis=0, keepdims=True)
                    vec = vec + jnp.where(lane == bk, s, 0.0)
            out = jnp.where(row == e, vec, out)
        out_ref[0] = out

    return pl.pallas_call(
        body, name="bias_grad", grid=(N_PAIRS,),
        out_shape=jax.ShapeDtypeStruct((N_PAIRS, 8, LANES), F32),
        in_specs=[pl.BlockSpec((len(DILATIONS), 1, KB, 2 * QB), lambda p: (0, p, 0, 0)),
                  pl.BlockSpec((len(DILATIONS), KB, QB), lambda p: (0, 0, 0))],
        out_specs=pl.BlockSpec((1, 8, LANES), lambda p: (p, 0, 0)),
        compiler_params=_params(("arbitrary",), 16),
    )(ds_acc, bmat_t)


def _halo_specs(width, rows, T, L, cols_of):
    per = T // rows
    last = L // rows - 1
    prev = pl.BlockSpec((rows, width), lambda r, i: (jnp.maximum(i * per - 1, 0), cols_of(r)))
    nxt = pl.BlockSpec((rows, width), lambda r, i: (jnp.minimum((i + 1) * per, last), cols_of(r)))
    return prev, nxt


def _attn_fwd(g, dil, qv, kv, vv, gain, tables):
    L = qv.shape[0]
    T = min(1024, L)
    n_sub = T // QB

    def body(q_ref, km_ref, kp_ref, kn_ref, vm_ref, vp_ref, vn_ref, gain_ref, bias_ref, o_ref, lse_ref,
             kwin, vwin, s_scr, p_scr, vt_scr):
        i = pl.program_id(1)
        kwin[0:HALF] = kp_ref[...]
        kwin[HALF:HALF + T] = km_ref[...]
        kwin[HALF + T:] = kn_ref[...]
        vwin[0:HALF] = vp_ref[...]
        vwin[HALF:HALF + T] = vm_ref[...]
        vwin[HALF + T:] = vn_ref[...]
        for p in range(N_PAIRS):
            for kb in range(n_sub + 1):
                blk = vwin[kb * QB:(kb + 1) * QB, p * LANES:(p + 1) * LANES]
                vt_scr[p, kb] = blk.astype(F32).T.astype(BF16)
        lo = _lane((QB, LANES)) < HEAD_DIM
        krow = lax.broadcasted_iota(jnp.int32, (KB, 1), 0)
        chan_lo = lax.broadcasted_iota(jnp.int32, (LANES, 1), 0) < HEAD_DIM
        hrow = lax.broadcasted_iota(jnp.int32, (LANES, QB), 0)

        def sub(j, carry):
            r0 = pl.multiple_of(j * QB, QB)
            kpos = i * T + j * QB - HALF + krow
            kvalid = (kpos >= 0) & (kpos < L)
            lse_rows = jnp.zeros((LANES, QB), F32)
            cols = [slice(p * LANES, (p + 1) * LANES) for p in range(N_PAIRS)]
            for p, cs in enumerate(cols):
                qf = q_ref[pl.ds(r0, QB), cs].astype(F32) * gain_ref[:, cs]
                q2 = jnp.concatenate([jnp.where(lo, qf, 0.0), jnp.where(lo, 0.0, qf)], axis=0).astype(BF16)
                s_scr[p] = lax.dot_general(kwin[pl.ds(r0, KB), cs], q2, (((1,), (1,)), ((), ())),
                                           preferred_element_type=F32)
            inv_l = []
            for p in range(N_PAIRS):
                bias2 = jnp.concatenate([bias_ref[0, 2 * p], bias_ref[0, 2 * p + 1]], axis=1)
                logits = jnp.where(kvalid, s_scr[p] + bias2, NEG)
                m = jnp.max(logits, axis=0, keepdims=True)
                pt = jnp.exp(logits - m)
                l = jnp.sum(pt, axis=0, keepdims=True)
                p_scr[p] = pt.astype(BF16)
                inv_l.append(1.0 / l)
                lse2 = m + jnp.log(l)
                lse_rows = jnp.where(hrow == 2 * p, lse2[:, 0:QB], lse_rows)
                lse_rows = jnp.where(hrow == 2 * p + 1, lse2[:, QB:2 * QB], lse_rows)
            for p, cs in enumerate(cols):
                ot2 = (jnp.dot(vt_scr[p, j], p_scr[p, 0:QB, :], preferred_element_type=F32)
                       + jnp.dot(vt_scr[p, j + 1], p_scr[p, QB:KB, :], preferred_element_type=F32))
                s_scr[p, 0:LANES, :] = ot2 * inv_l[p]
            for p, cs in enumerate(cols):
                ot2 = s_scr[p, 0:LANES, :]
                ot = jnp.where(chan_lo, ot2[:, 0:QB], ot2[:, QB:2 * QB])
                o_ref[pl.ds(r0, QB), cs] = ot.T.astype(BF16)
            lse_ref[pl.ds(r0, QB), :] = lse_rows.T
            return carry

        lax.fori_loop(0, n_sub, sub, 0)

    main = pl.BlockSpec((T, ATTN_W), lambda r, i: (i, r))
    prev, nxt = _halo_specs(ATTN_W, HALF, T, L, lambda r: r)
    o_g, lse_g = pl.pallas_call(
        body, name=f"attn_fwd_d{dil}", grid=(dil, L // T),
        out_shape=(jax.ShapeDtypeStruct((L, dil * ATTN_W), BF16), jax.ShapeDtypeStruct((L, dil * LANES), F32)),
        in_specs=[main, main, prev, nxt, main, prev, nxt,
                  pl.BlockSpec((1, ATTN_W), lambda r, i: (0, 0)),
                  pl.BlockSpec((1, N_HEADS, KB, QB), lambda r, i: (g, 0, 0, 0))],
        out_specs=(main, pl.BlockSpec((T, LANES), lambda r, i: (i, r))),
        scratch_shapes=[pltpu.VMEM((T + 2 * HALF, ATTN_W), BF16), pltpu.VMEM((T + 2 * HALF, ATTN_W), BF16),
                        pltpu.VMEM((N_PAIRS, KB, 2 * QB), F32), pltpu.VMEM((N_PAIRS, KB, 2 * QB), BF16),
                        pltpu.VMEM((N_PAIRS, n_sub + 1, LANES, QB), BF16)],
        compiler_params=_params(("arbitrary", "arbitrary"), 40),
    )(qv, kv, kv, kv, vv, vv, vv, gain, tables)
    return o_g, lse_g


HALO = 16
CHUNK = 128


def _row_halo_specs(width, tm, S, col=0):
    rows = HALO
    per = tm // rows
    last = S // rows - 1
    prev = pl.BlockSpec((rows, width), lambda i: (jnp.maximum(i * per - 1, 0), col))
    nxt = pl.BlockSpec((rows, width), lambda i: (jnp.minimum((i + 1) * per, last), col))
    return prev, nxt


def _shift_rows(a, prev_row, next_row, tm):
    row = lax.broadcasted_iota(jnp.int32, a.shape, 0)
    a_m1 = jnp.where(row == 0, prev_row, pltpu.roll(a, 1, 0))
    a_p1 = jnp.where(row == tm - 1, next_row, pltpu.roll(a, tm - 1, 0))
    return a_m1, a_p1


def _combine_gates(o_parts, lse_parts, u, gb, gc, zc, za, conv_w, conv_b):
    S = u.shape[0]
    tm = 512
    n_t = S // tm

    def body(o1, o2, o3, l1, l2, l3, u_ref, up_ref, un_ref, gb_ref, gc_ref, gcp_ref, gcn_ref, zc_ref, za_ref,
             cw_ref, cb_ref, y_ref, o_ref, lse_ref, lse2_ref, lse3_ref, so2, so3, sl2, sl3, slse):
        i = pl.program_id(0)
        for blk, scr, dil in ((o2, so2, DILATIONS[1]), (o3, so3, DILATIONS[2]),
                              (l2, sl2, DILATIONS[1]), (l3, sl3, DILATIONS[2])):
            _from_view(blk, scr, dil)
        ls = [l1[...], sl2[0], sl3[0]]
        lmax = jnp.maximum(jnp.maximum(ls[0], ls[1]), ls[2])
        es = [jnp.exp(l - lmax) for l in ls]
        den = es[0] + es[1] + es[2]
        slse[0] = lmax + jnp.log(den)
        lse_ref[...] = slse[0]
        _to_view(slse, lse2_ref, DILATIONS[1], F32)
        _to_view(slse, lse3_ref, DILATIONS[2], F32)
        inv = 1.0 / den
        ws = [e * inv for e in es]
        for p in range(N_PAIRS):
            cs = slice(p * LANES, (p + 1) * LANES)
            acc = jnp.zeros((tm, LANES), F32)
            spread = _head_to_lanes(p)
            for w, o_g in zip(ws, (o1[:, cs].astype(F32), so2[p], so3[p])):
                acc = acc + _split_dot(w, spread) * o_g
            o_ref[:, cs] = acc.astype(BF16)
            za = za_ref[:, cs].astype(F32)
            y_ref[:, CONV_W + p * LANES:CONV_W + (p + 1) * LANES] = (acc * za * _sigmoid(za)).astype(BF16)
        a = gc_ref[...].astype(F32) * u_ref[...].astype(F32)
        a_prev = gcp_ref[15:16, :].astype(F32) * up_ref[15:16, :].astype(F32) * (i > 0).astype(F32)
        a_next = gcn_ref[0:1, :].astype(F32) * un_ref[0:1, :].astype(F32) * (i < n_t - 1).astype(F32)
        a_m1, a_p1 = _shift_rows(a, a_prev, a_next, tm)
        z3 = cw_ref[0:1, :] * a_m1 + cw_ref[1:2, :] * a + cw_ref[2:3, :] * a_p1
        zc = zc_ref[...].astype(F32)
        y_ref[:, 0:CONV_W] = (gb_ref[...].astype(F32) * (z3 + cb_ref[...]) * zc * _sigmoid(zc)).astype(BF16)

    pspec = pl.BlockSpec((tm, CONV_W), lambda i: (i, 0))
    hp, hn = _row_halo_specs(CONV_W, tm, S)
    wide_specs = [_view_spec(tm, d, ATTN_W) for d in DILATIONS]
    stat_specs = [_view_spec(tm, d, LANES) for d in DILATIONS]
    outs = pl.pallas_call(
        body, name="combine_gates", grid=(n_t,),
        out_shape=[jax.ShapeDtypeStruct((S, D_MODEL), BF16), jax.ShapeDtypeStruct((S, ATTN_W), BF16)]
        + [_view_shape(S, d, LANES, F32) for d in DILATIONS],
        in_specs=wide_specs + stat_specs + [pspec, hp, hn, pspec, pspec, hp, hn, pspec, pspec,
                                            pl.BlockSpec((3, CONV_W), lambda i: (0, 0)),
                                            pl.BlockSpec((1, CONV_W), lambda i: (0, 0))],
        out_specs=[pl.BlockSpec((tm, D_MODEL), lambda i: (i, 0)), pspec] + stat_specs,
        scratch_shapes=[_tile_scratch(tm, ATTN_W), _tile_scratch(tm, ATTN_W),
                        _tile_scratch(tm, LANES), _tile_scratch(tm, LANES), _tile_scratch(tm, LANES)],
        compiler_params=_params(("arbitrary",), 40),
    )(*o_parts, *lse_parts, u, u, u, gb, gc, gc, gc, zc, za, conv_w, conv_b.reshape(1, CONV_W))
    return outs[0], outs[1], outs[2:]


def _out_proj(y, x, target, wo):
    S = x.shape[0]
    tm = 512

    def body(y_ref, x_ref, t_ref, wo_ref, g_ref, dy_ref, dwo_ref, loss_ref):
        i = pl.program_id(0)

        @pl.when(i == 0)
        def _():
            dwo_ref[...] = jnp.zeros_like(dwo_ref)
            loss_ref[...] = jnp.zeros_like(loss_ref)

        yv = y_ref[...]
        wo_v = wo_ref[...]
        err = x_ref[...] + jnp.dot(yv, wo_v, preferred_element_type=F32) - t_ref[...]
        e2 = (err * err).reshape(tm // 8, 8, D_MODEL).sum(axis=0)
        part = e2[:, 0:LANES]
        for k in range(1, D_MODEL // LANES):
            part = part + e2[:, k * LANES:(k + 1) * LANES]
        loss_ref[...] += part
        gv = (err * (1.0 / D_MODEL)).astype(BF16)
        g_ref[...] = gv
        dy_ref[...] = lax.dot_general(gv, wo_v, (((1,), (1,)), ((), ())), preferred_element_type=F32).astype(BF16)
        dwo_ref[...] += lax.dot_general(yv, gv, (((0,), (0,)), ((), ())), preferred_element_type=F32)

    tile = pl.BlockSpec((tm, D_MODEL), lambda i: (i, 0))
    return pl.pallas_call(
        body, name="out_proj", grid=(S // tm,),
        out_shape=(jax.ShapeDtypeStruct((S, D_MODEL), BF16), jax.ShapeDtypeStruct((S, D_MODEL), BF16),
                   jax.ShapeDtypeStruct((D_MODEL, D_MODEL), F32), jax.ShapeDtypeStruct((8, LANES), F32)),
        in_specs=[tile, tile, tile, pl.BlockSpec((D_MODEL, D_MODEL), lambda i: (0, 0), pipeline_mode=pl.Buffered(1))],
        out_specs=(tile, tile, pl.BlockSpec((D_MODEL, D_MODEL), lambda i: (0, 0)),
                   pl.BlockSpec((8, LANES), lambda i: (0, 0))),
        compiler_params=_params(("arbitrary",), 48),
    )(y, x, target, wo)


def _gates_bwd(dy, u, gb, gc, zc, za, o, conv_w, conv_b):
    S = u.shape[0]
    tm = 512
    n_t = S // tm

    def body(dy_ref, dyp_ref, dyn_ref, u_ref, up_ref, un_ref, gb_ref, gbp_ref, gbn_ref, gc_ref, gcp_ref, gcn_ref,
             zc_ref, zcp_ref, zcn_ref, za_ref, o_ref, cw_ref, cb_ref,
             du_ref, dgb_ref, dgc_ref, dzc_ref, dza_ref, small_ref,
             do_ref, do2_ref, do3_ref, delta_ref, delta2_ref, delta3_ref, sdo, sdelta,
             u_ext, gc_ext, gb_ext, zc_ext, dy_ext):
        i = pl.program_id(0)

        @pl.when(i == 0)
        def _():
            small_ref[...] = jnp.zeros_like(small_ref)

        exts = ((u_ext, up_ref, u_ref, un_ref), (gc_ext, gcp_ref, gc_ref, gcn_ref), (gb_ext, gbp_ref, gb_ref, gbn_ref),
                (zc_ext, zcp_ref, zc_ref, zcn_ref))
        for ext, pr, mn, nx in exts:
            ext[0:HALO] = pr[...]
            ext[HALO:HALO + tm] = mn[...]
            ext[HALO + tm:] = nx[...]
        dy_ext[0:HALO] = dyp_ref[...]
        dy_ext[HALO:HALO + tm] = dy_ref[:, 0:CONV_W]
        dy_ext[HALO + tm:] = dyn_ref[...]
        zero_halo = jnp.zeros((HALO, CONV_W), BF16)

        @pl.when(i == 0)
        def _():
            u_ext[0:HALO] = zero_halo
            dy_ext[0:HALO] = zero_halo

        @pl.when(i == n_t - 1)
        def _():
            u_ext[HALO + tm:] = zero_halo
            dy_ext[HALO + tm:] = zero_halo

        wide_rows = CHUNK + 2 * HALO
        mid = slice(HALO, HALO + CHUNK)
        csum = lambda t: jnp.sum(t, axis=0, keepdims=True)

        def one_group(p, c0, sums):
            cs = slice(p * LANES, (p + 1) * LANES)
            w0, w1, w2, cb = cw_ref[0:1, cs], cw_ref[1:2, cs], cw_ref[2:3, cs], cb_ref[:, cs]
            rows = pl.ds(c0, CHUNK)
            wide = pl.ds(c0, wide_rows)
            u_e, gc_e, gb_e, zc_e, dy_e = (r[wide, cs].astype(F32) for r in (u_ext, gc_ext, gb_ext, zc_ext, dy_ext))
            a_e = gc_e * u_e
            sz_e = _sigmoid(zc_e)
            dz3_e = dy_e * zc_e * sz_e * gb_e
            a, a_m1, a_p1 = a_e[mid], pltpu.roll(a_e, 1, 0)[mid], pltpu.roll(a_e, wide_rows - 1, 0)[mid]
            dz3, dz3_m1, dz3_p1 = dz3_e[mid], pltpu.roll(dz3_e, 1, 0)[mid], pltpu.roll(dz3_e, wide_rows - 1, 0)[mid]
            uv, gcv, gbv, zcv, dyc, sz = u_e[mid], gc_e[mid], gb_e[mid], zc_e[mid], dy_e[mid], sz_e[mid]
            cvp = w0 * a_m1 + w1 * a + w2 * a_p1 + cb
            dcv = dyc * zcv * sz
            dzc_ref[rows, cs] = (dyc * gbv * cvp * (sz * (1.0 + zcv * (1.0 - sz)))).astype(BF16)
            dgb_ref[rows, cs] = (dcv * cvp).astype(BF16)
            da = w0 * dz3_p1 + w1 * dz3 + w2 * dz3_m1
            du_ref[rows, cs] = (da * gcv).astype(BF16)
            dgc_ref[rows, cs] = (da * uv).astype(BF16)
            sums = (sums[0] + csum(dz3 * a_m1), sums[1] + csum(dz3 * a), sums[2] + csum(dz3 * a_p1),
                    sums[3] + csum(dz3))

            dya = dy_ref[rows, CONV_W + p * LANES:CONV_W + (p + 1) * LANES].astype(F32)
            zav = za_ref[rows, cs].astype(F32)
            ov = o_ref[rows, cs].astype(F32)
            sa = _sigmoid(zav)
            d_o = dya * zav * sa
            sdo[p, rows, :] = d_o
            do_ref[rows, cs] = d_o.astype(BF16)
            dza_ref[rows, cs] = (dya * ov * (sa * (1.0 + zav * (1.0 - sa)))).astype(BF16)
            return sums, _split_dot(d_o * ov, _lanes_to_head(p, 1.0))

        def chunk(ci, carry):
            c0 = pl.multiple_of(ci * CHUNK, CHUNK)
            out, delta = [], jnp.zeros((CHUNK, LANES), F32)
            for p in range(N_PAIRS):
                sums, part = one_group(p, c0, carry[p])
                out.append(sums)
                delta = delta + part
            sdelta[0, pl.ds(c0, CHUNK), :] = delta
            return tuple(out)

        z = jnp.zeros((1, LANES), F32)
        all_sums = lax.fori_loop(0, tm // CHUNK, chunk, ((z, z, z, z),) * N_PAIRS)
        for p in range(N_PAIRS):
            for k in range(4):
                small_ref[k:k + 1, p * LANES:(p + 1) * LANES] += all_sums[p][k]
        delta_ref[...] = sdelta[0]
        for dil, wide_ref, stat_ref in zip(VIEW_DILATIONS, (do2_ref, do3_ref), (delta2_ref, delta3_ref)):
            _to_view(sdo, wide_ref, dil, BF16)
            _to_view(sdelta, stat_ref, dil, F32)

    pspec = pl.BlockSpec((tm, CONV_W), lambda i: (i, 0))
    hp, hn = _row_halo_specs(CONV_W, tm, S)
    piece = jax.ShapeDtypeStruct((S, CONV_W), BF16)
    outs = pl.pallas_call(
        body, name="gates_bwd", grid=(n_t,),
        out_shape=[piece] * 5 + [jax.ShapeDtypeStruct((8, CONV_W), F32)]
        + [_view_shape(S, d, ATTN_W, BF16) for d in DILATIONS] + [_view_shape(S, d, LANES, F32) for d in DILATIONS],
        in_specs=[pl.BlockSpec((tm, D_MODEL), lambda i: (i, 0)), hp, hn,
                  pspec, hp, hn, pspec, hp, hn, pspec, hp, hn, pspec, hp, hn, pspec, pspec,
                  pl.BlockSpec((3, CONV_W), lambda i: (0, 0)), pl.BlockSpec((1, CONV_W), lambda i: (0, 0))],
        out_specs=[pspec] * 5 + [pl.BlockSpec((8, CONV_W), lambda i: (0, 0))]
        + [_view_spec(tm, d, ATTN_W) for d in DILATIONS] + [_view_spec(tm, d, LANES) for d in DILATIONS],
        scratch_shapes=[_tile_scratch(tm, ATTN_W), _tile_scratch(tm, LANES)]
        + [pltpu.VMEM((tm + 2 * HALO, CONV_W), BF16)] * 5,
        compiler_params=_params(("arbitrary",), 40),
    )(dy, dy, dy, u, u, u, gb, gb, gb, gc, gc, gc, zc, zc, zc, za, o, conv_w, conv_b.reshape(1, CONV_W))
    return outs[:5], outs[5], outs[6:9], outs[9:12]


def _attn_bwd(g, dil, qv, kv, vv, dov, lv, dv_, rqv, rkv, gain, tables):
    L = qv.shape[0]
    T = min(1024, L)
    n_sub = T // QB + 1
    QW = T + 2 * HALF
    KW = T + 2 * QB

    def body(qm_ref, qp_ref, qn_ref, dom_ref, dop_ref, don_ref, km_ref, kp_ref, kn_ref, vm_ref, vp_ref, vn_ref,
             lm_ref, lp_ref, ln_ref, dm_ref, dp_ref, dn_ref, rqm_ref, rqp_ref, rqn_ref, rk_ref,
             gain_ref, bias_ref,
             dq_ref, dk_ref, dv_ref, ds_ref, a_ref,
             qwin, dowin, kwin, vwin, lwin, dwin, rqwin, dqwin, dkacc, dvacc,
             s_scr, d_scr, p_scr, ds_scr, q2_scr, do2_scr, dq_scr):
        r = pl.program_id(0)
        i = pl.program_id(1)

        @pl.when((r == 0) & (i == 0))
        def _():
            ds_ref[...] = jnp.zeros_like(ds_ref)
            a_ref[...] = jnp.zeros_like(a_ref)

        for win, (pr, mn, nx), h_rows in ((qwin, (qp_ref, qm_ref, qn_ref), HALF),
                                          (dowin, (dop_ref, dom_ref, don_ref), HALF),
                                          (lwin, (lp_ref, lm_ref, ln_ref), HALF),
                                          (dwin, (dp_ref, dm_ref, dn_ref), HALF),
                                          (rqwin, (rqp_ref, rqm_ref, rqn_ref), HALF),
                                          (kwin, (kp_ref, km_ref, kn_ref), QB),
                                          (vwin, (vp_ref, vm_ref, vn_ref), QB)):
            win[0:h_rows] = pr[...]
            win[h_rows:h_rows + T] = mn[...]
            win[h_rows + T:] = nx[...]
        dkacc[0:QB] = jnp.zeros((QB, ATTN_W), F32)
        dvacc[0:QB] = jnp.zeros((QB, ATTN_W), F32)

        lo = _lane((QB, LANES)) < HEAD_DIM
        same_head = _same_head()
        krow = lax.broadcasted_iota(jnp.int32, (KB, 1), 0)
        qcol = _lane((1, 2 * QB)) % QB
        row = lax.broadcasted_iota(jnp.int32, (QB, 1), 0)
        chan_lo = lax.broadcasted_iota(jnp.int32, (LANES, 1), 0) < HEAD_DIM

        cols = [slice(p * LANES, (p + 1) * LANES) for p in range(N_PAIRS)]
        nt = (((1,), (1,)), ((), ()))

        def scores(j):
            r0 = pl.multiple_of(j * QB, QB)
            for k, cs in enumerate(cols):
                qf = qwin[pl.ds(r0, QB), cs].astype(F32) * gain_ref[:, cs]
                q2_scr[k] = jnp.concatenate([jnp.where(lo, qf, 0.0), jnp.where(lo, 0.0, qf)], axis=0).astype(BF16)
                dov = dowin[pl.ds(r0, QB), cs]
                zero = jnp.zeros_like(dov)
                do2_scr[k] = jnp.concatenate([jnp.where(lo, dov, zero), jnp.where(lo, zero, dov)], axis=0)
                s_scr[k] = lax.dot_general(kwin[pl.ds(r0, KB), cs], q2_scr[k], nt, preferred_element_type=F32)
                d_scr[k] = lax.dot_general(vwin[pl.ds(r0, KB), cs], do2_scr[k], nt, preferred_element_type=F32)

        def sub(j):
            r0 = pl.multiple_of(j * QB, QB)
            kpos = i * T - QB + j * QB + krow
            qpos = i * T - HALF + j * QB + qcol
            valid = (kpos >= 0) & (kpos < L) & (qpos >= 0) & (qpos < L)
            owned_t = ((qpos >= i * T) & (qpos < i * T + T)).astype(F32)
            qpos_c = i * T - HALF + j * QB + row
            owned = ((qpos_c >= i * T) & (qpos_c < i * T + T)).astype(F32)
            lse_rows = lwin[pl.ds(r0, QB), :].T
            delta_rows = dwin[pl.ds(r0, QB), :].T
            rq_t = rqwin[pl.ds(r0, QB), :]
            for p in range(N_PAIRS):
                k = p
                bias2 = jnp.concatenate([bias_ref[0, 2 * p], bias_ref[0, 2 * p + 1]], axis=1)
                lse2 = jnp.concatenate([lse_rows[2 * p:2 * p + 1, :], lse_rows[2 * p + 1:2 * p + 2, :]], axis=1)
                delta2 = jnp.concatenate([delta_rows[2 * p:2 * p + 1, :], delta_rows[2 * p + 1:2 * p + 2, :]], axis=1)
                pt = jnp.where(valid, jnp.exp(s_scr[k] + bias2 - lse2), 0.0)
                dst = pt * (d_scr[k] - delta2)
                ds_ref[p] += dst * owned_t
                p_scr[p] = pt.astype(BF16)
                ds_scr[p] = dst.astype(BF16)
            for p, cs in enumerate(cols):
                k = p
                for acc_ref, lhs, rhs in ((dvacc, p_scr[p], do2_scr[k]), (dkacc, ds_scr[p], q2_scr[k])):
                    part = jnp.dot(lhs, rhs, preferred_element_type=F32)
                    acc_ref[pl.ds(r0, QB), cs] += part[0:QB]
                    acc_ref[pl.ds(r0 + QB, QB), cs] = part[QB:KB]
                dq_scr[p] = lax.dot_general(kwin[pl.ds(r0, KB), cs], ds_scr[p], (((0,), (0,)), ((), ())),
                                            preferred_element_type=F32)
            for p, cs in enumerate(cols):
                g2 = gain_ref[:, cs]
                q_f = qwin[pl.ds(r0, QB), cs].astype(F32)
                dqt2 = dq_scr[p]
                dqm2 = jnp.where(chan_lo, dqt2[:, 0:QB], dqt2[:, QB:2 * QB]).T
                a_ref[p] += (dqm2 * q_f * owned).reshape(QB // 8, 8, LANES).sum(axis=0)
                dqh = dqm2 * g2
                mean_c = _split_dot(dqh * q_f, same_head) * (1.0 / HEAD_DIM)
                rq2 = _split_dot(rq_t, _head_to_lanes(p))
                dqwin[pl.ds(r0, QB), cs] = (rq2 * (dqh - q_f * mean_c)).astype(BF16)

        def one_sub(j, carry):
            scores(j)
            sub(j)
            return carry

        lax.fori_loop(0, n_sub, one_sub, 0)
        dq_ref[...] = dqwin[HALF:HALF + T]

        def finish(j, carry):
            r0 = pl.multiple_of(j * QB, QB)
            rk_t = rk_ref[pl.ds(r0, QB), :]
            for p in range(N_PAIRS):
                cs = slice(p * LANES, (p + 1) * LANES)
                dkh = dkacc[pl.ds(QB + r0, QB), cs]
                k_f = km_ref[pl.ds(r0, QB), cs].astype(F32)
                mean_c = _split_dot(dkh * k_f, same_head) * (1.0 / HEAD_DIM)
                rk2 = _split_dot(rk_t, _head_to_lanes(p))
                dk_ref[pl.ds(r0, QB), cs] = (rk2 * (dkh - k_f * mean_c)).astype(BF16)
                dv_ref[pl.ds(r0, QB), cs] = dvacc[pl.ds(QB + r0, QB), cs].astype(BF16)
            return carry

        lax.fori_loop(0, T // QB, finish, 0)

    main = pl.BlockSpec((T, ATTN_W), lambda r, i: (i, r))
    smain = pl.BlockSpec((T, LANES), lambda r, i: (i, r))
    q_prev, q_next = _halo_specs(ATTN_W, HALF, T, L, lambda r: r)
    k_prev, k_next = _halo_specs(ATTN_W, QB, T, L, lambda r: r)
    s_prev, s_next = _halo_specs(LANES, HALF, T, L, lambda r: r)
    piece = jax.ShapeDtypeStruct((L, dil * ATTN_W), BF16)
    return pl.pallas_call(
        body, name=f"attn_bwd_d{dil}", grid=(dil, L // T),
        out_shape=(piece, piece, piece, jax.ShapeDtypeStruct((N_PAIRS, KB, 2 * QB), F32),
                   jax.ShapeDtypeStruct((N_PAIRS, 8, LANES), F32)),
        in_specs=[main, q_prev, q_next, main, q_prev, q_next, main, k_prev, k_next, main, k_prev, k_next,
                  smain, s_prev, s_next, smain, s_prev, s_next, smain, s_prev, s_next, smain,
                  pl.BlockSpec((1, ATTN_W), lambda r, i: (0, 0)),
                  pl.BlockSpec((1, N_HEADS, KB, QB), lambda r, i: (g, 0, 0, 0))],
        out_specs=(main, main, main, pl.BlockSpec((N_PAIRS, KB, 2 * QB), lambda r, i: (0, 0, 0)),
                   pl.BlockSpec((N_PAIRS, 8, LANES), lambda r, i: (0, 0, 0))),
        scratch_shapes=[pltpu.VMEM((QW, ATTN_W), BF16), pltpu.VMEM((QW, ATTN_W), BF16),
                        pltpu.VMEM((KW, ATTN_W), BF16), pltpu.VMEM((KW, ATTN_W), BF16),
                        pltpu.VMEM((QW, LANES), F32), pltpu.VMEM((QW, LANES), F32), pltpu.VMEM((QW, LANES), F32),
                        pltpu.VMEM((QW, ATTN_W), BF16),
                        pltpu.VMEM((KW, ATTN_W), F32), pltpu.VMEM((KW, ATTN_W), F32),
                        pltpu.VMEM((N_PAIRS, KB, 2 * QB), F32), pltpu.VMEM((N_PAIRS, KB, 2 * QB), F32),
                        pltpu.VMEM((N_PAIRS, KB, 2 * QB), BF16), pltpu.VMEM((N_PAIRS, KB, 2 * QB), BF16),
                        pltpu.VMEM((N_PAIRS, 2 * QB, LANES), BF16), pltpu.VMEM((N_PAIRS, 2 * QB, LANES), BF16),
                        pltpu.VMEM((N_PAIRS, LANES, 2 * QB), F32)],
        compiler_params=_params(("arbitrary", "arbitrary"), 56),
    )(qv, qv, qv, dov, dov, dov, kv, kv, kv, vv, vv, vv, lv, lv, lv, dv_, dv_, dv_, rqv, rqv, rqv, rkv, gain, tables)


def _proj_bwd(x, d_out, norm_w, wg, pieces):
    S = x.shape[0]
    tm = 512
    flat = [share for p in pieces for share in p]
    n_t = S // tm

    def body(*refs):
        x_ref, g_ref, nw_ref, wg_ref = refs[:4]
        piece_refs = refs[4:4 + len(flat)]
        gx_ref, dnw_ref, ht_ref, dproj_ref, sview = refs[4 + len(flat):]
        i = pl.program_id(0)

        @pl.when(i == 0)
        def _():
            dnw_ref[...] = jnp.zeros_like(dnw_ref)

        xv = x_ref[...]
        r = lax.rsqrt(jnp.mean(xv * xv, axis=-1, keepdims=True) + EPS)
        xh = xv * r
        nw = nw_ref[...]
        ht_ref[...] = (xh * nw).T.astype(BF16)
        dh = jnp.zeros((tm, D_MODEL), F32)
        k = 0
        for j in range(8):
            shares = pieces[j]
            if len(shares) == 1:
                dp = piece_refs[k][...]
            else:
                dp = jnp.zeros((tm, CONV_W), F32)
                for t, (_, dil) in enumerate(shares):
                    if dil == 1:
                        dp = dp + piece_refs[k + t][...].astype(F32)
                    else:
                        _from_view(piece_refs[k + t], sview, dil)
                        dp = dp + _scratch_value(sview)
                dp = dp.astype(BF16)
            k += len(shares)
            dproj_ref[j] = dp
            dh = dh + lax.dot_general(dp, wg_ref[j], (((1,), (1,)), ((), ())), preferred_element_type=F32)
        dnw_ref[...] += (dh * xh).reshape(tm // 8, 8, D_MODEL).sum(axis=0)
        dxh = dh * nw
        mean_c = jnp.mean(dxh * xh, axis=-1, keepdims=True)
        gx_ref[...] = g_ref[...].astype(F32) + r * (dxh - xh * mean_c)

    tile = pl.BlockSpec((tm, D_MODEL), lambda i: (i, 0))
    return pl.pallas_call(
        body, name="proj_bwd", grid=(n_t,),
        out_shape=(jax.ShapeDtypeStruct((S, D_MODEL), F32), jax.ShapeDtypeStruct((8, D_MODEL), F32),
                   jax.ShapeDtypeStruct((D_MODEL, S), BF16), jax.ShapeDtypeStruct((8, S, CONV_W), BF16)),
        in_specs=[tile, tile, pl.BlockSpec((1, D_MODEL), lambda i: (0, 0)),
                  pl.BlockSpec(wg.shape, lambda i: (0, 0, 0), pipeline_mode=pl.Buffered(1))]
        + [_view_spec(tm, dil, CONV_W) for _, dil in flat],
        out_specs=(tile, pl.BlockSpec((8, D_MODEL), lambda i: (0, 0)),
                   pl.BlockSpec((D_MODEL, tm), lambda i: (0, i)), pl.BlockSpec((8, tm, CONV_W), lambda i: (0, i, 0))),
        scratch_shapes=[_tile_scratch(tm, CONV_W)],
        compiler_params=_params(("arbitrary",), 56),
    )(x, d_out, norm_w.reshape(1, D_MODEL), wg, *[a for a, _ in flat])


def _dw_exchange(ht, dproj, gw_out, small):
    D, S = ht.shape
    tk = 2048
    n_t = S // tk
    me_outer = 4 * lax.axis_index("x") + 2 * lax.axis_index("y") + lax.axis_index("c")
    order = ((me_outer + 1 + jnp.arange(N_DEV, dtype=jnp.int32)) % N_DEV).astype(jnp.int32)

    def body(order_ref, ht_hbm, dp_ref, gout_ref, sm_ref, rin_ref, rout_ref, rsm_ref,
             acc, sbuf, ht_vmem, in_send, in_recv, side_send, side_recv, local_sems, ht_sems):
        s = pl.program_id(0)
        t = pl.program_id(1)
        x, y, c = lax.axis_index("x"), lax.axis_index("y"), lax.axis_index("c")
        me = 4 * x + 2 * y + c

        def ht_copy(j):
            return pltpu.make_async_copy(ht_hbm.at[:, j * tk:(j + 1) * tk], ht_vmem.at[j], ht_sems.at[j])

        def in_copy(k):
            to = (me + 1 + k) % N_DEV
            return pltpu.make_async_remote_copy(
                src_ref=sbuf.at[k % 2], dst_ref=rin_ref.at[me], send_sem=in_send.at[k], recv_sem=in_recv.at[k],
                device_id=(to // 4, (to // 2) % 2, to % 2), device_id_type=MESH)

        def in_landing(k):
            frm = (me + 2 * N_DEV - 1 - k) % N_DEV
            return pltpu.make_async_remote_copy(
                src_ref=sbuf.at[0], dst_ref=rin_ref.at[frm], send_sem=in_send.at[k], recv_sem=in_recv.at[k],
                device_id=(x, y, c), device_id_type=MESH)

        def side_copies():
            local = [pltpu.make_async_copy(gout_ref.at[me], rout_ref.at[me], local_sems.at[0]),
                     pltpu.make_async_copy(sm_ref, rsm_ref.at[me], local_sems.at[1])]
            remote = []
            for k in range(1, N_DEV):
                px = 1 - x if k & 4 else x
                py = 1 - y if k & 2 else y
                pc = 1 - c if k & 1 else c
                peer = 4 * px + 2 * py + pc
                for a, (src, dst) in enumerate(((gout_ref.at[peer], rout_ref.at[me]), (sm_ref, rsm_ref.at[me]))):
                    remote.append(pltpu.make_async_remote_copy(
                        src_ref=src, dst_ref=dst, send_sem=side_send.at[a * 7 + k - 1],
                        recv_sem=side_recv.at[a * 7 + k - 1], device_id=(px, py, pc), device_id_type=MESH))
            return local, remote

        @pl.when((s == 0) & (t == 0))
        def _():
            for j in range(n_t):
                ht_copy(j).start()
            local, remote = side_copies()
            for cp in local + remote:
                cp.start()

        for j in range(n_t):
            @pl.when((s == 0) & (t == j))
            def _(j=j):
                ht_copy(j).wait()

        @pl.when(t == 0)
        def _():
            acc[...] = jnp.zeros_like(acc)

        acc[...] += jnp.dot(ht_vmem[t], dp_ref[0], preferred_element_type=F32)

        for k in range(N_DEV):
            @pl.when((s == k) & (t == n_t - 1))
            def _(k=k):
                if k >= 2:
                    in_copy(k - 2).wait_send()
                sbuf[k % 2] = acc[...].astype(BF16)
                if k < N_DEV - 1:
                    in_copy(k).start()
                else:
                    own = pltpu.make_async_copy(sbuf.at[k % 2], rin_ref.at[me], local_sems.at[2])
                    own.start()
                    in_copy(k - 1).wait_send()
                    for j in range(N_DEV - 1):
                        in_landing(j).wait_recv()
                    local, remote = side_copies()
                    for cp in remote:
                        cp.wait_recv()
                    for cp in remote:
                        cp.wait_send()
                    for cp in local:
                        cp.wait()
                    own.wait()

    hbm = pl.BlockSpec(memory_space=pl.ANY)
    grid_spec = pltpu.PrefetchScalarGridSpec(
        num_scalar_prefetch=1, grid=(N_DEV, n_t),
        in_specs=[hbm,
                  pl.BlockSpec((1, tk, CONV_W), lambda s, t, order_ref: (order_ref[s], t, 0)),
                  hbm, hbm],
        out_specs=(hbm, hbm, hbm),
        scratch_shapes=[pltpu.VMEM((D, CONV_W), F32), pltpu.VMEM((2, D, CONV_W), BF16),
                        pltpu.VMEM((n_t, D, tk), BF16),
                        pltpu.SemaphoreType.DMA((N_DEV - 1,)), pltpu.SemaphoreType.DMA((N_DEV - 1,)),
                        pltpu.SemaphoreType.DMA((14,)), pltpu.SemaphoreType.DMA((14,)),
                        pltpu.SemaphoreType.DMA((3,)), pltpu.SemaphoreType.DMA((n_t,))])
    return pl.pallas_call(
        body, name="dw_exchange", grid_spec=grid_spec,
        out_shape=(jax.ShapeDtypeStruct((N_DEV, D, CONV_W), BF16), jax.ShapeDtypeStruct(gw_out.shape, F32),
                   jax.ShapeDtypeStruct((N_DEV,) + small.shape, F32)),
        compiler_params=_params(("arbitrary", "arbitrary"), 52),
    )(order, ht, dproj, gw_out, small)


def _adamw_math(w, g, m, v):
    m2 = ADAM_B1 * m + (1.0 - ADAM_B1) * g
    v2 = ADAM_B2 * v + (1.0 - ADAM_B2) * (g * g)
    m_hat = m2 / (1.0 - ADAM_B1 ** ADAM_STEP)
    v_hat = v2 / (1.0 - ADAM_B2 ** ADAM_STEP)
    delta = -ADAM_LR * (m_hat / (jnp.sqrt(v_hat) + ADAM_EPS) + ADAM_WD * w)
    return delta, m2, v2


def _adamw_sharded(name, parts, w, m, v, rows):
    R, C = w.shape

    def body(p_ref, w_ref, m_ref, v_ref, g_ref, d_ref, m2_ref, v2_ref):
        g = p_ref[0].astype(F32)
        for s in range(1, N_DEV):
            g = g + p_ref[s].astype(F32)
        g_ref[...] = g
        d_ref[...], m2_ref[...], v2_ref[...] = _adamw_math(w_ref[...], g, m_ref[...], v_ref[...])

    spec = pl.BlockSpec((rows, C), lambda i: (i, 0))
    out = jax.ShapeDtypeStruct((R, C), F32)
    return pl.pallas_call(
        body, name=name, grid=(R // rows,),
        out_shape=(out,) * 4,
        in_specs=[pl.BlockSpec((N_DEV, rows, C), lambda i: (0, i, 0)), spec, spec, spec],
        out_specs=(spec,) * 4,
        compiler_params=_params(("arbitrary",), 40),
    )(parts, w, m, v)


def _adamw_small(parts, w, m, v):
    R = w.shape[0]

    def body(p_ref, w_ref, m_ref, v_ref, g_ref, d_ref, m2_ref, v2_ref):
        g = p_ref[0]
        for s in range(1, N_DEV):
            g = g + p_ref[s]
        g_ref[...] = g
        d_ref[...], m2_ref[...], v2_ref[...] = _adamw_math(w_ref[...], g, m_ref[...], v_ref[...])

    vm = pl.BlockSpec(memory_space=pltpu.VMEM)
    out = jax.ShapeDtypeStruct((R, LANES), F32)
    return pl.pallas_call(
        body, name="adamw_small", out_shape=(out,) * 4, in_specs=[vm] * 4, out_specs=(vm,) * 4,
    )(parts, w, m, v)


def _qk_gain_grads(a_parts, q_norm_w, k_norm_w, bias_g):
    def body(a_ref, qw_ref, kw_ref, bg_ref, out_ref):
        tot = jnp.zeros((8, LANES), F32)
        for g in range(len(DILATIONS)):
            for p in range(N_PAIRS):
                tot = tot + a_ref[g, p]
        col = jnp.sum(tot, axis=0, keepdims=True)
        a64 = col[:, 0:HEAD_DIM] + col[:, HEAD_DIM:LANES]
        out_ref[...] = jnp.zeros_like(out_ref)
        out_ref[0:1, 0:HEAD_DIM] = LOGIT_SCALE * kw_ref[...] * a64
        out_ref[1:2, 0:HEAD_DIM] = LOGIT_SCALE * qw_ref[...] * a64
        for h in range(N_HEADS):
            r0, c0 = 2 + h // 4, (h % 4) * N_BUCKETS
            out_ref[r0:r0 + 1, c0:c0 + N_BUCKETS] = bg_ref[h // 2, h % 2:h % 2 + 1, 0:N_BUCKETS]

    vm = pl.BlockSpec(memory_space=pltpu.VMEM)
    return pl.pallas_call(
        body, name="qk_gain_grads", out_shape=jax.ShapeDtypeStruct((8, LANES), F32),
        in_specs=[vm] * 4, out_specs=vm,
    )(a_parts, q_norm_w.reshape(1, HEAD_DIM), k_norm_w.reshape(1, HEAD_DIM), bias_g)


SMALL_ROWS = 32


def _pack_small(norm_w, conv_b, q_norm_w, k_norm_w, rel_bias, conv_w_full):
    pad64 = lambda a: jnp.pad(a, (0, LANES - HEAD_DIM)).reshape(1, LANES)
    return jnp.concatenate([
        norm_w.reshape(8, LANES), conv_b.reshape(4, LANES), pad64(q_norm_w), pad64(k_norm_w),
        rel_bias.T.reshape(2, LANES), conv_w_full.reshape(12, LANES), jnp.zeros((4, LANES), F32)], axis=0)


def _unpack_small(s):
    return (s[0:8].reshape(D_MODEL), s[8:12].reshape(CONV_W), s[12, 0:HEAD_DIM], s[13, 0:HEAD_DIM],
            s[14:16].reshape(N_HEADS, N_BUCKETS).T, s[16:28].reshape(3, CONV_W))


def kernel(x, norm_w, w_in, conv_w, conv_b, q_norm_w, k_norm_w, rel_bias, w_out, loss_target, m_norm_w, m_w_in, m_conv_w, m_conv_b, m_q_norm_w, m_k_norm_w, m_rel_bias, m_w_out, v_norm_w, v_w_in, v_conv_w, v_conv_b, v_q_norm_w, v_k_norm_w, v_rel_bias, v_w_out):
    S = x.shape[1]
    x2 = x.reshape(S, D_MODEL)
    tgt = loss_target.reshape(S, D_MODEL)
    me = 4 * lax.axis_index("x") + 2 * lax.axis_index("y") + lax.axis_index("c")
    shard_w = CONV_W // N_DEV

    cw_pad = jnp.pad(conv_w, ((0, 5), (0, LANES - shard_w)))
    wg, wo_g, cw_g = _ag_weights(w_in, w_out, cw_pad)
    wo = wo_g.reshape(D_MODEL, D_MODEL)
    conv_w_full = cw_g[:, 0:3, 0:shard_w].transpose(1, 0, 2).reshape(3, CONV_W)

    bmat_t = jnp.asarray(np.stack([_bucket_matrix(d).T for d in DILATIONS]))
    tables = _bias_tables(rel_bias, bmat_t)
    gain = jnp.tile(q_norm_w * k_norm_w * LOGIT_SCALE, N_HEADS).reshape(1, ATTN_W)

    u, gb, gc, zc, za, q_hat, k_hat, v, rq, rk = _fwd_proj(x2, norm_w, wg)
    o_parts, lse_parts = [], []
    for g, dil in enumerate(DILATIONS):
        o_g, lse_g = _attn_fwd(g, dil, q_hat[g], k_hat[g], v[g], gain, tables)
        o_parts.append(o_g)
        lse_parts.append(lse_g)
    y, o, lse = _combine_gates(o_parts, lse_parts, u, gb, gc, zc, za, conv_w_full, conv_b)
    d_out, dy, gwo_part, loss_part = _out_proj(y, x2, tgt, wo)
    loss_mine = jnp.sum(loss_part) * (0.5 / D_MODEL)

    (du, dgb, dgc, dzc, dza), conv_small, d_o, delta = _gates_bwd(dy, u, gb, gc, zc, za, o, conv_w_full, conv_b)
    dqs, dks, dvs, ds_parts, a_parts = [], [], [], [], []
    for g, dil in enumerate(DILATIONS):
        dq, dk, dv, ds_sum, a_sum = _attn_bwd(g, dil, q_hat[g], k_hat[g], v[g], d_o[g], lse[g], delta[g], rq[g], rk[g],
                                              gain, tables)
        dqs.append((dq, dil))
        dks.append((dk, dil))
        dvs.append((dv, dil))
        ds_parts.append(ds_sum)
        a_parts.append(a_sum)
    grad_x, gnw_part, h_t, dproj = _proj_bwd(
        x2, d_out, norm_w, wg, [[(du, 1)], [(dgb, 1)], [(dgc, 1)], [(dzc, 1)], dqs, dks, dvs, [(dza, 1)]])
    bias_g = _bias_grad(jnp.stack(ds_parts), bmat_t)
    qk_small = _qk_gain_grads(jnp.stack(a_parts), q_norm_w, k_norm_w, bias_g)

    small_part = jnp.concatenate([
        gnw_part.sum(axis=0).reshape(8, LANES), conv_small[3].reshape(4, LANES), qk_small[0:4],
        conv_small[0:3].reshape(12, LANES), jnp.zeros((4, LANES), F32).at[0, 0].set(loss_mine)], axis=0)
    r_in, r_out, r_small = _dw_exchange(h_t, dproj, gwo_part.reshape(N_DEV, D_MODEL // N_DEV, D_MODEL), small_part)

    g_win, d_win, m_win, v_win = _adamw_sharded("adamw_w_in", r_in, w_in, m_w_in, v_w_in, 128)
    g_wo, d_wo, m_wo, v_wo = _adamw_sharded("adamw_w_out", r_out, w_out, m_w_out, v_w_out, 32)

    def full_conv(a):
        return jnp.zeros((3, CONV_W), F32).at[:, 0:shard_w].set(a)

    packs = [_pack_small(nw_, cb_, qw_, kw_, rb_, full_conv(cw_)) for nw_, cb_, qw_, kw_, rb_, cw_ in (
        (norm_w, conv_b, q_norm_w, k_norm_w, rel_bias, conv_w),
        (m_norm_w, m_conv_b, m_q_norm_w, m_k_norm_w, m_rel_bias, m_conv_w),
        (v_norm_w, v_conv_b, v_q_norm_w, v_k_norm_w, v_rel_bias, v_conv_w))]
    r_small_mine = r_small.at[:, 16:28, :].set(
        jnp.pad(lax.dynamic_slice_in_dim(r_small[:, 16:28, :].reshape(N_DEV, 3, CONV_W), me * shard_w, shard_w, axis=2),
                ((0, 0), (0, 0), (0, CONV_W - shard_w))).reshape(N_DEV, 12, LANES))
    outs_small = _adamw_small(r_small_mine, *packs)
    g_s, d_s, m_s, v_s = [_unpack_small(t) for t in outs_small]
    loss = outs_small[0][28, 0]

    def leaves(small, big_in, big_out):
        nw_, cb_, qw_, kw_, rb_, cwf = small
        return (nw_, big_in, cwf[:, 0:shard_w], cb_, qw_, kw_, rb_, big_out)

    return (loss, grad_x.reshape(x.shape),
            *leaves(g_s, g_win, g_wo), *leaves(d_s, d_win, d_wo), *leaves(m_s, m_win, m_wo), *leaves(v_s, v_win, v_wo))
```

```python
import math

import numpy as np
import jax
import jax.numpy as jnp
from jax import lax
from jax.experimental import pallas as pl
from jax.experimental.pallas import tpu as pltpu

F32 = jnp.float32
BF16 = jnp.bfloat16

N_DEV = 8
D_MODEL = 1024
CONV_W = 512
ATTN_W = 512
N_HEADS = 8
HEAD_DIM = 64
N_PAIRS = N_HEADS // 2
LANES = 128
HALF = 64
QB = 128
KB = QB + 2 * HALF
DILATIONS = (1, 4, 16)
N_BUCKETS = 32
MAX_DISTANCE = 1024
EPS = 1e-6
NEG = -1e30
LOGIT_SCALE = HEAD_DIM ** -0.5

ADAM_LR = 0.001
ADAM_B1 = 0.9
ADAM_B2 = 0.999
ADAM_EPS = 1e-08
ADAM_WD = 0.01
ADAM_STEP = 10

MESH = pl.DeviceIdType.MESH
MIB = 1024 * 1024


def _params(semantics, vmem_mib):
    return pltpu.CompilerParams(dimension_semantics=semantics, vmem_limit_bytes=vmem_mib * MIB)


def _lane(shape):
    return lax.broadcasted_iota(jnp.int32, shape, len(shape) - 1)


def _sigmoid(z):
    return 1.0 / (1.0 + jnp.exp(-z))


def _split_dot(x, w):
    hi = x.astype(BF16)
    lo = (x - hi.astype(F32)).astype(BF16)
    return jnp.dot(hi, w, preferred_element_type=F32) + jnp.dot(lo, w, preferred_element_type=F32)


def _same_head():
    r = lax.broadcasted_iota(jnp.int32, (LANES, LANES), 0) // HEAD_DIM
    c = lax.broadcasted_iota(jnp.int32, (LANES, LANES), 1) // HEAD_DIM
    return (r == c).astype(BF16)


def _head_to_lanes(p):
    r = lax.broadcasted_iota(jnp.int32, (LANES, LANES), 0)
    c = lax.broadcasted_iota(jnp.int32, (LANES, LANES), 1) // HEAD_DIM
    return (r == 2 * p + c).astype(BF16)


def _lanes_to_head(p, scale):
    r = lax.broadcasted_iota(jnp.int32, (LANES, LANES), 0) // HEAD_DIM
    c = lax.broadcasted_iota(jnp.int32, (LANES, LANES), 1)
    return jnp.where(c == 2 * p + r, scale, 0.0).astype(BF16)


VIEW_DILATIONS = DILATIONS[1:]


def _view_shape(S, dil, width, dtype):
    return jax.ShapeDtypeStruct((S // dil, dil * width), dtype)


def _view_spec(tm, dil, width):
    return pl.BlockSpec((tm // dil, dil * width), lambda i: (i, 0))


def _tile_scratch(tm, width):
    return pltpu.VMEM((width // LANES, tm, LANES), F32)


def _scratch_value(scr_ref):
    n = scr_ref.shape[0]
    return scr_ref[0] if n == 1 else jnp.concatenate([scr_ref[c] for c in range(n)], axis=1)


def _to_view(scr_ref, out_ref, dil, dtype):
    n, tm, _ = scr_ref.shape
    for r in range(dil):
        for c in range(n):
            col = (r * n + c) * LANES
            out_ref[:, col:col + LANES] = scr_ref[c, pl.ds(r, tm // dil, stride=dil), :].astype(dtype)


def _from_view(blk_ref, scr_ref, dil):
    n, tm, _ = scr_ref.shape
    for r in range(dil):
        for c in range(n):
            col = (r * n + c) * LANES
            scr_ref[c, pl.ds(r, tm // dil, stride=dil), :] = blk_ref[:, col:col + LANES].astype(F32)


def _bucket_matrix(dilation):
    rel = np.arange(KB)[None, :] - HALF - np.arange(QB)[:, None]
    band = np.abs(rel) <= HALF
    dist = np.clip(rel, -HALF, HALF) * dilation
    half_b = N_BUCKETS // 2
    max_exact = half_b // 2
    ret = np.where(dist > 0, half_b, 0)
    n = np.abs(dist)
    nf = np.maximum(n, 1).astype(np.float32)
    large = max_exact + (np.log(nf / np.float32(max_exact)) / np.float32(math.log(MAX_DISTANCE / max_exact))
                         * np.float32(half_b - max_exact)).astype(np.int32)
    large = np.minimum(large, half_b - 1)
    bucket = ret + np.where(n < max_exact, n, large)
    return np.where(band, bucket, -1).astype(np.int32)


def _ag_weights(w_in, w_out, cw_pad):
    n_arr = 3

    def body(win_ref, wout_ref, cw_ref, gin_ref, gout_ref, gcw_ref, send_sems, recv_sems):
        x, y, c = lax.axis_index("x"), lax.axis_index("y"), lax.axis_index("c")
        me = (x, y, c)
        sibling = (x, y, 1 - c)
        chips = [(1 - x, y), (x, 1 - y), (1 - x, 1 - y)]
        arrays = (gin_ref, gout_ref, gcw_ref)

        def slot(px, py, pc):
            return 4 * px + 2 * py + pc

        gin_ref[slot(*me)] = win_ref[...].astype(BF16)
        gout_ref[slot(*me)] = wout_ref[...].astype(BF16)
        gcw_ref[slot(*me)] = cw_ref[...]

        def copy(a, k, block, to):
            ref = arrays[a].at[slot(*block)]
            return pltpu.make_async_remote_copy(
                src_ref=ref, dst_ref=ref, send_sem=send_sems.at[a * 7 + k], recv_sem=recv_sems.at[a * 7 + k],
                device_id=to, device_id_type=MESH)

        first = [copy(a, 0, me, sibling) for a in range(n_arr)]
        for j, chip in enumerate(chips):
            first += [copy(a, 1 + j, me, (*chip, c)) for a in range(n_arr)]
        for cp in first:
            cp.start()
        passed = []
        for j, chip in enumerate(chips):
            for a in range(n_arr):
                copy(a, 1 + j, (*chip, c), me).wait_recv()
            for a in range(n_arr):
                cp = copy(a, 4 + j, (*chip, c), sibling)
                cp.start()
                passed.append(cp)
        for a in range(n_arr):
            copy(a, 0, sibling, me).wait_recv()
        for j, chip in enumerate(chips):
            for a in range(n_arr):
                copy(a, 4 + j, (*chip, 1 - c), me).wait_recv()
        for cp in first + passed:
            cp.wait_send()

    vm = pl.BlockSpec(memory_space=pltpu.VMEM)
    return pl.pallas_call(
        body, name="ag_weights",
        out_shape=(jax.ShapeDtypeStruct((N_DEV,) + w_in.shape, BF16),
                   jax.ShapeDtypeStruct((N_DEV,) + w_out.shape, BF16),
                   jax.ShapeDtypeStruct((N_DEV,) + cw_pad.shape, F32)),
        in_specs=[vm, vm, vm], out_specs=(vm, vm, vm),
        scratch_shapes=[pltpu.SemaphoreType.DMA((n_arr * 7,)), pltpu.SemaphoreType.DMA((n_arr * 7,))],
        compiler_params=pltpu.CompilerParams(vmem_limit_bytes=40 * MIB),
    )(w_in, w_out, cw_pad)


def _fwd_proj(x, norm_w, wg):
    S = x.shape[0]
    tm = 512

    def body(x_ref, nw_ref, wg_ref, u_ref, gb_ref, gc_ref, zc_ref, za_ref, *rest):
        q_refs, k_refs, v_refs, rq_refs, rk_refs = (rest[3 * n:3 * n + 3] for n in range(5))
        scr, rscr = rest[15:]
        xv = x_ref[...]
        r = lax.rsqrt(jnp.mean(xv * xv, axis=-1, keepdims=True) + EPS)
        h = (xv * r * nw_ref[...]).astype(BF16)
        lane = _lane((tm, LANES))
        lo = lane < HEAD_DIM
        plain = {0: u_ref, 1: gb_ref, 2: gc_ref, 3: zc_ref, 7: za_ref}
        normed = {4: (q_refs, rq_refs), 5: (k_refs, rk_refs)}

        def emit(src, refs, dtype):
            refs[0][...] = _scratch_value(src).astype(dtype)
            for dil, ref in zip(VIEW_DILATIONS, refs[1:]):
                _to_view(src, ref, dil, dtype)

        acc_next = jnp.dot(h, wg_ref[0], preferred_element_type=F32)
        for j in range(8):
            acc = acc_next
            if j < 7:
                acc_next = jnp.dot(h, wg_ref[j + 1], preferred_element_type=F32)
            if j in plain:
                plain[j][...] = acc.astype(BF16)
                continue
            if j == 6:
                for p in range(N_PAIRS):
                    scr[p] = acc[:, p * LANES:(p + 1) * LANES]
                emit(scr, v_refs, BF16)
                continue
            out_refs, r_refs = normed[j]
            r_tile = jnp.zeros((tm, LANES), F32)
            for p in range(N_PAIRS):
                blk = acc[:, p * LANES:(p + 1) * LANES]
                sq = blk * blk
                s_lo = jnp.sum(jnp.where(lo, sq, 0.0), axis=-1, keepdims=True)
                s_hi = jnp.sum(jnp.where(lo, 0.0, sq), axis=-1, keepdims=True)
                r_lo = lax.rsqrt(s_lo * (1.0 / HEAD_DIM) + EPS)
                r_hi = lax.rsqrt(s_hi * (1.0 / HEAD_DIM) + EPS)
                scr[p] = blk * jnp.where(lo, r_lo, r_hi)
                r_tile = jnp.where(lane == 2 * p, r_lo, r_tile)
                r_tile = jnp.where(lane == 2 * p + 1, r_hi, r_tile)
            rscr[0] = r_tile
            emit(scr, out_refs, BF16)
            emit(rscr, r_refs, F32)

    piece = jax.ShapeDtypeStruct((S, CONV_W), BF16)
    pspec = pl.BlockSpec((tm, CONV_W), lambda i: (i, 0))
    wide = [_view_shape(S, d, ATTN_W, BF16) for d in DILATIONS]
    wide_specs = [_view_spec(tm, d, ATTN_W) for d in DILATIONS]
    stat = [_view_shape(S, d, LANES, F32) for d in DILATIONS]
    stat_specs = [_view_spec(tm, d, LANES) for d in DILATIONS]
    outs = pl.pallas_call(
        body, name="fwd_proj", grid=(S // tm,),
        out_shape=[piece] * 5 + wide * 3 + stat * 2,
        in_specs=[pl.BlockSpec((tm, D_MODEL), lambda i: (i, 0)),
                  pl.BlockSpec((1, D_MODEL), lambda i: (0, 0)),
                  pl.BlockSpec(wg.shape, lambda i: (0, 0, 0), pipeline_mode=pl.Buffered(1))],
        out_specs=[pspec] * 5 + wide_specs * 3 + stat_specs * 2,
        scratch_shapes=[_tile_scratch(tm, ATTN_W), _tile_scratch(tm, LANES)],
        compiler_params=_params(("arbitrary",), 48),
    )(x, norm_w.reshape(1, D_MODEL), wg)
    u, gb, gc, zc, za = outs[:5]
    q_hat, k_hat, v, rq, rk = (outs[5 + 3 * n:8 + 3 * n] for n in range(5))
    return u, gb, gc, zc, za, q_hat, k_hat, v, rq, rk


def _bias_tables(rel_bias, bmat_t):
    def body(rb_ref, b_ref, out_ref):
        h = pl.program_id(1)
        b = b_ref[0]
        t = jnp.full((KB, QB), NEG, F32)
        for bk in range(N_BUCKETS):
            t = jnp.where(b == bk, rb_ref[bk, h], t)
        out_ref[0, 0] = t

    return pl.pallas_call(
        body, name="bias_tables", grid=(len(DILATIONS), N_HEADS),
        out_shape=jax.ShapeDtypeStruct((len(DILATIONS), N_HEADS, KB, QB), F32),
        in_specs=[pl.BlockSpec(memory_space=pltpu.SMEM),
                  pl.BlockSpec((1, KB, QB), lambda g, h: (g, 0, 0))],
        out_specs=pl.BlockSpec((1, 1, KB, QB), lambda g, h: (g, h, 0, 0)),
        compiler_params=_params(("arbitrary", "arbitrary"), 16),
    )(rel_bias, bmat_t)


def _bias_grad(ds_acc, bmat_t):
    present = [sorted(set(_bucket_matrix(d).ravel().tolist()) - {-1}) for d in DILATIONS]

    def body(ds_ref, b_ref, out_ref):
        lane = _lane((1, LANES))
        row = lax.broadcasted_iota(jnp.int32, (8, LANES), 0)
        out = jnp.zeros((8, LANES), F32)
        for e in range(2):
            vec = jnp.zeros((1, LANES), F32)
            for g in range(len(DILATIONS)):
                b = b_ref[g]
                ds = ds_ref[g, 0, :, e * QB:(e + 1) * QB]
                for bk in present[g]:
                    s = jnp.sum(jnp.where(b == bk, ds, 0.0), axis=-1, keepdims=True)
                    s = jnp.sum(s, axis=0, keepdims=True)
                    vec = vec + jnp.where(lane == bk, s, 0.0)
            out = jnp.where(row == e, vec, out)
        out_ref[0] = out

    return pl.pallas_call(
        body, name="bias_grad", grid=(N_PAIRS,),
        out_shape=jax.ShapeDtypeStruct((N_PAIRS, 8, LANES), F32),
        in_specs=[pl.BlockSpec((len(DILATIONS), 1, KB, 2 * QB), lambda p: (0, p, 0, 0)),
                  pl.BlockSpec((len(DILATIONS), KB, QB), lambda p: (0, 0, 0))],
        out_specs=pl.BlockSpec((1, 8, LANES), lambda p: (p, 0, 0)),
        compiler_params=_params(("arbitrary",), 16),
    )(ds_acc, bmat_t)


def _halo_specs(width, rows, T, L, cols_of):
    per = T // rows
    last = L // rows - 1
    prev = pl.BlockSpec((rows, width), lambda r, i: (jnp.maximum(i * per - 1, 0), cols_of(r)))
    nxt = pl.BlockSpec((rows, width), lambda r, i: (jnp.minimum((i + 1) * per, last), cols_of(r)))
    return prev, nxt


def _attn_fwd(g, dil, qv, kv, vv, gain, tables):
    L = qv.shape[0]
    T = min(1024, L)
    n_sub = T // QB

    def body(q_ref, km_ref, kp_ref, kn_ref, vm_ref, vp_ref, vn_ref, gain_ref, bias_ref, o_ref, lse_ref,
             kwin, vwin, s_scr, p_scr):
        i = pl.program_id(1)
        kwin[0:HALF] = kp_ref[...]
        kwin[HALF:HALF + T] = km_ref[...]
        kwin[HALF + T:] = kn_ref[...]
        vwin[0:HALF] = vp_ref[...]
        vwin[HALF:HALF + T] = vm_ref[...]
        vwin[HALF + T:] = vn_ref[...]
        lo = _lane((QB, LANES)) < HEAD_DIM
        krow = lax.broadcasted_iota(jnp.int32, (KB, 1), 0)
        chan_lo = lax.broadcasted_iota(jnp.int32, (LANES, 1), 0) < HEAD_DIM
        hrow = lax.broadcasted_iota(jnp.int32, (LANES, QB), 0)

        def sub(j, carry):
            r0 = pl.multiple_of(j * QB, QB)
            kpos = i * T + j * QB - HALF + krow
            kvalid = (kpos >= 0) & (kpos < L)
            lse_rows = jnp.zeros((LANES, QB), F32)
            cols = [slice(p * LANES, (p + 1) * LANES) for p in range(N_PAIRS)]
            for p, cs in enumerate(cols):
                qf = q_ref[pl.ds(r0, QB), cs].astype(F32) * gain_ref[:, cs]
                q2 = jnp.concatenate([jnp.where(lo, qf, 0.0), jnp.where(lo, 0.0, qf)], axis=0).astype(BF16)
                s_scr[p] = lax.dot_general(kwin[pl.ds(r0, KB), cs], q2, (((1,), (1,)), ((), ())),
                                           preferred_element_type=F32)
            inv_l = []
            for p in range(N_PAIRS):
                bias2 = jnp.concatenate([bias_ref[0, 2 * p], bias_ref[0, 2 * p + 1]], axis=1)
                logits = jnp.where(kvalid, s_scr[p] + bias2, NEG)
                m = jnp.max(logits, axis=0, keepdims=True)
                pt = jnp.exp(logits - m)
                l = jnp.sum(pt, axis=0, keepdims=True)
                p_scr[p] = pt.astype(BF16)
                inv_l.append(1.0 / l)
                lse2 = m + jnp.log(l)
                lse_rows = jnp.where(hrow == 2 * p, lse2[:, 0:QB], lse_rows)
                lse_rows = jnp.where(hrow == 2 * p + 1, lse2[:, QB:2 * QB], lse_rows)
            for p, cs in enumerate(cols):
                s_scr[p, 0:LANES, :] = lax.dot_general(vwin[pl.ds(r0, KB), cs], p_scr[p], (((0,), (0,)), ((), ())),
                                                       preferred_element_type=F32) * inv_l[p]
            for p, cs in enumerate(cols):
                ot2 = s_scr[p, 0:LANES, :]
                ot = jnp.where(chan_lo, ot2[:, 0:QB], ot2[:, QB:2 * QB])
                o_ref[pl.ds(r0, QB), cs] = ot.T.astype(BF16)
            lse_ref[pl.ds(r0, QB), :] = lse_rows.T
            return carry

        lax.fori_loop(0, n_sub, sub, 0)

    main = pl.BlockSpec((T, ATTN_W), lambda r, i: (i, r))
    prev, nxt = _halo_specs(ATTN_W, HALF, T, L, lambda r: r)
    o_g, lse_g = pl.pallas_call(
        body, name=f"attn_fwd_d{dil}", grid=(dil, L // T),
        out_shape=(jax.ShapeDtypeStruct((L, dil * ATTN_W), BF16), jax.ShapeDtypeStruct((L, dil * LANES), F32)),
        in_specs=[main, main, prev, nxt, main, prev, nxt,
                  pl.BlockSpec((1, ATTN_W), lambda r, i: (0, 0)),
                  pl.BlockSpec((1, N_HEADS, KB, QB), lambda r, i: (g, 0, 0, 0))],
        out_specs=(main, pl.BlockSpec((T, LANES), lambda r, i: (i, r))),
        scratch_shapes=[pltpu.VMEM((T + 2 * HALF, ATTN_W), BF16), pltpu.VMEM((T + 2 * HALF, ATTN_W), BF16),
                        pltpu.VMEM((N_PAIRS, KB, 2 * QB), F32), pltpu.VMEM((N_PAIRS, KB, 2 * QB), BF16)],
        compiler_params=_params(("arbitrary", "arbitrary"), 40),
    )(qv, kv, kv, kv, vv, vv, vv, gain, tables)
    return o_g, lse_g


HALO = 16
CHUNK = 128


def _row_halo_specs(width, tm, S, col=0):
    rows = HALO
    per = tm // rows
    last = S // rows - 1
    prev = pl.BlockSpec((rows, width), lambda i: (jnp.maximum(i * per - 1, 0), col))
    nxt = pl.BlockSpec((rows, width), lambda i: (jnp.minimum((i + 1) * per, last), col))
    return prev, nxt


def _shift_rows(a, prev_row, next_row, tm):
    row = lax.broadcasted_iota(jnp.int32, a.shape, 0)
    a_m1 = jnp.where(row == 0, prev_row, pltpu.roll(a, 1, 0))
    a_p1 = jnp.where(row == tm - 1, next_row, pltpu.roll(a, tm - 1, 0))
    return a_m1, a_p1


def _combine_gates(o_parts, lse_parts, u, gb, gc, zc, za, conv_w, conv_b):
    S = u.shape[0]
    tm = 512
    n_t = S // tm

    def body(o1, o2, o3, l1, l2, l3, u_ref, up_ref, un_ref, gb_ref, gc_ref, gcp_ref, gcn_ref, zc_ref, za_ref,
             cw_ref, cb_ref, y_ref, o_ref, lse_ref, lse2_ref, lse3_ref, so2, so3, sl2, sl3, slse):
        i = pl.program_id(0)
        for blk, scr, dil in ((o2, so2, DILATIONS[1]), (o3, so3, DILATIONS[2]),
                              (l2, sl2, DILATIONS[1]), (l3, sl3, DILATIONS[2])):
            _from_view(blk, scr, dil)
        ls = [l1[...], sl2[0], sl3[0]]
        lmax = jnp.maximum(jnp.maximum(ls[0], ls[1]), ls[2])
        es = [jnp.exp(l - lmax) for l in ls]
        den = es[0] + es[1] + es[2]
        slse[0] = lmax + jnp.log(den)
        lse_ref[...] = slse[0]
        _to_view(slse, lse2_ref, DILATIONS[1], F32)
        _to_view(slse, lse3_ref, DILATIONS[2], F32)
        inv = 1.0 / den
        ws = [e * inv for e in es]
        for p in range(N_PAIRS):
            cs = slice(p * LANES, (p + 1) * LANES)
            acc = jnp.zeros((tm, LANES), F32)
            spread = _head_to_lanes(p)
            for w, o_g in zip(ws, (o1[:, cs].astype(F32), so2[p], so3[p])):
                acc = acc + _split_dot(w, spread) * o_g
            o_ref[:, cs] = acc.astype(BF16)
            za = za_ref[:, cs].astype(F32)
            y_ref[:, CONV_W + p * LANES:CONV_W + (p + 1) * LANES] = (acc * za * _sigmoid(za)).astype(BF16)
        a = gc_ref[...].astype(F32) * u_ref[...].astype(F32)
        a_prev = gcp_ref[15:16, :].astype(F32) * up_ref[15:16, :].astype(F32) * (i > 0).astype(F32)
        a_next = gcn_ref[0:1, :].astype(F32) * un_ref[0:1, :].astype(F32) * (i < n_t - 1).astype(F32)
        a_m1, a_p1 = _shift_rows(a, a_prev, a_next, tm)
        z3 = cw_ref[0:1, :] * a_m1 + cw_ref[1:2, :] * a + cw_ref[2:3, :] * a_p1
        zc = zc_ref[...].astype(F32)
        y_ref[:, 0:CONV_W] = (gb_ref[...].astype(F32) * (z3 + cb_ref[...]) * zc * _sigmoid(zc)).astype(BF16)

    pspec = pl.BlockSpec((tm, CONV_W), lambda i: (i, 0))
    hp, hn = _row_halo_specs(CONV_W, tm, S)
    wide_specs = [_view_spec(tm, d, ATTN_W) for d in DILATIONS]
    stat_specs = [_view_spec(tm, d, LANES) for d in DILATIONS]
    outs = pl.pallas_call(
        body, name="combine_gates", grid=(n_t,),
        out_shape=[jax.ShapeDtypeStruct((S, D_MODEL), BF16), jax.ShapeDtypeStruct((S, ATTN_W), BF16)]
        + [_view_shape(S, d, LANES, F32) for d in DILATIONS],
        in_specs=wide_specs + stat_specs + [pspec, hp, hn, pspec, pspec, hp, hn, pspec, pspec,
                                            pl.BlockSpec((3, CONV_W), lambda i: (0, 0)),
                                            pl.BlockSpec((1, CONV_W), lambda i: (0, 0))],
        out_specs=[pl.BlockSpec((tm, D_MODEL), lambda i: (i, 0)), pspec] + stat_specs,
        scratch_shapes=[_tile_scratch(tm, ATTN_W), _tile_scratch(tm, ATTN_W),
                        _tile_scratch(tm, LANES), _tile_scratch(tm, LANES), _tile_scratch(tm, LANES)],
        compiler_params=_params(("arbitrary",), 40),
    )(*o_parts, *lse_parts, u, u, u, gb, gc, gc, gc, zc, za, conv_w, conv_b.reshape(1, CONV_W))
    return outs[0], outs[1], outs[2:]


def _out_proj(y, x, target, wo):
    S = x.shape[0]
    tm = 512

    def body(y_ref, x_ref, t_ref, wo_ref, g_ref, dy_ref, dwo_ref, loss_ref):
        i = pl.program_id(0)

        @pl.when(i == 0)
        def _():
            dwo_ref[...] = jnp.zeros_like(dwo_ref)
            loss_ref[...] = jnp.zeros_like(loss_ref)

        yv = y_ref[...]
        wo_v = wo_ref[...]
        err = x_ref[...] + jnp.dot(yv, wo_v, preferred_element_type=F32) - t_ref[...]
        e2 = (err * err).reshape(tm // 8, 8, D_MODEL).sum(axis=0)
        part = e2[:, 0:LANES]
        for k in range(1, D_MODEL // LANES):
            part = part + e2[:, k * LANES:(k + 1) * LANES]
        loss_ref[...] += part
        gv = (err * (1.0 / D_MODEL)).astype(BF16)
        g_ref[...] = gv
        dy_ref[...] = lax.dot_general(gv, wo_v, (((1,), (1,)), ((), ())), preferred_element_type=F32).astype(BF16)
        dwo_ref[...] += lax.dot_general(yv, gv, (((0,), (0,)), ((), ())), preferred_element_type=F32)

    tile = pl.BlockSpec((tm, D_MODEL), lambda i: (i, 0))
    return pl.pallas_call(
        body, name="out_proj", grid=(S // tm,),
        out_shape=(jax.ShapeDtypeStruct((S, D_MODEL), BF16), jax.ShapeDtypeStruct((S, D_MODEL), BF16),
                   jax.ShapeDtypeStruct((D_MODEL, D_MODEL), F32), jax.ShapeDtypeStruct((8, LANES), F32)),
        in_specs=[tile, tile, tile, pl.BlockSpec((D_MODEL, D_MODEL), lambda i: (0, 0), pipeline_mode=pl.Buffered(1))],
        out_specs=(tile, tile, pl.BlockSpec((D_MODEL, D_MODEL), lambda i: (0, 0)),
                   pl.BlockSpec((8, LANES), lambda i: (0, 0))),
        compiler_params=_params(("arbitrary",), 48),
    )(y, x, target, wo)


def _gates_bwd(dy, u, gb, gc, zc, za, o, conv_w, conv_b):
    S = u.shape[0]
    tm = 512
    n_t = S // tm

    def body(dy_ref, dyp_ref, dyn_ref, u_ref, up_ref, un_ref, gb_ref, gbp_ref, gbn_ref, gc_ref, gcp_ref, gcn_ref,
             zc_ref, zcp_ref, zcn_ref, za_ref, o_ref, cw_ref, cb_ref,
             d5_ref, small_ref,
             do_ref, do2_ref, do3_ref, delta_ref, delta2_ref, delta3_ref, sdo, sdelta,
             u_ext, gc_ext, gb_ext, zc_ext, dy_ext):
        i = pl.program_id(0)

        @pl.when(i == 0)
        def _():
            small_ref[...] = jnp.zeros_like(small_ref)

        exts = ((u_ext, up_ref, u_ref, un_ref), (gc_ext, gcp_ref, gc_ref, gcn_ref), (gb_ext, gbp_ref, gb_ref, gbn_ref),
                (zc_ext, zcp_ref, zc_ref, zcn_ref))
        for ext, pr, mn, nx in exts:
            ext[0:HALO] = pr[...]
            ext[HALO:HALO + tm] = mn[...]
            ext[HALO + tm:] = nx[...]
        dy_ext[0:HALO] = dyp_ref[...]
        dy_ext[HALO:HALO + tm] = dy_ref[:, 0:CONV_W]
        dy_ext[HALO + tm:] = dyn_ref[...]
        zero_halo = jnp.zeros((HALO, CONV_W), BF16)

        @pl.when(i == 0)
        def _():
            u_ext[0:HALO] = zero_halo
            dy_ext[0:HALO] = zero_halo

        @pl.when(i == n_t - 1)
        def _():
            u_ext[HALO + tm:] = zero_halo
            dy_ext[HALO + tm:] = zero_halo

        wide_rows = CHUNK + 2 * HALO
        mid = slice(HALO, HALO + CHUNK)
        csum = lambda t: jnp.sum(t, axis=0, keepdims=True)

        def one_group(p, c0, sums):
            cs = slice(p * LANES, (p + 1) * LANES)
            w0, w1, w2, cb = cw_ref[0:1, cs], cw_ref[1:2, cs], cw_ref[2:3, cs], cb_ref[:, cs]
            rows = pl.ds(c0, CHUNK)
            wide = pl.ds(c0, wide_rows)
            u_e, gc_e, gb_e, zc_e, dy_e = (r[wide, cs].astype(F32) for r in (u_ext, gc_ext, gb_ext, zc_ext, dy_ext))
            a_e = gc_e * u_e
            sz_e = _sigmoid(zc_e)
            dz3_e = dy_e * zc_e * sz_e * gb_e
            a, a_m1, a_p1 = a_e[mid], pltpu.roll(a_e, 1, 0)[mid], pltpu.roll(a_e, wide_rows - 1, 0)[mid]
            dz3, dz3_m1, dz3_p1 = dz3_e[mid], pltpu.roll(dz3_e, 1, 0)[mid], pltpu.roll(dz3_e, wide_rows - 1, 0)[mid]
            uv, gcv, gbv, zcv, dyc, sz = u_e[mid], gc_e[mid], gb_e[mid], zc_e[mid], dy_e[mid], sz_e[mid]
            cvp = w0 * a_m1 + w1 * a + w2 * a_p1 + cb
            dcv = dyc * zcv * sz
            d5_ref[3, rows, cs] =(dyc * gbv * cvp * (sz * (1.0 + zcv * (1.0 - sz)))).astype(BF16)
            d5_ref[1, rows, cs] =(dcv * cvp).astype(BF16)
            da = w0 * dz3_p1 + w1 * dz3 + w2 * dz3_m1
            d5_ref[0, rows, cs] =(da * gcv).astype(BF16)
            d5_ref[2, rows, cs] =(da * uv).astype(BF16)
            sums = (sums[0] + csum(dz3 * a_m1), sums[1] + csum(dz3 * a), sums[2] + csum(dz3 * a_p1),
                    sums[3] + csum(dz3))

            dya = dy_ref[rows, CONV_W + p * LANES:CONV_W + (p + 1) * LANES].astype(F32)
            zav = za_ref[rows, cs].astype(F32)
            ov = o_ref[rows, cs].astype(F32)
            sa = _sigmoid(zav)
            d_o = dya * zav * sa
            sdo[p, rows, :] = d_o
            do_ref[rows, cs] = d_o.astype(BF16)
            d5_ref[4, rows, cs] =(dya * ov * (sa * (1.0 + zav * (1.0 - sa)))).astype(BF16)
            return sums, _split_dot(d_o * ov, _lanes_to_head(p, 1.0))

        def chunk(ci, carry):
            c0 = pl.multiple_of(ci * CHUNK, CHUNK)
            out, delta = [], jnp.zeros((CHUNK, LANES), F32)
            for p in range(N_PAIRS):
                sums, part = one_group(p, c0, carry[p])
                out.append(sums)
                delta = delta + part
            sdelta[0, pl.ds(c0, CHUNK), :] = delta
            return tuple(out)

        z = jnp.zeros((1, LANES), F32)
        all_sums = lax.fori_loop(0, tm // CHUNK, chunk, ((z, z, z, z),) * N_PAIRS)
        for p in range(N_PAIRS):
            for k in range(4):
                small_ref[k:k + 1, p * LANES:(p + 1) * LANES] += all_sums[p][k]
        delta_ref[...] = sdelta[0]
        for dil, wide_ref, stat_ref in zip(VIEW_DILATIONS, (do2_ref, do3_ref), (delta2_ref, delta3_ref)):
            _to_view(sdo, wide_ref, dil, BF16)
            _to_view(sdelta, stat_ref, dil, F32)

    pspec = pl.BlockSpec((tm, CONV_W), lambda i: (i, 0))
    hp, hn = _row_halo_specs(CONV_W, tm, S)
    piece = jax.ShapeDtypeStruct((S, CONV_W), BF16)
    outs = pl.pallas_call(
        body, name="gates_bwd", grid=(n_t,),
        out_shape=[jax.ShapeDtypeStruct((5, S, CONV_W), BF16), jax.ShapeDtypeStruct((8, CONV_W), F32)]
        + [_view_shape(S, d, ATTN_W, BF16) for d in DILATIONS] + [_view_shape(S, d, LANES, F32) for d in DILATIONS],
        in_specs=[pl.BlockSpec((tm, D_MODEL), lambda i: (i, 0)), hp, hn,
                  pspec, hp, hn, pspec, hp, hn, pspec, hp, hn, pspec, hp, hn, pspec, pspec,
                  pl.BlockSpec((3, CONV_W), lambda i: (0, 0)), pl.BlockSpec((1, CONV_W), lambda i: (0, 0))],
        out_specs=[pl.BlockSpec((5, tm, CONV_W), lambda i: (0, i, 0)), pl.BlockSpec((8, CONV_W), lambda i: (0, 0))]
        + [_view_spec(tm, d, ATTN_W) for d in DILATIONS] + [_view_spec(tm, d, LANES) for d in DILATIONS],
        scratch_shapes=[_tile_scratch(tm, ATTN_W), _tile_scratch(tm, LANES)]
        + [pltpu.VMEM((tm + 2 * HALO, CONV_W), BF16)] * 5,
        compiler_params=_params(("arbitrary",), 40),
    )(dy, dy, dy, u, u, u, gb, gb, gb, gc, gc, gc, zc, zc, zc, za, o, conv_w, conv_b.reshape(1, CONV_W))
    return outs[0], outs[1], outs[2:5], outs[5:8]


def _attn_bwd(g, dil, qv, kv, vv, dov, lv, dv_, rqv, rkv, gain, tables):
    L = qv.shape[0]
    T = min(1024, L)
    n_sub = T // QB + 1
    QW = T + 2 * HALF
    KW = T + 2 * QB

    def body(qm_ref, qp_ref, qn_ref, dom_ref, dop_ref, don_ref, km_ref, kp_ref, kn_ref, vm_ref, vp_ref, vn_ref,
             lm_ref, lp_ref, ln_ref, dm_ref, dp_ref, dn_ref, rqm_ref, rqp_ref, rqn_ref, rk_ref,
             gain_ref, bias_ref,
             dq_ref, dk_ref, dv_ref, ds_ref, a_ref,
             qwin, dowin, kwin, vwin, lwin, dwin, rqwin, dqwin, dkacc, dvacc,
             s_scr, d_scr, p_scr, ds_scr, q2_scr, do2_scr, dq_scr):
        r = pl.program_id(0)
        i = pl.program_id(1)

        @pl.when((r == 0) & (i == 0))
        def _():
            ds_ref[...] = jnp.zeros_like(ds_ref)
            a_ref[...] = jnp.zeros_like(a_ref)

        for win, (pr, mn, nx), h_rows in ((qwin, (qp_ref, qm_ref, qn_ref), HALF),
                                          (dowin, (dop_ref, dom_ref, don_ref), HALF),
                                          (lwin, (lp_ref, lm_ref, ln_ref), HALF),
                                          (dwin, (dp_ref, dm_ref, dn_ref), HALF),
                                          (rqwin, (rqp_ref, rqm_ref, rqn_ref), HALF),
                                          (kwin, (kp_ref, km_ref, kn_ref), QB),
                                          (vwin, (vp_ref, vm_ref, vn_ref), QB)):
            win[0:h_rows] = pr[...]
            win[h_rows:h_rows + T] = mn[...]
            win[h_rows + T:] = nx[...]
        dkacc[0:QB] = jnp.zeros((QB, ATTN_W), F32)
        dvacc[0:QB] = jnp.zeros((QB, ATTN_W), F32)

        lo = _lane((QB, LANES)) < HEAD_DIM
        same_head = _same_head()
        krow = lax.broadcasted_iota(jnp.int32, (KB, 1), 0)
        qcol = _lane((1, 2 * QB)) % QB
        row = lax.broadcasted_iota(jnp.int32, (QB, 1), 0)
        chan_lo = lax.broadcasted_iota(jnp.int32, (LANES, 1), 0) < HEAD_DIM

        cols = [slice(p * LANES, (p + 1) * LANES) for p in range(N_PAIRS)]
        nt = (((1,), (1,)), ((), ()))

        def scores(j):
            r0 = pl.multiple_of(j * QB, QB)
            for k, cs in enumerate(cols):
                qf = qwin[pl.ds(r0, QB), cs].astype(F32) * gain_ref[:, cs]
                q2_scr[k] = jnp.concatenate([jnp.where(lo, qf, 0.0), jnp.where(lo, 0.0, qf)], axis=0).astype(BF16)
                dov = dowin[pl.ds(r0, QB), cs]
                zero = jnp.zeros_like(dov)
                do2_scr[k] = jnp.concatenate([jnp.where(lo, dov, zero), jnp.where(lo, zero, dov)], axis=0)
                s_scr[k] = lax.dot_general(kwin[pl.ds(r0, KB), cs], q2_scr[k], nt, preferred_element_type=F32)
                d_scr[k] = lax.dot_general(vwin[pl.ds(r0, KB), cs], do2_scr[k], nt, preferred_element_type=F32)

        def sub(j):
            r0 = pl.multiple_of(j * QB, QB)
            kpos = i * T - QB + j * QB + krow
            qpos = i * T - HALF + j * QB + qcol
            valid = (kpos >= 0) & (kpos < L) & (qpos >= 0) & (qpos < L)
            owned_t = ((qpos >= i * T) & (qpos < i * T + T)).astype(F32)
            qpos_c = i * T - HALF + j * QB + row
            owned = ((qpos_c >= i * T) & (qpos_c < i * T + T)).astype(F32)
            lse_rows = lwin[pl.ds(r0, QB), :].T
            delta_rows = dwin[pl.ds(r0, QB), :].T
            rq_t = rqwin[pl.ds(r0, QB), :]
            for p in range(N_PAIRS):
                k = p
                bias2 = jnp.concatenate([bias_ref[0, 2 * p], bias_ref[0, 2 * p + 1]], axis=1)
                lse2 = jnp.concatenate([lse_rows[2 * p:2 * p + 1, :], lse_rows[2 * p + 1:2 * p + 2, :]], axis=1)
                delta2 = jnp.concatenate([delta_rows[2 * p:2 * p + 1, :], delta_rows[2 * p + 1:2 * p + 2, :]], axis=1)
                pt = jnp.where(valid, jnp.exp(s_scr[k] + bias2 - lse2), 0.0)
                dst = pt * (d_scr[k] - delta2)
                ds_ref[p] += dst * owned_t
                p_scr[p] = pt.astype(BF16)
                ds_scr[p] = dst.astype(BF16)
            for p, cs in enumerate(cols):
                k = p
                for acc_ref, lhs, rhs in ((dvacc, p_scr[p], do2_scr[k]), (dkacc, ds_scr[p], q2_scr[k])):
                    part = jnp.dot(lhs, rhs, preferred_element_type=F32)
                    acc_ref[pl.ds(r0, QB), cs] += part[0:QB]
                    acc_ref[pl.ds(r0 + QB, QB), cs] = part[QB:KB]
                dq_scr[p] = lax.dot_general(kwin[pl.ds(r0, KB), cs], ds_scr[p], (((0,), (0,)), ((), ())),
                                            preferred_element_type=F32)
            for p, cs in enumerate(cols):
                g2 = gain_ref[:, cs]
                q_f = qwin[pl.ds(r0, QB), cs].astype(F32)
                dqt2 = dq_scr[p]
                dqm2 = jnp.where(chan_lo, dqt2[:, 0:QB], dqt2[:, QB:2 * QB]).T
                a_ref[p] += (dqm2 * q_f * owned).reshape(QB // 8, 8, LANES).sum(axis=0)
                dqh = dqm2 * g2
                mean_c = _split_dot(dqh * q_f, same_head) * (1.0 / HEAD_DIM)
                rq2 = _split_dot(rq_t, _head_to_lanes(p))
                dqwin[pl.ds(r0, QB), cs] = (rq2 * (dqh - q_f * mean_c)).astype(BF16)

        def one_sub(j, carry):
            scores(j)
            sub(j)
            return carry

        lax.fori_loop(0, n_sub, one_sub, 0)
        dq_ref[...] = dqwin[HALF:HALF + T]

        def finish(j, carry):
            r0 = pl.multiple_of(j * QB, QB)
            rk_t = rk_ref[pl.ds(r0, QB), :]
            for p in range(N_PAIRS):
                cs = slice(p * LANES, (p + 1) * LANES)
                dkh = dkacc[pl.ds(QB + r0, QB), cs]
                k_f = km_ref[pl.ds(r0, QB), cs].astype(F32)
                mean_c = _split_dot(dkh * k_f, same_head) * (1.0 / HEAD_DIM)
                rk2 = _split_dot(rk_t, _head_to_lanes(p))
                dk_ref[pl.ds(r0, QB), cs] = (rk2 * (dkh - k_f * mean_c)).astype(BF16)
                dv_ref[pl.ds(r0, QB), cs] = dvacc[pl.ds(QB + r0, QB), cs].astype(BF16)
            return carry

        lax.fori_loop(0, T // QB, finish, 0)

    main = pl.BlockSpec((T, ATTN_W), lambda r, i: (i, r))
    smain = pl.BlockSpec((T, LANES), lambda r, i: (i, r))
    q_prev, q_next = _halo_specs(ATTN_W, HALF, T, L, lambda r: r)
    k_prev, k_next = _halo_specs(ATTN_W, QB, T, L, lambda r: r)
    s_prev, s_next = _halo_specs(LANES, HALF, T, L, lambda r: r)
    piece = jax.ShapeDtypeStruct((L, dil * ATTN_W), BF16)
    return pl.pallas_call(
        body, name=f"attn_bwd_d{dil}", grid=(dil, L // T),
        out_shape=(piece, piece, piece, jax.ShapeDtypeStruct((N_PAIRS, KB, 2 * QB), F32),
                   jax.ShapeDtypeStruct((N_PAIRS, 8, LANES), F32)),
        in_specs=[main, q_prev, q_next, main, q_prev, q_next, main, k_prev, k_next, main, k_prev, k_next,
                  smain, s_prev, s_next, smain, s_prev, s_next, smain, s_prev, s_next, smain,
                  pl.BlockSpec((1, ATTN_W), lambda r, i: (0, 0)),
                  pl.BlockSpec((1, N_HEADS, KB, QB), lambda r, i: (g, 0, 0, 0))],
        out_specs=(main, main, main, pl.BlockSpec((N_PAIRS, KB, 2 * QB), lambda r, i: (0, 0, 0)),
                   pl.BlockSpec((N_PAIRS, 8, LANES), lambda r, i: (0, 0, 0))),
        scratch_shapes=[pltpu.VMEM((QW, ATTN_W), BF16), pltpu.VMEM((QW, ATTN_W), BF16),
                        pltpu.VMEM((KW, ATTN_W), BF16), pltpu.VMEM((KW, ATTN_W), BF16),
                        pltpu.VMEM((QW, LANES), F32), pltpu.VMEM((QW, LANES), F32), pltpu.VMEM((QW, LANES), F32),
                        pltpu.VMEM((QW, ATTN_W), BF16),
                        pltpu.VMEM((KW, ATTN_W), F32), pltpu.VMEM((KW, ATTN_W), F32),
                        pltpu.VMEM((N_PAIRS, KB, 2 * QB), F32), pltpu.VMEM((N_PAIRS, KB, 2 * QB), F32),
                        pltpu.VMEM((N_PAIRS, KB, 2 * QB), BF16), pltpu.VMEM((N_PAIRS, KB, 2 * QB), BF16),
                        pltpu.VMEM((N_PAIRS, 2 * QB, LANES), BF16), pltpu.VMEM((N_PAIRS, 2 * QB, LANES), BF16),
                        pltpu.VMEM((N_PAIRS, LANES, 2 * QB), F32)],
        compiler_params=_params(("arbitrary", "arbitrary"), 56),
    )(qv, qv, qv, dov, dov, dov, kv, kv, kv, vv, vv, vv, lv, lv, lv, dv_, dv_, dv_, rqv, rqv, rqv, rkv, gain, tables)


D5_GROUPS = (0, 1, 2, 3, 7)
QKV_GROUPS = (4, 5, 6)


def _proj_bwd(x, d_out, norm_w, wg, d5, qkv_shares):
    S = x.shape[0]
    tm = 512
    flat = [share for p in qkv_shares for share in p]
    n_t = S // tm

    def body(*refs):
        x_ref, g_ref, nw_ref, wg_ref, d5_ref = refs[:5]
        piece_refs = refs[5:5 + len(flat)]
        gx_ref, dnw_ref, ht_ref, dqkv_ref, sview = refs[5 + len(flat):]
        i = pl.program_id(0)

        @pl.when(i == 0)
        def _():
            dnw_ref[...] = jnp.zeros_like(dnw_ref)

        xv = x_ref[...]
        r = lax.rsqrt(jnp.mean(xv * xv, axis=-1, keepdims=True) + EPS)
        xh = xv * r
        nw = nw_ref[...]
        ht_ref[...] = (xh * nw).T.astype(BF16)
        dh = jnp.zeros((tm, D_MODEL), F32)
        k = 0
        for j in range(8):
            if j in D5_GROUPS:
                dp = d5_ref[D5_GROUPS.index(j)]
            else:
                shares = qkv_shares[QKV_GROUPS.index(j)]
                dp = jnp.zeros((tm, CONV_W), F32)
                for t, (_, dil) in enumerate(shares):
                    if dil == 1:
                        dp = dp + piece_refs[k + t][...].astype(F32)
                    else:
                        _from_view(piece_refs[k + t], sview, dil)
                        dp = dp + _scratch_value(sview)
                dp = dp.astype(BF16)
                k += len(shares)
                dqkv_ref[QKV_GROUPS.index(j)] = dp
            dh = dh + lax.dot_general(dp, wg_ref[j], (((1,), (1,)), ((), ())), preferred_element_type=F32)
        dnw_ref[...] += (dh * xh).reshape(tm // 8, 8, D_MODEL).sum(axis=0)
        dxh = dh * nw
        mean_c = jnp.mean(dxh * xh, axis=-1, keepdims=True)
        gx_ref[...] = g_ref[...].astype(F32) + r * (dxh - xh * mean_c)

    tile = pl.BlockSpec((tm, D_MODEL), lambda i: (i, 0))
    return pl.pallas_call(
        body, name="proj_bwd", grid=(n_t,),
        out_shape=(jax.ShapeDtypeStruct((S, D_MODEL), F32), jax.ShapeDtypeStruct((8, D_MODEL), F32),
                   jax.ShapeDtypeStruct((D_MODEL, S), BF16), jax.ShapeDtypeStruct((3, S, CONV_W), BF16)),
        in_specs=[tile, tile, pl.BlockSpec((1, D_MODEL), lambda i: (0, 0)),
                  pl.BlockSpec(wg.shape, lambda i: (0, 0, 0), pipeline_mode=pl.Buffered(1)),
                  pl.BlockSpec((5, tm, CONV_W), lambda i: (0, i, 0))]
        + [_view_spec(tm, dil, CONV_W) for _, dil in flat],
        out_specs=(tile, pl.BlockSpec((8, D_MODEL), lambda i: (0, 0)),
                   pl.BlockSpec((D_MODEL, tm), lambda i: (0, i)), pl.BlockSpec((3, tm, CONV_W), lambda i: (0, i, 0))),
        scratch_shapes=[_tile_scratch(tm, CONV_W)],
        compiler_params=_params(("arbitrary",), 56),
    )(x, d_out, norm_w.reshape(1, D_MODEL), wg, d5, *[a for a, _ in flat])


def _dw_exchange(ht, d5, dqkv, gw_out, small):
    D, S = ht.shape
    tk = 2048
    n_t = S // tk
    me_outer = 4 * lax.axis_index("x") + 2 * lax.axis_index("y") + lax.axis_index("c")
    order = ((me_outer + 1 + jnp.arange(N_DEV, dtype=jnp.int32)) % N_DEV).astype(jnp.int32)
    in_d5 = ((order < 4) | (order == 7)).astype(jnp.int32)
    at_d5 = jnp.where(order < 4, order, jnp.where(order == 7, 4, 0)).astype(jnp.int32)
    at_qkv = jnp.where(in_d5 == 1, 0, order - 4).astype(jnp.int32)

    def body(in_d5_ref, at_d5_ref, at_qkv_ref, ht_hbm, d5_ref, dqkv_ref, gout_ref, sm_ref, rin_ref, rout_ref, rsm_ref,
             acc, sbuf, ht_vmem, in_send, in_recv, side_send, side_recv, local_sems, ht_sems):
        s = pl.program_id(0)
        t = pl.program_id(1)
        x, y, c = lax.axis_index("x"), lax.axis_index("y"), lax.axis_index("c")
        me = 4 * x + 2 * y + c

        def ht_copy(j):
            return pltpu.make_async_copy(ht_hbm.at[:, j * tk:(j + 1) * tk], ht_vmem.at[j], ht_sems.at[j])

        def in_copy(k):
            to = (me + 1 + k) % N_DEV
            return pltpu.make_async_remote_copy(
                src_ref=sbuf.at[k % 2], dst_ref=rin_ref.at[me], send_sem=in_send.at[k], recv_sem=in_recv.at[k],
                device_id=(to // 4, (to // 2) % 2, to % 2), device_id_type=MESH)

        def in_landing(k):
            frm = (me + 2 * N_DEV - 1 - k) % N_DEV
            return pltpu.make_async_remote_copy(
                src_ref=sbuf.at[0], dst_ref=rin_ref.at[frm], send_sem=in_send.at[k], recv_sem=in_recv.at[k],
                device_id=(x, y, c), device_id_type=MESH)

        def side_copies():
            local = [pltpu.make_async_copy(gout_ref.at[me], rout_ref.at[me], local_sems.at[0]),
                     pltpu.make_async_copy(sm_ref, rsm_ref.at[me], local_sems.at[1])]
            remote = []
            for k in range(1, N_DEV):
                px = 1 - x if k & 4 else x
                py = 1 - y if k & 2 else y
                pc = 1 - c if k & 1 else c
                peer = 4 * px + 2 * py + pc
                for a, (src, dst) in enumerate(((gout_ref.at[peer], rout_ref.at[me]), (sm_ref, rsm_ref.at[me]))):
                    remote.append(pltpu.make_async_remote_copy(
                        src_ref=src, dst_ref=dst, send_sem=side_send.at[a * 7 + k - 1],
                        recv_sem=side_recv.at[a * 7 + k - 1], device_id=(px, py, pc), device_id_type=MESH))
            return local, remote

        @pl.when((s == 0) & (t == 0))
        def _():
            for j in range(n_t):
                ht_copy(j).start()
            local, remote = side_copies()
            for cp in local + remote:
                cp.start()

        for j in range(n_t):
            @pl.when((s == 0) & (t == j))
            def _(j=j):
                ht_copy(j).wait()

        @pl.when(t == 0)
        def _():
            acc[...] = jnp.zeros_like(acc)

        dp = jnp.where(in_d5_ref[s] == 1, d5_ref[0], dqkv_ref[0])
        acc[...] += jnp.dot(ht_vmem[t], dp, preferred_element_type=F32)

        for k in range(N_DEV):
            @pl.when((s == k) & (t == n_t - 1))
            def _(k=k):
                if k >= 2:
                    in_copy(k - 2).wait_send()
                sbuf[k % 2] = acc[...].astype(BF16)
                if k < N_DEV - 1:
                    in_copy(k).start()
                else:
                    own = pltpu.make_async_copy(sbuf.at[k % 2], rin_ref.at[me], local_sems.at[2])
                    own.start()
                    in_copy(k - 1).wait_send()
                    for j in range(N_DEV - 1):
                        in_landing(j).wait_recv()
                    local, remote = side_copies()
                    for cp in remote:
                        cp.wait_recv()
                    for cp in remote:
                        cp.wait_send()
                    for cp in local:
                        cp.wait()
                    own.wait()

    hbm = pl.BlockSpec(memory_space=pl.ANY)
    grid_spec = pltpu.PrefetchScalarGridSpec(
        num_scalar_prefetch=3, grid=(N_DEV, n_t),
        in_specs=[hbm,
                  pl.BlockSpec((1, tk, CONV_W), lambda s, t, use, a5, a3: (a5[s], jnp.where(use[s] == 1, t, 0), 0)),
                  pl.BlockSpec((1, tk, CONV_W), lambda s, t, use, a5, a3: (a3[s], jnp.where(use[s] == 1, 0, t), 0)),
                  hbm, hbm],
        out_specs=(hbm, hbm, hbm),
        scratch_shapes=[pltpu.VMEM((D, CONV_W), F32), pltpu.VMEM((2, D, CONV_W), BF16),
                        pltpu.VMEM((n_t, D, tk), BF16),
                        pltpu.SemaphoreType.DMA((N_DEV - 1,)), pltpu.SemaphoreType.DMA((N_DEV - 1,)),
                        pltpu.SemaphoreType.DMA((14,)), pltpu.SemaphoreType.DMA((14,)),
                        pltpu.SemaphoreType.DMA((3,)), pltpu.SemaphoreType.DMA((n_t,))])
    return pl.pallas_call(
        body, name="dw_exchange", grid_spec=grid_spec,
        out_shape=(jax.ShapeDtypeStruct((N_DEV, D, CONV_W), BF16), jax.ShapeDtypeStruct(gw_out.shape, F32),
                   jax.ShapeDtypeStruct((N_DEV,) + small.shape, F32)),
        compiler_params=_params(("arbitrary", "arbitrary"), 52),
    )(in_d5, at_d5, at_qkv, ht, d5, dqkv, gw_out, small)


def _adamw_math(w, g, m, v):
    m2 = ADAM_B1 * m + (1.0 - ADAM_B1) * g
    v2 = ADAM_B2 * v + (1.0 - ADAM_B2) * (g * g)
    m_hat = m2 / (1.0 - ADAM_B1 ** ADAM_STEP)
    v_hat = v2 / (1.0 - ADAM_B2 ** ADAM_STEP)
    delta = -ADAM_LR * (m_hat / (jnp.sqrt(v_hat) + ADAM_EPS) + ADAM_WD * w)
    return delta, m2, v2


def _adamw_sharded(name, parts, w, m, v, rows):
    R, C = w.shape

    def body(p_ref, w_ref, m_ref, v_ref, g_ref, d_ref, m2_ref, v2_ref):
        g = p_ref[0].astype(F32)
        for s in range(1, N_DEV):
            g = g + p_ref[s].astype(F32)
        g_ref[...] = g
        d_ref[...], m2_ref[...], v2_ref[...] = _adamw_math(w_ref[...], g, m_ref[...], v_ref[...])

    spec = pl.BlockSpec((rows, C), lambda i: (i, 0))
    out = jax.ShapeDtypeStruct((R, C), F32)
    return pl.pallas_call(
        body, name=name, grid=(R // rows,),
        out_shape=(out,) * 4,
        in_specs=[pl.BlockSpec((N_DEV, rows, C), lambda i: (0, i, 0)), spec, spec, spec],
        out_specs=(spec,) * 4,
        compiler_params=_params(("arbitrary",), 40),
    )(parts, w, m, v)


def _adamw_small(parts, w, m, v):
    R = w.shape[0]

    def body(p_ref, w_ref, m_ref, v_ref, g_ref, d_ref, m2_ref, v2_ref):
        g = p_ref[0]
        for s in range(1, N_DEV):
            g = g + p_ref[s]
        g_ref[...] = g
        d_ref[...], m2_ref[...], v2_ref[...] = _adamw_math(w_ref[...], g, m_ref[...], v_ref[...])

    vm = pl.BlockSpec(memory_space=pltpu.VMEM)
    out = jax.ShapeDtypeStruct((R, LANES), F32)
    return pl.pallas_call(
        body, name="adamw_small", out_shape=(out,) * 4, in_specs=[vm] * 4, out_specs=(vm,) * 4,
    )(parts, w, m, v)


def _qk_gain_grads(a_parts, q_norm_w, k_norm_w, bias_g):
    def body(a_ref, qw_ref, kw_ref, bg_ref, out_ref):
        tot = jnp.zeros((8, LANES), F32)
        for g in range(len(DILATIONS)):
            for p in range(N_PAIRS):
                tot = tot + a_ref[g, p]
        col = jnp.sum(tot, axis=0, keepdims=True)
        a64 = col[:, 0:HEAD_DIM] + col[:, HEAD_DIM:LANES]
        out_ref[...] = jnp.zeros_like(out_ref)
        out_ref[0:1, 0:HEAD_DIM] = LOGIT_SCALE * kw_ref[...] * a64
        out_ref[1:2, 0:HEAD_DIM] = LOGIT_SCALE * qw_ref[...] * a64
        for h in range(N_HEADS):
            r0, c0 = 2 + h // 4, (h % 4) * N_BUCKETS
            out_ref[r0:r0 + 1, c0:c0 + N_BUCKETS] = bg_ref[h // 2, h % 2:h % 2 + 1, 0:N_BUCKETS]

    vm = pl.BlockSpec(memory_space=pltpu.VMEM)
    return pl.pallas_call(
        body, name="qk_gain_grads", out_shape=jax.ShapeDtypeStruct((8, LANES), F32),
        in_specs=[vm] * 4, out_specs=vm,
    )(a_parts, q_norm_w.reshape(1, HEAD_DIM), k_norm_w.reshape(1, HEAD_DIM), bias_g)


SMALL_ROWS = 32


def _pack_small(norm_w, conv_b, q_norm_w, k_norm_w, rel_bias, conv_w_full):
    pad64 = lambda a: jnp.pad(a, (0, LANES - HEAD_DIM)).reshape(1, LANES)
    return jnp.concatenate([
        norm_w.reshape(8, LANES), conv_b.reshape(4, LANES), pad64(q_norm_w), pad64(k_norm_w),
        rel_bias.T.reshape(2, LANES), conv_w_full.reshape(12, LANES), jnp.zeros((4, LANES), F32)], axis=0)


def _unpack_small(s):
    return (s[0:8].reshape(D_MODEL), s[8:12].reshape(CONV_W), s[12, 0:HEAD_DIM], s[13, 0:HEAD_DIM],
            s[14:16].reshape(N_HEADS, N_BUCKETS).T, s[16:28].reshape(3, CONV_W))


def kernel(x, norm_w, w_in, conv_w, conv_b, q_norm_w, k_norm_w, rel_bias, w_out, loss_target, m_norm_w, m_w_in, m_conv_w, m_conv_b, m_q_norm_w, m_k_norm_w, m_rel_bias, m_w_out, v_norm_w, v_w_in, v_conv_w, v_conv_b, v_q_norm_w, v_k_norm_w, v_rel_bias, v_w_out):
    S = x.shape[1]
    x2 = x.reshape(S, D_MODEL)
    tgt = loss_target.reshape(S, D_MODEL)
    me = 4 * lax.axis_index("x") + 2 * lax.axis_index("y") + lax.axis_index("c")
    shard_w = CONV_W // N_DEV

    cw_pad = jnp.pad(conv_w, ((0, 5), (0, LANES - shard_w)))
    wg, wo_g, cw_g = _ag_weights(w_in, w_out, cw_pad)
    wo = wo_g.reshape(D_MODEL, D_MODEL)
    conv_w_full = cw_g[:, 0:3, 0:shard_w].transpose(1, 0, 2).reshape(3, CONV_W)

    bmat_t = jnp.asarray(np.stack([_bucket_matrix(d).T for d in DILATIONS]))
    tables = _bias_tables(rel_bias, bmat_t)
    gain = jnp.tile(q_norm_w * k_norm_w * LOGIT_SCALE, N_HEADS).reshape(1, ATTN_W)

    u, gb, gc, zc, za, q_hat, k_hat, v, rq, rk = _fwd_proj(x2, norm_w, wg)
    o_parts, lse_parts = [], []
    for g, dil in enumerate(DILATIONS):
        o_g, lse_g = _attn_fwd(g, dil, q_hat[g], k_hat[g], v[g], gain, tables)
        o_parts.append(o_g)
        lse_parts.append(lse_g)
    y, o, lse = _combine_gates(o_parts, lse_parts, u, gb, gc, zc, za, conv_w_full, conv_b)
    d_out, dy, gwo_part, loss_part = _out_proj(y, x2, tgt, wo)
    loss_mine = jnp.sum(loss_part) * (0.5 / D_MODEL)

    d5, conv_small, d_o, delta = _gates_bwd(dy, u, gb, gc, zc, za, o, conv_w_full, conv_b)
    dqs, dks, dvs, ds_parts, a_parts = [], [], [], [], []
    for g, dil in enumerate(DILATIONS):
        dq, dk, dv, ds_sum, a_sum = _attn_bwd(g, dil, q_hat[g], k_hat[g], v[g], d_o[g], lse[g], delta[g], rq[g], rk[g],
                                              gain, tables)
        dqs.append((dq, dil))
        dks.append((dk, dil))
        dvs.append((dv, dil))
        ds_parts.append(ds_sum)
        a_parts.append(a_sum)
    grad_x, gnw_part, h_t, dqkv = _proj_bwd(x2, d_out, norm_w, wg, d5, [dqs, dks, dvs])
    bias_g = _bias_grad(jnp.stack(ds_parts), bmat_t)
    qk_small = _qk_gain_grads(jnp.stack(a_parts), q_norm_w, k_norm_w, bias_g)

    small_part = jnp.concatenate([
        gnw_part.sum(axis=0).reshape(8, LANES), conv_small[3].reshape(4, LANES), qk_small[0:4],
        conv_small[0:3].reshape(12, LANES), jnp.zeros((4, LANES), F32).at[0, 0].set(loss_mine)], axis=0)
    r_in, r_out, r_small = _dw_exchange(h_t, d5, dqkv, gwo_part.reshape(N_DEV, D_MODEL // N_DEV, D_MODEL), small_part)

    g_win, d_win, m_win, v_win = _adamw_sharded("adamw_w_in", r_in, w_in, m_w_in, v_w_in, 128)
    g_wo, d_wo, m_wo, v_wo = _adamw_sharded("adamw_w_out", r_out, w_out, m_w_out, v_w_out, 32)

    def full_conv(a):
        return jnp.zeros((3, CONV_W), F32).at[:, 0:shard_w].set(a)

    packs = [_pack_small(nw_, cb_, qw_, kw_, rb_, full_conv(cw_)) for nw_, cb_, qw_, kw_, rb_, cw_ in (
        (norm_w, conv_b, q_norm_w, k_norm_w, rel_bias, conv_w),
        (m_norm_w, m_conv_b, m_q_norm_w, m_k_norm_w, m_rel_bias, m_conv_w),
        (v_norm_w, v_conv_b, v_q_norm_w, v_k_norm_w, v_rel_bias, v_conv_w))]
    r_small_mine = r_small.at[:, 16:28, :].set(
        jnp.pad(lax.dynamic_slice_in_dim(r_small[:, 16:28, :].reshape(N_DEV, 3, CONV_W), me * shard_w, shard_w, axis=2),
                ((0, 0), (0, 0), (0, CONV_W - shard_w))).reshape(N_DEV, 12, LANES))
    outs_small = _adamw_small(r_small_mine, *packs)
    g_s, d_s, m_s, v_s = [_unpack_small(t) for t in outs_small]
    loss = outs_small[0][28, 0]

    def leaves(small, big_in, big_out):
        nw_, cb_, qw_, kw_, rb_, cwf = small
        return (nw_, big_in, cwf[:, 0:shard_w], cb_, qw_, kw_, rb_, big_out)

    return (loss, grad_x.reshape(x.shape),
            *leaves(g_s, g_win, g_wo), *leaves(d_s, d_win, d_wo), *leaves(m_s, m_win, m_wo), *leaves(v_s, v_win, v_wo))
```

```python
import math

import numpy as np
import jax
import jax.numpy as jnp
from jax import lax
from jax.experimental import pallas as pl
from jax.experimental.pallas import tpu as pltpu

F32 = jnp.float32
BF16 = jnp.bfloat16

N_DEV = 8
D_MODEL = 1024
CONV_W = 512
ATTN_W = 512
N_HEADS = 8
HEAD_DIM = 64
N_PAIRS = N_HEADS // 2
LANES = 128
HALF = 64
QB = 128
KB = QB + 2 * HALF
DILATIONS = (1, 4, 16)
N_BUCKETS = 32
MAX_DISTANCE = 1024
EPS = 1e-6
NEG = -1e30
LOGIT_SCALE = HEAD_DIM ** -0.5

ADAM_LR = 0.001
ADAM_B1 = 0.9
ADAM_B2 = 0.999
ADAM_EPS = 1e-08
ADAM_WD = 0.01
ADAM_STEP = 10

MESH = pl.DeviceIdType.MESH
MIB = 1024 * 1024


def _params(semantics, vmem_mib):
    return pltpu.CompilerParams(dimension_semantics=semantics, vmem_limit_bytes=vmem_mib * MIB)


def _lane(shape):
    return lax.broadcasted_iota(jnp.int32, shape, len(shape) - 1)


def _sigmoid(z):
    return 1.0 / (1.0 + jnp.exp(-z))


def _split_dot(x, w):
    hi = x.astype(BF16)
    lo = (x - hi.astype(F32)).astype(BF16)
    return jnp.dot(hi, w, preferred_element_type=F32) + jnp.dot(lo, w, preferred_element_type=F32)


def _same_head():
    r = lax.broadcasted_iota(jnp.int32, (LANES, LANES), 0) // HEAD_DIM
    c = lax.broadcasted_iota(jnp.int32, (LANES, LANES), 1) // HEAD_DIM
    return (r == c).astype(BF16)


def _head_to_lanes(p):
    r = lax.broadcasted_iota(jnp.int32, (LANES, LANES), 0)
    c = lax.broadcasted_iota(jnp.int32, (LANES, LANES), 1) // HEAD_DIM
    return (r == 2 * p + c).astype(BF16)


def _lanes_to_head(p, scale):
    r = lax.broadcasted_iota(jnp.int32, (LANES, LANES), 0) // HEAD_DIM
    c = lax.broadcasted_iota(jnp.int32, (LANES, LANES), 1)
    return jnp.where(c == 2 * p + r, scale, 0.0).astype(BF16)


VIEW_DILATIONS = DILATIONS[1:]


def _view_shape(S, dil, width, dtype):
    return jax.ShapeDtypeStruct((S // dil, dil * width), dtype)


def _view_spec(tm, dil, width):
    return pl.BlockSpec((tm // dil, dil * width), lambda i: (i, 0))


def _tile_scratch(tm, width):
    return pltpu.VMEM((width // LANES, tm, LANES), F32)


def _scratch_value(scr_ref):
    n = scr_ref.shape[0]
    return scr_ref[0] if n == 1 else jnp.concatenate([scr_ref[c] for c in range(n)], axis=1)


def _to_view(scr_ref, out_ref, dil, dtype):
    n, tm, _ = scr_ref.shape
    for r in range(dil):
        for c in range(n):
            col = (r * n + c) * LANES
            out_ref[:, col:col + LANES] = scr_ref[c, pl.ds(r, tm // dil, stride=dil), :].astype(dtype)


def _from_view(blk_ref, scr_ref, dil):
    n, tm, _ = scr_ref.shape
    for r in range(dil):
        for c in range(n):
            col = (r * n + c) * LANES
            scr_ref[c, pl.ds(r, tm // dil, stride=dil), :] = blk_ref[:, col:col + LANES].astype(F32)


def _bucket_matrix(dilation):
    rel = np.arange(KB)[None, :] - HALF - np.arange(QB)[:, None]
    band = np.abs(rel) <= HALF
    dist = np.clip(rel, -HALF, HALF) * dilation
    half_b = N_BUCKETS // 2
    max_exact = half_b // 2
    ret = np.where(dist > 0, half_b, 0)
    n = np.abs(dist)
    nf = np.maximum(n, 1).astype(np.float32)
    large = max_exact + (np.log(nf / np.float32(max_exact)) / np.float32(math.log(MAX_DISTANCE / max_exact))
                         * np.float32(half_b - max_exact)).astype(np.int32)
    large = np.minimum(large, half_b - 1)
    bucket = ret + np.where(n < max_exact, n, large)
    return np.where(band, bucket, -1).astype(np.int32)


def _ag_weights(w_in, w_out, cw_pad):
    n_arr = 3

    def body(win_ref, wout_ref, cw_ref, gin_ref, gout_ref, gcw_ref, send_sems, recv_sems):
        x, y, c = lax.axis_index("x"), lax.axis_index("y"), lax.axis_index("c")
        me = (x, y, c)
        sibling = (x, y, 1 - c)
        chips = [(1 - x, y), (x, 1 - y), (1 - x, 1 - y)]
        arrays = (gin_ref, gout_ref, gcw_ref)

        def slot(px, py, pc):
            return 4 * px + 2 * py + pc

        gin_ref[slot(*me)] = win_ref[...].astype(BF16)
        gout_ref[slot(*me)] = wout_ref[...].astype(BF16)
        gcw_ref[slot(*me)] = cw_ref[...]

        def copy(a, k, block, to):
            ref = arrays[a].at[slot(*block)]
            return pltpu.make_async_remote_copy(
                src_ref=ref, dst_ref=ref, send_sem=send_sems.at[a * 7 + k], recv_sem=recv_sems.at[a * 7 + k],
                device_id=to, device_id_type=MESH)

        first = [copy(a, 0, me, sibling) for a in range(n_arr)]
        for j, chip in enumerate(chips):
            first += [copy(a, 1 + j, me, (*chip, c)) for a in range(n_arr)]
        for cp in first:
            cp.start()
        passed = []
        for j, chip in enumerate(chips):
            for a in range(n_arr):
                copy(a, 1 + j, (*chip, c), me).wait_recv()
            for a in range(n_arr):
                cp = copy(a, 4 + j, (*chip, c), sibling)
                cp.start()
                passed.append(cp)
        for a in range(n_arr):
            copy(a, 0, sibling, me).wait_recv()
        for j, chip in enumerate(chips):
            for a in range(n_arr):
                copy(a, 4 + j, (*chip, 1 - c), me).wait_recv()
        for cp in first + passed:
            cp.wait_send()

    vm = pl.BlockSpec(memory_space=pltpu.VMEM)
    return pl.pallas_call(
        body, name="ag_weights",
        out_shape=(jax.ShapeDtypeStruct((N_DEV,) + w_in.shape, BF16),
                   jax.ShapeDtypeStruct((N_DEV,) + w_out.shape, BF16),
                   jax.ShapeDtypeStruct((N_DEV,) + cw_pad.shape, F32)),
        in_specs=[vm, vm, vm], out_specs=(vm, vm, vm),
        scratch_shapes=[pltpu.SemaphoreType.DMA((n_arr * 7,)), pltpu.SemaphoreType.DMA((n_arr * 7,))],
        compiler_params=pltpu.CompilerParams(vmem_limit_bytes=40 * MIB),
    )(w_in, w_out, cw_pad)


def _fwd_proj(x, norm_w, wg):
    S = x.shape[0]
    tm = 512

    def body(x_ref, nw_ref, wg_ref, u_ref, gb_ref, gc_ref, zc_ref, za_ref, *rest):
        q_refs, k_refs, v_refs, rq_refs, rk_refs = (rest[3 * n:3 * n + 3] for n in range(5))
        scr, rscr = rest[15:]
        xv = x_ref[...]
        r = lax.rsqrt(jnp.mean(xv * xv, axis=-1, keepdims=True) + EPS)
        h = (xv * r * nw_ref[...]).astype(BF16)
        lane = _lane((tm, LANES))
        lo = lane < HEAD_DIM
        plain = {0: u_ref, 1: gb_ref, 2: gc_ref, 3: zc_ref, 7: za_ref}
        normed = {4: (q_refs, rq_refs), 5: (k_refs, rk_refs)}

        def emit(src, refs, dtype):
            refs[0][...] = _scratch_value(src).astype(dtype)
            for dil, ref in zip(VIEW_DILATIONS, refs[1:]):
                _to_view(src, ref, dil, dtype)

        group_order = (4, 0, 5, 1, 6, 2, 3, 7)
        acc_next = jnp.dot(h, wg_ref[group_order[0]], preferred_element_type=F32)
        for n, j in enumerate(group_order):
            acc = acc_next
            if n < 7:
                acc_next = jnp.dot(h, wg_ref[group_order[n + 1]], preferred_element_type=F32)
            if j in plain:
                plain[j][...] = acc.astype(BF16)
                continue
            if j == 6:
                for p in range(N_PAIRS):
                    scr[p] = acc[:, p * LANES:(p + 1) * LANES]
                emit(scr, v_refs, BF16)
                continue
            out_refs, r_refs = normed[j]
            r_tile = jnp.zeros((tm, LANES), F32)
            for p in range(N_PAIRS):
                blk = acc[:, p * LANES:(p + 1) * LANES]
                sq = blk * blk
                s_lo = jnp.sum(jnp.where(lo, sq, 0.0), axis=-1, keepdims=True)
                s_hi = jnp.sum(jnp.where(lo, 0.0, sq), axis=-1, keepdims=True)
                r_lo = lax.rsqrt(s_lo * (1.0 / HEAD_DIM) + EPS)
                r_hi = lax.rsqrt(s_hi * (1.0 / HEAD_DIM) + EPS)
                scr[p] = blk * jnp.where(lo, r_lo, r_hi)
                r_tile = jnp.where(lane == 2 * p, r_lo, r_tile)
                r_tile = jnp.where(lane == 2 * p + 1, r_hi, r_tile)
            rscr[0] = r_tile
            emit(scr, out_refs, BF16)
            emit(rscr, r_refs, F32)

    piece = jax.ShapeDtypeStruct((S, CONV_W), BF16)
    pspec = pl.BlockSpec((tm, CONV_W), lambda i: (i, 0))
    wide = [_view_shape(S, d, ATTN_W, BF16) for d in DILATIONS]
    wide_specs = [_view_spec(tm, d, ATTN_W) for d in DILATIONS]
    stat = [_view_shape(S, d, LANES, F32) for d in DILATIONS]
    stat_specs = [_view_spec(tm, d, LANES) for d in DILATIONS]
    outs = pl.pallas_call(
        body, name="fwd_proj", grid=(S // tm,),
        out_shape=[piece] * 5 + wide * 3 + stat * 2,
        in_specs=[pl.BlockSpec((tm, D_MODEL), lambda i: (i, 0)),
                  pl.BlockSpec((1, D_MODEL), lambda i: (0, 0)),
                  pl.BlockSpec(wg.shape, lambda i: (0, 0, 0), pipeline_mode=pl.Buffered(1))],
        out_specs=[pspec] * 5 + wide_specs * 3 + stat_specs * 2,
        scratch_shapes=[_tile_scratch(tm, ATTN_W), _tile_scratch(tm, LANES)],
        compiler_params=_params(("arbitrary",), 48),
    )(x, norm_w.reshape(1, D_MODEL), wg)
    u, gb, gc, zc, za = outs[:5]
    q_hat, k_hat, v, rq, rk = (outs[5 + 3 * n:8 + 3 * n] for n in range(5))
    return u, gb, gc, zc, za, q_hat, k_hat, v, rq, rk


def _bias_tables(rel_bias, bmat_t):
    def body(rb_ref, b_ref, out_ref):
        h = pl.program_id(1)
        b = b_ref[0]
        t = jnp.full((KB, QB), NEG, F32)
        for bk in range(N_BUCKETS):
            t = jnp.where(b == bk, rb_ref[bk, h], t)
        out_ref[0, 0] = t

    return pl.pallas_call(
        body, name="bias_tables", grid=(len(DILATIONS), N_HEADS),
        out_shape=jax.ShapeDtypeStruct((len(DILATIONS), N_HEADS, KB, QB), F32),
        in_specs=[pl.BlockSpec(memory_space=pltpu.SMEM),
                  pl.BlockSpec((1, KB, QB), lambda g, h: (g, 0, 0))],
        out_specs=pl.BlockSpec((1, 1, KB, QB), lambda g, h: (g, h, 0, 0)),
        compiler_params=_params(("arbitrary", "arbitrary"), 16),
    )(rel_bias, bmat_t)


def _bias_grad(ds_parts, bmat_t):
    present = [sorted(set(_bucket_matrix(d).ravel().tolist()) - {-1}) for d in DILATIONS]

    def body(ds1_ref, ds2_ref, ds3_ref, b_ref, out_ref):
        ds_refs = (ds1_ref, ds2_ref, ds3_ref)
        lane = _lane((1, LANES))
        row = lax.broadcasted_iota(jnp.int32, (8, LANES), 0)
        out = jnp.zeros((8, LANES), F32)
        for e in range(2):
            vec = jnp.zeros((1, LANES), F32)
            for g in range(len(DILATIONS)):
                b = b_ref[g]
                ds = ds_refs[g][0, :, e * QB:(e + 1) * QB]
                for bk in present[g]:
                    s = jnp.sum(jnp.where(b == bk, ds, 0.0), axis=-1, keepdims=True)
                    s = jnp.sum(s, axis=0, keepdims=True)
                    vec = vec + jnp.where(lane == bk, s, 0.0)
            out = jnp.where(row == e, vec, out)
        out_ref[0] = out

    return pl.pallas_call(
        body, name="bias_grad", grid=(N_PAIRS,),
        out_shape=jax.ShapeDtypeStruct((N_PAIRS, 8, LANES), F32),
        in_specs=[pl.BlockSpec((1, KB, 2 * QB), lambda p: (p, 0, 0))] * len(DILATIONS)
        + [pl.BlockSpec((len(DILATIONS), KB, QB), lambda p: (0, 0, 0))],
        out_specs=pl.BlockSpec((1, 8, LANES), lambda p: (p, 0, 0)),
        compiler_params=_params(("arbitrary",), 16),
    )(*ds_parts, bmat_t)


def _halo_specs(width, rows, T, L, cols_of):
    per = T // rows
    last = L // rows - 1
    prev = pl.BlockSpec((rows, width), lambda r, i: (jnp.maximum(i * per - 1, 0), cols_of(r)))
    nxt = pl.BlockSpec((rows, width), lambda r, i: (jnp.minimum((i + 1) * per, last), cols_of(r)))
    return prev, nxt


def _attn_fwd(g, dil, qv, kv, vv, gain, tables):
    L = qv.shape[0]
    T = min(1024, L)
    n_sub = T // QB

    def body(q_ref, km_ref, kp_ref, kn_ref, vm_ref, vp_ref, vn_ref, gain_ref, bias_ref, o_ref, lse_ref,
             kwin, vwin, s_scr, p_scr):
        i = pl.program_id(1)
        kwin[0:HALF] = kp_ref[...]
        kwin[HALF:HALF + T] = km_ref[...]
        kwin[HALF + T:] = kn_ref[...]
        vwin[0:HALF] = vp_ref[...]
        vwin[HALF:HALF + T] = vm_ref[...]
        vwin[HALF + T:] = vn_ref[...]
        lo = _lane((QB, LANES)) < HEAD_DIM
        krow = lax.broadcasted_iota(jnp.int32, (KB, 1), 0)
        chan_lo = lax.broadcasted_iota(jnp.int32, (LANES, 1), 0) < HEAD_DIM
        hrow = lax.broadcasted_iota(jnp.int32, (LANES, QB), 0)

        def sub(j, carry):
            r0 = pl.multiple_of(j * QB, QB)
            kpos = i * T + j * QB - HALF + krow
            kvalid = (kpos >= 0) & (kpos < L)
            lse_rows = jnp.zeros((LANES, QB), F32)
            cols = [slice(p * LANES, (p + 1) * LANES) for p in range(N_PAIRS)]
            for p, cs in enumerate(cols):
                qf = q_ref[pl.ds(r0, QB), cs].astype(F32) * gain_ref[:, cs]
                q2 = jnp.concatenate([jnp.where(lo, qf, 0.0), jnp.where(lo, 0.0, qf)], axis=0).astype(BF16)
                s_scr[p] = lax.dot_general(kwin[pl.ds(r0, KB), cs], q2, (((1,), (1,)), ((), ())),
                                           preferred_element_type=F32)
            inv_l = []
            for p in range(N_PAIRS):
                bias2 = jnp.concatenate([bias_ref[0, 2 * p], bias_ref[0, 2 * p + 1]], axis=1)
                logits = jnp.where(kvalid, s_scr[p] + bias2, NEG)
                m = jnp.max(logits, axis=0, keepdims=True)
                pt = jnp.exp(logits - m)
                l = jnp.sum(pt, axis=0, keepdims=True)
                p_scr[p] = pt.astype(BF16)
                inv_l.append(1.0 / l)
                lse2 = m + jnp.log(l)
                lse_rows = jnp.where(hrow == 2 * p, lse2[:, 0:QB], lse_rows)
                lse_rows = jnp.where(hrow == 2 * p + 1, lse2[:, QB:2 * QB], lse_rows)
            for p, cs in enumerate(cols):
                s_scr[p, 0:LANES, :] = lax.dot_general(vwin[pl.ds(r0, KB), cs], p_scr[p], (((0,), (0,)), ((), ())),
                                                       preferred_element_type=F32) * inv_l[p]
            for p, cs in enumerate(cols):
                ot2 = s_scr[p, 0:LANES, :]
                ot = jnp.where(chan_lo, ot2[:, 0:QB], ot2[:, QB:2 * QB])
                o_ref[pl.ds(r0, QB), cs] = ot.T.astype(BF16)
            lse_ref[pl.ds(r0, QB), :] = lse_rows.T
            return carry

        lax.fori_loop(0, n_sub, sub, 0)

    main = pl.BlockSpec((T, ATTN_W), lambda r, i: (i, r))
    prev, nxt = _halo_specs(ATTN_W, HALF, T, L, lambda r: r)
    o_g, lse_g = pl.pallas_call(
        body, name=f"attn_fwd_d{dil}", grid=(dil, L // T),
        out_shape=(jax.ShapeDtypeStruct((L, dil * ATTN_W), BF16), jax.ShapeDtypeStruct((L, dil * LANES), F32)),
        in_specs=[main, main, prev, nxt, main, prev, nxt,
                  pl.BlockSpec((1, ATTN_W), lambda r, i: (0, 0)),
                  pl.BlockSpec((1, N_HEADS, KB, QB), lambda r, i: (g, 0, 0, 0))],
        out_specs=(main, pl.BlockSpec((T, LANES), lambda r, i: (i, r))),
        scratch_shapes=[pltpu.VMEM((T + 2 * HALF, ATTN_W), BF16), pltpu.VMEM((T + 2 * HALF, ATTN_W), BF16),
                        pltpu.VMEM((N_PAIRS, KB, 2 * QB), F32), pltpu.VMEM((N_PAIRS, KB, 2 * QB), BF16)],
        compiler_params=_params(("arbitrary", "arbitrary"), 40),
    )(qv, kv, kv, kv, vv, vv, vv, gain, tables)
    return o_g, lse_g


HALO = 16
CHUNK = 128


def _row_halo_specs(width, tm, S, col=0):
    rows = HALO
    per = tm // rows
    last = S // rows - 1
    prev = pl.BlockSpec((rows, width), lambda i: (jnp.maximum(i * per - 1, 0), col))
    nxt = pl.BlockSpec((rows, width), lambda i: (jnp.minimum((i + 1) * per, last), col))
    return prev, nxt


def _shift_rows(a, prev_row, next_row, tm):
    row = lax.broadcasted_iota(jnp.int32, a.shape, 0)
    a_m1 = jnp.where(row == 0, prev_row, pltpu.roll(a, 1, 0))
    a_p1 = jnp.where(row == tm - 1, next_row, pltpu.roll(a, tm - 1, 0))
    return a_m1, a_p1


def _combine_gates(o_parts, lse_parts, u, gb, gc, zc, za, conv_w, conv_b):
    S = u.shape[0]
    tm = 512
    n_t = S // tm

    def body(o1, o2, o3, l1, l2, l3, u_ref, up_ref, un_ref, gb_ref, gc_ref, gcp_ref, gcn_ref, zc_ref, za_ref,
             cw_ref, cb_ref, y_ref, o_ref, lse_ref, lse2_ref, lse3_ref, so2, so3, sl2, sl3, slse):
        i = pl.program_id(0)
        for blk, scr, dil in ((o2, so2, DILATIONS[1]), (o3, so3, DILATIONS[2]),
                              (l2, sl2, DILATIONS[1]), (l3, sl3, DILATIONS[2])):
            _from_view(blk, scr, dil)
        ls = [l1[...], sl2[0], sl3[0]]
        lmax = jnp.maximum(jnp.maximum(ls[0], ls[1]), ls[2])
        es = [jnp.exp(l - lmax) for l in ls]
        den = es[0] + es[1] + es[2]
        slse[0] = lmax + jnp.log(den)
        lse_ref[...] = slse[0]
        _to_view(slse, lse2_ref, DILATIONS[1], F32)
        _to_view(slse, lse3_ref, DILATIONS[2], F32)
        inv = 1.0 / den
        ws = [e * inv for e in es]
        for p in range(N_PAIRS):
            cs = slice(p * LANES, (p + 1) * LANES)
            acc = jnp.zeros((tm, LANES), F32)
            spread = _head_to_lanes(p)
            for w, o_g in zip(ws, (o1[:, cs].astype(F32), so2[p], so3[p])):
                acc = acc + _split_dot(w, spread) * o_g
            o_ref[:, cs] = acc.astype(BF16)
            za = za_ref[:, cs].astype(F32)
            y_ref[:, CONV_W + p * LANES:CONV_W + (p + 1) * LANES] = (acc * za * _sigmoid(za)).astype(BF16)
        a = gc_ref[...].astype(F32) * u_ref[...].astype(F32)
        a_prev = gcp_ref[15:16, :].astype(F32) * up_ref[15:16, :].astype(F32) * (i > 0).astype(F32)
        a_next = gcn_ref[0:1, :].astype(F32) * un_ref[0:1, :].astype(F32) * (i < n_t - 1).astype(F32)
        a_m1, a_p1 = _shift_rows(a, a_prev, a_next, tm)
        z3 = cw_ref[0:1, :] * a_m1 + cw_ref[1:2, :] * a + cw_ref[2:3, :] * a_p1
        zc = zc_ref[...].astype(F32)
        y_ref[:, 0:CONV_W] = (gb_ref[...].astype(F32) * (z3 + cb_ref[...]) * zc * _sigmoid(zc)).astype(BF16)

    pspec = pl.BlockSpec((tm, CONV_W), lambda i: (i, 0))
    hp, hn = _row_halo_specs(CONV_W, tm, S)
    wide_specs = [_view_spec(tm, d, ATTN_W) for d in DILATIONS]
    stat_specs = [_view_spec(tm, d, LANES) for d in DILATIONS]
    outs = pl.pallas_call(
        body, name="combine_gates", grid=(n_t,),
        out_shape=[jax.ShapeDtypeStruct((S, D_MODEL), BF16), jax.ShapeDtypeStruct((S, ATTN_W), BF16)]
        + [_view_shape(S, d, LANES, F32) for d in DILATIONS],
        in_specs=wide_specs + stat_specs + [pspec, hp, hn, pspec, pspec, hp, hn, pspec, pspec,
                                            pl.BlockSpec((3, CONV_W), lambda i: (0, 0)),
                                            pl.BlockSpec((1, CONV_W), lambda i: (0, 0))],
        out_specs=[pl.BlockSpec((tm, D_MODEL), lambda i: (i, 0)), pspec] + stat_specs,
        scratch_shapes=[_tile_scratch(tm, ATTN_W), _tile_scratch(tm, ATTN_W),
                        _tile_scratch(tm, LANES), _tile_scratch(tm, LANES), _tile_scratch(tm, LANES)],
        compiler_params=_params(("arbitrary",), 40),
    )(*o_parts, *lse_parts, u, u, u, gb, gc, gc, gc, zc, za, conv_w, conv_b.reshape(1, CONV_W))
    return outs[0], outs[1], outs[2:]


def _out_proj(y, x, target, wo):
    S = x.shape[0]
    tm = 512

    def body(y_ref, x_ref, t_ref, wo_ref, g_ref, dy_ref, dwo_ref, loss_ref):
        i = pl.program_id(0)

        @pl.when(i == 0)
        def _():
            dwo_ref[...] = jnp.zeros_like(dwo_ref)
            loss_ref[...] = jnp.zeros_like(loss_ref)

        yv = y_ref[...]
        wo_v = wo_ref[...]
        err = x_ref[...] + jnp.dot(yv, wo_v, preferred_element_type=F32) - t_ref[...]
        e2 = (err * err).reshape(tm // 8, 8, D_MODEL).sum(axis=0)
        part = e2[:, 0:LANES]
        for k in range(1, D_MODEL // LANES):
            part = part + e2[:, k * LANES:(k + 1) * LANES]
        loss_ref[...] += part
        gv = (err * (1.0 / D_MODEL)).astype(BF16)
        g_ref[...] = gv
        dy_ref[...] = lax.dot_general(gv, wo_v, (((1,), (1,)), ((), ())), preferred_element_type=F32).astype(BF16)
        dwo_ref[...] += lax.dot_general(yv, gv, (((0,), (0,)), ((), ())), preferred_element_type=F32)

    tile = pl.BlockSpec((tm, D_MODEL), lambda i: (i, 0))
    return pl.pallas_call(
        body, name="out_proj", grid=(S // tm,),
        out_shape=(jax.ShapeDtypeStruct((S, D_MODEL), BF16), jax.ShapeDtypeStruct((S, D_MODEL), BF16),
                   jax.ShapeDtypeStruct((D_MODEL, D_MODEL), F32), jax.ShapeDtypeStruct((8, LANES), F32)),
        in_specs=[tile, tile, tile, pl.BlockSpec((D_MODEL, D_MODEL), lambda i: (0, 0), pipeline_mode=pl.Buffered(1))],
        out_specs=(tile, tile, pl.BlockSpec((D_MODEL, D_MODEL), lambda i: (0, 0)),
                   pl.BlockSpec((8, LANES), lambda i: (0, 0))),
        compiler_params=_params(("arbitrary",), 48),
    )(y, x, target, wo)


def _gates_bwd(dy, u, gb, gc, zc, za, o, conv_w, conv_b):
    S = u.shape[0]
    tm = 512
    n_t = S // tm

    def body(dy_ref, dyp_ref, dyn_ref, u_ref, up_ref, un_ref, gb_ref, gbp_ref, gbn_ref, gc_ref, gcp_ref, gcn_ref,
             zc_ref, zcp_ref, zcn_ref, za_ref, o_ref, cw_ref, cb_ref,
             d5_ref, small_ref,
             do_ref, do2_ref, do3_ref, delta_ref, delta2_ref, delta3_ref, sdo, sdelta,
             u_ext, gc_ext, gb_ext, zc_ext, dy_ext):
        i = pl.program_id(0)

        @pl.when(i == 0)
        def _():
            small_ref[...] = jnp.zeros_like(small_ref)

        exts = ((u_ext, up_ref, u_ref, un_ref), (gc_ext, gcp_ref, gc_ref, gcn_ref), (gb_ext, gbp_ref, gb_ref, gbn_ref),
                (zc_ext, zcp_ref, zc_ref, zcn_ref))
        for ext, pr, mn, nx in exts:
            ext[0:HALO] = pr[...]
            ext[HALO:HALO + tm] = mn[...]
            ext[HALO + tm:] = nx[...]
        dy_ext[0:HALO] = dyp_ref[...]
        dy_ext[HALO:HALO + tm] = dy_ref[:, 0:CONV_W]
        dy_ext[HALO + tm:] = dyn_ref[...]
        zero_halo = jnp.zeros((HALO, CONV_W), BF16)

        @pl.when(i == 0)
        def _():
            u_ext[0:HALO] = zero_halo
            dy_ext[0:HALO] = zero_halo

        @pl.when(i == n_t - 1)
        def _():
            u_ext[HALO + tm:] = zero_halo
            dy_ext[HALO + tm:] = zero_halo

        wide_rows = CHUNK + 2 * HALO
        mid = slice(HALO, HALO + CHUNK)
        csum = lambda t: jnp.sum(t, axis=0, keepdims=True)

        def one_group(p, c0, sums):
            cs = slice(p * LANES, (p + 1) * LANES)
            w0, w1, w2, cb = cw_ref[0:1, cs], cw_ref[1:2, cs], cw_ref[2:3, cs], cb_ref[:, cs]
            rows = pl.ds(c0, CHUNK)
            wide = pl.ds(c0, wide_rows)
            u_e, gc_e, gb_e, zc_e, dy_e = (r[wide, cs].astype(F32) for r in (u_ext, gc_ext, gb_ext, zc_ext, dy_ext))
            a_e = gc_e * u_e
            sz_e = _sigmoid(zc_e)
            dz3_e = dy_e * zc_e * sz_e * gb_e
            a, a_m1, a_p1 = a_e[mid], pltpu.roll(a_e, 1, 0)[mid], pltpu.roll(a_e, wide_rows - 1, 0)[mid]
            dz3, dz3_m1, dz3_p1 = dz3_e[mid], pltpu.roll(dz3_e, 1, 0)[mid], pltpu.roll(dz3_e, wide_rows - 1, 0)[mid]
            uv, gcv, gbv, zcv, dyc, sz = u_e[mid], gc_e[mid], gb_e[mid], zc_e[mid], dy_e[mid], sz_e[mid]
            cvp = w0 * a_m1 + w1 * a + w2 * a_p1 + cb
            dcv = dyc * zcv * sz
            d5_ref[3, rows, cs] =(dyc * gbv * cvp * (sz * (1.0 + zcv * (1.0 - sz)))).astype(BF16)
            d5_ref[1, rows, cs] =(dcv * cvp).astype(BF16)
            da = w0 * dz3_p1 + w1 * dz3 + w2 * dz3_m1
            d5_ref[0, rows, cs] =(da * gcv).astype(BF16)
            d5_ref[2, rows, cs] =(da * uv).astype(BF16)
            sums = (sums[0] + csum(dz3 * a_m1), sums[1] + csum(dz3 * a), sums[2] + csum(dz3 * a_p1),
                    sums[3] + csum(dz3))

            dya = dy_ref[rows, CONV_W + p * LANES:CONV_W + (p + 1) * LANES].astype(F32)
            zav = za_ref[rows, cs].astype(F32)
            ov = o_ref[rows, cs].astype(F32)
            sa = _sigmoid(zav)
            d_o = dya * zav * sa
            sdo[p, rows, :] = d_o
            do_ref[rows, cs] = d_o.astype(BF16)
            d5_ref[4, rows, cs] =(dya * ov * (sa * (1.0 + zav * (1.0 - sa)))).astype(BF16)
            return sums, _split_dot(d_o * ov, _lanes_to_head(p, 1.0))

        def chunk(ci, carry):
            c0 = pl.multiple_of(ci * CHUNK, CHUNK)
            out, delta = [], jnp.zeros((CHUNK, LANES), F32)
            for p in range(N_PAIRS):
                sums, part = one_group(p, c0, carry[p])
                out.append(sums)
                delta = delta + part
            sdelta[0, pl.ds(c0, CHUNK), :] = delta
            return tuple(out)

        z = jnp.zeros((1, LANES), F32)
        all_sums = lax.fori_loop(0, tm // CHUNK, chunk, ((z, z, z, z),) * N_PAIRS)
        for p in range(N_PAIRS):
            for k in range(4):
                small_ref[k:k + 1, p * LANES:(p + 1) * LANES] += all_sums[p][k]
        delta_ref[...] = sdelta[0]
        for dil, wide_ref, stat_ref in zip(VIEW_DILATIONS, (do2_ref, do3_ref), (delta2_ref, delta3_ref)):
            _to_view(sdo, wide_ref, dil, BF16)
            _to_view(sdelta, stat_ref, dil, F32)

    pspec = pl.BlockSpec((tm, CONV_W), lambda i: (i, 0))
    hp, hn = _row_halo_specs(CONV_W, tm, S)
    piece = jax.ShapeDtypeStruct((S, CONV_W), BF16)
    outs = pl.pallas_call(
        body, name="gates_bwd", grid=(n_t,),
        out_shape=[jax.ShapeDtypeStruct((5, S, CONV_W), BF16), jax.ShapeDtypeStruct((8, CONV_W), F32)]
        + [_view_shape(S, d, ATTN_W, BF16) for d in DILATIONS] + [_view_shape(S, d, LANES, F32) for d in DILATIONS],
        in_specs=[pl.BlockSpec((tm, D_MODEL), lambda i: (i, 0)), hp, hn,
                  pspec, hp, hn, pspec, hp, hn, pspec, hp, hn, pspec, hp, hn, pspec, pspec,
                  pl.BlockSpec((3, CONV_W), lambda i: (0, 0)), pl.BlockSpec((1, CONV_W), lambda i: (0, 0))],
        out_specs=[pl.BlockSpec((5, tm, CONV_W), lambda i: (0, i, 0)), pl.BlockSpec((8, CONV_W), lambda i: (0, 0))]
        + [_view_spec(tm, d, ATTN_W) for d in DILATIONS] + [_view_spec(tm, d, LANES) for d in DILATIONS],
        scratch_shapes=[_tile_scratch(tm, ATTN_W), _tile_scratch(tm, LANES)]
        + [pltpu.VMEM((tm + 2 * HALO, CONV_W), BF16)] * 5,
        compiler_params=_params(("arbitrary",), 40),
    )(dy, dy, dy, u, u, u, gb, gb, gb, gc, gc, gc, zc, zc, zc, za, o, conv_w, conv_b.reshape(1, CONV_W))
    return outs[0], outs[1], outs[2:5], outs[5:8]


def _attn_bwd(g, dil, qv, kv, vv, dov, lv, dv_, rqv, rkv, gain, tables):
    L = qv.shape[0]
    T = min(1024, L)
    n_sub = T // QB + 1
    QW = T + 2 * HALF
    KW = T + 2 * QB

    def body(qm_ref, qp_ref, qn_ref, dom_ref, dop_ref, don_ref, km_ref, kp_ref, kn_ref, vm_ref, vp_ref, vn_ref,
             lm_ref, lp_ref, ln_ref, dm_ref, dp_ref, dn_ref, rqm_ref, rqp_ref, rqn_ref, rk_ref,
             gain_ref, bias_ref,
             dq_ref, dk_ref, dv_ref, ds_ref, a_ref,
             qwin, dowin, kwin, vwin, lwin, dwin, rqwin, dqwin, dkacc, dvacc,
             s_scr, d_scr, p_scr, ds_scr, q2_scr, do2_scr, dq_scr):
        r = pl.program_id(0)
        i = pl.program_id(1)

        @pl.when((r == 0) & (i == 0))
        def _():
            ds_ref[...] = jnp.zeros_like(ds_ref)
            a_ref[...] = jnp.zeros_like(a_ref)

        for win, (pr, mn, nx), h_rows in ((qwin, (qp_ref, qm_ref, qn_ref), HALF),
                                          (dowin, (dop_ref, dom_ref, don_ref), HALF),
                                          (lwin, (lp_ref, lm_ref, ln_ref), HALF),
                                          (dwin, (dp_ref, dm_ref, dn_ref), HALF),
                                          (rqwin, (rqp_ref, rqm_ref, rqn_ref), HALF),
                                          (kwin, (kp_ref, km_ref, kn_ref), QB),
                                          (vwin, (vp_ref, vm_ref, vn_ref), QB)):
            win[0:h_rows] = pr[...]
            win[h_rows:h_rows + T] = mn[...]
            win[h_rows + T:] = nx[...]
        dkacc[0:QB] = jnp.zeros((QB, ATTN_W), F32)
        dvacc[0:QB] = jnp.zeros((QB, ATTN_W), F32)

        lo = _lane((QB, LANES)) < HEAD_DIM
        same_head = _same_head()
        krow = lax.broadcasted_iota(jnp.int32, (KB, 1), 0)
        qcol = _lane((1, 2 * QB)) % QB
        row = lax.broadcasted_iota(jnp.int32, (QB, 1), 0)
        chan_lo = lax.broadcasted_iota(jnp.int32, (LANES, 1), 0) < HEAD_DIM

        cols = [slice(p * LANES, (p + 1) * LANES) for p in range(N_PAIRS)]
        nt = (((1,), (1,)), ((), ()))

        def scores(j):
            r0 = pl.multiple_of(j * QB, QB)
            for k, cs in enumerate(cols):
                qf = qwin[pl.ds(r0, QB), cs].astype(F32) * gain_ref[:, cs]
                q2_scr[k] = jnp.concatenate([jnp.where(lo, qf, 0.0), jnp.where(lo, 0.0, qf)], axis=0).astype(BF16)
                dov = dowin[pl.ds(r0, QB), cs]
                zero = jnp.zeros_like(dov)
                do2_scr[k] = jnp.concatenate([jnp.where(lo, dov, zero), jnp.where(lo, zero, dov)], axis=0)
                s_scr[k] = lax.dot_general(kwin[pl.ds(r0, KB), cs], q2_scr[k], nt, preferred_element_type=F32)
                d_scr[k] = lax.dot_general(vwin[pl.ds(r0, KB), cs], do2_scr[k], nt, preferred_element_type=F32)

        def sub(j):
            r0 = pl.multiple_of(j * QB, QB)
            kpos = i * T - QB + j * QB + krow
            qpos = i * T - HALF + j * QB + qcol
            valid = (kpos >= 0) & (kpos < L) & (qpos >= 0) & (qpos < L)
            owned_t = ((qpos >= i * T) & (qpos < i * T + T)).astype(F32)
            qpos_c = i * T - HALF + j * QB + row
            owned = ((qpos_c >= i * T) & (qpos_c < i * T + T)).astype(F32)
            lse_rows = lwin[pl.ds(r0, QB), :].T
            delta_rows = dwin[pl.ds(r0, QB), :].T
            rq_t = rqwin[pl.ds(r0, QB), :]
            for p in range(N_PAIRS):
                k = p
                bias2 = jnp.concatenate([bias_ref[0, 2 * p], bias_ref[0, 2 * p + 1]], axis=1)
                lse2 = jnp.concatenate([lse_rows[2 * p:2 * p + 1, :], lse_rows[2 * p + 1:2 * p + 2, :]], axis=1)
                delta2 = jnp.concatenate([delta_rows[2 * p:2 * p + 1, :], delta_rows[2 * p + 1:2 * p + 2, :]], axis=1)
                pt = jnp.where(valid, jnp.exp(s_scr[k] + bias2 - lse2), 0.0)
                dst = pt * (d_scr[k] - delta2)
                ds_ref[p] += dst * owned_t
                p_scr[p] = pt.astype(BF16)
                ds_scr[p] = dst.astype(BF16)
            for p, cs in enumerate(cols):
                k = p
                for acc_ref, lhs, rhs in ((dvacc, p_scr[p], do2_scr[k]), (dkacc, ds_scr[p], q2_scr[k])):
                    part = jnp.dot(lhs, rhs, preferred_element_type=F32)
                    acc_ref[pl.ds(r0, QB), cs] += part[0:QB]
                    acc_ref[pl.ds(r0 + QB, QB), cs] = part[QB:KB]
                dq_scr[p] = lax.dot_general(kwin[pl.ds(r0, KB), cs], ds_scr[p], (((0,), (0,)), ((), ())),
                                            preferred_element_type=F32)
            for p, cs in enumerate(cols):
                g2 = gain_ref[:, cs]
                q_f = qwin[pl.ds(r0, QB), cs].astype(F32)
                dqt2 = dq_scr[p]
                dqm2 = jnp.where(chan_lo, dqt2[:, 0:QB], dqt2[:, QB:2 * QB]).T
                a_ref[p] += (dqm2 * q_f * owned).reshape(QB // 8, 8, LANES).sum(axis=0)
                dqh = dqm2 * g2
                mean_c = _split_dot(dqh * q_f, same_head) * (1.0 / HEAD_DIM)
                rq2 = _split_dot(rq_t, _head_to_lanes(p))
                dqwin[pl.ds(r0, QB), cs] = (rq2 * (dqh - q_f * mean_c)).astype(BF16)

        def one_sub(j, carry):
            scores(j)
            sub(j)
            return carry

        lax.fori_loop(0, n_sub, one_sub, 0)
        dq_ref[...] = dqwin[HALF:HALF + T]

        def finish(j, carry):
            r0 = pl.multiple_of(j * QB, QB)
            rk_t = rk_ref[pl.ds(r0, QB), :]
            for p in range(N_PAIRS):
                cs = slice(p * LANES, (p + 1) * LANES)
                dkh = dkacc[pl.ds(QB + r0, QB), cs]
                k_f = km_ref[pl.ds(r0, QB), cs].astype(F32)
                mean_c = _split_dot(dkh * k_f, same_head) * (1.0 / HEAD_DIM)
                rk2 = _split_dot(rk_t, _head_to_lanes(p))
                dk_ref[pl.ds(r0, QB), cs] = (rk2 * (dkh - k_f * mean_c)).astype(BF16)
                dv_ref[pl.ds(r0, QB), cs] = dvacc[pl.ds(QB + r0, QB), cs].astype(BF16)
            return carry

        lax.fori_loop(0, T // QB, finish, 0)

    main = pl.BlockSpec((T, ATTN_W), lambda r, i: (i, r))
    smain = pl.BlockSpec((T, LANES), lambda r, i: (i, r))
    q_prev, q_next = _halo_specs(ATTN_W, HALF, T, L, lambda r: r)
    k_prev, k_next = _halo_specs(ATTN_W, QB, T, L, lambda r: r)
    s_prev, s_next = _halo_specs(LANES, HALF, T, L, lambda r: r)
    piece = jax.ShapeDtypeStruct((L, dil * ATTN_W), BF16)
    return pl.pallas_call(
        body, name=f"attn_bwd_d{dil}", grid=(dil, L // T),
        out_shape=(piece, piece, piece, jax.ShapeDtypeStruct((N_PAIRS, KB, 2 * QB), F32),
                   jax.ShapeDtypeStruct((N_PAIRS, 8, LANES), F32)),
        in_specs=[main, q_prev, q_next, main, q_prev, q_next, main, k_prev, k_next, main, k_prev, k_next,
                  smain, s_prev, s_next, smain, s_prev, s_next, smain, s_prev, s_next, smain,
                  pl.BlockSpec((1, ATTN_W), lambda r, i: (0, 0)),
                  pl.BlockSpec((1, N_HEADS, KB, QB), lambda r, i: (g, 0, 0, 0))],
        out_specs=(main, main, main, pl.BlockSpec((N_PAIRS, KB, 2 * QB), lambda r, i: (0, 0, 0)),
                   pl.BlockSpec((N_PAIRS, 8, LANES), lambda r, i: (0, 0, 0))),
        scratch_shapes=[pltpu.VMEM((QW, ATTN_W), BF16), pltpu.VMEM((QW, ATTN_W), BF16),
                        pltpu.VMEM((KW, ATTN_W), BF16), pltpu.VMEM((KW, ATTN_W), BF16),
                        pltpu.VMEM((QW, LANES), F32), pltpu.VMEM((QW, LANES), F32), pltpu.VMEM((QW, LANES), F32),
                        pltpu.VMEM((QW, ATTN_W), BF16),
                        pltpu.VMEM((KW, ATTN_W), F32), pltpu.VMEM((KW, ATTN_W), F32),
                        pltpu.VMEM((N_PAIRS, KB, 2 * QB), F32), pltpu.VMEM((N_PAIRS, KB, 2 * QB), F32),
                        pltpu.VMEM((N_PAIRS, KB, 2 * QB), BF16), pltpu.VMEM((N_PAIRS, KB, 2 * QB), BF16),
                        pltpu.VMEM((N_PAIRS, 2 * QB, LANES), BF16), pltpu.VMEM((N_PAIRS, 2 * QB, LANES), BF16),
                        pltpu.VMEM((N_PAIRS, LANES, 2 * QB), F32)],
        compiler_params=_params(("arbitrary", "arbitrary"), 56),
    )(qv, qv, qv, dov, dov, dov, kv, kv, kv, vv, vv, vv, lv, lv, lv, dv_, dv_, dv_, rqv, rqv, rqv, rkv, gain, tables)


D5_GROUPS = (0, 1, 2, 3, 7)
QKV_GROUPS = (4, 5, 6)


def _proj_bwd(x, d_out, norm_w, wg, d5, qkv_shares):
    S = x.shape[0]
    tm = 512
    flat = [share for p in qkv_shares for share in p]
    n_t = S // tm

    def body(*refs):
        x_ref, g_ref, nw_ref, wg_ref, d5_ref = refs[:5]
        piece_refs = refs[5:5 + len(flat)]
        gx_ref, dnw_ref, ht_ref, dqkv_ref, sview = refs[5 + len(flat):]
        i = pl.program_id(0)

        @pl.when(i == 0)
        def _():
            dnw_ref[...] = jnp.zeros_like(dnw_ref)

        xv = x_ref[...]
        r = lax.rsqrt(jnp.mean(xv * xv, axis=-1, keepdims=True) + EPS)
        xh = xv * r
        nw = nw_ref[...]
        ht_ref[...] = (xh * nw).T.astype(BF16)
        dh = jnp.zeros((tm, D_MODEL), F32)
        k = 0
        for j in range(8):
            if j in D5_GROUPS:
                dp = d5_ref[D5_GROUPS.index(j)]
            else:
                shares = qkv_shares[QKV_GROUPS.index(j)]
                dp = jnp.zeros((tm, CONV_W), F32)
                for t, (_, dil) in enumerate(shares):
                    if dil == 1:
                        dp = dp + piece_refs[k + t][...].astype(F32)
                    else:
                        _from_view(piece_refs[k + t], sview, dil)
                        dp = dp + _scratch_value(sview)
                dp = dp.astype(BF16)
                k += len(shares)
                dqkv_ref[QKV_GROUPS.index(j)] = dp
            dh = dh + lax.dot_general(dp, wg_ref[j], (((1,), (1,)), ((), ())), preferred_element_type=F32)
        dnw_ref[...] += (dh * xh).reshape(tm // 8, 8, D_MODEL).sum(axis=0)
        dxh = dh * nw
        mean_c = jnp.mean(dxh * xh, axis=-1, keepdims=True)
        gx_ref[...] = g_ref[...].astype(F32) + r * (dxh - xh * mean_c)

    tile = pl.BlockSpec((tm, D_MODEL), lambda i: (i, 0))
    return pl.pallas_call(
        body, name="proj_bwd", grid=(n_t,),
        out_shape=(jax.ShapeDtypeStruct((S, D_MODEL), F32), jax.ShapeDtypeStruct((8, D_MODEL), F32),
                   jax.ShapeDtypeStruct((D_MODEL, S), BF16), jax.ShapeDtypeStruct((3, S, CONV_W), BF16)),
        in_specs=[tile, tile, pl.BlockSpec((1, D_MODEL), lambda i: (0, 0)),
                  pl.BlockSpec(wg.shape, lambda i: (0, 0, 0), pipeline_mode=pl.Buffered(1)),
                  pl.BlockSpec((5, tm, CONV_W), lambda i: (0, i, 0))]
        + [_view_spec(tm, dil, CONV_W) for _, dil in flat],
        out_specs=(tile, pl.BlockSpec((8, D_MODEL), lambda i: (0, 0)),
                   pl.BlockSpec((D_MODEL, tm), lambda i: (0, i)), pl.BlockSpec((3, tm, CONV_W), lambda i: (0, i, 0))),
        scratch_shapes=[_tile_scratch(tm, CONV_W)],
        compiler_params=_params(("arbitrary",), 56),
    )(x, d_out, norm_w.reshape(1, D_MODEL), wg, d5, *[a for a, _ in flat])


def _dw_exchange(ht, d5, dqkv, gw_out, small):
    D, S = ht.shape
    tk = 2048
    n_t = S // tk
    me_outer = 4 * lax.axis_index("x") + 2 * lax.axis_index("y") + lax.axis_index("c")
    order = ((me_outer + 1 + jnp.arange(N_DEV, dtype=jnp.int32)) % N_DEV).astype(jnp.int32)
    in_d5 = ((order < 4) | (order == 7)).astype(jnp.int32)
    at_d5 = jnp.where(order < 4, order, jnp.where(order == 7, 4, 0)).astype(jnp.int32)
    at_qkv = jnp.where(in_d5 == 1, 0, order - 4).astype(jnp.int32)

    def body(in_d5_ref, at_d5_ref, at_qkv_ref, ht_hbm, d5_ref, dqkv_ref, gout_ref, sm_ref, rin_ref, rout_ref, rsm_ref,
             acc, sbuf, ht_vmem, in_send, in_recv, side_send, side_recv, local_sems, ht_sems):
        s = pl.program_id(0)
        t = pl.program_id(1)
        x, y, c = lax.axis_index("x"), lax.axis_index("y"), lax.axis_index("c")
        me = 4 * x + 2 * y + c

        def ht_copy(j):
            return pltpu.make_async_copy(ht_hbm.at[:, j * tk:(j + 1) * tk], ht_vmem.at[j], ht_sems.at[j])

        def in_copy(k):
            to = (me + 1 + k) % N_DEV
            return pltpu.make_async_remote_copy(
                src_ref=sbuf.at[k % 2], dst_ref=rin_ref.at[me], send_sem=in_send.at[k], recv_sem=in_recv.at[k],
                device_id=(to // 4, (to // 2) % 2, to % 2), device_id_type=MESH)

        def in_landing(k):
            frm = (me + 2 * N_DEV - 1 - k) % N_DEV
            return pltpu.make_async_remote_copy(
                src_ref=sbuf.at[0], dst_ref=rin_ref.at[frm], send_sem=in_send.at[k], recv_sem=in_recv.at[k],
                device_id=(x, y, c), device_id_type=MESH)

        def side_copies():
            local = [pltpu.make_async_copy(gout_ref.at[me], rout_ref.at[me], local_sems.at[0]),
                     pltpu.make_async_copy(sm_ref, rsm_ref.at[me], local_sems.at[1])]
            remote = []
            for k in range(1, N_DEV):
                px = 1 - x if k & 4 else x
                py = 1 - y if k & 2 else y
                pc = 1 - c if k & 1 else c
                peer = 4 * px + 2 * py + pc
                for a, (src, dst) in enumerate(((gout_ref.at[peer], rout_ref.at[me]), (sm_ref, rsm_ref.at[me]))):
                    remote.append(pltpu.make_async_remote_copy(
                        src_ref=src, dst_ref=dst, send_sem=side_send.at[a * 7 + k - 1],
                        recv_sem=side_recv.at[a * 7 + k - 1], device_id=(px, py, pc), device_id_type=MESH))
            return local, remote

        @pl.when((s == 0) & (t == 0))
        def _():
            for j in range(n_t):
                ht_copy(j).start()
            local, remote = side_copies()
            for cp in local + remote:
                cp.start()

        for j in range(n_t):
            @pl.when((s == 0) & (t == j))
            def _(j=j):
                ht_copy(j).wait()

        @pl.when(t == 0)
        def _():
            acc[...] = jnp.zeros_like(acc)

        dp = jnp.where(in_d5_ref[s] == 1, d5_ref[0], dqkv_ref[0])
        acc[...] += jnp.dot(ht_vmem[t], dp, preferred_element_type=F32)

        for k in range(N_DEV):
            @pl.when((s == k) & (t == n_t - 1))
            def _(k=k):
                if k >= 2:
                    in_copy(k - 2).wait_send()
                sbuf[k % 2] = acc[...].astype(BF16)
                if k < N_DEV - 1:
                    in_copy(k).start()
                else:
                    own = pltpu.make_async_copy(sbuf.at[k % 2], rin_ref.at[me], local_sems.at[2])
                    own.start()
                    in_copy(k - 1).wait_send()
                    for j in range(N_DEV - 1):
                        in_landing(j).wait_recv()
                    local, remote = side_copies()
                    for cp in remote:
                        cp.wait_recv()
                    for cp in remote:
                        cp.wait_send()
                    for cp in local:
                        cp.wait()
                    own.wait()

    hbm = pl.BlockSpec(memory_space=pl.ANY)
    grid_spec = pltpu.PrefetchScalarGridSpec(
        num_scalar_prefetch=3, grid=(N_DEV, n_t),
        in_specs=[hbm,
                  pl.BlockSpec((1, tk, CONV_W), lambda s, t, use, a5, a3: (a5[s], jnp.where(use[s] == 1, t, 0), 0)),
                  pl.BlockSpec((1, tk, CONV_W), lambda s, t, use, a5, a3: (a3[s], jnp.where(use[s] == 1, 0, t), 0)),
                  hbm, hbm],
        out_specs=(hbm, hbm, hbm),
        scratch_shapes=[pltpu.VMEM((D, CONV_W), F32), pltpu.VMEM((2, D, CONV_W), BF16),
                        pltpu.VMEM((n_t, D, tk), BF16),
                        pltpu.SemaphoreType.DMA((N_DEV - 1,)), pltpu.SemaphoreType.DMA((N_DEV - 1,)),
                        pltpu.SemaphoreType.DMA((14,)), pltpu.SemaphoreType.DMA((14,)),
                        pltpu.SemaphoreType.DMA((3,)), pltpu.SemaphoreType.DMA((n_t,))])
    return pl.pallas_call(
        body, name="dw_exchange", grid_spec=grid_spec,
        out_shape=(jax.ShapeDtypeStruct((N_DEV, D, CONV_W), BF16), jax.ShapeDtypeStruct(gw_out.shape, F32),
                   jax.ShapeDtypeStruct((N_DEV,) + small.shape, F32)),
        compiler_params=_params(("arbitrary", "arbitrary"), 52),
    )(in_d5, at_d5, at_qkv, ht, d5, dqkv, gw_out, small)


def _adamw_math(w, g, m, v):
    m2 = ADAM_B1 * m + (1.0 - ADAM_B1) * g
    v2 = ADAM_B2 * v + (1.0 - ADAM_B2) * (g * g)
    m_hat = m2 / (1.0 - ADAM_B1 ** ADAM_STEP)
    v_hat = v2 / (1.0 - ADAM_B2 ** ADAM_STEP)
    delta = -ADAM_LR * (m_hat / (jnp.sqrt(v_hat) + ADAM_EPS) + ADAM_WD * w)
    return delta, m2, v2


def _adamw_sharded(name, parts, w, m, v, rows):
    R, C = w.shape

    def body(p_ref, w_ref, m_ref, v_ref, g_ref, d_ref, m2_ref, v2_ref):
        g = p_ref[0].astype(F32)
        for s in range(1, N_DEV):
            g = g + p_ref[s].astype(F32)
        g_ref[...] = g
        d_ref[...], m2_ref[...], v2_ref[...] = _adamw_math(w_ref[...], g, m_ref[...], v_ref[...])

    spec = pl.BlockSpec((rows, C), lambda i: (i, 0))
    out = jax.ShapeDtypeStruct((R, C), F32)
    return pl.pallas_call(
        body, name=name, grid=(R // rows,),
        out_shape=(out,) * 4,
        in_specs=[pl.BlockSpec((N_DEV, rows, C), lambda i: (0, i, 0)), spec, spec, spec],
        out_specs=(spec,) * 4,
        compiler_params=_params(("arbitrary",), 40),
    )(parts, w, m, v)


def _adamw_small(parts, w, m, v):
    R = w.shape[0]

    def body(p_ref, w_ref, m_ref, v_ref, g_ref, d_ref, m2_ref, v2_ref):
        g = p_ref[0]
        for s in range(1, N_DEV):
            g = g + p_ref[s]
        g_ref[...] = g
        d_ref[...], m2_ref[...], v2_ref[...] = _adamw_math(w_ref[...], g, m_ref[...], v_ref[...])

    vm = pl.BlockSpec(memory_space=pltpu.VMEM)
    out = jax.ShapeDtypeStruct((R, LANES), F32)
    return pl.pallas_call(
        body, name="adamw_small", out_shape=(out,) * 4, in_specs=[vm] * 4, out_specs=(vm,) * 4,
    )(parts, w, m, v)


def _qk_gain_grads(a_parts, q_norm_w, k_norm_w, bias_g):
    def body(a1_ref, a2_ref, a3_ref, qw_ref, kw_ref, bg_ref, out_ref):
        tot = jnp.zeros((8, LANES), F32)
        for a_ref in (a1_ref, a2_ref, a3_ref):
            for p in range(N_PAIRS):
                tot = tot + a_ref[p]
        col = jnp.sum(tot, axis=0, keepdims=True)
        a64 = col[:, 0:HEAD_DIM] + col[:, HEAD_DIM:LANES]
        out_ref[...] = jnp.zeros_like(out_ref)
        out_ref[0:1, 0:HEAD_DIM] = LOGIT_SCALE * kw_ref[...] * a64
        out_ref[1:2, 0:HEAD_DIM] = LOGIT_SCALE * qw_ref[...] * a64
        for h in range(N_HEADS):
            r0, c0 = 2 + h // 4, (h % 4) * N_BUCKETS
            out_ref[r0:r0 + 1, c0:c0 + N_BUCKETS] = bg_ref[h // 2, h % 2:h % 2 + 1, 0:N_BUCKETS]

    vm = pl.BlockSpec(memory_space=pltpu.VMEM)
    return pl.pallas_call(
        body, name="qk_gain_grads", out_shape=jax.ShapeDtypeStruct((8, LANES), F32),
        in_specs=[vm] * 6, out_specs=vm,
    )(*a_parts, q_norm_w.reshape(1, HEAD_DIM), k_norm_w.reshape(1, HEAD_DIM), bias_g)


SMALL_ROWS = 32


def _pack_small(norm_w, conv_b, q_norm_w, k_norm_w, rel_bias, conv_w_full):
    pad64 = lambda a: jnp.pad(a, (0, LANES - HEAD_DIM)).reshape(1, LANES)
    return jnp.concatenate([
        norm_w.reshape(8, LANES), conv_b.reshape(4, LANES), pad64(q_norm_w), pad64(k_norm_w),
        rel_bias.T.reshape(2, LANES), conv_w_full.reshape(12, LANES), jnp.zeros((4, LANES), F32)], axis=0)


def _unpack_small(s):
    return (s[0:8].reshape(D_MODEL), s[8:12].reshape(CONV_W), s[12, 0:HEAD_DIM], s[13, 0:HEAD_DIM],
            s[14:16].reshape(N_HEADS, N_BUCKETS).T, s[16:28].reshape(3, CONV_W))


def kernel(x, norm_w, w_in, conv_w, conv_b, q_norm_w, k_norm_w, rel_bias, w_out, loss_target, m_norm_w, m_w_in, m_conv_w, m_conv_b, m_q_norm_w, m_k_norm_w, m_rel_bias, m_w_out, v_norm_w, v_w_in, v_conv_w, v_conv_b, v_q_norm_w, v_k_norm_w, v_rel_bias, v_w_out):
    S = x.shape[1]
    x2 = x.reshape(S, D_MODEL)
    tgt = loss_target.reshape(S, D_MODEL)
    me = 4 * lax.axis_index("x") + 2 * lax.axis_index("y") + lax.axis_index("c")
    shard_w = CONV_W // N_DEV

    cw_pad = jnp.pad(conv_w, ((0, 5), (0, LANES - shard_w)))
    wg, wo_g, cw_g = _ag_weights(w_in, w_out, cw_pad)
    wo = wo_g.reshape(D_MODEL, D_MODEL)
    conv_w_full = cw_g[:, 0:3, 0:shard_w].transpose(1, 0, 2).reshape(3, CONV_W)

    bmat_t = jnp.asarray(np.stack([_bucket_matrix(d).T for d in DILATIONS]))
    tables = _bias_tables(rel_bias, bmat_t)
    gain = jnp.tile(q_norm_w * k_norm_w * LOGIT_SCALE, N_HEADS).reshape(1, ATTN_W)

    u, gb, gc, zc, za, q_hat, k_hat, v, rq, rk = _fwd_proj(x2, norm_w, wg)
    o_parts, lse_parts = [], []
    for g, dil in enumerate(DILATIONS):
        o_g, lse_g = _attn_fwd(g, dil, q_hat[g], k_hat[g], v[g], gain, tables)
        o_parts.append(o_g)
        lse_parts.append(lse_g)
    y, o, lse = _combine_gates(o_parts, lse_parts, u, gb, gc, zc, za, conv_w_full, conv_b)
    d_out, dy, gwo_part, loss_part = _out_proj(y, x2, tgt, wo)
    loss_mine = jnp.sum(loss_part) * (0.5 / D_MODEL)

    d5, conv_small, d_o, delta = _gates_bwd(dy, u, gb, gc, zc, za, o, conv_w_full, conv_b)
    dqs, dks, dvs, ds_parts, a_parts = [], [], [], [], []
    for g, dil in enumerate(DILATIONS):
        dq, dk, dv, ds_sum, a_sum = _attn_bwd(g, dil, q_hat[g], k_hat[g], v[g], d_o[g], lse[g], delta[g], rq[g], rk[g],
                                              gain, tables)
        dqs.append((dq, dil))
        dks.append((dk, dil))
        dvs.append((dv, dil))
        ds_parts.append(ds_sum)
        a_parts.append(a_sum)
    grad_x, gnw_part, h_t, dqkv = _proj_bwd(x2, d_out, norm_w, wg, d5, [dqs, dks, dvs])
    bias_g = _bias_grad(ds_parts, bmat_t)
    qk_small = _qk_gain_grads(a_parts, q_norm_w, k_norm_w, bias_g)

    small_part = jnp.concatenate([
        gnw_part.sum(axis=0).reshape(8, LANES), conv_small[3].reshape(4, LANES), qk_small[0:4],
        conv_small[0:3].reshape(12, LANES), jnp.zeros((4, LANES), F32).at[0, 0].set(loss_mine)], axis=0)
    r_in, r_out, r_small = _dw_exchange(h_t, d5, dqkv, gwo_part.reshape(N_DEV, D_MODEL // N_DEV, D_MODEL), small_part)

    g_win, d_win, m_win, v_win = _adamw_sharded("adamw_w_in", r_in, w_in, m_w_in, v_w_in, 128)
    g_wo, d_wo, m_wo, v_wo = _adamw_sharded("adamw_w_out", r_out, w_out, m_w_out, v_w_out, 32)

    def full_conv(a):
        return jnp.zeros((3, CONV_W), F32).at[:, 0:shard_w].set(a)

    packs = [_pack_small(nw_, cb_, qw_, kw_, rb_, full_conv(cw_)) for nw_, cb_, qw_, kw_, rb_, cw_ in (
        (norm_w, conv_b, q_norm_w, k_norm_w, rel_bias, conv_w),
        (m_norm_w, m_conv_b, m_q_norm_w, m_k_norm_w, m_rel_bias, m_conv_w),
        (v_norm_w, v_conv_b, v_q_norm_w, v_k_norm_w, v_rel_bias, v_conv_w))]
    r_small_mine = r_small.at[:, 16:28, :].set(
        jnp.pad(lax.dynamic_slice_in_dim(r_small[:, 16:28, :].reshape(N_DEV, 3, CONV_W), me * shard_w, shard_w, axis=2),
                ((0, 0), (0, 0), (0, CONV_W - shard_w))).reshape(N_DEV, 12, LANES))
    outs_small = _adamw_small(r_small_mine, *packs)
    g_s, d_s, m_s, v_s = [_unpack_small(t) for t in outs_small]
    loss = outs_small[0][28, 0]

    def leaves(small, big_in, big_out):
        nw_, cb_, qw_, kw_, rb_, cwf = small
        return (nw_, big_in, cwf[:, 0:shard_w], cb_, qw_, kw_, rb_, big_out)

    return (loss, grad_x.reshape(x.shape),
            *leaves(g_s, g_win, g_wo), *leaves(d_s, d_win, d_wo), *leaves(m_s, m_win, m_wo), *leaves(v_s, v_win, v_wo))
```

```python
import math

import numpy as np
import jax
import jax.numpy as jnp
from jax import lax
from jax.experimental import pallas as pl
from jax.experimental.pallas import tpu as pltpu

F32 = jnp.float32
BF16 = jnp.bfloat16

N_DEV = 8
D_MODEL = 1024
CONV_W = 512
ATTN_W = 512
N_HEADS = 8
HEAD_DIM = 64
N_PAIRS = N_HEADS // 2
LANES = 128
HALF = 64
QB = 128
KB = QB + 2 * HALF
DILATIONS = (1, 4, 16)
N_BUCKETS = 32
MAX_DISTANCE = 1024
EPS = 1e-6
NEG = -1e30
LOGIT_SCALE = HEAD_DIM ** -0.5

ADAM_LR = 0.001
ADAM_B1 = 0.9
ADAM_B2 = 0.999
ADAM_EPS = 1e-08
ADAM_WD = 0.01
ADAM_STEP = 10

MESH = pl.DeviceIdType.MESH
MIB = 1024 * 1024


def _params(semantics, vmem_mib):
    return pltpu.CompilerParams(dimension_semantics=semantics, vmem_limit_bytes=vmem_mib * MIB)


def _lane(shape):
    return lax.broadcasted_iota(jnp.int32, shape, len(shape) - 1)


def _sigmoid(z):
    return 1.0 / (1.0 + jnp.exp(-z))


def _split_dot(x, w):
    hi = x.astype(BF16)
    lo = (x - hi.astype(F32)).astype(BF16)
    return jnp.dot(hi, w, preferred_element_type=F32) + jnp.dot(lo, w, preferred_element_type=F32)


def _same_head():
    r = lax.broadcasted_iota(jnp.int32, (LANES, LANES), 0) // HEAD_DIM
    c = lax.broadcasted_iota(jnp.int32, (LANES, LANES), 1) // HEAD_DIM
    return (r == c).astype(BF16)


def _head_to_lanes(p):
    r = lax.broadcasted_iota(jnp.int32, (LANES, LANES), 0)
    c = lax.broadcasted_iota(jnp.int32, (LANES, LANES), 1) // HEAD_DIM
    return (r == 2 * p + c).astype(BF16)


def _lanes_to_head(p, scale):
    r = lax.broadcasted_iota(jnp.int32, (LANES, LANES), 0) // HEAD_DIM
    c = lax.broadcasted_iota(jnp.int32, (LANES, LANES), 1)
    return jnp.where(c == 2 * p + r, scale, 0.0).astype(BF16)


VIEW_DILATIONS = DILATIONS[1:]


def _view_shape(S, dil, width, dtype):
    return jax.ShapeDtypeStruct((S // dil, dil * width), dtype)


def _view_spec(tm, dil, width):
    return pl.BlockSpec((tm // dil, dil * width), lambda i: (i, 0))


def _tile_scratch(tm, width):
    return pltpu.VMEM((width // LANES, tm, LANES), F32)


def _scratch_value(scr_ref):
    n = scr_ref.shape[0]
    return scr_ref[0] if n == 1 else jnp.concatenate([scr_ref[c] for c in range(n)], axis=1)


def _to_view(scr_ref, out_ref, dil, dtype):
    n, tm, _ = scr_ref.shape
    for r in range(dil):
        for c in range(n):
            col = (r * n + c) * LANES
            out_ref[:, col:col + LANES] = scr_ref[c, pl.ds(r, tm // dil, stride=dil), :].astype(dtype)


def _from_view(blk_ref, scr_ref, dil):
    n, tm, _ = scr_ref.shape
    for r in range(dil):
        for c in range(n):
            col = (r * n + c) * LANES
            scr_ref[c, pl.ds(r, tm // dil, stride=dil), :] = blk_ref[:, col:col + LANES].astype(F32)


def _bucket_matrix(dilation):
    rel = np.arange(KB)[None, :] - HALF - np.arange(QB)[:, None]
    band = np.abs(rel) <= HALF
    dist = np.clip(rel, -HALF, HALF) * dilation
    half_b = N_BUCKETS // 2
    max_exact = half_b // 2
    ret = np.where(dist > 0, half_b, 0)
    n = np.abs(dist)
    nf = np.maximum(n, 1).astype(np.float32)
    large = max_exact + (np.log(nf / np.float32(max_exact)) / np.float32(math.log(MAX_DISTANCE / max_exact))
                         * np.float32(half_b - max_exact)).astype(np.int32)
    large = np.minimum(large, half_b - 1)
    bucket = ret + np.where(n < max_exact, n, large)
    return np.where(band, bucket, -1).astype(np.int32)


def _ag_weights(w_in, w_out, cw_pad):
    n_arr = 3

    def body(win_ref, wout_ref, cw_ref, gin_ref, gout_ref, gcw_ref, send_sems, recv_sems):
        x, y, c = lax.axis_index("x"), lax.axis_index("y"), lax.axis_index("c")
        me = (x, y, c)
        sibling = (x, y, 1 - c)
        chips = [(1 - x, y), (x, 1 - y), (1 - x, 1 - y)]
        arrays = (gin_ref, gout_ref, gcw_ref)

        def slot(px, py, pc):
            return 4 * px + 2 * py + pc

        gin_ref[slot(*me)] = win_ref[...].astype(BF16)
        gout_ref[slot(*me)] = wout_ref[...].astype(BF16)
        gcw_ref[slot(*me)] = cw_ref[...]

        def copy(a, k, block, to):
            ref = arrays[a].at[slot(*block)]
            return pltpu.make_async_remote_copy(
                src_ref=ref, dst_ref=ref, send_sem=send_sems.at[a * 7 + k], recv_sem=recv_sems.at[a * 7 + k],
                device_id=to, device_id_type=MESH)

        first = [copy(a, 0, me, sibling) for a in range(n_arr)]
        for j, chip in enumerate(chips):
            first += [copy(a, 1 + j, me, (*chip, c)) for a in range(n_arr)]
        for cp in first:
            cp.start()
        passed = []
        for j, chip in enumerate(chips):
            for a in range(n_arr):
                copy(a, 1 + j, (*chip, c), me).wait_recv()
            for a in range(n_arr):
                cp = copy(a, 4 + j, (*chip, c), sibling)
                cp.start()
                passed.append(cp)
        for a in range(n_arr):
            copy(a, 0, sibling, me).wait_recv()
        for j, chip in enumerate(chips):
            for a in range(n_arr):
                copy(a, 4 + j, (*chip, 1 - c), me).wait_recv()
        for cp in first + passed:
            cp.wait_send()

    vm = pl.BlockSpec(memory_space=pltpu.VMEM)
    return pl.pallas_call(
        body, name="ag_weights",
        out_shape=(jax.ShapeDtypeStruct((N_DEV,) + w_in.shape, BF16),
                   jax.ShapeDtypeStruct((N_DEV,) + w_out.shape, BF16),
                   jax.ShapeDtypeStruct((N_DEV,) + cw_pad.shape, F32)),
        in_specs=[vm, vm, vm], out_specs=(vm, vm, vm),
        scratch_shapes=[pltpu.SemaphoreType.DMA((n_arr * 7,)), pltpu.SemaphoreType.DMA((n_arr * 7,))],
        compiler_params=pltpu.CompilerParams(vmem_limit_bytes=40 * MIB),
    )(w_in, w_out, cw_pad)


def _fwd_proj(x, norm_w, wg):
    S = x.shape[0]
    tm = 512

    def body(x_ref, nw_ref, wg_ref, u_ref, gb_ref, gc_ref, zc_ref, za_ref, *rest):
        q_refs, k_refs, v_refs, rq_refs, rk_refs = (rest[3 * n:3 * n + 3] for n in range(5))
        scr, rscr = rest[15:]
        xv = x_ref[...]
        r = lax.rsqrt(jnp.mean(xv * xv, axis=-1, keepdims=True) + EPS)
        h = (xv * r * nw_ref[...]).astype(BF16)
        lane = _lane((tm, LANES))
        lo = lane < HEAD_DIM
        plain = {0: u_ref, 1: gb_ref, 2: gc_ref, 3: zc_ref, 7: za_ref}
        normed = {4: (q_refs, rq_refs), 5: (k_refs, rk_refs)}

        def emit(src, refs, dtype):
            refs[0][...] = _scratch_value(src).astype(dtype)
            for dil, ref in zip(VIEW_DILATIONS, refs[1:]):
                _to_view(src, ref, dil, dtype)

        acc_next = jnp.dot(h, wg_ref[0], preferred_element_type=F32)
        for j in range(8):
            acc = acc_next
            if j < 7:
                acc_next = jnp.dot(h, wg_ref[j + 1], preferred_element_type=F32)
            if j in plain:
                plain[j][...] = acc.astype(BF16)
                continue
            if j == 6:
                for p in range(N_PAIRS):
                    scr[p] = acc[:, p * LANES:(p + 1) * LANES]
                emit(scr, v_refs, BF16)
                continue
            out_refs, r_refs = normed[j]
            r_tile = jnp.zeros((tm, LANES), F32)
            for p in range(N_PAIRS):
                blk = acc[:, p * LANES:(p + 1) * LANES]
                sq = blk * blk
                s_lo = jnp.sum(jnp.where(lo, sq, 0.0), axis=-1, keepdims=True)
                s_hi = jnp.sum(jnp.where(lo, 0.0, sq), axis=-1, keepdims=True)
                r_lo = lax.rsqrt(s_lo * (1.0 / HEAD_DIM) + EPS)
                r_hi = lax.rsqrt(s_hi * (1.0 / HEAD_DIM) + EPS)
                scr[p] = blk * jnp.where(lo, r_lo, r_hi)
                r_tile = jnp.where(lane == 2 * p, r_lo, r_tile)
                r_tile = jnp.where(lane == 2 * p + 1, r_hi, r_tile)
            rscr[0] = r_tile
            emit(scr, out_refs, BF16)
            emit(rscr, r_refs, F32)

    piece = jax.ShapeDtypeStruct((S, CONV_W), BF16)
    pspec = pl.BlockSpec((tm, CONV_W), lambda i: (i, 0))
    wide = [_view_shape(S, d, ATTN_W, BF16) for d in DILATIONS]
    wide_specs = [_view_spec(tm, d, ATTN_W) for d in DILATIONS]
    stat = [_view_shape(S, d, LANES, F32) for d in DILATIONS]
    stat_specs = [_view_spec(tm, d, LANES) for d in DILATIONS]
    outs = pl.pallas_call(
        body, name="fwd_proj", grid=(S // tm,),
        out_shape=[piece] * 5 + wide * 3 + stat * 2,
        in_specs=[pl.BlockSpec((tm, D_MODEL), lambda i: (i, 0)),
                  pl.BlockSpec((1, D_MODEL), lambda i: (0, 0)),
                  pl.BlockSpec(wg.shape, lambda i: (0, 0, 0), pipeline_mode=pl.Buffered(1))],
        out_specs=[pspec] * 5 + wide_specs * 3 + stat_specs * 2,
        scratch_shapes=[_tile_scratch(tm, ATTN_W), _tile_scratch(tm, LANES)],
        compiler_params=_params(("arbitrary",), 48),
    )(x, norm_w.reshape(1, D_MODEL), wg)
    u, gb, gc, zc, za = outs[:5]
    q_hat, k_hat, v, rq, rk = (outs[5 + 3 * n:8 + 3 * n] for n in range(5))
    return u, gb, gc, zc, za, q_hat, k_hat, v, rq, rk


def _bias_tables(rel_bias, bmat_t):
    def body(rb_ref, b_ref, out_ref):
        h = pl.program_id(1)
        b = b_ref[0]
        t = jnp.full((KB, QB), NEG, F32)
        for bk in range(N_BUCKETS):
            t = jnp.where(b == bk, rb_ref[bk, h], t)
        out_ref[0, 0] = t

    return pl.pallas_call(
        body, name="bias_tables", grid=(len(DILATIONS), N_HEADS),
        out_shape=jax.ShapeDtypeStruct((len(DILATIONS), N_HEADS, KB, QB), F32),
        in_specs=[pl.BlockSpec(memory_space=pltpu.SMEM),
                  pl.BlockSpec((1, KB, QB), lambda g, h: (g, 0, 0))],
        out_specs=pl.BlockSpec((1, 1, KB, QB), lambda g, h: (g, h, 0, 0)),
        compiler_params=_params(("arbitrary", "arbitrary"), 16),
    )(rel_bias, bmat_t)


def _bias_grad(ds_acc, bmat_t):
    present = [sorted(set(_bucket_matrix(d).ravel().tolist()) - {-1}) for d in DILATIONS]

    def body(ds_ref, b_ref, out_ref):
        lane = _lane((1, LANES))
        row = lax.broadcasted_iota(jnp.int32, (8, LANES), 0)
        out = jnp.zeros((8, LANES), F32)
        for e in range(2):
            vec = jnp.zeros((1, LANES), F32)
            for g in range(len(DILATIONS)):
                b = b_ref[g]
                ds = ds_ref[g, 0, :, e * QB:(e + 1) * QB]
                for bk in present[g]:
                    s = jnp.sum(jnp.where(b == bk, ds, 0.0), axis=-1, keepdims=True)
                    s = jnp.sum(s, axis=0, keepdims=True)
                    vec = vec + jnp.where(lane == bk, s, 0.0)
            out = jnp.where(row == e, vec, out)
        out_ref[0] = out

    return pl.pallas_call(
        body, name="bias_grad", grid=(N_PAIRS,),
        out_shape=jax.ShapeDtypeStruct((N_PAIRS, 8, LANES), F32),
        in_specs=[pl.BlockSpec((len(DILATIONS), 1, KB, 2 * QB), lambda p: (0, p, 0, 0)),
                  pl.BlockSpec((len(DILATIONS), KB, QB), lambda p: (0, 0, 0))],
        out_specs=pl.BlockSpec((1, 8, LANES), lambda p: (p, 0, 0)),
        compiler_params=_params(("arbitrary",), 16),
    )(ds_acc, bmat_t)


def _halo_specs(width, rows, T, L, cols_of):
    per = T // rows
    last = L // rows - 1
    prev = pl.BlockSpec((rows, width), lambda r, i: (jnp.maximum(i * per - 1, 0), cols_of(r)))
    nxt = pl.BlockSpec((rows, width), lambda r, i: (jnp.minimum((i + 1) * per, last), cols_of(r)))
    return prev, nxt


def _attn_fwd(g, dil, qv, kv, vv, gain, tables):
    L = qv.shape[0]
    T = min(1024, L)
    n_sub = T // QB

    n_i = L // T
    n_steps = dil * n_i
    N_SLOTS = 3

    def body(q_hbm, k_hbm, v_hbm, gain_ref, bias_ref, o_ref, lse_ref, qbuf, kwins, vwins, s_scr, p_scr, sems):
        i = pl.program_id(1)
        step = pl.program_id(0) * n_i + i
        slot = step % N_SLOTS

        def transfers(st, act):
            r_, i_ = st // n_i, st % n_i
            sl = st % N_SLOTS
            col = pl.ds(pl.multiple_of(r_ * ATTN_W, ATTN_W), ATTN_W)
            row0 = i_ * T
            act(pltpu.make_async_copy(q_hbm.at[pl.ds(row0, T), col], qbuf.at[sl], sems.at[sl, 0]))
            for a, (src, win) in enumerate(((k_hbm, kwins), (v_hbm, vwins))):
                act(pltpu.make_async_copy(src.at[pl.ds(row0, T), col], win.at[sl, pl.ds(HALF, T)], sems.at[sl, 1 + 3 * a]))

                @pl.when(i_ > 0)
                def _(src=src, win=win, a=a):
                    act(pltpu.make_async_copy(src.at[pl.ds(row0 - HALF, HALF), col], win.at[sl, pl.ds(0, HALF)],
                                              sems.at[sl, 2 + 3 * a]))

                @pl.when(i_ < n_i - 1)
                def _(src=src, win=win, a=a):
                    act(pltpu.make_async_copy(src.at[pl.ds(row0 + T, HALF), col], win.at[sl, pl.ds(HALF + T, HALF)],
                                              sems.at[sl, 3 + 3 * a]))

        @pl.when(step == 0)
        def _():
            for st in range(min(N_SLOTS - 1, n_steps)):
                transfers(st, lambda cp: cp.start())

        @pl.when(step + N_SLOTS - 1 < n_steps)
        def _():
            transfers(step + N_SLOTS - 1, lambda cp: cp.start())

        transfers(step, lambda cp: cp.wait())
        zero_halo = jnp.zeros((HALF, ATTN_W), BF16)

        @pl.when(i == 0)
        def _():
            kwins[slot, 0:HALF] = zero_halo
            vwins[slot, 0:HALF] = zero_halo

        @pl.when(i == n_i - 1)
        def _():
            kwins[slot, HALF + T:] = zero_halo
            vwins[slot, HALF + T:] = zero_halo

        q_ref, kwin, vwin = qbuf.at[slot], kwins.at[slot], vwins.at[slot]
        lo = _lane((QB, LANES)) < HEAD_DIM
        krow = lax.broadcasted_iota(jnp.int32, (KB, 1), 0)
        chan_lo = lax.broadcasted_iota(jnp.int32, (LANES, 1), 0) < HEAD_DIM
        hrow = lax.broadcasted_iota(jnp.int32, (LANES, QB), 0)

        def sub(j, carry):
            r0 = pl.multiple_of(j * QB, QB)
            kpos = i * T + j * QB - HALF + krow
            kvalid = (kpos >= 0) & (kpos < L)
            lse_rows = jnp.zeros((LANES, QB), F32)
            cols = [slice(p * LANES, (p + 1) * LANES) for p in range(N_PAIRS)]
            for p, cs in enumerate(cols):
                qf = q_ref[pl.ds(r0, QB), cs].astype(F32) * gain_ref[:, cs]
                q2 = jnp.concatenate([jnp.where(lo, qf, 0.0), jnp.where(lo, 0.0, qf)], axis=0).astype(BF16)
                s_scr[p] = lax.dot_general(kwin[pl.ds(r0, KB), cs], q2, (((1,), (1,)), ((), ())),
                                           preferred_element_type=F32)
            inv_l = []
            for p in range(N_PAIRS):
                bias2 = jnp.concatenate([bias_ref[0, 2 * p], bias_ref[0, 2 * p + 1]], axis=1)
                logits = jnp.where(kvalid, s_scr[p] + bias2, NEG)
                m = jnp.max(logits, axis=0, keepdims=True)
                pt = jnp.exp(logits - m)
                l = jnp.sum(pt, axis=0, keepdims=True)
                p_scr[p] = pt.astype(BF16)
                inv_l.append(1.0 / l)
                lse2 = m + jnp.log(l)
                lse_rows = jnp.where(hrow == 2 * p, lse2[:, 0:QB], lse_rows)
                lse_rows = jnp.where(hrow == 2 * p + 1, lse2[:, QB:2 * QB], lse_rows)
            for p, cs in enumerate(cols):
                s_scr[p, 0:LANES, :] = lax.dot_general(vwin[pl.ds(r0, KB), cs], p_scr[p], (((0,), (0,)), ((), ())),
                                                       preferred_element_type=F32) * inv_l[p]
            for p, cs in enumerate(cols):
                ot2 = s_scr[p, 0:LANES, :]
                ot = jnp.where(chan_lo, ot2[:, 0:QB], ot2[:, QB:2 * QB])
                o_ref[pl.ds(r0, QB), cs] = ot.T.astype(BF16)
            lse_ref[pl.ds(r0, QB), :] = lse_rows.T
            return carry

        lax.fori_loop(0, n_sub, sub, 0)

    main = pl.BlockSpec((T, ATTN_W), lambda r, i: (i, r))
    hbm = pl.BlockSpec(memory_space=pl.ANY)
    o_g, lse_g = pl.pallas_call(
        body, name=f"attn_fwd_d{dil}", grid=(dil, n_i),
        out_shape=(jax.ShapeDtypeStruct((L, dil * ATTN_W), BF16), jax.ShapeDtypeStruct((L, dil * LANES), F32)),
        in_specs=[hbm, hbm, hbm,
                  pl.BlockSpec((1, ATTN_W), lambda r, i: (0, 0)),
                  pl.BlockSpec((1, N_HEADS, KB, QB), lambda r, i: (g, 0, 0, 0))],
        out_specs=(main, pl.BlockSpec((T, LANES), lambda r, i: (i, r))),
        scratch_shapes=[pltpu.VMEM((N_SLOTS, T, ATTN_W), BF16),
                        pltpu.VMEM((N_SLOTS, T + 2 * HALF, ATTN_W), BF16),
                        pltpu.VMEM((N_SLOTS, T + 2 * HALF, ATTN_W), BF16),
                        pltpu.VMEM((N_PAIRS, KB, 2 * QB), F32), pltpu.VMEM((N_PAIRS, KB, 2 * QB), BF16),
                        pltpu.SemaphoreType.DMA((N_SLOTS, 7))],
        compiler_params=_params(("arbitrary", "arbitrary"), 40),
    )(qv, kv, vv, gain, tables)
    return o_g, lse_g


HALO = 16
CHUNK = 128


def _row_halo_specs(width, tm, S, col=0):
    rows = HALO
    per = tm // rows
    last = S // rows - 1
    prev = pl.BlockSpec((rows, width), lambda i: (jnp.maximum(i * per - 1, 0), col))
    nxt = pl.BlockSpec((rows, width), lambda i: (jnp.minimum((i + 1) * per, last), col))
    return prev, nxt


def _shift_rows(a, prev_row, next_row, tm):
    row = lax.broadcasted_iota(jnp.int32, a.shape, 0)
    a_m1 = jnp.where(row == 0, prev_row, pltpu.roll(a, 1, 0))
    a_p1 = jnp.where(row == tm - 1, next_row, pltpu.roll(a, tm - 1, 0))
    return a_m1, a_p1


def _combine_gates(o_parts, lse_parts, u, gb, gc, zc, za, conv_w, conv_b):
    S = u.shape[0]
    tm = 512
    n_t = S // tm

    def body(o1, o2, o3, l1, l2, l3, u_ref, up_ref, un_ref, gb_ref, gc_ref, gcp_ref, gcn_ref, zc_ref, za_ref,
             cw_ref, cb_ref, y_ref, o_ref, lse_ref, lse2_ref, lse3_ref, so2, so3, sl2, sl3, slse):
        i = pl.program_id(0)
        for blk, scr, dil in ((o2, so2, DILATIONS[1]), (o3, so3, DILATIONS[2]),
                              (l2, sl2, DILATIONS[1]), (l3, sl3, DILATIONS[2])):
            _from_view(blk, scr, dil)
        ls = [l1[...], sl2[0], sl3[0]]
        lmax = jnp.maximum(jnp.maximum(ls[0], ls[1]), ls[2])
        es = [jnp.exp(l - lmax) for l in ls]
        den = es[0] + es[1] + es[2]
        slse[0] = lmax + jnp.log(den)
        lse_ref[...] = slse[0]
        _to_view(slse, lse2_ref, DILATIONS[1], F32)
        _to_view(slse, lse3_ref, DILATIONS[2], F32)
        inv = 1.0 / den
        ws = [e * inv for e in es]
        for p in range(N_PAIRS):
            cs = slice(p * LANES, (p + 1) * LANES)
            acc = jnp.zeros((tm, LANES), F32)
            spread = _head_to_lanes(p)
            for w, o_g in zip(ws, (o1[:, cs].astype(F32), so2[p], so3[p])):
                acc = acc + _split_dot(w, spread) * o_g
            o_ref[:, cs] = acc.astype(BF16)
            za = za_ref[:, cs].astype(F32)
            y_ref[:, CONV_W + p * LANES:CONV_W + (p + 1) * LANES] = (acc * za * _sigmoid(za)).astype(BF16)
        a = gc_ref[...].astype(F32) * u_ref[...].astype(F32)
        a_prev = gcp_ref[15:16, :].astype(F32) * up_ref[15:16, :].astype(F32) * (i > 0).astype(F32)
        a_next = gcn_ref[0:1, :].astype(F32) * un_ref[0:1, :].astype(F32) * (i < n_t - 1).astype(F32)
        a_m1, a_p1 = _shift_rows(a, a_prev, a_next, tm)
        z3 = cw_ref[0:1, :] * a_m1 + cw_ref[1:2, :] * a + cw_ref[2:3, :] * a_p1
        zc = zc_ref[...].astype(F32)
        y_ref[:, 0:CONV_W] = (gb_ref[...].astype(F32) * (z3 + cb_ref[...]) * zc * _sigmoid(zc)).astype(BF16)

    pspec = pl.BlockSpec((tm, CONV_W), lambda i: (i, 0))
    hp, hn = _row_halo_specs(CONV_W, tm, S)
    wide_specs = [_view_spec(tm, d, ATTN_W) for d in DILATIONS]
    stat_specs = [_view_spec(tm, d, LANES) for d in DILATIONS]
    outs = pl.pallas_call(
        body, name="combine_gates", grid=(n_t,),
        out_shape=[jax.ShapeDtypeStruct((S, D_MODEL), BF16), jax.ShapeDtypeStruct((S, ATTN_W), BF16)]
        + [_view_shape(S, d, LANES, F32) for d in DILATIONS],
        in_specs=wide_specs + stat_specs + [pspec, hp, hn, pspec, pspec, hp, hn, pspec, pspec,
                                            pl.BlockSpec((3, CONV_W), lambda i: (0, 0)),
                                            pl.BlockSpec((1, CONV_W), lambda i: (0, 0))],
        out_specs=[pl.BlockSpec((tm, D_MODEL), lambda i: (i, 0)), pspec] + stat_specs,
        scratch_shapes=[_tile_scratch(tm, ATTN_W), _tile_scratch(tm, ATTN_W),
                        _tile_scratch(tm, LANES), _tile_scratch(tm, LANES), _tile_scratch(tm, LANES)],
        compiler_params=_params(("arbitrary",), 40),
    )(*o_parts, *lse_parts, u, u, u, gb, gc, gc, gc, zc, za, conv_w, conv_b.reshape(1, CONV_W))
    return outs[0], outs[1], outs[2:]


def _out_proj(y, x, target, wo):
    S = x.shape[0]
    tm = 512

    def body(y_ref, x_ref, t_ref, wo_ref, g_ref, dy_ref, dwo_ref, loss_ref):
        i = pl.program_id(0)

        @pl.when(i == 0)
        def _():
            dwo_ref[...] = jnp.zeros_like(dwo_ref)
            loss_ref[...] = jnp.zeros_like(loss_ref)

        yv = y_ref[...]
        wo_v = wo_ref[...]
        err = x_ref[...] + jnp.dot(yv, wo_v, preferred_element_type=F32) - t_ref[...]
        e2 = (err * err).reshape(tm // 8, 8, D_MODEL).sum(axis=0)
        part = e2[:, 0:LANES]
        for k in range(1, D_MODEL // LANES):
            part = part + e2[:, k * LANES:(k + 1) * LANES]
        loss_ref[...] += part
        gv = (err * (1.0 / D_MODEL)).astype(BF16)
        g_ref[...] = gv
        dy_ref[...] = lax.dot_general(gv, wo_v, (((1,), (1,)), ((), ())), preferred_element_type=F32).astype(BF16)
        dwo_ref[...] += lax.dot_general(yv, gv, (((0,), (0,)), ((), ())), preferred_element_type=F32)

    tile = pl.BlockSpec((tm, D_MODEL), lambda i: (i, 0))
    return pl.pallas_call(
        body, name="out_proj", grid=(S // tm,),
        out_shape=(jax.ShapeDtypeStruct((S, D_MODEL), BF16), jax.ShapeDtypeStruct((S, D_MODEL), BF16),
                   jax.ShapeDtypeStruct((D_MODEL, D_MODEL), F32), jax.ShapeDtypeStruct((8, LANES), F32)),
        in_specs=[tile, tile, tile, pl.BlockSpec((D_MODEL, D_MODEL), lambda i: (0, 0), pipeline_mode=pl.Buffered(1))],
        out_specs=(tile, tile, pl.BlockSpec((D_MODEL, D_MODEL), lambda i: (0, 0)),
                   pl.BlockSpec((8, LANES), lambda i: (0, 0))),
        compiler_params=_params(("arbitrary",), 48),
    )(y, x, target, wo)


def _gates_bwd(dy, u, gb, gc, zc, za, o, conv_w, conv_b):
    S = u.shape[0]
    tm = 512
    n_t = S // tm

    def body(dy_ref, dyp_ref, dyn_ref, u_ref, up_ref, un_ref, gb_ref, gbp_ref, gbn_ref, gc_ref, gcp_ref, gcn_ref,
             zc_ref, zcp_ref, zcn_ref, za_ref, o_ref, cw_ref, cb_ref,
             d5_ref, small_ref,
             do_ref, do2_ref, do3_ref, delta_ref, delta2_ref, delta3_ref, sdo, sdelta,
             u_ext, gc_ext, gb_ext, zc_ext, dy_ext):
        i = pl.program_id(0)

        @pl.when(i == 0)
        def _():
            small_ref[...] = jnp.zeros_like(small_ref)

        exts = ((u_ext, up_ref, u_ref, un_ref), (gc_ext, gcp_ref, gc_ref, gcn_ref), (gb_ext, gbp_ref, gb_ref, gbn_ref),
                (zc_ext, zcp_ref, zc_ref, zcn_ref))
        for ext, pr, mn, nx in exts:
            ext[0:HALO] = pr[...]
            ext[HALO:HALO + tm] = mn[...]
            ext[HALO + tm:] = nx[...]
        dy_ext[0:HALO] = dyp_ref[...]
        dy_ext[HALO:HALO + tm] = dy_ref[:, 0:CONV_W]
        dy_ext[HALO + tm:] = dyn_ref[...]
        zero_halo = jnp.zeros((HALO, CONV_W), BF16)

        @pl.when(i == 0)
        def _():
            u_ext[0:HALO] = zero_halo
            dy_ext[0:HALO] = zero_halo

        @pl.when(i == n_t - 1)
        def _():
            u_ext[HALO + tm:] = zero_halo
            dy_ext[HALO + tm:] = zero_halo

        wide_rows = CHUNK + 2 * HALO
        mid = slice(HALO, HALO + CHUNK)
        csum = lambda t: jnp.sum(t, axis=0, keepdims=True)

        def one_group(p, c0, sums):
            cs = slice(p * LANES, (p + 1) * LANES)
            w0, w1, w2, cb = cw_ref[0:1, cs], cw_ref[1:2, cs], cw_ref[2:3, cs], cb_ref[:, cs]
            rows = pl.ds(c0, CHUNK)
            wide = pl.ds(c0, wide_rows)
            u_e, gc_e, gb_e, zc_e, dy_e = (r[wide, cs].astype(F32) for r in (u_ext, gc_ext, gb_ext, zc_ext, dy_ext))
            a_e = gc_e * u_e
            sz_e = _sigmoid(zc_e)
            dz3_e = dy_e * zc_e * sz_e * gb_e
            a, a_m1, a_p1 = a_e[mid], pltpu.roll(a_e, 1, 0)[mid], pltpu.roll(a_e, wide_rows - 1, 0)[mid]
            dz3, dz3_m1, dz3_p1 = dz3_e[mid], pltpu.roll(dz3_e, 1, 0)[mid], pltpu.roll(dz3_e, wide_rows - 1, 0)[mid]
            uv, gcv, gbv, zcv, dyc, sz = u_e[mid], gc_e[mid], gb_e[mid], zc_e[mid], dy_e[mid], sz_e[mid]
            cvp = w0 * a_m1 + w1 * a + w2 * a_p1 + cb
            dcv = dyc * zcv * sz
            d5_ref[3, rows, cs] =(dyc * gbv * cvp * (sz * (1.0 + zcv * (1.0 - sz)))).astype(BF16)
            d5_ref[1, rows, cs] =(dcv * cvp).astype(BF16)
            da = w0 * dz3_p1 + w1 * dz3 + w2 * dz3_m1
            d5_ref[0, rows, cs] =(da * gcv).astype(BF16)
            d5_ref[2, rows, cs] =(da * uv).astype(BF16)
            sums = (sums[0] + csum(dz3 * a_m1), sums[1] + csum(dz3 * a), sums[2] + csum(dz3 * a_p1),
                    sums[3] + csum(dz3))

            dya = dy_ref[rows, CONV_W + p * LANES:CONV_W + (p + 1) * LANES].astype(F32)
            zav = za_ref[rows, cs].astype(F32)
            ov = o_ref[rows, cs].astype(F32)
            sa = _sigmoid(zav)
            d_o = dya * zav * sa
            sdo[p, rows, :] = d_o
            do_ref[rows, cs] = d_o.astype(BF16)
            d5_ref[4, rows, cs] =(dya * ov * (sa * (1.0 + zav * (1.0 - sa)))).astype(BF16)
            return sums, _split_dot(d_o * ov, _lanes_to_head(p, 1.0))

        def chunk(ci, carry):
            c0 = pl.multiple_of(ci * CHUNK, CHUNK)
            out, delta = [], jnp.zeros((CHUNK, LANES), F32)
            for p in range(N_PAIRS):
                sums, part = one_group(p, c0, carry[p])
                out.append(sums)
                delta = delta + part
            sdelta[0, pl.ds(c0, CHUNK), :] = delta
            return tuple(out)

        z = jnp.zeros((1, LANES), F32)
        all_sums = lax.fori_loop(0, tm // CHUNK, chunk, ((z, z, z, z),) * N_PAIRS)
        for p in range(N_PAIRS):
            for k in range(4):
                small_ref[k:k + 1, p * LANES:(p + 1) * LANES] += all_sums[p][k]
        delta_ref[...] = sdelta[0]
        for dil, wide_ref, stat_ref in zip(VIEW_DILATIONS, (do2_ref, do3_ref), (delta2_ref, delta3_ref)):
            _to_view(sdo, wide_ref, dil, BF16)
            _to_view(sdelta, stat_ref, dil, F32)

    pspec = pl.BlockSpec((tm, CONV_W), lambda i: (i, 0))
    hp, hn = _row_halo_specs(CONV_W, tm, S)
    piece = jax.ShapeDtypeStruct((S, CONV_W), BF16)
    outs = pl.pallas_call(
        body, name="gates_bwd", grid=(n_t,),
        out_shape=[jax.ShapeDtypeStruct((5, S, CONV_W), BF16), jax.ShapeDtypeStruct((8, CONV_W), F32)]
        + [_view_shape(S, d, ATTN_W, BF16) for d in DILATIONS] + [_view_shape(S, d, LANES, F32) for d in DILATIONS],
        in_specs=[pl.BlockSpec((tm, D_MODEL), lambda i: (i, 0)), hp, hn,
                  pspec, hp, hn, pspec, hp, hn, pspec, hp, hn, pspec, hp, hn, pspec, pspec,
                  pl.BlockSpec((3, CONV_W), lambda i: (0, 0)), pl.BlockSpec((1, CONV_W), lambda i: (0, 0))],
        out_specs=[pl.BlockSpec((5, tm, CONV_W), lambda i: (0, i, 0)), pl.BlockSpec((8, CONV_W), lambda i: (0, 0))]
        + [_view_spec(tm, d, ATTN_W) for d in DILATIONS] + [_view_spec(tm, d, LANES) for d in DILATIONS],
        scratch_shapes=[_tile_scratch(tm, ATTN_W), _tile_scratch(tm, LANES)]
        + [pltpu.VMEM((tm + 2 * HALO, CONV_W), BF16)] * 5,
        compiler_params=_params(("arbitrary",), 40),
    )(dy, dy, dy, u, u, u, gb, gb, gb, gc, gc, gc, zc, zc, zc, za, o, conv_w, conv_b.reshape(1, CONV_W))
    return outs[0], outs[1], outs[2:5], outs[5:8]


def _attn_bwd(g, dil, qv, kv, vv, dov, lv, dv_, rqv, rkv, gain, tables):
    L = qv.shape[0]
    T = min(1024, L)
    n_sub = T // QB + 1
    QW = T + 2 * HALF
    KW = T + 2 * QB

    def body(qm_ref, qp_ref, qn_ref, dom_ref, dop_ref, don_ref, km_ref, kp_ref, kn_ref, vm_ref, vp_ref, vn_ref,
             lm_ref, lp_ref, ln_ref, dm_ref, dp_ref, dn_ref, rqm_ref, rqp_ref, rqn_ref, rk_ref,
             gain_ref, bias_ref,
             dq_ref, dk_ref, dv_ref, ds_ref, a_ref,
             qwin, dowin, kwin, vwin, lwin, dwin, rqwin, dqwin, dkacc, dvacc,
             s_scr, d_scr, p_scr, ds_scr, q2_scr, do2_scr, dq_scr):
        r = pl.program_id(0)
        i = pl.program_id(1)

        @pl.when((r == 0) & (i == 0))
        def _():
            ds_ref[...] = jnp.zeros_like(ds_ref)
            a_ref[...] = jnp.zeros_like(a_ref)

        for win, (pr, mn, nx), h_rows in ((qwin, (qp_ref, qm_ref, qn_ref), HALF),
                                          (dowin, (dop_ref, dom_ref, don_ref), HALF),
                                          (lwin, (lp_ref, lm_ref, ln_ref), HALF),
                                          (dwin, (dp_ref, dm_ref, dn_ref), HALF),
                                          (rqwin, (rqp_ref, rqm_ref, rqn_ref), HALF),
                                          (kwin, (kp_ref, km_ref, kn_ref), QB),
                                          (vwin, (vp_ref, vm_ref, vn_ref), QB)):
            win[0:h_rows] = pr[...]
            win[h_rows:h_rows + T] = mn[...]
            win[h_rows + T:] = nx[...]
        dkacc[0:QB] = jnp.zeros((QB, ATTN_W), F32)
        dvacc[0:QB] = jnp.zeros((QB, ATTN_W), F32)

        lo = _lane((QB, LANES)) < HEAD_DIM
        same_head = _same_head()
        krow = lax.broadcasted_iota(jnp.int32, (KB, 1), 0)
        qcol = _lane((1, 2 * QB)) % QB
        row = lax.broadcasted_iota(jnp.int32, (QB, 1), 0)
        chan_lo = lax.broadcasted_iota(jnp.int32, (LANES, 1), 0) < HEAD_DIM

        cols = [slice(p * LANES, (p + 1) * LANES) for p in range(N_PAIRS)]
        nt = (((1,), (1,)), ((), ()))

        def scores(j):
            r0 = pl.multiple_of(j * QB, QB)
            for k, cs in enumerate(cols):
                qf = qwin[pl.ds(r0, QB), cs].astype(F32) * gain_ref[:, cs]
                q2_scr[k] = jnp.concatenate([jnp.where(lo, qf, 0.0), jnp.where(lo, 0.0, qf)], axis=0).astype(BF16)
                dov = dowin[pl.ds(r0, QB), cs]
                zero = jnp.zeros_like(dov)
                do2_scr[k] = jnp.concatenate([jnp.where(lo, dov, zero), jnp.where(lo, zero, dov)], axis=0)
                s_scr[k] = lax.dot_general(kwin[pl.ds(r0, KB), cs], q2_scr[k], nt, preferred_element_type=F32)
                d_scr[k] = lax.dot_general(vwin[pl.ds(r0, KB), cs], do2_scr[k], nt, preferred_element_type=F32)

        def sub(j):
            r0 = pl.multiple_of(j * QB, QB)
            kpos = i * T - QB + j * QB + krow
            qpos = i * T - HALF + j * QB + qcol
            valid = (kpos >= 0) & (kpos < L) & (qpos >= 0) & (qpos < L)
            owned_t = ((qpos >= i * T) & (qpos < i * T + T)).astype(F32)
            qpos_c = i * T - HALF + j * QB + row
            owned = ((qpos_c >= i * T) & (qpos_c < i * T + T)).astype(F32)
            lse_rows = lwin[pl.ds(r0, QB), :].T
            delta_rows = dwin[pl.ds(r0, QB), :].T
            rq_t = rqwin[pl.ds(r0, QB), :]
            for p in range(N_PAIRS):
                k = p
                bias2 = jnp.concatenate([bias_ref[0, 2 * p], bias_ref[0, 2 * p + 1]], axis=1)
                lse2 = jnp.concatenate([lse_rows[2 * p:2 * p + 1, :], lse_rows[2 * p + 1:2 * p + 2, :]], axis=1)
                delta2 = jnp.concatenate([delta_rows[2 * p:2 * p + 1, :], delta_rows[2 * p + 1:2 * p + 2, :]], axis=1)
                pt = jnp.where(valid, jnp.exp(s_scr[k] + bias2 - lse2), 0.0)
                dst = pt * (d_scr[k] - delta2)
                ds_ref[p] += dst * owned_t
                p_scr[p] = pt.astype(BF16)
                ds_scr[p] = dst.astype(BF16)
            for p, cs in enumerate(cols):
                k = p
                for acc_ref, lhs, rhs in ((dvacc, p_scr[p], do2_scr[k]), (dkacc, ds_scr[p], q2_scr[k])):
                    part = jnp.dot(lhs, rhs, preferred_element_type=F32)
                    acc_ref[pl.ds(r0, QB), cs] += part[0:QB]
                    acc_ref[pl.ds(r0 + QB, QB), cs] = part[QB:KB]
                dq_scr[p] = lax.dot_general(kwin[pl.ds(r0, KB), cs], ds_scr[p], (((0,), (0,)), ((), ())),
                                            preferred_element_type=F32)
            for p, cs in enumerate(cols):
                g2 = gain_ref[:, cs]
                q_f = qwin[pl.ds(r0, QB), cs].astype(F32)
                dqt2 = dq_scr[p]
                dqm2 = jnp.where(chan_lo, dqt2[:, 0:QB], dqt2[:, QB:2 * QB]).T
                a_ref[p] += (dqm2 * q_f * owned).reshape(QB // 8, 8, LANES).sum(axis=0)
                dqh = dqm2 * g2
                mean_c = _split_dot(dqh * q_f, same_head) * (1.0 / HEAD_DIM)
                rq2 = _split_dot(rq_t, _head_to_lanes(p))
                dqwin[pl.ds(r0, QB), cs] = (rq2 * (dqh - q_f * mean_c)).astype(BF16)

        def one_sub(j, carry):
            scores(j)
            sub(j)
            return carry

        lax.fori_loop(0, n_sub, one_sub, 0)
        dq_ref[...] = dqwin[HALF:HALF + T]

        def finish(j, carry):
            r0 = pl.multiple_of(j * QB, QB)
            rk_t = rk_ref[pl.ds(r0, QB), :]
            for p in range(N_PAIRS):
                cs = slice(p * LANES, (p + 1) * LANES)
                dkh = dkacc[pl.ds(QB + r0, QB), cs]
                k_f = km_ref[pl.ds(r0, QB), cs].astype(F32)
                mean_c = _split_dot(dkh * k_f, same_head) * (1.0 / HEAD_DIM)
                rk2 = _split_dot(rk_t, _head_to_lanes(p))
                dk_ref[pl.ds(r0, QB), cs] = (rk2 * (dkh - k_f * mean_c)).astype(BF16)
                dv_ref[pl.ds(r0, QB), cs] = dvacc[pl.ds(QB + r0, QB), cs].astype(BF16)
            return carry

        lax.fori_loop(0, T // QB, finish, 0)

    main = pl.BlockSpec((T, ATTN_W), lambda r, i: (i, r))
    smain = pl.BlockSpec((T, LANES), lambda r, i: (i, r))
    q_prev, q_next = _halo_specs(ATTN_W, HALF, T, L, lambda r: r)
    k_prev, k_next = _halo_specs(ATTN_W, QB, T, L, lambda r: r)
    s_prev, s_next = _halo_specs(LANES, HALF, T, L, lambda r: r)
    piece = jax.ShapeDtypeStruct((L, dil * ATTN_W), BF16)
    return pl.pallas_call(
        body, name=f"attn_bwd_d{dil}", grid=(dil, L // T),
        out_shape=(piece, piece, piece, jax.ShapeDtypeStruct((N_PAIRS, KB, 2 * QB), F32),
                   jax.ShapeDtypeStruct((N_PAIRS, 8, LANES), F32)),
        in_specs=[main, q_prev, q_next, main, q_prev, q_next, main, k_prev, k_next, main, k_prev, k_next,
                  smain, s_prev, s_next, smain, s_prev, s_next, smain, s_prev, s_next, smain,
                  pl.BlockSpec((1, ATTN_W), lambda r, i: (0, 0)),
                  pl.BlockSpec((1, N_HEADS, KB, QB), lambda r, i: (g, 0, 0, 0))],
        out_specs=(main, main, main, pl.BlockSpec((N_PAIRS, KB, 2 * QB), lambda r, i: (0, 0, 0)),
                   pl.BlockSpec((N_PAIRS, 8, LANES), lambda r, i: (0, 0, 0))),
        scratch_shapes=[pltpu.VMEM((QW, ATTN_W), BF16), pltpu.VMEM((QW, ATTN_W), BF16),
                        pltpu.VMEM((KW, ATTN_W), BF16), pltpu.VMEM((KW, ATTN_W), BF16),
                        pltpu.VMEM((QW, LANES), F32), pltpu.VMEM((QW, LANES), F32), pltpu.VMEM((QW, LANES), F32),
                        pltpu.VMEM((QW, ATTN_W), BF16),
                        pltpu.VMEM((KW, ATTN_W), F32), pltpu.VMEM((KW, ATTN_W), F32),
                        pltpu.VMEM((N_PAIRS, KB, 2 * QB), F32), pltpu.VMEM((N_PAIRS, KB, 2 * QB), F32),
                        pltpu.VMEM((N_PAIRS, KB, 2 * QB), BF16), pltpu.VMEM((N_PAIRS, KB, 2 * QB), BF16),
                        pltpu.VMEM((N_PAIRS, 2 * QB, LANES), BF16), pltpu.VMEM((N_PAIRS, 2 * QB, LANES), BF16),
                        pltpu.VMEM((N_PAIRS, LANES, 2 * QB), F32)],
        compiler_params=_params(("arbitrary", "arbitrary"), 56),
    )(qv, qv, qv, dov, dov, dov, kv, kv, kv, vv, vv, vv, lv, lv, lv, dv_, dv_, dv_, rqv, rqv, rqv, rkv, gain, tables)


D5_GROUPS = (0, 1, 2, 3, 7)
QKV_GROUPS = (4, 5, 6)


def _proj_bwd(x, d_out, norm_w, wg, d5, qkv_shares):
    S = x.shape[0]
    tm = 512
    flat = [share for p in qkv_shares for share in p]
    n_t = S // tm

    def body(*refs):
        x_ref, g_ref, nw_ref, wg_ref, d5_ref = refs[:5]
        piece_refs = refs[5:5 + len(flat)]
        gx_ref, dnw_ref, ht_ref, dqkv_ref, sview = refs[5 + len(flat):]
        i = pl.program_id(0)

        @pl.when(i == 0)
        def _():
            dnw_ref[...] = jnp.zeros_like(dnw_ref)

        xv = x_ref[...]
        r = lax.rsqrt(jnp.mean(xv * xv, axis=-1, keepdims=True) + EPS)
        xh = xv * r
        nw = nw_ref[...]
        ht_ref[...] = (xh * nw).T.astype(BF16)
        dh = jnp.zeros((tm, D_MODEL), F32)
        k = 0
        for j in range(8):
            if j in D5_GROUPS:
                dp = d5_ref[D5_GROUPS.index(j)]
            else:
                shares = qkv_shares[QKV_GROUPS.index(j)]
                dp = jnp.zeros((tm, CONV_W), F32)
                for t, (_, dil) in enumerate(shares):
                    if dil == 1:
                        dp = dp + piece_refs[k + t][...].astype(F32)
                    else:
                        _from_view(piece_refs[k + t], sview, dil)
                        dp = dp + _scratch_value(sview)
                dp = dp.astype(BF16)
                k += len(shares)
                dqkv_ref[QKV_GROUPS.index(j)] = dp
            dh = dh + lax.dot_general(dp, wg_ref[j], (((1,), (1,)), ((), ())), preferred_element_type=F32)
        dnw_ref[...] += (dh * xh).reshape(tm // 8, 8, D_MODEL).sum(axis=0)
        dxh = dh * nw
        mean_c = jnp.mean(dxh * xh, axis=-1, keepdims=True)
        gx_ref[...] = g_ref[...].astype(F32) + r * (dxh - xh * mean_c)

    tile = pl.BlockSpec((tm, D_MODEL), lambda i: (i, 0))
    return pl.pallas_call(
        body, name="proj_bwd", grid=(n_t,),
        out_shape=(jax.ShapeDtypeStruct((S, D_MODEL), F32), jax.ShapeDtypeStruct((8, D_MODEL), F32),
                   jax.ShapeDtypeStruct((D_MODEL, S), BF16), jax.ShapeDtypeStruct((3, S, CONV_W), BF16)),
        in_specs=[tile, tile, pl.BlockSpec((1, D_MODEL), lambda i: (0, 0)),
                  pl.BlockSpec(wg.shape, lambda i: (0, 0, 0), pipeline_mode=pl.Buffered(1)),
                  pl.BlockSpec((5, tm, CONV_W), lambda i: (0, i, 0))]
        + [_view_spec(tm, dil, CONV_W) for _, dil in flat],
        out_specs=(tile, pl.BlockSpec((8, D_MODEL), lambda i: (0, 0)),
                   pl.BlockSpec((D_MODEL, tm), lambda i: (0, i)), pl.BlockSpec((3, tm, CONV_W), lambda i: (0, i, 0))),
        scratch_shapes=[_tile_scratch(tm, CONV_W)],
        compiler_params=_params(("arbitrary",), 56),
    )(x, d_out, norm_w.reshape(1, D_MODEL), wg, d5, *[a for a, _ in flat])


def _dw_exchange(ht, d5, dqkv, gw_out, small):
    D, S = ht.shape
    tk = 2048
    n_t = S // tk
    me_outer = 4 * lax.axis_index("x") + 2 * lax.axis_index("y") + lax.axis_index("c")
    order = ((me_outer + 1 + jnp.arange(N_DEV, dtype=jnp.int32)) % N_DEV).astype(jnp.int32)
    in_d5 = ((order < 4) | (order == 7)).astype(jnp.int32)
    at_d5 = jnp.where(order < 4, order, jnp.where(order == 7, 4, 0)).astype(jnp.int32)
    at_qkv = jnp.where(in_d5 == 1, 0, order - 4).astype(jnp.int32)

    def body(in_d5_ref, at_d5_ref, at_qkv_ref, ht_hbm, d5_ref, dqkv_ref, gout_ref, sm_ref, rin_ref, rout_ref, rsm_ref,
             acc, sbuf, ht_vmem, in_send, in_recv, side_send, side_recv, local_sems, ht_sems):
        s = pl.program_id(0)
        t = pl.program_id(1)
        x, y, c = lax.axis_index("x"), lax.axis_index("y"), lax.axis_index("c")
        me = 4 * x + 2 * y + c

        def ht_copy(j):
            return pltpu.make_async_copy(ht_hbm.at[:, j * tk:(j + 1) * tk], ht_vmem.at[j], ht_sems.at[j])

        def in_copy(k):
            to = (me + 1 + k) % N_DEV
            return pltpu.make_async_remote_copy(
                src_ref=sbuf.at[k % 2], dst_ref=rin_ref.at[me], send_sem=in_send.at[k], recv_sem=in_recv.at[k],
                device_id=(to // 4, (to // 2) % 2, to % 2), device_id_type=MESH)

        def in_landing(k):
            frm = (me + 2 * N_DEV - 1 - k) % N_DEV
            return pltpu.make_async_remote_copy(
                src_ref=sbuf.at[0], dst_ref=rin_ref.at[frm], send_sem=in_send.at[k], recv_sem=in_recv.at[k],
                device_id=(x, y, c), device_id_type=MESH)

        def side_copies():
            local = [pltpu.make_async_copy(gout_ref.at[me], rout_ref.at[me], local_sems.at[0]),
                     pltpu.make_async_copy(sm_ref, rsm_ref.at[me], local_sems.at[1])]
            remote = []
            for k in range(1, N_DEV):
                px = 1 - x if k & 4 else x
                py = 1 - y if k & 2 else y
                pc = 1 - c if k & 1 else c
                peer = 4 * px + 2 * py + pc
                for a, (src, dst) in enumerate(((gout_ref.at[peer], rout_ref.at[me]), (sm_ref, rsm_ref.at[me]))):
                    remote.append(pltpu.make_async_remote_copy(
                        src_ref=src, dst_ref=dst, send_sem=side_send.at[a * 7 + k - 1],
                        recv_sem=side_recv.at[a * 7 + k - 1], device_id=(px, py, pc), device_id_type=MESH))
            return local, remote

        @pl.when((s == 0) & (t == 0))
        def _():
            for j in range(n_t):
                ht_copy(j).start()
            local, remote = side_copies()
            for cp in local + remote:
                cp.start()

        for j in range(n_t):
            @pl.when((s == 0) & (t == j))
            def _(j=j):
                ht_copy(j).wait()

        @pl.when(t == 0)
        def _():
            acc[...] = jnp.zeros_like(acc)

        dp = jnp.where(in_d5_ref[s] == 1, d5_ref[0], dqkv_ref[0])
        acc[...] += jnp.dot(ht_vmem[t], dp, preferred_element_type=F32)

        for k in range(N_DEV):
            @pl.when((s == k) & (t == n_t - 1))
            def _(k=k):
                if k >= 2:
                    in_copy(k - 2).wait_send()
                sbuf[k % 2] = acc[...].astype(BF16)
                if k < N_DEV - 1:
                    in_copy(k).start()
                else:
                    own = pltpu.make_async_copy(sbuf.at[k % 2], rin_ref.at[me], local_sems.at[2])
                    own.start()
                    in_copy(k - 1).wait_send()
                    for j in range(N_DEV - 1):
                        in_landing(j).wait_recv()
                    local, remote = side_copies()
                    for cp in remote:
                        cp.wait_recv()
                    for cp in remote:
                        cp.wait_send()
                    for cp in local:
                        cp.wait()
                    own.wait()

    hbm = pl.BlockSpec(memory_space=pl.ANY)
    grid_spec = pltpu.PrefetchScalarGridSpec(
        num_scalar_prefetch=3, grid=(N_DEV, n_t),
        in_specs=[hbm,
                  pl.BlockSpec((1, tk, CONV_W), lambda s, t, use, a5, a3: (a5[s], jnp.where(use[s] == 1, t, 0), 0)),
                  pl.BlockSpec((1, tk, CONV_W), lambda s, t, use, a5, a3: (a3[s], jnp.where(use[s] == 1, 0, t), 0)),
                  hbm, hbm],
        out_specs=(hbm, hbm, hbm),
        scratch_shapes=[pltpu.VMEM((D, CONV_W), F32), pltpu.VMEM((2, D, CONV_W), BF16),
                        pltpu.VMEM((n_t, D, tk), BF16),
                        pltpu.SemaphoreType.DMA((N_DEV - 1,)), pltpu.SemaphoreType.DMA((N_DEV - 1,)),
                        pltpu.SemaphoreType.DMA((14,)), pltpu.SemaphoreType.DMA((14,)),
                        pltpu.SemaphoreType.DMA((3,)), pltpu.SemaphoreType.DMA((n_t,))])
    return pl.pallas_call(
        body, name="dw_exchange", grid_spec=grid_spec,
        out_shape=(jax.ShapeDtypeStruct((N_DEV, D, CONV_W), BF16), jax.ShapeDtypeStruct(gw_out.shape, F32),
                   jax.ShapeDtypeStruct((N_DEV,) + small.shape, F32)),
        compiler_params=_params(("arbitrary", "arbitrary"), 52),
    )(in_d5, at_d5, at_qkv, ht, d5, dqkv, gw_out, small)


def _adamw_math(w, g, m, v):
    m2 = ADAM_B1 * m + (1.0 - ADAM_B1) * g
    v2 = ADAM_B2 * v + (1.0 - ADAM_B2) * (g * g)
    m_hat = m2 / (1.0 - ADAM_B1 ** ADAM_STEP)
    v_hat = v2 / (1.0 - ADAM_B2 ** ADAM_STEP)
    delta = -ADAM_LR * (m_hat / (jnp.sqrt(v_hat) + ADAM_EPS) + ADAM_WD * w)
    return delta, m2, v2


def _adamw_sharded(name, parts, w, m, v, rows):
    R, C = w.shape

    def body(p_ref, w_ref, m_ref, v_ref, g_ref, d_ref, m2_ref, v2_ref):
        g = p_ref[0].astype(F32)
        for s in range(1, N_DEV):
            g = g + p_ref[s].astype(F32)
        g_ref[...] = g
        d_ref[...], m2_ref[...], v2_ref[...] = _adamw_math(w_ref[...], g, m_ref[...], v_ref[...])

    spec = pl.BlockSpec((rows, C), lambda i: (i, 0))
    out = jax.ShapeDtypeStruct((R, C), F32)
    return pl.pallas_call(
        body, name=name, grid=(R // rows,),
        out_shape=(out,) * 4,
        in_specs=[pl.BlockSpec((N_DEV, rows, C), lambda i: (0, i, 0)), spec, spec, spec],
        out_specs=(spec,) * 4,
        compiler_params=_params(("arbitrary",), 40),
    )(parts, w, m, v)


def _adamw_small(parts, w, m, v):
    R = w.shape[0]

    def body(p_ref, w_ref, m_ref, v_ref, g_ref, d_ref, m2_ref, v2_ref):
        g = p_ref[0]
        for s in range(1, N_DEV):
            g = g + p_ref[s]
        g_ref[...] = g
        d_ref[...], m2_ref[...], v2_ref[...] = _adamw_math(w_ref[...], g, m_ref[...], v_ref[...])

    vm = pl.BlockSpec(memory_space=pltpu.VMEM)
    out = jax.ShapeDtypeStruct((R, LANES), F32)
    return pl.pallas_call(
        body, name="adamw_small", out_shape=(out,) * 4, in_specs=[vm] * 4, out_specs=(vm,) * 4,
    )(parts, w, m, v)


def _qk_gain_grads(a_parts, q_norm_w, k_norm_w, bias_g):
    def body(a_ref, qw_ref, kw_ref, bg_ref, out_ref):
        tot = jnp.zeros((8, LANES), F32)
        for g in range(len(DILATIONS)):
            for p in range(N_PAIRS):
                tot = tot + a_ref[g, p]
        col = jnp.sum(tot, axis=0, keepdims=True)
        a64 = col[:, 0:HEAD_DIM] + col[:, HEAD_DIM:LANES]
        out_ref[...] = jnp.zeros_like(out_ref)
        out_ref[0:1, 0:HEAD_DIM] = LOGIT_SCALE * kw_ref[...] * a64
        out_ref[1:2, 0:HEAD_DIM] = LOGIT_SCALE * qw_ref[...] * a64
        for h in range(N_HEADS):
            r0, c0 = 2 + h // 4, (h % 4) * N_BUCKETS
            out_ref[r0:r0 + 1, c0:c0 + N_BUCKETS] = bg_ref[h // 2, h % 2:h % 2 + 1, 0:N_BUCKETS]

    vm = pl.BlockSpec(memory_space=pltpu.VMEM)
    return pl.pallas_call(
        body, name="qk_gain_grads", out_shape=jax.ShapeDtypeStruct((8, LANES), F32),
        in_specs=[vm] * 4, out_specs=vm,
    )(a_parts, q_norm_w.reshape(1, HEAD_DIM), k_norm_w.reshape(1, HEAD_DIM), bias_g)


SMALL_ROWS = 32


def _pack_small(norm_w, conv_b, q_norm_w, k_norm_w, rel_bias, conv_w_full):
    pad64 = lambda a: jnp.pad(a, (0, LANES - HEAD_DIM)).reshape(1, LANES)
    return jnp.concatenate([
        norm_w.reshape(8, LANES), conv_b.reshape(4, LANES), pad64(q_norm_w), pad64(k_norm_w),
        rel_bias.T.reshape(2, LANES), conv_w_full.reshape(12, LANES), jnp.zeros((4, LANES), F32)], axis=0)


def _unpack_small(s):
    return (s[0:8].reshape(D_MODEL), s[8:12].reshape(CONV_W), s[12, 0:HEAD_DIM], s[13, 0:HEAD_DIM],
            s[14:16].reshape(N_HEADS, N_BUCKETS).T, s[16:28].reshape(3, CONV_W))


def kernel(x, norm_w, w_in, conv_w, conv_b, q_norm_w, k_norm_w, rel_bias, w_out, loss_target, m_norm_w, m_w_in, m_conv_w, m_conv_b, m_q_norm_w, m_k_norm_w, m_rel_bias, m_w_out, v_norm_w, v_w_in, v_conv_w, v_conv_b, v_q_norm_w, v_k_norm_w, v_rel_bias, v_w_out):
    S = x.shape[1]
    x2 = x.reshape(S, D_MODEL)
    tgt = loss_target.reshape(S, D_MODEL)
    me = 4 * lax.axis_index("x") + 2 * lax.axis_index("y") + lax.axis_index("c")
    shard_w = CONV_W // N_DEV

    cw_pad = jnp.pad(conv_w, ((0, 5), (0, LANES - shard_w)))
    wg, wo_g, cw_g = _ag_weights(w_in, w_out, cw_pad)
    wo = wo_g.reshape(D_MODEL, D_MODEL)
    conv_w_full = cw_g[:, 0:3, 0:shard_w].transpose(1, 0, 2).reshape(3, CONV_W)

    bmat_t = jnp.asarray(np.stack([_bucket_matrix(d).T for d in DILATIONS]))
    tables = _bias_tables(rel_bias, bmat_t)
    gain = jnp.tile(q_norm_w * k_norm_w * LOGIT_SCALE, N_HEADS).reshape(1, ATTN_W)

    u, gb, gc, zc, za, q_hat, k_hat, v, rq, rk = _fwd_proj(x2, norm_w, wg)
    o_parts, lse_parts = [], []
    for g, dil in enumerate(DILATIONS):
        o_g, lse_g = _attn_fwd(g, dil, q_hat[g], k_hat[g], v[g], gain, tables)
        o_parts.append(o_g)
        lse_parts.append(lse_g)
    y, o, lse = _combine_gates(o_parts, lse_parts, u, gb, gc, zc, za, conv_w_full, conv_b)
    d_out, dy, gwo_part, loss_part = _out_proj(y, x2, tgt, wo)
    loss_mine = jnp.sum(loss_part) * (0.5 / D_MODEL)

    d5, conv_small, d_o, delta = _gates_bwd(dy, u, gb, gc, zc, za, o, conv_w_full, conv_b)
    dqs, dks, dvs, ds_parts, a_parts = [], [], [], [], []
    for g, dil in enumerate(DILATIONS):
        dq, dk, dv, ds_sum, a_sum = _attn_bwd(g, dil, q_hat[g], k_hat[g], v[g], d_o[g], lse[g], delta[g], rq[g], rk[g],
                                              gain, tables)
        dqs.append((dq, dil))
        dks.append((dk, dil))
        dvs.append((dv, dil))
        ds_parts.append(ds_sum)
        a_parts.append(a_sum)
    grad_x, gnw_part, h_t, dqkv = _proj_bwd(x2, d_out, norm_w, wg, d5, [dqs, dks, dvs])
    bias_g = _bias_grad(jnp.stack(ds_parts), bmat_t)
    qk_small = _qk_gain_grads(jnp.stack(a_parts), q_norm_w, k_norm_w, bias_g)

    small_part = jnp.concatenate([
        gnw_part.sum(axis=0).reshape(8, LANES), conv_small[3].reshape(4, LANES), qk_small[0:4],
        conv_small[0:3].reshape(12, LANES), jnp.zeros((4, LANES), F32).at[0, 0].set(loss_mine)], axis=0)
    r_in, r_out, r_small = _dw_exchange(h_t, d5, dqkv, gwo_part.reshape(N_DEV, D_MODEL // N_DEV, D_MODEL), small_part)

    g_win, d_win, m_win, v_win = _adamw_sharded("adamw_w_in", r_in, w_in, m_w_in, v_w_in, 128)
    g_wo, d_wo, m_wo, v_wo = _adamw_sharded("adamw_w_out", r_out, w_out, m_w_out, v_w_out, 32)

    def full_conv(a):
        return jnp.zeros((3, CONV_W), F32).at[:, 0:shard_w].set(a)

    packs = [_pack_small(nw_, cb_, qw_, kw_, rb_, full_conv(cw_)) for nw_, cb_, qw_, kw_, rb_, cw_ in (
        (norm_w, conv_b, q_norm_w, k_norm_w, rel_bias, conv_w),
        (m_norm_w, m_conv_b, m_q_norm_w, m_k_norm_w, m_rel_bias, m_conv_w),
        (v_norm_w, v_conv_b, v_q_norm_w, v_k_norm_w, v_rel_bias, v_conv_w))]
    r_small_mine = r_small.at[:, 16:28, :].set(
        jnp.pad(lax.dynamic_slice_in_dim(r_small[:, 16:28, :].reshape(N_DEV, 3, CONV_W), me * shard_w, shard_w, axis=2),
                ((0, 0), (0, 0), (0, CONV_W - shard_w))).reshape(N_DEV, 12, LANES))
    outs_small = _adamw_small(r_small_mine, *packs)
    g_s, d_s, m_s, v_s = [_unpack_small(t) for t in outs_small]
    loss = outs_small[0][28, 0]

    def leaves(small, big_in, big_out):
        nw_, cb_, qw_, kw_, rb_, cwf = small
        return (nw_, big_in, cwf[:, 0:shard_w], cb_, qw_, kw_, rb_, big_out)

    return (loss, grad_x.reshape(x.shape),
            *leaves(g_s, g_win, g_wo), *leaves(d_s, d_win, d_wo), *leaves(m_s, m_win, m_wo), *leaves(v_s, v_win, v_wo))
```

```python
import math

import numpy as np
import jax
import jax.numpy as jnp
from jax import lax
from jax.experimental import pallas as pl
from jax.experimental.pallas import tpu as pltpu

F32 = jnp.float32
BF16 = jnp.bfloat16

N_DEV = 8
D_MODEL = 1024
CONV_W = 512
ATTN_W = 512
N_HEADS = 8
HEAD_DIM = 64
N_PAIRS = N_HEADS // 2
LANES = 128
HALF = 64
QB = 128
KB = QB + 2 * HALF
DILATIONS = (1, 4, 16)
N_BUCKETS = 32
MAX_DISTANCE = 1024
EPS = 1e-6
NEG = -1e30
LOGIT_SCALE = HEAD_DIM ** -0.5

ADAM_LR = 0.001
ADAM_B1 = 0.9
ADAM_B2 = 0.999
ADAM_EPS = 1e-08
ADAM_WD = 0.01
ADAM_STEP = 10

MESH = pl.DeviceIdType.MESH
MIB = 1024 * 1024


def _params(semantics, vmem_mib):
    return pltpu.CompilerParams(dimension_semantics=semantics, vmem_limit_bytes=vmem_mib * MIB)


def _lane(shape):
    return lax.broadcasted_iota(jnp.int32, shape, len(shape) - 1)


def _sigmoid(z):
    return 1.0 / (1.0 + jnp.exp(-z))


def _split_dot(x, w):
    hi = x.astype(BF16)
    lo = (x - hi.astype(F32)).astype(BF16)
    return jnp.dot(hi, w, preferred_element_type=F32) + jnp.dot(lo, w, preferred_element_type=F32)


def _same_head():
    r = lax.broadcasted_iota(jnp.int32, (LANES, LANES), 0) // HEAD_DIM
    c = lax.broadcasted_iota(jnp.int32, (LANES, LANES), 1) // HEAD_DIM
    return (r == c).astype(BF16)


def _head_to_lanes(p):
    r = lax.broadcasted_iota(jnp.int32, (LANES, LANES), 0)
    c = lax.broadcasted_iota(jnp.int32, (LANES, LANES), 1) // HEAD_DIM
    return (r == 2 * p + c).astype(BF16)


def _lanes_to_head(p, scale):
    r = lax.broadcasted_iota(jnp.int32, (LANES, LANES), 0) // HEAD_DIM
    c = lax.broadcasted_iota(jnp.int32, (LANES, LANES), 1)
    return jnp.where(c == 2 * p + r, scale, 0.0).astype(BF16)


VIEW_DILATIONS = DILATIONS[1:]


def _view_shape(S, dil, width, dtype):
    return jax.ShapeDtypeStruct((S // dil, dil * width), dtype)


def _view_spec(tm, dil, width):
    return pl.BlockSpec((tm // dil, dil * width), lambda i: (i, 0))


def _tile_scratch(tm, width):
    return pltpu.VMEM((width // LANES, tm, LANES), F32)


def _scratch_value(scr_ref):
    n = scr_ref.shape[0]
    return scr_ref[0] if n == 1 else jnp.concatenate([scr_ref[c] for c in range(n)], axis=1)


def _to_view(scr_ref, out_ref, dil, dtype):
    n, tm, _ = scr_ref.shape
    for r in range(dil):
        for c in range(n):
            col = (r * n + c) * LANES
            out_ref[:, col:col + LANES] = scr_ref[c, pl.ds(r, tm // dil, stride=dil), :].astype(dtype)


def _from_view(blk_ref, scr_ref, dil):
    n, tm, _ = scr_ref.shape
    for r in range(dil):
        for c in range(n):
            col = (r * n + c) * LANES
            scr_ref[c, pl.ds(r, tm // dil, stride=dil), :] = blk_ref[:, col:col + LANES].astype(F32)


def _bucket_matrix(dilation):
    rel = np.arange(KB)[None, :] - HALF - np.arange(QB)[:, None]
    band = np.abs(rel) <= HALF
    dist = np.clip(rel, -HALF, HALF) * dilation
    half_b = N_BUCKETS // 2
    max_exact = half_b // 2
    ret = np.where(dist > 0, half_b, 0)
    n = np.abs(dist)
    nf = np.maximum(n, 1).astype(np.float32)
    large = max_exact + (np.log(nf / np.float32(max_exact)) / np.float32(math.log(MAX_DISTANCE / max_exact))
                         * np.float32(half_b - max_exact)).astype(np.int32)
    large = np.minimum(large, half_b - 1)
    bucket = ret + np.where(n < max_exact, n, large)
    return np.where(band, bucket, -1).astype(np.int32)


def _ag_weights(w_in, w_out, cw_pad):
    n_arr = 3

    def body(win_ref, wout_ref, cw_ref, gin_ref, gout_ref, gcw_ref, send_sems, recv_sems):
        x, y, c = lax.axis_index("x"), lax.axis_index("y"), lax.axis_index("c")
        me = (x, y, c)
        sibling = (x, y, 1 - c)
        chips = [(1 - x, y), (x, 1 - y), (1 - x, 1 - y)]
        arrays = (gin_ref, gout_ref, gcw_ref)

        def slot(px, py, pc):
            return 4 * px + 2 * py + pc

        gin_ref[slot(*me)] = win_ref[...].astype(BF16)
        gout_ref[slot(*me)] = wout_ref[...].astype(BF16)
        gcw_ref[slot(*me)] = cw_ref[...]

        def copy(a, k, block, to):
            ref = arrays[a].at[slot(*block)]
            return pltpu.make_async_remote_copy(
                src_ref=ref, dst_ref=ref, send_sem=send_sems.at[a * 7 + k], recv_sem=recv_sems.at[a * 7 + k],
                device_id=to, device_id_type=MESH)

        first = [copy(a, 0, me, sibling) for a in range(n_arr)]
        for j, chip in enumerate(chips):
            first += [copy(a, 1 + j, me, (*chip, c)) for a in range(n_arr)]
        for cp in first:
            cp.start()
        passed = []
        for j, chip in enumerate(chips):
            for a in range(n_arr):
                copy(a, 1 + j, (*chip, c), me).wait_recv()
            for a in range(n_arr):
                cp = copy(a, 4 + j, (*chip, c), sibling)
                cp.start()
                passed.append(cp)
        for a in range(n_arr):
            copy(a, 0, sibling, me).wait_recv()
        for j, chip in enumerate(chips):
            for a in range(n_arr):
                copy(a, 4 + j, (*chip, 1 - c), me).wait_recv()
        for cp in first + passed:
            cp.wait_send()

    vm = pl.BlockSpec(memory_space=pltpu.VMEM)
    return pl.pallas_call(
        body, name="ag_weights",
        out_shape=(jax.ShapeDtypeStruct((N_DEV,) + w_in.shape, BF16),
                   jax.ShapeDtypeStruct((N_DEV,) + w_out.shape, BF16),
                   jax.ShapeDtypeStruct((N_DEV,) + cw_pad.shape, F32)),
        in_specs=[vm, vm, vm], out_specs=(vm, vm, vm),
        scratch_shapes=[pltpu.SemaphoreType.DMA((n_arr * 7,)), pltpu.SemaphoreType.DMA((n_arr * 7,))],
        compiler_params=pltpu.CompilerParams(vmem_limit_bytes=40 * MIB),
    )(w_in, w_out, cw_pad)


def _fwd_proj(x, norm_w, wg):
    S = x.shape[0]
    tm = 512

    def body(x_ref, nw_ref, wg_ref, u_ref, gb_ref, gc_ref, zc_ref, za_ref, *rest):
        q_refs, k_refs, v_refs, rq_refs, rk_refs = (rest[3 * n:3 * n + 3] for n in range(5))
        scr, rscr = rest[15:]
        xv = x_ref[...]
        r = lax.rsqrt(jnp.mean(xv * xv, axis=-1, keepdims=True) + EPS)
        h = (xv * r * nw_ref[...]).astype(BF16)
        lane = _lane((tm, LANES))
        lo = lane < HEAD_DIM
        plain = {0: u_ref, 1: gb_ref, 2: gc_ref, 3: zc_ref, 7: za_ref}
        normed = {4: (q_refs, rq_refs), 5: (k_refs, rk_refs)}

        def emit(src, refs, dtype):
            refs[0][...] = _scratch_value(src).astype(dtype)
            for dil, ref in zip(VIEW_DILATIONS, refs[1:]):
                _to_view(src, ref, dil, dtype)

        group_order = (4, 0, 5, 1, 6, 2, 3, 7)
        acc_next = jnp.dot(h, wg_ref[group_order[0]], preferred_element_type=F32)
        for n, j in enumerate(group_order):
            acc = acc_next
            if n < 7:
                acc_next = jnp.dot(h, wg_ref[group_order[n + 1]], preferred_element_type=F32)
            if j in plain:
                plain[j][...] = acc.astype(BF16)
                continue
            if j == 6:
                for p in range(N_PAIRS):
                    scr[p] = acc[:, p * LANES:(p + 1) * LANES]
                emit(scr, v_refs, BF16)
                continue
            out_refs, r_refs = normed[j]
            r_tile = jnp.zeros((tm, LANES), F32)
            for p in range(N_PAIRS):
                blk = acc[:, p * LANES:(p + 1) * LANES]
                sq = blk * blk
                s_lo = jnp.sum(jnp.where(lo, sq, 0.0), axis=-1, keepdims=True)
                s_hi = jnp.sum(jnp.where(lo, 0.0, sq), axis=-1, keepdims=True)
                r_lo = lax.rsqrt(s_lo * (1.0 / HEAD_DIM) + EPS)
                r_hi = lax.rsqrt(s_hi * (1.0 / HEAD_DIM) + EPS)
                scr[p] = blk * jnp.where(lo, r_lo, r_hi)
                r_tile = jnp.where(lane == 2 * p, r_lo, r_tile)
                r_tile = jnp.where(lane == 2 * p + 1, r_hi, r_tile)
            rscr[0] = r_tile
            emit(scr, out_refs, BF16)
            emit(rscr, r_refs, F32)

    piece = jax.ShapeDtypeStruct((S, CONV_W), BF16)
    pspec = pl.BlockSpec((tm, CONV_W), lambda i: (i, 0))
    wide = [_view_shape(S, d, ATTN_W, BF16) for d in DILATIONS]
    wide_specs = [_view_spec(tm, d, ATTN_W) for d in DILATIONS]
    stat = [_view_shape(S, d, LANES, F32) for d in DILATIONS]
    stat_specs = [_view_spec(tm, d, LANES) for d in DILATIONS]
    outs = pl.pallas_call(
        body, name="fwd_proj", grid=(S // tm,),
        out_shape=[piece] * 5 + wide * 3 + stat * 2,
        in_specs=[pl.BlockSpec((tm, D_MODEL), lambda i: (i, 0)),
                  pl.BlockSpec((1, D_MODEL), lambda i: (0, 0)),
                  pl.BlockSpec(wg.shape, lambda i: (0, 0, 0), pipeline_mode=pl.Buffered(1))],
        out_specs=[pspec] * 5 + wide_specs * 3 + stat_specs * 2,
        scratch_shapes=[_tile_scratch(tm, ATTN_W), _tile_scratch(tm, LANES)],
        compiler_params=_params(("arbitrary",), 48),
    )(x, norm_w.reshape(1, D_MODEL), wg)
    u, gb, gc, zc, za = outs[:5]
    q_hat, k_hat, v, rq, rk = (outs[5 + 3 * n:8 + 3 * n] for n in range(5))
    return u, gb, gc, zc, za, q_hat, k_hat, v, rq, rk


def _bias_tables(rel_bias, bmat_t):
    def body(rb_ref, b_ref, out_ref):
        h = pl.program_id(1)
        b = b_ref[0]
        t = jnp.full((KB, QB), NEG, F32)
        for bk in range(N_BUCKETS):
            t = jnp.where(b == bk, rb_ref[bk, h], t)
        out_ref[0, 0] = t

    return pl.pallas_call(
        body, name="bias_tables", grid=(len(DILATIONS), N_HEADS),
        out_shape=jax.ShapeDtypeStruct((len(DILATIONS), N_HEADS, KB, QB), F32),
        in_specs=[pl.BlockSpec(memory_space=pltpu.SMEM),
                  pl.BlockSpec((1, KB, QB), lambda g, h: (g, 0, 0))],
        out_specs=pl.BlockSpec((1, 1, KB, QB), lambda g, h: (g, h, 0, 0)),
        compiler_params=_params(("arbitrary", "arbitrary"), 16),
    )(rel_bias, bmat_t)


def _bias_grad(ds_acc, bmat_t):
    present = [sorted(set(_bucket_matrix(d).ravel().tolist()) - {-1}) for d in DILATIONS]

    def body(ds_ref, b_ref, out_ref):
        lane = _lane((1, LANES))
        row = lax.broadcasted_iota(jnp.int32, (8, LANES), 0)
        out = jnp.zeros((8, LANES), F32)
        for e in range(2):
            vec = jnp.zeros((1, LANES), F32)
            for g in range(len(DILATIONS)):
                b = b_ref[g]
                ds = ds_ref[g, 0, :, e * QB:(e + 1) * QB]
                for bk in present[g]:
                    s = jnp.sum(jnp.where(b == bk, ds, 0.0), axis=-1, keepdims=True)
                    s = jnp.sum(s, axis=0, keepdims=True)
                    vec = vec + jnp.where(lane == bk, s, 0.0)
            out = jnp.where(row == e, vec, out)
        out_ref[0] = out

    return pl.pallas_call(
        body, name="bias_grad", grid=(N_PAIRS,),
        out_shape=jax.ShapeDtypeStruct((N_PAIRS, 8, LANES), F32),
        in_specs=[pl.BlockSpec((len(DILATIONS), 1, KB, 2 * QB), lambda p: (0, p, 0, 0)),
                  pl.BlockSpec((len(DILATIONS), KB, QB), lambda p: (0, 0, 0))],
        out_specs=pl.BlockSpec((1, 8, LANES), lambda p: (p, 0, 0)),
        compiler_params=_params(("arbitrary",), 16),
    )(ds_acc, bmat_t)


def _halo_specs(width, rows, T, L, cols_of):
    per = T // rows
    last = L // rows - 1
    prev = pl.BlockSpec((rows, width), lambda r, i: (jnp.maximum(i * per - 1, 0), cols_of(r)))
    nxt = pl.BlockSpec((rows, width), lambda r, i: (jnp.minimum((i + 1) * per, last), cols_of(r)))
    return prev, nxt


def _attn_fwd(g, dil, qv, kv, vv, gain, tables):
    L = qv.shape[0]
    T = min(1024, L)
    n_sub = T // QB

    n_i = L // T
    n_steps = dil * n_i
    N_SLOTS = 3

    def body(q_hbm, k_hbm, v_hbm, gain_ref, bias_ref, o_ref, lse_ref, qbuf, kwins, vwins, s_scr, p_scr, sems):
        i = pl.program_id(1)
        step = pl.program_id(0) * n_i + i
        slot = step % N_SLOTS

        def transfers(st, act):
            r_, i_ = st // n_i, st % n_i
            sl = st % N_SLOTS
            col = pl.ds(pl.multiple_of(r_ * ATTN_W, ATTN_W), ATTN_W)
            row0 = i_ * T
            act(pltpu.make_async_copy(q_hbm.at[pl.ds(row0, T), col], qbuf.at[sl], sems.at[sl, 0]))
            for a, (src, win) in enumerate(((k_hbm, kwins), (v_hbm, vwins))):
                act(pltpu.make_async_copy(src.at[pl.ds(row0, T), col], win.at[sl, pl.ds(HALF, T)], sems.at[sl, 1 + 3 * a]))

                @pl.when(i_ > 0)
                def _(src=src, win=win, a=a):
                    act(pltpu.make_async_copy(src.at[pl.ds(row0 - HALF, HALF), col], win.at[sl, pl.ds(0, HALF)],
                                              sems.at[sl, 2 + 3 * a]))

                @pl.when(i_ < n_i - 1)
                def _(src=src, win=win, a=a):
                    act(pltpu.make_async_copy(src.at[pl.ds(row0 + T, HALF), col], win.at[sl, pl.ds(HALF + T, HALF)],
                                              sems.at[sl, 3 + 3 * a]))

        @pl.when(step == 0)
        def _():
            for st in range(min(N_SLOTS - 1, n_steps)):
                transfers(st, lambda cp: cp.start())

        @pl.when(step + N_SLOTS - 1 < n_steps)
        def _():
            transfers(step + N_SLOTS - 1, lambda cp: cp.start())

        transfers(step, lambda cp: cp.wait())
        zero_halo = jnp.zeros((HALF, ATTN_W), BF16)

        @pl.when(i == 0)
        def _():
            kwins[slot, 0:HALF] = zero_halo
            vwins[slot, 0:HALF] = zero_halo

        @pl.when(i == n_i - 1)
        def _():
            kwins[slot, HALF + T:] = zero_halo
            vwins[slot, HALF + T:] = zero_halo

        q_ref, kwin, vwin = qbuf.at[slot], kwins.at[slot], vwins.at[slot]
        lo = _lane((QB, LANES)) < HEAD_DIM
        krow = lax.broadcasted_iota(jnp.int32, (KB, 1), 0)
        chan_lo = lax.broadcasted_iota(jnp.int32, (LANES, 1), 0) < HEAD_DIM
        hrow = lax.broadcasted_iota(jnp.int32, (LANES, QB), 0)

        def sub(j, carry):
            r0 = pl.multiple_of(j * QB, QB)
            kpos = i * T + j * QB - HALF + krow
            kvalid = (kpos >= 0) & (kpos < L)
            lse_rows = jnp.zeros((LANES, QB), F32)
            cols = [slice(p * LANES, (p + 1) * LANES) for p in range(N_PAIRS)]
            for p, cs in enumerate(cols):
                qf = q_ref[pl.ds(r0, QB), cs].astype(F32) * gain_ref[:, cs]
                q2 = jnp.concatenate([jnp.where(lo, qf, 0.0), jnp.where(lo, 0.0, qf)], axis=0).astype(BF16)
                s_scr[p] = lax.dot_general(kwin[pl.ds(r0, KB), cs], q2, (((1,), (1,)), ((), ())),
                                           preferred_element_type=F32)
            inv_l = []
            for p in range(N_PAIRS):
                bias2 = jnp.concatenate([bias_ref[0, 2 * p], bias_ref[0, 2 * p + 1]], axis=1)
                logits = jnp.where(kvalid, s_scr[p] + bias2, NEG)
                m = jnp.max(logits, axis=0, keepdims=True)
                pt = jnp.exp(logits - m)
                l = jnp.sum(pt, axis=0, keepdims=True)
                p_scr[p] = pt.astype(BF16)
                inv_l.append(1.0 / l)
                lse2 = m + jnp.log(l)
                lse_rows = jnp.where(hrow == 2 * p, lse2[:, 0:QB], lse_rows)
                lse_rows = jnp.where(hrow == 2 * p + 1, lse2[:, QB:2 * QB], lse_rows)
            for p, cs in enumerate(cols):
                s_scr[p, 0:LANES, :] = lax.dot_general(vwin[pl.ds(r0, KB), cs], p_scr[p], (((0,), (0,)), ((), ())),
                                                       preferred_element_type=F32) * inv_l[p]
            for p, cs in enumerate(cols):
                ot2 = s_scr[p, 0:LANES, :]
                ot = jnp.where(chan_lo, ot2[:, 0:QB], ot2[:, QB:2 * QB])
                o_ref[pl.ds(r0, QB), cs] = ot.T.astype(BF16)
            lse_ref[pl.ds(r0, QB), :] = lse_rows.T
            return carry

        lax.fori_loop(0, n_sub, sub, 0)

    main = pl.BlockSpec((T, ATTN_W), lambda r, i: (i, r))
    hbm = pl.BlockSpec(memory_space=pl.ANY)
    o_g, lse_g = pl.pallas_call(
        body, name=f"attn_fwd_d{dil}", grid=(dil, n_i),
        out_shape=(jax.ShapeDtypeStruct((L, dil * ATTN_W), BF16), jax.ShapeDtypeStruct((L, dil * LANES), F32)),
        in_specs=[hbm, hbm, hbm,
                  pl.BlockSpec((1, ATTN_W), lambda r, i: (0, 0)),
                  pl.BlockSpec((1, N_HEADS, KB, QB), lambda r, i: (g, 0, 0, 0))],
        out_specs=(main, pl.BlockSpec((T, LANES), lambda r, i: (i, r))),
        scratch_shapes=[pltpu.VMEM((N_SLOTS, T, ATTN_W), BF16),
                        pltpu.VMEM((N_SLOTS, T + 2 * HALF, ATTN_W), BF16),
                        pltpu.VMEM((N_SLOTS, T + 2 * HALF, ATTN_W), BF16),
                        pltpu.VMEM((N_PAIRS, KB, 2 * QB), F32), pltpu.VMEM((N_PAIRS, KB, 2 * QB), BF16),
                        pltpu.SemaphoreType.DMA((N_SLOTS, 7))],
        compiler_params=_params(("arbitrary", "arbitrary"), 40),
    )(qv, kv, vv, gain, tables)
    return o_g, lse_g


HALO = 16
CHUNK = 128


def _row_halo_specs(width, tm, S, col=0):
    rows = HALO
    per = tm // rows
    last = S // rows - 1
    prev = pl.BlockSpec((rows, width), lambda i: (jnp.maximum(i * per - 1, 0), col))
    nxt = pl.BlockSpec((rows, width), lambda i: (jnp.minimum((i + 1) * per, last), col))
    return prev, nxt


def _shift_rows(a, prev_row, next_row, tm):
    row = lax.broadcasted_iota(jnp.int32, a.shape, 0)
    a_m1 = jnp.where(row == 0, prev_row, pltpu.roll(a, 1, 0))
    a_p1 = jnp.where(row == tm - 1, next_row, pltpu.roll(a, tm - 1, 0))
    return a_m1, a_p1


def _combine_gates(o_parts, lse_parts, u, gb, gc, zc, za, conv_w, conv_b):
    S = u.shape[0]
    tm = 512
    n_t = S // tm

    def body(o1, o2, o3, l1, l2, l3, u_ref, up_ref, un_ref, gb_ref, gc_ref, gcp_ref, gcn_ref, zc_ref, za_ref,
             cw_ref, cb_ref, y_ref, o_ref, lse_ref, lse2_ref, lse3_ref, so2, so3, sl2, sl3, slse):
        i = pl.program_id(0)
        for blk, scr, dil in ((o2, so2, DILATIONS[1]), (o3, so3, DILATIONS[2]),
                              (l2, sl2, DILATIONS[1]), (l3, sl3, DILATIONS[2])):
            _from_view(blk, scr, dil)
        ls = [l1[...], sl2[0], sl3[0]]
        lmax = jnp.maximum(jnp.maximum(ls[0], ls[1]), ls[2])
        es = [jnp.exp(l - lmax) for l in ls]
        den = es[0] + es[1] + es[2]
        slse[0] = lmax + jnp.log(den)
        lse_ref[...] = slse[0]
        _to_view(slse, lse2_ref, DILATIONS[1], F32)
        _to_view(slse, lse3_ref, DILATIONS[2], F32)
        inv = 1.0 / den
        ws = [e * inv for e in es]
        for p in range(N_PAIRS):
            cs = slice(p * LANES, (p + 1) * LANES)
            acc = jnp.zeros((tm, LANES), F32)
            spread = _head_to_lanes(p)
            for w, o_g in zip(ws, (o1[:, cs].astype(F32), so2[p], so3[p])):
                acc = acc + _split_dot(w, spread) * o_g
            o_ref[:, cs] = acc.astype(BF16)
            za = za_ref[:, cs].astype(F32)
            y_ref[:, CONV_W + p * LANES:CONV_W + (p + 1) * LANES] = (acc * za * _sigmoid(za)).astype(BF16)
        a = gc_ref[...].astype(F32) * u_ref[...].astype(F32)
        a_prev = gcp_ref[15:16, :].astype(F32) * up_ref[15:16, :].astype(F32) * (i > 0).astype(F32)
        a_next = gcn_ref[0:1, :].astype(F32) * un_ref[0:1, :].astype(F32) * (i < n_t - 1).astype(F32)
        a_m1, a_p1 = _shift_rows(a, a_prev, a_next, tm)
        z3 = cw_ref[0:1, :] * a_m1 + cw_ref[1:2, :] * a + cw_ref[2:3, :] * a_p1
        zc = zc_ref[...].astype(F32)
        y_ref[:, 0:CONV_W] = (gb_ref[...].astype(F32) * (z3 + cb_ref[...]) * zc * _sigmoid(zc)).astype(BF16)

    pspec = pl.BlockSpec((tm, CONV_W), lambda i: (i, 0))
    hp, hn = _row_halo_specs(CONV_W, tm, S)
    wide_specs = [_view_spec(tm, d, ATTN_W) for d in DILATIONS]
    stat_specs = [_view_spec(tm, d, LANES) for d in DILATIONS]
    outs = pl.pallas_call(
        body, name="combine_gates", grid=(n_t,),
        out_shape=[jax.ShapeDtypeStruct((S, D_MODEL), BF16), jax.ShapeDtypeStruct((S, ATTN_W), BF16)]
        + [_view_shape(S, d, LANES, F32) for d in DILATIONS],
        in_specs=wide_specs + stat_specs + [pspec, hp, hn, pspec, pspec, hp, hn, pspec, pspec,
                                            pl.BlockSpec((3, CONV_W), lambda i: (0, 0)),
                                            pl.BlockSpec((1, CONV_W), lambda i: (0, 0))],
        out_specs=[pl.BlockSpec((tm, D_MODEL), lambda i: (i, 0)), pspec] + stat_specs,
        scratch_shapes=[_tile_scratch(tm, ATTN_W), _tile_scratch(tm, ATTN_W),
                        _tile_scratch(tm, LANES), _tile_scratch(tm, LANES), _tile_scratch(tm, LANES)],
        compiler_params=_params(("arbitrary",), 40),
    )(*o_parts, *lse_parts, u, u, u, gb, gc, gc, gc, zc, za, conv_w, conv_b.reshape(1, CONV_W))
    return outs[0], outs[1], outs[2:]


def _out_proj(y, x, target, wo):
    S = x.shape[0]
    tm = 512

    def body(y_ref, x_ref, t_ref, wo_ref, g_ref, dy_ref, dwo_ref, loss_ref):
        i = pl.program_id(0)

        @pl.when(i == 0)
        def _():
            dwo_ref[...] = jnp.zeros_like(dwo_ref)
            loss_ref[...] = jnp.zeros_like(loss_ref)

        yv = y_ref[...]
        wo_v = wo_ref[...]
        err = x_ref[...] + jnp.dot(yv, wo_v, preferred_element_type=F32) - t_ref[...]
        e2 = (err * err).reshape(tm // 8, 8, D_MODEL).sum(axis=0)
        part = e2[:, 0:LANES]
        for k in range(1, D_MODEL // LANES):
            part = part + e2[:, k * LANES:(k + 1) * LANES]
        loss_ref[...] += part
        gv = (err * (1.0 / D_MODEL)).astype(BF16)
        g_ref[...] = gv
        dy_ref[...] = lax.dot_general(gv, wo_v, (((1,), (1,)), ((), ())), preferred_element_type=F32).astype(BF16)
        dwo_ref[...] += lax.dot_general(yv, gv, (((0,), (0,)), ((), ())), preferred_element_type=F32)

    tile = pl.BlockSpec((tm, D_MODEL), lambda i: (i, 0))
    return pl.pallas_call(
        body, name="out_proj", grid=(S // tm,),
        out_shape=(jax.ShapeDtypeStruct((S, D_MODEL), BF16), jax.ShapeDtypeStruct((S, D_MODEL), BF16),
                   jax.ShapeDtypeStruct((D_MODEL, D_MODEL), F32), jax.ShapeDtypeStruct((8, LANES), F32)),
        in_specs=[tile, tile, tile, pl.BlockSpec((D_MODEL, D_MODEL), lambda i: (0, 0), pipeline_mode=pl.Buffered(1))],
        out_specs=(tile, tile, pl.BlockSpec((D_MODEL, D_MODEL), lambda i: (0, 0)),
                   pl.BlockSpec((8, LANES), lambda i: (0, 0))),
        compiler_params=_params(("arbitrary",), 48),
    )(y, x, target, wo)


def _gates_bwd(dy, u, gb, gc, zc, za, o, conv_w, conv_b):
    S = u.shape[0]
    tm = 512
    n_t = S // tm

    def body(dy_ref, dyp_ref, dyn_ref, u_ref, up_ref, un_ref, gb_ref, gbp_ref, gbn_ref, gc_ref, gcp_ref, gcn_ref,
             zc_ref, zcp_ref, zcn_ref, za_ref, o_ref, cw_ref, cb_ref,
             d5_ref, small_ref,
             do_ref, do2_ref, do3_ref, delta_ref, delta2_ref, delta3_ref, sdo, sdelta,
             u_ext, gc_ext, gb_ext, zc_ext, dy_ext):
        i = pl.program_id(0)

        @pl.when(i == 0)
        def _():
            small_ref[...] = jnp.zeros_like(small_ref)

        exts = ((u_ext, up_ref, u_ref, un_ref), (gc_ext, gcp_ref, gc_ref, gcn_ref), (gb_ext, gbp_ref, gb_ref, gbn_ref),
                (zc_ext, zcp_ref, zc_ref, zcn_ref))
        for ext, pr, mn, nx in exts:
            ext[0:HALO] = pr[...]
            ext[HALO:HALO + tm] = mn[...]
            ext[HALO + tm:] = nx[...]
        dy_ext[0:HALO] = dyp_ref[...]
        dy_ext[HALO:HALO + tm] = dy_ref[:, 0:CONV_W]
        dy_ext[HALO + tm:] = dyn_ref[...]
        zero_halo = jnp.zeros((HALO, CONV_W), BF16)

        @pl.when(i == 0)
        def _():
            u_ext[0:HALO] = zero_halo
            dy_ext[0:HALO] = zero_halo

        @pl.when(i == n_t - 1)
        def _():
            u_ext[HALO + tm:] = zero_halo
            dy_ext[HALO + tm:] = zero_halo

        wide_rows = CHUNK + 2 * HALO
        mid = slice(HALO, HALO + CHUNK)
        csum = lambda t: jnp.sum(t, axis=0, keepdims=True)

        def one_group(p, c0, sums):
            cs = slice(p * LANES, (p + 1) * LANES)
            w0, w1, w2, cb = cw_ref[0:1, cs], cw_ref[1:2, cs], cw_ref[2:3, cs], cb_ref[:, cs]
            rows = pl.ds(c0, CHUNK)
            wide = pl.ds(c0, wide_rows)
            u_e, gc_e, gb_e, zc_e, dy_e = (r[wide, cs].astype(F32) for r in (u_ext, gc_ext, gb_ext, zc_ext, dy_ext))
            a_e = gc_e * u_e
            sz_e = _sigmoid(zc_e)
            dz3_e = dy_e * zc_e * sz_e * gb_e
            a, a_m1, a_p1 = a_e[mid], pltpu.roll(a_e, 1, 0)[mid], pltpu.roll(a_e, wide_rows - 1, 0)[mid]
            dz3, dz3_m1, dz3_p1 = dz3_e[mid], pltpu.roll(dz3_e, 1, 0)[mid], pltpu.roll(dz3_e, wide_rows - 1, 0)[mid]
            uv, gcv, gbv, zcv, dyc, sz = u_e[mid], gc_e[mid], gb_e[mid], zc_e[mid], dy_e[mid], sz_e[mid]
            cvp = w0 * a_m1 + w1 * a + w2 * a_p1 + cb
            dcv = dyc * zcv * sz
            d5_ref[3, rows, cs] =(dyc * gbv * cvp * (sz * (1.0 + zcv * (1.0 - sz)))).astype(BF16)
            d5_ref[1, rows, cs] =(dcv * cvp).astype(BF16)
            da = w0 * dz3_p1 + w1 * dz3 + w2 * dz3_m1
            d5_ref[0, rows, cs] =(da * gcv).astype(BF16)
            d5_ref[2, rows, cs] =(da * uv).astype(BF16)
            sums = (sums[0] + csum(dz3 * a_m1), sums[1] + csum(dz3 * a), sums[2] + csum(dz3 * a_p1),
                    sums[3] + csum(dz3))

            dya = dy_ref[rows, CONV_W + p * LANES:CONV_W + (p + 1) * LANES].astype(F32)
            zav = za_ref[rows, cs].astype(F32)
            ov = o_ref[rows, cs].astype(F32)
            sa = _sigmoid(zav)
            d_o = dya * zav * sa
            sdo[p, rows, :] = d_o
            do_ref[rows, cs] = d_o.astype(BF16)
            d5_ref[4, rows, cs] =(dya * ov * (sa * (1.0 + zav * (1.0 - sa)))).astype(BF16)
            return sums, _split_dot(d_o * ov, _lanes_to_head(p, 1.0))

        def chunk(ci, carry):
            c0 = pl.multiple_of(ci * CHUNK, CHUNK)
            out, delta = [], jnp.zeros((CHUNK, LANES), F32)
            for p in range(N_PAIRS):
                sums, part = one_group(p, c0, carry[p])
                out.append(sums)
                delta = delta + part
            sdelta[0, pl.ds(c0, CHUNK), :] = delta
            return tuple(out)

        z = jnp.zeros((1, LANES), F32)
        all_sums = lax.fori_loop(0, tm // CHUNK, chunk, ((z, z, z, z),) * N_PAIRS)
        for p in range(N_PAIRS):
            for k in range(4):
                small_ref[k:k + 1, p * LANES:(p + 1) * LANES] += all_sums[p][k]
        delta_ref[...] = sdelta[0]
        for dil, wide_ref, stat_ref in zip(VIEW_DILATIONS, (do2_ref, do3_ref), (delta2_ref, delta3_ref)):
            _to_view(sdo, wide_ref, dil, BF16)
            _to_view(sdelta, stat_ref, dil, F32)

    pspec = pl.BlockSpec((tm, CONV_W), lambda i: (i, 0))
    hp, hn = _row_halo_specs(CONV_W, tm, S)
    piece = jax.ShapeDtypeStruct((S, CONV_W), BF16)
    outs = pl.pallas_call(
        body, name="gates_bwd", grid=(n_t,),
        out_shape=[jax.ShapeDtypeStruct((5, S, CONV_W), BF16), jax.ShapeDtypeStruct((8, CONV_W), F32)]
        + [_view_shape(S, d, ATTN_W, BF16) for d in DILATIONS] + [_view_shape(S, d, LANES, F32) for d in DILATIONS],
        in_specs=[pl.BlockSpec((tm, D_MODEL), lambda i: (i, 0)), hp, hn,
                  pspec, hp, hn, pspec, hp, hn, pspec, hp, hn, pspec, hp, hn, pspec, pspec,
                  pl.BlockSpec((3, CONV_W), lambda i: (0, 0)), pl.BlockSpec((1, CONV_W), lambda i: (0, 0))],
        out_specs=[pl.BlockSpec((5, tm, CONV_W), lambda i: (0, i, 0)), pl.BlockSpec((8, CONV_W), lambda i: (0, 0))]
        + [_view_spec(tm, d, ATTN_W) for d in DILATIONS] + [_view_spec(tm, d, LANES) for d in DILATIONS],
        scratch_shapes=[_tile_scratch(tm, ATTN_W), _tile_scratch(tm, LANES)]
        + [pltpu.VMEM((tm + 2 * HALO, CONV_W), BF16)] * 5,
        compiler_params=_params(("arbitrary",), 40),
    )(dy, dy, dy, u, u, u, gb, gb, gb, gc, gc, gc, zc, zc, zc, za, o, conv_w, conv_b.reshape(1, CONV_W))
    return outs[0], outs[1], outs[2:5], outs[5:8]


def _attn_bwd(g, dil, qv, kv, vv, dov, lv, dv_, rqv, rkv, gain, tables):
    L = qv.shape[0]
    T = min(1024, L)
    n_sub = T // QB + 1
    QW = T + 2 * HALF
    KW = T + 2 * QB

    def body(qm_ref, qp_ref, qn_ref, dom_ref, dop_ref, don_ref, km_ref, kp_ref, kn_ref, vm_ref, vp_ref, vn_ref,
             lm_ref, lp_ref, ln_ref, dm_ref, dp_ref, dn_ref, rqm_ref, rqp_ref, rqn_ref, rk_ref,
             gain_ref, bias_ref,
             dq_ref, dk_ref, dv_ref, ds_ref, a_ref,
             qwin, dowin, kwin, vwin, lwin, dwin, rqwin, dqwin, dkacc, dvacc,
             s_scr, d_scr, p_scr, ds_scr, q2_scr, do2_scr, dq_scr):
        r = pl.program_id(0)
        i = pl.program_id(1)

        @pl.when((r == 0) & (i == 0))
        def _():
            ds_ref[...] = jnp.zeros_like(ds_ref)
            a_ref[...] = jnp.zeros_like(a_ref)

        for win, (pr, mn, nx), h_rows in ((qwin, (qp_ref, qm_ref, qn_ref), HALF),
                                          (dowin, (dop_ref, dom_ref, don_ref), HALF),
                                          (lwin, (lp_ref, lm_ref, ln_ref), HALF),
                                          (dwin, (dp_ref, dm_ref, dn_ref), HALF),
                                          (rqwin, (rqp_ref, rqm_ref, rqn_ref), HALF),
                                          (kwin, (kp_ref, km_ref, kn_ref), QB),
                                          (vwin, (vp_ref, vm_ref, vn_ref), QB)):
            win[0:h_rows] = pr[...]
            win[h_rows:h_rows + T] = mn[...]
            win[h_rows + T:] = nx[...]
        dkacc[0:QB] = jnp.zeros((QB, ATTN_W), F32)
        dvacc[0:QB] = jnp.zeros((QB, ATTN_W), F32)

        lo = _lane((QB, LANES)) < HEAD_DIM
        same_head = _same_head()
        krow = lax.broadcasted_iota(jnp.int32, (KB, 1), 0)
        qcol = _lane((1, 2 * QB)) % QB
        row = lax.broadcasted_iota(jnp.int32, (QB, 1), 0)
        chan_lo = lax.broadcasted_iota(jnp.int32, (LANES, 1), 0) < HEAD_DIM

        cols = [slice(p * LANES, (p + 1) * LANES) for p in range(N_PAIRS)]
        nt = (((1,), (1,)), ((), ()))

        def scores(j):
            r0 = pl.multiple_of(j * QB, QB)
            for k, cs in enumerate(cols):
                qf = qwin[pl.ds(r0, QB), cs].astype(F32) * gain_ref[:, cs]
                q2_scr[k] = jnp.concatenate([jnp.where(lo, qf, 0.0), jnp.where(lo, 0.0, qf)], axis=0).astype(BF16)
                dov = dowin[pl.ds(r0, QB), cs]
                zero = jnp.zeros_like(dov)
                do2_scr[k] = jnp.concatenate([jnp.where(lo, dov, zero), jnp.where(lo, zero, dov)], axis=0)
                s_scr[k] = lax.dot_general(kwin[pl.ds(r0, KB), cs], q2_scr[k], nt, preferred_element_type=F32)
                d_scr[k] = lax.dot_general(vwin[pl.ds(r0, KB), cs], do2_scr[k], nt, preferred_element_type=F32)

        def sub(j):
            r0 = pl.multiple_of(j * QB, QB)
            kpos = i * T - QB + j * QB + krow
            qpos = i * T - HALF + j * QB + qcol
            valid = (kpos >= 0) & (kpos < L) & (qpos >= 0) & (qpos < L)
            owned_t = ((qpos >= i * T) & (qpos < i * T + T)).astype(F32)
            qpos_c = i * T - HALF + j * QB + row
            owned = ((qpos_c >= i * T) & (qpos_c < i * T + T)).astype(F32)
            lse_rows = lwin[pl.ds(r0, QB), :].T
            delta_rows = dwin[pl.ds(r0, QB), :].T
            rq_t = rqwin[pl.ds(r0, QB), :]
            for p in range(N_PAIRS):
                k = p
                bias2 = jnp.concatenate([bias_ref[0, 2 * p], bias_ref[0, 2 * p + 1]], axis=1)
                lse2 = jnp.concatenate([lse_rows[2 * p:2 * p + 1, :], lse_rows[2 * p + 1:2 * p + 2, :]], axis=1)
                delta2 = jnp.concatenate([delta_rows[2 * p:2 * p + 1, :], delta_rows[2 * p + 1:2 * p + 2, :]], axis=1)
                pt = jnp.where(valid, jnp.exp(s_scr[k] + bias2 - lse2), 0.0)
                dst = pt * (d_scr[k] - delta2)
                ds_ref[p] += dst * owned_t
                p_scr[p] = pt.astype(BF16)
                ds_scr[p] = dst.astype(BF16)
            for p, cs in enumerate(cols):
                k = p
                for acc_ref, lhs, rhs in ((dvacc, p_scr[p], do2_scr[k]), (dkacc, ds_scr[p], q2_scr[k])):
                    part = jnp.dot(lhs, rhs, preferred_element_type=F32)
                    acc_ref[pl.ds(r0, QB), cs] += part[0:QB]
                    acc_ref[pl.ds(r0 + QB, QB), cs] = part[QB:KB]
                dq_scr[p] = lax.dot_general(kwin[pl.ds(r0, KB), cs], ds_scr[p], (((0,), (0,)), ((), ())),
                                            preferred_element_type=F32)
            for p, cs in enumerate(cols):
                g2 = gain_ref[:, cs]
                q_f = qwin[pl.ds(r0, QB), cs].astype(F32)
                dqt2 = dq_scr[p]
                dqm2 = jnp.where(chan_lo, dqt2[:, 0:QB], dqt2[:, QB:2 * QB]).T
                a_ref[p] += (dqm2 * q_f * owned).reshape(QB // 8, 8, LANES).sum(axis=0)
                dqh = dqm2 * g2
                mean_c = _split_dot(dqh * q_f, same_head) * (1.0 / HEAD_DIM)
                rq2 = _split_dot(rq_t, _head_to_lanes(p))
                dqwin[pl.ds(r0, QB), cs] = (rq2 * (dqh - q_f * mean_c)).astype(BF16)

        def one_sub(j, carry):
            scores(j)
            sub(j)
            return carry

        lax.fori_loop(0, n_sub, one_sub, 0)
        dq_ref[...] = dqwin[HALF:HALF + T]

        def finish(j, carry):
            r0 = pl.multiple_of(j * QB, QB)
            rk_t = rk_ref[pl.ds(r0, QB), :]
            for p in range(N_PAIRS):
                cs = slice(p * LANES, (p + 1) * LANES)
                dkh = dkacc[pl.ds(QB + r0, QB), cs]
                k_f = km_ref[pl.ds(r0, QB), cs].astype(F32)
                mean_c = _split_dot(dkh * k_f, same_head) * (1.0 / HEAD_DIM)
                rk2 = _split_dot(rk_t, _head_to_lanes(p))
                dk_ref[pl.ds(r0, QB), cs] = (rk2 * (dkh - k_f * mean_c)).astype(BF16)
                dv_ref[pl.ds(r0, QB), cs] = dvacc[pl.ds(QB + r0, QB), cs].astype(BF16)
            return carry

        lax.fori_loop(0, T // QB, finish, 0)

    main = pl.BlockSpec((T, ATTN_W), lambda r, i: (i, r))
    smain = pl.BlockSpec((T, LANES), lambda r, i: (i, r))
    q_prev, q_next = _halo_specs(ATTN_W, HALF, T, L, lambda r: r)
    k_prev, k_next = _halo_specs(ATTN_W, QB, T, L, lambda r: r)
    s_prev, s_next = _halo_specs(LANES, HALF, T, L, lambda r: r)
    piece = jax.ShapeDtypeStruct((L, dil * ATTN_W), BF16)
    return pl.pallas_call(
        body, name=f"attn_bwd_d{dil}", grid=(dil, L // T),
        out_shape=(piece, piece, piece, jax.ShapeDtypeStruct((N_PAIRS, KB, 2 * QB), F32),
                   jax.ShapeDtypeStruct((N_PAIRS, 8, LANES), F32)),
        in_specs=[main, q_prev, q_next, main, q_prev, q_next, main, k_prev, k_next, main, k_prev, k_next,
                  smain, s_prev, s_next, smain, s_prev, s_next, smain, s_prev, s_next, smain,
                  pl.BlockSpec((1, ATTN_W), lambda r, i: (0, 0)),
                  pl.BlockSpec((1, N_HEADS, KB, QB), lambda r, i: (g, 0, 0, 0))],
        out_specs=(main, main, main, pl.BlockSpec((N_PAIRS, KB, 2 * QB), lambda r, i: (0, 0, 0)),
                   pl.BlockSpec((N_PAIRS, 8, LANES), lambda r, i: (0, 0, 0))),
        scratch_shapes=[pltpu.VMEM((QW, ATTN_W), BF16), pltpu.VMEM((QW, ATTN_W), BF16),
                        pltpu.VMEM((KW, ATTN_W), BF16), pltpu.VMEM((KW, ATTN_W), BF16),
                        pltpu.VMEM((QW, LANES), F32), pltpu.VMEM((QW, LANES), F32), pltpu.VMEM((QW, LANES), F32),
                        pltpu.VMEM((QW, ATTN_W), BF16),
                        pltpu.VMEM((KW, ATTN_W), F32), pltpu.VMEM((KW, ATTN_W), F32),
                        pltpu.VMEM((N_PAIRS, KB, 2 * QB), F32), pltpu.VMEM((N_PAIRS, KB, 2 * QB), F32),
                        pltpu.VMEM((N_PAIRS, KB, 2 * QB), BF16), pltpu.VMEM((N_PAIRS, KB, 2 * QB), BF16),
                        pltpu.VMEM((N_PAIRS, 2 * QB, LANES), BF16), pltpu.VMEM((N_PAIRS, 2 * QB, LANES), BF16),
                        pltpu.VMEM((N_PAIRS, LANES, 2 * QB), F32)],
        compiler_params=_params(("arbitrary", "arbitrary"), 56),
    )(qv, qv, qv, dov, dov, dov, kv, kv, kv, vv, vv, vv, lv, lv, lv, dv_, dv_, dv_, rqv, rqv, rqv, rkv, gain, tables)


D5_GROUPS = (0, 1, 2, 3, 7)
QKV_GROUPS = (4, 5, 6)


def _proj_bwd(x, d_out, norm_w, wg, d5, qkv_shares):
    S = x.shape[0]
    tm = 512
    flat = [share for p in qkv_shares for share in p]
    n_t = S // tm

    def body(*refs):
        x_ref, g_ref, nw_ref, wg_ref, d5_ref = refs[:5]
        piece_refs = refs[5:5 + len(flat)]
        gx_ref, dnw_ref, ht_ref, dqkv_ref, sview = refs[5 + len(flat):]
        i = pl.program_id(0)

        @pl.when(i == 0)
        def _():
            dnw_ref[...] = jnp.zeros_like(dnw_ref)

        xv = x_ref[...]
        r = lax.rsqrt(jnp.mean(xv * xv, axis=-1, keepdims=True) + EPS)
        xh = xv * r
        nw = nw_ref[...]
        ht_ref[...] = (xh * nw).T.astype(BF16)
        dh = jnp.zeros((tm, D_MODEL), F32)
        k = 0
        for j in range(8):
            if j in D5_GROUPS:
                dp = d5_ref[D5_GROUPS.index(j)]
            else:
                shares = qkv_shares[QKV_GROUPS.index(j)]
                dp = jnp.zeros((tm, CONV_W), F32)
                for t, (_, dil) in enumerate(shares):
                    if dil == 1:
                        dp = dp + piece_refs[k + t][...].astype(F32)
                    else:
                        _from_view(piece_refs[k + t], sview, dil)
                        dp = dp + _scratch_value(sview)
                dp = dp.astype(BF16)
                k += len(shares)
                dqkv_ref[QKV_GROUPS.index(j)] = dp
            dh = dh + lax.dot_general(dp, wg_ref[j], (((1,), (1,)), ((), ())), preferred_element_type=F32)
        dnw_ref[...] += (dh * xh).reshape(tm // 8, 8, D_MODEL).sum(axis=0)
        dxh = dh * nw
        mean_c = jnp.mean(dxh * xh, axis=-1, keepdims=True)
        gx_ref[...] = g_ref[...].astype(F32) + r * (dxh - xh * mean_c)

    tile = pl.BlockSpec((tm, D_MODEL), lambda i: (i, 0))
    return pl.pallas_call(
        body, name="proj_bwd", grid=(n_t,),
        out_shape=(jax.ShapeDtypeStruct((S, D_MODEL), F32), jax.ShapeDtypeStruct((8, D_MODEL), F32),
                   jax.ShapeDtypeStruct((D_MODEL, S), BF16), jax.ShapeDtypeStruct((3, S, CONV_W), BF16)),
        in_specs=[tile, tile, pl.BlockSpec((1, D_MODEL), lambda i: (0, 0)),
                  pl.BlockSpec(wg.shape, lambda i: (0, 0, 0), pipeline_mode=pl.Buffered(1)),
                  pl.BlockSpec((5, tm, CONV_W), lambda i: (0, i, 0))]
        + [_view_spec(tm, dil, CONV_W) for _, dil in flat],
        out_specs=(tile, pl.BlockSpec((8, D_MODEL), lambda i: (0, 0)),
                   pl.BlockSpec((D_MODEL, tm), lambda i: (0, i)), pl.BlockSpec((3, tm, CONV_W), lambda i: (0, i, 0))),
        scratch_shapes=[_tile_scratch(tm, CONV_W)],
        compiler_params=_params(("arbitrary",), 56),
    )(x, d_out, norm_w.reshape(1, D_MODEL), wg, d5, *[a for a, _ in flat])


def _dw_exchange(ht, d5, dqkv, gw_out, small):
    D, S = ht.shape
    tk = 2048
    n_t = S // tk
    me_outer = 4 * lax.axis_index("x") + 2 * lax.axis_index("y") + lax.axis_index("c")
    order = ((me_outer + 1 + jnp.arange(N_DEV, dtype=jnp.int32)) % N_DEV).astype(jnp.int32)
    in_d5 = ((order < 4) | (order == 7)).astype(jnp.int32)
    at_d5 = jnp.where(order < 4, order, jnp.where(order == 7, 4, 0)).astype(jnp.int32)
    at_qkv = jnp.where(in_d5 == 1, 0, order - 4).astype(jnp.int32)

    def body(in_d5_ref, at_d5_ref, at_qkv_ref, ht_hbm, d5_ref, dqkv_ref, gout_ref, sm_ref, rin_ref, rout_ref, rsm_ref,
             acc, sbuf, ht_vmem, in_send, in_recv, side_send, side_recv, local_sems, ht_sems):
        s = pl.program_id(0)
        t = pl.program_id(1)
        x, y, c = lax.axis_index("x"), lax.axis_index("y"), lax.axis_index("c")
        me = 4 * x + 2 * y + c

        def ht_copy(j):
            return pltpu.make_async_copy(ht_hbm.at[:, j * tk:(j + 1) * tk], ht_vmem.at[j], ht_sems.at[j])

        def in_copy(k):
            to = (me + 1 + k) % N_DEV
            return pltpu.make_async_remote_copy(
                src_ref=sbuf.at[k % 2], dst_ref=rin_ref.at[me], send_sem=in_send.at[k], recv_sem=in_recv.at[k],
                device_id=(to // 4, (to // 2) % 2, to % 2), device_id_type=MESH)

        def in_landing(k):
            frm = (me + 2 * N_DEV - 1 - k) % N_DEV
            return pltpu.make_async_remote_copy(
                src_ref=sbuf.at[0], dst_ref=rin_ref.at[frm], send_sem=in_send.at[k], recv_sem=in_recv.at[k],
                device_id=(x, y, c), device_id_type=MESH)

        def side_copies():
            local = [pltpu.make_async_copy(gout_ref.at[me], rout_ref.at[me], local_sems.at[0]),
                     pltpu.make_async_copy(sm_ref, rsm_ref.at[me], local_sems.at[1])]
            remote = []
            for k in range(1, N_DEV):
                px = 1 - x if k & 4 else x
                py = 1 - y if k & 2 else y
                pc = 1 - c if k & 1 else c
                peer = 4 * px + 2 * py + pc
                for a, (src, dst) in enumerate(((gout_ref.at[peer], rout_ref.at[me]), (sm_ref, rsm_ref.at[me]))):
                    remote.append(pltpu.make_async_remote_copy(
                        src_ref=src, dst_ref=dst, send_sem=side_send.at[a * 7 + k - 1],
                        recv_sem=side_recv.at[a * 7 + k - 1], device_id=(px, py, pc), device_id_type=MESH))
            return local, remote

        @pl.when((s == 0) & (t == 0))
        def _():
            for j in range(n_t):
                ht_copy(j).start()
            local, remote = side_copies()
            for cp in local + remote:
                cp.start()

        for j in range(n_t):
            @pl.when((s == 0) & (t == j))
            def _(j=j):
                ht_copy(j).wait()

        @pl.when(t == 0)
        def _():
            acc[...] = jnp.zeros_like(acc)

        dp = jnp.where(in_d5_ref[s] == 1, d5_ref[0], dqkv_ref[0])
        acc[...] += jnp.dot(ht_vmem[t], dp, preferred_element_type=F32)

        for k in range(N_DEV):
            @pl.when((s == k) & (t == n_t - 1))
            def _(k=k):
                if k >= 2:
                    in_copy(k - 2).wait_send()
                sbuf[k % 2] = acc[...].astype(BF16)
                if k < N_DEV - 1:
                    in_copy(k).start()
                else:
                    own = pltpu.make_async_copy(sbuf.at[k % 2], rin_ref.at[me], local_sems.at[2])
                    own.start()
                    in_copy(k - 1).wait_send()
                    for j in range(N_DEV - 1):
                        in_landing(j).wait_recv()
                    local, remote = side_copies()
                    for cp in remote:
                        cp.wait_recv()
                    for cp in remote:
                        cp.wait_send()
                    for cp in local:
                        cp.wait()
                    own.wait()

    hbm = pl.BlockSpec(memory_space=pl.ANY)
    grid_spec = pltpu.PrefetchScalarGridSpec(
        num_scalar_prefetch=3, grid=(N_DEV, n_t),
        in_specs=[hbm,
                  pl.BlockSpec((1, tk, CONV_W), lambda s, t, use, a5, a3: (a5[s], jnp.where(use[s] == 1, t, 0), 0)),
                  pl.BlockSpec((1, tk, CONV_W), lambda s, t, use, a5, a3: (a3[s], jnp.where(use[s] == 1, 0, t), 0)),
                  hbm, hbm],
        out_specs=(hbm, hbm, hbm),
        scratch_shapes=[pltpu.VMEM((D, CONV_W), F32), pltpu.VMEM((2, D, CONV_W), BF16),
                        pltpu.VMEM((n_t, D, tk), BF16),
                        pltpu.SemaphoreType.DMA((N_DEV - 1,)), pltpu.SemaphoreType.DMA((N_DEV - 1,)),
                        pltpu.SemaphoreType.DMA((14,)), pltpu.SemaphoreType.DMA((14,)),
                        pltpu.SemaphoreType.DMA((3,)), pltpu.SemaphoreType.DMA((n_t,))])
    return pl.pallas_call(
        body, name="dw_exchange", grid_spec=grid_spec,
        out_shape=(jax.ShapeDtypeStruct((N_DEV, D, CONV_W), BF16), jax.ShapeDtypeStruct(gw_out.shape, F32),
                   jax.ShapeDtypeStruct((N_DEV,) + small.shape, F32)),
        compiler_params=_params(("arbitrary", "arbitrary"), 52),
    )(in_d5, at_d5, at_qkv, ht, d5, dqkv, gw_out, small)


def _adamw_math(w, g, m, v):
    m2 = ADAM_B1 * m + (1.0 - ADAM_B1) * g
    v2 = ADAM_B2 * v + (1.0 - ADAM_B2) * (g * g)
    m_hat = m2 / (1.0 - ADAM_B1 ** ADAM_STEP)
    v_hat = v2 / (1.0 - ADAM_B2 ** ADAM_STEP)
    delta = -ADAM_LR * (m_hat / (jnp.sqrt(v_hat) + ADAM_EPS) + ADAM_WD * w)
    return delta, m2, v2


def _adamw_sharded(name, parts, w, m, v, rows):
    R, C = w.shape

    def body(p_ref, w_ref, m_ref, v_ref, g_ref, d_ref, m2_ref, v2_ref):
        g = p_ref[0].astype(F32)
        for s in range(1, N_DEV):
            g = g + p_ref[s].astype(F32)
        g_ref[...] = g
        d_ref[...], m2_ref[...], v2_ref[...] = _adamw_math(w_ref[...], g, m_ref[...], v_ref[...])

    spec = pl.BlockSpec((rows, C), lambda i: (i, 0))
    out = jax.ShapeDtypeStruct((R, C), F32)
    return pl.pallas_call(
        body, name=name, grid=(R // rows,),
        out_shape=(out,) * 4,
        in_specs=[pl.BlockSpec((N_DEV, rows, C), lambda i: (0, i, 0)), spec, spec, spec],
        out_specs=(spec,) * 4,
        compiler_params=_params(("arbitrary",), 40),
    )(parts, w, m, v)


def _adamw_small(parts, w, m, v):
    R = w.shape[0]

    def body(p_ref, w_ref, m_ref, v_ref, g_ref, d_ref, m2_ref, v2_ref):
        g = p_ref[0]
        for s in range(1, N_DEV):
            g = g + p_ref[s]
        g_ref[...] = g
        d_ref[...], m2_ref[...], v2_ref[...] = _adamw_math(w_ref[...], g, m_ref[...], v_ref[...])

    vm = pl.BlockSpec(memory_space=pltpu.VMEM)
    out = jax.ShapeDtypeStruct((R, LANES), F32)
    return pl.pallas_call(
        body, name="adamw_small", out_shape=(out,) * 4, in_specs=[vm] * 4, out_specs=(vm,) * 4,
    )(parts, w, m, v)


def _qk_gain_grads(a_parts, q_norm_w, k_norm_w, bias_g):
    def body(a_ref, qw_ref, kw_ref, bg_ref, out_ref):
        tot = jnp.zeros((8, LANES), F32)
        for g in range(len(DILATIONS)):
            for p in range(N_PAIRS):
                tot = tot + a_ref[g, p]
        col = jnp.sum(tot, axis=0, keepdims=True)
        a64 = col[:, 0:HEAD_DIM] + col[:, HEAD_DIM:LANES]
        out_ref[...] = jnp.zeros_like(out_ref)
        out_ref[0:1, 0:HEAD_DIM] = LOGIT_SCALE * kw_ref[...] * a64
        out_ref[1:2, 0:HEAD_DIM] = LOGIT_SCALE * qw_ref[...] * a64
        for h in range(N_HEADS):
            r0, c0 = 2 + h // 4, (h % 4) * N_BUCKETS
            out_ref[r0:r0 + 1, c0:c0 + N_BUCKETS] = bg_ref[h // 2, h % 2:h % 2 + 1, 0:N_BUCKETS]

    vm = pl.BlockSpec(memory_space=pltpu.VMEM)
    return pl.pallas_call(
        body, name="qk_gain_grads", out_shape=jax.ShapeDtypeStruct((8, LANES), F32),
        in_specs=[vm] * 4, out_specs=vm,
    )(a_parts, q_norm_w.reshape(1, HEAD_DIM), k_norm_w.reshape(1, HEAD_DIM), bias_g)


SMALL_ROWS = 32


def _pack_small(norm_w, conv_b, q_norm_w, k_norm_w, rel_bias, conv_w_full):
    pad64 = lambda a: jnp.pad(a, (0, LANES - HEAD_DIM)).reshape(1, LANES)
    return jnp.concatenate([
        norm_w.reshape(8, LANES), conv_b.reshape(4, LANES), pad64(q_norm_w), pad64(k_norm_w),
        rel_bias.T.reshape(2, LANES), conv_w_full.reshape(12, LANES), jnp.zeros((4, LANES), F32)], axis=0)


def _unpack_small(s):
    return (s[0:8].reshape(D_MODEL), s[8:12].reshape(CONV_W), s[12, 0:HEAD_DIM], s[13, 0:HEAD_DIM],
            s[14:16].reshape(N_HEADS, N_BUCKETS).T, s[16:28].reshape(3, CONV_W))


def kernel(x, norm_w, w_in, conv_w, conv_b, q_norm_w, k_norm_w, rel_bias, w_out, loss_target, m_norm_w, m_w_in, m_conv_w, m_conv_b, m_q_norm_w, m_k_norm_w, m_rel_bias, m_w_out, v_norm_w, v_w_in, v_conv_w, v_conv_b, v_q_norm_w, v_k_norm_w, v_rel_bias, v_w_out):
    S = x.shape[1]
    x2 = x.reshape(S, D_MODEL)
    tgt = loss_target.reshape(S, D_MODEL)
    me = 4 * lax.axis_index("x") + 2 * lax.axis_index("y") + lax.axis_index("c")
    shard_w = CONV_W // N_DEV

    cw_pad = jnp.pad(conv_w, ((0, 5), (0, LANES - shard_w)))
    wg, wo_g, cw_g = _ag_weights(w_in, w_out, cw_pad)
    wo = wo_g.reshape(D_MODEL, D_MODEL)
    conv_w_full = cw_g[:, 0:3, 0:shard_w].transpose(1, 0, 2).reshape(3, CONV_W)

    bmat_t = jnp.asarray(np.stack([_bucket_matrix(d).T for d in DILATIONS]))
    tables = _bias_tables(rel_bias, bmat_t)
    gain = jnp.tile(q_norm_w * k_norm_w * LOGIT_SCALE, N_HEADS).reshape(1, ATTN_W)

    u, gb, gc, zc, za, q_hat, k_hat, v, rq, rk = _fwd_proj(x2, norm_w, wg)
    o_parts, lse_parts = [], []
    for g, dil in enumerate(DILATIONS):
        o_g, lse_g = _attn_fwd(g, dil, q_hat[g], k_hat[g], v[g], gain, tables)
        o_parts.append(o_g)
        lse_parts.append(lse_g)
    y, o, lse = _combine_gates(o_parts, lse_parts, u, gb, gc, zc, za, conv_w_full, conv_b)
    d_out, dy, gwo_part, loss_part = _out_proj(y, x2, tgt, wo)
    loss_mine = jnp.sum(loss_part) * (0.5 / D_MODEL)

    d5, conv_small, d_o, delta = _gates_bwd(dy, u, gb, gc, zc, za, o, conv_w_full, conv_b)
    dqs, dks, dvs, ds_parts, a_parts = [], [], [], [], []
    for g, dil in enumerate(DILATIONS):
        dq, dk, dv, ds_sum, a_sum = _attn_bwd(g, dil, q_hat[g], k_hat[g], v[g], d_o[g], lse[g], delta[g], rq[g], rk[g],
                                              gain, tables)
        dqs.append((dq, dil))
        dks.append((dk, dil))
        dvs.append((dv, dil))
        ds_parts.append(ds_sum)
        a_parts.append(a_sum)
    grad_x, gnw_part, h_t, dqkv = _proj_bwd(x2, d_out, norm_w, wg, d5, [dqs, dks, dvs])
    bias_g = _bias_grad(jnp.stack(ds_parts), bmat_t)
    qk_small = _qk_gain_grads(jnp.stack(a_parts), q_norm_w, k_norm_w, bias_g)

    small_part = jnp.concatenate([
        gnw_part.sum(axis=0).reshape(8, LANES), conv_small[3].reshape(4, LANES), qk_small[0:4],
        conv_small[0:3].reshape(12, LANES), jnp.zeros((4, LANES), F32).at[0, 0].set(loss_mine)], axis=0)
    r_in, r_out, r_small = _dw_exchange(h_t, d5, dqkv, gwo_part.reshape(N_DEV, D_MODEL // N_DEV, D_MODEL), small_part)

    g_win, d_win, m_win, v_win = _adamw_sharded("adamw_w_in", r_in, w_in, m_w_in, v_w_in, 128)
    g_wo, d_wo, m_wo, v_wo = _adamw_sharded("adamw_w_out", r_out, w_out, m_w_out, v_w_out, 32)

    def full_conv(a):
        return jnp.zeros((3, CONV_W), F32).at[:, 0:shard_w].set(a)

    packs = [_pack_small(nw_, cb_, qw_, kw_, rb_, full_conv(cw_)) for nw_, cb_, qw_, kw_, rb_, cw_ in (
        (norm_w, conv_b, q_norm_w, k_norm_w, rel_bias, conv_w),
        (m_norm_w, m_conv_b, m_q_norm_w, m_k_norm_w, m_rel_bias, m_conv_w),
        (v_norm_w, v_conv_b, v_q_norm_w, v_k_norm_w, v_rel_bias, v_conv_w))]
    r_small_mine = r_small.at[:, 16:28, :].set(
        jnp.pad(lax.dynamic_slice_in_dim(r_small[:, 16:28, :].reshape(N_DEV, 3, CONV_W), me * shard_w, shard_w, axis=2),
                ((0, 0), (0, 0), (0, CONV_W - shard_w))).reshape(N_DEV, 12, LANES))
    outs_small = _adamw_small(r_small_mine, *packs)
    g_s, d_s, m_s, v_s = [_unpack_small(t) for t in outs_small]
    loss = outs_small[0][28, 0]

    def leaves(small, big_in, big_out):
        nw_, cb_, qw_, kw_, rb_, cwf = small
        return (nw_, big_in, cwf[:, 0:shard_w], cb_, qw_, kw_, rb_, big_out)

    return (loss, grad_x.reshape(x.shape),
            *leaves(g_s, g_win, g_wo), *leaves(d_s, d_win, d_wo), *leaves(m_s, m_win, m_wo), *leaves(v_s, v_win, v_wo))
```

```python
import math

import numpy as np
import jax
import jax.numpy as jnp
from jax import lax
from jax.experimental import pallas as pl
from jax.experimental.pallas import tpu as pltpu

F32 = jnp.float32
BF16 = jnp.bfloat16

N_DEV = 8
D_MODEL = 1024
CONV_W = 512
ATTN_W = 512
N_HEADS = 8
HEAD_DIM = 64
N_PAIRS = N_HEADS // 2
LANES = 128
HALF = 64
QB = 128
KB = QB + 2 * HALF
DILATIONS = (1, 4, 16)
N_BUCKETS = 32
MAX_DISTANCE = 1024
EPS = 1e-6
NEG = -1e30
LOGIT_SCALE = HEAD_DIM ** -0.5

ADAM_LR = 0.001
ADAM_B1 = 0.9
ADAM_B2 = 0.999
ADAM_EPS = 1e-08
ADAM_WD = 0.01
ADAM_STEP = 10

MESH = pl.DeviceIdType.MESH
MIB = 1024 * 1024


def _params(semantics, vmem_mib):
    return pltpu.CompilerParams(dimension_semantics=semantics, vmem_limit_bytes=vmem_mib * MIB)


def _lane(shape):
    return lax.broadcasted_iota(jnp.int32, shape, len(shape) - 1)


def _sigmoid(z):
    return 1.0 / (1.0 + jnp.exp(-z))


def _split_dot(x, w):
    hi = x.astype(BF16)
    lo = (x - hi.astype(F32)).astype(BF16)
    return jnp.dot(hi, w, preferred_element_type=F32) + jnp.dot(lo, w, preferred_element_type=F32)


def _same_head():
    r = lax.broadcasted_iota(jnp.int32, (LANES, LANES), 0) // HEAD_DIM
    c = lax.broadcasted_iota(jnp.int32, (LANES, LANES), 1) // HEAD_DIM
    return (r == c).astype(BF16)


def _head_to_lanes(p):
    r = lax.broadcasted_iota(jnp.int32, (LANES, LANES), 0)
    c = lax.broadcasted_iota(jnp.int32, (LANES, LANES), 1) // HEAD_DIM
    return (r == 2 * p + c).astype(BF16)


def _lanes_to_head(p, scale):
    r = lax.broadcasted_iota(jnp.int32, (LANES, LANES), 0) // HEAD_DIM
    c = lax.broadcasted_iota(jnp.int32, (LANES, LANES), 1)
    return jnp.where(c == 2 * p + r, scale, 0.0).astype(BF16)


VIEW_DILATIONS = DILATIONS[1:]


def _view_shape(S, dil, width, dtype):
    return jax.ShapeDtypeStruct((S // dil, dil * width), dtype)


def _view_spec(tm, dil, width):
    return pl.BlockSpec((tm // dil, dil * width), lambda i: (i, 0))


def _tile_scratch(tm, width):
    return pltpu.VMEM((width // LANES, tm, LANES), F32)


def _scratch_value(scr_ref):
    n = scr_ref.shape[0]
    return scr_ref[0] if n == 1 else jnp.concatenate([scr_ref[c] for c in range(n)], axis=1)


def _to_view(scr_ref, out_ref, dil, dtype):
    n, tm, _ = scr_ref.shape
    for r in range(dil):
        for c in range(n):
            col = (r * n + c) * LANES
            out_ref[:, col:col + LANES] = scr_ref[c, pl.ds(r, tm // dil, stride=dil), :].astype(dtype)


def _from_view(blk_ref, scr_ref, dil):
    n, tm, _ = scr_ref.shape
    for r in range(dil):
        for c in range(n):
            col = (r * n + c) * LANES
            scr_ref[c, pl.ds(r, tm // dil, stride=dil), :] = blk_ref[:, col:col + LANES].astype(F32)


def _bucket_matrix(dilation):
    rel = np.arange(KB)[None, :] - HALF - np.arange(QB)[:, None]
    band = np.abs(rel) <= HALF
    dist = np.clip(rel, -HALF, HALF) * dilation
    half_b = N_BUCKETS // 2
    max_exact = half_b // 2
    ret = np.where(dist > 0, half_b, 0)
    n = np.abs(dist)
    nf = np.maximum(n, 1).astype(np.float32)
    large = max_exact + (np.log(nf / np.float32(max_exact)) / np.float32(math.log(MAX_DISTANCE / max_exact))
                         * np.float32(half_b - max_exact)).astype(np.int32)
    large = np.minimum(large, half_b - 1)
    bucket = ret + np.where(n < max_exact, n, large)
    return np.where(band, bucket, -1).astype(np.int32)


def _ag_weights(w_in, w_out, cw_pad):
    n_arr = 3

    def body(win_ref, wout_ref, cw_ref, gin_ref, gout_ref, gcw_ref, send_sems, recv_sems):
        x, y, c = lax.axis_index("x"), lax.axis_index("y"), lax.axis_index("c")
        me = (x, y, c)
        sibling = (x, y, 1 - c)
        chips = [(1 - x, y), (x, 1 - y), (1 - x, 1 - y)]
        arrays = (gin_ref, gout_ref, gcw_ref)

        def slot(px, py, pc):
            return 4 * px + 2 * py + pc

        gin_ref[slot(*me)] = win_ref[...].astype(BF16)
        gout_ref[slot(*me)] = wout_ref[...].astype(BF16)
        gcw_ref[slot(*me)] = cw_ref[...]

        def copy(a, k, block, to):
            ref = arrays[a].at[slot(*block)]
            return pltpu.make_async_remote_copy(
                src_ref=ref, dst_ref=ref, send_sem=send_sems.at[a * 7 + k], recv_sem=recv_sems.at[a * 7 + k],
                device_id=to, device_id_type=MESH)

        first = [copy(a, 0, me, sibling) for a in range(n_arr)]
        for j, chip in enumerate(chips):
            first += [copy(a, 1 + j, me, (*chip, c)) for a in range(n_arr)]
        for cp in first:
            cp.start()
        passed = []
        for j, chip in enumerate(chips):
            for a in range(n_arr):
                copy(a, 1 + j, (*chip, c), me).wait_recv()
            for a in range(n_arr):
                cp = copy(a, 4 + j, (*chip, c), sibling)
                cp.start()
                passed.append(cp)
        for a in range(n_arr):
            copy(a, 0, sibling, me).wait_recv()
        for j, chip in enumerate(chips):
            for a in range(n_arr):
                copy(a, 4 + j, (*chip, 1 - c), me).wait_recv()
        for cp in first + passed:
            cp.wait_send()

    vm = pl.BlockSpec(memory_space=pltpu.VMEM)
    return pl.pallas_call(
        body, name="ag_weights",
        out_shape=(jax.ShapeDtypeStruct((N_DEV,) + w_in.shape, BF16),
                   jax.ShapeDtypeStruct((N_DEV,) + w_out.shape, BF16),
                   jax.ShapeDtypeStruct((N_DEV,) + cw_pad.shape, F32)),
        in_specs=[vm, vm, vm], out_specs=(vm, vm, vm),
        scratch_shapes=[pltpu.SemaphoreType.DMA((n_arr * 7,)), pltpu.SemaphoreType.DMA((n_arr * 7,))],
        compiler_params=pltpu.CompilerParams(vmem_limit_bytes=40 * MIB),
    )(w_in, w_out, cw_pad)


def _fwd_proj(x, norm_w, wg):
    S = x.shape[0]
    tm = 512

    def body(x_ref, nw_ref, wg_ref, u_ref, gb_ref, gc_ref, zc_ref, za_ref, *rest):
        q_refs, k_refs, v_refs, rq_refs, rk_refs = (rest[3 * n:3 * n + 3] for n in range(5))
        scr, rscr = rest[15:]
        xv = x_ref[...]
        r = lax.rsqrt(jnp.mean(xv * xv, axis=-1, keepdims=True) + EPS)
        h = (xv * r * nw_ref[...]).astype(BF16)
        lane = _lane((tm, LANES))
        lo = lane < HEAD_DIM
        plain = {0: u_ref, 1: gb_ref, 2: gc_ref, 3: zc_ref, 7: za_ref}
        normed = {4: (q_refs, rq_refs), 5: (k_refs, rk_refs)}

        def emit(src, refs, dtype):
            refs[0][...] = _scratch_value(src).astype(dtype)
            for dil, ref in zip(VIEW_DILATIONS, refs[1:]):
                _to_view(src, ref, dil, dtype)

        group_order = (4, 0, 5, 1, 6, 2, 3, 7)
        acc_next = jnp.dot(h, wg_ref[group_order[0]], preferred_element_type=F32)
        for n, j in enumerate(group_order):
            acc = acc_next
            if n < 7:
                acc_next = jnp.dot(h, wg_ref[group_order[n + 1]], preferred_element_type=F32)
            if j in plain:
                plain[j][...] = acc.astype(BF16)
                continue
            if j == 6:
                for p in range(N_PAIRS):
                    scr[p] = acc[:, p * LANES:(p + 1) * LANES]
                emit(scr, v_refs, BF16)
                continue
            out_refs, r_refs = normed[j]
            r_tile = jnp.zeros((tm, LANES), F32)
            for p in range(N_PAIRS):
                blk = acc[:, p * LANES:(p + 1) * LANES]
                sq = blk * blk
                s_lo = jnp.sum(jnp.where(lo, sq, 0.0), axis=-1, keepdims=True)
                s_hi = jnp.sum(jnp.where(lo, 0.0, sq), axis=-1, keepdims=True)
                r_lo = lax.rsqrt(s_lo * (1.0 / HEAD_DIM) + EPS)
                r_hi = lax.rsqrt(s_hi * (1.0 / HEAD_DIM) + EPS)
                scr[p] = blk * jnp.where(lo, r_lo, r_hi)
                r_tile = jnp.where(lane == 2 * p, r_lo, r_tile)
                r_tile = jnp.where(lane == 2 * p + 1, r_hi, r_tile)
            rscr[0] = r_tile
            emit(scr, out_refs, BF16)
            emit(rscr, r_refs, F32)

    piece = jax.ShapeDtypeStruct((S, CONV_W), BF16)
    pspec = pl.BlockSpec((tm, CONV_W), lambda i: (i, 0))
    wide = [_view_shape(S, d, ATTN_W, BF16) for d in DILATIONS]
    wide_specs = [_view_spec(tm, d, ATTN_W) for d in DILATIONS]
    stat = [_view_shape(S, d, LANES, F32) for d in DILATIONS]
    stat_specs = [_view_spec(tm, d, LANES) for d in DILATIONS]
    outs = pl.pallas_call(
        body, name="fwd_proj", grid=(S // tm,),
        out_shape=[piece] * 5 + wide * 3 + stat * 2,
        in_specs=[pl.BlockSpec((tm, D_MODEL), lambda i: (i, 0)),
                  pl.BlockSpec((1, D_MODEL), lambda i: (0, 0)),
                  pl.BlockSpec(wg.shape, lambda i: (0, 0, 0), pipeline_mode=pl.Buffered(1))],
        out_specs=[pspec] * 5 + wide_specs * 3 + stat_specs * 2,
        scratch_shapes=[_tile_scratch(tm, ATTN_W), _tile_scratch(tm, LANES)],
        compiler_params=_params(("arbitrary",), 48),
    )(x, norm_w.reshape(1, D_MODEL), wg)
    u, gb, gc, zc, za = outs[:5]
    q_hat, k_hat, v, rq, rk = (outs[5 + 3 * n:8 + 3 * n] for n in range(5))
    return u, gb, gc, zc, za, q_hat, k_hat, v, rq, rk


def _bias_tables(rel_bias, bmat_t):
    def body(rb_ref, b_ref, out_ref):
        h = pl.program_id(1)
        b = b_ref[0]
        t = jnp.full((KB, QB), NEG, F32)
        for bk in range(N_BUCKETS):
            t = jnp.where(b == bk, rb_ref[bk, h], t)
        out_ref[0, 0] = t

    return pl.pallas_call(
        body, name="bias_tables", grid=(len(DILATIONS), N_HEADS),
        out_shape=jax.ShapeDtypeStruct((len(DILATIONS), N_HEADS, KB, QB), F32),
        in_specs=[pl.BlockSpec(memory_space=pltpu.SMEM),
                  pl.BlockSpec((1, KB, QB), lambda g, h: (g, 0, 0))],
        out_specs=pl.BlockSpec((1, 1, KB, QB), lambda g, h: (g, h, 0, 0)),
        compiler_params=_params(("arbitrary", "arbitrary"), 16),
    )(rel_bias, bmat_t)


def _bias_grad(ds_parts, bmat_t):
    present = [sorted(set(_bucket_matrix(d).ravel().tolist()) - {-1}) for d in DILATIONS]

    def body(ds1_ref, ds2_ref, ds3_ref, b_ref, out_ref):
        ds_refs = (ds1_ref, ds2_ref, ds3_ref)
        lane = _lane((1, LANES))
        row = lax.broadcasted_iota(jnp.int32, (8, LANES), 0)
        out = jnp.zeros((8, LANES), F32)
        for e in range(2):
            vec = jnp.zeros((1, LANES), F32)
            for g in range(len(DILATIONS)):
                b = b_ref[g]
                ds = ds_refs[g][0, :, e * QB:(e + 1) * QB]
                for bk in present[g]:
                    s = jnp.sum(jnp.where(b == bk, ds, 0.0), axis=-1, keepdims=True)
                    s = jnp.sum(s, axis=0, keepdims=True)
                    vec = vec + jnp.where(lane == bk, s, 0.0)
            out = jnp.where(row == e, vec, out)
        out_ref[0] = out

    return pl.pallas_call(
        body, name="bias_grad", grid=(N_PAIRS,),
        out_shape=jax.ShapeDtypeStruct((N_PAIRS, 8, LANES), F32),
        in_specs=[pl.BlockSpec((1, KB, 2 * QB), lambda p: (p, 0, 0))] * len(DILATIONS)
        + [pl.BlockSpec((len(DILATIONS), KB, QB), lambda p: (0, 0, 0))],
        out_specs=pl.BlockSpec((1, 8, LANES), lambda p: (p, 0, 0)),
        compiler_params=_params(("arbitrary",), 16),
    )(*ds_parts, bmat_t)


def _halo_specs(width, rows, T, L, cols_of):
    per = T // rows
    last = L // rows - 1
    prev = pl.BlockSpec((rows, width), lambda r, i: (jnp.maximum(i * per - 1, 0), cols_of(r)))
    nxt = pl.BlockSpec((rows, width), lambda r, i: (jnp.minimum((i + 1) * per, last), cols_of(r)))
    return prev, nxt


def _attn_fwd(g, dil, qv, kv, vv, gain, tables):
    L = qv.shape[0]
    T = min(1024, L)
    n_sub = T // QB

    n_i = L // T
    n_steps = dil * n_i
    N_SLOTS = 3

    def body(q_hbm, k_hbm, v_hbm, gain_ref, bias_ref, o_ref, lse_ref, qbuf, kwins, vwins, s_scr, p_scr, sems):
        i = pl.program_id(1)
        step = pl.program_id(0) * n_i + i
        slot = step % N_SLOTS

        def transfers(st, act):
            r_, i_ = st // n_i, st % n_i
            sl = st % N_SLOTS
            col = pl.ds(pl.multiple_of(r_ * ATTN_W, ATTN_W), ATTN_W)
            row0 = i_ * T
            act(pltpu.make_async_copy(q_hbm.at[pl.ds(row0, T), col], qbuf.at[sl], sems.at[sl, 0]))
            for a, (src, win) in enumerate(((k_hbm, kwins), (v_hbm, vwins))):
                act(pltpu.make_async_copy(src.at[pl.ds(row0, T), col], win.at[sl, pl.ds(HALF, T)], sems.at[sl, 1 + 3 * a]))

                @pl.when(i_ > 0)
                def _(src=src, win=win, a=a):
                    act(pltpu.make_async_copy(src.at[pl.ds(row0 - HALF, HALF), col], win.at[sl, pl.ds(0, HALF)],
                                              sems.at[sl, 2 + 3 * a]))

                @pl.when(i_ < n_i - 1)
                def _(src=src, win=win, a=a):
                    act(pltpu.make_async_copy(src.at[pl.ds(row0 + T, HALF), col], win.at[sl, pl.ds(HALF + T, HALF)],
                                              sems.at[sl, 3 + 3 * a]))

        @pl.when(step == 0)
        def _():
            for st in range(min(N_SLOTS - 1, n_steps)):
                transfers(st, lambda cp: cp.start())

        @pl.when(step + N_SLOTS - 1 < n_steps)
        def _():
            transfers(step + N_SLOTS - 1, lambda cp: cp.start())

        transfers(step, lambda cp: cp.wait())
        zero_halo = jnp.zeros((HALF, ATTN_W), BF16)

        @pl.when(i == 0)
        def _():
            kwins[slot, 0:HALF] = zero_halo
            vwins[slot, 0:HALF] = zero_halo

        @pl.when(i == n_i - 1)
        def _():
            kwins[slot, HALF + T:] = zero_halo
            vwins[slot, HALF + T:] = zero_halo

        q_ref, kwin, vwin = qbuf.at[slot], kwins.at[slot], vwins.at[slot]
        lo = _lane((QB, LANES)) < HEAD_DIM
        krow = lax.broadcasted_iota(jnp.int32, (KB, 1), 0)
        chan_lo = lax.broadcasted_iota(jnp.int32, (LANES, 1), 0) < HEAD_DIM
        hrow = lax.broadcasted_iota(jnp.int32, (LANES, QB), 0)

        def sub(j, carry):
            r0 = pl.multiple_of(j * QB, QB)
            kpos = i * T + j * QB - HALF + krow
            kvalid = (kpos >= 0) & (kpos < L)
            lse_rows = jnp.zeros((LANES, QB), F32)
            cols = [slice(p * LANES, (p + 1) * LANES) for p in range(N_PAIRS)]
            for p, cs in enumerate(cols):
                qf = q_ref[pl.ds(r0, QB), cs].astype(F32) * gain_ref[:, cs]
                q2 = jnp.concatenate([jnp.where(lo, qf, 0.0), jnp.where(lo, 0.0, qf)], axis=0).astype(BF16)
                s_scr[p] = lax.dot_general(kwin[pl.ds(r0, KB), cs], q2, (((1,), (1,)), ((), ())),
                                           preferred_element_type=F32)
            inv_l = []
            for p in range(N_PAIRS):
                bias2 = jnp.concatenate([bias_ref[0, 2 * p], bias_ref[0, 2 * p + 1]], axis=1)
                logits = jnp.where(kvalid, s_scr[p] + bias2, NEG)
                m = jnp.max(logits, axis=0, keepdims=True)
                pt = jnp.exp(logits - m)
                l = jnp.sum(pt, axis=0, keepdims=True)
                p_scr[p] = pt.astype(BF16)
                inv_l.append(1.0 / l)
                lse2 = m + jnp.log(l)
                lse_rows = jnp.where(hrow == 2 * p, lse2[:, 0:QB], lse_rows)
                lse_rows = jnp.where(hrow == 2 * p + 1, lse2[:, QB:2 * QB], lse_rows)
            for p, cs in enumerate(cols):
                s_scr[p, 0:LANES, :] = lax.dot_general(vwin[pl.ds(r0, KB), cs], p_scr[p], (((0,), (0,)), ((), ())),
                                                       preferred_element_type=F32) * inv_l[p]
            for p, cs in enumerate(cols):
                ot2 = s_scr[p, 0:LANES, :]
                ot = jnp.where(chan_lo, ot2[:, 0:QB], ot2[:, QB:2 * QB])
                o_ref[pl.ds(r0, QB), cs] = ot.T.astype(BF16)
            lse_ref[pl.ds(r0, QB), :] = lse_rows.T
            return carry

        lax.fori_loop(0, n_sub, sub, 0)

    main = pl.BlockSpec((T, ATTN_W), lambda r, i: (i, r))
    hbm = pl.BlockSpec(memory_space=pl.ANY)
    o_g, lse_g = pl.pallas_call(
        body, name=f"attn_fwd_d{dil}", grid=(dil, n_i),
        out_shape=(jax.ShapeDtypeStruct((L, dil * ATTN_W), BF16), jax.ShapeDtypeStruct((L, dil * LANES), F32)),
        in_specs=[hbm, hbm, hbm,
                  pl.BlockSpec((1, ATTN_W), lambda r, i: (0, 0)),
                  pl.BlockSpec((1, N_HEADS, KB, QB), lambda r, i: (g, 0, 0, 0))],
        out_specs=(main, pl.BlockSpec((T, LANES), lambda r, i: (i, r))),
        scratch_shapes=[pltpu.VMEM((N_SLOTS, T, ATTN_W), BF16),
                        pltpu.VMEM((N_SLOTS, T + 2 * HALF, ATTN_W), BF16),
                        pltpu.VMEM((N_SLOTS, T + 2 * HALF, ATTN_W), BF16),
                        pltpu.VMEM((N_PAIRS, KB, 2 * QB), F32), pltpu.VMEM((N_PAIRS, KB, 2 * QB), BF16),
                        pltpu.SemaphoreType.DMA((N_SLOTS, 7))],
        compiler_params=_params(("arbitrary", "arbitrary"), 40),
    )(qv, kv, vv, gain, tables)
    return o_g, lse_g


HALO = 16
CHUNK = 128


def _row_halo_specs(width, tm, S, col=0):
    rows = HALO
    per = tm // rows
    last = S // rows - 1
    prev = pl.BlockSpec((rows, width), lambda i: (jnp.maximum(i * per - 1, 0), col))
    nxt = pl.BlockSpec((rows, width), lambda i: (jnp.minimum((i + 1) * per, last), col))
    return prev, nxt


def _shift_rows(a, prev_row, next_row, tm):
    row = lax.broadcasted_iota(jnp.int32, a.shape, 0)
    a_m1 = jnp.where(row == 0, prev_row, pltpu.roll(a, 1, 0))
    a_p1 = jnp.where(row == tm - 1, next_row, pltpu.roll(a, tm - 1, 0))
    return a_m1, a_p1


def _combine_gates(o_parts, lse_parts, u, gb, gc, zc, za, conv_w, conv_b):
    S = u.shape[0]
    tm = 512
    n_t = S // tm

    def body(o1, o2, o3, l1, l2, l3, u_ref, up_ref, un_ref, gb_ref, gc_ref, gcp_ref, gcn_ref, zc_ref, za_ref,
             cw_ref, cb_ref, y_ref, o_ref, lse_ref, lse2_ref, lse3_ref, so2, so3, sl2, sl3, slse):
        i = pl.program_id(0)
        for blk, scr, dil in ((o2, so2, DILATIONS[1]), (o3, so3, DILATIONS[2]),
                              (l2, sl2, DILATIONS[1]), (l3, sl3, DILATIONS[2])):
            _from_view(blk, scr, dil)
        ls = [l1[...], sl2[0], sl3[0]]
        lmax = jnp.maximum(jnp.maximum(ls[0], ls[1]), ls[2])
        es = [jnp.exp(l - lmax) for l in ls]
        den = es[0] + es[1] + es[2]
        slse[0] = lmax + jnp.log(den)
        lse_ref[...] = slse[0]
        _to_view(slse, lse2_ref, DILATIONS[1], F32)
        _to_view(slse, lse3_ref, DILATIONS[2], F32)
        inv = 1.0 / den
        ws = [e * inv for e in es]
        for p in range(N_PAIRS):
            cs = slice(p * LANES, (p + 1) * LANES)
            acc = jnp.zeros((tm, LANES), F32)
            spread = _head_to_lanes(p)
            for w, o_g in zip(ws, (o1[:, cs].astype(F32), so2[p], so3[p])):
                acc = acc + _split_dot(w, spread) * o_g
            o_ref[:, cs] = acc.astype(BF16)
            za = za_ref[:, cs].astype(F32)
            y_ref[:, CONV_W + p * LANES:CONV_W + (p + 1) * LANES] = (acc * za * _sigmoid(za)).astype(BF16)
        a = gc_ref[...].astype(F32) * u_ref[...].astype(F32)
        a_prev = gcp_ref[15:16, :].astype(F32) * up_ref[15:16, :].astype(F32) * (i > 0).astype(F32)
        a_next = gcn_ref[0:1, :].astype(F32) * un_ref[0:1, :].astype(F32) * (i < n_t - 1).astype(F32)
        a_m1, a_p1 = _shift_rows(a, a_prev, a_next, tm)
        z3 = cw_ref[0:1, :] * a_m1 + cw_ref[1:2, :] * a + cw_ref[2:3, :] * a_p1
        zc = zc_ref[...].astype(F32)
        y_ref[:, 0:CONV_W] = (gb_ref[...].astype(F32) * (z3 + cb_ref[...]) * zc * _sigmoid(zc)).astype(BF16)

    pspec = pl.BlockSpec((tm, CONV_W), lambda i: (i, 0))
    hp, hn = _row_halo_specs(CONV_W, tm, S)
    wide_specs = [_view_spec(tm, d, ATTN_W) for d in DILATIONS]
    stat_specs = [_view_spec(tm, d, LANES) for d in DILATIONS]
    outs = pl.pallas_call(
        body, name="combine_gates", grid=(n_t,),
        out_shape=[jax.ShapeDtypeStruct((S, D_MODEL), BF16), jax.ShapeDtypeStruct((S, ATTN_W), BF16)]
        + [_view_shape(S, d, LANES, F32) for d in DILATIONS],
        in_specs=wide_specs + stat_specs + [pspec, hp, hn, pspec, pspec, hp, hn, pspec, pspec,
                                            pl.BlockSpec((3, CONV_W), lambda i: (0, 0)),
                                            pl.BlockSpec((1, CONV_W), lambda i: (0, 0))],
        out_specs=[pl.BlockSpec((tm, D_MODEL), lambda i: (i, 0)), pspec] + stat_specs,
        scratch_shapes=[_tile_scratch(tm, ATTN_W), _tile_scratch(tm, ATTN_W),
                        _tile_scratch(tm, LANES), _tile_scratch(tm, LANES), _tile_scratch(tm, LANES)],
        compiler_params=_params(("arbitrary",), 40),
    )(*o_parts, *lse_parts, u, u, u, gb, gc, gc, gc, zc, za, conv_w, conv_b.reshape(1, CONV_W))
    return outs[0], outs[1], outs[2:]


def _out_proj(y, x, target, wo):
    S = x.shape[0]
    tm = 512

    def body(y_ref, x_ref, t_ref, wo_ref, g_ref, dy_ref, dwo_ref, loss_ref):
        i = pl.program_id(0)

        @pl.when(i == 0)
        def _():
            dwo_ref[...] = jnp.zeros_like(dwo_ref)
            loss_ref[...] = jnp.zeros_like(loss_ref)

        yv = y_ref[...]
        wo_v = wo_ref[...]
        err = x_ref[...] + jnp.dot(yv, wo_v, preferred_element_type=F32) - t_ref[...]
        e2 = (err * err).reshape(tm // 8, 8, D_MODEL).sum(axis=0)
        part = e2[:, 0:LANES]
        for k in range(1, D_MODEL // LANES):
            part = part + e2[:, k * LANES:(k + 1) * LANES]
        loss_ref[...] += part
        gv = (err * (1.0 / D_MODEL)).astype(BF16)
        g_ref[...] = gv
        dy_ref[...] = lax.dot_general(gv, wo_v, (((1,), (1,)), ((), ())), preferred_element_type=F32).astype(BF16)
        dwo_ref[...] += lax.dot_general(yv, gv, (((0,), (0,)), ((), ())), preferred_element_type=F32)

    tile = pl.BlockSpec((tm, D_MODEL), lambda i: (i, 0))
    return pl.pallas_call(
        body, name="out_proj", grid=(S // tm,),
        out_shape=(jax.ShapeDtypeStruct((S, D_MODEL), BF16), jax.ShapeDtypeStruct((S, D_MODEL), BF16),
                   jax.ShapeDtypeStruct((D_MODEL, D_MODEL), F32), jax.ShapeDtypeStruct((8, LANES), F32)),
        in_specs=[tile, tile, tile, pl.BlockSpec((D_MODEL, D_MODEL), lambda i: (0, 0), pipeline_mode=pl.Buffered(1))],
        out_specs=(tile, tile, pl.BlockSpec((D_MODEL, D_MODEL), lambda i: (0, 0)),
                   pl.BlockSpec((8, LANES), lambda i: (0, 0))),
        compiler_params=_params(("arbitrary",), 48),
    )(y, x, target, wo)


def _gates_bwd(dy, u, gb, gc, zc, za, o, conv_w, conv_b):
    S = u.shape[0]
    tm = 512
    n_t = S // tm

    def body(dy_ref, dyp_ref, dyn_ref, u_ref, up_ref, un_ref, gb_ref, gbp_ref, gbn_ref, gc_ref, gcp_ref, gcn_ref,
             zc_ref, zcp_ref, zcn_ref, za_ref, o_ref, cw_ref, cb_ref,
             d5_ref, small_ref,
             do_ref, do2_ref, do3_ref, delta_ref, delta2_ref, delta3_ref, sdo, sdelta,
             u_ext, gc_ext, gb_ext, zc_ext, dy_ext):
        i = pl.program_id(0)

        @pl.when(i == 0)
        def _():
            small_ref[...] = jnp.zeros_like(small_ref)

        exts = ((u_ext, up_ref, u_ref, un_ref), (gc_ext, gcp_ref, gc_ref, gcn_ref), (gb_ext, gbp_ref, gb_ref, gbn_ref),
                (zc_ext, zcp_ref, zc_ref, zcn_ref))
        for ext, pr, mn, nx in exts:
            ext[0:HALO] = pr[...]
            ext[HALO:HALO + tm] = mn[...]
            ext[HALO + tm:] = nx[...]
        dy_ext[0:HALO] = dyp_ref[...]
        dy_ext[HALO:HALO + tm] = dy_ref[:, 0:CONV_W]
        dy_ext[HALO + tm:] = dyn_ref[...]
        zero_halo = jnp.zeros((HALO, CONV_W), BF16)

        @pl.when(i == 0)
        def _():
            u_ext[0:HALO] = zero_halo
            dy_ext[0:HALO] = zero_halo

        @pl.when(i == n_t - 1)
        def _():
            u_ext[HALO + tm:] = zero_halo
            dy_ext[HALO + tm:] = zero_halo

        wide_rows = CHUNK + 2 * HALO
        mid = slice(HALO, HALO + CHUNK)
        csum = lambda t: jnp.sum(t, axis=0, keepdims=True)

        def one_group(p, c0, sums):
            cs = slice(p * LANES, (p + 1) * LANES)
            w0, w1, w2, cb = cw_ref[0:1, cs], cw_ref[1:2, cs], cw_ref[2:3, cs], cb_ref[:, cs]
            rows = pl.ds(c0, CHUNK)
            wide = pl.ds(c0, wide_rows)
            u_e, gc_e, gb_e, zc_e, dy_e = (r[wide, cs].astype(F32) for r in (u_ext, gc_ext, gb_ext, zc_ext, dy_ext))
            a_e = gc_e * u_e
            sz_e = _sigmoid(zc_e)
            dz3_e = dy_e * zc_e * sz_e * gb_e
            a, a_m1, a_p1 = a_e[mid], pltpu.roll(a_e, 1, 0)[mid], pltpu.roll(a_e, wide_rows - 1, 0)[mid]
            dz3, dz3_m1, dz3_p1 = dz3_e[mid], pltpu.roll(dz3_e, 1, 0)[mid], pltpu.roll(dz3_e, wide_rows - 1, 0)[mid]
            uv, gcv, gbv, zcv, dyc, sz = u_e[mid], gc_e[mid], gb_e[mid], zc_e[mid], dy_e[mid], sz_e[mid]
            cvp = w0 * a_m1 + w1 * a + w2 * a_p1 + cb
            dcv = dyc * zcv * sz
            d5_ref[3, rows, cs] =(dyc * gbv * cvp * (sz * (1.0 + zcv * (1.0 - sz)))).astype(BF16)
            d5_ref[1, rows, cs] =(dcv * cvp).astype(BF16)
            da = w0 * dz3_p1 + w1 * dz3 + w2 * dz3_m1
            d5_ref[0, rows, cs] =(da * gcv).astype(BF16)
            d5_ref[2, rows, cs] =(da * uv).astype(BF16)
            sums = (sums[0] + csum(dz3 * a_m1), sums[1] + csum(dz3 * a), sums[2] + csum(dz3 * a_p1),
                    sums[3] + csum(dz3))

            dya = dy_ref[rows, CONV_W + p * LANES:CONV_W + (p + 1) * LANES].astype(F32)
            zav = za_ref[rows, cs].astype(F32)
            ov = o_ref[rows, cs].astype(F32)
            sa = _sigmoid(zav)
            d_o = dya * zav * sa
            sdo[p, rows, :] = d_o
            do_ref[rows, cs] = d_o.astype(BF16)
            d5_ref[4, rows, cs] =(dya * ov * (sa * (1.0 + zav * (1.0 - sa)))).astype(BF16)
            return sums, _split_dot(d_o * ov, _lanes_to_head(p, 1.0))

        def chunk(ci, carry):
            c0 = pl.multiple_of(ci * CHUNK, CHUNK)
            out, delta = [], jnp.zeros((CHUNK, LANES), F32)
            for p in range(N_PAIRS):
                sums, part = one_group(p, c0, carry[p])
                out.append(sums)
                delta = delta + part
            sdelta[0, pl.ds(c0, CHUNK), :] = delta
            return tuple(out)

        z = jnp.zeros((1, LANES), F32)
        all_sums = lax.fori_loop(0, tm // CHUNK, chunk, ((z, z, z, z),) * N_PAIRS)
        for p in range(N_PAIRS):
            for k in range(4):
                small_ref[k:k + 1, p * LANES:(p + 1) * LANES] += all_sums[p][k]
        delta_ref[...] = sdelta[0]
        for dil, wide_ref, stat_ref in zip(VIEW_DILATIONS, (do2_ref, do3_ref), (delta2_ref, delta3_ref)):
            _to_view(sdo, wide_ref, dil, BF16)
            _to_view(sdelta, stat_ref, dil, F32)

    pspec = pl.BlockSpec((tm, CONV_W), lambda i: (i, 0))
    hp, hn = _row_halo_specs(CONV_W, tm, S)
    piece = jax.ShapeDtypeStruct((S, CONV_W), BF16)
    outs = pl.pallas_call(
        body, name="gates_bwd", grid=(n_t,),
        out_shape=[jax.ShapeDtypeStruct((5, S, CONV_W), BF16), jax.ShapeDtypeStruct((8, CONV_W), F32)]
        + [_view_shape(S, d, ATTN_W, BF16) for d in DILATIONS] + [_view_shape(S, d, LANES, F32) for d in DILATIONS],
        in_specs=[pl.BlockSpec((tm, D_MODEL), lambda i: (i, 0)), hp, hn,
                  pspec, hp, hn, pspec, hp, hn, pspec, hp, hn, pspec, hp, hn, pspec, pspec,
                  pl.BlockSpec((3, CONV_W), lambda i: (0, 0)), pl.BlockSpec((1, CONV_W), lambda i: (0, 0))],
        out_specs=[pl.BlockSpec((5, tm, CONV_W), lambda i: (0, i, 0)), pl.BlockSpec((8, CONV_W), lambda i: (0, 0))]
        + [_view_spec(tm, d, ATTN_W) for d in DILATIONS] + [_view_spec(tm, d, LANES) for d in DILATIONS],
        scratch_shapes=[_tile_scratch(tm, ATTN_W), _tile_scratch(tm, LANES)]
        + [pltpu.VMEM((tm + 2 * HALO, CONV_W), BF16)] * 5,
        compiler_params=_params(("arbitrary",), 40),
    )(dy, dy, dy, u, u, u, gb, gb, gb, gc, gc, gc, zc, zc, zc, za, o, conv_w, conv_b.reshape(1, CONV_W))
    return outs[0], outs[1], outs[2:5], outs[5:8]


def _attn_bwd(g, dil, qv, kv, vv, dov, lv, dv_, rqv, rkv, gain, tables):
    L = qv.shape[0]
    T = min(1024, L)
    n_sub = T // QB + 1
    QW = T + 2 * HALF
    KW = T + 2 * QB

    def body(qm_ref, qp_ref, qn_ref, dom_ref, dop_ref, don_ref, km_ref, kp_ref, kn_ref, vm_ref, vp_ref, vn_ref,
             lm_ref, lp_ref, ln_ref, dm_ref, dp_ref, dn_ref, rqm_ref, rqp_ref, rqn_ref, rk_ref,
             gain_ref, bias_ref,
             dq_ref, dk_ref, dv_ref, ds_ref, a_ref,
             qwin, dowin, kwin, vwin, lwin, dwin, rqwin, dqwin, dkacc, dvacc,
             s_scr, d_scr, p_scr, ds_scr, q2_scr, do2_scr, dq_scr):
        r = pl.program_id(0)
        i = pl.program_id(1)

        @pl.when((r == 0) & (i == 0))
        def _():
            ds_ref[...] = jnp.zeros_like(ds_ref)
            a_ref[...] = jnp.zeros_like(a_ref)

        for win, (pr, mn, nx), h_rows in ((qwin, (qp_ref, qm_ref, qn_ref), HALF),
                                          (dowin, (dop_ref, dom_ref, don_ref), HALF),
                                          (lwin, (lp_ref, lm_ref, ln_ref), HALF),
                                          (dwin, (dp_ref, dm_ref, dn_ref), HALF),
                                          (rqwin, (rqp_ref, rqm_ref, rqn_ref), HALF),
                                          (kwin, (kp_ref, km_ref, kn_ref), QB),
                                          (vwin, (vp_ref, vm_ref, vn_ref), QB)):
            win[0:h_rows] = pr[...]
            win[h_rows:h_rows + T] = mn[...]
            win[h_rows + T:] = nx[...]
        dkacc[0:QB] = jnp.zeros((QB, ATTN_W), F32)
        dvacc[0:QB] = jnp.zeros((QB, ATTN_W), F32)

        lo = _lane((QB, LANES)) < HEAD_DIM
        same_head = _same_head()
        krow = lax.broadcasted_iota(jnp.int32, (KB, 1), 0)
        qcol = _lane((1, 2 * QB)) % QB
        row = lax.broadcasted_iota(jnp.int32, (QB, 1), 0)
        chan_lo = lax.broadcasted_iota(jnp.int32, (LANES, 1), 0) < HEAD_DIM

        cols = [slice(p * LANES, (p + 1) * LANES) for p in range(N_PAIRS)]
        nt = (((1,), (1,)), ((), ()))

        def scores(j):
            r0 = pl.multiple_of(j * QB, QB)
            for k, cs in enumerate(cols):
                qf = qwin[pl.ds(r0, QB), cs].astype(F32) * gain_ref[:, cs]
                q2_scr[k] = jnp.concatenate([jnp.where(lo, qf, 0.0), jnp.where(lo, 0.0, qf)], axis=0).astype(BF16)
                dov = dowin[pl.ds(r0, QB), cs]
                zero = jnp.zeros_like(dov)
                do2_scr[k] = jnp.concatenate([jnp.where(lo, dov, zero), jnp.where(lo, zero, dov)], axis=0)
                s_scr[k] = lax.dot_general(kwin[pl.ds(r0, KB), cs], q2_scr[k], nt, preferred_element_type=F32)
                d_scr[k] = lax.dot_general(vwin[pl.ds(r0, KB), cs], do2_scr[k], nt, preferred_element_type=F32)

        def sub(j):
            r0 = pl.multiple_of(j * QB, QB)
            kpos = i * T - QB + j * QB + krow
            qpos = i * T - HALF + j * QB + qcol
            valid = (kpos >= 0) & (kpos < L) & (qpos >= 0) & (qpos < L)
            owned_t = ((qpos >= i * T) & (qpos < i * T + T)).astype(F32)
            qpos_c = i * T - HALF + j * QB + row
            owned = ((qpos_c >= i * T) & (qpos_c < i * T + T)).astype(F32)
            lse_rows = lwin[pl.ds(r0, QB), :].T
            delta_rows = dwin[pl.ds(r0, QB), :].T
            rq_t = rqwin[pl.ds(r0, QB), :]
            for p in range(N_PAIRS):
                k = p
                bias2 = jnp.concatenate([bias_ref[0, 2 * p], bias_ref[0, 2 * p + 1]], axis=1)
                lse2 = jnp.concatenate([lse_rows[2 * p:2 * p + 1, :], lse_rows[2 * p + 1:2 * p + 2, :]], axis=1)
                delta2 = jnp.concatenate([delta_rows[2 * p:2 * p + 1, :], delta_rows[2 * p + 1:2 * p + 2, :]], axis=1)
                pt = jnp.where(valid, jnp.exp(s_scr[k] + bias2 - lse2), 0.0)
                dst = pt * (d_scr[k] - delta2)
                ds_ref[p] += dst * owned_t
                p_scr[p] = pt.astype(BF16)
                ds_scr[p] = dst.astype(BF16)
            for p, cs in enumerate(cols):
                k = p
                for acc_ref, lhs, rhs in ((dvacc, p_scr[p], do2_scr[k]), (dkacc, ds_scr[p], q2_scr[k])):
                    part = jnp.dot(lhs, rhs, preferred_element_type=F32)
                    acc_ref[pl.ds(r0, QB), cs] += part[0:QB]
                    acc_ref[pl.ds(r0 + QB, QB), cs] = part[QB:KB]
                dq_scr[p] = lax.dot_general(kwin[pl.ds(r0, KB), cs], ds_scr[p], (((0,), (0,)), ((), ())),
                                            preferred_element_type=F32)
            for p, cs in enumerate(cols):
                g2 = gain_ref[:, cs]
                q_f = qwin[pl.ds(r0, QB), cs].astype(F32)
                dqt2 = dq_scr[p]
                dqm2 = jnp.where(chan_lo, dqt2[:, 0:QB], dqt2[:, QB:2 * QB]).T
                a_ref[p] += (dqm2 * q_f * owned).reshape(QB // 8, 8, LANES).sum(axis=0)
                dqh = dqm2 * g2
                mean_c = _split_dot(dqh * q_f, same_head) * (1.0 / HEAD_DIM)
                rq2 = _split_dot(rq_t, _head_to_lanes(p))
                dqwin[pl.ds(r0, QB), cs] = (rq2 * (dqh - q_f * mean_c)).astype(BF16)

        def one_sub(j, carry):
            scores(j)
            sub(j)
            return carry

        lax.fori_loop(0, n_sub, one_sub, 0)
        dq_ref[...] = dqwin[HALF:HALF + T]

        def finish(j, carry):
            r0 = pl.multiple_of(j * QB, QB)
            rk_t = rk_ref[pl.ds(r0, QB), :]
            for p in range(N_PAIRS):
                cs = slice(p * LANES, (p + 1) * LANES)
                dkh = dkacc[pl.ds(QB + r0, QB), cs]
                k_f = km_ref[pl.ds(r0, QB), cs].astype(F32)
                mean_c = _split_dot(dkh * k_f, same_head) * (1.0 / HEAD_DIM)
                rk2 = _split_dot(rk_t, _head_to_lanes(p))
                dk_ref[pl.ds(r0, QB), cs] = (rk2 * (dkh - k_f * mean_c)).astype(BF16)
                dv_ref[pl.ds(r0, QB), cs] = dvacc[pl.ds(QB + r0, QB), cs].astype(BF16)
            return carry

        lax.fori_loop(0, T // QB, finish, 0)

    main = pl.BlockSpec((T, ATTN_W), lambda r, i: (i, r))
    smain = pl.BlockSpec((T, LANES), lambda r, i: (i, r))
    q_prev, q_next = _halo_specs(ATTN_W, HALF, T, L, lambda r: r)
    k_prev, k_next = _halo_specs(ATTN_W, QB, T, L, lambda r: r)
    s_prev, s_next = _halo_specs(LANES, HALF, T, L, lambda r: r)
    piece = jax.ShapeDtypeStruct((L, dil * ATTN_W), BF16)
    return pl.pallas_call(
        body, name=f"attn_bwd_d{dil}", grid=(dil, L // T),
        out_shape=(piece, piece, piece, jax.ShapeDtypeStruct((N_PAIRS, KB, 2 * QB), F32),
                   jax.ShapeDtypeStruct((N_PAIRS, 8, LANES), F32)),
        in_specs=[main, q_prev, q_next, main, q_prev, q_next, main, k_prev, k_next, main, k_prev, k_next,
                  smain, s_prev, s_next, smain, s_prev, s_next, smain, s_prev, s_next, smain,
                  pl.BlockSpec((1, ATTN_W), lambda r, i: (0, 0)),
                  pl.BlockSpec((1, N_HEADS, KB, QB), lambda r, i: (g, 0, 0, 0))],
        out_specs=(main, main, main, pl.BlockSpec((N_PAIRS, KB, 2 * QB), lambda r, i: (0, 0, 0)),
                   pl.BlockSpec((N_PAIRS, 8, LANES), lambda r, i: (0, 0, 0))),
        scratch_shapes=[pltpu.VMEM((QW, ATTN_W), BF16), pltpu.VMEM((QW, ATTN_W), BF16),
                        pltpu.VMEM((KW, ATTN_W), BF16), pltpu.VMEM((KW, ATTN_W), BF16),
                        pltpu.VMEM((QW, LANES), F32), pltpu.VMEM((QW, LANES), F32), pltpu.VMEM((QW, LANES), F32),
                        pltpu.VMEM((QW, ATTN_W), BF16),
                        pltpu.VMEM((KW, ATTN_W), F32), pltpu.VMEM((KW, ATTN_W), F32),
                        pltpu.VMEM((N_PAIRS, KB, 2 * QB), F32), pltpu.VMEM((N_PAIRS, KB, 2 * QB), F32),
                        pltpu.VMEM((N_PAIRS, KB, 2 * QB), BF16), pltpu.VMEM((N_PAIRS, KB, 2 * QB), BF16),
                        pltpu.VMEM((N_PAIRS, 2 * QB, LANES), BF16), pltpu.VMEM((N_PAIRS, 2 * QB, LANES), BF16),
                        pltpu.VMEM((N_PAIRS, LANES, 2 * QB), F32)],
        compiler_params=_params(("arbitrary", "arbitrary"), 56),
    )(qv, qv, qv, dov, dov, dov, kv, kv, kv, vv, vv, vv, lv, lv, lv, dv_, dv_, dv_, rqv, rqv, rqv, rkv, gain, tables)


D5_GROUPS = (0, 1, 2, 3, 7)
QKV_GROUPS = (4, 5, 6)


def _proj_bwd(x, d_out, norm_w, wg, d5, qkv_shares):
    S = x.shape[0]
    tm = 512
    flat = [share for p in qkv_shares for share in p]
    n_t = S // tm

    def body(*refs):
        x_ref, g_ref, nw_ref, wg_ref, d5_ref = refs[:5]
        piece_refs = refs[5:5 + len(flat)]
        gx_ref, dnw_ref, ht_ref, dqkv_ref, sview = refs[5 + len(flat):]
        i = pl.program_id(0)

        @pl.when(i == 0)
        def _():
            dnw_ref[...] = jnp.zeros_like(dnw_ref)

        xv = x_ref[...]
        r = lax.rsqrt(jnp.mean(xv * xv, axis=-1, keepdims=True) + EPS)
        xh = xv * r
        nw = nw_ref[...]
        ht_ref[...] = (xh * nw).T.astype(BF16)
        dh = jnp.zeros((tm, D_MODEL), F32)
        k = 0
        for j in range(8):
            if j in D5_GROUPS:
                dp = d5_ref[D5_GROUPS.index(j)]
            else:
                shares = qkv_shares[QKV_GROUPS.index(j)]
                dp = jnp.zeros((tm, CONV_W), F32)
                for t, (_, dil) in enumerate(shares):
                    if dil == 1:
                        dp = dp + piece_refs[k + t][...].astype(F32)
                    else:
                        _from_view(piece_refs[k + t], sview, dil)
                        dp = dp + _scratch_value(sview)
                dp = dp.astype(BF16)
                k += len(shares)
                dqkv_ref[QKV_GROUPS.index(j)] = dp
            dh = dh + lax.dot_general(dp, wg_ref[j], (((1,), (1,)), ((), ())), preferred_element_type=F32)
        dnw_ref[...] += (dh * xh).reshape(tm // 8, 8, D_MODEL).sum(axis=0)
        dxh = dh * nw
        mean_c = jnp.mean(dxh * xh, axis=-1, keepdims=True)
        gx_ref[...] = g_ref[...].astype(F32) + r * (dxh - xh * mean_c)

    tile = pl.BlockSpec((tm, D_MODEL), lambda i: (i, 0))
    return pl.pallas_call(
        body, name="proj_bwd", grid=(n_t,),
        out_shape=(jax.ShapeDtypeStruct((S, D_MODEL), F32), jax.ShapeDtypeStruct((8, D_MODEL), F32),
                   jax.ShapeDtypeStruct((D_MODEL, S), BF16), jax.ShapeDtypeStruct((3, S, CONV_W), BF16)),
        in_specs=[tile, tile, pl.BlockSpec((1, D_MODEL), lambda i: (0, 0)),
                  pl.BlockSpec(wg.shape, lambda i: (0, 0, 0), pipeline_mode=pl.Buffered(1)),
                  pl.BlockSpec((5, tm, CONV_W), lambda i: (0, i, 0))]
        + [_view_spec(tm, dil, CONV_W) for _, dil in flat],
        out_specs=(tile, pl.BlockSpec((8, D_MODEL), lambda i: (0, 0)),
                   pl.BlockSpec((D_MODEL, tm), lambda i: (0, i)), pl.BlockSpec((3, tm, CONV_W), lambda i: (0, i, 0))),
        scratch_shapes=[_tile_scratch(tm, CONV_W)],
        compiler_params=_params(("arbitrary",), 56),
    )(x, d_out, norm_w.reshape(1, D_MODEL), wg, d5, *[a for a, _ in flat])


def _dw_exchange(ht, d5, dqkv, gw_out, small):
    D, S = ht.shape
    tk = 2048
    n_t = S // tk
    me_outer = 4 * lax.axis_index("x") + 2 * lax.axis_index("y") + lax.axis_index("c")
    order = ((me_outer + 1 + jnp.arange(N_DEV, dtype=jnp.int32)) % N_DEV).astype(jnp.int32)
    in_d5 = ((order < 4) | (order == 7)).astype(jnp.int32)
    at_d5 = jnp.where(order < 4, order, jnp.where(order == 7, 4, 0)).astype(jnp.int32)
    at_qkv = jnp.where(in_d5 == 1, 0, order - 4).astype(jnp.int32)

    def body(in_d5_ref, at_d5_ref, at_qkv_ref, ht_hbm, d5_ref, dqkv_ref, gout_ref, sm_ref, rin_ref, rout_ref, rsm_ref,
             acc, sbuf, ht_vmem, in_send, in_recv, side_send, side_recv, local_sems, ht_sems):
        s = pl.program_id(0)
        t = pl.program_id(1)
        x, y, c = lax.axis_index("x"), lax.axis_index("y"), lax.axis_index("c")
        me = 4 * x + 2 * y + c

        def ht_copy(j):
            return pltpu.make_async_copy(ht_hbm.at[:, j * tk:(j + 1) * tk], ht_vmem.at[j], ht_sems.at[j])

        def in_copy(k):
            to = (me + 1 + k) % N_DEV
            return pltpu.make_async_remote_copy(
                src_ref=sbuf.at[k % 2], dst_ref=rin_ref.at[me], send_sem=in_send.at[k], recv_sem=in_recv.at[k],
                device_id=(to // 4, (to // 2) % 2, to % 2), device_id_type=MESH)

        def in_landing(k):
            frm = (me + 2 * N_DEV - 1 - k) % N_DEV
            return pltpu.make_async_remote_copy(
                src_ref=sbuf.at[0], dst_ref=rin_ref.at[frm], send_sem=in_send.at[k], recv_sem=in_recv.at[k],
                device_id=(x, y, c), device_id_type=MESH)

        def side_copies():
            local = [pltpu.make_async_copy(gout_ref.at[me], rout_ref.at[me], local_sems.at[0]),
                     pltpu.make_async_copy(sm_ref, rsm_ref.at[me], local_sems.at[1])]
            remote = []
            for k in range(1, N_DEV):
                px = 1 - x if k & 4 else x
                py = 1 - y if k & 2 else y
                pc = 1 - c if k & 1 else c
                peer = 4 * px + 2 * py + pc
                for a, (src, dst) in enumerate(((gout_ref.at[peer], rout_ref.at[me]), (sm_ref, rsm_ref.at[me]))):
                    remote.append(pltpu.make_async_remote_copy(
                        src_ref=src, dst_ref=dst, send_sem=side_send.at[a * 7 + k - 1],
                        recv_sem=side_recv.at[a * 7 + k - 1], device_id=(px, py, pc), device_id_type=MESH))
            return local, remote

        @pl.when((s == 0) & (t == 0))
        def _():
            for j in range(n_t):
                ht_copy(j).start()
            local, remote = side_copies()
            for cp in local + remote:
                cp.start()

        for j in range(n_t):
            @pl.when((s == 0) & (t == j))
            def _(j=j):
                ht_copy(j).wait()

        @pl.when(t == 0)
        def _():
            acc[...] = jnp.zeros_like(acc)

        dp = jnp.where(in_d5_ref[s] == 1, d5_ref[0], dqkv_ref[0])
        acc[...] += jnp.dot(ht_vmem[t], dp, preferred_element_type=F32)

        for k in range(N_DEV):
            @pl.when((s == k) & (t == n_t - 1))
            def _(k=k):
                if k >= 2:
                    in_copy(k - 2).wait_send()
                sbuf[k % 2] = acc[...].astype(BF16)
                if k < N_DEV - 1:
                    in_copy(k).start()
                else:
                    own = pltpu.make_async_copy(sbuf.at[k % 2], rin_ref.at[me], local_sems.at[2])
                    own.start()
                    in_copy(k - 1).wait_send()
                    for j in range(N_DEV - 1):
                        in_landing(j).wait_recv()
                    local, remote = side_copies()
                    for cp in remote:
                        cp.wait_recv()
                    for cp in remote:
                        cp.wait_send()
                    for cp in local:
                        cp.wait()
                    own.wait()

    hbm = pl.BlockSpec(memory_space=pl.ANY)
    grid_spec = pltpu.PrefetchScalarGridSpec(
        num_scalar_prefetch=3, grid=(N_DEV, n_t),
        in_specs=[hbm,
                  pl.BlockSpec((1, tk, CONV_W), lambda s, t, use, a5, a3: (a5[s], jnp.where(use[s] == 1, t, 0), 0)),
                  pl.BlockSpec((1, tk, CONV_W), lambda s, t, use, a5, a3: (a3[s], jnp.where(use[s] == 1, 0, t), 0)),
                  hbm, hbm],
        out_specs=(hbm, hbm, hbm),
        scratch_shapes=[pltpu.VMEM((D, CONV_W), F32), pltpu.VMEM((2, D, CONV_W), BF16),
                        pltpu.VMEM((n_t, D, tk), BF16),
                        pltpu.SemaphoreType.DMA((N_DEV - 1,)), pltpu.SemaphoreType.DMA((N_DEV - 1,)),
                        pltpu.SemaphoreType.DMA((14,)), pltpu.SemaphoreType.DMA((14,)),
                        pltpu.SemaphoreType.DMA((3,)), pltpu.SemaphoreType.DMA((n_t,))])
    return pl.pallas_call(
        body, name="dw_exchange", grid_spec=grid_spec,
        out_shape=(jax.ShapeDtypeStruct((N_DEV, D, CONV_W), BF16), jax.ShapeDtypeStruct(gw_out.shape, F32),
                   jax.ShapeDtypeStruct((N_DEV,) + small.shape, F32)),
        compiler_params=_params(("arbitrary", "arbitrary"), 52),
    )(in_d5, at_d5, at_qkv, ht, d5, dqkv, gw_out, small)


def _adamw_math(w, g, m, v):
    m2 = ADAM_B1 * m + (1.0 - ADAM_B1) * g
    v2 = ADAM_B2 * v + (1.0 - ADAM_B2) * (g * g)
    m_hat = m2 / (1.0 - ADAM_B1 ** ADAM_STEP)
    v_hat = v2 / (1.0 - ADAM_B2 ** ADAM_STEP)
    delta = -ADAM_LR * (m_hat / (jnp.sqrt(v_hat) + ADAM_EPS) + ADAM_WD * w)
    return delta, m2, v2


def _adamw_sharded(name, parts, w, m, v, rows):
    R, C = w.shape

    def body(p_ref, w_ref, m_ref, v_ref, g_ref, d_ref, m2_ref, v2_ref):
        g = p_ref[0].astype(F32)
        for s in range(1, N_DEV):
            g = g + p_ref[s].astype(F32)
        g_ref[...] = g
        d_ref[...], m2_ref[...], v2_ref[...] = _adamw_math(w_ref[...], g, m_ref[...], v_ref[...])

    spec = pl.BlockSpec((rows, C), lambda i: (i, 0))
    out = jax.ShapeDtypeStruct((R, C), F32)
    return pl.pallas_call(
        body, name=name, grid=(R // rows,),
        out_shape=(out,) * 4,
        in_specs=[pl.BlockSpec((N_DEV, rows, C), lambda i: (0, i, 0)), spec, spec, spec],
        out_specs=(spec,) * 4,
        compiler_params=_params(("arbitrary",), 40),
    )(parts, w, m, v)


def _adamw_small(parts, w, m, v):
    R = w.shape[0]

    def body(p_ref, w_ref, m_ref, v_ref, g_ref, d_ref, m2_ref, v2_ref):
        g = p_ref[0]
        for s in range(1, N_DEV):
            g = g + p_ref[s]
        g_ref[...] = g
        d_ref[...], m2_ref[...], v2_ref[...] = _adamw_math(w_ref[...], g, m_ref[...], v_ref[...])

    vm = pl.BlockSpec(memory_space=pltpu.VMEM)
    out = jax.ShapeDtypeStruct((R, LANES), F32)
    return pl.pallas_call(
        body, name="adamw_small", out_shape=(out,) * 4, in_specs=[vm] * 4, out_specs=(vm,) * 4,
    )(parts, w, m, v)


def _qk_gain_grads(a_parts, q_norm_w, k_norm_w, bias_g):
    def body(a1_ref, a2_ref, a3_ref, qw_ref, kw_ref, bg_ref, out_ref):
        tot = jnp.zeros((8, LANES), F32)
        for a_ref in (a1_ref, a2_ref, a3_ref):
            for p in range(N_PAIRS):
                tot = tot + a_ref[p]
        col = jnp.sum(tot, axis=0, keepdims=True)
        a64 = col[:, 0:HEAD_DIM] + col[:, HEAD_DIM:LANES]
        out_ref[...] = jnp.zeros_like(out_ref)
        out_ref[0:1, 0:HEAD_DIM] = LOGIT_SCALE * kw_ref[...] * a64
        out_ref[1:2, 0:HEAD_DIM] = LOGIT_SCALE * qw_ref[...] * a64
        for h in range(N_HEADS):
            r0, c0 = 2 + h // 4, (h % 4) * N_BUCKETS
            out_ref[r0:r0 + 1, c0:c0 + N_BUCKETS] = bg_ref[h // 2, h % 2:h % 2 + 1, 0:N_BUCKETS]

    vm = pl.BlockSpec(memory_space=pltpu.VMEM)
    return pl.pallas_call(
        body, name="qk_gain_grads", out_shape=jax.ShapeDtypeStruct((8, LANES), F32),
        in_specs=[vm] * 6, out_specs=vm,
    )(*a_parts, q_norm_w.reshape(1, HEAD_DIM), k_norm_w.reshape(1, HEAD_DIM), bias_g)


SMALL_ROWS = 32


def _pack_small(norm_w, conv_b, q_norm_w, k_norm_w, rel_bias, conv_w_full):
    pad64 = lambda a: jnp.pad(a, (0, LANES - HEAD_DIM)).reshape(1, LANES)
    return jnp.concatenate([
        norm_w.reshape(8, LANES), conv_b.reshape(4, LANES), pad64(q_norm_w), pad64(k_norm_w),
        rel_bias.T.reshape(2, LANES), conv_w_full.reshape(12, LANES), jnp.zeros((4, LANES), F32)], axis=0)


def _unpack_small(s):
    return (s[0:8].reshape(D_MODEL), s[8:12].reshape(CONV_W), s[12, 0:HEAD_DIM], s[13, 0:HEAD_DIM],
            s[14:16].reshape(N_HEADS, N_BUCKETS).T, s[16:28].reshape(3, CONV_W))


def kernel(x, norm_w, w_in, conv_w, conv_b, q_norm_w, k_norm_w, rel_bias, w_out, loss_target, m_norm_w, m_w_in, m_conv_w, m_conv_b, m_q_norm_w, m_k_norm_w, m_rel_bias, m_w_out, v_norm_w, v_w_in, v_conv_w, v_conv_b, v_q_norm_w, v_k_norm_w, v_rel_bias, v_w_out):
    S = x.shape[1]
    x2 = x.reshape(S, D_MODEL)
    tgt = loss_target.reshape(S, D_MODEL)
    me = 4 * lax.axis_index("x") + 2 * lax.axis_index("y") + lax.axis_index("c")
    shard_w = CONV_W // N_DEV

    cw_pad = jnp.pad(conv_w, ((0, 5), (0, LANES - shard_w)))
    wg, wo_g, cw_g = _ag_weights(w_in, w_out, cw_pad)
    wo = wo_g.reshape(D_MODEL, D_MODEL)
    conv_w_full = cw_g[:, 0:3, 0:shard_w].transpose(1, 0, 2).reshape(3, CONV_W)

    bmat_t = jnp.asarray(np.stack([_bucket_matrix(d).T for d in DILATIONS]))
    tables = _bias_tables(rel_bias, bmat_t)
    gain = jnp.tile(q_norm_w * k_norm_w * LOGIT_SCALE, N_HEADS).reshape(1, ATTN_W)

    u, gb, gc, zc, za, q_hat, k_hat, v, rq, rk = _fwd_proj(x2, norm_w, wg)
    o_parts, lse_parts = [], []
    for g, dil in enumerate(DILATIONS):
        o_g, lse_g = _attn_fwd(g, dil, q_hat[g], k_hat[g], v[g], gain, tables)
        o_parts.append(o_g)
        lse_parts.append(lse_g)
    y, o, lse = _combine_gates(o_parts, lse_parts, u, gb, gc, zc, za, conv_w_full, conv_b)
    d_out, dy, gwo_part, loss_part = _out_proj(y, x2, tgt, wo)
    loss_mine = jnp.sum(loss_part) * (0.5 / D_MODEL)

    d5, conv_small, d_o, delta = _gates_bwd(dy, u, gb, gc, zc, za, o, conv_w_full, conv_b)
    dqs, dks, dvs, ds_parts, a_parts = [], [], [], [], []
    for g, dil in enumerate(DILATIONS):
        dq, dk, dv, ds_sum, a_sum = _attn_bwd(g, dil, q_hat[g], k_hat[g], v[g], d_o[g], lse[g], delta[g], rq[g], rk[g],
                                              gain, tables)
        dqs.append((dq, dil))
        dks.append((dk, dil))
        dvs.append((dv, dil))
        ds_parts.append(ds_sum)
        a_parts.append(a_sum)
    grad_x, gnw_part, h_t, dqkv = _proj_bwd(x2, d_out, norm_w, wg, d5, [dqs, dks, dvs])
    bias_g = _bias_grad(ds_parts, bmat_t)
    qk_small = _qk_gain_grads(a_parts, q_norm_w, k_norm_w, bias_g)

    small_part = jnp.concatenate([
        gnw_part.sum(axis=0).reshape(8, LANES), conv_small[3].reshape(4, LANES), qk_small[0:4],
        conv_small[0:3].reshape(12, LANES), jnp.zeros((4, LANES), F32).at[0, 0].set(loss_mine)], axis=0)
    r_in, r_out, r_small = _dw_exchange(h_t, d5, dqkv, gwo_part.reshape(N_DEV, D_MODEL // N_DEV, D_MODEL), small_part)

    g_win, d_win, m_win, v_win = _adamw_sharded("adamw_w_in", r_in, w_in, m_w_in, v_w_in, 128)
    g_wo, d_wo, m_wo, v_wo = _adamw_sharded("adamw_w_out", r_out, w_out, m_w_out, v_w_out, 32)

    def full_conv(a):
        return jnp.zeros((3, CONV_W), F32).at[:, 0:shard_w].set(a)

    packs = [_pack_small(nw_, cb_, qw_, kw_, rb_, full_conv(cw_)) for nw_, cb_, qw_, kw_, rb_, cw_ in (
        (norm_w, conv_b, q_norm_w, k_norm_w, rel_bias, conv_w),
        (m_norm_w, m_conv_b, m_q_norm_w, m_k_norm_w, m_rel_bias, m_conv_w),
        (v_norm_w, v_conv_b, v_q_norm_w, v_k_norm_w, v_rel_bias, v_conv_w))]
    r_small_mine = r_small.at[:, 16:28, :].set(
        jnp.pad(lax.dynamic_slice_in_dim(r_small[:, 16:28, :].reshape(N_DEV, 3, CONV_W), me * shard_w, shard_w, axis=2),
                ((0, 0), (0, 0), (0, CONV_W - shard_w))).reshape(N_DEV, 12, LANES))
    outs_small = _adamw_small(r_small_mine, *packs)
    g_s, d_s, m_s, v_s = [_unpack_small(t) for t in outs_small]
    loss = outs_small[0][28, 0]

    def leaves(small, big_in, big_out):
        nw_, cb_, qw_, kw_, rb_, cwf = small
        return (nw_, big_in, cwf[:, 0:shard_w], cb_, qw_, kw_, rb_, big_out)

    return (loss, grad_x.reshape(x.shape),
            *leaves(g_s, g_win, g_wo), *leaves(d_s, d_win, d_wo), *leaves(m_s, m_win, m_wo), *leaves(v_s, v_win, v_wo))
```

```python
import math

import numpy as np
import jax
import jax.numpy as jnp
from jax import lax
from jax.experimental import pallas as pl
from jax.experimental.pallas import tpu as pltpu

F32 = jnp.float32
BF16 = jnp.bfloat16

N_DEV = 8
D_MODEL = 1024
CONV_W = 512
ATTN_W = 512
N_HEADS = 8
HEAD_DIM = 64
N_PAIRS = N_HEADS // 2
LANES = 128
HALF = 64
QB = 128
KB = QB + 2 * HALF
DILATIONS = (1, 4, 16)
N_BUCKETS = 32
MAX_DISTANCE = 1024
EPS = 1e-6
NEG = -1e30
LOGIT_SCALE = HEAD_DIM ** -0.5

ADAM_LR = 0.001
ADAM_B1 = 0.9
ADAM_B2 = 0.999
ADAM_EPS = 1e-08
ADAM_WD = 0.01
ADAM_STEP = 10

MESH = pl.DeviceIdType.MESH
MIB = 1024 * 1024


def _params(semantics, vmem_mib):
    return pltpu.CompilerParams(dimension_semantics=semantics, vmem_limit_bytes=vmem_mib * MIB)


def _lane(shape):
    return lax.broadcasted_iota(jnp.int32, shape, len(shape) - 1)


def _sigmoid(z):
    return 1.0 / (1.0 + jnp.exp(-z))


def _split_dot(x, w):
    hi = x.astype(BF16)
    lo = (x - hi.astype(F32)).astype(BF16)
    return jnp.dot(hi, w, preferred_element_type=F32) + jnp.dot(lo, w, preferred_element_type=F32)


def _same_head():
    r = lax.broadcasted_iota(jnp.int32, (LANES, LANES), 0) // HEAD_DIM
    c = lax.broadcasted_iota(jnp.int32, (LANES, LANES), 1) // HEAD_DIM
    return (r == c).astype(BF16)


def _head_to_lanes(p):
    r = lax.broadcasted_iota(jnp.int32, (LANES, LANES), 0)
    c = lax.broadcasted_iota(jnp.int32, (LANES, LANES), 1) // HEAD_DIM
    return (r == 2 * p + c).astype(BF16)


def _lanes_to_head(p, scale):
    r = lax.broadcasted_iota(jnp.int32, (LANES, LANES), 0) // HEAD_DIM
    c = lax.broadcasted_iota(jnp.int32, (LANES, LANES), 1)
    return jnp.where(c == 2 * p + r, scale, 0.0).astype(BF16)


VIEW_DILATIONS = DILATIONS[1:]


def _view_shape(S, dil, width, dtype):
    return jax.ShapeDtypeStruct((S // dil, dil * width), dtype)


def _view_spec(tm, dil, width):
    return pl.BlockSpec((tm // dil, dil * width), lambda i: (i, 0))


def _tile_scratch(tm, width):
    return pltpu.VMEM((width // LANES, tm, LANES), F32)


def _scratch_value(scr_ref):
    n = scr_ref.shape[0]
    return scr_ref[0] if n == 1 else jnp.concatenate([scr_ref[c] for c in range(n)], axis=1)


def _to_view(scr_ref, out_ref, dil, dtype):
    n, tm, _ = scr_ref.shape
    for r in range(dil):
        for c in range(n):
            col = (r * n + c) * LANES
            out_ref[:, col:col + LANES] = scr_ref[c, pl.ds(r, tm // dil, stride=dil), :].astype(dtype)


def _from_view(blk_ref, scr_ref, dil):
    n, tm, _ = scr_ref.shape
    for r in range(dil):
        for c in range(n):
            col = (r * n + c) * LANES
            scr_ref[c, pl.ds(r, tm // dil, stride=dil), :] = blk_ref[:, col:col + LANES].astype(F32)


def _bucket_matrix(dilation):
    rel = np.arange(KB)[None, :] - HALF - np.arange(QB)[:, None]
    band = np.abs(rel) <= HALF
    dist = np.clip(rel, -HALF, HALF) * dilation
    half_b = N_BUCKETS // 2
    max_exact = half_b // 2
    ret = np.where(dist > 0, half_b, 0)
    n = np.abs(dist)
    nf = np.maximum(n, 1).astype(np.float32)
    large = max_exact + (np.log(nf / np.float32(max_exact)) / np.float32(math.log(MAX_DISTANCE / max_exact))
                         * np.float32(half_b - max_exact)).astype(np.int32)
    large = np.minimum(large, half_b - 1)
    bucket = ret + np.where(n < max_exact, n, large)
    return np.where(band, bucket, -1).astype(np.int32)


def _ag_weights(w_in, w_out, cw_pad):
    n_arr = 3

    def body(win_ref, wout_ref, cw_ref, gin_ref, gout_ref, gcw_ref, send_sems, recv_sems):
        x, y, c = lax.axis_index("x"), lax.axis_index("y"), lax.axis_index("c")
        me = (x, y, c)
        sibling = (x, y, 1 - c)
        chips = [(1 - x, y), (x, 1 - y), (1 - x, 1 - y)]
        arrays = (gin_ref, gout_ref, gcw_ref)

        def slot(px, py, pc):
            return 4 * px + 2 * py + pc

        gin_ref[slot(*me)] = win_ref[...].astype(BF16)
        gout_ref[slot(*me)] = wout_ref[...].astype(BF16)
        gcw_ref[slot(*me)] = cw_ref[...]

        def copy(a, k, block, to):
            ref = arrays[a].at[slot(*block)]
            return pltpu.make_async_remote_copy(
                src_ref=ref, dst_ref=ref, send_sem=send_sems.at[a * 7 + k], recv_sem=recv_sems.at[a * 7 + k],
                device_id=to, device_id_type=MESH)

        first = [copy(a, 0, me, sibling) for a in range(n_arr)]
        for j, chip in enumerate(chips):
            first += [copy(a, 1 + j, me, (*chip, c)) for a in range(n_arr)]
        for cp in first:
            cp.start()
        passed = []
        for j, chip in enumerate(chips):
            for a in range(n_arr):
                copy(a, 1 + j, (*chip, c), me).wait_recv()
            for a in range(n_arr):
                cp = copy(a, 4 + j, (*chip, c), sibling)
                cp.start()
                passed.append(cp)
        for a in range(n_arr):
            copy(a, 0, sibling, me).wait_recv()
        for j, chip in enumerate(chips):
            for a in range(n_arr):
                copy(a, 4 + j, (*chip, 1 - c), me).wait_recv()
        for cp in first + passed:
            cp.wait_send()

    vm = pl.BlockSpec(memory_space=pltpu.VMEM)
    return pl.pallas_call(
        body, name="ag_weights",
        out_shape=(jax.ShapeDtypeStruct((N_DEV,) + w_in.shape, BF16),
                   jax.ShapeDtypeStruct((N_DEV,) + w_out.shape, BF16),
                   jax.ShapeDtypeStruct((N_DEV,) + cw_pad.shape, F32)),
        in_specs=[vm, vm, vm], out_specs=(vm, vm, vm),
        scratch_shapes=[pltpu.SemaphoreType.DMA((n_arr * 7,)), pltpu.SemaphoreType.DMA((n_arr * 7,))],
        compiler_params=pltpu.CompilerParams(vmem_limit_bytes=40 * MIB),
    )(w_in, w_out, cw_pad)


def _fwd_proj(x, norm_w, wg):
    S = x.shape[0]
    tm = 512

    def body(x_ref, nw_ref, wg_ref, u_ref, gb_ref, gc_ref, zc_ref, za_ref, *rest):
        q_refs, k_refs, v_refs, rq_refs, rk_refs = (rest[3 * n:3 * n + 3] for n in range(5))
        scr, rscr = rest[15:]
        xv = x_ref[...]
        r = lax.rsqrt(jnp.mean(xv * xv, axis=-1, keepdims=True) + EPS)
        h = (xv * r * nw_ref[...]).astype(BF16)
        lane = _lane((tm, LANES))
        lo = lane < HEAD_DIM
        plain = {0: u_ref, 1: gb_ref, 2: gc_ref, 3: zc_ref, 7: za_ref}
        normed = {4: (q_refs, rq_refs), 5: (k_refs, rk_refs)}

        def emit(src, refs, dtype):
            refs[0][...] = _scratch_value(src).astype(dtype)
            for dil, ref in zip(VIEW_DILATIONS, refs[1:]):
                _to_view(src, ref, dil, dtype)

        group_order = (4, 0, 5, 1, 6, 2, 3, 7)
        acc_next = jnp.dot(h, wg_ref[group_order[0]], preferred_element_type=F32)
        for n, j in enumerate(group_order):
            acc = acc_next
            if n < 7:
                acc_next = jnp.dot(h, wg_ref[group_order[n + 1]], preferred_element_type=F32)
            if j in plain:
                plain[j][...] = acc.astype(BF16)
                continue
            if j == 6:
                for p in range(N_PAIRS):
                    scr[p] = acc[:, p * LANES:(p + 1) * LANES]
                emit(scr, v_refs, BF16)
                continue
            out_refs, r_refs = normed[j]
            r_tile = jnp.zeros((tm, LANES), F32)
            for p in range(N_PAIRS):
                blk = acc[:, p * LANES:(p + 1) * LANES]
                sq = blk * blk
                s_lo = jnp.sum(jnp.where(lo, sq, 0.0), axis=-1, keepdims=True)
                s_hi = jnp.sum(jnp.where(lo, 0.0, sq), axis=-1, keepdims=True)
                r_lo = lax.rsqrt(s_lo * (1.0 / HEAD_DIM) + EPS)
                r_hi = lax.rsqrt(s_hi * (1.0 / HEAD_DIM) + EPS)
                scr[p] = blk * jnp.where(lo, r_lo, r_hi)
                r_tile = jnp.where(lane == 2 * p, r_lo, r_tile)
                r_tile = jnp.where(lane == 2 * p + 1, r_hi, r_tile)
            rscr[0] = r_tile
            emit(scr, out_refs, BF16)
            emit(rscr, r_refs, F32)

    piece = jax.ShapeDtypeStruct((S, CONV_W), BF16)
    pspec = pl.BlockSpec((tm, CONV_W), lambda i: (i, 0))
    wide = [_view_shape(S, d, ATTN_W, BF16) for d in DILATIONS]
    wide_specs = [_view_spec(tm, d, ATTN_W) for d in DILATIONS]
    stat = [_view_shape(S, d, LANES, F32) for d in DILATIONS]
    stat_specs = [_view_spec(tm, d, LANES) for d in DILATIONS]
    outs = pl.pallas_call(
        body, name="fwd_proj", grid=(S // tm,),
        out_shape=[piece] * 5 + wide * 3 + stat * 2,
        in_specs=[pl.BlockSpec((tm, D_MODEL), lambda i: (i, 0)),
                  pl.BlockSpec((1, D_MODEL), lambda i: (0, 0)),
                  pl.BlockSpec(wg.shape, lambda i: (0, 0, 0), pipeline_mode=pl.Buffered(1))],
        out_specs=[pspec] * 5 + wide_specs * 3 + stat_specs * 2,
        scratch_shapes=[_tile_scratch(tm, ATTN_W), _tile_scratch(tm, LANES)],
        compiler_params=_params(("arbitrary",), 48),
    )(x, norm_w.reshape(1, D_MODEL), wg)
    u, gb, gc, zc, za = outs[:5]
    q_hat, k_hat, v, rq, rk = (outs[5 + 3 * n:8 + 3 * n] for n in range(5))
    return u, gb, gc, zc, za, q_hat, k_hat, v, rq, rk


def _bias_tables(rel_bias, bmat_t):
    def body(rb_ref, b_ref, out_ref):
        h = pl.program_id(1)
        b = b_ref[0]
        t = jnp.full((KB, QB), NEG, F32)
        for bk in range(N_BUCKETS):
            t = jnp.where(b == bk, rb_ref[bk, h], t)
        out_ref[0, 0] = t

    return pl.pallas_call(
        body, name="bias_tables", grid=(len(DILATIONS), N_HEADS),
        out_shape=jax.ShapeDtypeStruct((len(DILATIONS), N_HEADS, KB, QB), F32),
        in_specs=[pl.BlockSpec(memory_space=pltpu.SMEM),
                  pl.BlockSpec((1, KB, QB), lambda g, h: (g, 0, 0))],
        out_specs=pl.BlockSpec((1, 1, KB, QB), lambda g, h: (g, h, 0, 0)),
        compiler_params=_params(("arbitrary", "arbitrary"), 16),
    )(rel_bias, bmat_t)


def _bias_grad(ds_parts, bmat_t):
    present = [sorted(set(_bucket_matrix(d).ravel().tolist()) - {-1}) for d in DILATIONS]

    def body(ds1_ref, ds2_ref, ds3_ref, b_ref, out_ref):
        ds_refs = (ds1_ref, ds2_ref, ds3_ref)
        lane = _lane((1, LANES))
        row = lax.broadcasted_iota(jnp.int32, (8, LANES), 0)
        out = jnp.zeros((8, LANES), F32)
        for e in range(2):
            vec = jnp.zeros((1, LANES), F32)
            for g in range(len(DILATIONS)):
                b = b_ref[g]
                ds = ds_refs[g][0, :, e * QB:(e + 1) * QB]
                for bk in present[g]:
                    s = jnp.sum(jnp.where(b == bk, ds, 0.0), axis=-1, keepdims=True)
                    s = jnp.sum(s, axis=0, keepdims=True)
                    vec = vec + jnp.where(lane == bk, s, 0.0)
            out = jnp.where(row == e, vec, out)
        out_ref[0] = out

    return pl.pallas_call(
        body, name="bias_grad", grid=(N_PAIRS,),
        out_shape=jax.ShapeDtypeStruct((N_PAIRS, 8, LANES), F32),
        in_specs=[pl.BlockSpec((1, KB, 2 * QB), lambda p: (p, 0, 0))] * len(DILATIONS)
        + [pl.BlockSpec((len(DILATIONS), KB, QB), lambda p: (0, 0, 0))],
        out_specs=pl.BlockSpec((1, 8, LANES), lambda p: (p, 0, 0)),
        compiler_params=_params(("arbitrary",), 16),
    )(*ds_parts, bmat_t)


def _halo_specs(width, rows, T, L, cols_of):
    per = T // rows
    last = L // rows - 1
    prev = pl.BlockSpec((rows, width), lambda r, i: (jnp.maximum(i * per - 1, 0), cols_of(r)))
    nxt = pl.BlockSpec((rows, width), lambda r, i: (jnp.minimum((i + 1) * per, last), cols_of(r)))
    return prev, nxt


def _attn_fwd(g, dil, qv, kv, vv, gain, tables):
    L = qv.shape[0]
    T = min(1024, L)
    n_sub = T // QB

    n_i = L // T
    n_steps = dil * n_i
    N_SLOTS = 3

    def body(q_hbm, k_hbm, v_hbm, gain_ref, bias_ref, o_ref, lse_ref, qbuf, kwins, vwins, s_scr, p_scr, sems):
        i = pl.program_id(1)
        step = pl.program_id(0) * n_i + i
        slot = step % N_SLOTS

        def transfers(st, act):
            r_, i_ = st // n_i, st % n_i
            sl = st % N_SLOTS
            col = pl.ds(pl.multiple_of(r_ * ATTN_W, ATTN_W), ATTN_W)
            row0 = i_ * T
            act(pltpu.make_async_copy(q_hbm.at[pl.ds(row0, T), col], qbuf.at[sl], sems.at[sl, 0]))
            for a, (src, win) in enumerate(((k_hbm, kwins), (v_hbm, vwins))):
                act(pltpu.make_async_copy(src.at[pl.ds(row0, T), col], win.at[sl, pl.ds(HALF, T)], sems.at[sl, 1 + 3 * a]))

                @pl.when(i_ > 0)
                def _(src=src, win=win, a=a):
                    act(pltpu.make_async_copy(src.at[pl.ds(row0 - HALF, HALF), col], win.at[sl, pl.ds(0, HALF)],
                                              sems.at[sl, 2 + 3 * a]))

                @pl.when(i_ < n_i - 1)
                def _(src=src, win=win, a=a):
                    act(pltpu.make_async_copy(src.at[pl.ds(row0 + T, HALF), col], win.at[sl, pl.ds(HALF + T, HALF)],
                                              sems.at[sl, 3 + 3 * a]))

        @pl.when(step == 0)
        def _():
            for st in range(min(N_SLOTS - 1, n_steps)):
                transfers(st, lambda cp: cp.start())

        @pl.when(step + N_SLOTS - 1 < n_steps)
        def _():
            transfers(step + N_SLOTS - 1, lambda cp: cp.start())

        transfers(step, lambda cp: cp.wait())
        zero_halo = jnp.zeros((HALF, ATTN_W), BF16)

        @pl.when(i == 0)
        def _():
            kwins[slot, 0:HALF] = zero_halo
            vwins[slot, 0:HALF] = zero_halo

        @pl.when(i == n_i - 1)
        def _():
            kwins[slot, HALF + T:] = zero_halo
            vwins[slot, HALF + T:] = zero_halo

        q_ref, kwin, vwin = qbuf.at[slot], kwins.at[slot], vwins.at[slot]
        lo = _lane((QB, LANES)) < HEAD_DIM
        krow = lax.broadcasted_iota(jnp.int32, (KB, 1), 0)
        chan_lo = lax.broadcasted_iota(jnp.int32, (LANES, 1), 0) < HEAD_DIM
        hrow = lax.broadcasted_iota(jnp.int32, (LANES, QB), 0)

        def sub(j, carry):
            r0 = pl.multiple_of(j * QB, QB)
            kpos = i * T + j * QB - HALF + krow
            kvalid = (kpos >= 0) & (kpos < L)
            lse_rows = jnp.zeros((LANES, QB), F32)
            cols = [slice(p * LANES, (p + 1) * LANES) for p in range(N_PAIRS)]
            for p, cs in enumerate(cols):
                qf = q_ref[pl.ds(r0, QB), cs].astype(F32) * gain_ref[:, cs]
                q2 = jnp.concatenate([jnp.where(lo, qf, 0.0), jnp.where(lo, 0.0, qf)], axis=0).astype(BF16)
                s_scr[p] = lax.dot_general(kwin[pl.ds(r0, KB), cs], q2, (((1,), (1,)), ((), ())),
                                           preferred_element_type=F32)
            inv_l = []
            for p in range(N_PAIRS):
                bias2 = jnp.concatenate([bias_ref[0, 2 * p], bias_ref[0, 2 * p + 1]], axis=1)
                logits = jnp.where(kvalid, s_scr[p] + bias2, NEG)
                m = jnp.max(logits, axis=0, keepdims=True)
                pt = jnp.exp(logits - m)
                l = jnp.sum(pt, axis=0, keepdims=True)
                p_scr[p] = pt.astype(BF16)
                inv_l.append(1.0 / l)
                lse2 = m + jnp.log(l)
                lse_rows = jnp.where(hrow == 2 * p, lse2[:, 0:QB], lse_rows)
                lse_rows = jnp.where(hrow == 2 * p + 1, lse2[:, QB:2 * QB], lse_rows)
            for p, cs in enumerate(cols):
                s_scr[p, 0:LANES, :] = lax.dot_general(vwin[pl.ds(r0, KB), cs], p_scr[p], (((0,), (0,)), ((), ())),
                                                       preferred_element_type=F32) * inv_l[p]
            for p, cs in enumerate(cols):
                ot2 = s_scr[p, 0:LANES, :]
                ot = jnp.where(chan_lo, ot2[:, 0:QB], ot2[:, QB:2 * QB])
                o_ref[pl.ds(r0, QB), cs] = ot.T.astype(BF16)
            lse_ref[pl.ds(r0, QB), :] = lse_rows.T
            return carry

        lax.fori_loop(0, n_sub, sub, 0)

    main = pl.BlockSpec((T, ATTN_W), lambda r, i: (i, r))
    hbm = pl.BlockSpec(memory_space=pl.ANY)
    o_g, lse_g = pl.pallas_call(
        body, name=f"attn_fwd_d{dil}", grid=(dil, n_i),
        out_shape=(jax.ShapeDtypeStruct((L, dil * ATTN_W), BF16), jax.ShapeDtypeStruct((L, dil * LANES), F32)),
        in_specs=[hbm, hbm, hbm,
                  pl.BlockSpec((1, ATTN_W), lambda r, i: (0, 0)),
                  pl.BlockSpec((1, N_HEADS, KB, QB), lambda r, i: (g, 0, 0, 0))],
        out_specs=(main, pl.BlockSpec((T, LANES), lambda r, i: (i, r))),
        scratch_shapes=[pltpu.VMEM((N_SLOTS, T, ATTN_W), BF16),
                        pltpu.VMEM((N_SLOTS, T + 2 * HALF, ATTN_W), BF16),
                        pltpu.VMEM((N_SLOTS, T + 2 * HALF, ATTN_W), BF16),
                        pltpu.VMEM((N_PAIRS, KB, 2 * QB), F32), pltpu.VMEM((N_PAIRS, KB, 2 * QB), BF16),
                        pltpu.SemaphoreType.DMA((N_SLOTS, 7))],
        compiler_params=_params(("arbitrary", "arbitrary"), 40),
    )(qv, kv, vv, gain, tables)
    return o_g, lse_g


HALO = 16
CHUNK = 128


def _row_halo_specs(width, tm, S, col=0):
    rows = HALO
    per = tm // rows
    last = S // rows - 1
    prev = pl.BlockSpec((rows, width), lambda i: (jnp.maximum(i * per - 1, 0), col))
    nxt = pl.BlockSpec((rows, width), lambda i: (jnp.minimum((i + 1) * per, last), col))
    return prev, nxt


def _shift_rows(a, prev_row, next_row, tm):
    row = lax.broadcasted_iota(jnp.int32, a.shape, 0)
    a_m1 = jnp.where(row == 0, prev_row, pltpu.roll(a, 1, 0))
    a_p1 = jnp.where(row == tm - 1, next_row, pltpu.roll(a, tm - 1, 0))
    return a_m1, a_p1


def _combine_gates(o_parts, lse_parts, u, gb, gc, zc, za, conv_w, conv_b):
    S = u.shape[0]
    tm = 512
    n_t = S // tm

    def body(o1, o2, o3, l1, l2, l3, u_ref, up_ref, un_ref, gb_ref, gc_ref, gcp_ref, gcn_ref, zc_ref, za_ref,
             cw_ref, cb_ref, y_ref, o_ref, lse_ref, lse2_ref, lse3_ref, so2, so3, sl2, sl3, slse):
        i = pl.program_id(0)
        for blk, scr, dil in ((o2, so2, DILATIONS[1]), (o3, so3, DILATIONS[2]),
                              (l2, sl2, DILATIONS[1]), (l3, sl3, DILATIONS[2])):
            _from_view(blk, scr, dil)
        ls = [l1[...], sl2[0], sl3[0]]
        lmax = jnp.maximum(jnp.maximum(ls[0], ls[1]), ls[2])
        es = [jnp.exp(l - lmax) for l in ls]
        den = es[0] + es[1] + es[2]
        slse[0] = lmax + jnp.log(den)
        lse_ref[...] = slse[0]
        _to_view(slse, lse2_ref, DILATIONS[1], F32)
        _to_view(slse, lse3_ref, DILATIONS[2], F32)
        inv = 1.0 / den
        ws = [e * inv for e in es]
        for p in range(N_PAIRS):
            cs = slice(p * LANES, (p + 1) * LANES)
            acc = jnp.zeros((tm, LANES), F32)
            spread = _head_to_lanes(p)
            for w, o_g in zip(ws, (o1[:, cs].astype(F32), so2[p], so3[p])):
                acc = acc + _split_dot(w, spread) * o_g
            o_ref[:, cs] = acc.astype(BF16)
            za = za_ref[:, cs].astype(F32)
            y_ref[:, CONV_W + p * LANES:CONV_W + (p + 1) * LANES] = (acc * za * _sigmoid(za)).astype(BF16)
        a = gc_ref[...].astype(F32) * u_ref[...].astype(F32)
        a_prev = gcp_ref[15:16, :].astype(F32) * up_ref[15:16, :].astype(F32) * (i > 0).astype(F32)
        a_next = gcn_ref[0:1, :].astype(F32) * un_ref[0:1, :].astype(F32) * (i < n_t - 1).astype(F32)
        a_m1, a_p1 = _shift_rows(a, a_prev, a_next, tm)
        z3 = cw_ref[0:1, :] * a_m1 + cw_ref[1:2, :] * a + cw_ref[2:3, :] * a_p1
        zc = zc_ref[...].astype(F32)
        y_ref[:, 0:CONV_W] = (gb_ref[...].astype(F32) * (z3 + cb_ref[...]) * zc * _sigmoid(zc)).astype(BF16)

    pspec = pl.BlockSpec((tm, CONV_W), lambda i: (i, 0))
    hp, hn = _row_halo_specs(CONV_W, tm, S)
    wide_specs = [_view_spec(tm, d, ATTN_W) for d in DILATIONS]
    stat_specs = [_view_spec(tm, d, LANES) for d in DILATIONS]
    outs = pl.pallas_call(
        body, name="combine_gates", grid=(n_t,),
        out_shape=[jax.ShapeDtypeStruct((S, D_MODEL), BF16), jax.ShapeDtypeStruct((S, ATTN_W), BF16)]
        + [_view_shape(S, d, LANES, F32) for d in DILATIONS],
        in_specs=wide_specs + stat_specs + [pspec, hp, hn, pspec, pspec, hp, hn, pspec, pspec,
                                            pl.BlockSpec((3, CONV_W), lambda i: (0, 0)),
                                            pl.BlockSpec((1, CONV_W), lambda i: (0, 0))],
        out_specs=[pl.BlockSpec((tm, D_MODEL), lambda i: (i, 0)), pspec] + stat_specs,
        scratch_shapes=[_tile_scratch(tm, ATTN_W), _tile_scratch(tm, ATTN_W),
                        _tile_scratch(tm, LANES), _tile_scratch(tm, LANES), _tile_scratch(tm, LANES)],
        compiler_params=_params(("arbitrary",), 40),
    )(*o_parts, *lse_parts, u, u, u, gb, gc, gc, gc, zc, za, conv_w, conv_b.reshape(1, CONV_W))
    return outs[0], outs[1], outs[2:]


def _out_proj(y, x, target, wo):
    S = x.shape[0]
    tm = 512

    def body(y_ref, x_ref, t_ref, wo_ref, g_ref, dy_ref, dwo_ref, loss_ref):
        i = pl.program_id(0)

        @pl.when(i == 0)
        def _():
            dwo_ref[...] = jnp.zeros_like(dwo_ref)
            loss_ref[...] = jnp.zeros_like(loss_ref)

        yv = y_ref[...]
        wo_v = wo_ref[...]
        err = x_ref[...] + jnp.dot(yv, wo_v, preferred_element_type=F32) - t_ref[...]
        e2 = (err * err).reshape(tm // 8, 8, D_MODEL).sum(axis=0)
        part = e2[:, 0:LANES]
        for k in range(1, D_MODEL // LANES):
            part = part + e2[:, k * LANES:(k + 1) * LANES]
        loss_ref[...] += part
        gv = (err * (1.0 / D_MODEL)).astype(BF16)
        g_ref[...] = gv
        dy_ref[...] = lax.dot_general(gv, wo_v, (((1,), (1,)), ((), ())), preferred_element_type=F32).astype(BF16)
        dwo_ref[...] += lax.dot_general(yv, gv, (((0,), (0,)), ((), ())), preferred_element_type=F32)

    tile = pl.BlockSpec((tm, D_MODEL), lambda i: (i, 0))
    return pl.pallas_call(
        body, name="out_proj", grid=(S // tm,),
        out_shape=(jax.ShapeDtypeStruct((S, D_MODEL), BF16), jax.ShapeDtypeStruct((S, D_MODEL), BF16),
                   jax.ShapeDtypeStruct((D_MODEL, D_MODEL), F32), jax.ShapeDtypeStruct((8, LANES), F32)),
        in_specs=[tile, tile, tile, pl.BlockSpec((D_MODEL, D_MODEL), lambda i: (0, 0), pipeline_mode=pl.Buffered(1))],
        out_specs=(tile, tile, pl.BlockSpec((D_MODEL, D_MODEL), lambda i: (0, 0)),
                   pl.BlockSpec((8, LANES), lambda i: (0, 0))),
        compiler_params=_params(("arbitrary",), 48),
    )(y, x, target, wo)


def _gates_bwd(dy, u, gb, gc, zc, za, o, conv_w, conv_b):
    S = u.shape[0]
    tm = 512
    n_t = S // tm

    def body(dy_ref, dyp_ref, dyn_ref, u_ref, up_ref, un_ref, gb_ref, gbp_ref, gbn_ref, gc_ref, gcp_ref, gcn_ref,
             zc_ref, zcp_ref, zcn_ref, za_ref, o_ref, cw_ref, cb_ref,
             d5_ref, small_ref,
             do_ref, do2_ref, do3_ref, delta_ref, delta2_ref, delta3_ref, sdo, sdelta,
             u_ext, gc_ext, gb_ext, zc_ext, dy_ext):
        i = pl.program_id(0)

        @pl.when(i == 0)
        def _():
            small_ref[...] = jnp.zeros_like(small_ref)

        exts = ((u_ext, up_ref, u_ref, un_ref), (gc_ext, gcp_ref, gc_ref, gcn_ref), (gb_ext, gbp_ref, gb_ref, gbn_ref),
                (zc_ext, zcp_ref, zc_ref, zcn_ref))
        for ext, pr, mn, nx in exts:
            ext[0:HALO] = pr[...]
            ext[HALO:HALO + tm] = mn[...]
            ext[HALO + tm:] = nx[...]
        dy_ext[0:HALO] = dyp_ref[...]
        dy_ext[HALO:HALO + tm] = dy_ref[:, 0:CONV_W]
        dy_ext[HALO + tm:] = dyn_ref[...]
        zero_halo = jnp.zeros((HALO, CONV_W), BF16)

        @pl.when(i == 0)
        def _():
            u_ext[0:HALO] = zero_halo
            dy_ext[0:HALO] = zero_halo

        @pl.when(i == n_t - 1)
        def _():
            u_ext[HALO + tm:] = zero_halo
            dy_ext[HALO + tm:] = zero_halo

        wide_rows = CHUNK + 2 * HALO
        mid = slice(HALO, HALO + CHUNK)
        csum = lambda t: jnp.sum(t, axis=0, keepdims=True)

        def one_group(p, c0, sums):
            cs = slice(p * LANES, (p + 1) * LANES)
            w0, w1, w2, cb = cw_ref[0:1, cs], cw_ref[1:2, cs], cw_ref[2:3, cs], cb_ref[:, cs]
            rows = pl.ds(c0, CHUNK)
            wide = pl.ds(c0, wide_rows)
            u_e, gc_e, gb_e, zc_e, dy_e = (r[wide, cs].astype(F32) for r in (u_ext, gc_ext, gb_ext, zc_ext, dy_ext))
            a_e = gc_e * u_e
            sz_e = _sigmoid(zc_e)
            dz3_e = dy_e * zc_e * sz_e * gb_e
            a, a_m1, a_p1 = a_e[mid], pltpu.roll(a_e, 1, 0)[mid], pltpu.roll(a_e, wide_rows - 1, 0)[mid]
            dz3, dz3_m1, dz3_p1 = dz3_e[mid], pltpu.roll(dz3_e, 1, 0)[mid], pltpu.roll(dz3_e, wide_rows - 1, 0)[mid]
            uv, gcv, gbv, zcv, dyc, sz = u_e[mid], gc_e[mid], gb_e[mid], zc_e[mid], dy_e[mid], sz_e[mid]
            cvp = w0 * a_m1 + w1 * a + w2 * a_p1 + cb
            dcv = dyc * zcv * sz
            d5_ref[3, rows, cs] =(dyc * gbv * cvp * (sz * (1.0 + zcv * (1.0 - sz)))).astype(BF16)
            d5_ref[1, rows, cs] =(dcv * cvp).astype(BF16)
            da = w0 * dz3_p1 + w1 * dz3 + w2 * dz3_m1
            d5_ref[0, rows, cs] =(da * gcv).astype(BF16)
            d5_ref[2, rows, cs] =(da * uv).astype(BF16)
            sums = (sums[0] + csum(dz3 * a_m1), sums[1] + csum(dz3 * a), sums[2] + csum(dz3 * a_p1),
                    sums[3] + csum(dz3))

            dya = dy_ref[rows, CONV_W + p * LANES:CONV_W + (p + 1) * LANES].astype(F32)
            zav = za_ref[rows, cs].astype(F32)
            ov = o_ref[rows, cs].astype(F32)
            sa = _sigmoid(zav)
            d_o = dya * zav * sa
            sdo[p, rows, :] = d_o
            do_ref[rows, cs] = d_o.astype(BF16)
            d5_ref[4, rows, cs] =(dya * ov * (sa * (1.0 + zav * (1.0 - sa)))).astype(BF16)
            return sums, _split_dot(d_o * ov, _lanes_to_head(p, 1.0))

        def chunk(ci, carry):
            c0 = pl.multiple_of(ci * CHUNK, CHUNK)
            out, delta = [], jnp.zeros((CHUNK, LANES), F32)
            for p in range(N_PAIRS):
                sums, part = one_group(p, c0, carry[p])
                out.append(sums)
                delta = delta + part
            sdelta[0, pl.ds(c0, CHUNK), :] = delta
            return tuple(out)

        z = jnp.zeros((1, LANES), F32)
        all_sums = lax.fori_loop(0, tm // CHUNK, chunk, ((z, z, z, z),) * N_PAIRS)
        for p in range(N_PAIRS):
            for k in range(4):
                small_ref[k:k + 1, p * LANES:(p + 1) * LANES] += all_sums[p][k]
        delta_ref[...] = sdelta[0]
        for dil, wide_ref, stat_ref in zip(VIEW_DILATIONS, (do2_ref, do3_ref), (delta2_ref, delta3_ref)):
            _to_view(sdo, wide_ref, dil, BF16)
            _to_view(sdelta, stat_ref, dil, F32)

    pspec = pl.BlockSpec((tm, CONV_W), lambda i: (i, 0))
    hp, hn = _row_halo_specs(CONV_W, tm, S)
    piece = jax.ShapeDtypeStruct((S, CONV_W), BF16)
    outs = pl.pallas_call(
        body, name="gates_bwd", grid=(n_t,),
        out_shape=[jax.ShapeDtypeStruct((5, S, CONV_W), BF16), jax.ShapeDtypeStruct((8, CONV_W), F32)]
        + [_view_shape(S, d, ATTN_W, BF16) for d in DILATIONS] + [_view_shape(S, d, LANES, F32) for d in DILATIONS],
        in_specs=[pl.BlockSpec((tm, D_MODEL), lambda i: (i, 0)), hp, hn,
                  pspec, hp, hn, pspec, hp, hn, pspec, hp, hn, pspec, hp, hn, pspec, pspec,
                  pl.BlockSpec((3, CONV_W), lambda i: (0, 0)), pl.BlockSpec((1, CONV_W), lambda i: (0, 0))],
        out_specs=[pl.BlockSpec((5, tm, CONV_W), lambda i: (0, i, 0)), pl.BlockSpec((8, CONV_W), lambda i: (0, 0))]
        + [_view_spec(tm, d, ATTN_W) for d in DILATIONS] + [_view_spec(tm, d, LANES) for d in DILATIONS],
        scratch_shapes=[_tile_scratch(tm, ATTN_W), _tile_scratch(tm, LANES)]
        + [pltpu.VMEM((tm + 2 * HALO, CONV_W), BF16)] * 5,
        compiler_params=_params(("arbitrary",), 40),
    )(dy, dy, dy, u, u, u, gb, gb, gb, gc, gc, gc, zc, zc, zc, za, o, conv_w, conv_b.reshape(1, CONV_W))
    return outs[0], outs[1], outs[2:5], outs[5:8]


def _attn_bwd(g, dil, qv, kv, vv, dov, lv, dv_, rqv, rkv, gain, tables):
    L = qv.shape[0]
    T = min(1024, L)
    n_sub = T // QB + 1
    QW = T + 2 * HALF
    KW = T + 2 * QB

    def body(qm_ref, qp_ref, qn_ref, dom_ref, dop_ref, don_ref, km_ref, kp_ref, kn_ref, vm_ref, vp_ref, vn_ref,
             lm_ref, lp_ref, ln_ref, dm_ref, dp_ref, dn_ref, rqm_ref, rqp_ref, rqn_ref, rk_ref,
             gain_ref, bias_ref,
             dq_ref, dk_ref, dv_ref, ds_ref, a_ref,
             qwin, dowin, kwin, vwin, lwin, dwin, rqwin, dqwin, dkacc, dvacc,
             s_scr, d_scr, p_scr, ds_scr, q2_scr, do2_scr, dq_scr):
        r = pl.program_id(0)
        i = pl.program_id(1)

        @pl.when((r == 0) & (i == 0))
        def _():
            ds_ref[...] = jnp.zeros_like(ds_ref)
            a_ref[...] = jnp.zeros_like(a_ref)

        for win, (pr, mn, nx), h_rows in ((qwin, (qp_ref, qm_ref, qn_ref), HALF),
                                          (dowin, (dop_ref, dom_ref, don_ref), HALF),
                                          (lwin, (lp_ref, lm_ref, ln_ref), HALF),
                                          (dwin, (dp_ref, dm_ref, dn_ref), HALF),
                                          (rqwin, (rqp_ref, rqm_ref, rqn_ref), HALF),
                                          (kwin, (kp_ref, km_ref, kn_ref), QB),
                                          (vwin, (vp_ref, vm_ref, vn_ref), QB)):
            win[0:h_rows] = pr[...]
            win[h_rows:h_rows + T] = mn[...]
            win[h_rows + T:] = nx[...]
        dkacc[0:QB] = jnp.zeros((QB, ATTN_W), F32)
        dvacc[0:QB] = jnp.zeros((QB, ATTN_W), F32)

        lo = _lane((QB, LANES)) < HEAD_DIM
        same_head = _same_head()
        krow = lax.broadcasted_iota(jnp.int32, (KB, 1), 0)
        qcol = _lane((1, 2 * QB)) % QB
        row = lax.broadcasted_iota(jnp.int32, (QB, 1), 0)
        chan_lo = lax.broadcasted_iota(jnp.int32, (LANES, 1), 0) < HEAD_DIM

        cols = [slice(p * LANES, (p + 1) * LANES) for p in range(N_PAIRS)]
        nt = (((1,), (1,)), ((), ()))

        def scores(j):
            r0 = pl.multiple_of(j * QB, QB)
            for k, cs in enumerate(cols):
                qf = qwin[pl.ds(r0, QB), cs].astype(F32) * gain_ref[:, cs]
                q2_scr[k] = jnp.concatenate([jnp.where(lo, qf, 0.0), jnp.where(lo, 0.0, qf)], axis=0).astype(BF16)
                dov = dowin[pl.ds(r0, QB), cs]
                zero = jnp.zeros_like(dov)
                do2_scr[k] = jnp.concatenate([jnp.where(lo, dov, zero), jnp.where(lo, zero, dov)], axis=0)
                s_scr[k] = lax.dot_general(kwin[pl.ds(r0, KB), cs], q2_scr[k], nt, preferred_element_type=F32)
                d_scr[k] = lax.dot_general(vwin[pl.ds(r0, KB), cs], do2_scr[k], nt, preferred_element_type=F32)

        def sub(j):
            r0 = pl.multiple_of(j * QB, QB)
            kpos = i * T - QB + j * QB + krow
            qpos = i * T - HALF + j * QB + qcol
            valid = (kpos >= 0) & (kpos < L) & (qpos >= 0) & (qpos < L)
            owned_t = ((qpos >= i * T) & (qpos < i * T + T)).astype(F32)
            qpos_c = i * T - HALF + j * QB + row
            owned = ((qpos_c >= i * T) & (qpos_c < i * T + T)).astype(F32)
            lse_rows = lwin[pl.ds(r0, QB), :].T
            delta_rows = dwin[pl.ds(r0, QB), :].T
            rq_t = rqwin[pl.ds(r0, QB), :]
            for p in range(N_PAIRS):
                k = p
                bias2 = jnp.concatenate([bias_ref[0, 2 * p], bias_ref[0, 2 * p + 1]], axis=1)
                lse2 = jnp.concatenate([lse_rows[2 * p:2 * p + 1, :], lse_rows[2 * p + 1:2 * p + 2, :]], axis=1)
                delta2 = jnp.concatenate([delta_rows[2 * p:2 * p + 1, :], delta_rows[2 * p + 1:2 * p + 2, :]], axis=1)
                pt = jnp.where(valid, jnp.exp(s_scr[k] + bias2 - lse2), 0.0)
                dst = pt * (d_scr[k] - delta2)
                ds_ref[p] += dst * owned_t
                p_scr[p] = pt.astype(BF16)
                ds_scr[p] = dst.astype(BF16)
            for p, cs in enumerate(cols):
                k = p
                for acc_ref, lhs, rhs in ((dvacc, p_scr[p], do2_scr[k]), (dkacc, ds_scr[p], q2_scr[k])):
                    part = jnp.dot(lhs, rhs, preferred_element_type=F32)
                    acc_ref[pl.ds(r0, QB), cs] += part[0:QB]
                    acc_ref[pl.ds(r0 + QB, QB), cs] = part[QB:KB]
                dq_scr[p] = lax.dot_general(kwin[pl.ds(r0, KB), cs], ds_scr[p], (((0,), (0,)), ((), ())),
                                            preferred_element_type=F32)
            for p, cs in enumerate(cols):
                g2 = gain_ref[:, cs]
                q_f = qwin[pl.ds(r0, QB), cs].astype(F32)
                dqt2 = dq_scr[p]
                dqm2 = jnp.where(chan_lo, dqt2[:, 0:QB], dqt2[:, QB:2 * QB]).T
                a_ref[p] += (dqm2 * q_f * owned).reshape(QB // 8, 8, LANES).sum(axis=0)
                dqh = dqm2 * g2
                mean_c = _split_dot(dqh * q_f, same_head) * (1.0 / HEAD_DIM)
                rq2 = _split_dot(rq_t, _head_to_lanes(p))
                dqwin[pl.ds(r0, QB), cs] = (rq2 * (dqh - q_f * mean_c)).astype(BF16)

        def one_sub(j, carry):
            scores(j)
            sub(j)
            return carry

        lax.fori_loop(0, n_sub, one_sub, 0)
        dq_ref[...] = dqwin[HALF:HALF + T]

        def finish(j, carry):
            r0 = pl.multiple_of(j * QB, QB)
            rk_t = rk_ref[pl.ds(r0, QB), :]
            for p in range(N_PAIRS):
                cs = slice(p * LANES, (p + 1) * LANES)
                dkh = dkacc[pl.ds(QB + r0, QB), cs]
                k_f = km_ref[pl.ds(r0, QB), cs].astype(F32)
                mean_c = _split_dot(dkh * k_f, same_head) * (1.0 / HEAD_DIM)
                rk2 = _split_dot(rk_t, _head_to_lanes(p))
                dk_ref[pl.ds(r0, QB), cs] = (rk2 * (dkh - k_f * mean_c)).astype(BF16)
                dv_ref[pl.ds(r0, QB), cs] = dvacc[pl.ds(QB + r0, QB), cs].astype(BF16)
            return carry

        lax.fori_loop(0, T // QB, finish, 0)

    main = pl.BlockSpec((T, ATTN_W), lambda r, i: (i, r))
    smain = pl.BlockSpec((T, LANES), lambda r, i: (i, r))
    q_prev, q_next = _halo_specs(ATTN_W, HALF, T, L, lambda r: r)
    k_prev, k_next = _halo_specs(ATTN_W, QB, T, L, lambda r: r)
    s_prev, s_next = _halo_specs(LANES, HALF, T, L, lambda r: r)
    piece = jax.ShapeDtypeStruct((L, dil * ATTN_W), BF16)
    return pl.pallas_call(
        body, name=f"attn_bwd_d{dil}", grid=(dil, L // T),
        out_shape=(piece, piece, piece, jax.ShapeDtypeStruct((N_PAIRS, KB, 2 * QB), F32),
                   jax.ShapeDtypeStruct((N_PAIRS, 8, LANES), F32)),
        in_specs=[main, q_prev, q_next, main, q_prev, q_next, main, k_prev, k_next, main, k_prev, k_next,
                  smain, s_prev, s_next, smain, s_prev, s_next, smain, s_prev, s_next, smain,
                  pl.BlockSpec((1, ATTN_W), lambda r, i: (0, 0)),
                  pl.BlockSpec((1, N_HEADS, KB, QB), lambda r, i: (g, 0, 0, 0))],
        out_specs=(main, main, main, pl.BlockSpec((N_PAIRS, KB, 2 * QB), lambda r, i: (0, 0, 0)),
                   pl.BlockSpec((N_PAIRS, 8, LANES), lambda r, i: (0, 0, 0))),
        scratch_shapes=[pltpu.VMEM((QW, ATTN_W), BF16), pltpu.VMEM((QW, ATTN_W), BF16),
                        pltpu.VMEM((KW, ATTN_W), BF16), pltpu.VMEM((KW, ATTN_W), BF16),
                        pltpu.VMEM((QW, LANES), F32), pltpu.VMEM((QW, LANES), F32), pltpu.VMEM((QW, LANES), F32),
                        pltpu.VMEM((QW, ATTN_W), BF16),
                        pltpu.VMEM((KW, ATTN_W), F32), pltpu.VMEM((KW, ATTN_W), F32),
                        pltpu.VMEM((N_PAIRS, KB, 2 * QB), F32), pltpu.VMEM((N_PAIRS, KB, 2 * QB), F32),
                        pltpu.VMEM((N_PAIRS, KB, 2 * QB), BF16), pltpu.VMEM((N_PAIRS, KB, 2 * QB), BF16),
                        pltpu.VMEM((N_PAIRS, 2 * QB, LANES), BF16), pltpu.VMEM((N_PAIRS, 2 * QB, LANES), BF16),
                        pltpu.VMEM((N_PAIRS, LANES, 2 * QB), F32)],
        compiler_params=_params(("arbitrary", "arbitrary"), 56),
    )(qv, qv, qv, dov, dov, dov, kv, kv, kv, vv, vv, vv, lv, lv, lv, dv_, dv_, dv_, rqv, rqv, rqv, rkv, gain, tables)


D5_GROUPS = (0, 1, 2, 3, 7)
QKV_GROUPS = (4, 5, 6)


def _proj_bwd(x, d_out, norm_w, wg, d5, qkv_shares):
    S = x.shape[0]
    tm = 512
    flat = [share for p in qkv_shares for share in p]
    n_t = S // tm

    def body(*refs):
        x_ref, g_ref, nw_ref, wg_ref, d5_ref = refs[:5]
        piece_refs = refs[5:5 + len(flat)]
        gx_ref, dnw_ref, ht_ref, dqkv_ref, sview = refs[5 + len(flat):]
        i = pl.program_id(0)

        @pl.when(i == 0)
        def _():
            dnw_ref[...] = jnp.zeros_like(dnw_ref)

        xv = x_ref[...]
        r = lax.rsqrt(jnp.mean(xv * xv, axis=-1, keepdims=True) + EPS)
        xh = xv * r
        nw = nw_ref[...]
        ht_ref[...] = (xh * nw).T.astype(BF16)
        dh = jnp.zeros((tm, D_MODEL), F32)
        k = 0
        for j in (4, 0, 5, 1, 6, 2, 3, 7):
            if j in D5_GROUPS:
                dp = d5_ref[D5_GROUPS.index(j)]
            else:
                shares = qkv_shares[QKV_GROUPS.index(j)]
                dp = jnp.zeros((tm, CONV_W), F32)
                for t, (_, dil) in enumerate(shares):
                    if dil == 1:
                        dp = dp + piece_refs[k + t][...].astype(F32)
                    else:
                        _from_view(piece_refs[k + t], sview, dil)
                        dp = dp + _scratch_value(sview)
                dp = dp.astype(BF16)
                k += len(shares)
                dqkv_ref[QKV_GROUPS.index(j)] = dp
            dh = dh + lax.dot_general(dp, wg_ref[j], (((1,), (1,)), ((), ())), preferred_element_type=F32)
        dnw_ref[...] += (dh * xh).reshape(tm // 8, 8, D_MODEL).sum(axis=0)
        dxh = dh * nw
        mean_c = jnp.mean(dxh * xh, axis=-1, keepdims=True)
        gx_ref[...] = g_ref[...].astype(F32) + r * (dxh - xh * mean_c)

    tile = pl.BlockSpec((tm, D_MODEL), lambda i: (i, 0))
    return pl.pallas_call(
        body, name="proj_bwd", grid=(n_t,),
        out_shape=(jax.ShapeDtypeStruct((S, D_MODEL), F32), jax.ShapeDtypeStruct((8, D_MODEL), F32),
                   jax.ShapeDtypeStruct((D_MODEL, S), BF16), jax.ShapeDtypeStruct((3, S, CONV_W), BF16)),
        in_specs=[tile, tile, pl.BlockSpec((1, D_MODEL), lambda i: (0, 0)),
                  pl.BlockSpec(wg.shape, lambda i: (0, 0, 0), pipeline_mode=pl.Buffered(1)),
                  pl.BlockSpec((5, tm, CONV_W), lambda i: (0, i, 0))]
        + [_view_spec(tm, dil, CONV_W) for _, dil in flat],
        out_specs=(tile, pl.BlockSpec((8, D_MODEL), lambda i: (0, 0)),
                   pl.BlockSpec((D_MODEL, tm), lambda i: (0, i)), pl.BlockSpec((3, tm, CONV_W), lambda i: (0, i, 0))),
        scratch_shapes=[_tile_scratch(tm, CONV_W)],
        compiler_params=_params(("arbitrary",), 56),
    )(x, d_out, norm_w.reshape(1, D_MODEL), wg, d5, *[a for a, _ in flat])


def _dw_exchange(ht, d5, dqkv, gw_out, small):
    D, S = ht.shape
    tk = 2048
    n_t = S // tk
    me_outer = 4 * lax.axis_index("x") + 2 * lax.axis_index("y") + lax.axis_index("c")
    order = ((me_outer + 1 + jnp.arange(N_DEV, dtype=jnp.int32)) % N_DEV).astype(jnp.int32)
    in_d5 = ((order < 4) | (order == 7)).astype(jnp.int32)
    at_d5 = jnp.where(order < 4, order, jnp.where(order == 7, 4, 0)).astype(jnp.int32)
    at_qkv = jnp.where(in_d5 == 1, 0, order - 4).astype(jnp.int32)

    def body(in_d5_ref, at_d5_ref, at_qkv_ref, ht_hbm, d5_ref, dqkv_ref, gout_ref, sm_ref, rin_ref, rout_ref, rsm_ref,
             acc, sbuf, ht_vmem, in_send, in_recv, side_send, side_recv, local_sems, ht_sems):
        s = pl.program_id(0)
        t = pl.program_id(1)
        x, y, c = lax.axis_index("x"), lax.axis_index("y"), lax.axis_index("c")
        me = 4 * x + 2 * y + c

        def ht_copy(j):
            return pltpu.make_async_copy(ht_hbm.at[:, j * tk:(j + 1) * tk], ht_vmem.at[j], ht_sems.at[j])

        def in_copy(k):
            to = (me + 1 + k) % N_DEV
            return pltpu.make_async_remote_copy(
                src_ref=sbuf.at[k % 2], dst_ref=rin_ref.at[me], send_sem=in_send.at[k], recv_sem=in_recv.at[k],
                device_id=(to // 4, (to // 2) % 2, to % 2), device_id_type=MESH)

        def in_landing(k):
            frm = (me + 2 * N_DEV - 1 - k) % N_DEV
            return pltpu.make_async_remote_copy(
                src_ref=sbuf.at[0], dst_ref=rin_ref.at[frm], send_sem=in_send.at[k], recv_sem=in_recv.at[k],
                device_id=(x, y, c), device_id_type=MESH)

        def side_copies():
            local = [pltpu.make_async_copy(gout_ref.at[me], rout_ref.at[me], local_sems.at[0]),
                     pltpu.make_async_copy(sm_ref, rsm_ref.at[me], local_sems.at[1])]
            remote = []
            for k in range(1, N_DEV):
                px = 1 - x if k & 4 else x
                py = 1 - y if k & 2 else y
                pc = 1 - c if k & 1 else c
                peer = 4 * px + 2 * py + pc
                for a, (src, dst) in enumerate(((gout_ref.at[peer], rout_ref.at[me]), (sm_ref, rsm_ref.at[me]))):
                    remote.append(pltpu.make_async_remote_copy(
                        src_ref=src, dst_ref=dst, send_sem=side_send.at[a * 7 + k - 1],
                        recv_sem=side_recv.at[a * 7 + k - 1], device_id=(px, py, pc), device_id_type=MESH))
            return local, remote

        @pl.when((s == 0) & (t == 0))
        def _():
            for j in range(n_t):
                ht_copy(j).start()
            local, remote = side_copies()
            for cp in local + remote:
                cp.start()

        for j in range(n_t):
            @pl.when((s == 0) & (t == j))
            def _(j=j):
                ht_copy(j).wait()

        @pl.when(t == 0)
        def _():
            acc[...] = jnp.zeros_like(acc)

        dp = jnp.where(in_d5_ref[s] == 1, d5_ref[0], dqkv_ref[0])
        acc[...] += jnp.dot(ht_vmem[t], dp, preferred_element_type=F32)

        for k in range(N_DEV):
            @pl.when((s == k) & (t == n_t - 1))
            def _(k=k):
                if k >= 2:
                    in_copy(k - 2).wait_send()
                sbuf[k % 2] = acc[...].astype(BF16)
                if k < N_DEV - 1:
                    in_copy(k).start()
                else:
                    own = pltpu.make_async_copy(sbuf.at[k % 2], rin_ref.at[me], local_sems.at[2])
                    own.start()
                    in_copy(k - 1).wait_send()
                    for j in range(N_DEV - 1):
                        in_landing(j).wait_recv()
                    local, remote = side_copies()
                    for cp in remote:
                        cp.wait_recv()
                    for cp in remote:
                        cp.wait_send()
                    for cp in local:
                        cp.wait()
                    own.wait()

    hbm = pl.BlockSpec(memory_space=pl.ANY)
    grid_spec = pltpu.PrefetchScalarGridSpec(
        num_scalar_prefetch=3, grid=(N_DEV, n_t),
        in_specs=[hbm,
                  pl.BlockSpec((1, tk, CONV_W), lambda s, t, use, a5, a3: (a5[s], jnp.where(use[s] == 1, t, 0), 0)),
                  pl.BlockSpec((1, tk, CONV_W), lambda s, t, use, a5, a3: (a3[s], jnp.where(use[s] == 1, 0, t), 0)),
                  hbm, hbm],
        out_specs=(hbm, hbm, hbm),
        scratch_shapes=[pltpu.VMEM((D, CONV_W), F32), pltpu.VMEM((2, D, CONV_W), BF16),
                        pltpu.VMEM((n_t, D, tk), BF16),
                        pltpu.SemaphoreType.DMA((N_DEV - 1,)), pltpu.SemaphoreType.DMA((N_DEV - 1,)),
                        pltpu.SemaphoreType.DMA((14,)), pltpu.SemaphoreType.DMA((14,)),
                        pltpu.SemaphoreType.DMA((3,)), pltpu.SemaphoreType.DMA((n_t,))])
    return pl.pallas_call(
        body, name="dw_exchange", grid_spec=grid_spec,
        out_shape=(jax.ShapeDtypeStruct((N_DEV, D, CONV_W), BF16), jax.ShapeDtypeStruct(gw_out.shape, F32),
                   jax.ShapeDtypeStruct((N_DEV,) + small.shape, F32)),
        compiler_params=_params(("arbitrary", "arbitrary"), 52),
    )(in_d5, at_d5, at_qkv, ht, d5, dqkv, gw_out, small)


def _adamw_math(w, g, m, v):
    m2 = ADAM_B1 * m + (1.0 - ADAM_B1) * g
    v2 = ADAM_B2 * v + (1.0 - ADAM_B2) * (g * g)
    m_hat = m2 / (1.0 - ADAM_B1 ** ADAM_STEP)
    v_hat = v2 / (1.0 - ADAM_B2 ** ADAM_STEP)
    delta = -ADAM_LR * (m_hat / (jnp.sqrt(v_hat) + ADAM_EPS) + ADAM_WD * w)
    return delta, m2, v2


def _adamw_sharded(name, parts, w, m, v, rows):
    R, C = w.shape

    def body(p_ref, w_ref, m_ref, v_ref, g_ref, d_ref, m2_ref, v2_ref):
        g = p_ref[0].astype(F32)
        for s in range(1, N_DEV):
            g = g + p_ref[s].astype(F32)
        g_ref[...] = g
        d_ref[...], m2_ref[...], v2_ref[...] = _adamw_math(w_ref[...], g, m_ref[...], v_ref[...])

    spec = pl.BlockSpec((rows, C), lambda i: (i, 0))
    out = jax.ShapeDtypeStruct((R, C), F32)
    return pl.pallas_call(
        body, name=name, grid=(R // rows,),
        out_shape=(out,) * 4,
        in_specs=[pl.BlockSpec((N_DEV, rows, C), lambda i: (0, i, 0)), spec, spec, spec],
        out_specs=(spec,) * 4,
        compiler_params=_params(("arbitrary",), 40),
    )(parts, w, m, v)


def _adamw_small(parts, w, m, v):
    R = w.shape[0]

    def body(p_ref, w_ref, m_ref, v_ref, g_ref, d_ref, m2_ref, v2_ref):
        g = p_ref[0]
        for s in range(1, N_DEV):
            g = g + p_ref[s]
        g_ref[...] = g
        d_ref[...], m2_ref[...], v2_ref[...] = _adamw_math(w_ref[...], g, m_ref[...], v_ref[...])

    vm = pl.BlockSpec(memory_space=pltpu.VMEM)
    out = jax.ShapeDtypeStruct((R, LANES), F32)
    return pl.pallas_call(
        body, name="adamw_small", out_shape=(out,) * 4, in_specs=[vm] * 4, out_specs=(vm,) * 4,
    )(parts, w, m, v)


def _qk_gain_grads(a_parts, q_norm_w, k_norm_w, bias_g):
    def body(a1_ref, a2_ref, a3_ref, qw_ref, kw_ref, bg_ref, out_ref):
        tot = jnp.zeros((8, LANES), F32)
        for a_ref in (a1_ref, a2_ref, a3_ref):
            for p in range(N_PAIRS):
                tot = tot + a_ref[p]
        col = jnp.sum(tot, axis=0, keepdims=True)
        a64 = col[:, 0:HEAD_DIM] + col[:, HEAD_DIM:LANES]
        out_ref[...] = jnp.zeros_like(out_ref)
        out_ref[0:1, 0:HEAD_DIM] = LOGIT_SCALE * kw_ref[...] * a64
        out_ref[1:2, 0:HEAD_DIM] = LOGIT_SCALE * qw_ref[...] * a64
        for h in range(N_HEADS):
            r0, c0 = 2 + h // 4, (h % 4) * N_BUCKETS
            out_ref[r0:r0 + 1, c0:c0 + N_BUCKETS] = bg_ref[h // 2, h % 2:h % 2 + 1, 0:N_BUCKETS]

    vm = pl.BlockSpec(memory_space=pltpu.VMEM)
    return pl.pallas_call(
        body, name="qk_gain_grads", out_shape=jax.ShapeDtypeStruct((8, LANES), F32),
        in_specs=[vm] * 6, out_specs=vm,
    )(*a_parts, q_norm_w.reshape(1, HEAD_DIM), k_norm_w.reshape(1, HEAD_DIM), bias_g)


SMALL_ROWS = 32


def _pack_small(norm_w, conv_b, q_norm_w, k_norm_w, rel_bias, conv_w_full):
    pad64 = lambda a: jnp.pad(a, (0, LANES - HEAD_DIM)).reshape(1, LANES)
    return jnp.concatenate([
        norm_w.reshape(8, LANES), conv_b.reshape(4, LANES), pad64(q_norm_w), pad64(k_norm_w),
        rel_bias.T.reshape(2, LANES), conv_w_full.reshape(12, LANES), jnp.zeros((4, LANES), F32)], axis=0)


def _unpack_small(s):
    return (s[0:8].reshape(D_MODEL), s[8:12].reshape(CONV_W), s[12, 0:HEAD_DIM], s[13, 0:HEAD_DIM],
            s[14:16].reshape(N_HEADS, N_BUCKETS).T, s[16:28].reshape(3, CONV_W))


def kernel(x, norm_w, w_in, conv_w, conv_b, q_norm_w, k_norm_w, rel_bias, w_out, loss_target, m_norm_w, m_w_in, m_conv_w, m_conv_b, m_q_norm_w, m_k_norm_w, m_rel_bias, m_w_out, v_norm_w, v_w_in, v_conv_w, v_conv_b, v_q_norm_w, v_k_norm_w, v_rel_bias, v_w_out):
    S = x.shape[1]
    x2 = x.reshape(S, D_MODEL)
    tgt = loss_target.reshape(S, D_MODEL)
    me = 4 * lax.axis_index("x") + 2 * lax.axis_index("y") + lax.axis_index("c")
    shard_w = CONV_W // N_DEV

    cw_pad = jnp.pad(conv_w, ((0, 5), (0, LANES - shard_w)))
    wg, wo_g, cw_g = _ag_weights(w_in, w_out, cw_pad)
    wo = wo_g.reshape(D_MODEL, D_MODEL)
    conv_w_full = cw_g[:, 0:3, 0:shard_w].transpose(1, 0, 2).reshape(3, CONV_W)

    bmat_t = jnp.asarray(np.stack([_bucket_matrix(d).T for d in DILATIONS]))
    tables = _bias_tables(rel_bias, bmat_t)
    gain = jnp.tile(q_norm_w * k_norm_w * LOGIT_SCALE, N_HEADS).reshape(1, ATTN_W)

    u, gb, gc, zc, za, q_hat, k_hat, v, rq, rk = _fwd_proj(x2, norm_w, wg)
    o_parts, lse_parts = [], []
    for g, dil in enumerate(DILATIONS):
        o_g, lse_g = _attn_fwd(g, dil, q_hat[g], k_hat[g], v[g], gain, tables)
        o_parts.append(o_g)
        lse_parts.append(lse_g)
    y, o, lse = _combine_gates(o_parts, lse_parts, u, gb, gc, zc, za, conv_w_full, conv_b)
    d_out, dy, gwo_part, loss_part = _out_proj(y, x2, tgt, wo)
    loss_mine = jnp.sum(loss_part) * (0.5 / D_MODEL)

    d5, conv_small, d_o, delta = _gates_bwd(dy, u, gb, gc, zc, za, o, conv_w_full, conv_b)
    dqs, dks, dvs, ds_parts, a_parts = [], [], [], [], []
    for g, dil in enumerate(DILATIONS):
        dq, dk, dv, ds_sum, a_sum = _attn_bwd(g, dil, q_hat[g], k_hat[g], v[g], d_o[g], lse[g], delta[g], rq[g], rk[g],
                                              gain, tables)
        dqs.append((dq, dil))
        dks.append((dk, dil))
        dvs.append((dv, dil))
        ds_parts.append(ds_sum)
        a_parts.append(a_sum)
    grad_x, gnw_part, h_t, dqkv = _proj_bwd(x2, d_out, norm_w, wg, d5, [dqs, dks, dvs])
    bias_g = _bias_grad(ds_parts, bmat_t)
    qk_small = _qk_gain_grads(a_parts, q_norm_w, k_norm_w, bias_g)

    small_part = jnp.concatenate([
        gnw_part.sum(axis=0).reshape(8, LANES), conv_small[3].reshape(4, LANES), qk_small[0:4],
        conv_small[0:3].reshape(12, LANES), jnp.zeros((4, LANES), F32).at[0, 0].set(loss_mine)], axis=0)
    r_in, r_out, r_small = _dw_exchange(h_t, d5, dqkv, gwo_part.reshape(N_DEV, D_MODEL // N_DEV, D_MODEL), small_part)

    g_win, d_win, m_win, v_win = _adamw_sharded("adamw_w_in", r_in, w_in, m_w_in, v_w_in, 128)
    g_wo, d_wo, m_wo, v_wo = _adamw_sharded("adamw_w_out", r_out, w_out, m_w_out, v_w_out, 32)

    def full_conv(a):
        return jnp.zeros((3, CONV_W), F32).at[:, 0:shard_w].set(a)

    packs = [_pack_small(nw_, cb_, qw_, kw_, rb_, full_conv(cw_)) for nw_, cb_, qw_, kw_, rb_, cw_ in (
        (norm_w, conv_b, q_norm_w, k_norm_w, rel_bias, conv_w),
        (m_norm_w, m_conv_b, m_q_norm_w, m_k_norm_w, m_rel_bias, m_conv_w),
        (v_norm_w, v_conv_b, v_q_norm_w, v_k_norm_w, v_rel_bias, v_conv_w))]
    r_small_mine = r_small.at[:, 16:28, :].set(
        jnp.pad(lax.dynamic_slice_in_dim(r_small[:, 16:28, :].reshape(N_DEV, 3, CONV_W), me * shard_w, shard_w, axis=2),
                ((0, 0), (0, 0), (0, CONV_W - shard_w))).reshape(N_DEV, 12, LANES))
    outs_small = _adamw_small(r_small_mine, *packs)
    g_s, d_s, m_s, v_s = [_unpack_small(t) for t in outs_small]
    loss = outs_small[0][28, 0]

    def leaves(small, big_in, big_out):
        nw_, cb_, qw_, kw_, rb_, cwf = small
        return (nw_, big_in, cwf[:, 0:shard_w], cb_, qw_, kw_, rb_, big_out)

    return (loss, grad_x.reshape(x.shape),
            *leaves(g_s, g_win, g_wo), *leaves(d_s, d_win, d_wo), *leaves(m_s, m_win, m_wo), *leaves(v_s, v_win, v_wo))
```
